```python
import math
import jax
import jax.numpy as jnp
from jax import lax
import numpy as np

D_MODEL = 1024
BATCH = 2
SEQ = 8192
DEPTH = 2

S5_GROUPS = 32
S5_GROUP_CH = 16
S5_STATE = 64
S5_WIDTH = S5_GROUPS * S5_GROUP_CH
S5_DT_MIN = 1e-3
S5_DT_MAX = 1e-1

NSA_HEADS = 8
NSA_KV_GROUPS = 2
NSA_HPG = NSA_HEADS // NSA_KV_GROUPS
NSA_HEAD_DIM = 64
NSA_WIDTH = NSA_HEADS * NSA_HEAD_DIM
NSA_KV_WIDTH = NSA_KV_GROUPS * NSA_HEAD_DIM
NSA_ROT_DIM = NSA_HEAD_DIM // 4
ROPE_THETA = 500000.0
CMP_BLOCK = 32
CMP_STRIDE = 16
CMP_HIDDEN = 128
SLC_BLOCK = 64
N_SLC = 16
WINDOW = 512
Q_BLOCK = 128
SLC_PER_STRIDE = SLC_BLOCK // CMP_STRIDE
CMP_SPAN = CMP_BLOCK // CMP_STRIDE
FORCE_SCORE = 1e6
MASK_VALUE = -1e30

RWKV_HEADS = 8
RWKV_HEAD_DIM = 64
RWKV_WIDTH = RWKV_HEADS * RWKV_HEAD_DIM
RWKV_LORA_W = 64
RWKV_LORA_A = 64
RWKV_LORA_G = 128
RWKV_COLS = 3 * RWKV_WIDTH + RWKV_LORA_W + RWKV_LORA_A + RWKV_LORA_G
RWKV_GN_EPS = 64e-5

RET_HEADS = 4
RET_DK = 64
RET_DV = 128
RET_QK_WIDTH = RET_HEADS * RET_DK
RET_V_WIDTH = RET_HEADS * RET_DV
RET_CHUNK = 128
RET_THETA = 10000.0
RET_GN_EPS = 1e-5

N_EXPERTS = 64
TOP_K = 8
EXPERT_FF = 256
N_EXPERT_GROUPS = 8
EXPERTS_PER_GROUP = N_EXPERTS // N_EXPERT_GROUPS
TOPK_GROUPS = 4
ROUTED_SCALE = 2.5
EXPERT_BLOCK = 128

MIX_WIDTH = S5_WIDTH + NSA_WIDTH
AB_COLS = (S5_WIDTH, NSA_WIDTH) + (NSA_KV_WIDTH,) * 6 + (3 * NSA_HEADS,)
CD_COLS = (RWKV_COLS, RET_QK_WIDTH, RET_QK_WIDTH, RET_V_WIDTH, RET_V_WIDTH)
AB_IN = sum(AB_COLS)
CD_IN = sum(CD_COLS)
N_EVEN = (DEPTH + 1) // 2
N_ODD = DEPTH // 2
ALPHA = (2 * DEPTH) ** 0.25
BETA = (8 * DEPTH) ** -0.25
LN_EPS = 1e-5

kernel_name = 'hybrid_s5_nsa_rwkv7_retention_moe'


def split_cols(h, sizes):
    return jnp.split(h, np.cumsum(sizes)[:-1].tolist(), axis=-1)


def layer_norm(x, g, b):
    xf = x.astype(jnp.float32)
    mu = xf.mean(-1, keepdims=True)
    var = jnp.square(xf - mu).mean(-1, keepdims=True)
    return ((xf - mu) * lax.rsqrt(var + LN_EPS) * g + b).astype(x.dtype)


def head_group_norm(y, g, b, eps):
    mu = y.mean(-1, keepdims=True)
    var = jnp.square(y - mu).mean(-1, keepdims=True)
    yn = (y - mu) * lax.rsqrt(var + eps)
    return yn.reshape(y.shape[0], y.shape[1], -1) * g + b


def rope(x, pos, rot_dim, theta):
    half = rot_dim // 2
    inv_freq = theta ** (-jnp.arange(half, dtype=jnp.float32) / half)
    ang = pos.astype(jnp.float32)[:, None] * inv_freq[None, :]
    cos = jnp.cos(ang)[:, None, :]
    sin = jnp.sin(ang)[:, None, :]
    x1 = x[..., :half]
    x2 = x[..., half:rot_dim]
    return jnp.concatenate([x1 * cos - x2 * sin, x2 * cos + x1 * sin, x[..., rot_dim:]], axis=-1)


def masked_softmax(s, mask):
    p = jax.nn.softmax(jnp.where(mask, s, MASK_VALUE), axis=-1)
    return jnp.where(mask, p, 0.0)


def swiglu(x, w1, w3, w2):
    return (jax.nn.silu(x @ w1) * (x @ w3)) @ w2


def s5_mixer(u, lam_re, lam_im, log_dt, b_re, b_im, c_re, c_im, d_skip, w_glu, b_glu):
    bsz, seq, _ = u.shape
    ug = u.reshape(bsz, seq, S5_GROUPS, S5_GROUP_CH)
    dt = jnp.exp(log_dt.astype(jnp.float32))[:, None]
    lam_re = lam_re.astype(jnp.float32)
    lam_im = lam_im.astype(jnp.float32)
    mag = jnp.exp(lam_re * dt)
    lb_re = mag * jnp.cos(lam_im * dt)
    lb_im = mag * jnp.sin(lam_im * dt)
    den = lam_re ** 2 + lam_im ** 2
    f_re = ((lb_re - 1.0) * lam_re + lb_im * lam_im) / den
    f_im = (lb_im * lam_re - (lb_re - 1.0) * lam_im) / den
    bb_re = f_re[..., None] * b_re - f_im[..., None] * b_im
    bb_im = f_re[..., None] * b_im + f_im[..., None] * b_re
    drive_re = jnp.einsum('blgh,gph->lbgp', ug, bb_re)
    drive_im = jnp.einsum('blgh,gph->lbgp', ug, bb_im)
    a_re = jnp.broadcast_to(lb_re, (seq, 1) + lb_re.shape)
    a_im = jnp.broadcast_to(lb_im, (seq, 1) + lb_im.shape)

    def combine(e1, e2):
        a1r, a1i, s1r, s1i = e1
        a2r, a2i, s2r, s2i = e2
        return (a2r * a1r - a2i * a1i,
                a2r * a1i + a2i * a1r,
                a2r * s1r - a2i * s1i + s2r,
                a2r * s1i + a2i * s1r + s2i)

    _, _, s_re, s_im = lax.associative_scan(combine, (a_re, a_im, drive_re, drive_im), axis=0)
    y = jnp.einsum('lbgp,ghp->blgh', s_re, c_re) - jnp.einsum('lbgp,ghp->blgh', s_im, c_im)
    y = y.reshape(bsz, seq, S5_WIDTH) + d_skip * u
    y = jax.nn.gelu(y)
    return y * jax.nn.sigmoid(y @ w_glu + b_glu)


def compress_blocks(kv, cmp_idx, pe, w1, b1, w2):
    blocks = kv[:, cmp_idx] + pe[None, None, :, None, :]
    bsz, n_cmp, cb, grp, d = blocks.shape
    flat = jnp.transpose(blocks, (0, 1, 3, 2, 4)).reshape(bsz, n_cmp, grp, cb * d)
    return jax.nn.gelu(flat @ w1 + b1) @ w2


def nsa_mixer(q, kc, vc, ks, vs, kw, vw, gates, pe_k, pe_v, ck_w1, ck_b1, ck_w2, cv_w1, cv_b1, cv_w2):
    bsz, seq = q.shape[:2]
    pos = jnp.arange(seq)
    q = rope(q, pos, NSA_ROT_DIM, ROPE_THETA)
    ks = rope(ks, pos, NSA_ROT_DIM, ROPE_THETA)
    kw = rope(kw, pos, NSA_ROT_DIM, ROPE_THETA)
    n_cmp = (seq - CMP_BLOCK) // CMP_STRIDE + 1
    cmp_idx = jnp.arange(n_cmp)[:, None] * CMP_STRIDE + jnp.arange(CMP_BLOCK)[None, :]
    cmp_end = cmp_idx[:, -1]
    k_cmp = rope(compress_blocks(kc, cmp_idx, pe_k, ck_w1, ck_b1, ck_w2), cmp_end, NSA_ROT_DIM, ROPE_THETA)
    v_cmp = compress_blocks(vc, cmp_idx, pe_v, cv_w1, cv_b1, cv_w2)
    n_blk = seq // SLC_BLOCK
    n_sel = min(N_SLC, n_blk)
    ks_blk = jnp.transpose(ks.reshape(bsz, n_blk, SLC_BLOCK, NSA_KV_GROUPS, NSA_HEAD_DIM), (0, 3, 1, 2, 4))
    vs_blk = jnp.transpose(vs.reshape(bsz, n_blk, SLC_BLOCK, NSA_KV_GROUPS, NSA_HEAD_DIM), (0, 3, 1, 2, 4))
    kw_pad = jnp.pad(kw, ((0, 0), (WINDOW, 0), (0, 0), (0, 0)))
    vw_pad = jnp.pad(vw, ((0, 0), (WINDOW, 0), (0, 0), (0, 0)))
    gates = jax.nn.sigmoid(gates).reshape(bsz, seq, NSA_KV_GROUPS, NSA_HPG, 3)
    scale = NSA_HEAD_DIM ** -0.5
    b_idx = jnp.arange(bsz)[:, None, None, None]
    g_idx = jnp.arange(NSA_KV_GROUPS)[None, :, None, None]
    blk = jnp.arange(n_blk)

    def query_block(q0):
        tpos = q0 + jnp.arange(Q_BLOCK)
        qb = lax.dynamic_slice_in_dim(q, q0, Q_BLOCK, axis=1).reshape(
            bsz, Q_BLOCK, NSA_KV_GROUPS, NSA_HPG, NSA_HEAD_DIM)
        gb = lax.dynamic_slice_in_dim(gates, q0, Q_BLOCK, axis=1)
        s = jnp.einsum('bqghd,bngd->bghqn', qb, k_cmp).astype(jnp.float32) * scale
        p_cmp = masked_softmax(s, cmp_end[None, :] <= tpos[:, None])
        o_cmp = jnp.einsum('bghqn,bngd->bqghd', p_cmp, v_cmp)
        imp = p_cmp.sum(axis=2)
        imp = jnp.pad(imp, ((0, 0), (0, 0), (0, 0), (CMP_SPAN - 1, SLC_PER_STRIDE * n_blk - n_cmp)))
        imp_slc = sum(imp[..., m + n: m + n + SLC_PER_STRIDE * n_blk: SLC_PER_STRIDE]
                      for m in range(SLC_PER_STRIDE) for n in range(CMP_SPAN))
        cur = tpos // SLC_BLOCK
        forced = (blk[None, :] == 0) | (blk[None, :] == cur[:, None]) | (blk[None, :] == cur[:, None] - 1)
        score = jnp.where(forced, FORCE_SCORE, imp_slc)
        score = jnp.where(blk[None, :] * SLC_BLOCK <= tpos[:, None], score, -FORCE_SCORE)
        _, sel = lax.top_k(score, n_sel)
        k_sel = ks_blk[b_idx, g_idx, sel]
        v_sel = vs_blk[b_idx, g_idx, sel]
        tok = sel[..., None] * SLC_BLOCK + jnp.arange(SLC_BLOCK)
        m_sel = (tok <= tpos[None, None, :, None, None]).reshape(bsz, NSA_KV_GROUPS, 1, Q_BLOCK, -1)
        s = jnp.einsum('bqghd,bgqnkd->bghqnk', qb, k_sel).astype(jnp.float32) * scale
        p = masked_softmax(s.reshape(bsz, NSA_KV_GROUPS, NSA_HPG, Q_BLOCK, -1), m_sel)
        o_slc = jnp.einsum('bghqk,bgqkd->bqghd', p,
                           v_sel.reshape(bsz, NSA_KV_GROUPS, Q_BLOCK, -1, NSA_HEAD_DIM))
        kwb = lax.dynamic_slice_in_dim(kw_pad, q0, WINDOW + Q_BLOCK, axis=1)
        vwb = lax.dynamic_slice_in_dim(vw_pad, q0, WINDOW + Q_BLOCK, axis=1)
        kpos = q0 - WINDOW + jnp.arange(WINDOW + Q_BLOCK)
        m_win = ((kpos[None, :] <= tpos[:, None]) & (kpos[None, :] > tpos[:, None] - WINDOW)
                 & (kpos[None, :] >= 0))
        s = jnp.einsum('bqghd,bkgd->bghqk', qb, kwb).astype(jnp.float32) * scale
        o_win = jnp.einsum('bghqk,bkgd->bqghd', masked_softmax(s, m_win), vwb)
        o = gb[..., 0:1] * o_cmp + gb[..., 1:2] * o_slc + gb[..., 2:3] * o_win
        return o.reshape(bsz, Q_BLOCK, NSA_WIDTH)

    out = lax.map(query_block, jnp.arange(seq // Q_BLOCK) * Q_BLOCK)
    return jnp.transpose(out, (1, 0, 2, 3)).reshape(bsz, seq, NSA_WIDTH)


def rwkv7_mixer(p, mu, w0, w_up, a0, a_up, g_up, k_k, k_a, r_k, ln_g, ln_b):
    bsz, seq, _ = p.shape
    p_prev = jnp.pad(p, ((0, 0), (1, 0), (0, 0)))[:, :-1]
    p = p + (p_prev - p) * mu
    r, k, v, w_lo, a_lo, g_lo = split_cols(
        p, (RWKV_WIDTH,) * 3 + (RWKV_LORA_W, RWKV_LORA_A, RWKV_LORA_G))
    w = -jax.nn.softplus(-(w0 + jnp.tanh(w_lo) @ w_up)) - 0.5
    decay = jnp.exp(-jnp.exp(w))
    a = jax.nn.sigmoid(a0 + a_lo @ a_up)
    g = jax.nn.sigmoid(g_lo) @ g_up

    def heads(t):
        return t.reshape(bsz, seq, RWKV_HEADS, RWKV_HEAD_DIM)

    kk = heads(k * k_k)
    kk = kk / jnp.maximum(jnp.linalg.norm(kk, axis=-1, keepdims=True), 1e-12)
    k = k * (1.0 + (a - 1.0) * k_a)
    r_h, k_h, v_h, w_h, a_h = heads(r), heads(k), heads(v), heads(decay), heads(a)

    def step(state, inp):
        r_t, w_t, k_t, v_t, kk_t, a_t = inp
        sa = jnp.einsum('bhij,bhj->bhi', state, -kk_t)
        state = (state * w_t[:, :, None, :] + sa[..., None] * (kk_t * a_t)[:, :, None, :]
                 + v_t[..., None] * k_t[:, :, None, :])
        return state, jnp.einsum('bhij,bhj->bhi', state, r_t)

    def seq_major(t):
        return jnp.swapaxes(t, 0, 1)

    state0 = jnp.zeros((bsz, RWKV_HEADS, RWKV_HEAD_DIM, RWKV_HEAD_DIM), jnp.float32)
    _, y = lax.scan(step, state0, (seq_major(r_h), seq_major(w_h), seq_major(k_h),
                                   seq_major(v_h), seq_major(kk), seq_major(a_h)))
    y = head_group_norm(jnp.swapaxes(y, 0, 1), ln_g, ln_b, RWKV_GN_EPS)
    bonus = (r_h * k_h * r_k).sum(-1, keepdims=True) * v_h
    return (y + bonus.reshape(bsz, seq, RWKV_WIDTH)) * g


def retention_mixer(q, k, v, gate, ln_g, ln_b):
    bsz, seq = q.shape[:2]
    pos = jnp.arange(seq)
    q = rope(q, pos, RET_DK, RET_THETA)
    k = rope(k, pos, RET_DK, RET_THETA) * RET_DK ** -0.5
    n_chunk = seq // RET_CHUNK
    log_gamma = jnp.log(1.0 - 2.0 ** (-5.0 - jnp.arange(RET_HEADS, dtype=jnp.float32)))
    qc = q.reshape(bsz, n_chunk, RET_CHUNK, RET_HEADS, RET_DK)
    kc = k.reshape(bsz, n_chunk, RET_CHUNK, RET_HEADS, RET_DK)
    vc = v.reshape(bsz, n_chunk, RET_CHUNK, RET_HEADS, RET_DV)
    i = jnp.arange(RET_CHUNK, dtype=jnp.float32)
    diff = i[:, None] - i[None, :]
    decay_mat = jnp.where(diff >= 0, jnp.exp(diff[None] * log_gamma[:, None, None]), 0.0)
    scores = jnp.einsum('bnihd,bnjhd->bnhij', qc, kc) * decay_mat
    inner = jnp.einsum('bnhij,bnjhe->bnihe', scores, vc)
    k_dec = kc * jnp.exp((RET_CHUNK - 1.0 - i)[:, None] * log_gamma[None, :])[..., None]
    chunk_kv = jnp.einsum('bnjhd,bnjhe->bnhde', k_dec, vc)
    chunk_decay = jnp.exp(RET_CHUNK * log_gamma)[None, :, None, None]

    def chunk_step(state, kv_n):
        return state * chunk_decay + kv_n, state

    state0 = jnp.zeros((bsz, RET_HEADS, RET_DK, RET_DV), jnp.float32)
    _, prev_state = lax.scan(chunk_step, state0, jnp.swapaxes(chunk_kv, 0, 1))
    prev_state = jnp.swapaxes(prev_state, 0, 1)
    q_dec = qc * jnp.exp((i + 1.0)[:, None] * log_gamma[None, :])[..., None]
    cross = jnp.einsum('bnihd,bnhde->bnihe', q_dec, prev_state)
    y = (inner + cross).reshape(bsz, seq, RET_HEADS, RET_DV)
    y = head_group_norm(y, ln_g, ln_b, RET_GN_EPS)
    return jax.nn.silu(gate) * y


def hybrid_ab_mixer(x, w_in, w_out, lam_re, lam_im, log_dt, b_re, b_im, c_re, c_im, d_skip,
                    w_glu, b_glu, pe_k, pe_v, ck_w1, ck_b1, ck_w2, cv_w1, cv_b1, cv_w2):
    bsz, seq, _ = x.shape
    h = (x @ w_in).astype(jnp.float32)
    u, q, kc, vc, ks, vs, kw, vw, gates = split_cols(h, AB_COLS)

    def heads(t):
        return t.reshape(bsz, seq, -1, NSA_HEAD_DIM)

    y_a = s5_mixer(u, lam_re, lam_im, log_dt, b_re, b_im, c_re, c_im, d_skip, w_glu, b_glu)
    y_b = nsa_mixer(heads(q), heads(kc), heads(vc), heads(ks), heads(vs), heads(kw), heads(vw),
                    gates, pe_k, pe_v, ck_w1, ck_b1, ck_w2, cv_w1, cv_b1, cv_w2)
    return (jnp.concatenate([y_a, y_b], axis=-1) @ w_out).astype(x.dtype)


def hybrid_cd_mixer(x, w_in, w_out, mu, w0, w_up, a0, a_up, g_up, k_k, k_a, r_k,
                    rwkv_ln_g, rwkv_ln_b, ret_ln_g, ret_ln_b):
    bsz, seq, _ = x.shape
    h = (x @ w_in).astype(jnp.float32)
    c_in, rq, rk, rv, rgate = split_cols(h, CD_COLS)
    y_c = rwkv7_mixer(c_in, mu, w0, w_up, a0, a_up, g_up, k_k, k_a, r_k, rwkv_ln_g, rwkv_ln_b)
    y_d = retention_mixer(rq.reshape(bsz, seq, RET_HEADS, RET_DK),
                          rk.reshape(bsz, seq, RET_HEADS, RET_DK),
                          rv.reshape(bsz, seq, RET_HEADS, RET_DV), rgate, ret_ln_g, ret_ln_b)
    return (jnp.concatenate([y_c, y_d], axis=-1) @ w_out).astype(x.dtype)


def routed_experts(xf, idx, gate, w1, w3, w2):
    n_tok, d = xf.shape
    n_asg = n_tok * TOP_K
    flat_e = idx.reshape(-1)
    order = jnp.argsort(flat_e)
    e_sorted = flat_e[order]
    tok_sorted = (order // TOP_K).astype(jnp.int32)
    gate_sorted = gate.reshape(-1)[order]
    counts = jnp.bincount(flat_e, length=N_EXPERTS)
    padded = (counts + EXPERT_BLOCK - 1) // EXPERT_BLOCK * EXPERT_BLOCK
    pad_end = jnp.cumsum(padded)
    pad_start = pad_end - padded
    start = jnp.cumsum(counts) - counts
    dest = pad_start[e_sorted] + jnp.arange(n_asg) - start[e_sorted]
    n_blocks = -(-n_asg // EXPERT_BLOCK) + N_EXPERTS
    n_rows = n_blocks * EXPERT_BLOCK
    row_tok = jnp.full((n_rows,), n_tok, jnp.int32).at[dest].set(tok_sorted)
    row_gate = jnp.zeros((n_rows,), gate.dtype).at[dest].set(gate_sorted)
    block_expert = jnp.minimum(
        jnp.searchsorted(pad_end, jnp.arange(n_blocks) * EXPERT_BLOCK, side='right'), N_EXPERTS - 1)
    x_pad = jnp.concatenate([xf, jnp.zeros((1, d), xf.dtype)], axis=0)

    def expert_block(args):
        rows, g, e = args
        xb = x_pad[rows]
        h = jax.nn.silu(xb @ w1[e]) * (xb @ w3[e])
        return (h @ w2[e]) * g[:, None]

    y = lax.map(expert_block, (row_tok.reshape(n_blocks, EXPERT_BLOCK),
                               row_gate.reshape(n_blocks, EXPERT_BLOCK), block_expert))
    return jax.ops.segment_sum(y.reshape(n_rows, d), row_tok, num_segments=n_tok + 1)[:n_tok]


def moe_ffn(x, router, bias, w1, w3, w2, sw1, sw3, sw2):
    bsz, seq, d = x.shape
    xf = x.reshape(bsz * seq, d)
    n_tok = xf.shape[0]
    scores = jax.nn.sigmoid((xf @ router).astype(jnp.float32))
    biased = scores + bias
    grp = biased.reshape(n_tok, N_EXPERT_GROUPS, EXPERTS_PER_GROUP)
    grp_score = lax.top_k(grp, 2)[0].sum(-1)
    _, top_grp = lax.top_k(grp_score, TOPK_GROUPS)
    grp_keep = jax.nn.one_hot(top_grp, N_EXPERT_GROUPS, dtype=jnp.float32).sum(1) > 0
    keep = jnp.repeat(grp_keep, EXPERTS_PER_GROUP, axis=1)
    _, idx = lax.top_k(jnp.where(keep, biased, -jnp.inf), TOP_K)
    gate = jnp.take_along_axis(scores, idx, axis=1)
    gate = gate / gate.sum(-1, keepdims=True) * ROUTED_SCALE
    routed = routed_experts(xf, idx, gate, w1, w3, w2)
    shared = swiglu(xf, sw1, sw3, sw2)
    return (routed + shared).reshape(bsz, seq, d).astype(x.dtype)


def setup_inputs(seed: int = 0) -> dict:
    key = jax.random.key(seed)
    keys = iter(jax.random.split(key, 64))

    def nrm(shape, scale):
        return scale * jax.random.normal(next(keys), shape, jnp.float32)

    def unif(shape, lo, hi):
        return jax.random.uniform(next(keys), shape, jnp.float32, lo, hi)

    ne, no, nl = N_EVEN, N_ODD, DEPTH
    n_idx = jnp.arange(S5_STATE, dtype=jnp.float32)
    gps = (S5_GROUPS, S5_STATE)
    return {
        'x': nrm((BATCH, SEQ, D_MODEL), 1.0),
        'ab_w_in': nrm((ne, D_MODEL, AB_IN), D_MODEL ** -0.5),
        'ab_w_out': nrm((ne, MIX_WIDTH, D_MODEL), BETA * MIX_WIDTH ** -0.5),
        's5_lam_re': -0.5 + nrm((ne,) + gps, 0.01),
        's5_lam_im': math.pi * n_idx + nrm((ne,) + gps, 0.01),
        's5_log_dt': unif((ne, S5_GROUPS), math.log(S5_DT_MIN), math.log(S5_DT_MAX)),
        's5_b_re': nrm((ne, S5_GROUPS, S5_STATE, S5_GROUP_CH), (2 * S5_GROUP_CH) ** -0.5),
        's5_b_im': nrm((ne, S5_GROUPS, S5_STATE, S5_GROUP_CH), (2 * S5_GROUP_CH) ** -0.5),
        's5_c_re': nrm((ne, S5_GROUPS, S5_GROUP_CH, S5_STATE), (2 * S5_STATE) ** -0.5),
        's5_c_im': nrm((ne, S5_GROUPS, S5_GROUP_CH, S5_STATE), (2 * S5_STATE) ** -0.5),
        's5_d': nrm((ne, S5_WIDTH), 1.0),
        's5_w_glu': nrm((ne, S5_WIDTH, S5_WIDTH), S5_WIDTH ** -0.5),
        's5_b_glu': nrm((ne, S5_WIDTH), 0.01),
        'nsa_pe_k': nrm((ne, CMP_BLOCK, NSA_HEAD_DIM), 0.1),
        'nsa_pe_v': nrm((ne, CMP_BLOCK, NSA_HEAD_DIM), 0.1),
        'nsa_ck_w1': nrm((ne, CMP_BLOCK * NSA_HEAD_DIM, CMP_HIDDEN), (CMP_BLOCK * NSA_HEAD_DIM) ** -0.5),
        'nsa_ck_b1': nrm((ne, CMP_HIDDEN), 0.01),
        'nsa_ck_w2': nrm((ne, CMP_HIDDEN, NSA_HEAD_DIM), CMP_HIDDEN ** -0.5),
        'nsa_cv_w1': nrm((ne, CMP_BLOCK * NSA_HEAD_DIM, CMP_HIDDEN), (CMP_BLOCK * NSA_HEAD_DIM) ** -0.5),
        'nsa_cv_b1': nrm((ne, CMP_HIDDEN), 0.01),
        'nsa_cv_w2': nrm((ne, CMP_HIDDEN, NSA_HEAD_DIM), CMP_HIDDEN ** -0.5),
        'cd_w_in': nrm((no, D_MODEL, CD_IN), D_MODEL ** -0.5),
        'cd_w_out': nrm((no, MIX_WIDTH, D_MODEL), BETA * MIX_WIDTH ** -0.5),
        'rwkv_mu': unif((no, RWKV_COLS), 0.0, 1.0),
        'rwkv_w0': unif((no, RWKV_WIDTH), -6.0, -1.0),
        'rwkv_w_up': nrm((no, RWKV_LORA_W, RWKV_WIDTH), 0.1 * RWKV_LORA_W ** -0.5),
        'rwkv_a0': nrm((no, RWKV_WIDTH), 0.1),
        'rwkv_a_up': nrm((no, RWKV_LORA_A, RWKV_WIDTH), RWKV_LORA_A ** -0.5),
        'rwkv_g_up': nrm((no, RWKV_LORA_G, RWKV_WIDTH), RWKV_LORA_G ** -0.5),
        'rwkv_k_k': 0.85 + nrm((no, RWKV_WIDTH), 0.05),
        'rwkv_k_a': 1.0 + nrm((no, RWKV_WIDTH), 0.05),
        'rwkv_r_k': nrm((no, RWKV_HEADS, RWKV_HEAD_DIM), 0.1),
        'rwkv_ln_g': 1.0 + nrm((no, RWKV_WIDTH), 0.05),
        'rwkv_ln_b': nrm((no, RWKV_WIDTH), 0.01),
        'ret_ln_g': 1.0 + nrm((no, RET_V_WIDTH), 0.05),
        'ret_ln_b': nrm((no, RET_V_WIDTH), 0.01),
        'ln1_g': 1.0 + nrm((nl, D_MODEL), 0.05),
        'ln1_b': nrm((nl, D_MODEL), 0.01),
        'ln2_g': 1.0 + nrm((nl, D_MODEL), 0.05),
        'ln2_b': nrm((nl, D_MODEL), 0.01),
        'moe_router': nrm((nl, D_MODEL, N_EXPERTS), D_MODEL ** -0.5),
        'moe_bias': nrm((nl, N_EXPERTS), 0.01),
        'moe_w1': nrm((nl, N_EXPERTS, D_MODEL, EXPERT_FF), D_MODEL ** -0.5),
        'moe_w3': nrm((nl, N_EXPERTS, D_MODEL, EXPERT_FF), D_MODEL ** -0.5),
        'moe_w2': nrm((nl, N_EXPERTS, EXPERT_FF, D_MODEL), BETA * EXPERT_FF ** -0.5),
        'sh_w1': nrm((nl, D_MODEL, EXPERT_FF), D_MODEL ** -0.5),
        'sh_w3': nrm((nl, D_MODEL, EXPERT_FF), D_MODEL ** -0.5),
        'sh_w2': nrm((nl, EXPERT_FF, D_MODEL), BETA * EXPERT_FF ** -0.5),
    }


def reference(x, ab_w_in, ab_w_out, s5_lam_re, s5_lam_im, s5_log_dt, s5_b_re, s5_b_im, s5_c_re,
              s5_c_im, s5_d, s5_w_glu, s5_b_glu, nsa_pe_k, nsa_pe_v, nsa_ck_w1, nsa_ck_b1, nsa_ck_w2,
              nsa_cv_w1, nsa_cv_b1, nsa_cv_w2, cd_w_in, cd_w_out, rwkv_mu, rwkv_w0, rwkv_w_up,
              rwkv_a0, rwkv_a_up, rwkv_g_up, rwkv_k_k, rwkv_k_a, rwkv_r_k, rwkv_ln_g, rwkv_ln_b,
              ret_ln_g, ret_ln_b, ln1_g, ln1_b, ln2_g, ln2_b, moe_router, moe_bias, moe_w1, moe_w3,
              moe_w2, sh_w1, sh_w3, sh_w2):
    for layer in range(DEPTH):
        i = layer // 2
        if layer % 2 == 0:
            mix = hybrid_ab_mixer(x, ab_w_in[i], ab_w_out[i], s5_lam_re[i], s5_lam_im[i], s5_log_dt[i],
                                  s5_b_re[i], s5_b_im[i], s5_c_re[i], s5_c_im[i], s5_d[i],
                                  s5_w_glu[i], s5_b_glu[i], nsa_pe_k[i], nsa_pe_v[i], nsa_ck_w1[i],
                                  nsa_ck_b1[i], nsa_ck_w2[i], nsa_cv_w1[i], nsa_cv_b1[i], nsa_cv_w2[i])
        else:
            mix = hybrid_cd_mixer(x, cd_w_in[i], cd_w_out[i], rwkv_mu[i], rwkv_w0[i], rwkv_w_up[i],
                                  rwkv_a0[i], rwkv_a_up[i], rwkv_g_up[i], rwkv_k_k[i], rwkv_k_a[i],
                                  rwkv_r_k[i], rwkv_ln_g[i], rwkv_ln_b[i], ret_ln_g[i], ret_ln_b[i])
        x = layer_norm(ALPHA * x + mix, ln1_g[layer], ln1_b[layer])
        ffn = moe_ffn(x, moe_router[layer], moe_bias[layer], moe_w1[layer], moe_w3[layer],
                      moe_w2[layer], sh_w1[layer], sh_w3[layer], sh_w2[layer])
        x = layer_norm(ALPHA * x + ffn, ln2_g[layer], ln2_b[layer])
    return x
```

```python
import functools
import math

import jax
import jax.numpy as jnp
import numpy as np
from jax import lax
from jax.experimental import pallas as pl
from jax.experimental.pallas import tpu as pltpu

F32 = jnp.float32
BF16 = jnp.bfloat16
HIGHEST = lax.Precision.HIGHEST

VMEM_LIMIT_BYTES = 52 * 1024 * 1024
LANES = 128

LN_EPS = 1e-5
DEPTH = 2
ALPHA = (2 * DEPTH) ** 0.25

S5_GROUPS, S5_GROUP_CH, S5_STATE = 32, 16, 64
S5_WIDTH = S5_GROUPS * S5_GROUP_CH
S5_CHUNK = 16
NSA_HEADS, NSA_KV_GROUPS, NSA_HEAD_DIM = 8, 2, 64
NSA_HPG = NSA_HEADS // NSA_KV_GROUPS
NSA_WIDTH = NSA_HEADS * NSA_HEAD_DIM
NSA_ROT_DIM = NSA_HEAD_DIM // 4
ROPE_THETA = 500000.0
CMP_BLOCK, CMP_STRIDE, CMP_HIDDEN = 32, 16, 128
SLC_BLOCK, N_SLC, WINDOW, Q_BLOCK = 64, 16, 512, 128
FORCE_SCORE = 1e6
MASK_VALUE = -1e30
RWKV_HEADS, RWKV_HEAD_DIM = 8, 64
RWKV_WIDTH = RWKV_HEADS * RWKV_HEAD_DIM
RWKV_LORA_W, RWKV_LORA_A, RWKV_LORA_G = 64, 64, 128
RWKV_COLS = 3 * RWKV_WIDTH + RWKV_LORA_W + RWKV_LORA_A + RWKV_LORA_G
RWKV_GN_EPS = 64e-5
RWKV_CHUNK = 64
RWKV_PACK = 4
RET_HEADS, RET_DK, RET_DV, RET_CHUNK = 4, 64, 128, 128
RET_THETA = 10000.0
RET_GN_EPS = 1e-5
N_EXPERTS, TOP_K, EXPERT_FF = 64, 8, 256
N_EXPERT_GROUPS, TOPK_GROUPS = 8, 4
EXPERTS_PER_GROUP = N_EXPERTS // N_EXPERT_GROUPS
ROUTED_SCALE = 2.5


def _params(*sem):
    return pltpu.CompilerParams(dimension_semantics=sem, vmem_limit_bytes=VMEM_LIMIT_BYTES)


def _dot(a, b, **kw):
    return jnp.dot(a, b, preferred_element_type=F32, **kw)


def _dot_nt(a, b, **kw):
    return lax.dot_general(a, b, (((1,), (1,)), ((), ())), preferred_element_type=F32, **kw)


def _dot_tn(a, b, **kw):
    return lax.dot_general(a, b, (((0,), (0,)), ((), ())), preferred_element_type=F32, **kw)


def _gelu(x):
    return 0.5 * x * (1.0 + jnp.tanh(math.sqrt(2.0 / math.pi) * (x + 0.044715 * (x * x * x))))


def _sigmoid(x):
    return 1.0 / (1.0 + jnp.exp(-x))


def _layer_norm_rows(z, g, b):
    mu = jnp.mean(z, axis=-1, keepdims=True)
    zc = z - mu
    var = jnp.mean(zc * zc, axis=-1, keepdims=True)
    return zc * lax.rsqrt(var + LN_EPS) * g + b


def _proj_kernel(x_ref, w_ref, o_ref):
    o_ref[...] = _dot(x_ref[...].astype(BF16), w_ref[...]).astype(o_ref.dtype)


def project(x, w_bf16, tm):
    m, k = x.shape
    n = w_bf16.shape[1]
    return pl.pallas_call(
        _proj_kernel,
        grid=(m // tm,),
        in_specs=[pl.BlockSpec((tm, k), lambda i: (i, 0)), pl.BlockSpec((k, n), lambda i: (0, 0))],
        out_specs=pl.BlockSpec((tm, n), lambda i: (i, 0)),
        out_shape=jax.ShapeDtypeStruct((m, n), F32),
        compiler_params=_params("parallel"),
        name="project",
    )(x, w_bf16)


def _out_proj_ln_kernel(ya_ref, yb_ref, wa_ref, wb_ref, x_ref, g_ref, b_ref, o_ref, obf_ref):
    mix = _dot(ya_ref[...], wa_ref[...]) + _dot(yb_ref[...], wb_ref[...])
    y = _layer_norm_rows(ALPHA * x_ref[...] + mix, g_ref[...], b_ref[...])
    o_ref[...] = y
    obf_ref[...] = y.astype(BF16)


def out_proj_ln(ya, yb, w_out, x, g, b, tm=512):
    n, d = x.shape
    ka, kb = ya.shape[1], yb.shape[1]
    wa = w_out[:ka].astype(BF16)
    wb = w_out[ka:].astype(BF16)
    row = lambda i: (i, 0)
    fixed = lambda i: (0, 0)
    return pl.pallas_call(
        _out_proj_ln_kernel,
        grid=(n // tm,),
        in_specs=[pl.BlockSpec((tm, ka), row), pl.BlockSpec((tm, kb), row),
                  pl.BlockSpec((ka, d), fixed), pl.BlockSpec((kb, d), fixed),
                  pl.BlockSpec((tm, d), row), pl.BlockSpec((1, d), fixed), pl.BlockSpec((1, d), fixed)],
        out_specs=[pl.BlockSpec((tm, d), row), pl.BlockSpec((tm, d), row)],
        out_shape=[jax.ShapeDtypeStruct((n, d), F32), jax.ShapeDtypeStruct((n, d), BF16)],
        compiler_params=_params("parallel"),
        name="out_proj_ln",
    )(ya, yb, wa, wb, x, g.reshape(1, d), b.reshape(1, d))


def _router_kernel(x_ref, rt_ref, bias_ref, o_ref):
    tr = x_ref.shape[0]
    scores = _sigmoid(_dot_nt(rt_ref[...], x_ref[...], precision=HIGHEST))
    biased = scores + bias_ref[...]
    grp = biased.reshape(N_EXPERT_GROUPS, EXPERTS_PER_GROUP, tr)
    pos = lax.broadcasted_iota(jnp.int32, grp.shape, 1)
    m1 = jnp.max(grp, axis=1, keepdims=True)
    first = jnp.min(jnp.where(grp == m1, pos, EXPERTS_PER_GROUP), axis=1, keepdims=True)
    m2 = jnp.max(jnp.where(pos == first, -jnp.inf, grp), axis=1, keepdims=True)
    gscore = (m1 + m2).reshape(N_EXPERT_GROUPS, tr)
    gidx = lax.broadcasted_iota(jnp.int32, gscore.shape, 0)
    grank = jnp.zeros(gscore.shape, F32)
    for j in range(N_EXPERT_GROUPS):
        row = gscore[j:j + 1, :]
        grank = grank + jnp.where(gidx > j, jnp.where(row >= gscore, 1.0, 0.0), jnp.where(row > gscore, 1.0, 0.0))
    gkeep = jnp.where(grank < TOPK_GROUPS, 1.0, 0.0)
    keep = jnp.broadcast_to(gkeep[:, None, :], grp.shape).reshape(N_EXPERTS, tr)
    masked = jnp.where(keep > 0.5, biased, -jnp.inf)
    eidx = lax.broadcasted_iota(jnp.int32, masked.shape, 0)
    rank = jnp.zeros(masked.shape, F32)
    for j in range(N_EXPERTS):
        row = masked[j:j + 1, :]
        rank = rank + jnp.where(eidx > j, jnp.where(row >= masked, 1.0, 0.0), jnp.where(row > masked, 1.0, 0.0))
    gate = jnp.where(rank < TOP_K, scores, 0.0)
    gate = gate / jnp.sum(gate, axis=0, keepdims=True) * ROUTED_SCALE
    padded = jnp.concatenate([gate, jnp.zeros((LANES - N_EXPERTS, tr), F32)], axis=0)
    lane = lax.broadcasted_iota(jnp.int32, (tr, LANES), 1)
    o_ref[...] = jnp.where(lane == N_EXPERTS, 1.0, padded.T)


def moe_router(x, router, bias, tr=512):
    n, d = x.shape
    return pl.pallas_call(
        _router_kernel,
        grid=(n // tr,),
        in_specs=[pl.BlockSpec((tr, d), lambda i: (i, 0)),
                  pl.BlockSpec((N_EXPERTS, d), lambda i: (0, 0)),
                  pl.BlockSpec((N_EXPERTS, 1), lambda i: (0, 0))],
        out_specs=pl.BlockSpec((tr, LANES), lambda i: (i, 0)),
        out_shape=jax.ShapeDtypeStruct((n, LANES), F32),
        compiler_params=_params("parallel"),
        name="moe_router",
    )(x, router.T, bias.reshape(N_EXPERTS, 1))


def _experts_ln_kernel(xbf_ref, x_ref, gate_ref, w1_ref, w3_ref, w2_ref, g_ref, b_ref, o_ref, obf_ref, acc_ref):
    e = pl.program_id(1)

    @pl.when(e == 0)
    def _():
        acc_ref[...] = jnp.zeros_like(acc_ref)

    xb = xbf_ref[...]
    lane = lax.broadcasted_iota(jnp.int32, gate_ref.shape, 1)
    gcol = jnp.sum(jnp.where(lane == e, gate_ref[...], 0.0), axis=1, keepdims=True)
    h1 = _dot(xb, w1_ref[0])
    h3 = _dot(xb, w3_ref[0])
    h = h1 * _sigmoid(h1) * h3 * gcol
    acc_ref[...] += _dot(h.astype(BF16), w2_ref[0])

    @pl.when(e == pl.num_programs(1) - 1)
    def _():
        y = _layer_norm_rows(ALPHA * x_ref[...] + acc_ref[...], g_ref[...], b_ref[...])
        o_ref[...] = y
        obf_ref[...] = y.astype(BF16)


def moe_experts_ln(x, x_bf16, gates, w1, w3, w2, g, b, tm=1024):
    n, d = x.shape
    ne = w1.shape[0]
    tok = lambda i, e: (i, 0)
    fixed = lambda i, e: (0, 0)
    return pl.pallas_call(
        _experts_ln_kernel,
        grid=(n // tm, ne),
        in_specs=[pl.BlockSpec((tm, d), tok), pl.BlockSpec((tm, d), tok), pl.BlockSpec((tm, LANES), tok),
                  pl.BlockSpec((1, d, EXPERT_FF), lambda i, e: (e, 0, 0)),
                  pl.BlockSpec((1, d, EXPERT_FF), lambda i, e: (e, 0, 0)),
                  pl.BlockSpec((1, EXPERT_FF, d), lambda i, e: (e, 0, 0)),
                  pl.BlockSpec((1, d), fixed), pl.BlockSpec((1, d), fixed)],
        out_specs=[pl.BlockSpec((tm, d), tok), pl.BlockSpec((tm, d), tok)],
        out_shape=[jax.ShapeDtypeStruct((n, d), F32), jax.ShapeDtypeStruct((n, d), BF16)],
        scratch_shapes=[pltpu.VMEM((tm, d), F32)],
        compiler_params=_params("parallel", "arbitrary"),
        name="moe_experts_ln",
    )(x_bf16, x, gates, w1, w3, w2, g.reshape(1, d), b.reshape(1, d))


def moe_block(x, x_bf16, router, bias, w1, w3, w2, sw1, sw3, sw2, g, b):
    gates = moe_router(x, router, bias)
    w1c = jnp.concatenate([w1, sw1[None]], axis=0).astype(BF16)
    w3c = jnp.concatenate([w3, sw3[None]], axis=0).astype(BF16)
    w2c = jnp.concatenate([w2, sw2[None]], axis=0).astype(BF16)
    return moe_experts_ln(x, x_bf16, gates, w1c, w3c, w2c, g, b)


def _s5_tables(lam_re, lam_im, log_dt, b_re, b_im, c_re, c_im, n_chunk):
    t, h, p = S5_CHUNK, S5_GROUP_CH, S5_STATE
    dt = jnp.exp(log_dt.astype(F32))[:, None]
    den = lam_re ** 2 + lam_im ** 2

    def lam_pow(k):
        k = jnp.asarray(k, F32)[..., None, None]
        mag = jnp.exp(lam_re * dt * k)
        return mag * jnp.cos(lam_im * dt * k), mag * jnp.sin(lam_im * dt * k)

    lb_re, lb_im = lam_pow(1.0)
    f_re = ((lb_re - 1.0) * lam_re + lb_im * lam_im) / den
    f_im = (lb_im * lam_re - (lb_re - 1.0) * lam_im) / den
    bb_re = f_re[..., None] * b_re - f_im[..., None] * b_im
    bb_im = f_re[..., None] * b_im + f_im[..., None] * b_re
    pr, pi = lam_pow(jnp.arange(t))
    cl_re = c_re[None] * pr[:, :, None, :] - c_im[None] * pi[:, :, None, :]
    cl_im = c_re[None] * pi[:, :, None, :] + c_im[None] * pr[:, :, None, :]
    klag = jnp.einsum('tgop,gpi->tgoi', cl_re, bb_re) - jnp.einsum('tgop,gpi->tgoi', cl_im, bb_im)
    i_idx = jnp.arange(t)[:, None]
    j_idx = jnp.arange(t)[None, :]
    lag = i_idx - j_idx
    toe = jnp.where((lag >= 0)[:, :, None, None, None], klag[jnp.clip(lag, 0)], 0.0)
    w_toe = jnp.transpose(toe, (2, 1, 4, 0, 3)).reshape(S5_GROUPS, t * h, t * h)
    qr, qi = lam_pow(t - 1 - jnp.arange(t))
    st_re = qr[..., None] * bb_re[None] - qi[..., None] * bb_im[None]
    st_im = qr[..., None] * bb_im[None] + qi[..., None] * bb_re[None]
    w_state = jnp.concatenate([jnp.transpose(st_re, (1, 0, 3, 2)).reshape(S5_GROUPS, t * h, p),
                               jnp.transpose(st_im, (1, 0, 3, 2)).reshape(S5_GROUPS, t * h, p)], axis=-1)
    w_cat = jnp.concatenate([w_toe, w_state], axis=-1)
    er, ei = lam_pow(jnp.arange(t) + 1)
    x_re = c_re[None] * er[:, :, None, :] - c_im[None] * ei[:, :, None, :]
    x_im = c_re[None] * ei[:, :, None, :] + c_im[None] * er[:, :, None, :]
    w_cross = jnp.concatenate([jnp.transpose(x_re, (1, 3, 0, 2)).reshape(S5_GROUPS, p, t * h),
                               -jnp.transpose(x_im, (1, 3, 0, 2)).reshape(S5_GROUPS, p, t * h)], axis=1)
    levels = max(1, int(math.log2(n_chunk)))
    sr, si = lam_pow(t * (2.0 ** jnp.arange(levels)))
    a1 = jnp.concatenate([sr, sr], axis=-1)
    a2 = jnp.concatenate([-si, si], axis=-1)
    scan = jnp.transpose(jnp.stack([a1, a2], axis=1), (2, 0, 1, 3))
    return w_cat.astype(BF16), w_cross.astype(BF16), scan.astype(F32)


def _s5_kernel(n_chunk, x_ref, wcat_ref, wcross_ref, scan_ref, o_ref):
    th = x_ref.shape[2]
    ye = _dot(x_ref[0], wcat_ref[0])
    local = ye[:, :th]
    state = ye[:, th:]
    rows = state.shape[0]
    n_in_seq = lax.broadcasted_iota(jnp.int32, state.shape, 0) % n_chunk
    s = jnp.where(n_in_seq >= 1, pltpu.roll(state, 1, 0), 0.0)
    half = state.shape[1] // 2
    level = 0
    d = 1
    while d < n_chunk:
        mult = scan_ref[0, level]
        prev = jnp.where(n_in_seq >= d, pltpu.roll(s, d, 0), 0.0)
        s = s + mult[0:1, :] * prev + mult[1:2, :] * pltpu.roll(prev, half, 1)
        d *= 2
        level += 1
    o_ref[0] = local + _dot(s.astype(BF16), wcross_ref[0])


def s5_scan(xg, w_cat, w_cross, scan, n_chunk):
    g, rows, th = xg.shape
    kern = functools.partial(_s5_kernel, n_chunk)
    return pl.pallas_call(
        kern,
        grid=(g,),
        in_specs=[pl.BlockSpec((1, rows, th), lambda i: (i, 0, 0)),
                  pl.BlockSpec((1,) + w_cat.shape[1:], lambda i: (i, 0, 0)),
                  pl.BlockSpec((1,) + w_cross.shape[1:], lambda i: (i, 0, 0)),
                  pl.BlockSpec((1,) + scan.shape[1:], lambda i: (i, 0, 0, 0))],
        out_specs=pl.BlockSpec((1, rows, th), lambda i: (i, 0, 0)),
        out_shape=jax.ShapeDtypeStruct((g, rows, th), F32),
        compiler_params=_params("parallel"),
        name="s5_scan",
    )(xg, w_cat, w_cross, scan)


def _s5_post_kernel(y_ref, u_ref, d_ref, w_ref, b_ref, o_ref):
    y = _gelu(y_ref[...] + d_ref[...] * u_ref[...])
    o_ref[...] = (y * _sigmoid(_dot(y.astype(BF16), w_ref[...]) + b_ref[...])).astype(o_ref.dtype)


def s5_post(y, h, d_skip, w_glu, b_glu, tm=1024):
    n, w = y.shape
    row = lambda i: (i, 0)
    fixed = lambda i: (0, 0)
    return pl.pallas_call(
        _s5_post_kernel,
        grid=(n // tm,),
        in_specs=[pl.BlockSpec((tm, w), row), pl.BlockSpec((tm, w), row), pl.BlockSpec((1, w), fixed),
                  pl.BlockSpec((w, w), fixed), pl.BlockSpec((1, w), fixed)],
        out_specs=pl.BlockSpec((tm, w), row),
        out_shape=jax.ShapeDtypeStruct((n, w), BF16),
        compiler_params=_params("parallel"),
        name="s5_post",
    )(y, h, d_skip.reshape(1, w), w_glu.astype(BF16), b_glu.reshape(1, w))


def s5_mixer(h, bsz, seq, lam_re, lam_im, log_dt, b_re, b_im, c_re, c_im, d_skip, w_glu, b_glu):
    t = S5_CHUNK
    n_chunk = seq // t
    w_cat, w_cross, scan = _s5_tables(lam_re, lam_im, log_dt, b_re, b_im, c_re, c_im, n_chunk)
    u = h[:, :S5_WIDTH]
    xg = u.reshape(bsz, n_chunk, t, S5_GROUPS, S5_GROUP_CH).transpose(3, 0, 1, 2, 4)
    xg = xg.reshape(S5_GROUPS, bsz * n_chunk, t * S5_GROUP_CH).astype(BF16)
    yg = s5_scan(xg, w_cat, w_cross, scan, n_chunk)
    y = yg.reshape(S5_GROUPS, bsz, n_chunk, t, S5_GROUP_CH).transpose(1, 2, 3, 0, 4).reshape(bsz * seq, S5_WIDTH)
    return s5_post(y, h, d_skip, w_glu, b_glu, tm=min(1024, bsz * seq))


def _rope_tables(pos, rot_dim, theta, head_dim, n_heads):
    half = rot_dim // 2
    inv_freq = theta ** (-jnp.arange(half, dtype=F32) / half)
    ang = pos.astype(F32)[:, None] * inv_freq[None, :]
    cos, sin = jnp.cos(ang), jnp.sin(ang)
    rest = head_dim - rot_dim
    n = pos.shape[0]
    c = jnp.concatenate([cos, cos, jnp.ones((n, rest), F32)], axis=1)
    s_up = jnp.concatenate([-sin, jnp.zeros((n, half + rest), F32)], axis=1)
    s_dn = jnp.concatenate([jnp.zeros((n, half), F32), sin, jnp.zeros((n, rest), F32)], axis=1)
    tile = lambda a: jnp.tile(a, (1, n_heads))
    return tile(c), tile(s_up), tile(s_dn)


def _rope_apply(x, c, s_up, s_dn, half):
    return x * c + pltpu.roll(x, LANES - half, 1) * s_up + pltpu.roll(x, half, 1) * s_dn


def _retention_tables():
    c = RET_CHUNK
    log_gamma = np.log(1.0 - 2.0 ** (-5.0 - np.arange(RET_HEADS, dtype=np.float64)))
    i = np.arange(c, dtype=np.float64)
    diff = i[:, None] - i[None, :]
    decay = np.where(diff >= 0, np.exp(diff[None] * log_gamma[:, None, None]), 0.0)
    qdec = np.repeat(np.exp((i + 1.0)[:, None] * log_gamma[None, :]), RET_DK, axis=1)
    kdec = np.repeat(np.exp((c - 1.0 - i)[:, None] * log_gamma[None, :]), RET_DK, axis=1)
    chunk_decay = [float(v) for v in np.exp(c * log_gamma)]
    return jnp.asarray(decay, F32), jnp.asarray(qdec, F32), jnp.asarray(kdec, F32), chunk_decay


def _retention_kernel(chunk_decay, q_ref, k_ref, v0_ref, v1_ref, g0_ref, g1_ref, c_ref, su_ref, sd_ref,
                      dec_ref, qdec_ref, kdec_ref, lng_ref, lnb_ref, o_ref, state_ref):
    @pl.when(pl.program_id(1) == 0)
    def _():
        state_ref[...] = jnp.zeros_like(state_ref)

    half = RET_DK // 2
    tabs = (c_ref[...], su_ref[...], sd_ref[...])
    q = jnp.concatenate([_rope_apply(q_ref[:, s:s + LANES], *tabs, half) for s in (0, LANES)], axis=1)
    k = jnp.concatenate([_rope_apply(k_ref[:, s:s + LANES], *tabs, half) for s in (0, LANES)], axis=1)
    k = k * (RET_DK ** -0.5)
    q_dec = q * qdec_ref[...]
    k_dec = k * kdec_ref[...]
    v = jnp.concatenate([v0_ref[...], v1_ref[...]], axis=1)
    gate = jnp.concatenate([g0_ref[...], g1_ref[...]], axis=1)
    outs = []
    for h in range(RET_HEADS):
        ks = slice(h * RET_DK, (h + 1) * RET_DK)
        vs = slice(h * RET_DV, (h + 1) * RET_DV)
        vh = v[:, vs].astype(BF16)
        scores = _dot_nt(q[:, ks].astype(BF16), k[:, ks].astype(BF16)) * dec_ref[h]
        y = _dot(scores.astype(BF16), vh) + _dot(q_dec[:, ks].astype(BF16), state_ref[h].astype(BF16))
        state_ref[h] = state_ref[h] * chunk_decay[h] + _dot_tn(k_dec[:, ks].astype(BF16), vh)
        mu = jnp.mean(y, axis=-1, keepdims=True)
        yc = y - mu
        var = jnp.mean(yc * yc, axis=-1, keepdims=True)
        outs.append(yc * lax.rsqrt(var + RET_GN_EPS))
    yn = jnp.concatenate(outs, axis=1) * lng_ref[...] + lnb_ref[...]
    o_ref[...] = (gate * _sigmoid(gate) * yn).astype(o_ref.dtype)


def retention_mixer(h, bsz, seq, col0, ln_g, ln_b):
    c = RET_CHUNK
    n_chunk = seq // c
    qk_w = RET_HEADS * RET_DK
    v_w = RET_HEADS * RET_DV
    assert col0 % qk_w == 0 and qk_w == 2 * LANES and v_w == 2 * qk_w
    cb = col0 // qk_w
    rc, rsu, rsd = _rope_tables(jnp.arange(seq), RET_DK, RET_THETA, RET_DK, 2)
    dec, qdec, kdec, chunk_decay = _retention_tables()
    row = lambda j: (lambda b, n: (b * n_chunk + n, j))
    pos = lambda b, n: (n, 0)
    fixed2 = lambda b, n: (0, 0)
    kern = functools.partial(_retention_kernel, chunk_decay)
    return pl.pallas_call(
        kern,
        grid=(bsz, n_chunk),
        in_specs=[pl.BlockSpec((c, qk_w), row(cb)), pl.BlockSpec((c, qk_w), row(cb + 1)),
                  pl.BlockSpec((c, qk_w), row(cb + 2)), pl.BlockSpec((c, qk_w), row(cb + 3)),
                  pl.BlockSpec((c, qk_w), row(cb + 4)), pl.BlockSpec((c, qk_w), row(cb + 5)),
                  pl.BlockSpec((c, LANES), pos), pl.BlockSpec((c, LANES), pos), pl.BlockSpec((c, LANES), pos),
                  pl.BlockSpec((RET_HEADS, c, c), lambda b, n: (0, 0, 0)),
                  pl.BlockSpec((c, qk_w), fixed2), pl.BlockSpec((c, qk_w), fixed2),
                  pl.BlockSpec((1, v_w), fixed2), pl.BlockSpec((1, v_w), fixed2)],
        out_specs=pl.BlockSpec((c, v_w), lambda b, n: (b * n_chunk + n, 0)),
        out_shape=jax.ShapeDtypeStruct((bsz * seq, v_w), BF16),
        scratch_shapes=[pltpu.VMEM((RET_HEADS, RET_DK, RET_DV), F32)],
        compiler_params=_params("parallel", "arbitrary"),
        name="retention",
    )(h, h, h, h, h, h, rc, rsu, rsd, dec, qdec, kdec, ln_g.reshape(1, v_w), ln_b.reshape(1, v_w))


NSA_KV_W = NSA_KV_GROUPS * NSA_HEAD_DIM
NSA_GATE_COLS = 3 * NSA_HEADS


def _nsa_prep_kernel(q_ref, kvc_ref, kvs_ref, kvw_ref, c_ref, su_ref, sd_ref,
                     qo_ref, kc_ref, vc_ref, ks_ref, vs_ref, kw_ref, vw_ref):
    half = NSA_ROT_DIM // 2
    tabs = (c_ref[...], su_ref[...], sd_ref[...])
    scale = NSA_HEAD_DIM ** -0.5
    qo_ref[...] = jnp.concatenate(
        [_rope_apply(q_ref[:, s:s + LANES], *tabs, half) * scale for s in range(0, NSA_WIDTH, LANES)],
        axis=1).astype(qo_ref.dtype)

    def split(x, o_ref):
        for g in range(NSA_KV_GROUPS):
            o_ref[0, g] = x[:, g * NSA_HEAD_DIM:(g + 1) * NSA_HEAD_DIM].astype(o_ref.dtype)

    split(kvc_ref[:, :NSA_KV_W], kc_ref)
    split(kvc_ref[:, NSA_KV_W:], vc_ref)
    split(_rope_apply(kvs_ref[:, :NSA_KV_W], *tabs, half), ks_ref)
    split(kvs_ref[:, NSA_KV_W:], vs_ref)
    split(_rope_apply(kvw_ref[:, :NSA_KV_W], *tabs, half), kw_ref)
    split(kvw_ref[:, NSA_KV_W:], vw_ref)


def nsa_prep(h, bsz, seq, tl=512):
    tl = min(tl, seq)
    nl = seq // tl
    rc, rsu, rsd = _rope_tables(jnp.arange(seq), NSA_ROT_DIM, ROPE_THETA, NSA_HEAD_DIM, LANES // NSA_HEAD_DIM)
    row = lambda j: (lambda b, l: (b * nl + l, j))
    pos = lambda b, l: (l, 0)
    kv_out = pl.BlockSpec((1, NSA_KV_GROUPS, tl, NSA_HEAD_DIM), lambda b, l: (b, 0, l, 0))
    kv_shape = lambda dt: jax.ShapeDtypeStruct((bsz, NSA_KV_GROUPS, seq, NSA_HEAD_DIM), dt)
    two = 2 * NSA_KV_W
    return pl.pallas_call(
        _nsa_prep_kernel,
        grid=(bsz, nl),
        in_specs=[pl.BlockSpec((tl, NSA_WIDTH), row(1)),
                  pl.BlockSpec((tl, two), row(4)), pl.BlockSpec((tl, two), row(5)), pl.BlockSpec((tl, two), row(6)),
                  pl.BlockSpec((tl, LANES), pos), pl.BlockSpec((tl, LANES), pos), pl.BlockSpec((tl, LANES), pos)],
        out_specs=[pl.BlockSpec((tl, NSA_WIDTH), lambda b, l: (b * nl + l, 0)),
                   kv_out, kv_out, kv_out, kv_out, kv_out, kv_out],
        out_shape=[jax.ShapeDtypeStruct((bsz * seq, NSA_WIDTH), BF16),
                   kv_shape(F32), kv_shape(F32), kv_shape(BF16), kv_shape(BF16), kv_shape(BF16), kv_shape(BF16)],
        compiler_params=_params("parallel", "parallel"),
        name="nsa_prep",
    )(h, h, h, h, rc, rsu, rsd)


def _nsa_compress_kernel(hk_ref, hv_ref, pek_ref, pev_ref, kw1_ref, kb1_ref, kw2_ref, vw1_ref, vb1_ref, vw2_ref,
                         c_ref, su_ref, sd_ref, ko_ref, vo_ref):
    def mlp(h_ref, pe_ref, w1_ref, b1_ref, w2_ref):
        hb = h_ref[0, 0]
        rows = hb.shape[0]
        first = _dot((hb + pe_ref[0:1, :]).astype(BF16), w1_ref[0])
        second = _dot((hb + pe_ref[1:2, :]).astype(BF16), w1_ref[1])
        hid = _gelu(first + pltpu.roll(second, rows - 1, 0) + b1_ref[...])
        return _dot(hid.astype(BF16), w2_ref[...])

    kc = _rope_apply(mlp(hk_ref, pek_ref, kw1_ref, kb1_ref, kw2_ref), c_ref[...], su_ref[...], sd_ref[...],
                     NSA_ROT_DIM // 2)
    vc = mlp(hv_ref, pev_ref, vw1_ref, vb1_ref, vw2_ref)
    ko_ref[0, 0] = kc[:, :NSA_HEAD_DIM].astype(ko_ref.dtype)
    vo_ref[0, 0] = vc[:, :NSA_HEAD_DIM].astype(vo_ref.dtype)


def nsa_compress(kc, vc, pe_k, pe_v, ck_w1, ck_b1, ck_w2, cv_w1, cv_b1, cv_w2):
    bsz, grp, seq, d = kc.shape
    n_rows = seq // CMP_STRIDE
    flat = CMP_STRIDE * d
    cmp_end = jnp.arange(n_rows) * CMP_STRIDE + CMP_BLOCK - 1
    rc, rsu, rsd = _rope_tables(cmp_end, NSA_ROT_DIM, ROPE_THETA, NSA_HEAD_DIM, LANES // NSA_HEAD_DIM)
    pad_w2 = lambda w: jnp.pad(w, ((0, 0), (0, LANES - d))).astype(BF16)
    blk = pl.BlockSpec((1, 1, n_rows, flat), lambda b, g: (b, g, 0, 0))
    f2 = lambda b, g: (0, 0)
    f3 = lambda b, g: (0, 0, 0)
    w_specs = [pl.BlockSpec((2, flat, CMP_HIDDEN), f3), pl.BlockSpec((1, CMP_HIDDEN), f2),
               pl.BlockSpec((CMP_HIDDEN, LANES), f2)]
    out_spec = pl.BlockSpec((1, 1, n_rows, d), lambda b, g: (b, g, 0, 0))
    out_shape = jax.ShapeDtypeStruct((bsz, grp, n_rows, d), BF16)
    return pl.pallas_call(
        _nsa_compress_kernel,
        grid=(bsz, grp),
        in_specs=[blk, blk, pl.BlockSpec((2, flat), f2), pl.BlockSpec((2, flat), f2)] + w_specs + w_specs
                 + [pl.BlockSpec((n_rows, LANES), f2)] * 3,
        out_specs=[out_spec, out_spec],
        out_shape=[out_shape, out_shape],
        compiler_params=_params("parallel", "parallel"),
        name="nsa_compress",
    )(kc.reshape(bsz, grp, n_rows, flat), vc.reshape(bsz, grp, n_rows, flat),
      pe_k.reshape(2, flat), pe_v.reshape(2, flat),
      ck_w1.reshape(2, flat, CMP_HIDDEN).astype(BF16), ck_b1.reshape(1, CMP_HIDDEN), pad_w2(ck_w2),
      cv_w1.reshape(2, flat, CMP_HIDDEN).astype(BF16), cv_b1.reshape(1, CMP_HIDDEN), pad_w2(cv_w2),
      rc, rsu, rsd)


def _softmax_rows(s, mask):
    s = jnp.where(mask, s, MASK_VALUE)
    m = jnp.max(s, axis=-1, keepdims=True)
    p = jnp.where(mask, jnp.exp(s - m), 0.0)
    l = jnp.sum(p, axis=-1, keepdims=True)
    return p / jnp.where(l > 0.0, l, 1.0)


def _nsa_attn_kernel(seq, tk, q_ref, gate_ref, gexp_ref, kc_ref, vc_ref, ks_ref, vs_ref, kw_ref, vw_ref,
                     mmap_ref, o_ref):
    n_blk = seq // SLC_BLOCK
    n_sel = min(N_SLC, n_blk)
    hd = NSA_HEAD_DIM
    q0 = pl.program_id(2) * Q_BLOCK
    q = q_ref[...]
    qs = jnp.concatenate([q[:, h * hd:(h + 1) * hd] for h in range(NSA_HPG)], axis=0)
    rows = NSA_HPG * Q_BLOCK
    t_q = q0 + lax.broadcasted_iota(jnp.int32, (Q_BLOCK, 1), 0)
    t_row = jnp.concatenate([t_q] * NSA_HPG, axis=0)

    def per_head_rows(x):
        return jnp.concatenate([x] * NSA_HPG, axis=0)

    kc = kc_ref[0, 0]
    n_cmp = kc.shape[0]
    cmp_end = lax.broadcasted_iota(jnp.int32, (1, n_cmp), 1) * CMP_STRIDE + (CMP_BLOCK - 1)
    p_cmp = _softmax_rows(_dot_nt(qs, kc), cmp_end <= t_row)
    o_cmp = _dot(p_cmp.astype(BF16), vc_ref[0, 0])
    imp = p_cmp[0:Q_BLOCK]
    for h in range(1, NSA_HPG):
        imp = imp + p_cmp[h * Q_BLOCK:(h + 1) * Q_BLOCK]
    imp_slc = _dot(imp, mmap_ref[...], precision=HIGHEST)
    blk = lax.broadcasted_iota(jnp.int32, (1, n_blk), 1)
    cur = t_q // SLC_BLOCK
    forced = (blk == 0) | (blk == cur) | (blk == cur - 1)
    score = jnp.where(forced, FORCE_SCORE, imp_slc)
    score = jnp.where(blk * SLC_BLOCK <= t_q, score, -FORCE_SCORE)
    sel = jnp.zeros((Q_BLOCK, n_blk), F32)
    for _ in range(n_sel):
        best = jnp.max(score, axis=-1, keepdims=True)
        idx = jnp.min(jnp.where(score == best, blk, n_blk), axis=-1, keepdims=True)
        pick = blk == idx
        sel = jnp.where(pick, 1.0, sel)
        score = jnp.where(pick, -jnp.inf, score)
    sel_b = sel.astype(BF16)

    def slc_tile(kt, carry):
        m, l, acc = carry
        k0 = pl.multiple_of(kt * tk, tk)
        s = _dot_nt(qs, ks_ref[0, 0, pl.ds(k0, tk), :])
        kpos = k0 + lax.broadcasted_iota(jnp.int32, (1, tk), 1)
        kblk = (k0 + lax.broadcasted_iota(jnp.int32, (n_blk, tk), 1)) // SLC_BLOCK
        expand = jnp.where(kblk == lax.broadcasted_iota(jnp.int32, (n_blk, tk), 0), 1.0, 0.0).astype(BF16)
        chosen = jnp.where(kpos <= t_q, _dot(sel_b, expand), 0.0)
        valid = per_head_rows(chosen) > 0.5
        s = jnp.where(valid, s, MASK_VALUE)
        m_new = jnp.maximum(m, jnp.max(s, axis=-1, keepdims=True))
        alpha = jnp.exp(m - m_new)
        p = jnp.where(valid, jnp.exp(s - m_new), 0.0)
        l = alpha * l + jnp.sum(p, axis=-1, keepdims=True)
        acc = alpha * acc + _dot(p.astype(BF16), vs_ref[0, 0, pl.ds(k0, tk), :])
        return m_new, l, acc

    n_tiles = (q0 + Q_BLOCK + tk - 1) // tk
    init = (jnp.full((rows, 1), MASK_VALUE, F32), jnp.zeros((rows, 1), F32), jnp.zeros((rows, hd), F32))
    _, l_slc, acc_slc = lax.fori_loop(0, n_tiles, slc_tile, init)
    o_slc = acc_slc / jnp.where(l_slc > 0.0, l_slc, 1.0)

    band = WINDOW + Q_BLOCK
    w0 = pl.multiple_of(jnp.maximum(q0 - WINDOW, 0), Q_BLOCK)
    kpos = w0 + lax.broadcasted_iota(jnp.int32, (1, band), 1)
    m_win = (kpos <= t_row) & (kpos > t_row - WINDOW)
    p_win = _softmax_rows(_dot_nt(qs, kw_ref[0, 0, pl.ds(w0, band), :]), m_win)
    o_win = _dot(p_win.astype(BF16), vw_ref[0, 0, pl.ds(w0, band), :])

    def head_cols(o):
        return jnp.concatenate([o[h * Q_BLOCK:(h + 1) * Q_BLOCK] for h in range(NSA_HPG)], axis=1)

    w = NSA_HPG * hd
    gates = _dot(_sigmoid(gate_ref[...]), gexp_ref[0], precision=HIGHEST)
    out = (gates[:, 0:w] * head_cols(o_cmp) + gates[:, w:2 * w] * head_cols(o_slc)
           + gates[:, 2 * w:3 * w] * head_cols(o_win))
    o_ref[...] = out.astype(o_ref.dtype)


def _nsa_constants(seq):
    n_blk = seq // SLC_BLOCK
    n_rows = seq // CMP_STRIDE
    per_stride = SLC_BLOCK // CMP_STRIDE
    span = CMP_BLOCK // CMP_STRIDE
    mmap = np.zeros((n_rows, n_blk), np.float32)
    for j in range(n_blk):
        for m in range(per_stride):
            for n in range(span):
                c = per_stride * j + m + n - (span - 1)
                if 0 <= c < n_rows - 1:
                    mmap[c, j] += 1.0
    w = NSA_HPG * NSA_HEAD_DIM
    gexp = np.zeros((NSA_KV_GROUPS, LANES, 3 * w), np.float32)
    for g in range(NSA_KV_GROUPS):
        for h in range(NSA_HPG):
            for br in range(3):
                gexp[g, (g * NSA_HPG + h) * 3 + br, br * w + h * NSA_HEAD_DIM: br * w + (h + 1) * NSA_HEAD_DIM] = 1.0
    return jnp.asarray(mmap), jnp.asarray(gexp)


def nsa_attention(qr, h, gate_col_block, k_cmp, v_cmp, ks, vs, kw, vw, bsz, seq, tk=512):
    tk = min(tk, seq)
    nq = seq // Q_BLOCK
    w = NSA_HPG * NSA_HEAD_DIM
    mmap, gexp = _nsa_constants(seq)
    n_rows = k_cmp.shape[2]
    qblk = lambda b, g, i: (b * nq + i, g)
    kv = lambda n: pl.BlockSpec((1, 1, n, NSA_HEAD_DIM), lambda b, g, i: (b, g, 0, 0))
    kern = functools.partial(_nsa_attn_kernel, seq, tk)
    return pl.pallas_call(
        kern,
        grid=(bsz, NSA_KV_GROUPS, nq),
        in_specs=[pl.BlockSpec((Q_BLOCK, w), qblk),
                  pl.BlockSpec((Q_BLOCK, LANES), lambda b, g, i: (b * nq + i, gate_col_block)),
                  pl.BlockSpec((1, LANES, 3 * w), lambda b, g, i: (g, 0, 0)),
                  kv(n_rows), kv(n_rows), kv(seq), kv(seq), kv(seq), kv(seq),
                  pl.BlockSpec(mmap.shape, lambda b, g, i: (0, 0))],
        out_specs=pl.BlockSpec((Q_BLOCK, w), qblk),
        out_shape=jax.ShapeDtypeStruct((bsz * seq, NSA_WIDTH), BF16),
        compiler_params=_params("parallel", "parallel", "arbitrary"),
        name="nsa_attention",
    )(qr, h, gexp, k_cmp, v_cmp, ks, vs, kw, vw, mmap)


def nsa_mixer(h, bsz, seq, gate_col_block, pe_k, pe_v, ck_w1, ck_b1, ck_w2, cv_w1, cv_b1, cv_w2):
    qr, kc, vc, ks, vs, kw, vw = nsa_prep(h, bsz, seq)
    k_cmp, v_cmp = nsa_compress(kc, vc, pe_k, pe_v, ck_w1, ck_b1, ck_w2, cv_w1, cv_b1, cv_w2)
    return nsa_attention(qr, h, gate_col_block, k_cmp, v_cmp, ks, vs, kw, vw, bsz, seq)


def _head_ones(width, head_dim):
    idx = np.arange(width) // head_dim
    return jnp.asarray((idx[:, None] == idx[None, :]).astype(np.float32))


def _softplus(x):
    return jnp.maximum(x, 0.0) + jnp.log(1.0 + jnp.exp(-jnp.abs(x)))


def _rwkv_pre_kernel(p_ref, prev_ref, mu_ref, w0_ref, wup_ref, a0_ref, aup_ref, gup_ref, kk_ref, ka_ref, rk_ref,
                     ones_ref, r_o, k_o, v_o, kk_o, b_o, ld_o, g_o, bonus_o):
    w = RWKV_WIDTH
    p = p_ref[...]
    first_row = jnp.where(pl.program_id(1) == 0, 0.0, prev_ref[7:8, :])
    is_row0 = lax.broadcasted_iota(jnp.int32, p.shape, 0) == 0
    prev = jnp.where(is_row0, first_row, pltpu.roll(p, 1, 0))
    ps = p + (prev - p) * mu_ref[...]
    r, k, v = ps[:, 0:w], ps[:, w:2 * w], ps[:, 2 * w:3 * w]
    o = 3 * w
    w_lo = ps[:, o:o + RWKV_LORA_W]
    a_lo = ps[:, o + RWKV_LORA_W:o + RWKV_LORA_W + RWKV_LORA_A]
    g_lo = ps[:, o + RWKV_LORA_W + RWKV_LORA_A:]
    wlog = -_softplus(-(w0_ref[...] + _dot(jnp.tanh(w_lo).astype(BF16), wup_ref[...]))) - 0.5
    a = _sigmoid(a0_ref[...] + _dot(a_lo.astype(BF16), aup_ref[...]))
    g = _dot(_sigmoid(g_lo).astype(BF16), gup_ref[...])
    kk = k * kk_ref[...]
    norm = jnp.sqrt(_dot(kk * kk, ones_ref[...], precision=HIGHEST))
    kk = kk / jnp.maximum(norm, 1e-12)
    k2 = k * (1.0 + (a - 1.0) * ka_ref[...])
    r_o[...] = r
    k_o[...] = k2
    v_o[...] = v
    kk_o[...] = kk
    b_o[...] = kk * a
    ld_o[...] = -jnp.exp(wlog)
    g_o[...] = g
    bonus_o[...] = _dot(r * k2 * rk_ref[...], ones_ref[...], precision=HIGHEST) * v


def rwkv_pre(h, bsz, seq, mu, w0, w_up, a0, a_up, g_up, k_k, k_a, r_k, tl=512):
    tl = min(tl, seq)
    nl = seq // tl
    w = RWKV_WIDTH
    cols = RWKV_COLS
    ones = _head_ones(w, RWKV_HEAD_DIM)
    f2 = lambda b, l: (0, 0)
    vec = pl.BlockSpec((1, w), f2)
    out_spec = pl.BlockSpec((tl, w), lambda b, l: (b * nl + l, 0))
    out_shape = jax.ShapeDtypeStruct((bsz * seq, w), F32)
    return pl.pallas_call(
        _rwkv_pre_kernel,
        grid=(bsz, nl),
        in_specs=[pl.BlockSpec((tl, cols), lambda b, l: (b * nl + l, 0)),
                  pl.BlockSpec((8, cols), lambda b, l: (jnp.maximum((b * seq + l * tl) // 8 - 1, 0), 0)),
                  pl.BlockSpec((1, cols), f2), vec, pl.BlockSpec((RWKV_LORA_W, w), f2),
                  vec, pl.BlockSpec((RWKV_LORA_A, w), f2), pl.BlockSpec((RWKV_LORA_G, w), f2),
                  vec, vec, vec, pl.BlockSpec((w, w), f2)],
        out_specs=[out_spec] * 8,
        out_shape=[out_shape] * 8,
        compiler_params=_params("parallel", "parallel"),
        name="rwkv_pre",
    )(h, h, mu.reshape(1, cols), w0.reshape(1, w), w_up.astype(BF16), a0.reshape(1, w), a_up.astype(BF16),
      g_up.astype(BF16), k_k.reshape(1, w), k_a.reshape(1, w), r_k.reshape(1, w), ones)


def _rwkv_masks():
    t, pk = RWKV_CHUNK, RWKV_PACK
    n = t * pk
    ri = np.arange(n)
    same = (ri[:, None] // t) == (ri[None, :] // t)
    tt, ss = ri[:, None] % t, ri[None, :] % t
    levels = []
    k = 1
    while k < t:
        levels.append(same & (tt // (2 * k) == ss // (2 * k)) & ((tt // k) % 2 == 1) & ((ss // k) % 2 == 0))
        k *= 2
    lvl = np.stack(levels).astype(np.float32)
    tri = (np.arange(t)[:, None] >= np.arange(t)[None, :]).astype(np.float32)
    head_lane = ((ri[:, None] // t) == (np.arange(pk * RWKV_HEAD_DIM)[None, :] // RWKV_HEAD_DIM)).astype(np.float32)
    return (jnp.asarray(tri), jnp.asarray(head_lane), jnp.asarray(same.astype(np.float32)), jnp.asarray(lvl))


def _rwkv_chunk_kernel(r_ref, k_ref, v_ref, kk_ref, b_ref, ld_ref, tri_ref, hl_ref, bd_ref, lvl_ref, y_ref, st_ref):
    t, pk = RWKV_CHUNK, RWKV_PACK
    n = t * pk

    @pl.when(pl.program_id(2) == 0)
    def _():
        st_ref[...] = jnp.zeros_like(st_ref)

    ld = ld_ref[...]
    c = _dot(tri_ref[...], ld, precision=HIGHEST)
    c_end = c[t - 1:t, :]
    e_c = jnp.exp(c)
    e_neg = jnp.exp(-c)
    e_end = jnp.exp(c_end - c)
    r, k, v, kk, b = r_ref[...], k_ref[...], v_ref[...], kk_ref[...], b_ref[...]
    kkd = (kk * jnp.exp(c - ld)).astype(BF16)
    rd = (r * e_c).astype(BF16)
    hl = hl_ref[...]
    bd = bd_ref[...]

    def big(x):
        return (jnp.concatenate([x] * pk, axis=0) * hl).astype(BF16)

    st = st_ref[...]
    st_b = st.astype(BF16)
    v_big = big(v)
    a_all = _dot_nt(jnp.concatenate([kkd, rd], axis=0),
                    jnp.concatenate([big(k * e_neg), big(b * e_neg)], axis=0))
    ti = lax.broadcasted_iota(jnp.int32, (t, n), 0)
    si = lax.broadcasted_iota(jnp.int32, (t, n), 1) % t
    strict = ti > si
    incl = ti >= si
    a_kk = jnp.where(strict, a_all[:t, :n], 0.0)
    a_kb = jnp.where(strict, a_all[:t, n:], 0.0)
    a_rk = jnp.where(incl, a_all[t:, :n], 0.0)
    a_rb = jnp.where(incl, a_all[t:, n:], 0.0)
    rhs = _dot(kkd, st_b) + _dot(a_kk.astype(BF16), v_big)
    a_bd = jnp.concatenate([a_kb] * pk, axis=0) * bd
    eye = jnp.where(lax.broadcasted_iota(jnp.int32, (n, n), 0) == lax.broadcasted_iota(jnp.int32, (n, n), 1), 1.0, 0.0)
    m = eye
    for lv in range(lvl_ref.shape[0]):
        mb = m.astype(BF16)
        m = m - _dot(_dot(mb, (a_bd * lvl_ref[lv]).astype(BF16)).astype(BF16), mb)
    u_big = _dot(m.astype(BF16), big(rhs))
    u = u_big[0:t]
    for h in range(1, pk):
        u = u + u_big[h * t:(h + 1) * t]
    y_ref[...] = _dot(rd, st_b) + _dot(a_rk.astype(BF16), v_big) - _dot(a_rb.astype(BF16), big(u))
    decay_col = jnp.broadcast_to(jnp.exp(c_end), st.shape).T
    kb_end = jnp.concatenate([k * e_end, -(b * e_end)], axis=0).astype(BF16)
    vu = jnp.concatenate([v, u], axis=0).astype(BF16)
    st_ref[...] = decay_col * st + bd * _dot_tn(kb_end, vu)


def rwkv_chunk(r, k, v, kk, b, ld, bsz, seq):
    t, pk = RWKV_CHUNK, RWKV_PACK
    n_chunk = seq // t
    wp = pk * RWKV_HEAD_DIM
    n_pack = RWKV_WIDTH // wp
    assert t == RWKV_HEAD_DIM
    tri, hl, bd, lvl = _rwkv_masks()
    blk = pl.BlockSpec((t, wp), lambda bb, g, c: (bb * n_chunk + c, g))
    f2 = lambda bb, g, c: (0, 0)
    return pl.pallas_call(
        _rwkv_chunk_kernel,
        grid=(bsz, n_pack, n_chunk),
        in_specs=[blk] * 6 + [pl.BlockSpec(tri.shape, f2), pl.BlockSpec(hl.shape, f2), pl.BlockSpec(bd.shape, f2),
                              pl.BlockSpec(lvl.shape, lambda bb, g, c: (0, 0, 0))],
        out_specs=blk,
        out_shape=jax.ShapeDtypeStruct((bsz * seq, RWKV_WIDTH), F32),
        scratch_shapes=[pltpu.VMEM((wp, wp), F32)],
        compiler_params=_params("parallel", "parallel", "arbitrary"),
        name="rwkv_chunk",
    )(r, k, v, kk, b, ld, tri, hl, bd, lvl)


def _rwkv_post_kernel(y_ref, bonus_ref, g_ref, lng_ref, lnb_ref, ones_ref, o_ref):
    y = y_ref[...]
    inv = 1.0 / RWKV_HEAD_DIM
    mu = _dot(y, ones_ref[...], precision=HIGHEST) * inv
    yc = y - mu
    var = _dot(yc * yc, ones_ref[...], precision=HIGHEST) * inv
    yn = yc * lax.rsqrt(var + RWKV_GN_EPS) * lng_ref[...] + lnb_ref[...]
    o_ref[...] = ((yn + bonus_ref[...]) * g_ref[...]).astype(o_ref.dtype)


def rwkv_post(y, bonus, g, ln_g, ln_b, tm=1024):
    n, w = y.shape
    tm = min(tm, n)
    row = pl.BlockSpec((tm, w), lambda i: (i, 0))
    vec = pl.BlockSpec((1, w), lambda i: (0, 0))
    return pl.pallas_call(
        _rwkv_post_kernel,
        grid=(n // tm,),
        in_specs=[row, row, row, vec, vec, pl.BlockSpec((w, w), lambda i: (0, 0))],
        out_specs=row,
        out_shape=jax.ShapeDtypeStruct((n, w), BF16),
        compiler_params=_params("parallel"),
        name="rwkv_post",
    )(y, bonus, g, ln_g.reshape(1, w), ln_b.reshape(1, w), _head_ones(w, RWKV_HEAD_DIM))


def rwkv7_mixer(h, bsz, seq, mu, w0, w_up, a0, a_up, g_up, k_k, k_a, r_k, ln_g, ln_b):
    r, k, v, kk, b, ld, g, bonus = rwkv_pre(h, bsz, seq, mu, w0, w_up, a0, a_up, g_up, k_k, k_a, r_k)
    y = rwkv_chunk(r, k, v, kk, b, ld, bsz, seq)
    return rwkv_post(y, bonus, g, ln_g, ln_b)


AB_IN = S5_WIDTH + NSA_WIDTH + 6 * NSA_KV_W + NSA_GATE_COLS
AB_IN_PADDED = -(-AB_IN // LANES) * LANES
NSA_GATE_COL_BLOCK = (AB_IN - NSA_GATE_COLS) // LANES
PROJ_TM = 512


def kernel(x, ab_w_in, ab_w_out, s5_lam_re, s5_lam_im, s5_log_dt, s5_b_re, s5_b_im, s5_c_re, s5_c_im, s5_d, s5_w_glu, s5_b_glu, nsa_pe_k, nsa_pe_v, nsa_ck_w1, nsa_ck_b1, nsa_ck_w2, nsa_cv_w1, nsa_cv_b1, nsa_cv_w2, cd_w_in, cd_w_out, rwkv_mu, rwkv_w0, rwkv_w_up, rwkv_a0, rwkv_a_up, rwkv_g_up, rwkv_k_k, rwkv_k_a, rwkv_r_k, rwkv_ln_g, rwkv_ln_b, ret_ln_g, ret_ln_b, ln1_g, ln1_b, ln2_g, ln2_b, moe_router, moe_bias, moe_w1, moe_w3, moe_w2, sh_w1, sh_w3, sh_w2):
    bsz, seq, d = x.shape
    assert (AB_IN - NSA_GATE_COLS) % LANES == 0
    xf = x.reshape(bsz * seq, d)
    x_in = xf
    for layer in range(DEPTH):
        i = layer // 2
        if layer % 2 == 0:
            w_in = jnp.pad(ab_w_in[i], ((0, 0), (0, AB_IN_PADDED - AB_IN))).astype(BF16)
            h = project(x_in, w_in, PROJ_TM)
            y_1 = s5_mixer(h, bsz, seq, s5_lam_re[i], s5_lam_im[i], s5_log_dt[i], s5_b_re[i], s5_b_im[i],
                           s5_c_re[i], s5_c_im[i], s5_d[i], s5_w_glu[i], s5_b_glu[i])
            y_2 = nsa_mixer(h, bsz, seq, NSA_GATE_COL_BLOCK, nsa_pe_k[i], nsa_pe_v[i], nsa_ck_w1[i], nsa_ck_b1[i],
                            nsa_ck_w2[i], nsa_cv_w1[i], nsa_cv_b1[i], nsa_cv_w2[i])
            w_out = ab_w_out[i]
        else:
            h = project(x_in, cd_w_in[i].astype(BF16), PROJ_TM)
            y_1 = rwkv7_mixer(h, bsz, seq, rwkv_mu[i], rwkv_w0[i], rwkv_w_up[i], rwkv_a0[i], rwkv_a_up[i],
                              rwkv_g_up[i], rwkv_k_k[i], rwkv_k_a[i], rwkv_r_k[i], rwkv_ln_g[i], rwkv_ln_b[i])
            y_2 = retention_mixer(h, bsz, seq, RWKV_COLS, ret_ln_g[i], ret_ln_b[i])
            w_out = cd_w_out[i]
        xf, x_bf = out_proj_ln(y_1, y_2, w_out, xf, ln1_g[layer], ln1_b[layer])
        xf, x_in = moe_block(xf, x_bf, moe_router[layer], moe_bias[layer], moe_w1[layer], moe_w3[layer],
                             moe_w2[layer], sh_w1[layer], sh_w3[layer], sh_w2[layer], ln2_g[layer], ln2_b[layer])
    return xf.reshape(bsz, seq, d)
```

```python
import functools
import math

import jax
import jax.numpy as jnp
import numpy as np
from jax import lax
from jax.experimental import pallas as pl
from jax.experimental.pallas import tpu as pltpu

F32 = jnp.float32
BF16 = jnp.bfloat16
HIGHEST = lax.Precision.HIGHEST

VMEM_LIMIT_BYTES = 52 * 1024 * 1024
LANES = 128

LN_EPS = 1e-5
DEPTH = 2
ALPHA = (2 * DEPTH) ** 0.25

S5_GROUPS, S5_GROUP_CH, S5_STATE = 32, 16, 64
S5_WIDTH = S5_GROUPS * S5_GROUP_CH
S5_CHUNK = 16
NSA_HEADS, NSA_KV_GROUPS, NSA_HEAD_DIM = 8, 2, 64
NSA_HPG = NSA_HEADS // NSA_KV_GROUPS
NSA_WIDTH = NSA_HEADS * NSA_HEAD_DIM
NSA_ROT_DIM = NSA_HEAD_DIM // 4
ROPE_THETA = 500000.0
CMP_BLOCK, CMP_STRIDE, CMP_HIDDEN = 32, 16, 128
SLC_BLOCK, N_SLC, WINDOW, Q_BLOCK = 64, 16, 512, 128
FORCE_SCORE = 1e6
MASK_VALUE = -1e30
RWKV_HEADS, RWKV_HEAD_DIM = 8, 64
RWKV_WIDTH = RWKV_HEADS * RWKV_HEAD_DIM
RWKV_LORA_W, RWKV_LORA_A, RWKV_LORA_G = 64, 64, 128
RWKV_COLS = 3 * RWKV_WIDTH + RWKV_LORA_W + RWKV_LORA_A + RWKV_LORA_G
RWKV_GN_EPS = 64e-5
RWKV_CHUNK = 64
RWKV_PACK = 4
RET_HEADS, RET_DK, RET_DV, RET_CHUNK = 4, 64, 128, 128
RET_THETA = 10000.0
RET_GN_EPS = 1e-5
N_EXPERTS, TOP_K, EXPERT_FF = 64, 8, 256
N_EXPERT_GROUPS, TOPK_GROUPS = 8, 4
EXPERTS_PER_GROUP = N_EXPERTS // N_EXPERT_GROUPS
ROUTED_SCALE = 2.5


def _params(*sem):
    return pltpu.CompilerParams(dimension_semantics=sem, vmem_limit_bytes=VMEM_LIMIT_BYTES)


def _dot(a, b, **kw):
    return jnp.dot(a, b, preferred_element_type=F32, **kw)


def _dot_nt(a, b, **kw):
    return lax.dot_general(a, b, (((1,), (1,)), ((), ())), preferred_element_type=F32, **kw)


def _dot_tn(a, b, **kw):
    return lax.dot_general(a, b, (((0,), (0,)), ((), ())), preferred_element_type=F32, **kw)


def _gelu(x):
    return 0.5 * x * (1.0 + jnp.tanh(math.sqrt(2.0 / math.pi) * (x + 0.044715 * (x * x * x))))


def _sigmoid(x):
    return 1.0 / (1.0 + jnp.exp(-x))


def _layer_norm_rows(z, g, b):
    mu = jnp.mean(z, axis=-1, keepdims=True)
    zc = z - mu
    var = jnp.mean(zc * zc, axis=-1, keepdims=True)
    return zc * lax.rsqrt(var + LN_EPS) * g + b


def _proj_kernel(x_ref, w_ref, o_ref):
    o_ref[...] = _dot(x_ref[...].astype(BF16), w_ref[...]).astype(o_ref.dtype)


def project(x, w_bf16, tm):
    m, k = x.shape
    n = w_bf16.shape[1]
    return pl.pallas_call(
        _proj_kernel,
        grid=(m // tm,),
        in_specs=[pl.BlockSpec((tm, k), lambda i: (i, 0)), pl.BlockSpec((k, n), lambda i: (0, 0))],
        out_specs=pl.BlockSpec((tm, n), lambda i: (i, 0)),
        out_shape=jax.ShapeDtypeStruct((m, n), F32),
        compiler_params=_params("parallel"),
        name="project",
    )(x, w_bf16)


def _out_proj_ln_kernel(ya_ref, yb_ref, wa_ref, wb_ref, x_ref, g_ref, b_ref, o_ref, obf_ref):
    mix = _dot(ya_ref[...], wa_ref[...]) + _dot(yb_ref[...], wb_ref[...])
    y = _layer_norm_rows(ALPHA * x_ref[...] + mix, g_ref[...], b_ref[...])
    o_ref[...] = y
    obf_ref[...] = y.astype(BF16)


def out_proj_ln(ya, yb, w_out, x, g, b, tm=512):
    n, d = x.shape
    ka, kb = ya.shape[1], yb.shape[1]
    wa = w_out[:ka].astype(BF16)
    wb = w_out[ka:].astype(BF16)
    row = lambda i: (i, 0)
    fixed = lambda i: (0, 0)
    return pl.pallas_call(
        _out_proj_ln_kernel,
        grid=(n // tm,),
        in_specs=[pl.BlockSpec((tm, ka), row), pl.BlockSpec((tm, kb), row),
                  pl.BlockSpec((ka, d), fixed), pl.BlockSpec((kb, d), fixed),
                  pl.BlockSpec((tm, d), row), pl.BlockSpec((1, d), fixed), pl.BlockSpec((1, d), fixed)],
        out_specs=[pl.BlockSpec((tm, d), row), pl.BlockSpec((tm, d), row)],
        out_shape=[jax.ShapeDtypeStruct((n, d), F32), jax.ShapeDtypeStruct((n, d), BF16)],
        compiler_params=_params("parallel"),
        name="out_proj_ln",
    )(ya, yb, wa, wb, x, g.reshape(1, d), b.reshape(1, d))


def _router_kernel(x_ref, rt_ref, bias_ref, o_ref):
    tr = x_ref.shape[0]
    scores = _sigmoid(_dot_nt(rt_ref[...], x_ref[...], precision=HIGHEST))
    biased = scores + bias_ref[...]
    grp = biased.reshape(N_EXPERT_GROUPS, EXPERTS_PER_GROUP, tr)
    pos = lax.broadcasted_iota(jnp.int32, grp.shape, 1)
    m1 = jnp.max(grp, axis=1, keepdims=True)
    first = jnp.min(jnp.where(grp == m1, pos, EXPERTS_PER_GROUP), axis=1, keepdims=True)
    m2 = jnp.max(jnp.where(pos == first, -jnp.inf, grp), axis=1, keepdims=True)
    gscore = (m1 + m2).reshape(N_EXPERT_GROUPS, tr)
    gidx = lax.broadcasted_iota(jnp.int32, gscore.shape, 0)
    grank = jnp.zeros(gscore.shape, F32)
    for j in range(N_EXPERT_GROUPS):
        row = gscore[j:j + 1, :]
        grank = grank + jnp.where(gidx > j, jnp.where(row >= gscore, 1.0, 0.0), jnp.where(row > gscore, 1.0, 0.0))
    gkeep = jnp.where(grank < TOPK_GROUPS, 1.0, 0.0)
    keep = jnp.broadcast_to(gkeep[:, None, :], grp.shape).reshape(N_EXPERTS, tr)
    masked = jnp.where(keep > 0.5, biased, -jnp.inf)
    eidx = lax.broadcasted_iota(jnp.int32, masked.shape, 0)
    rank = jnp.zeros(masked.shape, F32)
    for j in range(N_EXPERTS):
        row = masked[j:j + 1, :]
        rank = rank + jnp.where(eidx > j, jnp.where(row >= masked, 1.0, 0.0), jnp.where(row > masked, 1.0, 0.0))
    gate = jnp.where(rank < TOP_K, scores, 0.0)
    gate = gate / jnp.sum(gate, axis=0, keepdims=True) * ROUTED_SCALE
    padded = jnp.concatenate([gate, jnp.zeros((LANES - N_EXPERTS, tr), F32)], axis=0)
    lane = lax.broadcasted_iota(jnp.int32, (tr, LANES), 1)
    o_ref[...] = jnp.where(lane == N_EXPERTS, 1.0, padded.T)


def moe_router(x, router, bias, tr=512):
    n, d = x.shape
    return pl.pallas_call(
        _router_kernel,
        grid=(n // tr,),
        in_specs=[pl.BlockSpec((tr, d), lambda i: (i, 0)),
                  pl.BlockSpec((N_EXPERTS, d), lambda i: (0, 0)),
                  pl.BlockSpec((N_EXPERTS, 1), lambda i: (0, 0))],
        out_specs=pl.BlockSpec((tr, LANES), lambda i: (i, 0)),
        out_shape=jax.ShapeDtypeStruct((n, LANES), F32),
        compiler_params=_params("parallel"),
        name="moe_router",
    )(x, router.T, bias.reshape(N_EXPERTS, 1))


def _experts_ln_kernel(xbf_ref, x_ref, gate_ref, w1_ref, w3_ref, w2_ref, g_ref, b_ref, o_ref, obf_ref, acc_ref):
    e = pl.program_id(1)

    @pl.when(e == 0)
    def _():
        acc_ref[...] = jnp.zeros_like(acc_ref)

    xb = xbf_ref[...]
    lane = lax.broadcasted_iota(jnp.int32, gate_ref.shape, 1)
    gcol = jnp.sum(jnp.where(lane == e, gate_ref[...], 0.0), axis=1, keepdims=True)
    h1 = _dot(xb, w1_ref[0])
    h3 = _dot(xb, w3_ref[0])
    h = h1 * _sigmoid(h1) * h3 * gcol
    acc_ref[...] += _dot(h.astype(BF16), w2_ref[0])

    @pl.when(e == pl.num_programs(1) - 1)
    def _():
        y = _layer_norm_rows(ALPHA * x_ref[...] + acc_ref[...], g_ref[...], b_ref[...])
        o_ref[...] = y
        obf_ref[...] = y.astype(BF16)


def moe_experts_ln(x, x_bf16, gates, w1, w3, w2, g, b, tm=1024):
    n, d = x.shape
    ne = w1.shape[0]
    tok = lambda i, e: (i, 0)
    fixed = lambda i, e: (0, 0)
    return pl.pallas_call(
        _experts_ln_kernel,
        grid=(n // tm, ne),
        in_specs=[pl.BlockSpec((tm, d), tok), pl.BlockSpec((tm, d), tok), pl.BlockSpec((tm, LANES), tok),
                  pl.BlockSpec((1, d, EXPERT_FF), lambda i, e: (e, 0, 0)),
                  pl.BlockSpec((1, d, EXPERT_FF), lambda i, e: (e, 0, 0)),
                  pl.BlockSpec((1, EXPERT_FF, d), lambda i, e: (e, 0, 0)),
                  pl.BlockSpec((1, d), fixed), pl.BlockSpec((1, d), fixed)],
        out_specs=[pl.BlockSpec((tm, d), tok), pl.BlockSpec((tm, d), tok)],
        out_shape=[jax.ShapeDtypeStruct((n, d), F32), jax.ShapeDtypeStruct((n, d), BF16)],
        scratch_shapes=[pltpu.VMEM((tm, d), F32)],
        compiler_params=_params("parallel", "arbitrary"),
        name="moe_experts_ln",
    )(x_bf16, x, gates, w1, w3, w2, g.reshape(1, d), b.reshape(1, d))


def moe_block(x, x_bf16, router, bias, w1, w3, w2, sw1, sw3, sw2, g, b):
    gates = moe_router(x, router, bias)
    w1c = jnp.concatenate([w1, sw1[None]], axis=0).astype(BF16)
    w3c = jnp.concatenate([w3, sw3[None]], axis=0).astype(BF16)
    w2c = jnp.concatenate([w2, sw2[None]], axis=0).astype(BF16)
    return moe_experts_ln(x, x_bf16, gates, w1c, w3c, w2c, g, b)


def _s5_tables(lam_re, lam_im, log_dt, b_re, b_im, c_re, c_im, n_chunk):
    t, h, p = S5_CHUNK, S5_GROUP_CH, S5_STATE
    dt = jnp.exp(log_dt.astype(F32))[:, None]
    den = lam_re ** 2 + lam_im ** 2

    def lam_pow(k):
        k = jnp.asarray(k, F32)[..., None, None]
        mag = jnp.exp(lam_re * dt * k)
        return mag * jnp.cos(lam_im * dt * k), mag * jnp.sin(lam_im * dt * k)

    lb_re, lb_im = lam_pow(1.0)
    f_re = ((lb_re - 1.0) * lam_re + lb_im * lam_im) / den
    f_im = (lb_im * lam_re - (lb_re - 1.0) * lam_im) / den
    bb_re = f_re[..., None] * b_re - f_im[..., None] * b_im
    bb_im = f_re[..., None] * b_im + f_im[..., None] * b_re
    pr, pi = lam_pow(jnp.arange(t))
    cl_re = c_re[None] * pr[:, :, None, :] - c_im[None] * pi[:, :, None, :]
    cl_im = c_re[None] * pi[:, :, None, :] + c_im[None] * pr[:, :, None, :]
    klag = jnp.einsum('tgop,gpi->tgoi', cl_re, bb_re) - jnp.einsum('tgop,gpi->tgoi', cl_im, bb_im)
    i_idx = jnp.arange(t)[:, None]
    j_idx = jnp.arange(t)[None, :]
    lag = i_idx - j_idx
    toe = jnp.where((lag >= 0)[:, :, None, None, None], klag[jnp.clip(lag, 0)], 0.0)
    w_toe = jnp.transpose(toe, (2, 1, 4, 0, 3)).reshape(S5_GROUPS, t * h, t * h)
    qr, qi = lam_pow(t - 1 - jnp.arange(t))
    st_re = qr[..., None] * bb_re[None] - qi[..., None] * bb_im[None]
    st_im = qr[..., None] * bb_im[None] + qi[..., None] * bb_re[None]
    w_state = jnp.concatenate([jnp.transpose(st_re, (1, 0, 3, 2)).reshape(S5_GROUPS, t * h, p),
                               jnp.transpose(st_im, (1, 0, 3, 2)).reshape(S5_GROUPS, t * h, p)], axis=-1)
    w_cat = jnp.concatenate([w_toe, w_state], axis=-1)
    er, ei = lam_pow(jnp.arange(t) + 1)
    x_re = c_re[None] * er[:, :, None, :] - c_im[None] * ei[:, :, None, :]
    x_im = c_re[None] * ei[:, :, None, :] + c_im[None] * er[:, :, None, :]
    w_cross = jnp.concatenate([jnp.transpose(x_re, (1, 3, 0, 2)).reshape(S5_GROUPS, p, t * h),
                               -jnp.transpose(x_im, (1, 3, 0, 2)).reshape(S5_GROUPS, p, t * h)], axis=1)
    levels = max(1, int(math.log2(n_chunk)))
    sr, si = lam_pow(t * (2.0 ** jnp.arange(levels)))
    a1 = jnp.concatenate([sr, sr], axis=-1)
    a2 = jnp.concatenate([-si, si], axis=-1)
    scan = jnp.transpose(jnp.stack([a1, a2], axis=1), (2, 0, 1, 3))
    return w_cat.astype(BF16), w_cross.astype(BF16), scan.astype(F32)


def _s5_kernel(n_chunk, x_ref, wcat_ref, wcross_ref, scan_ref, o_ref):
    th = x_ref.shape[2]
    ye = _dot(x_ref[0], wcat_ref[0])
    local = ye[:, :th]
    state = ye[:, th:]
    rows = state.shape[0]
    n_in_seq = lax.broadcasted_iota(jnp.int32, state.shape, 0) % n_chunk
    s = jnp.where(n_in_seq >= 1, pltpu.roll(state, 1, 0), 0.0)
    half = state.shape[1] // 2
    level = 0
    d = 1
    while d < n_chunk:
        mult = scan_ref[0, level]
        prev = jnp.where(n_in_seq >= d, pltpu.roll(s, d, 0), 0.0)
        s = s + mult[0:1, :] * prev + mult[1:2, :] * pltpu.roll(prev, half, 1)
        d *= 2
        level += 1
    o_ref[0] = local + _dot(s.astype(BF16), wcross_ref[0])


def s5_scan(xg, w_cat, w_cross, scan, n_chunk):
    g, rows, th = xg.shape
    kern = functools.partial(_s5_kernel, n_chunk)
    return pl.pallas_call(
        kern,
        grid=(g,),
        in_specs=[pl.BlockSpec((1, rows, th), lambda i: (i, 0, 0)),
                  pl.BlockSpec((1,) + w_cat.shape[1:], lambda i: (i, 0, 0)),
                  pl.BlockSpec((1,) + w_cross.shape[1:], lambda i: (i, 0, 0)),
                  pl.BlockSpec((1,) + scan.shape[1:], lambda i: (i, 0, 0, 0))],
        out_specs=pl.BlockSpec((1, rows, th), lambda i: (i, 0, 0)),
        out_shape=jax.ShapeDtypeStruct((g, rows, th), F32),
        compiler_params=_params("parallel"),
        name="s5_scan",
    )(xg, w_cat, w_cross, scan)


def _s5_post_kernel(y_ref, u_ref, d_ref, w_ref, b_ref, o_ref):
    y = _gelu(y_ref[...] + d_ref[...] * u_ref[...])
    o_ref[...] = (y * _sigmoid(_dot(y.astype(BF16), w_ref[...]) + b_ref[...])).astype(o_ref.dtype)


def s5_post(y, h, d_skip, w_glu, b_glu, tm=1024):
    n, w = y.shape
    row = lambda i: (i, 0)
    fixed = lambda i: (0, 0)
    return pl.pallas_call(
        _s5_post_kernel,
        grid=(n // tm,),
        in_specs=[pl.BlockSpec((tm, w), row), pl.BlockSpec((tm, w), row), pl.BlockSpec((1, w), fixed),
                  pl.BlockSpec((w, w), fixed), pl.BlockSpec((1, w), fixed)],
        out_specs=pl.BlockSpec((tm, w), row),
        out_shape=jax.ShapeDtypeStruct((n, w), BF16),
        compiler_params=_params("parallel"),
        name="s5_post",
    )(y, h, d_skip.reshape(1, w), w_glu.astype(BF16), b_glu.reshape(1, w))


def s5_mixer(h, bsz, seq, lam_re, lam_im, log_dt, b_re, b_im, c_re, c_im, d_skip, w_glu, b_glu):
    t = S5_CHUNK
    n_chunk = seq // t
    w_cat, w_cross, scan = _s5_tables(lam_re, lam_im, log_dt, b_re, b_im, c_re, c_im, n_chunk)
    u = h[:, :S5_WIDTH]
    xg = u.reshape(bsz, n_chunk, t, S5_GROUPS, S5_GROUP_CH).transpose(3, 0, 1, 2, 4)
    xg = xg.reshape(S5_GROUPS, bsz * n_chunk, t * S5_GROUP_CH).astype(BF16)
    yg = s5_scan(xg, w_cat, w_cross, scan, n_chunk)
    y = yg.reshape(S5_GROUPS, bsz, n_chunk, t, S5_GROUP_CH).transpose(1, 2, 3, 0, 4).reshape(bsz * seq, S5_WIDTH)
    return s5_post(y, h, d_skip, w_glu, b_glu, tm=min(1024, bsz * seq))


def _rope_tables(pos, rot_dim, theta, head_dim, n_heads):
    half = rot_dim // 2
    inv_freq = theta ** (-jnp.arange(half, dtype=F32) / half)
    ang = pos.astype(F32)[:, None] * inv_freq[None, :]
    cos, sin = jnp.cos(ang), jnp.sin(ang)
    rest = head_dim - rot_dim
    n = pos.shape[0]
    c = jnp.concatenate([cos, cos, jnp.ones((n, rest), F32)], axis=1)
    s_up = jnp.concatenate([-sin, jnp.zeros((n, half + rest), F32)], axis=1)
    s_dn = jnp.concatenate([jnp.zeros((n, half), F32), sin, jnp.zeros((n, rest), F32)], axis=1)
    tile = lambda a: jnp.tile(a, (1, n_heads))
    return tile(c), tile(s_up), tile(s_dn)


def _rope_apply(x, c, s_up, s_dn, half):
    return x * c + pltpu.roll(x, LANES - half, 1) * s_up + pltpu.roll(x, half, 1) * s_dn


def _retention_tables():
    c = RET_CHUNK
    log_gamma = np.log(1.0 - 2.0 ** (-5.0 - np.arange(RET_HEADS, dtype=np.float64)))
    i = np.arange(c, dtype=np.float64)
    diff = i[:, None] - i[None, :]
    decay = np.where(diff >= 0, np.exp(diff[None] * log_gamma[:, None, None]), 0.0)
    qdec = np.repeat(np.exp((i + 1.0)[:, None] * log_gamma[None, :]), RET_DK, axis=1)
    kdec = np.repeat(np.exp((c - 1.0 - i)[:, None] * log_gamma[None, :]), RET_DK, axis=1)
    chunk_decay = [float(v) for v in np.exp(c * log_gamma)]
    return jnp.asarray(decay, F32), jnp.asarray(qdec, F32), jnp.asarray(kdec, F32), chunk_decay


def _retention_kernel(chunk_decay, q_ref, k_ref, v0_ref, v1_ref, g0_ref, g1_ref, c_ref, su_ref, sd_ref,
                      dec_ref, qdec_ref, kdec_ref, lng_ref, lnb_ref, o_ref, state_ref):
    @pl.when(pl.program_id(1) == 0)
    def _():
        state_ref[...] = jnp.zeros_like(state_ref)

    half = RET_DK // 2
    tabs = (c_ref[...], su_ref[...], sd_ref[...])
    q = jnp.concatenate([_rope_apply(q_ref[:, s:s + LANES], *tabs, half) for s in (0, LANES)], axis=1)
    k = jnp.concatenate([_rope_apply(k_ref[:, s:s + LANES], *tabs, half) for s in (0, LANES)], axis=1)
    k = k * (RET_DK ** -0.5)
    q_dec = q * qdec_ref[...]
    k_dec = k * kdec_ref[...]
    v = jnp.concatenate([v0_ref[...], v1_ref[...]], axis=1)
    gate = jnp.concatenate([g0_ref[...], g1_ref[...]], axis=1)
    outs = []
    for h in range(RET_HEADS):
        ks = slice(h * RET_DK, (h + 1) * RET_DK)
        vs = slice(h * RET_DV, (h + 1) * RET_DV)
        vh = v[:, vs].astype(BF16)
        scores = _dot_nt(q[:, ks].astype(BF16), k[:, ks].astype(BF16)) * dec_ref[h]
        y = _dot(scores.astype(BF16), vh) + _dot(q_dec[:, ks].astype(BF16), state_ref[h].astype(BF16))
        state_ref[h] = state_ref[h] * chunk_decay[h] + _dot_tn(k_dec[:, ks].astype(BF16), vh)
        mu = jnp.mean(y, axis=-1, keepdims=True)
        yc = y - mu
        var = jnp.mean(yc * yc, axis=-1, keepdims=True)
        outs.append(yc * lax.rsqrt(var + RET_GN_EPS))
    yn = jnp.concatenate(outs, axis=1) * lng_ref[...] + lnb_ref[...]
    o_ref[...] = (gate * _sigmoid(gate) * yn).astype(o_ref.dtype)


def retention_mixer(h, bsz, seq, col0, ln_g, ln_b):
    c = RET_CHUNK
    n_chunk = seq // c
    qk_w = RET_HEADS * RET_DK
    v_w = RET_HEADS * RET_DV
    assert col0 % qk_w == 0 and qk_w == 2 * LANES and v_w == 2 * qk_w
    cb = col0 // qk_w
    rc, rsu, rsd = _rope_tables(jnp.arange(seq), RET_DK, RET_THETA, RET_DK, 2)
    dec, qdec, kdec, chunk_decay = _retention_tables()
    row = lambda j: (lambda b, n: (b * n_chunk + n, j))
    pos = lambda b, n: (n, 0)
    fixed2 = lambda b, n: (0, 0)
    kern = functools.partial(_retention_kernel, chunk_decay)
    return pl.pallas_call(
        kern,
        grid=(bsz, n_chunk),
        in_specs=[pl.BlockSpec((c, qk_w), row(cb)), pl.BlockSpec((c, qk_w), row(cb + 1)),
                  pl.BlockSpec((c, qk_w), row(cb + 2)), pl.BlockSpec((c, qk_w), row(cb + 3)),
                  pl.BlockSpec((c, qk_w), row(cb + 4)), pl.BlockSpec((c, qk_w), row(cb + 5)),
                  pl.BlockSpec((c, LANES), pos), pl.BlockSpec((c, LANES), pos), pl.BlockSpec((c, LANES), pos),
                  pl.BlockSpec((RET_HEADS, c, c), lambda b, n: (0, 0, 0)),
                  pl.BlockSpec((c, qk_w), fixed2), pl.BlockSpec((c, qk_w), fixed2),
                  pl.BlockSpec((1, v_w), fixed2), pl.BlockSpec((1, v_w), fixed2)],
        out_specs=pl.BlockSpec((c, v_w), lambda b, n: (b * n_chunk + n, 0)),
        out_shape=jax.ShapeDtypeStruct((bsz * seq, v_w), BF16),
        scratch_shapes=[pltpu.VMEM((RET_HEADS, RET_DK, RET_DV), F32)],
        compiler_params=_params("parallel", "arbitrary"),
        name="retention",
    )(h, h, h, h, h, h, rc, rsu, rsd, dec, qdec, kdec, ln_g.reshape(1, v_w), ln_b.reshape(1, v_w))


NSA_KV_W = NSA_KV_GROUPS * NSA_HEAD_DIM
NSA_GATE_COLS = 3 * NSA_HEADS


def _nsa_prep_kernel(q_ref, kvc_ref, kvs_ref, kvw_ref, c_ref, su_ref, sd_ref,
                     qo_ref, kc_ref, vc_ref, ks_ref, vs_ref, kw_ref, vw_ref):
    half = NSA_ROT_DIM // 2
    tabs = (c_ref[...], su_ref[...], sd_ref[...])
    scale = NSA_HEAD_DIM ** -0.5 * math.log2(math.e)
    qo_ref[...] = jnp.concatenate(
        [_rope_apply(q_ref[:, s:s + LANES], *tabs, half) * scale for s in range(0, NSA_WIDTH, LANES)],
        axis=1).astype(qo_ref.dtype)

    def split(x, o_ref):
        for g in range(NSA_KV_GROUPS):
            o_ref[0, g] = x[:, g * NSA_HEAD_DIM:(g + 1) * NSA_HEAD_DIM].astype(o_ref.dtype)

    split(kvc_ref[:, :NSA_KV_W], kc_ref)
    split(kvc_ref[:, NSA_KV_W:], vc_ref)
    split(_rope_apply(kvs_ref[:, :NSA_KV_W], *tabs, half), ks_ref)
    split(kvs_ref[:, NSA_KV_W:], vs_ref)
    split(_rope_apply(kvw_ref[:, :NSA_KV_W], *tabs, half), kw_ref)
    split(kvw_ref[:, NSA_KV_W:], vw_ref)


def nsa_prep(h, bsz, seq, tl=512):
    tl = min(tl, seq)
    nl = seq // tl
    rc, rsu, rsd = _rope_tables(jnp.arange(seq), NSA_ROT_DIM, ROPE_THETA, NSA_HEAD_DIM, LANES // NSA_HEAD_DIM)
    row = lambda j: (lambda b, l: (b * nl + l, j))
    pos = lambda b, l: (l, 0)
    kv_out = pl.BlockSpec((1, NSA_KV_GROUPS, tl, NSA_HEAD_DIM), lambda b, l: (b, 0, l, 0))
    kv_shape = lambda dt: jax.ShapeDtypeStruct((bsz, NSA_KV_GROUPS, seq, NSA_HEAD_DIM), dt)
    two = 2 * NSA_KV_W
    return pl.pallas_call(
        _nsa_prep_kernel,
        grid=(bsz, nl),
        in_specs=[pl.BlockSpec((tl, NSA_WIDTH), row(1)),
                  pl.BlockSpec((tl, two), row(4)), pl.BlockSpec((tl, two), row(5)), pl.BlockSpec((tl, two), row(6)),
                  pl.BlockSpec((tl, LANES), pos), pl.BlockSpec((tl, LANES), pos), pl.BlockSpec((tl, LANES), pos)],
        out_specs=[pl.BlockSpec((tl, NSA_WIDTH), lambda b, l: (b * nl + l, 0)),
                   kv_out, kv_out, kv_out, kv_out, kv_out, kv_out],
        out_shape=[jax.ShapeDtypeStruct((bsz * seq, NSA_WIDTH), BF16),
                   kv_shape(F32), kv_shape(F32), kv_shape(BF16), kv_shape(BF16), kv_shape(BF16), kv_shape(BF16)],
        compiler_params=_params("parallel", "parallel"),
        name="nsa_prep",
    )(h, h, h, h, rc, rsu, rsd)


def _nsa_compress_kernel(hk_ref, hv_ref, pek_ref, pev_ref, kw1_ref, kb1_ref, kw2_ref, vw1_ref, vb1_ref, vw2_ref,
                         c_ref, su_ref, sd_ref, ko_ref, vo_ref):
    def mlp(h_ref, pe_ref, w1_ref, b1_ref, w2_ref):
        hb = h_ref[0, 0]
        rows = hb.shape[0]
        first = _dot((hb + pe_ref[0:1, :]).astype(BF16), w1_ref[0])
        second = _dot((hb + pe_ref[1:2, :]).astype(BF16), w1_ref[1])
        hid = _gelu(first + pltpu.roll(second, rows - 1, 0) + b1_ref[...])
        return _dot(hid.astype(BF16), w2_ref[...])

    kc = _rope_apply(mlp(hk_ref, pek_ref, kw1_ref, kb1_ref, kw2_ref), c_ref[...], su_ref[...], sd_ref[...],
                     NSA_ROT_DIM // 2)
    vc = mlp(hv_ref, pev_ref, vw1_ref, vb1_ref, vw2_ref)
    ko_ref[0, 0] = kc[:, :NSA_HEAD_DIM].astype(ko_ref.dtype)
    vo_ref[0, 0] = vc[:, :NSA_HEAD_DIM].astype(vo_ref.dtype)


def nsa_compress(kc, vc, pe_k, pe_v, ck_w1, ck_b1, ck_w2, cv_w1, cv_b1, cv_w2):
    bsz, grp, seq, d = kc.shape
    n_rows = seq // CMP_STRIDE
    flat = CMP_STRIDE * d
    cmp_end = jnp.arange(n_rows) * CMP_STRIDE + CMP_BLOCK - 1
    rc, rsu, rsd = _rope_tables(cmp_end, NSA_ROT_DIM, ROPE_THETA, NSA_HEAD_DIM, LANES // NSA_HEAD_DIM)
    pad_w2 = lambda w: jnp.pad(w, ((0, 0), (0, LANES - d))).astype(BF16)
    blk = pl.BlockSpec((1, 1, n_rows, flat), lambda b, g: (b, g, 0, 0))
    f2 = lambda b, g: (0, 0)
    f3 = lambda b, g: (0, 0, 0)
    w_specs = [pl.BlockSpec((2, flat, CMP_HIDDEN), f3), pl.BlockSpec((1, CMP_HIDDEN), f2),
               pl.BlockSpec((CMP_HIDDEN, LANES), f2)]
    out_spec = pl.BlockSpec((1, 1, n_rows, d), lambda b, g: (b, g, 0, 0))
    out_shape = jax.ShapeDtypeStruct((bsz, grp, n_rows, d), BF16)
    return pl.pallas_call(
        _nsa_compress_kernel,
        grid=(bsz, grp),
        in_specs=[blk, blk, pl.BlockSpec((2, flat), f2), pl.BlockSpec((2, flat), f2)] + w_specs + w_specs
                 + [pl.BlockSpec((n_rows, LANES), f2)] * 3,
        out_specs=[out_spec, out_spec],
        out_shape=[out_shape, out_shape],
        compiler_params=_params("parallel", "parallel"),
        name="nsa_compress",
    )(kc.reshape(bsz, grp, n_rows, flat), vc.reshape(bsz, grp, n_rows, flat),
      pe_k.reshape(2, flat), pe_v.reshape(2, flat),
      ck_w1.reshape(2, flat, CMP_HIDDEN).astype(BF16), ck_b1.reshape(1, CMP_HIDDEN), pad_w2(ck_w2),
      cv_w1.reshape(2, flat, CMP_HIDDEN).astype(BF16), cv_b1.reshape(1, CMP_HIDDEN), pad_w2(cv_w2),
      rc, rsu, rsd)


def _bias_rows(bias):
    return jnp.concatenate([bias] * NSA_HPG, axis=0)


def _nsa_attn_kernel(seq, tk, q_ref, gate_ref, gexp_ref, kc_ref, vc_ref, ks_ref, vs_ref, kw_ref, vw_ref,
                     mmap_ref, expand_ref, o_ref):
    n_blk = seq // SLC_BLOCK
    n_sel = min(N_SLC, n_blk)
    hd = NSA_HEAD_DIM
    q0 = pl.program_id(2) * Q_BLOCK
    q = q_ref[...]
    qs = jnp.concatenate([q[:, h * hd:(h + 1) * hd] for h in range(NSA_HPG)], axis=0)
    rows = NSA_HPG * Q_BLOCK
    t_q = q0 + lax.broadcasted_iota(jnp.int32, (Q_BLOCK, 1), 0)

    kc = kc_ref[0, 0]
    n_cmp = kc.shape[0]
    cmp_end = lax.broadcasted_iota(jnp.int32, (1, n_cmp), 1) * CMP_STRIDE + (CMP_BLOCK - 1)
    s = _dot_nt(qs, kc) + _bias_rows(jnp.where(cmp_end <= t_q, 0.0, MASK_VALUE))
    p = jnp.exp2(s - jnp.max(s, axis=-1, keepdims=True))
    any_key = _bias_rows(jnp.where(t_q >= CMP_BLOCK - 1, 1.0, 0.0))
    inv_l = any_key / jnp.sum(p, axis=-1, keepdims=True)
    o_cmp = _dot(p.astype(BF16), vc_ref[0, 0]) * inv_l
    p = p * inv_l
    imp = p[0:Q_BLOCK]
    for h in range(1, NSA_HPG):
        imp = imp + p[h * Q_BLOCK:(h + 1) * Q_BLOCK]
    imp_t = _dot(imp, mmap_ref[...], precision=HIGHEST).T
    blk = lax.broadcasted_iota(jnp.int32, (n_blk, 1), 0)
    t_l = q0 + lax.broadcasted_iota(jnp.int32, (1, Q_BLOCK), 1)
    cur = t_l // SLC_BLOCK
    score = jnp.where(blk == 0, FORCE_SCORE,
                      jnp.where(blk == cur, FORCE_SCORE, jnp.where(blk == cur - 1, FORCE_SCORE, imp_t)))
    score = jnp.where(blk * SLC_BLOCK <= t_l, score, -FORCE_SCORE)
    sel_t = jnp.zeros((n_blk, Q_BLOCK), F32)
    for _ in range(n_sel):
        best = jnp.max(score, axis=0, keepdims=True)
        idx = jnp.min(jnp.where(score == best, blk, n_blk), axis=0, keepdims=True)
        pick = blk == idx
        sel_t = jnp.where(pick, 1.0, sel_t)
        score = jnp.where(pick, -jnp.inf, score)
    sel_b = sel_t.T.astype(BF16)

    def slc_tile(kt, carry):
        m, l, acc = carry
        k0 = pl.multiple_of(kt * tk, tk)
        kpos = k0 + lax.broadcasted_iota(jnp.int32, (1, tk), 1)
        chosen = _dot(sel_b, expand_ref[:, pl.ds(k0, tk)])
        bias = jnp.where(kpos <= t_q, jnp.where(chosen > 0.5, 0.0, MASK_VALUE), MASK_VALUE)
        s = _dot_nt(qs, ks_ref[0, 0, pl.ds(k0, tk), :]) + _bias_rows(bias)
        m_new = jnp.maximum(m, jnp.max(s, axis=-1, keepdims=True))
        alpha = jnp.exp2(m - m_new)
        p = jnp.exp2(s - m_new)
        l = alpha * l + jnp.sum(p, axis=-1, keepdims=True)
        acc = alpha * acc + _dot(p.astype(BF16), vs_ref[0, 0, pl.ds(k0, tk), :])
        return m_new, l, acc

    n_tiles = (q0 + Q_BLOCK + tk - 1) // tk
    init = (jnp.full((rows, 1), MASK_VALUE, F32), jnp.zeros((rows, 1), F32), jnp.zeros((rows, hd), F32))
    _, l_slc, acc_slc = lax.fori_loop(0, n_tiles, slc_tile, init)
    o_slc = acc_slc / l_slc

    band = WINDOW + Q_BLOCK
    w0 = pl.multiple_of(jnp.maximum(q0 - WINDOW, 0), Q_BLOCK)
    kpos = w0 + lax.broadcasted_iota(jnp.int32, (1, band), 1)
    bias = jnp.where(kpos <= t_q, jnp.where(kpos > t_q - WINDOW, 0.0, MASK_VALUE), MASK_VALUE)
    s = _dot_nt(qs, kw_ref[0, 0, pl.ds(w0, band), :]) + _bias_rows(bias)
    p = jnp.exp2(s - jnp.max(s, axis=-1, keepdims=True))
    o_win = _dot(p.astype(BF16), vw_ref[0, 0, pl.ds(w0, band), :]) / jnp.sum(p, axis=-1, keepdims=True)

    def head_cols(o):
        return jnp.concatenate([o[h * Q_BLOCK:(h + 1) * Q_BLOCK] for h in range(NSA_HPG)], axis=1)

    w = NSA_HPG * hd
    gates = _dot(_sigmoid(gate_ref[...]), gexp_ref[0], precision=HIGHEST)
    out = (gates[:, 0:w] * head_cols(o_cmp) + gates[:, w:2 * w] * head_cols(o_slc)
           + gates[:, 2 * w:3 * w] * head_cols(o_win))
    o_ref[...] = out.astype(o_ref.dtype)


def _nsa_constants(seq):
    n_blk = seq // SLC_BLOCK
    n_rows = seq // CMP_STRIDE
    per_stride = SLC_BLOCK // CMP_STRIDE
    span = CMP_BLOCK // CMP_STRIDE
    mmap = np.zeros((n_rows, n_blk), np.float32)
    for j in range(n_blk):
        for m in range(per_stride):
            for n in range(span):
                c = per_stride * j + m + n - (span - 1)
                if 0 <= c < n_rows - 1:
                    mmap[c, j] += 1.0
    w = NSA_HPG * NSA_HEAD_DIM
    gexp = np.zeros((NSA_KV_GROUPS, LANES, 3 * w), np.float32)
    for g in range(NSA_KV_GROUPS):
        for h in range(NSA_HPG):
            for br in range(3):
                gexp[g, (g * NSA_HPG + h) * 3 + br, br * w + h * NSA_HEAD_DIM: br * w + (h + 1) * NSA_HEAD_DIM] = 1.0
    expand = (np.arange(n_blk)[:, None] == (np.arange(seq)[None, :] // SLC_BLOCK)).astype(np.float32)
    return jnp.asarray(mmap), jnp.asarray(gexp), jnp.asarray(expand, BF16)


def nsa_attention(qr, h, gate_col_block, k_cmp, v_cmp, ks, vs, kw, vw, bsz, seq, tk=512):
    tk = min(tk, seq)
    nq = seq // Q_BLOCK
    w = NSA_HPG * NSA_HEAD_DIM
    mmap, gexp, expand = _nsa_constants(seq)
    n_rows = k_cmp.shape[2]
    qblk = lambda b, g, i: (b * nq + i, g)
    kv = lambda n: pl.BlockSpec((1, 1, n, NSA_HEAD_DIM), lambda b, g, i: (b, g, 0, 0))
    kern = functools.partial(_nsa_attn_kernel, seq, tk)
    return pl.pallas_call(
        kern,
        grid=(bsz, NSA_KV_GROUPS, nq),
        in_specs=[pl.BlockSpec((Q_BLOCK, w), qblk),
                  pl.BlockSpec((Q_BLOCK, LANES), lambda b, g, i: (b * nq + i, gate_col_block)),
                  pl.BlockSpec((1, LANES, 3 * w), lambda b, g, i: (g, 0, 0)),
                  kv(n_rows), kv(n_rows), kv(seq), kv(seq), kv(seq), kv(seq),
                  pl.BlockSpec(mmap.shape, lambda b, g, i: (0, 0)),
                  pl.BlockSpec(expand.shape, lambda b, g, i: (0, 0))],
        out_specs=pl.BlockSpec((Q_BLOCK, w), qblk),
        out_shape=jax.ShapeDtypeStruct((bsz * seq, NSA_WIDTH), BF16),
        compiler_params=_params("parallel", "parallel", "arbitrary"),
        name="nsa_attention",
    )(qr, h, gexp, k_cmp, v_cmp, ks, vs, kw, vw, mmap, expand)


def nsa_mixer(h, bsz, seq, gate_col_block, pe_k, pe_v, ck_w1, ck_b1, ck_w2, cv_w1, cv_b1, cv_w2):
    qr, kc, vc, ks, vs, kw, vw = nsa_prep(h, bsz, seq)
    k_cmp, v_cmp = nsa_compress(kc, vc, pe_k, pe_v, ck_w1, ck_b1, ck_w2, cv_w1, cv_b1, cv_w2)
    return nsa_attention(qr, h, gate_col_block, k_cmp, v_cmp, ks, vs, kw, vw, bsz, seq)


def _head_ones(width, head_dim):
    idx = np.arange(width) // head_dim
    return jnp.asarray((idx[:, None] == idx[None, :]).astype(np.float32))


def _softplus(x):
    return jnp.maximum(x, 0.0) + jnp.log(1.0 + jnp.exp(-jnp.abs(x)))


def _rwkv_pre_kernel(p_ref, prev_ref, mu_ref, w0_ref, wup_ref, a0_ref, aup_ref, gup_ref, kk_ref, ka_ref, rk_ref,
                     ones_ref, r_o, k_o, v_o, kk_o, b_o, ld_o, g_o, bonus_o):
    w = RWKV_WIDTH
    p = p_ref[...]
    first_row = jnp.where(pl.program_id(1) == 0, 0.0, prev_ref[7:8, :])
    is_row0 = lax.broadcasted_iota(jnp.int32, p.shape, 0) == 0
    prev = jnp.where(is_row0, first_row, pltpu.roll(p, 1, 0))
    ps = p + (prev - p) * mu_ref[...]
    r, k, v = ps[:, 0:w], ps[:, w:2 * w], ps[:, 2 * w:3 * w]
    o = 3 * w
    w_lo = ps[:, o:o + RWKV_LORA_W]
    a_lo = ps[:, o + RWKV_LORA_W:o + RWKV_LORA_W + RWKV_LORA_A]
    g_lo = ps[:, o + RWKV_LORA_W + RWKV_LORA_A:]
    wlog = -_softplus(-(w0_ref[...] + _dot(jnp.tanh(w_lo).astype(BF16), wup_ref[...]))) - 0.5
    a = _sigmoid(a0_ref[...] + _dot(a_lo.astype(BF16), aup_ref[...]))
    g = _dot(_sigmoid(g_lo).astype(BF16), gup_ref[...])
    kk = k * kk_ref[...]
    norm = jnp.sqrt(_dot(kk * kk, ones_ref[...], precision=HIGHEST))
    kk = kk / jnp.maximum(norm, 1e-12)
    k2 = k * (1.0 + (a - 1.0) * ka_ref[...])
    r_o[...] = r
    k_o[...] = k2
    v_o[...] = v
    kk_o[...] = kk
    b_o[...] = kk * a
    ld_o[...] = -jnp.exp(wlog)
    g_o[...] = g
    bonus_o[...] = _dot(r * k2 * rk_ref[...], ones_ref[...], precision=HIGHEST) * v


def rwkv_pre(h, bsz, seq, mu, w0, w_up, a0, a_up, g_up, k_k, k_a, r_k, tl=512):
    tl = min(tl, seq)
    nl = seq // tl
    w = RWKV_WIDTH
    cols = RWKV_COLS
    ones = _head_ones(w, RWKV_HEAD_DIM)
    f2 = lambda b, l: (0, 0)
    vec = pl.BlockSpec((1, w), f2)
    out_spec = pl.BlockSpec((tl, w), lambda b, l: (b * nl + l, 0))
    out_shape = jax.ShapeDtypeStruct((bsz * seq, w), F32)
    return pl.pallas_call(
        _rwkv_pre_kernel,
        grid=(bsz, nl),
        in_specs=[pl.BlockSpec((tl, cols), lambda b, l: (b * nl + l, 0)),
                  pl.BlockSpec((8, cols), lambda b, l: (jnp.maximum((b * seq + l * tl) // 8 - 1, 0), 0)),
                  pl.BlockSpec((1, cols), f2), vec, pl.BlockSpec((RWKV_LORA_W, w), f2),
                  vec, pl.BlockSpec((RWKV_LORA_A, w), f2), pl.BlockSpec((RWKV_LORA_G, w), f2),
                  vec, vec, vec, pl.BlockSpec((w, w), f2)],
        out_specs=[out_spec] * 8,
        out_shape=[out_shape] * 8,
        compiler_params=_params("parallel", "parallel"),
        name="rwkv_pre",
    )(h, h, mu.reshape(1, cols), w0.reshape(1, w), w_up.astype(BF16), a0.reshape(1, w), a_up.astype(BF16),
      g_up.astype(BF16), k_k.reshape(1, w), k_a.reshape(1, w), r_k.reshape(1, w), ones)


def _rwkv_masks():
    t, pk = RWKV_CHUNK, RWKV_PACK
    n = t * pk
    ri = np.arange(n)
    same = (ri[:, None] // t) == (ri[None, :] // t)
    tt, ss = ri[:, None] % t, ri[None, :] % t
    levels = []
    k = 1
    while k < t:
        levels.append(same & (tt // (2 * k) == ss // (2 * k)) & ((tt // k) % 2 == 1) & ((ss // k) % 2 == 0))
        k *= 2
    lvl = np.stack(levels).astype(np.float32)
    tri = (np.arange(t)[:, None] >= np.arange(t)[None, :]).astype(np.float32)
    head_lane = ((ri[:, None] // t) == (np.arange(pk * RWKV_HEAD_DIM)[None, :] // RWKV_HEAD_DIM)).astype(np.float32)
    return (jnp.asarray(tri), jnp.asarray(head_lane), jnp.asarray(same.astype(np.float32)), jnp.asarray(lvl))


def _rwkv_chunk_kernel(r_ref, k_ref, v_ref, kk_ref, b_ref, ld_ref, tri_ref, hl_ref, bd_ref, lvl_ref, y_ref, st_ref):
    t, pk = RWKV_CHUNK, RWKV_PACK
    n = t * pk

    @pl.when(pl.program_id(2) == 0)
    def _():
        st_ref[...] = jnp.zeros_like(st_ref)

    ld = ld_ref[...]
    c = _dot(tri_ref[...], ld, precision=HIGHEST)
    c_end = c[t - 1:t, :]
    e_c = jnp.exp(c)
    e_neg = jnp.exp(-c)
    e_end = jnp.exp(c_end - c)
    r, k, v, kk, b = r_ref[...], k_ref[...], v_ref[...], kk_ref[...], b_ref[...]
    kkd = (kk * jnp.exp(c - ld)).astype(BF16)
    rd = (r * e_c).astype(BF16)
    hl = hl_ref[...]
    bd = bd_ref[...]

    def big(x):
        return (jnp.concatenate([x] * pk, axis=0) * hl).astype(BF16)

    st = st_ref[...]
    st_b = st.astype(BF16)
    v_big = big(v)
    a_all = _dot_nt(jnp.concatenate([kkd, rd], axis=0),
                    jnp.concatenate([big(k * e_neg), big(b * e_neg)], axis=0))
    ti = lax.broadcasted_iota(jnp.int32, (t, n), 0)
    si = lax.broadcasted_iota(jnp.int32, (t, n), 1) % t
    strict = ti > si
    incl = ti >= si
    a_kk = jnp.where(strict, a_all[:t, :n], 0.0)
    a_kb = jnp.where(strict, a_all[:t, n:], 0.0)
    a_rk = jnp.where(incl, a_all[t:, :n], 0.0)
    a_rb = jnp.where(incl, a_all[t:, n:], 0.0)
    rhs = _dot(kkd, st_b) + _dot(a_kk.astype(BF16), v_big)
    a_bd = jnp.concatenate([a_kb] * pk, axis=0) * bd
    eye = jnp.where(lax.broadcasted_iota(jnp.int32, (n, n), 0) == lax.broadcasted_iota(jnp.int32, (n, n), 1), 1.0, 0.0)
    m = eye
    for lv in range(lvl_ref.shape[0]):
        mb = m.astype(BF16)
        m = m - _dot(_dot(mb, (a_bd * lvl_ref[lv]).astype(BF16)).astype(BF16), mb)
    u_big = _dot(m.astype(BF16), big(rhs))
    u = u_big[0:t]
    for h in range(1, pk):
        u = u + u_big[h * t:(h + 1) * t]
    y_ref[...] = _dot(rd, st_b) + _dot(a_rk.astype(BF16), v_big) - _dot(a_rb.astype(BF16), big(u))
    decay_col = jnp.broadcast_to(jnp.exp(c_end), st.shape).T
    kb_end = jnp.concatenate([k * e_end, -(b * e_end)], axis=0).astype(BF16)
    vu = jnp.concatenate([v, u], axis=0).astype(BF16)
    st_ref[...] = decay_col * st + bd * _dot_tn(kb_end, vu)


def rwkv_chunk(r, k, v, kk, b, ld, bsz, seq):
    t, pk = RWKV_CHUNK, RWKV_PACK
    n_chunk = seq // t
    wp = pk * RWKV_HEAD_DIM
    n_pack = RWKV_WIDTH // wp
    assert t == RWKV_HEAD_DIM
    tri, hl, bd, lvl = _rwkv_masks()
    blk = pl.BlockSpec((t, wp), lambda bb, g, c: (bb * n_chunk + c, g))
    f2 = lambda bb, g, c: (0, 0)
    return pl.pallas_call(
        _rwkv_chunk_kernel,
        grid=(bsz, n_pack, n_chunk),
        in_specs=[blk] * 6 + [pl.BlockSpec(tri.shape, f2), pl.BlockSpec(hl.shape, f2), pl.BlockSpec(bd.shape, f2),
                              pl.BlockSpec(lvl.shape, lambda bb, g, c: (0, 0, 0))],
        out_specs=blk,
        out_shape=jax.ShapeDtypeStruct((bsz * seq, RWKV_WIDTH), F32),
        scratch_shapes=[pltpu.VMEM((wp, wp), F32)],
        compiler_params=_params("parallel", "parallel", "arbitrary"),
        name="rwkv_chunk",
    )(r, k, v, kk, b, ld, tri, hl, bd, lvl)


def _rwkv_post_kernel(y_ref, bonus_ref, g_ref, lng_ref, lnb_ref, ones_ref, o_ref):
    y = y_ref[...]
    inv = 1.0 / RWKV_HEAD_DIM
    mu = _dot(y, ones_ref[...], precision=HIGHEST) * inv
    yc = y - mu
    var = _dot(yc * yc, ones_ref[...], precision=HIGHEST) * inv
    yn = yc * lax.rsqrt(var + RWKV_GN_EPS) * lng_ref[...] + lnb_ref[...]
    o_ref[...] = ((yn + bonus_ref[...]) * g_ref[...]).astype(o_ref.dtype)


def rwkv_post(y, bonus, g, ln_g, ln_b, tm=1024):
    n, w = y.shape
    tm = min(tm, n)
    row = pl.BlockSpec((tm, w), lambda i: (i, 0))
    vec = pl.BlockSpec((1, w), lambda i: (0, 0))
    return pl.pallas_call(
        _rwkv_post_kernel,
        grid=(n // tm,),
        in_specs=[row, row, row, vec, vec, pl.BlockSpec((w, w), lambda i: (0, 0))],
        out_specs=row,
        out_shape=jax.ShapeDtypeStruct((n, w), BF16),
        compiler_params=_params("parallel"),
        name="rwkv_post",
    )(y, bonus, g, ln_g.reshape(1, w), ln_b.reshape(1, w), _head_ones(w, RWKV_HEAD_DIM))


def rwkv7_mixer(h, bsz, seq, mu, w0, w_up, a0, a_up, g_up, k_k, k_a, r_k, ln_g, ln_b):
    r, k, v, kk, b, ld, g, bonus = rwkv_pre(h, bsz, seq, mu, w0, w_up, a0, a_up, g_up, k_k, k_a, r_k)
    y = rwkv_chunk(r, k, v, kk, b, ld, bsz, seq)
    return rwkv_post(y, bonus, g, ln_g, ln_b)


AB_IN = S5_WIDTH + NSA_WIDTH + 6 * NSA_KV_W + NSA_GATE_COLS
AB_IN_PADDED = -(-AB_IN // LANES) * LANES
NSA_GATE_COL_BLOCK = (AB_IN - NSA_GATE_COLS) // LANES
PROJ_TM = 512


def kernel(x, ab_w_in, ab_w_out, s5_lam_re, s5_lam_im, s5_log_dt, s5_b_re, s5_b_im, s5_c_re, s5_c_im, s5_d, s5_w_glu, s5_b_glu, nsa_pe_k, nsa_pe_v, nsa_ck_w1, nsa_ck_b1, nsa_ck_w2, nsa_cv_w1, nsa_cv_b1, nsa_cv_w2, cd_w_in, cd_w_out, rwkv_mu, rwkv_w0, rwkv_w_up, rwkv_a0, rwkv_a_up, rwkv_g_up, rwkv_k_k, rwkv_k_a, rwkv_r_k, rwkv_ln_g, rwkv_ln_b, ret_ln_g, ret_ln_b, ln1_g, ln1_b, ln2_g, ln2_b, moe_router, moe_bias, moe_w1, moe_w3, moe_w2, sh_w1, sh_w3, sh_w2):
    bsz, seq, d = x.shape
    assert (AB_IN - NSA_GATE_COLS) % LANES == 0
    xf = x.reshape(bsz * seq, d)
    x_in = xf
    for layer in range(DEPTH):
        i = layer // 2
        if layer % 2 == 0:
            w_in = jnp.pad(ab_w_in[i], ((0, 0), (0, AB_IN_PADDED - AB_IN))).astype(BF16)
            h = project(x_in, w_in, PROJ_TM)
            y_1 = s5_mixer(h, bsz, seq, s5_lam_re[i], s5_lam_im[i], s5_log_dt[i], s5_b_re[i], s5_b_im[i],
                           s5_c_re[i], s5_c_im[i], s5_d[i], s5_w_glu[i], s5_b_glu[i])
            y_2 = nsa_mixer(h, bsz, seq, NSA_GATE_COL_BLOCK, nsa_pe_k[i], nsa_pe_v[i], nsa_ck_w1[i], nsa_ck_b1[i],
                            nsa_ck_w2[i], nsa_cv_w1[i], nsa_cv_b1[i], nsa_cv_w2[i])
            w_out = ab_w_out[i]
        else:
            h = project(x_in, cd_w_in[i].astype(BF16), PROJ_TM)
            y_1 = rwkv7_mixer(h, bsz, seq, rwkv_mu[i], rwkv_w0[i], rwkv_w_up[i], rwkv_a0[i], rwkv_a_up[i],
                              rwkv_g_up[i], rwkv_k_k[i], rwkv_k_a[i], rwkv_r_k[i], rwkv_ln_g[i], rwkv_ln_b[i])
            y_2 = retention_mixer(h, bsz, seq, RWKV_COLS, ret_ln_g[i], ret_ln_b[i])
            w_out = cd_w_out[i]
        xf, x_bf = out_proj_ln(y_1, y_2, w_out, xf, ln1_g[layer], ln1_b[layer])
        xf, x_in = moe_block(xf, x_bf, moe_router[layer], moe_bias[layer], moe_w1[layer], moe_w3[layer],
                             moe_w2[layer], sh_w1[layer], sh_w3[layer], sh_w2[layer], ln2_g[layer], ln2_b[layer])
    return xf.reshape(bsz, seq, d)
```

```python
import functools
import math

import jax
import jax.numpy as jnp
import numpy as np
from jax import lax
from jax.experimental import pallas as pl
from jax.experimental.pallas import tpu as pltpu

F32 = jnp.float32
BF16 = jnp.bfloat16
HIGHEST = lax.Precision.HIGHEST

VMEM_LIMIT_BYTES = 52 * 1024 * 1024
LANES = 128

LN_EPS = 1e-5
DEPTH = 2
ALPHA = (2 * DEPTH) ** 0.25

S5_GROUPS, S5_GROUP_CH, S5_STATE = 32, 16, 64
S5_WIDTH = S5_GROUPS * S5_GROUP_CH
S5_CHUNK = 16
S5_PACK = 8
NSA_HEADS, NSA_KV_GROUPS, NSA_HEAD_DIM = 8, 2, 64
NSA_HPG = NSA_HEADS // NSA_KV_GROUPS
NSA_WIDTH = NSA_HEADS * NSA_HEAD_DIM
NSA_ROT_DIM = NSA_HEAD_DIM // 4
ROPE_THETA = 500000.0
CMP_BLOCK, CMP_STRIDE, CMP_HIDDEN = 32, 16, 128
SLC_BLOCK, N_SLC, WINDOW, Q_BLOCK = 64, 16, 512, 128
FORCE_SCORE = 1e6
MASK_VALUE = -1e30
RWKV_HEADS, RWKV_HEAD_DIM = 8, 64
RWKV_WIDTH = RWKV_HEADS * RWKV_HEAD_DIM
RWKV_LORA_W, RWKV_LORA_A, RWKV_LORA_G = 64, 64, 128
RWKV_COLS = 3 * RWKV_WIDTH + RWKV_LORA_W + RWKV_LORA_A + RWKV_LORA_G
RWKV_GN_EPS = 64e-5
RWKV_CHUNK = 64
RWKV_PACK = 4
RET_HEADS, RET_DK, RET_DV, RET_CHUNK = 4, 64, 128, 128
RET_THETA = 10000.0
RET_GN_EPS = 1e-5
N_EXPERTS, TOP_K, EXPERT_FF = 64, 8, 256
N_EXPERT_GROUPS, TOPK_GROUPS = 8, 4
EXPERTS_PER_GROUP = N_EXPERTS // N_EXPERT_GROUPS
ROUTED_SCALE = 2.5


def _params(*sem):
    return pltpu.CompilerParams(dimension_semantics=sem, vmem_limit_bytes=VMEM_LIMIT_BYTES)


def _dot(a, b, **kw):
    return jnp.dot(a, b, preferred_element_type=F32, **kw)


def _dot_nt(a, b, **kw):
    return lax.dot_general(a, b, (((1,), (1,)), ((), ())), preferred_element_type=F32, **kw)


def _dot_tn(a, b, **kw):
    return lax.dot_general(a, b, (((0,), (0,)), ((), ())), preferred_element_type=F32, **kw)


def _run_interleaved(gens):
    results = [None] * len(gens)
    live = list(range(len(gens)))
    while live:
        for i in list(live):
            try:
                next(gens[i])
            except StopIteration as done:
                results[i] = done.value
                live.remove(i)
    return results


def _gelu(x):
    return 0.5 * x * (1.0 + jnp.tanh(math.sqrt(2.0 / math.pi) * (x + 0.044715 * (x * x * x))))


def _sigmoid(x):
    return 1.0 / (1.0 + jnp.exp(-x))


def _layer_norm_rows(z, g, b):
    mu = jnp.mean(z, axis=-1, keepdims=True)
    zc = z - mu
    var = jnp.mean(zc * zc, axis=-1, keepdims=True)
    return zc * lax.rsqrt(var + LN_EPS) * g + b


def _proj_kernel(x_ref, w_ref, o_ref):
    o_ref[...] = _dot(x_ref[...].astype(BF16), w_ref[...]).astype(o_ref.dtype)


def project(x, w_bf16, tm):
    m, k = x.shape
    n = w_bf16.shape[1]
    return pl.pallas_call(
        _proj_kernel,
        grid=(m // tm,),
        in_specs=[pl.BlockSpec((tm, k), lambda i: (i, 0)), pl.BlockSpec((k, n), lambda i: (0, 0))],
        out_specs=pl.BlockSpec((tm, n), lambda i: (i, 0)),
        out_shape=jax.ShapeDtypeStruct((m, n), F32),
        compiler_params=_params("parallel"),
        name="project",
    )(x, w_bf16)


def _out_proj_ln_kernel(ya_ref, yb_ref, wa_ref, wb_ref, x_ref, g_ref, b_ref, o_ref, obf_ref):
    mix = _dot(ya_ref[...], wa_ref[...]) + _dot(yb_ref[...], wb_ref[...])
    y = _layer_norm_rows(ALPHA * x_ref[...] + mix, g_ref[...], b_ref[...])
    o_ref[...] = y
    obf_ref[...] = y.astype(BF16)


def out_proj_ln(ya, yb, w_out, x, g, b, tm=512):
    n, d = x.shape
    ka, kb = ya.shape[1], yb.shape[1]
    wa = w_out[:ka].astype(BF16)
    wb = w_out[ka:].astype(BF16)
    row = lambda i: (i, 0)
    fixed = lambda i: (0, 0)
    return pl.pallas_call(
        _out_proj_ln_kernel,
        grid=(n // tm,),
        in_specs=[pl.BlockSpec((tm, ka), row), pl.BlockSpec((tm, kb), row),
                  pl.BlockSpec((ka, d), fixed), pl.BlockSpec((kb, d), fixed),
                  pl.BlockSpec((tm, d), row), pl.BlockSpec((1, d), fixed), pl.BlockSpec((1, d), fixed)],
        out_specs=[pl.BlockSpec((tm, d), row), pl.BlockSpec((tm, d), row)],
        out_shape=[jax.ShapeDtypeStruct((n, d), F32), jax.ShapeDtypeStruct((n, d), BF16)],
        compiler_params=_params("parallel"),
        name="out_proj_ln",
    )(ya, yb, wa, wb, x, g.reshape(1, d), b.reshape(1, d))


def _router_kernel(x_ref, rt_ref, bias_ref, o_ref):
    tr = x_ref.shape[0]
    scores = _sigmoid(_dot_nt(rt_ref[...], x_ref[...], precision=HIGHEST))
    biased = scores + bias_ref[...]
    grp = biased.reshape(N_EXPERT_GROUPS, EXPERTS_PER_GROUP, tr)
    pos = lax.broadcasted_iota(jnp.int32, grp.shape, 1)
    m1 = jnp.max(grp, axis=1, keepdims=True)
    first = jnp.min(jnp.where(grp == m1, pos, EXPERTS_PER_GROUP), axis=1, keepdims=True)
    m2 = jnp.max(jnp.where(pos == first, -jnp.inf, grp), axis=1, keepdims=True)
    gscore = (m1 + m2).reshape(N_EXPERT_GROUPS, tr)
    gidx = lax.broadcasted_iota(jnp.int32, gscore.shape, 0)
    grank = jnp.zeros(gscore.shape, F32)
    for j in range(N_EXPERT_GROUPS):
        row = gscore[j:j + 1, :]
        grank = grank + jnp.where(gidx > j, jnp.where(row >= gscore, 1.0, 0.0), jnp.where(row > gscore, 1.0, 0.0))
    gkeep = jnp.where(grank < TOPK_GROUPS, 1.0, 0.0)
    keep = jnp.broadcast_to(gkeep[:, None, :], grp.shape).reshape(N_EXPERTS, tr)
    masked = jnp.where(keep > 0.5, biased, -jnp.inf)
    eidx = lax.broadcasted_iota(jnp.int32, masked.shape, 0)
    rank = jnp.zeros(masked.shape, F32)
    for j in range(N_EXPERTS):
        row = masked[j:j + 1, :]
        rank = rank + jnp.where(eidx > j, jnp.where(row >= masked, 1.0, 0.0), jnp.where(row > masked, 1.0, 0.0))
    gate = jnp.where(rank < TOP_K, scores, 0.0)
    gate = gate / jnp.sum(gate, axis=0, keepdims=True) * ROUTED_SCALE
    o_ref[...] = jnp.concatenate([gate, jnp.zeros((LANES - N_EXPERTS, tr), F32)], axis=0).T


def moe_router(x, router, bias, tr=512):
    n, d = x.shape
    return pl.pallas_call(
        _router_kernel,
        grid=(n // tr,),
        in_specs=[pl.BlockSpec((tr, d), lambda i: (i, 0)),
                  pl.BlockSpec((N_EXPERTS, d), lambda i: (0, 0)),
                  pl.BlockSpec((N_EXPERTS, 1), lambda i: (0, 0))],
        out_specs=pl.BlockSpec((tr, LANES), lambda i: (i, 0)),
        out_shape=jax.ShapeDtypeStruct((n, LANES), F32),
        compiler_params=_params("parallel"),
        name="moe_router",
    )(x, router.T, bias.reshape(N_EXPERTS, 1))


def _swiglu_hidden(xb, w1, w3):
    h1 = _dot(xb, w1.astype(BF16))
    return h1 * _sigmoid(h1) * _dot(xb, w3.astype(BF16))


def _experts_ln_kernel(xbf_ref, x_ref, gate_ref, w1_ref, w3_ref, w2_ref, sw1_ref, sw3_ref, sw2_ref, g_ref, b_ref,
                       o_ref, obf_ref, acc_ref):
    e = pl.program_id(1)
    xb = xbf_ref[...]

    @pl.when(e == 0)
    def _():
        h = _swiglu_hidden(xb, sw1_ref[...], sw3_ref[...])
        acc_ref[...] = _dot(h.astype(BF16), sw2_ref[...].astype(BF16))

    lane = lax.broadcasted_iota(jnp.int32, gate_ref.shape, 1)
    gcol = jnp.sum(jnp.where(lane == e, gate_ref[...], 0.0), axis=1, keepdims=True)
    h = _swiglu_hidden(xb, w1_ref[0], w3_ref[0]) * gcol
    acc_ref[...] += _dot(h.astype(BF16), w2_ref[0].astype(BF16))

    @pl.when(e == pl.num_programs(1) - 1)
    def _():
        y = _layer_norm_rows(ALPHA * x_ref[...] + acc_ref[...], g_ref[...], b_ref[...])
        o_ref[...] = y
        obf_ref[...] = y.astype(BF16)


def moe_experts_ln(x, x_bf16, gates, w1, w3, w2, sw1, sw3, sw2, g, b, tm=1024):
    n, d = x.shape
    ne = w1.shape[0]
    tok = lambda i, e: (i, 0)
    fixed = lambda i, e: (0, 0)
    return pl.pallas_call(
        _experts_ln_kernel,
        grid=(n // tm, ne),
        in_specs=[pl.BlockSpec((tm, d), tok), pl.BlockSpec((tm, d), tok), pl.BlockSpec((tm, LANES), tok),
                  pl.BlockSpec((1, d, EXPERT_FF), lambda i, e: (e, 0, 0)),
                  pl.BlockSpec((1, d, EXPERT_FF), lambda i, e: (e, 0, 0)),
                  pl.BlockSpec((1, EXPERT_FF, d), lambda i, e: (e, 0, 0)),
                  pl.BlockSpec((d, EXPERT_FF), fixed), pl.BlockSpec((d, EXPERT_FF), fixed),
                  pl.BlockSpec((EXPERT_FF, d), fixed),
                  pl.BlockSpec((1, d), fixed), pl.BlockSpec((1, d), fixed)],
        out_specs=[pl.BlockSpec((tm, d), tok), pl.BlockSpec((tm, d), tok)],
        out_shape=[jax.ShapeDtypeStruct((n, d), F32), jax.ShapeDtypeStruct((n, d), BF16)],
        scratch_shapes=[pltpu.VMEM((tm, d), F32)],
        compiler_params=_params("parallel", "arbitrary"),
        name="moe_experts_ln",
    )(x_bf16, x, gates, w1, w3, w2, sw1, sw3, sw2, g.reshape(1, d), b.reshape(1, d))


def moe_block(x, x_bf16, router, bias, w1, w3, w2, sw1, sw3, sw2, g, b):
    gates = moe_router(x, router, bias)
    return moe_experts_ln(x, x_bf16, gates, w1, w3, w2, sw1, sw3, sw2, g, b)


def _s5_tables(lam_re, lam_im, log_dt, b_re, b_im, c_re, c_im, n_chunk):
    t, h, p = S5_CHUNK, S5_GROUP_CH, S5_STATE
    dt = jnp.exp(log_dt.astype(F32))[:, None]
    den = lam_re ** 2 + lam_im ** 2

    def lam_pow(k):
        k = jnp.asarray(k, F32)[..., None, None]
        mag = jnp.exp(lam_re * dt * k)
        return mag * jnp.cos(lam_im * dt * k), mag * jnp.sin(lam_im * dt * k)

    lb_re, lb_im = lam_pow(1.0)
    f_re = ((lb_re - 1.0) * lam_re + lb_im * lam_im) / den
    f_im = (lb_im * lam_re - (lb_re - 1.0) * lam_im) / den
    bb_re = f_re[..., None] * b_re - f_im[..., None] * b_im
    bb_im = f_re[..., None] * b_im + f_im[..., None] * b_re
    pr, pi = lam_pow(jnp.arange(t))
    cl_re = c_re[None] * pr[:, :, None, :] - c_im[None] * pi[:, :, None, :]
    cl_im = c_re[None] * pi[:, :, None, :] + c_im[None] * pr[:, :, None, :]
    klag = jnp.einsum('tgop,gpi->tgoi', cl_re, bb_re) - jnp.einsum('tgop,gpi->tgoi', cl_im, bb_im)
    i_idx = jnp.arange(t)[:, None]
    j_idx = jnp.arange(t)[None, :]
    lag = i_idx - j_idx
    toe = jnp.where((lag >= 0)[:, :, None, None, None], klag[jnp.clip(lag, 0)], 0.0)
    nb = S5_GROUPS // S5_PACK
    eye = jnp.eye(S5_PACK, dtype=F32)
    lanes = t * S5_PACK * h
    split = lambda a, axis: a.reshape(a.shape[:axis] + (nb, S5_PACK) + a.shape[axis + 1:])
    w_toe = jnp.einsum('ijbgoh,gk->bjghiko', split(toe, 2), eye).reshape(nb, lanes, lanes)
    qr, qi = lam_pow(t - 1 - jnp.arange(t))
    st_re = qr[..., None] * bb_re[None] - qi[..., None] * bb_im[None]
    st_im = qr[..., None] * bb_im[None] + qi[..., None] * bb_re[None]
    pack_state = lambda a: jnp.einsum('jbgph,gk->bjghkp', split(a, 1), eye).reshape(nb, lanes, S5_PACK * p)
    w_cat = jnp.concatenate([w_toe, pack_state(st_re), pack_state(st_im)], axis=-1)
    er, ei = lam_pow(jnp.arange(t) + 1)
    x_re = c_re[None] * er[:, :, None, :] - c_im[None] * ei[:, :, None, :]
    x_im = c_re[None] * ei[:, :, None, :] + c_im[None] * er[:, :, None, :]
    pack_cross = lambda a: jnp.einsum('ibgop,gk->bkpigo', split(a, 1), eye).reshape(nb, S5_PACK * p, lanes)
    w_cross = jnp.concatenate([pack_cross(x_re), -pack_cross(x_im)], axis=1)
    levels = max(1, int(math.log2(n_chunk)))
    sr, si = lam_pow(t * (2.0 ** jnp.arange(levels)))
    sr = sr.reshape(levels, nb, S5_PACK * p)
    si = si.reshape(levels, nb, S5_PACK * p)
    a1 = jnp.concatenate([sr, sr], axis=-1)
    a2 = jnp.concatenate([-si, si], axis=-1)
    scan = jnp.transpose(jnp.stack([a1, a2], axis=1), (2, 0, 1, 3))
    return w_cat.astype(BF16), w_cross.astype(BF16), scan.astype(F32)


def _s5_kernel(h_ref, wcat_ref, wcross_ref, scan_ref, o_ref):
    n_chunk, t, _ = h_ref.shape
    x = jnp.concatenate([h_ref[:, j, :] for j in range(t)], axis=1).astype(BF16)
    ye = _dot(x, wcat_ref[0])
    th = x.shape[1]
    local = ye[:, :th]
    state = ye[:, th:]
    row = lax.broadcasted_iota(jnp.int32, state.shape, 0)
    s = jnp.where(row >= 1, pltpu.roll(state, 1, 0), 0.0)
    half = state.shape[1] // 2
    level = 0
    d = 1
    while d < n_chunk:
        mult = scan_ref[0, level]
        prev = jnp.where(row >= d, pltpu.roll(s, d, 0), 0.0)
        s = s + mult[0:1, :] * prev + mult[1:2, :] * pltpu.roll(prev, half, 1)
        d *= 2
        level += 1
    y = local + _dot(s.astype(BF16), wcross_ref[0])
    for i in range(t):
        o_ref[:, i, :] = y[:, i * LANES:(i + 1) * LANES]


def s5_scan(h, bsz, seq, w_cat, w_cross, scan):
    t = S5_CHUNK
    n_chunk = seq // t
    nb = S5_WIDTH // LANES
    once = pl.Buffered(1)
    y = pl.pallas_call(
        _s5_kernel,
        grid=(nb, bsz),
        in_specs=[pl.BlockSpec((n_chunk, t, LANES), lambda j, b: (b, 0, j)),
                  pl.BlockSpec((1,) + w_cat.shape[1:], lambda j, b: (j, 0, 0), pipeline_mode=once),
                  pl.BlockSpec((1,) + w_cross.shape[1:], lambda j, b: (j, 0, 0), pipeline_mode=once),
                  pl.BlockSpec((1,) + scan.shape[1:], lambda j, b: (j, 0, 0, 0))],
        out_specs=pl.BlockSpec((n_chunk, t, LANES), lambda j, b: (b, 0, j)),
        out_shape=jax.ShapeDtypeStruct((bsz * n_chunk, t, S5_WIDTH), F32),
        compiler_params=_params("parallel", "parallel"),
        name="s5_scan",
    )(h.reshape(bsz * n_chunk, t, h.shape[1]), w_cat, w_cross, scan)
    return y.reshape(bsz * seq, S5_WIDTH)


def _s5_post_kernel(y_ref, u_ref, d_ref, w_ref, b_ref, o_ref):
    y = _gelu(y_ref[...] + d_ref[...] * u_ref[...])
    o_ref[...] = (y * _sigmoid(_dot(y.astype(BF16), w_ref[...]) + b_ref[...])).astype(o_ref.dtype)


def s5_post(y, h, d_skip, w_glu, b_glu, tm=1024):
    n, w = y.shape
    row = lambda i: (i, 0)
    fixed = lambda i: (0, 0)
    return pl.pallas_call(
        _s5_post_kernel,
        grid=(n // tm,),
        in_specs=[pl.BlockSpec((tm, w), row), pl.BlockSpec((tm, w), row), pl.BlockSpec((1, w), fixed),
                  pl.BlockSpec((w, w), fixed), pl.BlockSpec((1, w), fixed)],
        out_specs=pl.BlockSpec((tm, w), row),
        out_shape=jax.ShapeDtypeStruct((n, w), BF16),
        compiler_params=_params("parallel"),
        name="s5_post",
    )(y, h, d_skip.reshape(1, w), w_glu.astype(BF16), b_glu.reshape(1, w))


def s5_mixer(h, bsz, seq, lam_re, lam_im, log_dt, b_re, b_im, c_re, c_im, d_skip, w_glu, b_glu):
    w_cat, w_cross, scan = _s5_tables(lam_re, lam_im, log_dt, b_re, b_im, c_re, c_im, seq // S5_CHUNK)
    y = s5_scan(h, bsz, seq, w_cat, w_cross, scan)
    return s5_post(y, h, d_skip, w_glu, b_glu, tm=min(1024, bsz * seq))


def _rope_tables(pos, rot_dim, theta, head_dim, n_heads):
    half = rot_dim // 2
    f32 = np.float32
    inv_freq = f32(theta) ** (-np.arange(half, dtype=f32) / f32(half))
    ang = (pos.astype(f32)[:, None] * inv_freq[None, :]).astype(np.float64)
    cos, sin = np.cos(ang), np.sin(ang)
    rest = head_dim - rot_dim
    n = pos.shape[0]
    c = np.concatenate([cos, cos, np.ones((n, rest))], axis=1)
    s_up = np.concatenate([-sin, np.zeros((n, half + rest))], axis=1)
    s_dn = np.concatenate([np.zeros((n, half)), sin, np.zeros((n, rest))], axis=1)
    tile = lambda a: jnp.asarray(np.tile(a, (1, n_heads)), F32)
    return tile(c), tile(s_up), tile(s_dn)


def _rope_apply(x, c, s_up, s_dn, half):
    return x * c + pltpu.roll(x, LANES - half, 1) * s_up + pltpu.roll(x, half, 1) * s_dn


def _retention_tables():
    c = RET_CHUNK
    log_gamma = np.log(1.0 - 2.0 ** (-5.0 - np.arange(RET_HEADS, dtype=np.float64)))
    i = np.arange(c, dtype=np.float64)
    diff = i[:, None] - i[None, :]
    decay = np.where(diff >= 0, np.exp(diff[None] * log_gamma[:, None, None]), 0.0)
    qdec = np.repeat(np.exp((i + 1.0)[:, None] * log_gamma[None, :]), RET_DK, axis=1)
    kdec = np.repeat(np.exp((c - 1.0 - i)[:, None] * log_gamma[None, :]), RET_DK, axis=1)
    chunk_decay = [float(v) for v in np.exp(c * log_gamma)]
    return jnp.asarray(decay, F32), jnp.asarray(qdec, F32), jnp.asarray(kdec, F32), chunk_decay


def _retention_kernel(chunk_decay, q_ref, k_ref, v0_ref, v1_ref, g0_ref, g1_ref, c_ref, su_ref, sd_ref,
                      dec_ref, qdec_ref, kdec_ref, lng_ref, lnb_ref, o_ref, state_ref):
    @pl.when(pl.program_id(1) == 0)
    def _():
        state_ref[...] = jnp.zeros_like(state_ref)

    half = RET_DK // 2
    tabs = (c_ref[...], su_ref[...], sd_ref[...])
    q = jnp.concatenate([_rope_apply(q_ref[:, s:s + LANES], *tabs, half) for s in (0, LANES)], axis=1)
    k = jnp.concatenate([_rope_apply(k_ref[:, s:s + LANES], *tabs, half) for s in (0, LANES)], axis=1)
    k = k * (RET_DK ** -0.5)
    q_dec = q * qdec_ref[...]
    k_dec = k * kdec_ref[...]
    v = jnp.concatenate([v0_ref[...], v1_ref[...]], axis=1)
    gate = jnp.concatenate([g0_ref[...], g1_ref[...]], axis=1)
    outs = []
    for h in range(RET_HEADS):
        ks = slice(h * RET_DK, (h + 1) * RET_DK)
        vs = slice(h * RET_DV, (h + 1) * RET_DV)
        vh = v[:, vs].astype(BF16)
        scores = _dot_nt(q[:, ks].astype(BF16), k[:, ks].astype(BF16)) * dec_ref[h]
        y = _dot(scores.astype(BF16), vh) + _dot(q_dec[:, ks].astype(BF16), state_ref[h].astype(BF16))
        state_ref[h] = state_ref[h] * chunk_decay[h] + _dot_tn(k_dec[:, ks].astype(BF16), vh)
        mu = jnp.mean(y, axis=-1, keepdims=True)
        yc = y - mu
        var = jnp.mean(yc * yc, axis=-1, keepdims=True)
        outs.append(yc * lax.rsqrt(var + RET_GN_EPS))
    yn = jnp.concatenate(outs, axis=1) * lng_ref[...] + lnb_ref[...]
    o_ref[...] = (gate * _sigmoid(gate) * yn).astype(o_ref.dtype)


def retention_mixer(h, bsz, seq, col0, ln_g, ln_b):
    c = RET_CHUNK
    n_chunk = seq // c
    qk_w = RET_HEADS * RET_DK
    v_w = RET_HEADS * RET_DV
    assert col0 % qk_w == 0 and qk_w == 2 * LANES and v_w == 2 * qk_w
    cb = col0 // qk_w
    rc, rsu, rsd = _rope_tables(np.arange(seq), RET_DK, RET_THETA, RET_DK, 2)
    dec, qdec, kdec, chunk_decay = _retention_tables()
    row = lambda j: (lambda b, n: (b * n_chunk + n, j))
    pos = lambda b, n: (n, 0)
    fixed2 = lambda b, n: (0, 0)
    kern = functools.partial(_retention_kernel, chunk_decay)
    return pl.pallas_call(
        kern,
        grid=(bsz, n_chunk),
        in_specs=[pl.BlockSpec((c, qk_w), row(cb)), pl.BlockSpec((c, qk_w), row(cb + 1)),
                  pl.BlockSpec((c, qk_w), row(cb + 2)), pl.BlockSpec((c, qk_w), row(cb + 3)),
                  pl.BlockSpec((c, qk_w), row(cb + 4)), pl.BlockSpec((c, qk_w), row(cb + 5)),
                  pl.BlockSpec((c, LANES), pos), pl.BlockSpec((c, LANES), pos), pl.BlockSpec((c, LANES), pos),
                  pl.BlockSpec((RET_HEADS, c, c), lambda b, n: (0, 0, 0)),
                  pl.BlockSpec((c, qk_w), fixed2), pl.BlockSpec((c, qk_w), fixed2),
                  pl.BlockSpec((1, v_w), fixed2), pl.BlockSpec((1, v_w), fixed2)],
        out_specs=pl.BlockSpec((c, v_w), lambda b, n: (b * n_chunk + n, 0)),
        out_shape=jax.ShapeDtypeStruct((bsz * seq, v_w), BF16),
        scratch_shapes=[pltpu.VMEM((RET_HEADS, RET_DK, RET_DV), F32)],
        compiler_params=_params("parallel", "arbitrary"),
        name="retention",
    )(h, h, h, h, h, h, rc, rsu, rsd, dec, qdec, kdec, ln_g.reshape(1, v_w), ln_b.reshape(1, v_w))


NSA_KV_W = NSA_KV_GROUPS * NSA_HEAD_DIM
NSA_GATE_COLS = 3 * NSA_HEADS


def _nsa_prep_kernel(q_ref, kvc_ref, kvs_ref, kvw_ref, c_ref, su_ref, sd_ref,
                     qo_ref, kc_ref, vc_ref, ks_ref, vs_ref, kw_ref, vw_ref):
    half = NSA_ROT_DIM // 2
    tabs = (c_ref[...], su_ref[...], sd_ref[...])
    scale = NSA_HEAD_DIM ** -0.5 * math.log2(math.e)
    qo_ref[...] = jnp.concatenate(
        [_rope_apply(q_ref[:, s:s + LANES], *tabs, half) * scale for s in range(0, NSA_WIDTH, LANES)],
        axis=1).astype(qo_ref.dtype)

    def split(x, o_ref):
        for g in range(NSA_KV_GROUPS):
            o_ref[0, g] = x[:, g * NSA_HEAD_DIM:(g + 1) * NSA_HEAD_DIM].astype(o_ref.dtype)

    split(kvc_ref[:, :NSA_KV_W], kc_ref)
    split(kvc_ref[:, NSA_KV_W:], vc_ref)
    split(_rope_apply(kvs_ref[:, :NSA_KV_W], *tabs, half), ks_ref)
    split(kvs_ref[:, NSA_KV_W:], vs_ref)
    split(_rope_apply(kvw_ref[:, :NSA_KV_W], *tabs, half), kw_ref)
    split(kvw_ref[:, NSA_KV_W:], vw_ref)


def nsa_prep(h, bsz, seq, tl=512):
    tl = min(tl, seq)
    nl = seq // tl
    rc, rsu, rsd = _rope_tables(np.arange(seq), NSA_ROT_DIM, ROPE_THETA, NSA_HEAD_DIM, LANES // NSA_HEAD_DIM)
    row = lambda j: (lambda b, l: (b * nl + l, j))
    pos = lambda b, l: (l, 0)
    kv_out = pl.BlockSpec((1, NSA_KV_GROUPS, tl, NSA_HEAD_DIM), lambda b, l: (b, 0, l, 0))
    kv_shape = lambda dt: jax.ShapeDtypeStruct((bsz, NSA_KV_GROUPS, seq, NSA_HEAD_DIM), dt)
    two = 2 * NSA_KV_W
    return pl.pallas_call(
        _nsa_prep_kernel,
        grid=(bsz, nl),
        in_specs=[pl.BlockSpec((tl, NSA_WIDTH), row(1)),
                  pl.BlockSpec((tl, two), row(4)), pl.BlockSpec((tl, two), row(5)), pl.BlockSpec((tl, two), row(6)),
                  pl.BlockSpec((tl, LANES), pos), pl.BlockSpec((tl, LANES), pos), pl.BlockSpec((tl, LANES), pos)],
        out_specs=[pl.BlockSpec((tl, NSA_WIDTH), lambda b, l: (b * nl + l, 0)),
                   kv_out, kv_out, kv_out, kv_out, kv_out, kv_out],
        out_shape=[jax.ShapeDtypeStruct((bsz * seq, NSA_WIDTH), BF16),
                   kv_shape(F32), kv_shape(F32), kv_shape(BF16), kv_shape(BF16), kv_shape(BF16), kv_shape(BF16)],
        compiler_params=_params("parallel", "parallel"),
        name="nsa_prep",
    )(h, h, h, h, rc, rsu, rsd)


def _nsa_compress_kernel(hk_ref, hv_ref, pek_ref, pev_ref, kw1_ref, kb1_ref, kw2_ref, vw1_ref, vb1_ref, vw2_ref,
                         c_ref, su_ref, sd_ref, ko_ref, vo_ref):
    def mlp(h_ref, pe_ref, w1_ref, b1_ref, w2_ref):
        hb = h_ref[0, 0]
        rows = hb.shape[0]
        first = _dot((hb + pe_ref[0:1, :]).astype(BF16), w1_ref[0])
        second = _dot((hb + pe_ref[1:2, :]).astype(BF16), w1_ref[1])
        hid = _gelu(first + pltpu.roll(second, rows - 1, 0) + b1_ref[...])
        return _dot(hid.astype(BF16), w2_ref[...])

    kc = _rope_apply(mlp(hk_ref, pek_ref, kw1_ref, kb1_ref, kw2_ref), c_ref[...], su_ref[...], sd_ref[...],
                     NSA_ROT_DIM // 2)
    vc = mlp(hv_ref, pev_ref, vw1_ref, vb1_ref, vw2_ref)
    ko_ref[0, 0] = kc[:, :NSA_HEAD_DIM].astype(ko_ref.dtype)
    vo_ref[0, 0] = vc[:, :NSA_HEAD_DIM].astype(vo_ref.dtype)


def nsa_compress(kc, vc, pe_k, pe_v, ck_w1, ck_b1, ck_w2, cv_w1, cv_b1, cv_w2):
    bsz, grp, seq, d = kc.shape
    n_rows = seq // CMP_STRIDE
    flat = CMP_STRIDE * d
    cmp_end = np.arange(n_rows) * CMP_STRIDE + CMP_BLOCK - 1
    rc, rsu, rsd = _rope_tables(cmp_end, NSA_ROT_DIM, ROPE_THETA, NSA_HEAD_DIM, LANES // NSA_HEAD_DIM)
    pad_w2 = lambda w: jnp.pad(w, ((0, 0), (0, LANES - d))).astype(BF16)
    blk = pl.BlockSpec((1, 1, n_rows, flat), lambda b, g: (b, g, 0, 0))
    f2 = lambda b, g: (0, 0)
    f3 = lambda b, g: (0, 0, 0)
    w_specs = [pl.BlockSpec((2, flat, CMP_HIDDEN), f3), pl.BlockSpec((1, CMP_HIDDEN), f2),
               pl.BlockSpec((CMP_HIDDEN, LANES), f2)]
    out_spec = pl.BlockSpec((1, 1, n_rows, d), lambda b, g: (b, g, 0, 0))
    out_shape = jax.ShapeDtypeStruct((bsz, grp, n_rows, d), BF16)
    return pl.pallas_call(
        _nsa_compress_kernel,
        grid=(bsz, grp),
        in_specs=[blk, blk, pl.BlockSpec((2, flat), f2), pl.BlockSpec((2, flat), f2)] + w_specs + w_specs
                 + [pl.BlockSpec((n_rows, LANES), f2)] * 3,
        out_specs=[out_spec, out_spec],
        out_shape=[out_shape, out_shape],
        compiler_params=_params("parallel", "parallel"),
        name="nsa_compress",
    )(kc.reshape(bsz, grp, n_rows, flat), vc.reshape(bsz, grp, n_rows, flat),
      pe_k.reshape(2, flat), pe_v.reshape(2, flat),
      ck_w1.reshape(2, flat, CMP_HIDDEN).astype(BF16), ck_b1.reshape(1, CMP_HIDDEN), pad_w2(ck_w2),
      cv_w1.reshape(2, flat, CMP_HIDDEN).astype(BF16), cv_b1.reshape(1, CMP_HIDDEN), pad_w2(cv_w2),
      rc, rsu, rsd)


def _bias_rows(bias):
    return jnp.concatenate([bias] * NSA_HPG, axis=0)


def _nsa_attn_kernel(seq, tk, q_ref, gate_ref, gexp_ref, kc_ref, vc_ref, ks_ref, vs_ref, kw_ref, vw_ref,
                     mmap_ref, expand_ref, o_ref):
    n_blk = seq // SLC_BLOCK
    n_sel = min(N_SLC, n_blk)
    hd = NSA_HEAD_DIM
    w = NSA_HPG * hd
    groups = range(NSA_KV_GROUPS)
    q0 = pl.program_id(1) * Q_BLOCK
    rows = NSA_HPG * Q_BLOCK
    t_q = q0 + lax.broadcasted_iota(jnp.int32, (Q_BLOCK, 1), 0)
    t_l = q0 + lax.broadcasted_iota(jnp.int32, (1, Q_BLOCK), 1)

    def select(g):
        q = q_ref[:, g * w:(g + 1) * w]
        qs = jnp.concatenate([q[:, h * hd:(h + 1) * hd] for h in range(NSA_HPG)], axis=0)
        kc = kc_ref[0, g]
        n_cmp = kc.shape[0]
        cmp_end = lax.broadcasted_iota(jnp.int32, (1, n_cmp), 1) * CMP_STRIDE + (CMP_BLOCK - 1)
        s = _dot_nt(qs, kc) + _bias_rows(jnp.where(cmp_end <= t_q, 0.0, MASK_VALUE))
        yield
        p = jnp.exp2(s - jnp.max(s, axis=-1, keepdims=True))
        any_key = _bias_rows(jnp.where(t_q >= CMP_BLOCK - 1, 1.0, 0.0))
        inv_l = any_key / jnp.sum(p, axis=-1, keepdims=True)
        o_cmp = _dot(p.astype(BF16), vc_ref[0, g]) * inv_l
        yield
        p = p * inv_l
        imp = p[0:Q_BLOCK]
        for h in range(1, NSA_HPG):
            imp = imp + p[h * Q_BLOCK:(h + 1) * Q_BLOCK]
        imp_t = _dot(imp, mmap_ref[...], precision=HIGHEST).T
        yield
        blk = lax.broadcasted_iota(jnp.int32, (n_blk, 1), 0)
        cur = t_l // SLC_BLOCK
        score = jnp.where(blk == 0, FORCE_SCORE,
                          jnp.where(blk == cur, FORCE_SCORE, jnp.where(blk == cur - 1, FORCE_SCORE, imp_t)))
        score = jnp.where(blk * SLC_BLOCK <= t_l, score, -FORCE_SCORE)
        sel_t = jnp.zeros((n_blk, Q_BLOCK), F32)
        for _ in range(n_sel):
            best = jnp.max(score, axis=0, keepdims=True)
            idx = jnp.min(jnp.where(score == best, blk, n_blk), axis=0, keepdims=True)
            pick = blk == idx
            sel_t = jnp.where(pick, 1.0, sel_t)
            score = jnp.where(pick, -jnp.inf, score)
            yield
        return qs, o_cmp, sel_t.T.astype(BF16)

    selected = _run_interleaved([select(g) for g in groups])
    qs = [r[0] for r in selected]
    o_cmp = [r[1] for r in selected]
    sel_b = [r[2] for r in selected]

    def slc_tile(kt, carry):
        k0 = pl.multiple_of(kt * tk, tk)
        kpos = k0 + lax.broadcasted_iota(jnp.int32, (1, tk), 1)
        expand = expand_ref[:, pl.ds(k0, tk)]
        chosen = [_dot(sel_b[g], expand) for g in groups]
        scores = [_dot_nt(qs[g], ks_ref[0, g, pl.ds(k0, tk), :]) for g in groups]
        out = []
        for g in groups:
            m, l, acc = carry[g]
            bias = jnp.where(kpos <= t_q, jnp.where(chosen[g] > 0.5, 0.0, MASK_VALUE), MASK_VALUE)
            s = scores[g] + _bias_rows(bias)
            m_new = jnp.maximum(m, jnp.max(s, axis=-1, keepdims=True))
            alpha = jnp.exp2(m - m_new)
            p = jnp.exp2(s - m_new)
            l = alpha * l + jnp.sum(p, axis=-1, keepdims=True)
            acc = alpha * acc + _dot(p.astype(BF16), vs_ref[0, g, pl.ds(k0, tk), :])
            out.append((m_new, l, acc))
        return tuple(out)

    n_tiles = (q0 + Q_BLOCK + tk - 1) // tk
    init = tuple((jnp.full((rows, 1), MASK_VALUE, F32), jnp.zeros((rows, 1), F32), jnp.zeros((rows, hd), F32))
                 for _ in groups)
    slc = lax.fori_loop(0, n_tiles, slc_tile, init)

    band = WINDOW + Q_BLOCK
    w0 = pl.multiple_of(jnp.maximum(q0 - WINDOW, 0), Q_BLOCK)
    kpos = w0 + lax.broadcasted_iota(jnp.int32, (1, band), 1)
    win_bias = _bias_rows(jnp.where(kpos <= t_q, jnp.where(kpos > t_q - WINDOW, 0.0, MASK_VALUE), MASK_VALUE))
    sig = _sigmoid(gate_ref[...])

    def head_cols(o):
        return jnp.concatenate([o[h * Q_BLOCK:(h + 1) * Q_BLOCK] for h in range(NSA_HPG)], axis=1)

    def finish(g):
        s = _dot_nt(qs[g], kw_ref[0, g, pl.ds(w0, band), :]) + win_bias
        yield
        p = jnp.exp2(s - jnp.max(s, axis=-1, keepdims=True))
        o_win = _dot(p.astype(BF16), vw_ref[0, g, pl.ds(w0, band), :]) / jnp.sum(p, axis=-1, keepdims=True)
        yield
        gates = _dot(sig, gexp_ref[g], precision=HIGHEST)
        yield
        _, l_slc, acc_slc = slc[g]
        out = (gates[:, 0:w] * head_cols(o_cmp[g]) + gates[:, w:2 * w] * head_cols(acc_slc / l_slc)
               + gates[:, 2 * w:3 * w] * head_cols(o_win))
        o_ref[:, g * w:(g + 1) * w] = out.astype(o_ref.dtype)

    _run_interleaved([finish(g) for g in groups])


def _nsa_constants(seq):
    n_blk = seq // SLC_BLOCK
    n_rows = seq // CMP_STRIDE
    per_stride = SLC_BLOCK // CMP_STRIDE
    span = CMP_BLOCK // CMP_STRIDE
    mmap = np.zeros((n_rows, n_blk), np.float32)
    for j in range(n_blk):
        for m in range(per_stride):
            for n in range(span):
                c = per_stride * j + m + n - (span - 1)
                if 0 <= c < n_rows - 1:
                    mmap[c, j] += 1.0
    w = NSA_HPG * NSA_HEAD_DIM
    gexp = np.zeros((NSA_KV_GROUPS, LANES, 3 * w), np.float32)
    for g in range(NSA_KV_GROUPS):
        for h in range(NSA_HPG):
            for br in range(3):
                gexp[g, (g * NSA_HPG + h) * 3 + br, br * w + h * NSA_HEAD_DIM: br * w + (h + 1) * NSA_HEAD_DIM] = 1.0
    expand = (np.arange(n_blk)[:, None] == (np.arange(seq)[None, :] // SLC_BLOCK)).astype(np.float32)
    return jnp.asarray(mmap), jnp.asarray(gexp), jnp.asarray(expand, BF16)


def nsa_attention(qr, h, gate_col_block, k_cmp, v_cmp, ks, vs, kw, vw, bsz, seq, tk=512):
    tk = min(tk, seq)
    nq = seq // Q_BLOCK
    mmap, gexp, expand = _nsa_constants(seq)
    n_rows = k_cmp.shape[2]
    qblk = lambda b, i: (b * nq + i, 0)
    kv = lambda n: pl.BlockSpec((1, NSA_KV_GROUPS, n, NSA_HEAD_DIM), lambda b, i: (b, 0, 0, 0))
    kern = functools.partial(_nsa_attn_kernel, seq, tk)
    return pl.pallas_call(
        kern,
        grid=(bsz, nq),
        in_specs=[pl.BlockSpec((Q_BLOCK, NSA_WIDTH), qblk),
                  pl.BlockSpec((Q_BLOCK, LANES), lambda b, i: (b * nq + i, gate_col_block)),
                  pl.BlockSpec(gexp.shape, lambda b, i: (0, 0, 0)),
                  kv(n_rows), kv(n_rows), kv(seq), kv(seq), kv(seq), kv(seq),
                  pl.BlockSpec(mmap.shape, lambda b, i: (0, 0)),
                  pl.BlockSpec(expand.shape, lambda b, i: (0, 0))],
        out_specs=pl.BlockSpec((Q_BLOCK, NSA_WIDTH), qblk),
        out_shape=jax.ShapeDtypeStruct((bsz * seq, NSA_WIDTH), BF16),
        compiler_params=_params("parallel", "arbitrary"),
        name="nsa_attention",
    )(qr, h, gexp, k_cmp, v_cmp, ks, vs, kw, vw, mmap, expand)


def nsa_mixer(h, bsz, seq, gate_col_block, pe_k, pe_v, ck_w1, ck_b1, ck_w2, cv_w1, cv_b1, cv_w2):
    qr, kc, vc, ks, vs, kw, vw = nsa_prep(h, bsz, seq)
    k_cmp, v_cmp = nsa_compress(kc, vc, pe_k, pe_v, ck_w1, ck_b1, ck_w2, cv_w1, cv_b1, cv_w2)
    return nsa_attention(qr, h, gate_col_block, k_cmp, v_cmp, ks, vs, kw, vw, bsz, seq)


def _head_ones(width, head_dim):
    idx = np.arange(width) // head_dim
    return jnp.asarray((idx[:, None] == idx[None, :]).astype(np.float32))


def _softplus(x):
    return jnp.maximum(x, 0.0) + jnp.log(1.0 + jnp.exp(-jnp.abs(x)))


def _rwkv_pre_kernel(p_ref, prev_ref, mu_ref, w0_ref, wup_ref, a0_ref, aup_ref, gup_ref, kk_ref, ka_ref, rk_ref,
                     ones_ref, r_o, k_o, v_o, kk_o, b_o, ld_o, g_o, bonus_o):
    w = RWKV_WIDTH
    p = p_ref[...]
    first_row = jnp.where(pl.program_id(1) == 0, 0.0, prev_ref[7:8, :])
    is_row0 = lax.broadcasted_iota(jnp.int32, p.shape, 0) == 0
    prev = jnp.where(is_row0, first_row, pltpu.roll(p, 1, 0))
    ps = p + (prev - p) * mu_ref[...]
    r, k, v = ps[:, 0:w], ps[:, w:2 * w], ps[:, 2 * w:3 * w]
    o = 3 * w
    w_lo = ps[:, o:o + RWKV_LORA_W]
    a_lo = ps[:, o + RWKV_LORA_W:o + RWKV_LORA_W + RWKV_LORA_A]
    g_lo = ps[:, o + RWKV_LORA_W + RWKV_LORA_A:]
    wlog = -_softplus(-(w0_ref[...] + _dot(jnp.tanh(w_lo).astype(BF16), wup_ref[...]))) - 0.5
    a = _sigmoid(a0_ref[...] + _dot(a_lo.astype(BF16), aup_ref[...]))
    g = _dot(_sigmoid(g_lo).astype(BF16), gup_ref[...])
    kk = k * kk_ref[...]
    norm = jnp.sqrt(_dot(kk * kk, ones_ref[...], precision=HIGHEST))
    kk = kk / jnp.maximum(norm, 1e-12)
    k2 = k * (1.0 + (a - 1.0) * ka_ref[...])
    r_o[...] = r
    k_o[...] = k2
    v_o[...] = v
    kk_o[...] = kk
    b_o[...] = kk * a
    ld_o[...] = -jnp.exp(wlog)
    g_o[...] = g
    bonus_o[...] = _dot(r * k2 * rk_ref[...], ones_ref[...], precision=HIGHEST) * v


def rwkv_pre(h, bsz, seq, mu, w0, w_up, a0, a_up, g_up, k_k, k_a, r_k, tl=512):
    tl = min(tl, seq)
    nl = seq // tl
    w = RWKV_WIDTH
    cols = RWKV_COLS
    ones = _head_ones(w, RWKV_HEAD_DIM)
    f2 = lambda b, l: (0, 0)
    vec = pl.BlockSpec((1, w), f2)
    out_spec = pl.BlockSpec((tl, w), lambda b, l: (b * nl + l, 0))
    out_shape = jax.ShapeDtypeStruct((bsz * seq, w), F32)
    return pl.pallas_call(
        _rwkv_pre_kernel,
        grid=(bsz, nl),
        in_specs=[pl.BlockSpec((tl, cols), lambda b, l: (b * nl + l, 0)),
                  pl.BlockSpec((8, cols), lambda b, l: (jnp.maximum((b * seq + l * tl) // 8 - 1, 0), 0)),
                  pl.BlockSpec((1, cols), f2), vec, pl.BlockSpec((RWKV_LORA_W, w), f2),
                  vec, pl.BlockSpec((RWKV_LORA_A, w), f2), pl.BlockSpec((RWKV_LORA_G, w), f2),
                  vec, vec, vec, pl.BlockSpec((w, w), f2)],
        out_specs=[out_spec] * 8,
        out_shape=[out_shape] * 8,
        compiler_params=_params("parallel", "parallel"),
        name="rwkv_pre",
    )(h, h, mu.reshape(1, cols), w0.reshape(1, w), w_up.astype(BF16), a0.reshape(1, w), a_up.astype(BF16),
      g_up.astype(BF16), k_k.reshape(1, w), k_a.reshape(1, w), r_k.reshape(1, w), ones)


def _rwkv_masks():
    t, pk = RWKV_CHUNK, RWKV_PACK
    n = t * pk
    ri = np.arange(n)
    same = (ri[:, None] // t) == (ri[None, :] // t)
    tt, ss = ri[:, None] % t, ri[None, :] % t
    levels = []
    k = 1
    while k < t:
        levels.append(same & (tt // (2 * k) == ss // (2 * k)) & ((tt // k) % 2 == 1) & ((ss // k) % 2 == 0))
        k *= 2
    lvl = np.stack(levels).astype(np.float32)
    tri = (np.arange(t)[:, None] >= np.arange(t)[None, :]).astype(np.float32)
    head_lane = ((ri[:, None] // t) == (np.arange(pk * RWKV_HEAD_DIM)[None, :] // RWKV_HEAD_DIM)).astype(np.float32)
    return (jnp.asarray(tri), jnp.asarray(head_lane), jnp.asarray(same.astype(np.float32)), jnp.asarray(lvl))


def _rwkv_chain(r, k, v, kk, b, ld, st, tri, hl, bd, lvl_ref):
    t, pk = RWKV_CHUNK, RWKV_PACK
    n = t * pk
    c = _dot(tri, ld, precision=HIGHEST)
    yield
    c_end = c[t - 1:t, :]
    e_neg = jnp.exp(-c)
    e_end = jnp.exp(c_end - c)
    kkd = (kk * jnp.exp(c - ld)).astype(BF16)
    rd = (r * jnp.exp(c)).astype(BF16)

    def big(x):
        return (jnp.concatenate([x] * pk, axis=0) * hl).astype(BF16)

    st_b = st.astype(BF16)
    v_big = big(v)
    a_all = _dot_nt(jnp.concatenate([kkd, rd], axis=0),
                    jnp.concatenate([big(k * e_neg), big(b * e_neg)], axis=0))
    yield
    ti = lax.broadcasted_iota(jnp.int32, (t, n), 0)
    si = lax.broadcasted_iota(jnp.int32, (t, n), 1) % t
    strict = ti > si
    incl = ti >= si
    a_kk = jnp.where(strict, a_all[:t, :n], 0.0)
    a_kb = jnp.where(strict, a_all[:t, n:], 0.0)
    a_rk = jnp.where(incl, a_all[t:, :n], 0.0)
    a_rb = jnp.where(incl, a_all[t:, n:], 0.0)
    rhs = _dot(kkd, st_b) + _dot(a_kk.astype(BF16), v_big)
    yield
    a_bd = jnp.concatenate([a_kb] * pk, axis=0) * bd
    m = jnp.where(lax.broadcasted_iota(jnp.int32, (n, n), 0) == lax.broadcasted_iota(jnp.int32, (n, n), 1), 1.0, 0.0)
    for lv in range(lvl_ref.shape[0]):
        mb = m.astype(BF16)
        ma = _dot(mb, (a_bd * lvl_ref[lv]).astype(BF16)).astype(BF16)
        yield
        m = m - _dot(ma, mb)
        yield
    u_big = _dot(m.astype(BF16), big(rhs))
    yield
    u = u_big[0:t]
    for h in range(1, pk):
        u = u + u_big[h * t:(h + 1) * t]
    y = _dot(rd, st_b) + _dot(a_rk.astype(BF16), v_big) - _dot(a_rb.astype(BF16), big(u))
    yield
    decay_col = jnp.broadcast_to(jnp.exp(c_end), st.shape).T
    kb_end = jnp.concatenate([k * e_end, -(b * e_end)], axis=0).astype(BF16)
    vu = jnp.concatenate([v, u], axis=0).astype(BF16)
    return y, decay_col * st + bd * _dot_tn(kb_end, vu)


def _rwkv_chunk_kernel(r_ref, k_ref, v_ref, kk_ref, b_ref, ld_ref, tri_ref, hl_ref, bd_ref, lvl_ref, y_ref, st_ref):
    @pl.when(pl.program_id(0) == 0)
    def _():
        st_ref[...] = jnp.zeros_like(st_ref)

    wp = RWKV_PACK * RWKV_HEAD_DIM
    tri, hl, bd = tri_ref[...], hl_ref[...], bd_ref[...]
    n_pack = r_ref.shape[2] // wp
    where = [(bi, slice(g * wp, (g + 1) * wp)) for bi in range(r_ref.shape[0]) for g in range(n_pack)]
    loaded = [tuple(ref[bi, :, cols] for ref in (r_ref, k_ref, v_ref, kk_ref, b_ref, ld_ref)) + (st_ref[i],)
              for i, (bi, cols) in enumerate(where)]
    results = _run_interleaved([_rwkv_chain(*args, tri, hl, bd, lvl_ref) for args in loaded])
    for i, ((bi, cols), (y, st_new)) in enumerate(zip(where, results)):
        y_ref[bi, :, cols] = y
        st_ref[i] = st_new


def rwkv_chunk(r, k, v, kk, b, ld, bsz, seq):
    t, pk = RWKV_CHUNK, RWKV_PACK
    n_chunk = seq // t
    w = RWKV_WIDTH
    wp = pk * RWKV_HEAD_DIM
    assert t == RWKV_HEAD_DIM
    tri, hl, bd, lvl = _rwkv_masks()
    blk = pl.BlockSpec((bsz, t, w), lambda c: (0, c, 0))
    f2 = lambda c: (0, 0)
    shaped = lambda a: a.reshape(bsz, seq, w)
    y = pl.pallas_call(
        _rwkv_chunk_kernel,
        grid=(n_chunk,),
        in_specs=[blk] * 6 + [pl.BlockSpec(tri.shape, f2), pl.BlockSpec(hl.shape, f2), pl.BlockSpec(bd.shape, f2),
                              pl.BlockSpec(lvl.shape, lambda c: (0, 0, 0))],
        out_specs=blk,
        out_shape=jax.ShapeDtypeStruct((bsz, seq, w), F32),
        scratch_shapes=[pltpu.VMEM((bsz * (w // wp), wp, wp), F32)],
        compiler_params=_params("arbitrary"),
        name="rwkv_chunk",
    )(shaped(r), shaped(k), shaped(v), shaped(kk), shaped(b), shaped(ld), tri, hl, bd, lvl)
    return y.reshape(bsz * seq, w)


def _rwkv_post_kernel(y_ref, bonus_ref, g_ref, lng_ref, lnb_ref, ones_ref, o_ref):
    y = y_ref[...]
    inv = 1.0 / RWKV_HEAD_DIM
    mu = _dot(y, ones_ref[...], precision=HIGHEST) * inv
    yc = y - mu
    var = _dot(yc * yc, ones_ref[...], precision=HIGHEST) * inv
    yn = yc * lax.rsqrt(var + RWKV_GN_EPS) * lng_ref[...] + lnb_ref[...]
    o_ref[...] = ((yn + bonus_ref[...]) * g_ref[...]).astype(o_ref.dtype)


def rwkv_post(y, bonus, g, ln_g, ln_b, tm=1024):
    n, w = y.shape
    tm = min(tm, n)
    row = pl.BlockSpec((tm, w), lambda i: (i, 0))
    vec = pl.BlockSpec((1, w), lambda i: (0, 0))
    return pl.pallas_call(
        _rwkv_post_kernel,
        grid=(n // tm,),
        in_specs=[row, row, row, vec, vec, pl.BlockSpec((w, w), lambda i: (0, 0))],
        out_specs=row,
        out_shape=jax.ShapeDtypeStruct((n, w), BF16),
        compiler_params=_params("parallel"),
        name="rwkv_post",
    )(y, bonus, g, ln_g.reshape(1, w), ln_b.reshape(1, w), _head_ones(w, RWKV_HEAD_DIM))


def rwkv7_mixer(h, bsz, seq, mu, w0, w_up, a0, a_up, g_up, k_k, k_a, r_k, ln_g, ln_b):
    r, k, v, kk, b, ld, g, bonus = rwkv_pre(h, bsz, seq, mu, w0, w_up, a0, a_up, g_up, k_k, k_a, r_k)
    y = rwkv_chunk(r, k, v, kk, b, ld, bsz, seq)
    return rwkv_post(y, bonus, g, ln_g, ln_b)


AB_IN = S5_WIDTH + NSA_WIDTH + 6 * NSA_KV_W + NSA_GATE_COLS
AB_IN_PADDED = -(-AB_IN // LANES) * LANES
NSA_GATE_COL_BLOCK = (AB_IN - NSA_GATE_COLS) // LANES
PROJ_TM = 512


def kernel(x, ab_w_in, ab_w_out, s5_lam_re, s5_lam_im, s5_log_dt, s5_b_re, s5_b_im, s5_c_re, s5_c_im, s5_d, s5_w_glu, s5_b_glu, nsa_pe_k, nsa_pe_v, nsa_ck_w1, nsa_ck_b1, nsa_ck_w2, nsa_cv_w1, nsa_cv_b1, nsa_cv_w2, cd_w_in, cd_w_out, rwkv_mu, rwkv_w0, rwkv_w_up, rwkv_a0, rwkv_a_up, rwkv_g_up, rwkv_k_k, rwkv_k_a, rwkv_r_k, rwkv_ln_g, rwkv_ln_b, ret_ln_g, ret_ln_b, ln1_g, ln1_b, ln2_g, ln2_b, moe_router, moe_bias, moe_w1, moe_w3, moe_w2, sh_w1, sh_w3, sh_w2):
    bsz, seq, d = x.shape
    assert (AB_IN - NSA_GATE_COLS) % LANES == 0
    xf = x.reshape(bsz * seq, d)
    x_in = xf
    for layer in range(DEPTH):
        i = layer // 2
        if layer % 2 == 0:
            w_in = jnp.pad(ab_w_in[i], ((0, 0), (0, AB_IN_PADDED - AB_IN))).astype(BF16)
            h = project(x_in, w_in, PROJ_TM)
            y_1 = s5_mixer(h, bsz, seq, s5_lam_re[i], s5_lam_im[i], s5_log_dt[i], s5_b_re[i], s5_b_im[i],
                           s5_c_re[i], s5_c_im[i], s5_d[i], s5_w_glu[i], s5_b_glu[i])
            y_2 = nsa_mixer(h, bsz, seq, NSA_GATE_COL_BLOCK, nsa_pe_k[i], nsa_pe_v[i], nsa_ck_w1[i], nsa_ck_b1[i],
                            nsa_ck_w2[i], nsa_cv_w1[i], nsa_cv_b1[i], nsa_cv_w2[i])
            w_out = ab_w_out[i]
        else:
            h = project(x_in, cd_w_in[i].astype(BF16), PROJ_TM)
            y_1 = rwkv7_mixer(h, bsz, seq, rwkv_mu[i], rwkv_w0[i], rwkv_w_up[i], rwkv_a0[i], rwkv_a_up[i],
                              rwkv_g_up[i], rwkv_k_k[i], rwkv_k_a[i], rwkv_r_k[i], rwkv_ln_g[i], rwkv_ln_b[i])
            y_2 = retention_mixer(h, bsz, seq, RWKV_COLS, ret_ln_g[i], ret_ln_b[i])
            w_out = cd_w_out[i]
        xf, x_bf = out_proj_ln(y_1, y_2, w_out, xf, ln1_g[layer], ln1_b[layer])
        xf, x_in = moe_block(xf, x_bf, moe_router[layer], moe_bias[layer], moe_w1[layer], moe_w3[layer],
                             moe_w2[layer], sh_w1[layer], sh_w3[layer], sh_w2[layer], ln2_g[layer], ln2_b[layer])
    return xf.reshape(bsz, seq, d)
```

```python
import functools
import math

import jax
import jax.numpy as jnp
import numpy as np
from jax import lax
from jax.experimental import pallas as pl
from jax.experimental.pallas import tpu as pltpu

F32 = jnp.float32
BF16 = jnp.bfloat16
HIGHEST = lax.Precision.HIGHEST

VMEM_LIMIT_BYTES = 52 * 1024 * 1024
LANES = 128

LN_EPS = 1e-5
DEPTH = 2
ALPHA = (2 * DEPTH) ** 0.25

S5_GROUPS, S5_GROUP_CH, S5_STATE = 32, 16, 64
S5_WIDTH = S5_GROUPS * S5_GROUP_CH
S5_CHUNK = 16
S5_PACK = 8
NSA_HEADS, NSA_KV_GROUPS, NSA_HEAD_DIM = 8, 2, 64
NSA_HPG = NSA_HEADS // NSA_KV_GROUPS
NSA_WIDTH = NSA_HEADS * NSA_HEAD_DIM
NSA_ROT_DIM = NSA_HEAD_DIM // 4
ROPE_THETA = 500000.0
CMP_BLOCK, CMP_STRIDE, CMP_HIDDEN = 32, 16, 128
SLC_BLOCK, N_SLC, WINDOW, Q_BLOCK = 64, 16, 512, 128
FORCE_SCORE = 1e6
MASK_VALUE = -1e30
RWKV_HEADS, RWKV_HEAD_DIM = 8, 64
RWKV_WIDTH = RWKV_HEADS * RWKV_HEAD_DIM
RWKV_LORA_W, RWKV_LORA_A, RWKV_LORA_G = 64, 64, 128
RWKV_COLS = 3 * RWKV_WIDTH + RWKV_LORA_W + RWKV_LORA_A + RWKV_LORA_G
RWKV_GN_EPS = 64e-5
RWKV_CHUNK = 64
RWKV_PACK = 4
RET_HEADS, RET_DK, RET_DV, RET_CHUNK = 4, 64, 128, 128
RET_THETA = 10000.0
RET_GN_EPS = 1e-5
N_EXPERTS, TOP_K, EXPERT_FF = 64, 8, 256
N_EXPERT_GROUPS, TOPK_GROUPS = 8, 4
EXPERTS_PER_GROUP = N_EXPERTS // N_EXPERT_GROUPS
ROUTED_SCALE = 2.5


def _params(*sem):
    return pltpu.CompilerParams(dimension_semantics=sem, vmem_limit_bytes=VMEM_LIMIT_BYTES)


def _dot(a, b, **kw):
    return jnp.dot(a, b, preferred_element_type=F32, **kw)


def _dot_nt(a, b, **kw):
    return lax.dot_general(a, b, (((1,), (1,)), ((), ())), preferred_element_type=F32, **kw)


def _dot_tn(a, b, **kw):
    return lax.dot_general(a, b, (((0,), (0,)), ((), ())), preferred_element_type=F32, **kw)


def _dot_split(a, b, split, parts):
    rest = a if split == 'a' else b
    acc = None
    for _ in range(parts):
        piece = rest.astype(BF16)
        term = _dot(piece, b) if split == 'a' else _dot(a, piece)
        acc = term if acc is None else acc + term
        rest = rest - piece.astype(F32)
    return acc


def _run_interleaved(gens):
    results = [None] * len(gens)
    live = list(range(len(gens)))
    while live:
        for i in list(live):
            try:
                next(gens[i])
            except StopIteration as done:
                results[i] = done.value
                live.remove(i)
    return results


def _gelu(x):
    return 0.5 * x * (1.0 + jnp.tanh(math.sqrt(2.0 / math.pi) * (x + 0.044715 * (x * x * x))))


def _sigmoid(x):
    return 1.0 / (1.0 + jnp.exp(-x))


def _layer_norm_rows(z, g, b):
    mu = jnp.mean(z, axis=-1, keepdims=True)
    zc = z - mu
    var = jnp.mean(zc * zc, axis=-1, keepdims=True)
    return zc * lax.rsqrt(var + LN_EPS) * g + b


def _proj_kernel(x_ref, w_ref, o_ref, *chunked_ref):
    y = _dot(x_ref[...].astype(BF16), w_ref[...])
    o_ref[...] = y
    for c_ref in chunked_ref:
        rows, t, w = c_ref.shape
        c_ref[...] = y[:, :w].reshape(rows, t, w)


def project(x, w_bf16, tm, chunked=None):
    m, k = x.shape
    n = w_bf16.shape[1]
    out_specs = [pl.BlockSpec((tm, n), lambda i: (i, 0))]
    out_shape = [jax.ShapeDtypeStruct((m, n), F32)]
    if chunked is not None:
        t, w = chunked
        out_specs.append(pl.BlockSpec((tm // t, t, w), lambda i: (i, 0, 0)))
        out_shape.append(jax.ShapeDtypeStruct((m // t, t, w), F32))
    out = pl.pallas_call(
        _proj_kernel,
        grid=(m // tm,),
        in_specs=[pl.BlockSpec((tm, k), lambda i: (i, 0)), pl.BlockSpec((k, n), lambda i: (0, 0))],
        out_specs=out_specs,
        out_shape=out_shape,
        compiler_params=_params("parallel"),
        name="project",
    )(x, w_bf16)
    return out if chunked is not None else out[0]


def _out_proj_ln_kernel(ya_ref, yb_ref, wa_ref, wb_ref, x_ref, g_ref, b_ref, o_ref, obf_ref):
    mix = _dot(ya_ref[...], wa_ref[...]) + _dot(yb_ref[...], wb_ref[...])
    y = _layer_norm_rows(ALPHA * x_ref[...] + mix, g_ref[...], b_ref[...])
    o_ref[...] = y
    obf_ref[...] = y.astype(BF16)


def out_proj_ln(ya, yb, w_out, x, g, b, tm=512):
    n, d = x.shape
    ka, kb = ya.shape[1], yb.shape[1]
    wa = w_out[:ka].astype(BF16)
    wb = w_out[ka:].astype(BF16)
    row = lambda i: (i, 0)
    fixed = lambda i: (0, 0)
    return pl.pallas_call(
        _out_proj_ln_kernel,
        grid=(n // tm,),
        in_specs=[pl.BlockSpec((tm, ka), row), pl.BlockSpec((tm, kb), row),
                  pl.BlockSpec((ka, d), fixed), pl.BlockSpec((kb, d), fixed),
                  pl.BlockSpec((tm, d), row), pl.BlockSpec((1, d), fixed), pl.BlockSpec((1, d), fixed)],
        out_specs=[pl.BlockSpec((tm, d), row), pl.BlockSpec((tm, d), row)],
        out_shape=[jax.ShapeDtypeStruct((n, d), F32), jax.ShapeDtypeStruct((n, d), BF16)],
        compiler_params=_params("parallel"),
        name="out_proj_ln",
    )(ya, yb, wa, wb, x, g.reshape(1, d), b.reshape(1, d))


def _router_kernel(x_ref, rt_ref, bias_ref, o_ref):
    tr = x_ref.shape[0]
    scores = _sigmoid(_dot_nt(rt_ref[...], x_ref[...], precision=HIGHEST))
    biased = scores + bias_ref[...]
    grp = biased.reshape(N_EXPERT_GROUPS, EXPERTS_PER_GROUP, tr)
    pos = lax.broadcasted_iota(jnp.int32, grp.shape, 1)
    m1 = jnp.max(grp, axis=1, keepdims=True)
    first = jnp.min(jnp.where(grp == m1, pos, EXPERTS_PER_GROUP), axis=1, keepdims=True)
    m2 = jnp.max(jnp.where(pos == first, -jnp.inf, grp), axis=1, keepdims=True)
    gscore = (m1 + m2).reshape(N_EXPERT_GROUPS, tr)
    gidx = lax.broadcasted_iota(jnp.int32, gscore.shape, 0)
    grank = jnp.zeros(gscore.shape, F32)
    for j in range(N_EXPERT_GROUPS):
        row = gscore[j:j + 1, :]
        grank = grank + jnp.where(gidx > j, jnp.where(row >= gscore, 1.0, 0.0), jnp.where(row > gscore, 1.0, 0.0))
    gkeep = jnp.where(grank < TOPK_GROUPS, 1.0, 0.0)
    keep = jnp.broadcast_to(gkeep[:, None, :], grp.shape).reshape(N_EXPERTS, tr)
    masked = jnp.where(keep > 0.5, biased, -jnp.inf)
    eidx = lax.broadcasted_iota(jnp.int32, masked.shape, 0)
    rank = jnp.zeros(masked.shape, F32)
    for j in range(N_EXPERTS):
        row = masked[j:j + 1, :]
        rank = rank + jnp.where(eidx > j, jnp.where(row >= masked, 1.0, 0.0), jnp.where(row > masked, 1.0, 0.0))
    gate = jnp.where(rank < TOP_K, scores, 0.0)
    gate = gate / jnp.sum(gate, axis=0, keepdims=True) * ROUTED_SCALE
    o_ref[...] = jnp.concatenate([gate, jnp.zeros((LANES - N_EXPERTS, tr), F32)], axis=0).T


def moe_router(x, router, bias, tr=512):
    n, d = x.shape
    return pl.pallas_call(
        _router_kernel,
        grid=(n // tr,),
        in_specs=[pl.BlockSpec((tr, d), lambda i: (i, 0)),
                  pl.BlockSpec((N_EXPERTS, d), lambda i: (0, 0)),
                  pl.BlockSpec((N_EXPERTS, 1), lambda i: (0, 0))],
        out_specs=pl.BlockSpec((tr, LANES), lambda i: (i, 0)),
        out_shape=jax.ShapeDtypeStruct((n, LANES), F32),
        compiler_params=_params("parallel"),
        name="moe_router",
    )(x, router.T, bias.reshape(N_EXPERTS, 1))


def _swiglu_hidden(xb, w1, w3):
    h1 = _dot(xb, w1.astype(BF16))
    return h1 * _sigmoid(h1) * _dot(xb, w3.astype(BF16))


def _experts_ln_kernel(xbf_ref, x_ref, gate_ref, w1_ref, w3_ref, w2_ref, sw1_ref, sw3_ref, sw2_ref, g_ref, b_ref,
                       o_ref, obf_ref, acc_ref):
    e = pl.program_id(1)
    xb = xbf_ref[...]

    @pl.when(e == 0)
    def _():
        h = _swiglu_hidden(xb, sw1_ref[...], sw3_ref[...])
        acc_ref[...] = _dot(h.astype(BF16), sw2_ref[...].astype(BF16))

    lane = lax.broadcasted_iota(jnp.int32, gate_ref.shape, 1)
    gcol = jnp.sum(jnp.where(lane == e, gate_ref[...], 0.0), axis=1, keepdims=True)
    h = _swiglu_hidden(xb, w1_ref[0], w3_ref[0]) * gcol
    acc_ref[...] += _dot(h.astype(BF16), w2_ref[0].astype(BF16))

    @pl.when(e == pl.num_programs(1) - 1)
    def _():
        y = _layer_norm_rows(ALPHA * x_ref[...] + acc_ref[...], g_ref[...], b_ref[...])
        o_ref[...] = y
        obf_ref[...] = y.astype(BF16)


def moe_experts_ln(x, x_bf16, gates, w1, w3, w2, sw1, sw3, sw2, g, b, tm=1024):
    n, d = x.shape
    ne = w1.shape[0]
    tok = lambda i, e: (i, 0)
    fixed = lambda i, e: (0, 0)
    return pl.pallas_call(
        _experts_ln_kernel,
        grid=(n // tm, ne),
        in_specs=[pl.BlockSpec((tm, d), tok), pl.BlockSpec((tm, d), tok), pl.BlockSpec((tm, LANES), tok),
                  pl.BlockSpec((1, d, EXPERT_FF), lambda i, e: (e, 0, 0)),
                  pl.BlockSpec((1, d, EXPERT_FF), lambda i, e: (e, 0, 0)),
                  pl.BlockSpec((1, EXPERT_FF, d), lambda i, e: (e, 0, 0)),
                  pl.BlockSpec((d, EXPERT_FF), fixed), pl.BlockSpec((d, EXPERT_FF), fixed),
                  pl.BlockSpec((EXPERT_FF, d), fixed),
                  pl.BlockSpec((1, d), fixed), pl.BlockSpec((1, d), fixed)],
        out_specs=[pl.BlockSpec((tm, d), tok), pl.BlockSpec((tm, d), tok)],
        out_shape=[jax.ShapeDtypeStruct((n, d), F32), jax.ShapeDtypeStruct((n, d), BF16)],
        scratch_shapes=[pltpu.VMEM((tm, d), F32)],
        compiler_params=_params("parallel", "arbitrary"),
        name="moe_experts_ln",
    )(x_bf16, x, gates, w1, w3, w2, sw1, sw3, sw2, g.reshape(1, d), b.reshape(1, d))


def moe_block(x, x_bf16, router, bias, w1, w3, w2, sw1, sw3, sw2, g, b):
    gates = moe_router(x, router, bias)
    return moe_experts_ln(x, x_bf16, gates, w1, w3, w2, sw1, sw3, sw2, g, b)


def _s5_tables(lam_re, lam_im, log_dt, b_re, b_im, c_re, c_im, n_chunk):
    t, h, p = S5_CHUNK, S5_GROUP_CH, S5_STATE
    dt = jnp.exp(log_dt.astype(F32))[:, None]
    den = lam_re ** 2 + lam_im ** 2

    def lam_pow(k):
        k = jnp.asarray(k, F32)[..., None, None]
        mag = jnp.exp(lam_re * dt * k)
        return mag * jnp.cos(lam_im * dt * k), mag * jnp.sin(lam_im * dt * k)

    lb_re, lb_im = lam_pow(1.0)
    f_re = ((lb_re - 1.0) * lam_re + lb_im * lam_im) / den
    f_im = (lb_im * lam_re - (lb_re - 1.0) * lam_im) / den
    bb_re = f_re[..., None] * b_re - f_im[..., None] * b_im
    bb_im = f_re[..., None] * b_im + f_im[..., None] * b_re
    pr, pi = lam_pow(jnp.arange(t))
    cl_re = c_re[None] * pr[:, :, None, :] - c_im[None] * pi[:, :, None, :]
    cl_im = c_re[None] * pi[:, :, None, :] + c_im[None] * pr[:, :, None, :]
    klag = jnp.einsum('tgop,gpi->tgoi', cl_re, bb_re) - jnp.einsum('tgop,gpi->tgoi', cl_im, bb_im)
    i_idx = jnp.arange(t)[:, None]
    j_idx = jnp.arange(t)[None, :]
    lag = i_idx - j_idx
    toe = jnp.where((lag >= 0)[:, :, None, None, None], klag[jnp.clip(lag, 0)], 0.0)
    nb = S5_GROUPS // S5_PACK
    eye = jnp.eye(S5_PACK, dtype=F32)
    lanes = t * S5_PACK * h
    split = lambda a, axis: a.reshape(a.shape[:axis] + (nb, S5_PACK) + a.shape[axis + 1:])

    def expand(a, g_axis, k_axis):
        assert g_axis < k_axis
        a = jnp.expand_dims(a, k_axis)
        shape = [1] * a.ndim
        shape[g_axis] = shape[k_axis] = S5_PACK
        return (a * eye.reshape(shape)).astype(BF16)

    toe_b = jnp.transpose(split(toe, 2), (2, 1, 3, 5, 0, 4))
    w_toe = expand(toe_b, 2, 5).reshape(nb, lanes, lanes)
    qr, qi = lam_pow(t - 1 - jnp.arange(t))
    st_re = qr[..., None] * bb_re[None] - qi[..., None] * bb_im[None]
    st_im = qr[..., None] * bb_im[None] + qi[..., None] * bb_re[None]
    pack_state = lambda a: expand(jnp.transpose(split(a, 1), (1, 0, 2, 4, 3)), 2, 4).reshape(nb, lanes, S5_PACK * p)
    w_state = jnp.concatenate([pack_state(st_re), pack_state(st_im)], axis=-1)
    er, ei = lam_pow(jnp.arange(t) + 1)
    x_re = c_re[None] * er[:, :, None, :] - c_im[None] * ei[:, :, None, :]
    x_im = c_re[None] * ei[:, :, None, :] + c_im[None] * er[:, :, None, :]
    pack_cross = lambda a: expand(jnp.transpose(split(a, 1), (1, 2, 4, 0, 3)), 1, 4).reshape(nb, S5_PACK * p, lanes)
    w_cross = jnp.concatenate([pack_cross(x_re), pack_cross(-x_im)], axis=1)
    levels = max(1, int(math.log2(n_chunk)))
    sr, si = lam_pow(t * (2.0 ** jnp.arange(levels)))
    sr = sr.reshape(levels, nb, S5_PACK * p)
    si = si.reshape(levels, nb, S5_PACK * p)
    a1 = jnp.concatenate([sr, sr], axis=-1)
    a2 = jnp.concatenate([-si, si], axis=-1)
    scan = jnp.transpose(jnp.stack([a1, a2], axis=1), (2, 0, 1, 3))
    return w_toe, w_state, w_cross, scan.astype(F32)


def _s5_kernel(u_ref, wtoe_ref, wstate_ref, wcross_ref, scan_ref, o_ref):
    n_chunk, t, _ = u_ref.shape
    x = jnp.concatenate([u_ref[:, j, :] for j in range(t)], axis=1).astype(BF16)
    local = _dot(x, wtoe_ref[0])
    state = _dot(x, wstate_ref[0])
    row = lax.broadcasted_iota(jnp.int32, state.shape, 0)
    s = jnp.where(row >= 1, pltpu.roll(state, 1, 0), 0.0)
    half = state.shape[1] // 2
    level = 0
    d = 1
    while d < n_chunk:
        mult = scan_ref[0, level]
        prev = jnp.where(row >= d, pltpu.roll(s, d, 0), 0.0)
        s = s + mult[0:1, :] * prev + mult[1:2, :] * pltpu.roll(prev, half, 1)
        d *= 2
        level += 1
    y = local + _dot(s.astype(BF16), wcross_ref[0])
    for i in range(t):
        o_ref[:, i, :] = y[:, i * LANES:(i + 1) * LANES]


def s5_scan(u3, bsz, w_toe, w_state, w_cross, scan):
    rows, t, w = u3.shape
    n_chunk = rows // bsz
    once = pl.Buffered(1)
    table = lambda a: pl.BlockSpec((1,) + a.shape[1:], lambda j, b: (j, 0, 0), pipeline_mode=once)
    return pl.pallas_call(
        _s5_kernel,
        grid=(w // LANES, bsz),
        in_specs=[pl.BlockSpec((n_chunk, t, LANES), lambda j, b: (b, 0, j)),
                  table(w_toe), table(w_state), table(w_cross),
                  pl.BlockSpec((1,) + scan.shape[1:], lambda j, b: (j, 0, 0, 0))],
        out_specs=pl.BlockSpec((n_chunk, t, LANES), lambda j, b: (b, 0, j)),
        out_shape=jax.ShapeDtypeStruct(u3.shape, F32),
        compiler_params=_params("parallel", "parallel"),
        name="s5_scan",
    )(u3, w_toe, w_state, w_cross, scan)


def _s5_post_kernel(y_ref, u_ref, d_ref, w_ref, b_ref, o_ref):
    u = u_ref[...]
    y = _gelu(y_ref[...].reshape(u.shape) + d_ref[...] * u)
    o_ref[...] = (y * _sigmoid(_dot(y.astype(BF16), w_ref[...]) + b_ref[...])).astype(o_ref.dtype)


def s5_post(y3, h, d_skip, w_glu, b_glu, tm=1024):
    rows, t, w = y3.shape
    n = rows * t
    tm = min(tm, n)
    row = lambda i: (i, 0)
    fixed = lambda i: (0, 0)
    return pl.pallas_call(
        _s5_post_kernel,
        grid=(n // tm,),
        in_specs=[pl.BlockSpec((tm // t, t, w), lambda i: (i, 0, 0)), pl.BlockSpec((tm, w), row),
                  pl.BlockSpec((1, w), fixed), pl.BlockSpec((w, w), fixed), pl.BlockSpec((1, w), fixed)],
        out_specs=pl.BlockSpec((tm, w), row),
        out_shape=jax.ShapeDtypeStruct((n, w), BF16),
        compiler_params=_params("parallel"),
        name="s5_post",
    )(y3, h, d_skip.reshape(1, w), w_glu.astype(BF16), b_glu.reshape(1, w))


def s5_mixer(h, u3, bsz, seq, lam_re, lam_im, log_dt, b_re, b_im, c_re, c_im, d_skip, w_glu, b_glu):
    tables = _s5_tables(lam_re, lam_im, log_dt, b_re, b_im, c_re, c_im, seq // S5_CHUNK)
    return s5_post(s5_scan(u3, bsz, *tables), h, d_skip, w_glu, b_glu)


def _rope_tables(pos, rot_dim, theta, head_dim, n_heads):
    half = rot_dim // 2
    f32 = np.float32
    inv_freq = f32(theta) ** (-np.arange(half, dtype=f32) / f32(half))
    ang = (pos.astype(f32)[:, None] * inv_freq[None, :]).astype(np.float64)
    cos, sin = np.cos(ang), np.sin(ang)
    rest = head_dim - rot_dim
    n = pos.shape[0]
    c = np.concatenate([cos, cos, np.ones((n, rest))], axis=1)
    s_up = np.concatenate([-sin, np.zeros((n, half + rest))], axis=1)
    s_dn = np.concatenate([np.zeros((n, half)), sin, np.zeros((n, rest))], axis=1)
    tile = lambda a: jnp.asarray(np.tile(a, (1, n_heads)), F32)
    return tile(c), tile(s_up), tile(s_dn)


def _rope_apply(x, c, s_up, s_dn, half):
    return x * c + pltpu.roll(x, LANES - half, 1) * s_up + pltpu.roll(x, half, 1) * s_dn


def _retention_tables():
    c = RET_CHUNK
    log_gamma = np.log(1.0 - 2.0 ** (-5.0 - np.arange(RET_HEADS, dtype=np.float64)))
    i = np.arange(c, dtype=np.float64)
    diff = i[:, None] - i[None, :]
    decay = np.where(diff >= 0, np.exp(diff[None] * log_gamma[:, None, None]), 0.0)
    qdec = np.repeat(np.exp((i + 1.0)[:, None] * log_gamma[None, :]), RET_DK, axis=1)
    kdec = np.repeat(np.exp((c - 1.0 - i)[:, None] * log_gamma[None, :]), RET_DK, axis=1)
    chunk_decay = [float(v) for v in np.exp(c * log_gamma)]
    return jnp.asarray(decay, F32), jnp.asarray(qdec, F32), jnp.asarray(kdec, F32), chunk_decay


def _retention_kernel(chunk_decay, q_ref, k_ref, v0_ref, v1_ref, g0_ref, g1_ref, c_ref, su_ref, sd_ref,
                      dec_ref, qdec_ref, kdec_ref, lng_ref, lnb_ref, o_ref, state_ref):
    @pl.when(pl.program_id(1) == 0)
    def _():
        state_ref[...] = jnp.zeros_like(state_ref)

    half = RET_DK // 2
    tabs = (c_ref[...], su_ref[...], sd_ref[...])
    q = jnp.concatenate([_rope_apply(q_ref[:, s:s + LANES], *tabs, half) for s in (0, LANES)], axis=1)
    k = jnp.concatenate([_rope_apply(k_ref[:, s:s + LANES], *tabs, half) for s in (0, LANES)], axis=1)
    k = k * (RET_DK ** -0.5)
    q_dec = q * qdec_ref[...]
    k_dec = k * kdec_ref[...]
    v = jnp.concatenate([v0_ref[...], v1_ref[...]], axis=1)
    gate = jnp.concatenate([g0_ref[...], g1_ref[...]], axis=1)
    outs = []
    for h in range(RET_HEADS):
        ks = slice(h * RET_DK, (h + 1) * RET_DK)
        vs = slice(h * RET_DV, (h + 1) * RET_DV)
        vh = v[:, vs].astype(BF16)
        scores = _dot_nt(q[:, ks].astype(BF16), k[:, ks].astype(BF16)) * dec_ref[h]
        y = _dot(scores.astype(BF16), vh) + _dot(q_dec[:, ks].astype(BF16), state_ref[h].astype(BF16))
        state_ref[h] = state_ref[h] * chunk_decay[h] + _dot_tn(k_dec[:, ks].astype(BF16), vh)
        mu = jnp.mean(y, axis=-1, keepdims=True)
        yc = y - mu
        var = jnp.mean(yc * yc, axis=-1, keepdims=True)
        outs.append(yc * lax.rsqrt(var + RET_GN_EPS))
    yn = jnp.concatenate(outs, axis=1) * lng_ref[...] + lnb_ref[...]
    o_ref[...] = (gate * _sigmoid(gate) * yn).astype(o_ref.dtype)


def retention_mixer(h, bsz, seq, col0, ln_g, ln_b):
    c = RET_CHUNK
    n_chunk = seq // c
    qk_w = RET_HEADS * RET_DK
    v_w = RET_HEADS * RET_DV
    assert col0 % qk_w == 0 and qk_w == 2 * LANES and v_w == 2 * qk_w
    cb = col0 // qk_w
    rc, rsu, rsd = _rope_tables(np.arange(seq), RET_DK, RET_THETA, RET_DK, 2)
    dec, qdec, kdec, chunk_decay = _retention_tables()
    row = lambda j: (lambda b, n: (b * n_chunk + n, j))
    pos = lambda b, n: (n, 0)
    fixed2 = lambda b, n: (0, 0)
    kern = functools.partial(_retention_kernel, chunk_decay)
    return pl.pallas_call(
        kern,
        grid=(bsz, n_chunk),
        in_specs=[pl.BlockSpec((c, qk_w), row(cb)), pl.BlockSpec((c, qk_w), row(cb + 1)),
                  pl.BlockSpec((c, qk_w), row(cb + 2)), pl.BlockSpec((c, qk_w), row(cb + 3)),
                  pl.BlockSpec((c, qk_w), row(cb + 4)), pl.BlockSpec((c, qk_w), row(cb + 5)),
                  pl.BlockSpec((c, LANES), pos), pl.BlockSpec((c, LANES), pos), pl.BlockSpec((c, LANES), pos),
                  pl.BlockSpec((RET_HEADS, c, c), lambda b, n: (0, 0, 0)),
                  pl.BlockSpec((c, qk_w), fixed2), pl.BlockSpec((c, qk_w), fixed2),
                  pl.BlockSpec((1, v_w), fixed2), pl.BlockSpec((1, v_w), fixed2)],
        out_specs=pl.BlockSpec((c, v_w), lambda b, n: (b * n_chunk + n, 0)),
        out_shape=jax.ShapeDtypeStruct((bsz * seq, v_w), BF16),
        scratch_shapes=[pltpu.VMEM((RET_HEADS, RET_DK, RET_DV), F32)],
        compiler_params=_params("parallel", "arbitrary"),
        name="retention",
    )(h, h, h, h, h, h, rc, rsu, rsd, dec, qdec, kdec, ln_g.reshape(1, v_w), ln_b.reshape(1, v_w))


NSA_KV_W = NSA_KV_GROUPS * NSA_HEAD_DIM
NSA_GATE_COLS = 3 * NSA_HEADS


def _nsa_prep_kernel(q_ref, kvc_ref, kvs_ref, kvw_ref, c_ref, su_ref, sd_ref,
                     qo_ref, kc_ref, vc_ref, ks_ref, vs_ref, kw_ref, vw_ref):
    half = NSA_ROT_DIM // 2
    tabs = (c_ref[...], su_ref[...], sd_ref[...])
    scale = NSA_HEAD_DIM ** -0.5 * math.log2(math.e)
    qo_ref[...] = jnp.concatenate(
        [_rope_apply(q_ref[:, s:s + LANES], *tabs, half) * scale for s in range(0, NSA_WIDTH, LANES)],
        axis=1).astype(qo_ref.dtype)

    def split(x, o_ref):
        for g in range(NSA_KV_GROUPS):
            o_ref[0, g] = x[:, g * NSA_HEAD_DIM:(g + 1) * NSA_HEAD_DIM].astype(o_ref.dtype)

    split(kvc_ref[:, :NSA_KV_W], kc_ref)
    split(kvc_ref[:, NSA_KV_W:], vc_ref)
    split(_rope_apply(kvs_ref[:, :NSA_KV_W], *tabs, half), ks_ref)
    split(kvs_ref[:, NSA_KV_W:], vs_ref)
    split(_rope_apply(kvw_ref[:, :NSA_KV_W], *tabs, half), kw_ref)
    split(kvw_ref[:, NSA_KV_W:], vw_ref)


def nsa_prep(h, bsz, seq, tl=512):
    tl = min(tl, seq)
    nl = seq // tl
    rc, rsu, rsd = _rope_tables(np.arange(seq), NSA_ROT_DIM, ROPE_THETA, NSA_HEAD_DIM, LANES // NSA_HEAD_DIM)
    row = lambda j: (lambda b, l: (b * nl + l, j))
    pos = lambda b, l: (l, 0)
    kv_out = pl.BlockSpec((1, NSA_KV_GROUPS, tl, NSA_HEAD_DIM), lambda b, l: (b, 0, l, 0))
    kv_shape = lambda dt: jax.ShapeDtypeStruct((bsz, NSA_KV_GROUPS, seq, NSA_HEAD_DIM), dt)
    two = 2 * NSA_KV_W
    return pl.pallas_call(
        _nsa_prep_kernel,
        grid=(bsz, nl),
        in_specs=[pl.BlockSpec((tl, NSA_WIDTH), row(1)),
                  pl.BlockSpec((tl, two), row(4)), pl.BlockSpec((tl, two), row(5)), pl.BlockSpec((tl, two), row(6)),
                  pl.BlockSpec((tl, LANES), pos), pl.BlockSpec((tl, LANES), pos), pl.BlockSpec((tl, LANES), pos)],
        out_specs=[pl.BlockSpec((tl, NSA_WIDTH), lambda b, l: (b * nl + l, 0)),
                   kv_out, kv_out, kv_out, kv_out, kv_out, kv_out],
        out_shape=[jax.ShapeDtypeStruct((bsz * seq, NSA_WIDTH), BF16),
                   kv_shape(F32), kv_shape(F32), kv_shape(BF16), kv_shape(BF16), kv_shape(BF16), kv_shape(BF16)],
        compiler_params=_params("parallel", "parallel"),
        name="nsa_prep",
    )(h, h, h, h, rc, rsu, rsd)


def _nsa_compress_kernel(hk_ref, hv_ref, pek_ref, pev_ref, kw1_ref, kb1_ref, kw2_ref, vw1_ref, vb1_ref, vw2_ref,
                         c_ref, su_ref, sd_ref, ko_ref, vo_ref):
    def mlp(h_ref, pe_ref, w1_ref, b1_ref, w2_ref):
        hb = h_ref[0, 0]
        rows = hb.shape[0]
        first = _dot((hb + pe_ref[0:1, :]).astype(BF16), w1_ref[0])
        second = _dot((hb + pe_ref[1:2, :]).astype(BF16), w1_ref[1])
        hid = _gelu(first + pltpu.roll(second, rows - 1, 0) + b1_ref[...])
        return _dot(hid.astype(BF16), w2_ref[...])

    kc = _rope_apply(mlp(hk_ref, pek_ref, kw1_ref, kb1_ref, kw2_ref), c_ref[...], su_ref[...], sd_ref[...],
                     NSA_ROT_DIM // 2)
    vc = mlp(hv_ref, pev_ref, vw1_ref, vb1_ref, vw2_ref)
    ko_ref[0, 0] = kc[:, :NSA_HEAD_DIM].astype(ko_ref.dtype)
    vo_ref[0, 0] = vc[:, :NSA_HEAD_DIM].astype(vo_ref.dtype)


def nsa_compress(kc, vc, pe_k, pe_v, ck_w1, ck_b1, ck_w2, cv_w1, cv_b1, cv_w2):
    bsz, grp, seq, d = kc.shape
    n_rows = seq // CMP_STRIDE
    flat = CMP_STRIDE * d
    cmp_end = np.arange(n_rows) * CMP_STRIDE + CMP_BLOCK - 1
    rc, rsu, rsd = _rope_tables(cmp_end, NSA_ROT_DIM, ROPE_THETA, NSA_HEAD_DIM, LANES // NSA_HEAD_DIM)
    pad_w2 = lambda w: jnp.pad(w, ((0, 0), (0, LANES - d))).astype(BF16)
    blk = pl.BlockSpec((1, 1, n_rows, flat), lambda b, g: (b, g, 0, 0))
    f2 = lambda b, g: (0, 0)
    f3 = lambda b, g: (0, 0, 0)
    w_specs = [pl.BlockSpec((2, flat, CMP_HIDDEN), f3), pl.BlockSpec((1, CMP_HIDDEN), f2),
               pl.BlockSpec((CMP_HIDDEN, LANES), f2)]
    out_spec = pl.BlockSpec((1, 1, n_rows, d), lambda b, g: (b, g, 0, 0))
    out_shape = jax.ShapeDtypeStruct((bsz, grp, n_rows, d), BF16)
    return pl.pallas_call(
        _nsa_compress_kernel,
        grid=(bsz, grp),
        in_specs=[blk, blk, pl.BlockSpec((2, flat), f2), pl.BlockSpec((2, flat), f2)] + w_specs + w_specs
                 + [pl.BlockSpec((n_rows, LANES), f2)] * 3,
        out_specs=[out_spec, out_spec],
        out_shape=[out_shape, out_shape],
        compiler_params=_params("parallel", "parallel"),
        name="nsa_compress",
    )(kc.reshape(bsz, grp, n_rows, flat), vc.reshape(bsz, grp, n_rows, flat),
      pe_k.reshape(2, flat), pe_v.reshape(2, flat),
      ck_w1.reshape(2, flat, CMP_HIDDEN).astype(BF16), ck_b1.reshape(1, CMP_HIDDEN), pad_w2(ck_w2),
      cv_w1.reshape(2, flat, CMP_HIDDEN).astype(BF16), cv_b1.reshape(1, CMP_HIDDEN), pad_w2(cv_w2),
      rc, rsu, rsd)


def _bias_rows(bias):
    return jnp.concatenate([bias] * NSA_HPG, axis=0)


def _nsa_attn_kernel(seq, tk, q_ref, gate_ref, gexp_ref, kc_ref, vc_ref, ks_ref, vs_ref, kw_ref, vw_ref,
                     mmap_ref, expand_ref, o_ref):
    n_blk = seq // SLC_BLOCK
    n_sel = min(N_SLC, n_blk)
    hd = NSA_HEAD_DIM
    w = NSA_HPG * hd
    groups = range(NSA_KV_GROUPS)
    q0 = pl.program_id(1) * Q_BLOCK
    rows = NSA_HPG * Q_BLOCK
    t_q = q0 + lax.broadcasted_iota(jnp.int32, (Q_BLOCK, 1), 0)
    t_l = q0 + lax.broadcasted_iota(jnp.int32, (1, Q_BLOCK), 1)

    def select(g):
        q = q_ref[:, g * w:(g + 1) * w]
        qs = jnp.concatenate([q[:, h * hd:(h + 1) * hd] for h in range(NSA_HPG)], axis=0)
        kc = kc_ref[0, g]
        n_cmp = kc.shape[0]
        cmp_end = lax.broadcasted_iota(jnp.int32, (1, n_cmp), 1) * CMP_STRIDE + (CMP_BLOCK - 1)
        s = _dot_nt(qs, kc) + _bias_rows(jnp.where(cmp_end <= t_q, 0.0, MASK_VALUE))
        yield
        p = jnp.exp2(s - jnp.max(s, axis=-1, keepdims=True))
        any_key = _bias_rows(jnp.where(t_q >= CMP_BLOCK - 1, 1.0, 0.0))
        inv_l = any_key / jnp.sum(p, axis=-1, keepdims=True)
        o_cmp = _dot(p.astype(BF16), vc_ref[0, g]) * inv_l
        yield
        p = p * inv_l
        imp = p[0:Q_BLOCK]
        for h in range(1, NSA_HPG):
            imp = imp + p[h * Q_BLOCK:(h + 1) * Q_BLOCK]
        imp_t = _dot_split(imp, mmap_ref[...], 'a', 3).T
        yield
        blk = lax.broadcasted_iota(jnp.int32, (n_blk, 1), 0)
        cur = t_l // SLC_BLOCK
        score = jnp.where(blk == 0, FORCE_SCORE,
                          jnp.where(blk == cur, FORCE_SCORE, jnp.where(blk == cur - 1, FORCE_SCORE, imp_t)))
        score = jnp.where(blk * SLC_BLOCK <= t_l, score, -FORCE_SCORE)
        sel_t = jnp.zeros((n_blk, Q_BLOCK), F32)
        for _ in range(n_sel):
            best = jnp.max(score, axis=0, keepdims=True)
            idx = jnp.min(jnp.where(score == best, blk, n_blk), axis=0, keepdims=True)
            pick = blk == idx
            sel_t = jnp.where(pick, 1.0, sel_t)
            score = jnp.where(pick, -jnp.inf, score)
            yield
        return qs, o_cmp, sel_t.T.astype(BF16)

    selected = _run_interleaved([select(g) for g in groups])
    qs = [r[0] for r in selected]
    o_cmp = [r[1] for r in selected]
    sel_b = [r[2] for r in selected]

    def slc_tile(kt, carry):
        k0 = pl.multiple_of(kt * tk, tk)
        kpos = k0 + lax.broadcasted_iota(jnp.int32, (1, tk), 1)
        expand = expand_ref[:, pl.ds(k0, tk)]
        chosen = [_dot(sel_b[g], expand) for g in groups]
        scores = [_dot_nt(qs[g], ks_ref[0, g, pl.ds(k0, tk), :]) for g in groups]
        out = []
        for g in groups:
            m, l, acc = carry[g]
            bias = jnp.where(kpos <= t_q, jnp.where(chosen[g] > 0.5, 0.0, MASK_VALUE), MASK_VALUE)
            s = scores[g] + _bias_rows(bias)
            m_new = jnp.maximum(m, jnp.max(s, axis=-1, keepdims=True))
            alpha = jnp.exp2(m - m_new)
            p = jnp.exp2(s - m_new)
            l = alpha * l + jnp.sum(p, axis=-1, keepdims=True)
            acc = alpha * acc + _dot(p.astype(BF16), vs_ref[0, g, pl.ds(k0, tk), :])
            out.append((m_new, l, acc))
        return tuple(out)

    n_tiles = (q0 + Q_BLOCK + tk - 1) // tk
    init = tuple((jnp.full((rows, 1), MASK_VALUE, F32), jnp.zeros((rows, 1), F32), jnp.zeros((rows, hd), F32))
                 for _ in groups)
    slc = lax.fori_loop(0, n_tiles, slc_tile, init)

    band = WINDOW + Q_BLOCK
    w0 = pl.multiple_of(jnp.maximum(q0 - WINDOW, 0), Q_BLOCK)
    kpos = w0 + lax.broadcasted_iota(jnp.int32, (1, band), 1)
    win_bias = _bias_rows(jnp.where(kpos <= t_q, jnp.where(kpos > t_q - WINDOW, 0.0, MASK_VALUE), MASK_VALUE))
    sig = _sigmoid(gate_ref[...])

    def head_cols(o):
        return jnp.concatenate([o[h * Q_BLOCK:(h + 1) * Q_BLOCK] for h in range(NSA_HPG)], axis=1)

    def finish(g):
        s = _dot_nt(qs[g], kw_ref[0, g, pl.ds(w0, band), :]) + win_bias
        yield
        p = jnp.exp2(s - jnp.max(s, axis=-1, keepdims=True))
        o_win = _dot(p.astype(BF16), vw_ref[0, g, pl.ds(w0, band), :]) / jnp.sum(p, axis=-1, keepdims=True)
        yield
        gates = _dot_split(sig, gexp_ref[g], 'a', 2)
        yield
        _, l_slc, acc_slc = slc[g]
        out = (gates[:, 0:w] * head_cols(o_cmp[g]) + gates[:, w:2 * w] * head_cols(acc_slc / l_slc)
               + gates[:, 2 * w:3 * w] * head_cols(o_win))
        o_ref[:, g * w:(g + 1) * w] = out.astype(o_ref.dtype)

    _run_interleaved([finish(g) for g in groups])


def _nsa_constants(seq):
    n_blk = seq // SLC_BLOCK
    n_rows = seq // CMP_STRIDE
    per_stride = SLC_BLOCK // CMP_STRIDE
    span = CMP_BLOCK // CMP_STRIDE
    mmap = np.zeros((n_rows, n_blk), np.float32)
    for j in range(n_blk):
        for m in range(per_stride):
            for n in range(span):
                c = per_stride * j + m + n - (span - 1)
                if 0 <= c < n_rows - 1:
                    mmap[c, j] += 1.0
    w = NSA_HPG * NSA_HEAD_DIM
    gexp = np.zeros((NSA_KV_GROUPS, LANES, 3 * w), np.float32)
    for g in range(NSA_KV_GROUPS):
        for h in range(NSA_HPG):
            for br in range(3):
                gexp[g, (g * NSA_HPG + h) * 3 + br, br * w + h * NSA_HEAD_DIM: br * w + (h + 1) * NSA_HEAD_DIM] = 1.0
    expand = (np.arange(n_blk)[:, None] == (np.arange(seq)[None, :] // SLC_BLOCK)).astype(np.float32)
    return jnp.asarray(mmap, BF16), jnp.asarray(gexp, BF16), jnp.asarray(expand, BF16)


def nsa_attention(qr, h, gate_col_block, k_cmp, v_cmp, ks, vs, kw, vw, bsz, seq, tk=512):
    tk = min(tk, seq)
    nq = seq // Q_BLOCK
    mmap, gexp, expand = _nsa_constants(seq)
    n_rows = k_cmp.shape[2]
    qblk = lambda b, i: (b * nq + i, 0)
    kv = lambda n: pl.BlockSpec((1, NSA_KV_GROUPS, n, NSA_HEAD_DIM), lambda b, i: (b, 0, 0, 0))
    kern = functools.partial(_nsa_attn_kernel, seq, tk)
    return pl.pallas_call(
        kern,
        grid=(bsz, nq),
        in_specs=[pl.BlockSpec((Q_BLOCK, NSA_WIDTH), qblk),
                  pl.BlockSpec((Q_BLOCK, LANES), lambda b, i: (b * nq + i, gate_col_block)),
                  pl.BlockSpec(gexp.shape, lambda b, i: (0, 0, 0)),
                  kv(n_rows), kv(n_rows), kv(seq), kv(seq), kv(seq), kv(seq),
                  pl.BlockSpec(mmap.shape, lambda b, i: (0, 0)),
                  pl.BlockSpec(expand.shape, lambda b, i: (0, 0))],
        out_specs=pl.BlockSpec((Q_BLOCK, NSA_WIDTH), qblk),
        out_shape=jax.ShapeDtypeStruct((bsz * seq, NSA_WIDTH), BF16),
        compiler_params=_params("parallel", "arbitrary"),
        name="nsa_attention",
    )(qr, h, gexp, k_cmp, v_cmp, ks, vs, kw, vw, mmap, expand)


def nsa_mixer(h, bsz, seq, gate_col_block, pe_k, pe_v, ck_w1, ck_b1, ck_w2, cv_w1, cv_b1, cv_w2):
    qr, kc, vc, ks, vs, kw, vw = nsa_prep(h, bsz, seq)
    k_cmp, v_cmp = nsa_compress(kc, vc, pe_k, pe_v, ck_w1, ck_b1, ck_w2, cv_w1, cv_b1, cv_w2)
    return nsa_attention(qr, h, gate_col_block, k_cmp, v_cmp, ks, vs, kw, vw, bsz, seq)


def _head_ones(width, head_dim):
    idx = np.arange(width) // head_dim
    return jnp.asarray(idx[:, None] == idx[None, :], BF16)


def _softplus(x):
    return jnp.maximum(x, 0.0) + jnp.log(1.0 + jnp.exp(-jnp.abs(x)))


def _rwkv_pre_kernel(p_ref, prev_ref, mu_ref, w0_ref, wup_ref, a0_ref, aup_ref, gup_ref, kk_ref, ka_ref, rk_ref,
                     ones_ref, r_o, k_o, v_o, kk_o, b_o, ld_o, g_o, bonus_o):
    w = RWKV_WIDTH
    p = p_ref[...]
    first_row = jnp.where(pl.program_id(1) == 0, 0.0, prev_ref[7:8, :])
    is_row0 = lax.broadcasted_iota(jnp.int32, p.shape, 0) == 0
    prev = jnp.where(is_row0, first_row, pltpu.roll(p, 1, 0))
    ps = p + (prev - p) * mu_ref[...]
    r, k, v = ps[:, 0:w], ps[:, w:2 * w], ps[:, 2 * w:3 * w]
    o = 3 * w
    w_lo = ps[:, o:o + RWKV_LORA_W]
    a_lo = ps[:, o + RWKV_LORA_W:o + RWKV_LORA_W + RWKV_LORA_A]
    g_lo = ps[:, o + RWKV_LORA_W + RWKV_LORA_A:]
    wlog = -_softplus(-(w0_ref[...] + _dot(jnp.tanh(w_lo).astype(BF16), wup_ref[...]))) - 0.5
    a = _sigmoid(a0_ref[...] + _dot(a_lo.astype(BF16), aup_ref[...]))
    g = _dot(_sigmoid(g_lo).astype(BF16), gup_ref[...])
    kk = k * kk_ref[...]
    norm = jnp.sqrt(_dot_split(kk * kk, ones_ref[...], 'a', 2))
    kk = kk / jnp.maximum(norm, 1e-12)
    k2 = k * (1.0 + (a - 1.0) * ka_ref[...])
    r_o[...] = r
    k_o[...] = k2
    v_o[...] = v
    kk_o[...] = kk
    b_o[...] = kk * a
    ld_o[...] = -jnp.exp(wlog)
    g_o[...] = g
    bonus_o[...] = _dot_split(r * k2 * rk_ref[...], ones_ref[...], 'a', 2) * v


def rwkv_pre(h, bsz, seq, mu, w0, w_up, a0, a_up, g_up, k_k, k_a, r_k, tl=512):
    tl = min(tl, seq)
    nl = seq // tl
    w = RWKV_WIDTH
    cols = RWKV_COLS
    ones = _head_ones(w, RWKV_HEAD_DIM)
    f2 = lambda b, l: (0, 0)
    vec = pl.BlockSpec((1, w), f2)
    out_spec = pl.BlockSpec((tl, w), lambda b, l: (b * nl + l, 0))
    out_shape = jax.ShapeDtypeStruct((bsz * seq, w), F32)
    return pl.pallas_call(
        _rwkv_pre_kernel,
        grid=(bsz, nl),
        in_specs=[pl.BlockSpec((tl, cols), lambda b, l: (b * nl + l, 0)),
                  pl.BlockSpec((8, cols), lambda b, l: (jnp.maximum((b * seq + l * tl) // 8 - 1, 0), 0)),
                  pl.BlockSpec((1, cols), f2), vec, pl.BlockSpec((RWKV_LORA_W, w), f2),
                  vec, pl.BlockSpec((RWKV_LORA_A, w), f2), pl.BlockSpec((RWKV_LORA_G, w), f2),
                  vec, vec, vec, pl.BlockSpec((w, w), f2)],
        out_specs=[out_spec] * 8,
        out_shape=[out_shape] * 8,
        compiler_params=_params("parallel", "parallel"),
        name="rwkv_pre",
    )(h, h, mu.reshape(1, cols), w0.reshape(1, w), w_up.astype(BF16), a0.reshape(1, w), a_up.astype(BF16),
      g_up.astype(BF16), k_k.reshape(1, w), k_a.reshape(1, w), r_k.reshape(1, w), ones)


def _rwkv_masks():
    t, pk = RWKV_CHUNK, RWKV_PACK
    n = t * pk
    ri = np.arange(n)
    same = (ri[:, None] // t) == (ri[None, :] // t)
    tt, ss = ri[:, None] % t, ri[None, :] % t
    levels = []
    k = 1
    while k < t:
        levels.append(same & (tt // (2 * k) == ss // (2 * k)) & ((tt // k) % 2 == 1) & ((ss // k) % 2 == 0))
        k *= 2
    lvl = np.stack(levels).astype(np.float32)
    tri = (np.arange(t)[:, None] >= np.arange(t)[None, :]).astype(np.float32)
    head_lane = ((ri[:, None] // t) == (np.arange(pk * RWKV_HEAD_DIM)[None, :] // RWKV_HEAD_DIM)).astype(np.float32)
    return (jnp.asarray(tri, BF16), jnp.asarray(head_lane), jnp.asarray(same.astype(np.float32)), jnp.asarray(lvl))


def _rwkv_chain(r, k, v, kk, b, ld, st, tri, hl, bd, lvl_ref):
    t, pk = RWKV_CHUNK, RWKV_PACK
    n = t * pk
    c = _dot_split(tri, ld, 'b', 3)
    yield
    c_end = c[t - 1:t, :]
    e_neg = jnp.exp(-c)
    e_end = jnp.exp(c_end - c)
    kkd = (kk * jnp.exp(c - ld)).astype(BF16)
    rd = (r * jnp.exp(c)).astype(BF16)

    def big(x):
        return (jnp.concatenate([x] * pk, axis=0) * hl).astype(BF16)

    st_b = st.astype(BF16)
    v_big = big(v)
    a_all = _dot_nt(jnp.concatenate([kkd, rd], axis=0),
                    jnp.concatenate([big(k * e_neg), big(b * e_neg)], axis=0))
    yield
    ti = lax.broadcasted_iota(jnp.int32, (t, n), 0)
    si = lax.broadcasted_iota(jnp.int32, (t, n), 1) % t
    strict = ti > si
    incl = ti >= si
    a_kk = jnp.where(strict, a_all[:t, :n], 0.0)
    a_kb = jnp.where(strict, a_all[:t, n:], 0.0)
    a_rk = jnp.where(incl, a_all[t:, :n], 0.0)
    a_rb = jnp.where(incl, a_all[t:, n:], 0.0)
    rhs = _dot(kkd, st_b) + _dot(a_kk.astype(BF16), v_big)
    yield
    a_bd = jnp.concatenate([a_kb] * pk, axis=0) * bd
    m = jnp.where(lax.broadcasted_iota(jnp.int32, (n, n), 0) == lax.broadcasted_iota(jnp.int32, (n, n), 1), 1.0, 0.0)
    for lv in range(lvl_ref.shape[0]):
        mb = m.astype(BF16)
        ma = _dot(mb, (a_bd * lvl_ref[lv]).astype(BF16)).astype(BF16)
        yield
        m = m - _dot(ma, mb)
        yield
    u_big = _dot(m.astype(BF16), big(rhs))
    yield
    u = u_big[0:t]
    for h in range(1, pk):
        u = u + u_big[h * t:(h + 1) * t]
    y = _dot(rd, st_b) + _dot(a_rk.astype(BF16), v_big) - _dot(a_rb.astype(BF16), big(u))
    yield
    decay_col = jnp.broadcast_to(jnp.exp(c_end), st.shape).T
    kb_end = jnp.concatenate([k * e_end, -(b * e_end)], axis=0).astype(BF16)
    vu = jnp.concatenate([v, u], axis=0).astype(BF16)
    return y, decay_col * st + bd * _dot_tn(kb_end, vu)


def _rwkv_chunk_kernel(r_ref, k_ref, v_ref, kk_ref, b_ref, ld_ref, tri_ref, hl_ref, bd_ref, lvl_ref, y_ref, st_ref):
    @pl.when(pl.program_id(0) == 0)
    def _():
        st_ref[...] = jnp.zeros_like(st_ref)

    wp = RWKV_PACK * RWKV_HEAD_DIM
    tri, hl, bd = tri_ref[...], hl_ref[...], bd_ref[...]
    n_pack = r_ref.shape[2] // wp
    where = [(bi, slice(g * wp, (g + 1) * wp)) for bi in range(r_ref.shape[0]) for g in range(n_pack)]
    loaded = [tuple(ref[bi, :, cols] for ref in (r_ref, k_ref, v_ref, kk_ref, b_ref, ld_ref)) + (st_ref[i],)
              for i, (bi, cols) in enumerate(where)]
    results = _run_interleaved([_rwkv_chain(*args, tri, hl, bd, lvl_ref) for args in loaded])
    for i, ((bi, cols), (y, st_new)) in enumerate(zip(where, results)):
        y_ref[bi, :, cols] = y
        st_ref[i] = st_new


def rwkv_chunk(r, k, v, kk, b, ld, bsz, seq):
    t, pk = RWKV_CHUNK, RWKV_PACK
    n_chunk = seq // t
    w = RWKV_WIDTH
    wp = pk * RWKV_HEAD_DIM
    assert t == RWKV_HEAD_DIM
    tri, hl, bd, lvl = _rwkv_masks()
    blk = pl.BlockSpec((bsz, t, w), lambda c: (0, c, 0))
    f2 = lambda c: (0, 0)
    shaped = lambda a: a.reshape(bsz, seq, w)
    y = pl.pallas_call(
        _rwkv_chunk_kernel,
        grid=(n_chunk,),
        in_specs=[blk] * 6 + [pl.BlockSpec(tri.shape, f2), pl.BlockSpec(hl.shape, f2), pl.BlockSpec(bd.shape, f2),
                              pl.BlockSpec(lvl.shape, lambda c: (0, 0, 0))],
        out_specs=blk,
        out_shape=jax.ShapeDtypeStruct((bsz, seq, w), F32),
        scratch_shapes=[pltpu.VMEM((bsz * (w // wp), wp, wp), F32)],
        compiler_params=_params("arbitrary"),
        name="rwkv_chunk",
    )(shaped(r), shaped(k), shaped(v), shaped(kk), shaped(b), shaped(ld), tri, hl, bd, lvl)
    return y.reshape(bsz * seq, w)


def _rwkv_post_kernel(y_ref, bonus_ref, g_ref, lng_ref, lnb_ref, ones_ref, o_ref):
    y = y_ref[...]
    inv = 1.0 / RWKV_HEAD_DIM
    mu = _dot_split(y, ones_ref[...], 'a', 2) * inv
    yc = y - mu
    var = _dot_split(yc * yc, ones_ref[...], 'a', 2) * inv
    yn = yc * lax.rsqrt(var + RWKV_GN_EPS) * lng_ref[...] + lnb_ref[...]
    o_ref[...] = ((yn + bonus_ref[...]) * g_ref[...]).astype(o_ref.dtype)


def rwkv_post(y, bonus, g, ln_g, ln_b, tm=1024):
    n, w = y.shape
    tm = min(tm, n)
    row = pl.BlockSpec((tm, w), lambda i: (i, 0))
    vec = pl.BlockSpec((1, w), lambda i: (0, 0))
    return pl.pallas_call(
        _rwkv_post_kernel,
        grid=(n // tm,),
        in_specs=[row, row, row, vec, vec, pl.BlockSpec((w, w), lambda i: (0, 0))],
        out_specs=row,
        out_shape=jax.ShapeDtypeStruct((n, w), BF16),
        compiler_params=_params("parallel"),
        name="rwkv_post",
    )(y, bonus, g, ln_g.reshape(1, w), ln_b.reshape(1, w), _head_ones(w, RWKV_HEAD_DIM))


def rwkv7_mixer(h, bsz, seq, mu, w0, w_up, a0, a_up, g_up, k_k, k_a, r_k, ln_g, ln_b):
    r, k, v, kk, b, ld, g, bonus = rwkv_pre(h, bsz, seq, mu, w0, w_up, a0, a_up, g_up, k_k, k_a, r_k)
    y = rwkv_chunk(r, k, v, kk, b, ld, bsz, seq)
    return rwkv_post(y, bonus, g, ln_g, ln_b)


AB_IN = S5_WIDTH + NSA_WIDTH + 6 * NSA_KV_W + NSA_GATE_COLS
AB_IN_PADDED = -(-AB_IN // LANES) * LANES
NSA_GATE_COL_BLOCK = (AB_IN - NSA_GATE_COLS) // LANES
PROJ_TM = 512


def kernel(x, ab_w_in, ab_w_out, s5_lam_re, s5_lam_im, s5_log_dt, s5_b_re, s5_b_im, s5_c_re, s5_c_im, s5_d, s5_w_glu, s5_b_glu, nsa_pe_k, nsa_pe_v, nsa_ck_w1, nsa_ck_b1, nsa_ck_w2, nsa_cv_w1, nsa_cv_b1, nsa_cv_w2, cd_w_in, cd_w_out, rwkv_mu, rwkv_w0, rwkv_w_up, rwkv_a0, rwkv_a_up, rwkv_g_up, rwkv_k_k, rwkv_k_a, rwkv_r_k, rwkv_ln_g, rwkv_ln_b, ret_ln_g, ret_ln_b, ln1_g, ln1_b, ln2_g, ln2_b, moe_router, moe_bias, moe_w1, moe_w3, moe_w2, sh_w1, sh_w3, sh_w2):
    bsz, seq, d = x.shape
    assert (AB_IN - NSA_GATE_COLS) % LANES == 0
    xf = x.reshape(bsz * seq, d)
    x_in = xf
    for layer in range(DEPTH):
        i = layer // 2
        if layer % 2 == 0:
            w_in = jnp.pad(ab_w_in[i], ((0, 0), (0, AB_IN_PADDED - AB_IN))).astype(BF16)
            h, u3 = project(x_in, w_in, PROJ_TM, chunked=(S5_CHUNK, S5_WIDTH))
            y_1 = s5_mixer(h, u3, bsz, seq, s5_lam_re[i], s5_lam_im[i], s5_log_dt[i], s5_b_re[i], s5_b_im[i],
                           s5_c_re[i], s5_c_im[i], s5_d[i], s5_w_glu[i], s5_b_glu[i])
            y_2 = nsa_mixer(h, bsz, seq, NSA_GATE_COL_BLOCK, nsa_pe_k[i], nsa_pe_v[i], nsa_ck_w1[i], nsa_ck_b1[i],
                            nsa_ck_w2[i], nsa_cv_w1[i], nsa_cv_b1[i], nsa_cv_w2[i])
            w_out = ab_w_out[i]
        else:
            h = project(x_in, cd_w_in[i].astype(BF16), PROJ_TM)
            y_1 = rwkv7_mixer(h, bsz, seq, rwkv_mu[i], rwkv_w0[i], rwkv_w_up[i], rwkv_a0[i], rwkv_a_up[i],
                              rwkv_g_up[i], rwkv_k_k[i], rwkv_k_a[i], rwkv_r_k[i], rwkv_ln_g[i], rwkv_ln_b[i])
            y_2 = retention_mixer(h, bsz, seq, RWKV_COLS, ret_ln_g[i], ret_ln_b[i])
            w_out = cd_w_out[i]
        xf, x_bf = out_proj_ln(y_1, y_2, w_out, xf, ln1_g[layer], ln1_b[layer])
        xf, x_in = moe_block(xf, x_bf, moe_router[layer], moe_bias[layer], moe_w1[layer], moe_w3[layer],
                             moe_w2[layer], sh_w1[layer], sh_w3[layer], sh_w2[layer], ln2_g[layer], ln2_b[layer])
    return xf.reshape(bsz, seq, d)
```

```python
import functools
import math

import jax
import jax.numpy as jnp
import numpy as np
from jax import lax
from jax.experimental import pallas as pl
from jax.experimental.pallas import tpu as pltpu

F32 = jnp.float32
BF16 = jnp.bfloat16
HIGHEST = lax.Precision.HIGHEST
FP8 = jnp.float8_e4m3fn
FP8_MAX = 448.0
FP8_TINY = 1e-30

VMEM_LIMIT_BYTES = 52 * 1024 * 1024
LANES = 128

LN_EPS = 1e-5
DEPTH = 2
ALPHA = (2 * DEPTH) ** 0.25

S5_GROUPS, S5_GROUP_CH, S5_STATE = 32, 16, 64
S5_WIDTH = S5_GROUPS * S5_GROUP_CH
S5_CHUNK = 16
S5_PACK = 8
NSA_HEADS, NSA_KV_GROUPS, NSA_HEAD_DIM = 8, 2, 64
NSA_HPG = NSA_HEADS // NSA_KV_GROUPS
NSA_WIDTH = NSA_HEADS * NSA_HEAD_DIM
NSA_ROT_DIM = NSA_HEAD_DIM // 4
ROPE_THETA = 500000.0
CMP_BLOCK, CMP_STRIDE, CMP_HIDDEN = 32, 16, 128
SLC_BLOCK, N_SLC, WINDOW, Q_BLOCK = 64, 16, 512, 128
FORCE_SCORE = 1e6
MASK_VALUE = -1e30
RWKV_HEADS, RWKV_HEAD_DIM = 8, 64
RWKV_WIDTH = RWKV_HEADS * RWKV_HEAD_DIM
RWKV_LORA_W, RWKV_LORA_A, RWKV_LORA_G = 64, 64, 128
RWKV_COLS = 3 * RWKV_WIDTH + RWKV_LORA_W + RWKV_LORA_A + RWKV_LORA_G
RWKV_GN_EPS = 64e-5
RWKV_CHUNK = 64
RWKV_PACK = 4
RET_HEADS, RET_DK, RET_DV, RET_CHUNK = 4, 64, 128, 128
RET_THETA = 10000.0
RET_GN_EPS = 1e-5
N_EXPERTS, TOP_K, EXPERT_FF = 64, 8, 256
N_EXPERT_GROUPS, TOPK_GROUPS = 8, 4
EXPERTS_PER_GROUP = N_EXPERTS // N_EXPERT_GROUPS
ROUTED_SCALE = 2.5


def _params(*sem):
    return pltpu.CompilerParams(dimension_semantics=sem, vmem_limit_bytes=VMEM_LIMIT_BYTES)


def _dot(a, b, **kw):
    return jnp.dot(a, b, preferred_element_type=F32, **kw)


def _dot_nt(a, b, **kw):
    return lax.dot_general(a, b, (((1,), (1,)), ((), ())), preferred_element_type=F32, **kw)


def _dot_tn(a, b, **kw):
    return lax.dot_general(a, b, (((0,), (0,)), ((), ())), preferred_element_type=F32, **kw)


def _dot_split(a, b, split, parts):
    rest = a if split == 'a' else b
    acc = None
    for _ in range(parts):
        piece = rest.astype(BF16)
        term = _dot(piece, b) if split == 'a' else _dot(a, piece)
        acc = term if acc is None else acc + term
        rest = rest - piece.astype(F32)
    return acc


def _run_interleaved(gens):
    results = [None] * len(gens)
    live = list(range(len(gens)))
    while live:
        for i in list(live):
            try:
                next(gens[i])
            except StopIteration as done:
                results[i] = done.value
                live.remove(i)
    return results


def _gelu(x):
    return 0.5 * x * (1.0 + jnp.tanh(math.sqrt(2.0 / math.pi) * (x + 0.044715 * (x * x * x))))


def _sigmoid(x):
    return 1.0 / (1.0 + jnp.exp(-x))


def _layer_norm_rows(z, g, b):
    mu = jnp.mean(z, axis=-1, keepdims=True)
    zc = z - mu
    var = jnp.mean(zc * zc, axis=-1, keepdims=True)
    return zc * lax.rsqrt(var + LN_EPS) * g + b


def _proj_kernel(x_ref, w_ref, o_ref, *chunked_ref):
    y = _dot(x_ref[...].astype(BF16), w_ref[...])
    o_ref[...] = y
    for c_ref in chunked_ref:
        rows, t, w = c_ref.shape
        c_ref[...] = y[:, :w].reshape(rows, t, w)


def project(x, w_bf16, tm, chunked=None):
    m, k = x.shape
    n = w_bf16.shape[1]
    out_specs = [pl.BlockSpec((tm, n), lambda i: (i, 0))]
    out_shape = [jax.ShapeDtypeStruct((m, n), F32)]
    if chunked is not None:
        t, w = chunked
        out_specs.append(pl.BlockSpec((tm // t, t, w), lambda i: (i, 0, 0)))
        out_shape.append(jax.ShapeDtypeStruct((m // t, t, w), F32))
    out = pl.pallas_call(
        _proj_kernel,
        grid=(m // tm,),
        in_specs=[pl.BlockSpec((tm, k), lambda i: (i, 0)), pl.BlockSpec((k, n), lambda i: (0, 0))],
        out_specs=out_specs,
        out_shape=out_shape,
        compiler_params=_params("parallel"),
        name="project",
    )(x, w_bf16)
    return out if chunked is not None else out[0]


def _out_proj_ln_kernel(ya_ref, yb_ref, wa_ref, wb_ref, x_ref, g_ref, b_ref, o_ref):
    mix = _dot(ya_ref[...], wa_ref[...]) + _dot(yb_ref[...], wb_ref[...])
    o_ref[...] = _layer_norm_rows(ALPHA * x_ref[...] + mix, g_ref[...], b_ref[...])


def out_proj_ln(ya, yb, w_out, x, g, b, tm=512):
    n, d = x.shape
    ka, kb = ya.shape[1], yb.shape[1]
    wa = w_out[:ka].astype(BF16)
    wb = w_out[ka:].astype(BF16)
    row = lambda i: (i, 0)
    fixed = lambda i: (0, 0)
    return pl.pallas_call(
        _out_proj_ln_kernel,
        grid=(n // tm,),
        in_specs=[pl.BlockSpec((tm, ka), row), pl.BlockSpec((tm, kb), row),
                  pl.BlockSpec((ka, d), fixed), pl.BlockSpec((kb, d), fixed),
                  pl.BlockSpec((tm, d), row), pl.BlockSpec((1, d), fixed), pl.BlockSpec((1, d), fixed)],
        out_specs=pl.BlockSpec((tm, d), row),
        out_shape=jax.ShapeDtypeStruct((n, d), F32),
        compiler_params=_params("parallel"),
        name="out_proj_ln",
    )(ya, yb, wa, wb, x, g.reshape(1, d), b.reshape(1, d))


def _router_kernel(x_ref, rt_ref, bias_ref, o_ref):
    tr = x_ref.shape[0]
    scores = _sigmoid(_dot_nt(rt_ref[...], x_ref[...], precision=HIGHEST))
    biased = scores + bias_ref[...]
    grp = biased.reshape(N_EXPERT_GROUPS, EXPERTS_PER_GROUP, tr)
    pos = lax.broadcasted_iota(jnp.int32, grp.shape, 1)
    m1 = jnp.max(grp, axis=1, keepdims=True)
    first = jnp.min(jnp.where(grp == m1, pos, EXPERTS_PER_GROUP), axis=1, keepdims=True)
    m2 = jnp.max(jnp.where(pos == first, -jnp.inf, grp), axis=1, keepdims=True)
    gscore = (m1 + m2).reshape(N_EXPERT_GROUPS, tr)
    gidx = lax.broadcasted_iota(jnp.int32, gscore.shape, 0)
    grank = jnp.zeros(gscore.shape, F32)
    for j in range(N_EXPERT_GROUPS):
        row = gscore[j:j + 1, :]
        grank = grank + jnp.where(gidx > j, jnp.where(row >= gscore, 1.0, 0.0), jnp.where(row > gscore, 1.0, 0.0))
    gkeep = jnp.where(grank < TOPK_GROUPS, 1.0, 0.0)
    keep = jnp.broadcast_to(gkeep[:, None, :], grp.shape).reshape(N_EXPERTS, tr)
    masked = jnp.where(keep > 0.5, biased, -jnp.inf)
    eidx = lax.broadcasted_iota(jnp.int32, masked.shape, 0)
    rank = jnp.zeros(masked.shape, F32)
    for j in range(N_EXPERTS):
        row = masked[j:j + 1, :]
        rank = rank + jnp.where(eidx > j, jnp.where(row >= masked, 1.0, 0.0), jnp.where(row > masked, 1.0, 0.0))
    gate = jnp.where(rank < TOP_K, scores, 0.0)
    gate = gate / jnp.sum(gate, axis=0, keepdims=True) * ROUTED_SCALE
    o_ref[...] = jnp.concatenate([gate, jnp.zeros((LANES - N_EXPERTS, tr), F32)], axis=0).T


def moe_router(x, router, bias, tr=512):
    n, d = x.shape
    return pl.pallas_call(
        _router_kernel,
        grid=(n // tr,),
        in_specs=[pl.BlockSpec((tr, d), lambda i: (i, 0)),
                  pl.BlockSpec((N_EXPERTS, d), lambda i: (0, 0)),
                  pl.BlockSpec((N_EXPERTS, 1), lambda i: (0, 0))],
        out_specs=pl.BlockSpec((tr, LANES), lambda i: (i, 0)),
        out_shape=jax.ShapeDtypeStruct((n, LANES), F32),
        compiler_params=_params("parallel"),
        name="moe_router",
    )(x, router.T, bias.reshape(N_EXPERTS, 1))


def _quantize_fp8(a, axes):
    amax = jnp.max(jnp.abs(a), axis=axes, keepdims=True)
    scale = jnp.maximum(amax, FP8_TINY) * (1.0 / FP8_MAX)
    return (a * (1.0 / scale)).astype(FP8), scale


def _swiglu_hidden(xq, x_scale, w1q, w3q, w_scale, gate=None):
    col1 = x_scale * w_scale[0:1, 0:1]
    col3 = x_scale * w_scale[1:2, 0:1]
    if gate is not None:
        col3 = col3 * gate
    h1 = _dot(xq, w1q) * col1
    return h1 * _sigmoid(h1) * (_dot(xq, w3q) * col3)


def _experts_ln_kernel(x_ref, gate_ref, w1_ref, w3_ref, ws_ref, w2_ref, sw1_ref, sw3_ref, sws_ref, sw2_ref,
                       g_ref, b_ref, o_ref, obf_ref, acc_ref, xq_ref, xs_ref):
    e = pl.program_id(1)

    @pl.when(e == 0)
    def _():
        xq, xs = _quantize_fp8(x_ref[...], (1,))
        xq_ref[...] = xq
        xs_ref[...] = xs
        h = _swiglu_hidden(xq, xs, sw1_ref[...], sw3_ref[...], sws_ref[...])
        acc_ref[...] = _dot(h.astype(BF16), sw2_ref[...].astype(BF16))

    lane = lax.broadcasted_iota(jnp.int32, gate_ref.shape, 1)
    gcol = jnp.sum(jnp.where(lane == e, gate_ref[...], 0.0), axis=1, keepdims=True)
    h = _swiglu_hidden(xq_ref[...], xs_ref[...], w1_ref[0], w3_ref[0], ws_ref[0], gcol)
    acc_ref[...] += _dot(h.astype(BF16), w2_ref[0].astype(BF16))

    @pl.when(e == pl.num_programs(1) - 1)
    def _():
        y = _layer_norm_rows(ALPHA * x_ref[...] + acc_ref[...], g_ref[...], b_ref[...])
        o_ref[...] = y
        obf_ref[...] = y.astype(BF16)


def _quantize_expert_weights(w1, w3):
    w1q, s1 = _quantize_fp8(w1, (-2, -1))
    w3q, s3 = _quantize_fp8(w3, (-2, -1))
    scales = jnp.broadcast_to(jnp.concatenate([s1, s3], axis=-2), s1.shape[:-2] + (2, w1.shape[-1]))
    return w1q, w3q, scales


def moe_experts_ln(x, gates, w1, w3, w2, sw1, sw3, sw2, g, b, tm=1024):
    n, d = x.shape
    ne = w1.shape[0]
    w1q, w3q, ws = _quantize_expert_weights(w1, w3)
    sw1q, sw3q, sws = _quantize_expert_weights(sw1, sw3)
    tok = lambda i, e: (i, 0)
    fixed = lambda i, e: (0, 0)
    per_expert = lambda *blk: pl.BlockSpec((1,) + blk, lambda i, e: (e, 0, 0))
    return pl.pallas_call(
        _experts_ln_kernel,
        grid=(n // tm, ne),
        in_specs=[pl.BlockSpec((tm, d), tok), pl.BlockSpec((tm, LANES), tok),
                  per_expert(d, EXPERT_FF), per_expert(d, EXPERT_FF), per_expert(2, EXPERT_FF),
                  per_expert(EXPERT_FF, d),
                  pl.BlockSpec((d, EXPERT_FF), fixed), pl.BlockSpec((d, EXPERT_FF), fixed),
                  pl.BlockSpec((2, EXPERT_FF), fixed), pl.BlockSpec((EXPERT_FF, d), fixed),
                  pl.BlockSpec((1, d), fixed), pl.BlockSpec((1, d), fixed)],
        out_specs=[pl.BlockSpec((tm, d), tok), pl.BlockSpec((tm, d), tok)],
        out_shape=[jax.ShapeDtypeStruct((n, d), F32), jax.ShapeDtypeStruct((n, d), BF16)],
        scratch_shapes=[pltpu.VMEM((tm, d), F32), pltpu.VMEM((tm, d), FP8), pltpu.VMEM((tm, 1), F32)],
        compiler_params=_params("parallel", "arbitrary"),
        name="moe_experts_ln",
    )(x, gates, w1q, w3q, ws, w2, sw1q, sw3q, sws, sw2, g.reshape(1, d), b.reshape(1, d))


def moe_block(x, router, bias, w1, w3, w2, sw1, sw3, sw2, g, b):
    gates = moe_router(x, router, bias)
    return moe_experts_ln(x, gates, w1, w3, w2, sw1, sw3, sw2, g, b)


def _s5_tables(lam_re, lam_im, log_dt, b_re, b_im, c_re, c_im, n_chunk):
    t, h, p = S5_CHUNK, S5_GROUP_CH, S5_STATE
    dt = jnp.exp(log_dt.astype(F32))[:, None]
    den = lam_re ** 2 + lam_im ** 2

    def lam_pow(k):
        k = jnp.asarray(k, F32)[..., None, None]
        mag = jnp.exp(lam_re * dt * k)
        return mag * jnp.cos(lam_im * dt * k), mag * jnp.sin(lam_im * dt * k)

    lb_re, lb_im = lam_pow(1.0)
    f_re = ((lb_re - 1.0) * lam_re + lb_im * lam_im) / den
    f_im = (lb_im * lam_re - (lb_re - 1.0) * lam_im) / den
    bb_re = f_re[..., None] * b_re - f_im[..., None] * b_im
    bb_im = f_re[..., None] * b_im + f_im[..., None] * b_re
    pr, pi = lam_pow(jnp.arange(t))
    cl_re = c_re[None] * pr[:, :, None, :] - c_im[None] * pi[:, :, None, :]
    cl_im = c_re[None] * pi[:, :, None, :] + c_im[None] * pr[:, :, None, :]
    klag = jnp.einsum('tgop,gpi->tgoi', cl_re, bb_re) - jnp.einsum('tgop,gpi->tgoi', cl_im, bb_im)
    i_idx = jnp.arange(t)[:, None]
    j_idx = jnp.arange(t)[None, :]
    lag = i_idx - j_idx
    toe = jnp.where((lag >= 0)[:, :, None, None, None], klag[jnp.clip(lag, 0)], 0.0)
    nb = S5_GROUPS // S5_PACK
    eye = jnp.eye(S5_PACK, dtype=F32)
    lanes = t * S5_PACK * h
    split = lambda a, axis: a.reshape(a.shape[:axis] + (nb, S5_PACK) + a.shape[axis + 1:])

    def expand(a, g_axis, k_axis):
        assert g_axis < k_axis
        a = jnp.expand_dims(a, k_axis)
        shape = [1] * a.ndim
        shape[g_axis] = shape[k_axis] = S5_PACK
        return (a * eye.reshape(shape)).astype(BF16)

    toe_b = jnp.transpose(split(toe, 2), (2, 1, 3, 5, 0, 4))
    w_toe = expand(toe_b, 2, 5).reshape(nb, lanes, lanes)
    qr, qi = lam_pow(t - 1 - jnp.arange(t))
    st_re = qr[..., None] * bb_re[None] - qi[..., None] * bb_im[None]
    st_im = qr[..., None] * bb_im[None] + qi[..., None] * bb_re[None]
    pack_state = lambda a: expand(jnp.transpose(split(a, 1), (1, 0, 2, 4, 3)), 2, 4).reshape(nb, lanes, S5_PACK * p)
    w_state = jnp.concatenate([pack_state(st_re), pack_state(st_im)], axis=-1)
    er, ei = lam_pow(jnp.arange(t) + 1)
    x_re = c_re[None] * er[:, :, None, :] - c_im[None] * ei[:, :, None, :]
    x_im = c_re[None] * ei[:, :, None, :] + c_im[None] * er[:, :, None, :]
    pack_cross = lambda a: expand(jnp.transpose(split(a, 1), (1, 2, 4, 0, 3)), 1, 4).reshape(nb, S5_PACK * p, lanes)
    w_cross = jnp.concatenate([pack_cross(x_re), pack_cross(-x_im)], axis=1)
    levels = max(1, int(math.log2(n_chunk)))
    sr, si = lam_pow(t * (2.0 ** jnp.arange(levels)))
    sr = sr.reshape(levels, nb, S5_PACK * p)
    si = si.reshape(levels, nb, S5_PACK * p)
    a1 = jnp.concatenate([sr, sr], axis=-1)
    a2 = jnp.concatenate([-si, si], axis=-1)
    scan = jnp.transpose(jnp.stack([a1, a2], axis=1), (2, 0, 1, 3))
    return w_toe, w_state, w_cross, scan.astype(F32)


def _s5_kernel(u_ref, wtoe_ref, wstate_ref, wcross_ref, scan_ref, o_ref):
    n_chunk, t, _ = u_ref.shape
    x = jnp.concatenate([u_ref[:, j, :] for j in range(t)], axis=1).astype(BF16)
    local = _dot(x, wtoe_ref[0])
    state = _dot(x, wstate_ref[0])
    row = lax.broadcasted_iota(jnp.int32, state.shape, 0)
    s = jnp.where(row >= 1, pltpu.roll(state, 1, 0), 0.0)
    half = state.shape[1] // 2
    level = 0
    d = 1
    while d < n_chunk:
        mult = scan_ref[0, level]
        prev = jnp.where(row >= d, pltpu.roll(s, d, 0), 0.0)
        s = s + mult[0:1, :] * prev + mult[1:2, :] * pltpu.roll(prev, half, 1)
        d *= 2
        level += 1
    y = local + _dot(s.astype(BF16), wcross_ref[0])
    for i in range(t):
        o_ref[:, i, :] = y[:, i * LANES:(i + 1) * LANES]


def s5_scan(u3, bsz, w_toe, w_state, w_cross, scan):
    rows, t, w = u3.shape
    n_chunk = rows // bsz
    once = pl.Buffered(1)
    table = lambda a: pl.BlockSpec((1,) + a.shape[1:], lambda j, b: (j, 0, 0), pipeline_mode=once)
    return pl.pallas_call(
        _s5_kernel,
        grid=(w // LANES, bsz),
        in_specs=[pl.BlockSpec((n_chunk, t, LANES), lambda j, b: (b, 0, j)),
                  table(w_toe), table(w_state), table(w_cross),
                  pl.BlockSpec((1,) + scan.shape[1:], lambda j, b: (j, 0, 0, 0))],
        out_specs=pl.BlockSpec((n_chunk, t, LANES), lambda j, b: (b, 0, j)),
        out_shape=jax.ShapeDtypeStruct(u3.shape, F32),
        compiler_params=_params("parallel", "parallel"),
        name="s5_scan",
    )(u3, w_toe, w_state, w_cross, scan)


def _s5_post_kernel(y_ref, u_ref, d_ref, w_ref, b_ref, o_ref):
    u = u_ref[...]
    y = _gelu(y_ref[...].reshape(u.shape) + d_ref[...] * u)
    o_ref[...] = (y * _sigmoid(_dot(y.astype(BF16), w_ref[...]) + b_ref[...])).astype(o_ref.dtype)


def s5_post(y3, h, d_skip, w_glu, b_glu, tm=1024):
    rows, t, w = y3.shape
    n = rows * t
    tm = min(tm, n)
    row = lambda i: (i, 0)
    fixed = lambda i: (0, 0)
    return pl.pallas_call(
        _s5_post_kernel,
        grid=(n // tm,),
        in_specs=[pl.BlockSpec((tm // t, t, w), lambda i: (i, 0, 0)), pl.BlockSpec((tm, w), row),
                  pl.BlockSpec((1, w), fixed), pl.BlockSpec((w, w), fixed), pl.BlockSpec((1, w), fixed)],
        out_specs=pl.BlockSpec((tm, w), row),
        out_shape=jax.ShapeDtypeStruct((n, w), BF16),
        compiler_params=_params("parallel"),
        name="s5_post",
    )(y3, h, d_skip.reshape(1, w), w_glu.astype(BF16), b_glu.reshape(1, w))


def s5_mixer(h, u3, bsz, seq, lam_re, lam_im, log_dt, b_re, b_im, c_re, c_im, d_skip, w_glu, b_glu):
    tables = _s5_tables(lam_re, lam_im, log_dt, b_re, b_im, c_re, c_im, seq // S5_CHUNK)
    return s5_post(s5_scan(u3, bsz, *tables), h, d_skip, w_glu, b_glu)


def _rope_tables(pos, rot_dim, theta, head_dim, n_heads):
    half = rot_dim // 2
    f32 = np.float32
    inv_freq = f32(theta) ** (-np.arange(half, dtype=f32) / f32(half))
    ang = (pos.astype(f32)[:, None] * inv_freq[None, :]).astype(np.float64)
    cos, sin = np.cos(ang), np.sin(ang)
    rest = head_dim - rot_dim
    n = pos.shape[0]
    c = np.concatenate([cos, cos, np.ones((n, rest))], axis=1)
    s_up = np.concatenate([-sin, np.zeros((n, half + rest))], axis=1)
    s_dn = np.concatenate([np.zeros((n, half)), sin, np.zeros((n, rest))], axis=1)
    tile = lambda a: jnp.asarray(np.tile(a, (1, n_heads)), F32)
    return tile(c), tile(s_up), tile(s_dn)


def _rope_apply(x, c, s_up, s_dn, half):
    return x * c + pltpu.roll(x, LANES - half, 1) * s_up + pltpu.roll(x, half, 1) * s_dn


def _retention_tables():
    c = RET_CHUNK
    log_gamma = np.log(1.0 - 2.0 ** (-5.0 - np.arange(RET_HEADS, dtype=np.float64)))
    i = np.arange(c, dtype=np.float64)
    diff = i[:, None] - i[None, :]
    decay = np.where(diff >= 0, np.exp(diff[None] * log_gamma[:, None, None]), 0.0)
    qdec = np.repeat(np.exp((i + 1.0)[:, None] * log_gamma[None, :]), RET_DK, axis=1)
    kdec = np.repeat(np.exp((c - 1.0 - i)[:, None] * log_gamma[None, :]), RET_DK, axis=1)
    chunk_decay = [float(v) for v in np.exp(c * log_gamma)]
    return jnp.asarray(decay, F32), jnp.asarray(qdec, F32), jnp.asarray(kdec, F32), chunk_decay


def _retention_kernel(chunk_decay, q_ref, k_ref, v0_ref, v1_ref, g0_ref, g1_ref, c_ref, su_ref, sd_ref,
                      dec_ref, qdec_ref, kdec_ref, lng_ref, lnb_ref, o_ref, state_ref):
    @pl.when(pl.program_id(1) == 0)
    def _():
        state_ref[...] = jnp.zeros_like(state_ref)

    half = RET_DK // 2
    tabs = (c_ref[...], su_ref[...], sd_ref[...])
    q = jnp.concatenate([_rope_apply(q_ref[:, s:s + LANES], *tabs, half) for s in (0, LANES)], axis=1)
    k = jnp.concatenate([_rope_apply(k_ref[:, s:s + LANES], *tabs, half) for s in (0, LANES)], axis=1)
    k = k * (RET_DK ** -0.5)
    q_dec = q * qdec_ref[...]
    k_dec = k * kdec_ref[...]
    v = jnp.concatenate([v0_ref[...], v1_ref[...]], axis=1)
    gate = jnp.concatenate([g0_ref[...], g1_ref[...]], axis=1)
    outs = []
    for h in range(RET_HEADS):
        ks = slice(h * RET_DK, (h + 1) * RET_DK)
        vs = slice(h * RET_DV, (h + 1) * RET_DV)
        vh = v[:, vs].astype(BF16)
        scores = _dot_nt(q[:, ks].astype(BF16), k[:, ks].astype(BF16)) * dec_ref[h]
        y = _dot(scores.astype(BF16), vh) + _dot(q_dec[:, ks].astype(BF16), state_ref[h].astype(BF16))
        state_ref[h] = state_ref[h] * chunk_decay[h] + _dot_tn(k_dec[:, ks].astype(BF16), vh)
        mu = jnp.mean(y, axis=-1, keepdims=True)
        yc = y - mu
        var = jnp.mean(yc * yc, axis=-1, keepdims=True)
        outs.append(yc * lax.rsqrt(var + RET_GN_EPS))
    yn = jnp.concatenate(outs, axis=1) * lng_ref[...] + lnb_ref[...]
    o_ref[...] = (gate * _sigmoid(gate) * yn).astype(o_ref.dtype)


def retention_mixer(h, bsz, seq, col0, ln_g, ln_b):
    c = RET_CHUNK
    n_chunk = seq // c
    qk_w = RET_HEADS * RET_DK
    v_w = RET_HEADS * RET_DV
    assert col0 % qk_w == 0 and qk_w == 2 * LANES and v_w == 2 * qk_w
    cb = col0 // qk_w
    rc, rsu, rsd = _rope_tables(np.arange(seq), RET_DK, RET_THETA, RET_DK, 2)
    dec, qdec, kdec, chunk_decay = _retention_tables()
    row = lambda j: (lambda b, n: (b * n_chunk + n, j))
    pos = lambda b, n: (n, 0)
    fixed2 = lambda b, n: (0, 0)
    kern = functools.partial(_retention_kernel, chunk_decay)
    return pl.pallas_call(
        kern,
        grid=(bsz, n_chunk),
        in_specs=[pl.BlockSpec((c, qk_w), row(cb)), pl.BlockSpec((c, qk_w), row(cb + 1)),
                  pl.BlockSpec((c, qk_w), row(cb + 2)), pl.BlockSpec((c, qk_w), row(cb + 3)),
                  pl.BlockSpec((c, qk_w), row(cb + 4)), pl.BlockSpec((c, qk_w), row(cb + 5)),
                  pl.BlockSpec((c, LANES), pos), pl.BlockSpec((c, LANES), pos), pl.BlockSpec((c, LANES), pos),
                  pl.BlockSpec((RET_HEADS, c, c), lambda b, n: (0, 0, 0)),
                  pl.BlockSpec((c, qk_w), fixed2), pl.BlockSpec((c, qk_w), fixed2),
                  pl.BlockSpec((1, v_w), fixed2), pl.BlockSpec((1, v_w), fixed2)],
        out_specs=pl.BlockSpec((c, v_w), lambda b, n: (b * n_chunk + n, 0)),
        out_shape=jax.ShapeDtypeStruct((bsz * seq, v_w), BF16),
        scratch_shapes=[pltpu.VMEM((RET_HEADS, RET_DK, RET_DV), F32)],
        compiler_params=_params("parallel", "arbitrary"),
        name="retention",
    )(h, h, h, h, h, h, rc, rsu, rsd, dec, qdec, kdec, ln_g.reshape(1, v_w), ln_b.reshape(1, v_w))


NSA_KV_W = NSA_KV_GROUPS * NSA_HEAD_DIM
NSA_GATE_COLS = 3 * NSA_HEADS


def _nsa_prep_kernel(q_ref, kvc_ref, kvs_ref, kvw_ref, c_ref, su_ref, sd_ref,
                     qo_ref, kc_ref, vc_ref, ks_ref, vs_ref, kw_ref, vw_ref):
    half = NSA_ROT_DIM // 2
    tabs = (c_ref[...], su_ref[...], sd_ref[...])
    scale = NSA_HEAD_DIM ** -0.5 * math.log2(math.e)
    qo_ref[...] = jnp.concatenate(
        [_rope_apply(q_ref[:, s:s + LANES], *tabs, half) * scale for s in range(0, NSA_WIDTH, LANES)],
        axis=1).astype(qo_ref.dtype)

    def split(x, o_ref):
        for g in range(NSA_KV_GROUPS):
            o_ref[0, g] = x[:, g * NSA_HEAD_DIM:(g + 1) * NSA_HEAD_DIM].astype(o_ref.dtype)

    split(kvc_ref[:, :NSA_KV_W], kc_ref)
    split(kvc_ref[:, NSA_KV_W:], vc_ref)
    split(_rope_apply(kvs_ref[:, :NSA_KV_W], *tabs, half), ks_ref)
    split(kvs_ref[:, NSA_KV_W:], vs_ref)
    split(_rope_apply(kvw_ref[:, :NSA_KV_W], *tabs, half), kw_ref)
    split(kvw_ref[:, NSA_KV_W:], vw_ref)


def nsa_prep(h, bsz, seq, tl=512):
    tl = min(tl, seq)
    nl = seq // tl
    rc, rsu, rsd = _rope_tables(np.arange(seq), NSA_ROT_DIM, ROPE_THETA, NSA_HEAD_DIM, LANES // NSA_HEAD_DIM)
    row = lambda j: (lambda b, l: (b * nl + l, j))
    pos = lambda b, l: (l, 0)
    kv_out = pl.BlockSpec((1, NSA_KV_GROUPS, tl, NSA_HEAD_DIM), lambda b, l: (b, 0, l, 0))
    kv_shape = lambda dt: jax.ShapeDtypeStruct((bsz, NSA_KV_GROUPS, seq, NSA_HEAD_DIM), dt)
    two = 2 * NSA_KV_W
    return pl.pallas_call(
        _nsa_prep_kernel,
        grid=(bsz, nl),
        in_specs=[pl.BlockSpec((tl, NSA_WIDTH), row(1)),
                  pl.BlockSpec((tl, two), row(4)), pl.BlockSpec((tl, two), row(5)), pl.BlockSpec((tl, two), row(6)),
                  pl.BlockSpec((tl, LANES), pos), pl.BlockSpec((tl, LANES), pos), pl.BlockSpec((tl, LANES), pos)],
        out_specs=[pl.BlockSpec((tl, NSA_WIDTH), lambda b, l: (b * nl + l, 0)),
                   kv_out, kv_out, kv_out, kv_out, kv_out, kv_out],
        out_shape=[jax.ShapeDtypeStruct((bsz * seq, NSA_WIDTH), BF16),
                   kv_shape(F32), kv_shape(F32), kv_shape(BF16), kv_shape(BF16), kv_shape(BF16), kv_shape(BF16)],
        compiler_params=_params("parallel", "parallel"),
        name="nsa_prep",
    )(h, h, h, h, rc, rsu, rsd)


def _nsa_compress_kernel(hk_ref, hv_ref, pek_ref, pev_ref, kw1_ref, kb1_ref, kw2_ref, vw1_ref, vb1_ref, vw2_ref,
                         c_ref, su_ref, sd_ref, ko_ref, vo_ref):
    def mlp(h_ref, pe_ref, w1_ref, b1_ref, w2_ref):
        hb = h_ref[0, 0]
        rows = hb.shape[0]
        first = _dot((hb + pe_ref[0:1, :]).astype(BF16), w1_ref[0])
        second = _dot((hb + pe_ref[1:2, :]).astype(BF16), w1_ref[1])
        hid = _gelu(first + pltpu.roll(second, rows - 1, 0) + b1_ref[...])
        return _dot(hid.astype(BF16), w2_ref[...])

    kc = _rope_apply(mlp(hk_ref, pek_ref, kw1_ref, kb1_ref, kw2_ref), c_ref[...], su_ref[...], sd_ref[...],
                     NSA_ROT_DIM // 2)
    vc = mlp(hv_ref, pev_ref, vw1_ref, vb1_ref, vw2_ref)
    ko_ref[0, 0] = kc[:, :NSA_HEAD_DIM].astype(ko_ref.dtype)
    vo_ref[0, 0] = vc[:, :NSA_HEAD_DIM].astype(vo_ref.dtype)


def nsa_compress(kc, vc, pe_k, pe_v, ck_w1, ck_b1, ck_w2, cv_w1, cv_b1, cv_w2):
    bsz, grp, seq, d = kc.shape
    n_rows = seq // CMP_STRIDE
    flat = CMP_STRIDE * d
    cmp_end = np.arange(n_rows) * CMP_STRIDE + CMP_BLOCK - 1
    rc, rsu, rsd = _rope_tables(cmp_end, NSA_ROT_DIM, ROPE_THETA, NSA_HEAD_DIM, LANES // NSA_HEAD_DIM)
    pad_w2 = lambda w: jnp.pad(w, ((0, 0), (0, LANES - d))).astype(BF16)
    blk = pl.BlockSpec((1, 1, n_rows, flat), lambda b, g: (b, g, 0, 0))
    f2 = lambda b, g: (0, 0)
    f3 = lambda b, g: (0, 0, 0)
    w_specs = [pl.BlockSpec((2, flat, CMP_HIDDEN), f3), pl.BlockSpec((1, CMP_HIDDEN), f2),
               pl.BlockSpec((CMP_HIDDEN, LANES), f2)]
    out_spec = pl.BlockSpec((1, 1, n_rows, d), lambda b, g: (b, g, 0, 0))
    out_shape = jax.ShapeDtypeStruct((bsz, grp, n_rows, d), BF16)
    return pl.pallas_call(
        _nsa_compress_kernel,
        grid=(bsz, grp),
        in_specs=[blk, blk, pl.BlockSpec((2, flat), f2), pl.BlockSpec((2, flat), f2)] + w_specs + w_specs
                 + [pl.BlockSpec((n_rows, LANES), f2)] * 3,
        out_specs=[out_spec, out_spec],
        out_shape=[out_shape, out_shape],
        compiler_params=_params("parallel", "parallel"),
        name="nsa_compress",
    )(kc.reshape(bsz, grp, n_rows, flat), vc.reshape(bsz, grp, n_rows, flat),
      pe_k.reshape(2, flat), pe_v.reshape(2, flat),
      ck_w1.reshape(2, flat, CMP_HIDDEN).astype(BF16), ck_b1.reshape(1, CMP_HIDDEN), pad_w2(ck_w2),
      cv_w1.reshape(2, flat, CMP_HIDDEN).astype(BF16), cv_b1.reshape(1, CMP_HIDDEN), pad_w2(cv_w2),
      rc, rsu, rsd)


def _bias_rows(bias):
    return jnp.concatenate([bias] * NSA_HPG, axis=0)


def _nsa_attn_kernel(seq, tk, q_ref, gate_ref, gexp_ref, kc_ref, vc_ref, ks_ref, vs_ref, kw_ref, vw_ref,
                     mmap_ref, expand_ref, o_ref):
    n_blk = seq // SLC_BLOCK
    n_sel = min(N_SLC, n_blk)
    hd = NSA_HEAD_DIM
    w = NSA_HPG * hd
    groups = range(NSA_KV_GROUPS)
    q0 = pl.program_id(1) * Q_BLOCK
    rows = NSA_HPG * Q_BLOCK
    t_q = q0 + lax.broadcasted_iota(jnp.int32, (Q_BLOCK, 1), 0)
    t_l = q0 + lax.broadcasted_iota(jnp.int32, (1, Q_BLOCK), 1)

    def select(g):
        q = q_ref[:, g * w:(g + 1) * w]
        qs = jnp.concatenate([q[:, h * hd:(h + 1) * hd] for h in range(NSA_HPG)], axis=0)
        kc = kc_ref[0, g]
        n_cmp = kc.shape[0]
        cmp_end = lax.broadcasted_iota(jnp.int32, (1, n_cmp), 1) * CMP_STRIDE + (CMP_BLOCK - 1)
        s = _dot_nt(qs, kc) + _bias_rows(jnp.where(cmp_end <= t_q, 0.0, MASK_VALUE))
        yield
        p = jnp.exp2(s - jnp.max(s, axis=-1, keepdims=True))
        any_key = _bias_rows(jnp.where(t_q >= CMP_BLOCK - 1, 1.0, 0.0))
        inv_l = any_key / jnp.sum(p, axis=-1, keepdims=True)
        o_cmp = _dot(p.astype(BF16), vc_ref[0, g]) * inv_l
        yield
        p = p * inv_l
        imp = p[0:Q_BLOCK]
        for h in range(1, NSA_HPG):
            imp = imp + p[h * Q_BLOCK:(h + 1) * Q_BLOCK]
        imp_t = _dot_split(imp, mmap_ref[...], 'a', 3).T
        yield
        blk = lax.broadcasted_iota(jnp.int32, (n_blk, 1), 0)
        cur = t_l // SLC_BLOCK
        score = jnp.where(blk == 0, FORCE_SCORE,
                          jnp.where(blk == cur, FORCE_SCORE, jnp.where(blk == cur - 1, FORCE_SCORE, imp_t)))
        score = jnp.where(blk * SLC_BLOCK <= t_l, score, -FORCE_SCORE)
        sel_t = jnp.zeros((n_blk, Q_BLOCK), F32)
        for _ in range(n_sel):
            best = jnp.max(score, axis=0, keepdims=True)
            idx = jnp.min(jnp.where(score == best, blk, n_blk), axis=0, keepdims=True)
            pick = blk == idx
            sel_t = jnp.where(pick, 1.0, sel_t)
            score = jnp.where(pick, -jnp.inf, score)
            yield
        return qs, o_cmp, sel_t.T.astype(BF16)

    selected = _run_interleaved([select(g) for g in groups])
    qs = [r[0] for r in selected]
    o_cmp = [r[1] for r in selected]
    sel_b = [r[2] for r in selected]

    def slc_tile(kt, carry):
        k0 = pl.multiple_of(kt * tk, tk)
        kpos = k0 + lax.broadcasted_iota(jnp.int32, (1, tk), 1)
        expand = expand_ref[:, pl.ds(k0, tk)]
        chosen = [_dot(sel_b[g], expand) for g in groups]
        scores = [_dot_nt(qs[g], ks_ref[0, g, pl.ds(k0, tk), :]) for g in groups]
        out = []
        for g in groups:
            m, l, acc = carry[g]
            bias = jnp.where(kpos <= t_q, jnp.where(chosen[g] > 0.5, 0.0, MASK_VALUE), MASK_VALUE)
            s = scores[g] + _bias_rows(bias)
            m_new = jnp.maximum(m, jnp.max(s, axis=-1, keepdims=True))
            alpha = jnp.exp2(m - m_new)
            p = jnp.exp2(s - m_new)
            l = alpha * l + jnp.sum(p, axis=-1, keepdims=True)
            acc = alpha * acc + _dot(p.astype(BF16), vs_ref[0, g, pl.ds(k0, tk), :])
            out.append((m_new, l, acc))
        return tuple(out)

    n_tiles = (q0 + Q_BLOCK + tk - 1) // tk
    init = tuple((jnp.full((rows, 1), MASK_VALUE, F32), jnp.zeros((rows, 1), F32), jnp.zeros((rows, hd), F32))
                 for _ in groups)
    slc = lax.fori_loop(0, n_tiles, slc_tile, init)

    band = WINDOW + Q_BLOCK
    w0 = pl.multiple_of(jnp.maximum(q0 - WINDOW, 0), Q_BLOCK)
    kpos = w0 + lax.broadcasted_iota(jnp.int32, (1, band), 1)
    win_bias = _bias_rows(jnp.where(kpos <= t_q, jnp.where(kpos > t_q - WINDOW, 0.0, MASK_VALUE), MASK_VALUE))
    sig = _sigmoid(gate_ref[...])

    def head_cols(o):
        return jnp.concatenate([o[h * Q_BLOCK:(h + 1) * Q_BLOCK] for h in range(NSA_HPG)], axis=1)

    def finish(g):
        s = _dot_nt(qs[g], kw_ref[0, g, pl.ds(w0, band), :]) + win_bias
        yield
        p = jnp.exp2(s - jnp.max(s, axis=-1, keepdims=True))
        o_win = _dot(p.astype(BF16), vw_ref[0, g, pl.ds(w0, band), :]) / jnp.sum(p, axis=-1, keepdims=True)
        yield
        gates = _dot_split(sig, gexp_ref[g], 'a', 2)
        yield
        _, l_slc, acc_slc = slc[g]
        out = (gates[:, 0:w] * head_cols(o_cmp[g]) + gates[:, w:2 * w] * head_cols(acc_slc / l_slc)
               + gates[:, 2 * w:3 * w] * head_cols(o_win))
        o_ref[:, g * w:(g + 1) * w] = out.astype(o_ref.dtype)

    _run_interleaved([finish(g) for g in groups])


def _nsa_constants(seq):
    n_blk = seq // SLC_BLOCK
    n_rows = seq // CMP_STRIDE
    per_stride = SLC_BLOCK // CMP_STRIDE
    span = CMP_BLOCK // CMP_STRIDE
    mmap = np.zeros((n_rows, n_blk), np.float32)
    for j in range(n_blk):
        for m in range(per_stride):
            for n in range(span):
                c = per_stride * j + m + n - (span - 1)
                if 0 <= c < n_rows - 1:
                    mmap[c, j] += 1.0
    w = NSA_HPG * NSA_HEAD_DIM
    gexp = np.zeros((NSA_KV_GROUPS, LANES, 3 * w), np.float32)
    for g in range(NSA_KV_GROUPS):
        for h in range(NSA_HPG):
            for br in range(3):
                gexp[g, (g * NSA_HPG + h) * 3 + br, br * w + h * NSA_HEAD_DIM: br * w + (h + 1) * NSA_HEAD_DIM] = 1.0
    expand = (np.arange(n_blk)[:, None] == (np.arange(seq)[None, :] // SLC_BLOCK)).astype(np.float32)
    return jnp.asarray(mmap, BF16), jnp.asarray(gexp, BF16), jnp.asarray(expand, BF16)


def nsa_attention(qr, h, gate_col_block, k_cmp, v_cmp, ks, vs, kw, vw, bsz, seq, tk=512):
    tk = min(tk, seq)
    nq = seq // Q_BLOCK
    mmap, gexp, expand = _nsa_constants(seq)
    n_rows = k_cmp.shape[2]
    qblk = lambda b, i: (b * nq + i, 0)
    kv = lambda n: pl.BlockSpec((1, NSA_KV_GROUPS, n, NSA_HEAD_DIM), lambda b, i: (b, 0, 0, 0))
    kern = functools.partial(_nsa_attn_kernel, seq, tk)
    return pl.pallas_call(
        kern,
        grid=(bsz, nq),
        in_specs=[pl.BlockSpec((Q_BLOCK, NSA_WIDTH), qblk),
                  pl.BlockSpec((Q_BLOCK, LANES), lambda b, i: (b * nq + i, gate_col_block)),
                  pl.BlockSpec(gexp.shape, lambda b, i: (0, 0, 0)),
                  kv(n_rows), kv(n_rows), kv(seq), kv(seq), kv(seq), kv(seq),
                  pl.BlockSpec(mmap.shape, lambda b, i: (0, 0)),
                  pl.BlockSpec(expand.shape, lambda b, i: (0, 0))],
        out_specs=pl.BlockSpec((Q_BLOCK, NSA_WIDTH), qblk),
        out_shape=jax.ShapeDtypeStruct((bsz * seq, NSA_WIDTH), BF16),
        compiler_params=_params("parallel", "arbitrary"),
        name="nsa_attention",
    )(qr, h, gexp, k_cmp, v_cmp, ks, vs, kw, vw, mmap, expand)


def nsa_mixer(h, bsz, seq, gate_col_block, pe_k, pe_v, ck_w1, ck_b1, ck_w2, cv_w1, cv_b1, cv_w2):
    qr, kc, vc, ks, vs, kw, vw = nsa_prep(h, bsz, seq)
    k_cmp, v_cmp = nsa_compress(kc, vc, pe_k, pe_v, ck_w1, ck_b1, ck_w2, cv_w1, cv_b1, cv_w2)
    return nsa_attention(qr, h, gate_col_block, k_cmp, v_cmp, ks, vs, kw, vw, bsz, seq)


def _head_ones(width, head_dim):
    idx = np.arange(width) // head_dim
    return jnp.asarray(idx[:, None] == idx[None, :], BF16)


def _softplus(x):
    return jnp.maximum(x, 0.0) + jnp.log(1.0 + jnp.exp(-jnp.abs(x)))


def _rwkv_pre_kernel(p_ref, prev_ref, mu_ref, w0_ref, wup_ref, a0_ref, aup_ref, gup_ref, kk_ref, ka_ref, rk_ref,
                     ones_ref, r_o, k_o, v_o, kk_o, b_o, ld_o, g_o, bonus_o):
    w = RWKV_WIDTH
    p = p_ref[...]
    first_row = jnp.where(pl.program_id(1) == 0, 0.0, prev_ref[7:8, :])
    is_row0 = lax.broadcasted_iota(jnp.int32, p.shape, 0) == 0
    prev = jnp.where(is_row0, first_row, pltpu.roll(p, 1, 0))
    ps = p + (prev - p) * mu_ref[...]
    r, k, v = ps[:, 0:w], ps[:, w:2 * w], ps[:, 2 * w:3 * w]
    o = 3 * w
    w_lo = ps[:, o:o + RWKV_LORA_W]
    a_lo = ps[:, o + RWKV_LORA_W:o + RWKV_LORA_W + RWKV_LORA_A]
    g_lo = ps[:, o + RWKV_LORA_W + RWKV_LORA_A:]
    wlog = -_softplus(-(w0_ref[...] + _dot(jnp.tanh(w_lo).astype(BF16), wup_ref[...]))) - 0.5
    a = _sigmoid(a0_ref[...] + _dot(a_lo.astype(BF16), aup_ref[...]))
    g = _dot(_sigmoid(g_lo).astype(BF16), gup_ref[...])
    kk = k * kk_ref[...]
    norm = jnp.sqrt(_dot_split(kk * kk, ones_ref[...], 'a', 2))
    kk = kk / jnp.maximum(norm, 1e-12)
    k2 = k * (1.0 + (a - 1.0) * ka_ref[...])
    r_o[...] = r
    k_o[...] = k2
    v_o[...] = v
    kk_o[...] = kk
    b_o[...] = kk * a
    ld_o[...] = -jnp.exp(wlog)
    g_o[...] = g
    bonus_o[...] = _dot_split(r * k2 * rk_ref[...], ones_ref[...], 'a', 2) * v


def rwkv_pre(h, bsz, seq, mu, w0, w_up, a0, a_up, g_up, k_k, k_a, r_k, tl=512):
    tl = min(tl, seq)
    nl = seq // tl
    w = RWKV_WIDTH
    cols = RWKV_COLS
    ones = _head_ones(w, RWKV_HEAD_DIM)
    f2 = lambda b, l: (0, 0)
    vec = pl.BlockSpec((1, w), f2)
    out_spec = pl.BlockSpec((tl, w), lambda b, l: (b * nl + l, 0))
    out_shape = jax.ShapeDtypeStruct((bsz * seq, w), F32)
    return pl.pallas_call(
        _rwkv_pre_kernel,
        grid=(bsz, nl),
        in_specs=[pl.BlockSpec((tl, cols), lambda b, l: (b * nl + l, 0)),
                  pl.BlockSpec((8, cols), lambda b, l: (jnp.maximum((b * seq + l * tl) // 8 - 1, 0), 0)),
                  pl.BlockSpec((1, cols), f2), vec, pl.BlockSpec((RWKV_LORA_W, w), f2),
                  vec, pl.BlockSpec((RWKV_LORA_A, w), f2), pl.BlockSpec((RWKV_LORA_G, w), f2),
                  vec, vec, vec, pl.BlockSpec((w, w), f2)],
        out_specs=[out_spec] * 8,
        out_shape=[out_shape] * 8,
        compiler_params=_params("parallel", "parallel"),
        name="rwkv_pre",
    )(h, h, mu.reshape(1, cols), w0.reshape(1, w), w_up.astype(BF16), a0.reshape(1, w), a_up.astype(BF16),
      g_up.astype(BF16), k_k.reshape(1, w), k_a.reshape(1, w), r_k.reshape(1, w), ones)


def _rwkv_masks():
    t, pk = RWKV_CHUNK, RWKV_PACK
    n = t * pk
    ri = np.arange(n)
    same = (ri[:, None] // t) == (ri[None, :] // t)
    tt, ss = ri[:, None] % t, ri[None, :] % t
    levels = []
    k = 1
    while k < t:
        levels.append(same & (tt // (2 * k) == ss // (2 * k)) & ((tt // k) % 2 == 1) & ((ss // k) % 2 == 0))
        k *= 2
    lvl = np.stack(levels).astype(np.float32)
    tri = (np.arange(t)[:, None] >= np.arange(t)[None, :]).astype(np.float32)
    head_lane = ((ri[:, None] // t) == (np.arange(pk * RWKV_HEAD_DIM)[None, :] // RWKV_HEAD_DIM)).astype(np.float32)
    return (jnp.asarray(tri, BF16), jnp.asarray(head_lane), jnp.asarray(same.astype(np.float32)), jnp.asarray(lvl))


def _rwkv_chain(r, k, v, kk, b, ld, st, tri, hl, bd, lvl_ref):
    t, pk = RWKV_CHUNK, RWKV_PACK
    n = t * pk
    c = _dot_split(tri, ld, 'b', 3)
    yield
    c_end = c[t - 1:t, :]
    e_neg = jnp.exp(-c)
    e_end = jnp.exp(c_end - c)
    kkd = (kk * jnp.exp(c - ld)).astype(BF16)
    rd = (r * jnp.exp(c)).astype(BF16)

    def big(x):
        return (jnp.concatenate([x] * pk, axis=0) * hl).astype(BF16)

    st_b = st.astype(BF16)
    v_big = big(v)
    a_all = _dot_nt(jnp.concatenate([kkd, rd], axis=0),
                    jnp.concatenate([big(k * e_neg), big(b * e_neg)], axis=0))
    yield
    ti = lax.broadcasted_iota(jnp.int32, (t, n), 0)
    si = lax.broadcasted_iota(jnp.int32, (t, n), 1) % t
    strict = ti > si
    incl = ti >= si
    a_kk = jnp.where(strict, a_all[:t, :n], 0.0)
    a_kb = jnp.where(strict, a_all[:t, n:], 0.0)
    a_rk = jnp.where(incl, a_all[t:, :n], 0.0)
    a_rb = jnp.where(incl, a_all[t:, n:], 0.0)
    rhs = _dot(kkd, st_b) + _dot(a_kk.astype(BF16), v_big)
    yield
    a_bd = jnp.concatenate([a_kb] * pk, axis=0) * bd
    m = jnp.where(lax.broadcasted_iota(jnp.int32, (n, n), 0) == lax.broadcasted_iota(jnp.int32, (n, n), 1), 1.0, 0.0)
    for lv in range(lvl_ref.shape[0]):
        mb = m.astype(BF16)
        ma = _dot(mb, (a_bd * lvl_ref[lv]).astype(BF16)).astype(BF16)
        yield
        m = m - _dot(ma, mb)
        yield
    u_big = _dot(m.astype(BF16), big(rhs))
    yield
    u = u_big[0:t]
    for h in range(1, pk):
        u = u + u_big[h * t:(h + 1) * t]
    y = _dot(rd, st_b) + _dot(a_rk.astype(BF16), v_big) - _dot(a_rb.astype(BF16), big(u))
    yield
    decay_col = jnp.broadcast_to(jnp.exp(c_end), st.shape).T
    kb_end = jnp.concatenate([k * e_end, -(b * e_end)], axis=0).astype(BF16)
    vu = jnp.concatenate([v, u], axis=0).astype(BF16)
    return y, decay_col * st + bd * _dot_tn(kb_end, vu)


def _rwkv_chunk_kernel(r_ref, k_ref, v_ref, kk_ref, b_ref, ld_ref, tri_ref, hl_ref, bd_ref, lvl_ref, y_ref, st_ref):
    @pl.when(pl.program_id(0) == 0)
    def _():
        st_ref[...] = jnp.zeros_like(st_ref)

    wp = RWKV_PACK * RWKV_HEAD_DIM
    tri, hl, bd = tri_ref[...], hl_ref[...], bd_ref[...]
    n_pack = r_ref.shape[2] // wp
    where = [(bi, slice(g * wp, (g + 1) * wp)) for bi in range(r_ref.shape[0]) for g in range(n_pack)]
    loaded = [tuple(ref[bi, :, cols] for ref in (r_ref, k_ref, v_ref, kk_ref, b_ref, ld_ref)) + (st_ref[i],)
              for i, (bi, cols) in enumerate(where)]
    results = _run_interleaved([_rwkv_chain(*args, tri, hl, bd, lvl_ref) for args in loaded])
    for i, ((bi, cols), (y, st_new)) in enumerate(zip(where, results)):
        y_ref[bi, :, cols] = y
        st_ref[i] = st_new


def rwkv_chunk(r, k, v, kk, b, ld, bsz, seq):
    t, pk = RWKV_CHUNK, RWKV_PACK
    n_chunk = seq // t
    w = RWKV_WIDTH
    wp = pk * RWKV_HEAD_DIM
    assert t == RWKV_HEAD_DIM
    tri, hl, bd, lvl = _rwkv_masks()
    blk = pl.BlockSpec((bsz, t, w), lambda c: (0, c, 0))
    f2 = lambda c: (0, 0)
    shaped = lambda a: a.reshape(bsz, seq, w)
    y = pl.pallas_call(
        _rwkv_chunk_kernel,
        grid=(n_chunk,),
        in_specs=[blk] * 6 + [pl.BlockSpec(tri.shape, f2), pl.BlockSpec(hl.shape, f2), pl.BlockSpec(bd.shape, f2),
                              pl.BlockSpec(lvl.shape, lambda c: (0, 0, 0))],
        out_specs=blk,
        out_shape=jax.ShapeDtypeStruct((bsz, seq, w), F32),
        scratch_shapes=[pltpu.VMEM((bsz * (w // wp), wp, wp), F32)],
        compiler_params=_params("arbitrary"),
        name="rwkv_chunk",
    )(shaped(r), shaped(k), shaped(v), shaped(kk), shaped(b), shaped(ld), tri, hl, bd, lvl)
    return y.reshape(bsz * seq, w)


def _rwkv_post_kernel(y_ref, bonus_ref, g_ref, lng_ref, lnb_ref, ones_ref, o_ref):
    y = y_ref[...]
    inv = 1.0 / RWKV_HEAD_DIM
    mu = _dot_split(y, ones_ref[...], 'a', 2) * inv
    yc = y - mu
    var = _dot_split(yc * yc, ones_ref[...], 'a', 2) * inv
    yn = yc * lax.rsqrt(var + RWKV_GN_EPS) * lng_ref[...] + lnb_ref[...]
    o_ref[...] = ((yn + bonus_ref[...]) * g_ref[...]).astype(o_ref.dtype)


def rwkv_post(y, bonus, g, ln_g, ln_b, tm=1024):
    n, w = y.shape
    tm = min(tm, n)
    row = pl.BlockSpec((tm, w), lambda i: (i, 0))
    vec = pl.BlockSpec((1, w), lambda i: (0, 0))
    return pl.pallas_call(
        _rwkv_post_kernel,
        grid=(n // tm,),
        in_specs=[row, row, row, vec, vec, pl.BlockSpec((w, w), lambda i: (0, 0))],
        out_specs=row,
        out_shape=jax.ShapeDtypeStruct((n, w), BF16),
        compiler_params=_params("parallel"),
        name="rwkv_post",
    )(y, bonus, g, ln_g.reshape(1, w), ln_b.reshape(1, w), _head_ones(w, RWKV_HEAD_DIM))


def rwkv7_mixer(h, bsz, seq, mu, w0, w_up, a0, a_up, g_up, k_k, k_a, r_k, ln_g, ln_b):
    r, k, v, kk, b, ld, g, bonus = rwkv_pre(h, bsz, seq, mu, w0, w_up, a0, a_up, g_up, k_k, k_a, r_k)
    y = rwkv_chunk(r, k, v, kk, b, ld, bsz, seq)
    return rwkv_post(y, bonus, g, ln_g, ln_b)


AB_IN = S5_WIDTH + NSA_WIDTH + 6 * NSA_KV_W + NSA_GATE_COLS
AB_IN_PADDED = -(-AB_IN // LANES) * LANES
NSA_GATE_COL_BLOCK = (AB_IN - NSA_GATE_COLS) // LANES
PROJ_TM = 512


def kernel(x, ab_w_in, ab_w_out, s5_lam_re, s5_lam_im, s5_log_dt, s5_b_re, s5_b_im, s5_c_re, s5_c_im, s5_d, s5_w_glu, s5_b_glu, nsa_pe_k, nsa_pe_v, nsa_ck_w1, nsa_ck_b1, nsa_ck_w2, nsa_cv_w1, nsa_cv_b1, nsa_cv_w2, cd_w_in, cd_w_out, rwkv_mu, rwkv_w0, rwkv_w_up, rwkv_a0, rwkv_a_up, rwkv_g_up, rwkv_k_k, rwkv_k_a, rwkv_r_k, rwkv_ln_g, rwkv_ln_b, ret_ln_g, ret_ln_b, ln1_g, ln1_b, ln2_g, ln2_b, moe_router, moe_bias, moe_w1, moe_w3, moe_w2, sh_w1, sh_w3, sh_w2):
    bsz, seq, d = x.shape
    assert (AB_IN - NSA_GATE_COLS) % LANES == 0
    xf = x.reshape(bsz * seq, d)
    x_in = xf
    for layer in range(DEPTH):
        i = layer // 2
        if layer % 2 == 0:
            w_in = jnp.pad(ab_w_in[i], ((0, 0), (0, AB_IN_PADDED - AB_IN))).astype(BF16)
            h, u3 = project(x_in, w_in, PROJ_TM, chunked=(S5_CHUNK, S5_WIDTH))
            y_1 = s5_mixer(h, u3, bsz, seq, s5_lam_re[i], s5_lam_im[i], s5_log_dt[i], s5_b_re[i], s5_b_im[i],
                           s5_c_re[i], s5_c_im[i], s5_d[i], s5_w_glu[i], s5_b_glu[i])
            y_2 = nsa_mixer(h, bsz, seq, NSA_GATE_COL_BLOCK, nsa_pe_k[i], nsa_pe_v[i], nsa_ck_w1[i], nsa_ck_b1[i],
                            nsa_ck_w2[i], nsa_cv_w1[i], nsa_cv_b1[i], nsa_cv_w2[i])
            w_out = ab_w_out[i]
        else:
            h = project(x_in, cd_w_in[i].astype(BF16), PROJ_TM)
            y_1 = rwkv7_mixer(h, bsz, seq, rwkv_mu[i], rwkv_w0[i], rwkv_w_up[i], rwkv_a0[i], rwkv_a_up[i],
                              rwkv_g_up[i], rwkv_k_k[i], rwkv_k_a[i], rwkv_r_k[i], rwkv_ln_g[i], rwkv_ln_b[i])
            y_2 = retention_mixer(h, bsz, seq, RWKV_COLS, ret_ln_g[i], ret_ln_b[i])
            w_out = cd_w_out[i]
        xf = out_proj_ln(y_1, y_2, w_out, xf, ln1_g[layer], ln1_b[layer])
        xf, x_in = moe_block(xf, moe_router[layer], moe_bias[layer], moe_w1[layer], moe_w3[layer],
                             moe_w2[layer], sh_w1[layer], sh_w3[layer], sh_w2[layer], ln2_g[layer], ln2_b[layer])
    return xf.reshape(bsz, seq, d)
```

```python
import functools
import math

import jax
import jax.numpy as jnp
import numpy as np
from jax import lax
from jax.experimental import pallas as pl
from jax.experimental.pallas import tpu as pltpu

F32 = jnp.float32
BF16 = jnp.bfloat16
HIGHEST = lax.Precision.HIGHEST
FP8 = jnp.float8_e4m3fn
FP8_MAX = 448.0
FP8_TINY = 1e-30

VMEM_LIMIT_BYTES = 52 * 1024 * 1024
LANES = 128

LN_EPS = 1e-5
DEPTH = 2
ALPHA = (2 * DEPTH) ** 0.25

S5_GROUPS, S5_GROUP_CH, S5_STATE = 32, 16, 64
S5_WIDTH = S5_GROUPS * S5_GROUP_CH
S5_CHUNK = 16
S5_PACK = 8
NSA_HEADS, NSA_KV_GROUPS, NSA_HEAD_DIM = 8, 2, 64
NSA_HPG = NSA_HEADS // NSA_KV_GROUPS
NSA_WIDTH = NSA_HEADS * NSA_HEAD_DIM
NSA_ROT_DIM = NSA_HEAD_DIM // 4
ROPE_THETA = 500000.0
CMP_BLOCK, CMP_STRIDE, CMP_HIDDEN = 32, 16, 128
SLC_BLOCK, N_SLC, WINDOW, Q_BLOCK = 64, 16, 512, 128
FORCE_SCORE = 1e6
MASK_VALUE = -1e30
RWKV_HEADS, RWKV_HEAD_DIM = 8, 64
RWKV_WIDTH = RWKV_HEADS * RWKV_HEAD_DIM
RWKV_LORA_W, RWKV_LORA_A, RWKV_LORA_G = 64, 64, 128
RWKV_COLS = 3 * RWKV_WIDTH + RWKV_LORA_W + RWKV_LORA_A + RWKV_LORA_G
RWKV_GN_EPS = 64e-5
RWKV_CHUNK = 64
RWKV_PACK = 4
RET_HEADS, RET_DK, RET_DV, RET_CHUNK = 4, 64, 128, 128
RET_THETA = 10000.0
RET_GN_EPS = 1e-5
N_EXPERTS, TOP_K, EXPERT_FF = 64, 8, 256
N_EXPERT_GROUPS, TOPK_GROUPS = 8, 4
EXPERTS_PER_GROUP = N_EXPERTS // N_EXPERT_GROUPS
ROUTED_SCALE = 2.5
MOE_EXPERTS_PER_STEP = 4


def _params(*sem):
    return pltpu.CompilerParams(dimension_semantics=sem, vmem_limit_bytes=VMEM_LIMIT_BYTES)


def _dot(a, b, **kw):
    return jnp.dot(a, b, preferred_element_type=F32, **kw)


def _dot_nt(a, b, **kw):
    return lax.dot_general(a, b, (((1,), (1,)), ((), ())), preferred_element_type=F32, **kw)


def _dot_tn(a, b, **kw):
    return lax.dot_general(a, b, (((0,), (0,)), ((), ())), preferred_element_type=F32, **kw)


def _dot_split(a, b, split, parts):
    rest = a if split == 'a' else b
    acc = None
    for _ in range(parts):
        piece = rest.astype(BF16)
        term = _dot(piece, b) if split == 'a' else _dot(a, piece)
        acc = term if acc is None else acc + term
        rest = rest - piece.astype(F32)
    return acc


def _run_interleaved(gens):
    results = [None] * len(gens)
    live = list(range(len(gens)))
    while live:
        for i in list(live):
            try:
                next(gens[i])
            except StopIteration as done:
                results[i] = done.value
                live.remove(i)
    return results


def _gelu(x):
    return 0.5 * x * (1.0 + jnp.tanh(math.sqrt(2.0 / math.pi) * (x + 0.044715 * (x * x * x))))


def _sigmoid(x):
    return 1.0 / (1.0 + jnp.exp(-x))


def _layer_norm_rows(z, g, b):
    mu = jnp.mean(z, axis=-1, keepdims=True)
    zc = z - mu
    var = jnp.mean(zc * zc, axis=-1, keepdims=True)
    return zc * lax.rsqrt(var + LN_EPS) * g + b


def _proj_kernel(x_ref, w_ref, o_ref, *chunked_ref):
    y = _dot(x_ref[...].astype(BF16), w_ref[...])
    o_ref[...] = y
    for c_ref in chunked_ref:
        rows, t, w = c_ref.shape
        c_ref[...] = y[:, :w].reshape(rows, t, w)


def project(x, w_bf16, tm, chunked=None):
    m, k = x.shape
    n = w_bf16.shape[1]
    out_specs = [pl.BlockSpec((tm, n), lambda i: (i, 0))]
    out_shape = [jax.ShapeDtypeStruct((m, n), F32)]
    if chunked is not None:
        t, w = chunked
        out_specs.append(pl.BlockSpec((tm // t, t, w), lambda i: (i, 0, 0)))
        out_shape.append(jax.ShapeDtypeStruct((m // t, t, w), F32))
    out = pl.pallas_call(
        _proj_kernel,
        grid=(m // tm,),
        in_specs=[pl.BlockSpec((tm, k), lambda i: (i, 0)), pl.BlockSpec((k, n), lambda i: (0, 0))],
        out_specs=out_specs,
        out_shape=out_shape,
        compiler_params=_params("parallel"),
        name="project",
    )(x, w_bf16)
    return out if chunked is not None else out[0]


def _out_proj_ln_kernel(ya_ref, yb_ref, wa_ref, wb_ref, x_ref, g_ref, b_ref, o_ref):
    mix = _dot(ya_ref[...], wa_ref[...]) + _dot(yb_ref[...], wb_ref[...])
    o_ref[...] = _layer_norm_rows(ALPHA * x_ref[...] + mix, g_ref[...], b_ref[...])


def out_proj_ln(ya, yb, w_out, x, g, b, tm=512):
    n, d = x.shape
    ka, kb = ya.shape[1], yb.shape[1]
    wa = w_out[:ka].astype(BF16)
    wb = w_out[ka:].astype(BF16)
    row = lambda i: (i, 0)
    fixed = lambda i: (0, 0)
    return pl.pallas_call(
        _out_proj_ln_kernel,
        grid=(n // tm,),
        in_specs=[pl.BlockSpec((tm, ka), row), pl.BlockSpec((tm, kb), row),
                  pl.BlockSpec((ka, d), fixed), pl.BlockSpec((kb, d), fixed),
                  pl.BlockSpec((tm, d), row), pl.BlockSpec((1, d), fixed), pl.BlockSpec((1, d), fixed)],
        out_specs=pl.BlockSpec((tm, d), row),
        out_shape=jax.ShapeDtypeStruct((n, d), F32),
        compiler_params=_params("parallel"),
        name="out_proj_ln",
    )(ya, yb, wa, wb, x, g.reshape(1, d), b.reshape(1, d))


def _router_kernel(x_ref, rt_ref, bias_ref, o_ref):
    tr = x_ref.shape[0]
    scores = _sigmoid(_dot_nt(rt_ref[...], x_ref[...], precision=HIGHEST))
    biased = scores + bias_ref[...]
    grp = biased.reshape(N_EXPERT_GROUPS, EXPERTS_PER_GROUP, tr)
    pos = lax.broadcasted_iota(jnp.int32, grp.shape, 1)
    m1 = jnp.max(grp, axis=1, keepdims=True)
    first = jnp.min(jnp.where(grp == m1, pos, EXPERTS_PER_GROUP), axis=1, keepdims=True)
    m2 = jnp.max(jnp.where(pos == first, -jnp.inf, grp), axis=1, keepdims=True)
    gscore = (m1 + m2).reshape(N_EXPERT_GROUPS, tr)
    gidx = lax.broadcasted_iota(jnp.int32, gscore.shape, 0)
    grank = jnp.zeros(gscore.shape, F32)
    for j in range(N_EXPERT_GROUPS):
        row = gscore[j:j + 1, :]
        grank = grank + jnp.where(gidx > j, jnp.where(row >= gscore, 1.0, 0.0), jnp.where(row > gscore, 1.0, 0.0))
    gkeep = jnp.where(grank < TOPK_GROUPS, 1.0, 0.0)
    keep = jnp.broadcast_to(gkeep[:, None, :], grp.shape).reshape(N_EXPERTS, tr)
    masked = jnp.where(keep > 0.5, biased, -jnp.inf)
    eidx = lax.broadcasted_iota(jnp.int32, masked.shape, 0)
    rank = jnp.zeros(masked.shape, F32)
    for j in range(N_EXPERTS):
        row = masked[j:j + 1, :]
        rank = rank + jnp.where(eidx > j, jnp.where(row >= masked, 1.0, 0.0), jnp.where(row > masked, 1.0, 0.0))
    gate = jnp.where(rank < TOP_K, scores, 0.0)
    gate = gate / jnp.sum(gate, axis=0, keepdims=True) * ROUTED_SCALE
    o_ref[...] = jnp.concatenate([gate, jnp.zeros((LANES - N_EXPERTS, tr), F32)], axis=0).T


def moe_router(x, router, bias, tr=512):
    n, d = x.shape
    return pl.pallas_call(
        _router_kernel,
        grid=(n // tr,),
        in_specs=[pl.BlockSpec((tr, d), lambda i: (i, 0)),
                  pl.BlockSpec((N_EXPERTS, d), lambda i: (0, 0)),
                  pl.BlockSpec((N_EXPERTS, 1), lambda i: (0, 0))],
        out_specs=pl.BlockSpec((tr, LANES), lambda i: (i, 0)),
        out_shape=jax.ShapeDtypeStruct((n, LANES), F32),
        compiler_params=_params("parallel"),
        name="moe_router",
    )(x, router.T, bias.reshape(N_EXPERTS, 1))


def _quantize_fp8(a, axes):
    amax = jnp.max(jnp.abs(a), axis=axes, keepdims=True)
    scale = jnp.maximum(amax, FP8_TINY) * (1.0 / FP8_MAX)
    return (a * (1.0 / scale)).astype(FP8), scale


def _swiglu_hidden(xq, x_scale, w1q, w3q, w_scale, gate=None):
    col1 = x_scale * w_scale[0:1, 0:1]
    col3 = x_scale * w_scale[1:2, 0:1]
    if gate is not None:
        col3 = col3 * gate
    h1 = _dot(xq, w1q) * col1
    return h1 * _sigmoid(h1) * (_dot(xq, w3q) * col3)


def _experts_ln_kernel(x_ref, gate_ref, w1_ref, w3_ref, ws_ref, w2_ref, sw1_ref, sw3_ref, sws_ref, sw2_ref,
                       g_ref, b_ref, o_ref, obf_ref, acc_ref, xq_ref, xs_ref):
    step = pl.program_id(1)
    per_step = w1_ref.shape[0]

    @pl.when(step == 0)
    def _():
        xq, xs = _quantize_fp8(x_ref[...], (1,))
        xq_ref[...] = xq
        xs_ref[...] = xs
        h = _swiglu_hidden(xq, xs, sw1_ref[...], sw3_ref[...], sws_ref[...])
        acc_ref[...] = _dot(h.astype(BF16), sw2_ref[...].astype(BF16))

    lane = lax.broadcasted_iota(jnp.int32, gate_ref.shape, 1)
    gates = gate_ref[...]
    xq, xs = xq_ref[...], xs_ref[...]
    hidden = []
    for j in range(per_step):
        gcol = jnp.sum(jnp.where(lane == step * per_step + j, gates, 0.0), axis=1, keepdims=True)
        hidden.append(_swiglu_hidden(xq, xs, w1_ref[j], w3_ref[j], ws_ref[j], gcol).astype(BF16))
    w2 = w2_ref[...].astype(BF16)
    acc_ref[...] += _dot(jnp.concatenate(hidden, axis=1), w2.reshape(per_step * w2.shape[1], w2.shape[2]))

    @pl.when(step == pl.num_programs(1) - 1)
    def _():
        y = _layer_norm_rows(ALPHA * x_ref[...] + acc_ref[...], g_ref[...], b_ref[...])
        o_ref[...] = y
        obf_ref[...] = y.astype(BF16)


def _quantize_expert_weights(w1, w3):
    w1q, s1 = _quantize_fp8(w1, (-2, -1))
    w3q, s3 = _quantize_fp8(w3, (-2, -1))
    scales = jnp.broadcast_to(jnp.concatenate([s1, s3], axis=-2), s1.shape[:-2] + (2, w1.shape[-1]))
    return w1q, w3q, scales


def moe_experts_ln(x, gates, w1, w3, w2, sw1, sw3, sw2, g, b, tm=1024):
    n, d = x.shape
    ne = w1.shape[0]
    w1q, w3q, ws = _quantize_expert_weights(w1, w3)
    sw1q, sw3q, sws = _quantize_expert_weights(sw1, sw3)
    tok = lambda i, e: (i, 0)
    fixed = lambda i, e: (0, 0)
    per_expert = lambda *blk: pl.BlockSpec((MOE_EXPERTS_PER_STEP,) + blk, lambda i, e: (e, 0, 0))
    return pl.pallas_call(
        _experts_ln_kernel,
        grid=(n // tm, ne // MOE_EXPERTS_PER_STEP),
        in_specs=[pl.BlockSpec((tm, d), tok), pl.BlockSpec((tm, LANES), tok),
                  per_expert(d, EXPERT_FF), per_expert(d, EXPERT_FF), per_expert(2, EXPERT_FF),
                  per_expert(EXPERT_FF, d),
                  pl.BlockSpec((d, EXPERT_FF), fixed), pl.BlockSpec((d, EXPERT_FF), fixed),
                  pl.BlockSpec((2, EXPERT_FF), fixed), pl.BlockSpec((EXPERT_FF, d), fixed),
                  pl.BlockSpec((1, d), fixed), pl.BlockSpec((1, d), fixed)],
        out_specs=[pl.BlockSpec((tm, d), tok), pl.BlockSpec((tm, d), tok)],
        out_shape=[jax.ShapeDtypeStruct((n, d), F32), jax.ShapeDtypeStruct((n, d), BF16)],
        scratch_shapes=[pltpu.VMEM((tm, d), F32), pltpu.VMEM((tm, d), FP8), pltpu.VMEM((tm, 1), F32)],
        compiler_params=_params("parallel", "arbitrary"),
        name="moe_experts_ln",
    )(x, gates, w1q, w3q, ws, w2, sw1q, sw3q, sws, sw2, g.reshape(1, d), b.reshape(1, d))


def moe_block(x, router, bias, w1, w3, w2, sw1, sw3, sw2, g, b):
    gates = moe_router(x, router, bias)
    return moe_experts_ln(x, gates, w1, w3, w2, sw1, sw3, sw2, g, b)


def _s5_tables(lam_re, lam_im, log_dt, b_re, b_im, c_re, c_im, n_chunk):
    t, h, p = S5_CHUNK, S5_GROUP_CH, S5_STATE
    dt = jnp.exp(log_dt.astype(F32))[:, None]
    den = lam_re ** 2 + lam_im ** 2

    def lam_pow(k):
        k = jnp.asarray(k, F32)[..., None, None]
        mag = jnp.exp(lam_re * dt * k)
        return mag * jnp.cos(lam_im * dt * k), mag * jnp.sin(lam_im * dt * k)

    lb_re, lb_im = lam_pow(1.0)
    f_re = ((lb_re - 1.0) * lam_re + lb_im * lam_im) / den
    f_im = (lb_im * lam_re - (lb_re - 1.0) * lam_im) / den
    bb_re = f_re[..., None] * b_re - f_im[..., None] * b_im
    bb_im = f_re[..., None] * b_im + f_im[..., None] * b_re
    pr, pi = lam_pow(jnp.arange(t))
    cl_re = c_re[None] * pr[:, :, None, :] - c_im[None] * pi[:, :, None, :]
    cl_im = c_re[None] * pi[:, :, None, :] + c_im[None] * pr[:, :, None, :]
    klag = jnp.einsum('tgop,gpi->tgoi', cl_re, bb_re) - jnp.einsum('tgop,gpi->tgoi', cl_im, bb_im)
    nb = S5_GROUPS // S5_PACK
    split = lambda a, axis: a.reshape(a.shape[:axis] + (nb, S5_PACK) + a.shape[axis + 1:])
    eye = jnp.eye(S5_PACK, dtype=F32)
    lag_t = jnp.transpose(split(klag, 1), (1, 0, 2, 4, 3))
    lag_t = (lag_t[:, :, :, :, None, :] * eye[None, None, :, None, :, None]).reshape(nb, t, LANES, LANES)
    qr, qi = lam_pow(t - 1 - jnp.arange(t))
    st_re = qr[..., None] * bb_re[None] - qi[..., None] * bb_im[None]
    st_im = qr[..., None] * bb_im[None] + qi[..., None] * bb_re[None]
    st = jnp.stack([st_re, st_im], axis=0)
    st_t = jnp.transpose(split(st, 2), (2, 1, 3, 5, 0, 4)).reshape(nb, t, LANES, 2 * p)
    er, ei = lam_pow(jnp.arange(t) + 1)
    x_re = c_re[None] * er[:, :, None, :] - c_im[None] * ei[:, :, None, :]
    x_im = c_re[None] * ei[:, :, None, :] + c_im[None] * er[:, :, None, :]
    cr = jnp.stack([x_re, -x_im], axis=0)
    cr_t = jnp.transpose(split(cr, 2), (2, 1, 0, 5, 3, 4)).reshape(nb, t, 2 * p, LANES)
    levels = max(1, int(math.log2(n_chunk)))
    sr, si = lam_pow(t * (2.0 ** jnp.arange(levels)))
    sr = sr.reshape(levels, nb, S5_PACK * p)
    si = si.reshape(levels, nb, S5_PACK * p)
    a1 = jnp.concatenate([sr, sr], axis=-1)
    a2 = jnp.concatenate([-si, si], axis=-1)
    scan = jnp.transpose(jnp.stack([a1, a2], axis=1), (2, 0, 1, 3))
    return lag_t, st_t, cr_t, scan.astype(F32)


def _s5_build_tables(lag_ref, st_ref, cr_ref, wtoe_ref, wstate_ref, wcross_ref):
    t = lag_ref.shape[1]
    p = S5_STATE
    kp = S5_PACK * p
    wtoe_ref[...] = jnp.zeros_like(wtoe_ref)
    for d in range(t):
        tile = lag_ref[0, d].astype(BF16)
        for j in range(t - d):
            wtoe_ref[j * LANES:(j + 1) * LANES, (j + d) * LANES:(j + d + 1) * LANES] = tile
    lane = lax.broadcasted_iota(jnp.int32, (LANES, LANES), 1)
    row_g = lax.broadcasted_iota(jnp.int32, (LANES, kp), 0) // S5_GROUP_CH
    same_s = row_g == lax.broadcasted_iota(jnp.int32, (LANES, kp), 1) // p
    for j in range(t):
        a = st_ref[0, j]
        swapped = pltpu.roll(a, p, 1)
        for c, both in enumerate((jnp.where(lane < p, a, swapped), jnp.where(lane < p, swapped, a))):
            wide = jnp.concatenate([both] * (kp // LANES), axis=1)
            wstate_ref[j * LANES:(j + 1) * LANES, c * kp:(c + 1) * kp] = jnp.where(same_s, wide, 0.0).astype(BF16)
    same_c = (lax.broadcasted_iota(jnp.int32, (kp, LANES), 0) // p
              == lax.broadcasted_iota(jnp.int32, (kp, LANES), 1) // S5_GROUP_CH)
    for i in range(t):
        a = cr_ref[0, i]
        for c in range(2):
            tall = jnp.concatenate([a[c * p:(c + 1) * p]] * S5_PACK, axis=0)
            wcross_ref[c * kp:(c + 1) * kp, i * LANES:(i + 1) * LANES] = jnp.where(same_c, tall, 0.0).astype(BF16)


def _s5_kernel(u_ref, lag_ref, st_ref, cr_ref, scan_ref, o_ref, wtoe_ref, wstate_ref, wcross_ref):
    @pl.when(pl.program_id(1) == 0)
    def _():
        _s5_build_tables(lag_ref, st_ref, cr_ref, wtoe_ref, wstate_ref, wcross_ref)

    n_chunk, t, _ = u_ref.shape
    x = jnp.concatenate([u_ref[:, j, :] for j in range(t)], axis=1).astype(BF16)
    local = _dot(x, wtoe_ref[...])
    state = _dot(x, wstate_ref[...])
    row = lax.broadcasted_iota(jnp.int32, state.shape, 0)
    s = jnp.where(row >= 1, pltpu.roll(state, 1, 0), 0.0)
    half = state.shape[1] // 2
    level = 0
    d = 1
    while d < n_chunk:
        mult = scan_ref[0, level]
        prev = jnp.where(row >= d, pltpu.roll(s, d, 0), 0.0)
        s = s + mult[0:1, :] * prev + mult[1:2, :] * pltpu.roll(prev, half, 1)
        d *= 2
        level += 1
    y = local + _dot(s.astype(BF16), wcross_ref[...])
    for i in range(t):
        o_ref[:, i, :] = y[:, i * LANES:(i + 1) * LANES]


def s5_scan(u3, bsz, lag_t, st_t, cr_t, scan):
    rows, t, w = u3.shape
    n_chunk = rows // bsz
    kp2 = 2 * S5_PACK * S5_STATE
    table = lambda a: pl.BlockSpec((1,) + a.shape[1:], lambda j, b: (j, 0, 0, 0))
    return pl.pallas_call(
        _s5_kernel,
        grid=(w // LANES, bsz),
        in_specs=[pl.BlockSpec((n_chunk, t, LANES), lambda j, b: (b, 0, j)),
                  table(lag_t), table(st_t), table(cr_t), table(scan)],
        out_specs=pl.BlockSpec((n_chunk, t, LANES), lambda j, b: (b, 0, j)),
        out_shape=jax.ShapeDtypeStruct(u3.shape, F32),
        scratch_shapes=[pltpu.VMEM((t * LANES, t * LANES), BF16), pltpu.VMEM((t * LANES, kp2), BF16),
                        pltpu.VMEM((kp2, t * LANES), BF16)],
        compiler_params=_params("arbitrary", "arbitrary"),
        name="s5_scan",
    )(u3, lag_t, st_t, cr_t, scan)


def _s5_post_kernel(y_ref, u_ref, d_ref, w_ref, b_ref, o_ref):
    u = u_ref[...]
    y = _gelu(y_ref[...].reshape(u.shape) + d_ref[...] * u)
    o_ref[...] = (y * _sigmoid(_dot(y.astype(BF16), w_ref[...]) + b_ref[...])).astype(o_ref.dtype)


def s5_post(y3, h, d_skip, w_glu, b_glu, tm=1024):
    rows, t, w = y3.shape
    n = rows * t
    tm = min(tm, n)
    row = lambda i: (i, 0)
    fixed = lambda i: (0, 0)
    return pl.pallas_call(
        _s5_post_kernel,
        grid=(n // tm,),
        in_specs=[pl.BlockSpec((tm // t, t, w), lambda i: (i, 0, 0)), pl.BlockSpec((tm, w), row),
                  pl.BlockSpec((1, w), fixed), pl.BlockSpec((w, w), fixed), pl.BlockSpec((1, w), fixed)],
        out_specs=pl.BlockSpec((tm, w), row),
        out_shape=jax.ShapeDtypeStruct((n, w), BF16),
        compiler_params=_params("parallel"),
        name="s5_post",
    )(y3, h, d_skip.reshape(1, w), w_glu.astype(BF16), b_glu.reshape(1, w))


def s5_mixer(h, u3, bsz, seq, lam_re, lam_im, log_dt, b_re, b_im, c_re, c_im, d_skip, w_glu, b_glu):
    tables = _s5_tables(lam_re, lam_im, log_dt, b_re, b_im, c_re, c_im, seq // S5_CHUNK)
    return s5_post(s5_scan(u3, bsz, *tables), h, d_skip, w_glu, b_glu)


def _rope_tables(pos, rot_dim, theta, head_dim, n_heads):
    half = rot_dim // 2
    f32 = np.float32
    inv_freq = f32(theta) ** (-np.arange(half, dtype=f32) / f32(half))
    ang = (pos.astype(f32)[:, None] * inv_freq[None, :]).astype(np.float64)
    cos, sin = np.cos(ang), np.sin(ang)
    rest = head_dim - rot_dim
    n = pos.shape[0]
    c = np.concatenate([cos, cos, np.ones((n, rest))], axis=1)
    s_up = np.concatenate([-sin, np.zeros((n, half + rest))], axis=1)
    s_dn = np.concatenate([np.zeros((n, half)), sin, np.zeros((n, rest))], axis=1)
    tile = lambda a: jnp.asarray(np.tile(a, (1, n_heads)), F32)
    return tile(c), tile(s_up), tile(s_dn)


def _rope_apply(x, c, s_up, s_dn, half):
    return x * c + pltpu.roll(x, LANES - half, 1) * s_up + pltpu.roll(x, half, 1) * s_dn


def _retention_tables():
    c = RET_CHUNK
    log_gamma = np.log(1.0 - 2.0 ** (-5.0 - np.arange(RET_HEADS, dtype=np.float64)))
    i = np.arange(c, dtype=np.float64)
    diff = i[:, None] - i[None, :]
    decay = np.where(diff >= 0, np.exp(diff[None] * log_gamma[:, None, None]), 0.0)
    qdec = np.repeat(np.exp((i + 1.0)[:, None] * log_gamma[None, :]), RET_DK, axis=1)
    kdec = np.repeat(np.exp((c - 1.0 - i)[:, None] * log_gamma[None, :]), RET_DK, axis=1)
    chunk_decay = [float(v) for v in np.exp(c * log_gamma)]
    return jnp.asarray(decay, F32), jnp.asarray(qdec, F32), jnp.asarray(kdec, F32), chunk_decay


def _retention_kernel(chunk_decay, q_ref, k_ref, v0_ref, v1_ref, g0_ref, g1_ref, c_ref, su_ref, sd_ref,
                      dec_ref, qdec_ref, kdec_ref, lng_ref, lnb_ref, o_ref, state_ref):
    @pl.when(pl.program_id(1) == 0)
    def _():
        state_ref[...] = jnp.zeros_like(state_ref)

    half = RET_DK // 2
    tabs = (c_ref[...], su_ref[...], sd_ref[...])
    q = jnp.concatenate([_rope_apply(q_ref[:, s:s + LANES], *tabs, half) for s in (0, LANES)], axis=1)
    k = jnp.concatenate([_rope_apply(k_ref[:, s:s + LANES], *tabs, half) for s in (0, LANES)], axis=1)
    k = k * (RET_DK ** -0.5)
    q_dec = q * qdec_ref[...]
    k_dec = k * kdec_ref[...]
    v = jnp.concatenate([v0_ref[...], v1_ref[...]], axis=1)
    gate = jnp.concatenate([g0_ref[...], g1_ref[...]], axis=1)
    outs = []
    for h in range(RET_HEADS):
        ks = slice(h * RET_DK, (h + 1) * RET_DK)
        vs = slice(h * RET_DV, (h + 1) * RET_DV)
        vh = v[:, vs].astype(BF16)
        scores = _dot_nt(q[:, ks].astype(BF16), k[:, ks].astype(BF16)) * dec_ref[h]
        y = _dot(scores.astype(BF16), vh) + _dot(q_dec[:, ks].astype(BF16), state_ref[h].astype(BF16))
        state_ref[h] = state_ref[h] * chunk_decay[h] + _dot_tn(k_dec[:, ks].astype(BF16), vh)
        mu = jnp.mean(y, axis=-1, keepdims=True)
        yc = y - mu
        var = jnp.mean(yc * yc, axis=-1, keepdims=True)
        outs.append(yc * lax.rsqrt(var + RET_GN_EPS))
    yn = jnp.concatenate(outs, axis=1) * lng_ref[...] + lnb_ref[...]
    o_ref[...] = (gate * _sigmoid(gate) * yn).astype(o_ref.dtype)


def retention_mixer(h, bsz, seq, col0, ln_g, ln_b):
    c = RET_CHUNK
    n_chunk = seq // c
    qk_w = RET_HEADS * RET_DK
    v_w = RET_HEADS * RET_DV
    assert col0 % qk_w == 0 and qk_w == 2 * LANES and v_w == 2 * qk_w
    cb = col0 // qk_w
    rc, rsu, rsd = _rope_tables(np.arange(seq), RET_DK, RET_THETA, RET_DK, 2)
    dec, qdec, kdec, chunk_decay = _retention_tables()
    row = lambda j: (lambda b, n: (b * n_chunk + n, j))
    pos = lambda b, n: (n, 0)
    fixed2 = lambda b, n: (0, 0)
    kern = functools.partial(_retention_kernel, chunk_decay)
    return pl.pallas_call(
        kern,
        grid=(bsz, n_chunk),
        in_specs=[pl.BlockSpec((c, qk_w), row(cb)), pl.BlockSpec((c, qk_w), row(cb + 1)),
                  pl.BlockSpec((c, qk_w), row(cb + 2)), pl.BlockSpec((c, qk_w), row(cb + 3)),
                  pl.BlockSpec((c, qk_w), row(cb + 4)), pl.BlockSpec((c, qk_w), row(cb + 5)),
                  pl.BlockSpec((c, LANES), pos), pl.BlockSpec((c, LANES), pos), pl.BlockSpec((c, LANES), pos),
                  pl.BlockSpec((RET_HEADS, c, c), lambda b, n: (0, 0, 0)),
                  pl.BlockSpec((c, qk_w), fixed2), pl.BlockSpec((c, qk_w), fixed2),
                  pl.BlockSpec((1, v_w), fixed2), pl.BlockSpec((1, v_w), fixed2)],
        out_specs=pl.BlockSpec((c, v_w), lambda b, n: (b * n_chunk + n, 0)),
        out_shape=jax.ShapeDtypeStruct((bsz * seq, v_w), BF16),
        scratch_shapes=[pltpu.VMEM((RET_HEADS, RET_DK, RET_DV), F32)],
        compiler_params=_params("parallel", "arbitrary"),
        name="retention",
    )(h, h, h, h, h, h, rc, rsu, rsd, dec, qdec, kdec, ln_g.reshape(1, v_w), ln_b.reshape(1, v_w))


NSA_KV_W = NSA_KV_GROUPS * NSA_HEAD_DIM
NSA_GATE_COLS = 3 * NSA_HEADS


def _nsa_prep_kernel(q_ref, kvc_ref, kvs_ref, kvw_ref, c_ref, su_ref, sd_ref,
                     qo_ref, kc_ref, vc_ref, ks_ref, vs_ref, kw_ref, vw_ref):
    half = NSA_ROT_DIM // 2
    tabs = (c_ref[...], su_ref[...], sd_ref[...])
    scale = NSA_HEAD_DIM ** -0.5 * math.log2(math.e)
    qo_ref[...] = jnp.concatenate(
        [_rope_apply(q_ref[:, s:s + LANES], *tabs, half) * scale for s in range(0, NSA_WIDTH, LANES)],
        axis=1).astype(qo_ref.dtype)

    def split(x, o_ref):
        for g in range(NSA_KV_GROUPS):
            o_ref[0, g] = x[:, g * NSA_HEAD_DIM:(g + 1) * NSA_HEAD_DIM].astype(o_ref.dtype)

    split(kvc_ref[:, :NSA_KV_W], kc_ref)
    split(kvc_ref[:, NSA_KV_W:], vc_ref)
    split(_rope_apply(kvs_ref[:, :NSA_KV_W], *tabs, half), ks_ref)
    split(kvs_ref[:, NSA_KV_W:], vs_ref)
    split(_rope_apply(kvw_ref[:, :NSA_KV_W], *tabs, half), kw_ref)
    split(kvw_ref[:, NSA_KV_W:], vw_ref)


def nsa_prep(h, bsz, seq, tl=512):
    tl = min(tl, seq)
    nl = seq // tl
    rc, rsu, rsd = _rope_tables(np.arange(seq), NSA_ROT_DIM, ROPE_THETA, NSA_HEAD_DIM, LANES // NSA_HEAD_DIM)
    row = lambda j: (lambda b, l: (b * nl + l, j))
    pos = lambda b, l: (l, 0)
    kv_out = pl.BlockSpec((1, NSA_KV_GROUPS, tl, NSA_HEAD_DIM), lambda b, l: (b, 0, l, 0))
    kv_shape = lambda dt: jax.ShapeDtypeStruct((bsz, NSA_KV_GROUPS, seq, NSA_HEAD_DIM), dt)
    two = 2 * NSA_KV_W
    return pl.pallas_call(
        _nsa_prep_kernel,
        grid=(bsz, nl),
        in_specs=[pl.BlockSpec((tl, NSA_WIDTH), row(1)),
                  pl.BlockSpec((tl, two), row(4)), pl.BlockSpec((tl, two), row(5)), pl.BlockSpec((tl, two), row(6)),
                  pl.BlockSpec((tl, LANES), pos), pl.BlockSpec((tl, LANES), pos), pl.BlockSpec((tl, LANES), pos)],
        out_specs=[pl.BlockSpec((tl, NSA_WIDTH), lambda b, l: (b * nl + l, 0)),
                   kv_out, kv_out, kv_out, kv_out, kv_out, kv_out],
        out_shape=[jax.ShapeDtypeStruct((bsz * seq, NSA_WIDTH), BF16),
                   kv_shape(F32), kv_shape(F32), kv_shape(BF16), kv_shape(BF16), kv_shape(BF16), kv_shape(BF16)],
        compiler_params=_params("parallel", "parallel"),
        name="nsa_prep",
    )(h, h, h, h, rc, rsu, rsd)


def _nsa_compress_kernel(hk_ref, hv_ref, pek_ref, pev_ref, kw1_ref, kb1_ref, kw2_ref, vw1_ref, vb1_ref, vw2_ref,
                         c_ref, su_ref, sd_ref, ko_ref, vo_ref):
    def mlp(h_ref, pe_ref, w1_ref, b1_ref, w2_ref):
        hb = h_ref[0, 0]
        rows = hb.shape[0]
        first = _dot((hb + pe_ref[0:1, :]).astype(BF16), w1_ref[0])
        second = _dot((hb + pe_ref[1:2, :]).astype(BF16), w1_ref[1])
        hid = _gelu(first + pltpu.roll(second, rows - 1, 0) + b1_ref[...])
        return _dot(hid.astype(BF16), w2_ref[...])

    kc = _rope_apply(mlp(hk_ref, pek_ref, kw1_ref, kb1_ref, kw2_ref), c_ref[...], su_ref[...], sd_ref[...],
                     NSA_ROT_DIM // 2)
    vc = mlp(hv_ref, pev_ref, vw1_ref, vb1_ref, vw2_ref)
    ko_ref[0, 0] = kc[:, :NSA_HEAD_DIM].astype(ko_ref.dtype)
    vo_ref[0, 0] = vc[:, :NSA_HEAD_DIM].astype(vo_ref.dtype)


def nsa_compress(kc, vc, pe_k, pe_v, ck_w1, ck_b1, ck_w2, cv_w1, cv_b1, cv_w2):
    bsz, grp, seq, d = kc.shape
    n_rows = seq // CMP_STRIDE
    flat = CMP_STRIDE * d
    cmp_end = np.arange(n_rows) * CMP_STRIDE + CMP_BLOCK - 1
    rc, rsu, rsd = _rope_tables(cmp_end, NSA_ROT_DIM, ROPE_THETA, NSA_HEAD_DIM, LANES // NSA_HEAD_DIM)
    pad_w2 = lambda w: jnp.pad(w, ((0, 0), (0, LANES - d))).astype(BF16)
    blk = pl.BlockSpec((1, 1, n_rows, flat), lambda b, g: (b, g, 0, 0))
    f2 = lambda b, g: (0, 0)
    f3 = lambda b, g: (0, 0, 0)
    w_specs = [pl.BlockSpec((2, flat, CMP_HIDDEN), f3), pl.BlockSpec((1, CMP_HIDDEN), f2),
               pl.BlockSpec((CMP_HIDDEN, LANES), f2)]
    out_spec = pl.BlockSpec((1, 1, n_rows, d), lambda b, g: (b, g, 0, 0))
    out_shape = jax.ShapeDtypeStruct((bsz, grp, n_rows, d), BF16)
    return pl.pallas_call(
        _nsa_compress_kernel,
        grid=(bsz, grp),
        in_specs=[blk, blk, pl.BlockSpec((2, flat), f2), pl.BlockSpec((2, flat), f2)] + w_specs + w_specs
                 + [pl.BlockSpec((n_rows, LANES), f2)] * 3,
        out_specs=[out_spec, out_spec],
        out_shape=[out_shape, out_shape],
        compiler_params=_params("parallel", "parallel"),
        name="nsa_compress",
    )(kc.reshape(bsz, grp, n_rows, flat), vc.reshape(bsz, grp, n_rows, flat),
      pe_k.reshape(2, flat), pe_v.reshape(2, flat),
      ck_w1.reshape(2, flat, CMP_HIDDEN).astype(BF16), ck_b1.reshape(1, CMP_HIDDEN), pad_w2(ck_w2),
      cv_w1.reshape(2, flat, CMP_HIDDEN).astype(BF16), cv_b1.reshape(1, CMP_HIDDEN), pad_w2(cv_w2),
      rc, rsu, rsd)


def _bias_rows(bias):
    return jnp.concatenate([bias] * NSA_HPG, axis=0)


def _nsa_attn_kernel(seq, tk, q_ref, gate_ref, gexp_ref, kc_ref, vc_ref, ks_ref, vs_ref, kw_ref, vw_ref,
                     mmap_ref, expand_ref, o_ref):
    n_blk = seq // SLC_BLOCK
    n_sel = min(N_SLC, n_blk)
    hd = NSA_HEAD_DIM
    w = NSA_HPG * hd
    groups = range(NSA_KV_GROUPS)
    q0 = pl.program_id(1) * Q_BLOCK
    rows = NSA_HPG * Q_BLOCK
    t_q = q0 + lax.broadcasted_iota(jnp.int32, (Q_BLOCK, 1), 0)
    t_l = q0 + lax.broadcasted_iota(jnp.int32, (1, Q_BLOCK), 1)

    def select(g):
        q = q_ref[:, g * w:(g + 1) * w]
        qs = jnp.concatenate([q[:, h * hd:(h + 1) * hd] for h in range(NSA_HPG)], axis=0)
        kc = kc_ref[0, g]
        n_cmp = kc.shape[0]
        cmp_end = lax.broadcasted_iota(jnp.int32, (1, n_cmp), 1) * CMP_STRIDE + (CMP_BLOCK - 1)
        s = _dot_nt(qs, kc) + _bias_rows(jnp.where(cmp_end <= t_q, 0.0, MASK_VALUE))
        yield
        p = jnp.exp2(s - jnp.max(s, axis=-1, keepdims=True))
        any_key = _bias_rows(jnp.where(t_q >= CMP_BLOCK - 1, 1.0, 0.0))
        inv_l = any_key / jnp.sum(p, axis=-1, keepdims=True)
        o_cmp = _dot(p.astype(BF16), vc_ref[0, g]) * inv_l
        yield
        p = p * inv_l
        imp = p[0:Q_BLOCK]
        for h in range(1, NSA_HPG):
            imp = imp + p[h * Q_BLOCK:(h + 1) * Q_BLOCK]
        imp_t = _dot_split(imp, mmap_ref[...], 'a', 3).T
        yield
        blk = lax.broadcasted_iota(jnp.int32, (n_blk, 1), 0)
        cur = t_l // SLC_BLOCK
        score = jnp.where(blk == 0, FORCE_SCORE,
                          jnp.where(blk == cur, FORCE_SCORE, jnp.where(blk == cur - 1, FORCE_SCORE, imp_t)))
        score = jnp.where(blk * SLC_BLOCK <= t_l, score, -FORCE_SCORE)
        sel_t = jnp.zeros((n_blk, Q_BLOCK), F32)
        for _ in range(n_sel):
            best = jnp.max(score, axis=0, keepdims=True)
            idx = jnp.min(jnp.where(score == best, blk, n_blk), axis=0, keepdims=True)
            pick = blk == idx
            sel_t = jnp.where(pick, 1.0, sel_t)
            score = jnp.where(pick, -jnp.inf, score)
            yield
        return qs, o_cmp, sel_t.T.astype(BF16)

    selected = _run_interleaved([select(g) for g in groups])
    qs = [r[0] for r in selected]
    o_cmp = [r[1] for r in selected]
    sel_b = [r[2] for r in selected]

    def slc_tile(kt, carry):
        k0 = pl.multiple_of(kt * tk, tk)
        kpos = k0 + lax.broadcasted_iota(jnp.int32, (1, tk), 1)
        expand = expand_ref[:, pl.ds(k0, tk)]
        chosen = [_dot(sel_b[g], expand) for g in groups]
        scores = [_dot_nt(qs[g], ks_ref[0, g, pl.ds(k0, tk), :]) for g in groups]
        out = []
        for g in groups:
            m, l, acc = carry[g]
            bias = jnp.where(kpos <= t_q, jnp.where(chosen[g] > 0.5, 0.0, MASK_VALUE), MASK_VALUE)
            s = scores[g] + _bias_rows(bias)
            m_new = jnp.maximum(m, jnp.max(s, axis=-1, keepdims=True))
            alpha = jnp.exp2(m - m_new)
            p = jnp.exp2(s - m_new)
            l = alpha * l + jnp.sum(p, axis=-1, keepdims=True)
            acc = alpha * acc + _dot(p.astype(BF16), vs_ref[0, g, pl.ds(k0, tk), :])
            out.append((m_new, l, acc))
        return tuple(out)

    n_tiles = (q0 + Q_BLOCK + tk - 1) // tk
    init = tuple((jnp.full((rows, 1), MASK_VALUE, F32), jnp.zeros((rows, 1), F32), jnp.zeros((rows, hd), F32))
                 for _ in groups)
    slc = lax.fori_loop(0, n_tiles, slc_tile, init)

    band = WINDOW + Q_BLOCK
    w0 = pl.multiple_of(jnp.maximum(q0 - WINDOW, 0), Q_BLOCK)
    kpos = w0 + lax.broadcasted_iota(jnp.int32, (1, band), 1)
    win_bias = _bias_rows(jnp.where(kpos <= t_q, jnp.where(kpos > t_q - WINDOW, 0.0, MASK_VALUE), MASK_VALUE))
    sig = _sigmoid(gate_ref[...])

    def head_cols(o):
        return jnp.concatenate([o[h * Q_BLOCK:(h + 1) * Q_BLOCK] for h in range(NSA_HPG)], axis=1)

    def finish(g):
        s = _dot_nt(qs[g], kw_ref[0, g, pl.ds(w0, band), :]) + win_bias
        yield
        p = jnp.exp2(s - jnp.max(s, axis=-1, keepdims=True))
        o_win = _dot(p.astype(BF16), vw_ref[0, g, pl.ds(w0, band), :]) / jnp.sum(p, axis=-1, keepdims=True)
        yield
        gates = _dot_split(sig, gexp_ref[g], 'a', 2)
        yield
        _, l_slc, acc_slc = slc[g]
        out = (gates[:, 0:w] * head_cols(o_cmp[g]) + gates[:, w:2 * w] * head_cols(acc_slc / l_slc)
               + gates[:, 2 * w:3 * w] * head_cols(o_win))
        o_ref[:, g * w:(g + 1) * w] = out.astype(o_ref.dtype)

    _run_interleaved([finish(g) for g in groups])


def _nsa_constants(seq):
    n_blk = seq // SLC_BLOCK
    n_rows = seq // CMP_STRIDE
    per_stride = SLC_BLOCK // CMP_STRIDE
    span = CMP_BLOCK // CMP_STRIDE
    mmap = np.zeros((n_rows, n_blk), np.float32)
    for j in range(n_blk):
        for m in range(per_stride):
            for n in range(span):
                c = per_stride * j + m + n - (span - 1)
                if 0 <= c < n_rows - 1:
                    mmap[c, j] += 1.0
    w = NSA_HPG * NSA_HEAD_DIM
    gexp = np.zeros((NSA_KV_GROUPS, LANES, 3 * w), np.float32)
    for g in range(NSA_KV_GROUPS):
        for h in range(NSA_HPG):
            for br in range(3):
                gexp[g, (g * NSA_HPG + h) * 3 + br, br * w + h * NSA_HEAD_DIM: br * w + (h + 1) * NSA_HEAD_DIM] = 1.0
    expand = (np.arange(n_blk)[:, None] == (np.arange(seq)[None, :] // SLC_BLOCK)).astype(np.float32)
    return jnp.asarray(mmap, BF16), jnp.asarray(gexp, BF16), jnp.asarray(expand, BF16)


def nsa_attention(qr, h, gate_col_block, k_cmp, v_cmp, ks, vs, kw, vw, bsz, seq, tk=512):
    tk = min(tk, seq)
    nq = seq // Q_BLOCK
    mmap, gexp, expand = _nsa_constants(seq)
    n_rows = k_cmp.shape[2]
    qblk = lambda b, i: (b * nq + i, 0)
    kv = lambda n: pl.BlockSpec((1, NSA_KV_GROUPS, n, NSA_HEAD_DIM), lambda b, i: (b, 0, 0, 0))
    kern = functools.partial(_nsa_attn_kernel, seq, tk)
    return pl.pallas_call(
        kern,
        grid=(bsz, nq),
        in_specs=[pl.BlockSpec((Q_BLOCK, NSA_WIDTH), qblk),
                  pl.BlockSpec((Q_BLOCK, LANES), lambda b, i: (b * nq + i, gate_col_block)),
                  pl.BlockSpec(gexp.shape, lambda b, i: (0, 0, 0)),
                  kv(n_rows), kv(n_rows), kv(seq), kv(seq), kv(seq), kv(seq),
                  pl.BlockSpec(mmap.shape, lambda b, i: (0, 0)),
                  pl.BlockSpec(expand.shape, lambda b, i: (0, 0))],
        out_specs=pl.BlockSpec((Q_BLOCK, NSA_WIDTH), qblk),
        out_shape=jax.ShapeDtypeStruct((bsz * seq, NSA_WIDTH), BF16),
        compiler_params=_params("parallel", "arbitrary"),
        name="nsa_attention",
    )(qr, h, gexp, k_cmp, v_cmp, ks, vs, kw, vw, mmap, expand)


def nsa_mixer(h, bsz, seq, gate_col_block, pe_k, pe_v, ck_w1, ck_b1, ck_w2, cv_w1, cv_b1, cv_w2):
    qr, kc, vc, ks, vs, kw, vw = nsa_prep(h, bsz, seq)
    k_cmp, v_cmp = nsa_compress(kc, vc, pe_k, pe_v, ck_w1, ck_b1, ck_w2, cv_w1, cv_b1, cv_w2)
    return nsa_attention(qr, h, gate_col_block, k_cmp, v_cmp, ks, vs, kw, vw, bsz, seq)


def _head_ones(width, head_dim):
    idx = np.arange(width) // head_dim
    return jnp.asarray(idx[:, None] == idx[None, :], BF16)


def _softplus(x):
    return jnp.maximum(x, 0.0) + jnp.log(1.0 + jnp.exp(-jnp.abs(x)))


def _rwkv_pre_kernel(p_ref, prev_ref, mu_ref, w0_ref, wup_ref, a0_ref, aup_ref, gup_ref, kk_ref, ka_ref, rk_ref,
                     ones_ref, r_o, k_o, v_o, kk_o, b_o, ld_o, g_o, bonus_o):
    w = RWKV_WIDTH
    p = p_ref[...]
    first_row = jnp.where(pl.program_id(1) == 0, 0.0, prev_ref[7:8, :])
    is_row0 = lax.broadcasted_iota(jnp.int32, p.shape, 0) == 0
    prev = jnp.where(is_row0, first_row, pltpu.roll(p, 1, 0))
    ps = p + (prev - p) * mu_ref[...]
    r, k, v = ps[:, 0:w], ps[:, w:2 * w], ps[:, 2 * w:3 * w]
    o = 3 * w
    w_lo = ps[:, o:o + RWKV_LORA_W]
    a_lo = ps[:, o + RWKV_LORA_W:o + RWKV_LORA_W + RWKV_LORA_A]
    g_lo = ps[:, o + RWKV_LORA_W + RWKV_LORA_A:]
    wlog = -_softplus(-(w0_ref[...] + _dot(jnp.tanh(w_lo).astype(BF16), wup_ref[...]))) - 0.5
    a = _sigmoid(a0_ref[...] + _dot(a_lo.astype(BF16), aup_ref[...]))
    g = _dot(_sigmoid(g_lo).astype(BF16), gup_ref[...])
    kk = k * kk_ref[...]
    norm = jnp.sqrt(_dot_split(kk * kk, ones_ref[...], 'a', 2))
    kk = kk / jnp.maximum(norm, 1e-12)
    k2 = k * (1.0 + (a - 1.0) * ka_ref[...])
    r_o[...] = r
    k_o[...] = k2
    v_o[...] = v
    kk_o[...] = kk
    b_o[...] = kk * a
    ld_o[...] = -jnp.exp(wlog)
    g_o[...] = g
    bonus_o[...] = _dot_split(r * k2 * rk_ref[...], ones_ref[...], 'a', 2) * v


def rwkv_pre(h, bsz, seq, mu, w0, w_up, a0, a_up, g_up, k_k, k_a, r_k, tl=512):
    tl = min(tl, seq)
    nl = seq // tl
    w = RWKV_WIDTH
    cols = RWKV_COLS
    ones = _head_ones(w, RWKV_HEAD_DIM)
    f2 = lambda b, l: (0, 0)
    vec = pl.BlockSpec((1, w), f2)
    out_spec = pl.BlockSpec((tl, w), lambda b, l: (b * nl + l, 0))
    out_shape = jax.ShapeDtypeStruct((bsz * seq, w), F32)
    return pl.pallas_call(
        _rwkv_pre_kernel,
        grid=(bsz, nl),
        in_specs=[pl.BlockSpec((tl, cols), lambda b, l: (b * nl + l, 0)),
                  pl.BlockSpec((8, cols), lambda b, l: (jnp.maximum((b * seq + l * tl) // 8 - 1, 0), 0)),
                  pl.BlockSpec((1, cols), f2), vec, pl.BlockSpec((RWKV_LORA_W, w), f2),
                  vec, pl.BlockSpec((RWKV_LORA_A, w), f2), pl.BlockSpec((RWKV_LORA_G, w), f2),
                  vec, vec, vec, pl.BlockSpec((w, w), f2)],
        out_specs=[out_spec] * 8,
        out_shape=[out_shape] * 8,
        compiler_params=_params("parallel", "parallel"),
        name="rwkv_pre",
    )(h, h, mu.reshape(1, cols), w0.reshape(1, w), w_up.astype(BF16), a0.reshape(1, w), a_up.astype(BF16),
      g_up.astype(BF16), k_k.reshape(1, w), k_a.reshape(1, w), r_k.reshape(1, w), ones)


def _rwkv_masks():
    t, pk = RWKV_CHUNK, RWKV_PACK
    n = t * pk
    ri = np.arange(n)
    same = (ri[:, None] // t) == (ri[None, :] // t)
    tt, ss = ri[:, None] % t, ri[None, :] % t
    levels = []
    k = 1
    while k < t:
        levels.append(same & (tt // (2 * k) == ss // (2 * k)) & ((tt // k) % 2 == 1) & ((ss // k) % 2 == 0))
        k *= 2
    lvl = np.stack(levels).astype(np.float32)
    tri = (np.arange(t)[:, None] >= np.arange(t)[None, :]).astype(np.float32)
    head_lane = ((ri[:, None] // t) == (np.arange(pk * RWKV_HEAD_DIM)[None, :] // RWKV_HEAD_DIM)).astype(np.float32)
    return (jnp.asarray(tri, BF16), jnp.asarray(head_lane), jnp.asarray(same.astype(np.float32)), jnp.asarray(lvl))


def _rwkv_chain(r, k, v, kk, b, ld, st, tri, hl, bd, lvl_ref):
    t, pk = RWKV_CHUNK, RWKV_PACK
    n = t * pk
    c = _dot_split(tri, ld, 'b', 3)
    yield
    c_end = c[t - 1:t, :]
    e_neg = jnp.exp(-c)
    e_end = jnp.exp(c_end - c)
    kkd = (kk * jnp.exp(c - ld)).astype(BF16)
    rd = (r * jnp.exp(c)).astype(BF16)

    def big(x):
        return (jnp.concatenate([x] * pk, axis=0) * hl).astype(BF16)

    st_b = st.astype(BF16)
    v_big = big(v)
    a_all = _dot_nt(jnp.concatenate([kkd, rd], axis=0),
                    jnp.concatenate([big(k * e_neg), big(b * e_neg)], axis=0))
    yield
    ti = lax.broadcasted_iota(jnp.int32, (t, n), 0)
    si = lax.broadcasted_iota(jnp.int32, (t, n), 1) % t
    strict = ti > si
    incl = ti >= si
    a_kk = jnp.where(strict, a_all[:t, :n], 0.0)
    a_kb = jnp.where(strict, a_all[:t, n:], 0.0)
    a_rk = jnp.where(incl, a_all[t:, :n], 0.0)
    a_rb = jnp.where(incl, a_all[t:, n:], 0.0)
    rhs = _dot(kkd, st_b) + _dot(a_kk.astype(BF16), v_big)
    yield
    a_bd = jnp.concatenate([a_kb] * pk, axis=0) * bd
    m = jnp.where(lax.broadcasted_iota(jnp.int32, (n, n), 0) == lax.broadcasted_iota(jnp.int32, (n, n), 1), 1.0, 0.0)
    for lv in range(lvl_ref.shape[0]):
        mb = m.astype(BF16)
        ma = _dot(mb, (a_bd * lvl_ref[lv]).astype(BF16)).astype(BF16)
        yield
        m = m - _dot(ma, mb)
        yield
    u_big = _dot(m.astype(BF16), big(rhs))
    yield
    u = u_big[0:t]
    for h in range(1, pk):
        u = u + u_big[h * t:(h + 1) * t]
    y = _dot(rd, st_b) + _dot(a_rk.astype(BF16), v_big) - _dot(a_rb.astype(BF16), big(u))
    yield
    decay_col = jnp.broadcast_to(jnp.exp(c_end), st.shape).T
    kb_end = jnp.concatenate([k * e_end, -(b * e_end)], axis=0).astype(BF16)
    vu = jnp.concatenate([v, u], axis=0).astype(BF16)
    return y, decay_col * st + bd * _dot_tn(kb_end, vu)


def _rwkv_chunk_kernel(r_ref, k_ref, v_ref, kk_ref, b_ref, ld_ref, tri_ref, hl_ref, bd_ref, lvl_ref, y_ref, st_ref):
    @pl.when(pl.program_id(0) == 0)
    def _():
        st_ref[...] = jnp.zeros_like(st_ref)

    wp = RWKV_PACK * RWKV_HEAD_DIM
    tri, hl, bd = tri_ref[...], hl_ref[...], bd_ref[...]
    n_pack = r_ref.shape[2] // wp
    where = [(bi, slice(g * wp, (g + 1) * wp)) for bi in range(r_ref.shape[0]) for g in range(n_pack)]
    loaded = [tuple(ref[bi, :, cols] for ref in (r_ref, k_ref, v_ref, kk_ref, b_ref, ld_ref)) + (st_ref[i],)
              for i, (bi, cols) in enumerate(where)]
    results = _run_interleaved([_rwkv_chain(*args, tri, hl, bd, lvl_ref) for args in loaded])
    for i, ((bi, cols), (y, st_new)) in enumerate(zip(where, results)):
        y_ref[bi, :, cols] = y
        st_ref[i] = st_new


def rwkv_chunk(r, k, v, kk, b, ld, bsz, seq):
    t, pk = RWKV_CHUNK, RWKV_PACK
    n_chunk = seq // t
    w = RWKV_WIDTH
    wp = pk * RWKV_HEAD_DIM
    assert t == RWKV_HEAD_DIM
    tri, hl, bd, lvl = _rwkv_masks()
    blk = pl.BlockSpec((bsz, t, w), lambda c: (0, c, 0))
    f2 = lambda c: (0, 0)
    shaped = lambda a: a.reshape(bsz, seq, w)
    y = pl.pallas_call(
        _rwkv_chunk_kernel,
        grid=(n_chunk,),
        in_specs=[blk] * 6 + [pl.BlockSpec(tri.shape, f2), pl.BlockSpec(hl.shape, f2), pl.BlockSpec(bd.shape, f2),
                              pl.BlockSpec(lvl.shape, lambda c: (0, 0, 0))],
        out_specs=blk,
        out_shape=jax.ShapeDtypeStruct((bsz, seq, w), F32),
        scratch_shapes=[pltpu.VMEM((bsz * (w // wp), wp, wp), F32)],
        compiler_params=_params("arbitrary"),
        name="rwkv_chunk",
    )(shaped(r), shaped(k), shaped(v), shaped(kk), shaped(b), shaped(ld), tri, hl, bd, lvl)
    return y.reshape(bsz * seq, w)


def _rwkv_post_kernel(y_ref, bonus_ref, g_ref, lng_ref, lnb_ref, ones_ref, o_ref):
    y = y_ref[...]
    inv = 1.0 / RWKV_HEAD_DIM
    mu = _dot_split(y, ones_ref[...], 'a', 2) * inv
    yc = y - mu
    var = _dot_split(yc * yc, ones_ref[...], 'a', 2) * inv
    yn = yc * lax.rsqrt(var + RWKV_GN_EPS) * lng_ref[...] + lnb_ref[...]
    o_ref[...] = ((yn + bonus_ref[...]) * g_ref[...]).astype(o_ref.dtype)


def rwkv_post(y, bonus, g, ln_g, ln_b, tm=1024):
    n, w = y.shape
    tm = min(tm, n)
    row = pl.BlockSpec((tm, w), lambda i: (i, 0))
    vec = pl.BlockSpec((1, w), lambda i: (0, 0))
    return pl.pallas_call(
        _rwkv_post_kernel,
        grid=(n // tm,),
        in_specs=[row, row, row, vec, vec, pl.BlockSpec((w, w), lambda i: (0, 0))],
        out_specs=row,
        out_shape=jax.ShapeDtypeStruct((n, w), BF16),
        compiler_params=_params("parallel"),
        name="rwkv_post",
    )(y, bonus, g, ln_g.reshape(1, w), ln_b.reshape(1, w), _head_ones(w, RWKV_HEAD_DIM))


def rwkv7_mixer(h, bsz, seq, mu, w0, w_up, a0, a_up, g_up, k_k, k_a, r_k, ln_g, ln_b):
    r, k, v, kk, b, ld, g, bonus = rwkv_pre(h, bsz, seq, mu, w0, w_up, a0, a_up, g_up, k_k, k_a, r_k)
    y = rwkv_chunk(r, k, v, kk, b, ld, bsz, seq)
    return rwkv_post(y, bonus, g, ln_g, ln_b)


AB_IN = S5_WIDTH + NSA_WIDTH + 6 * NSA_KV_W + NSA_GATE_COLS
AB_IN_PADDED = -(-AB_IN // LANES) * LANES
NSA_GATE_COL_BLOCK = (AB_IN - NSA_GATE_COLS) // LANES
PROJ_TM = 512


def kernel(x, ab_w_in, ab_w_out, s5_lam_re, s5_lam_im, s5_log_dt, s5_b_re, s5_b_im, s5_c_re, s5_c_im, s5_d, s5_w_glu, s5_b_glu, nsa_pe_k, nsa_pe_v, nsa_ck_w1, nsa_ck_b1, nsa_ck_w2, nsa_cv_w1, nsa_cv_b1, nsa_cv_w2, cd_w_in, cd_w_out, rwkv_mu, rwkv_w0, rwkv_w_up, rwkv_a0, rwkv_a_up, rwkv_g_up, rwkv_k_k, rwkv_k_a, rwkv_r_k, rwkv_ln_g, rwkv_ln_b, ret_ln_g, ret_ln_b, ln1_g, ln1_b, ln2_g, ln2_b, moe_router, moe_bias, moe_w1, moe_w3, moe_w2, sh_w1, sh_w3, sh_w2):
    bsz, seq, d = x.shape
    assert (AB_IN - NSA_GATE_COLS) % LANES == 0
    xf = x.reshape(bsz * seq, d)
    x_in = xf
    for layer in range(DEPTH):
        i = layer // 2
        if layer % 2 == 0:
            w_in = jnp.pad(ab_w_in[i], ((0, 0), (0, AB_IN_PADDED - AB_IN))).astype(BF16)
            h, u3 = project(x_in, w_in, PROJ_TM, chunked=(S5_CHUNK, S5_WIDTH))
            y_1 = s5_mixer(h, u3, bsz, seq, s5_lam_re[i], s5_lam_im[i], s5_log_dt[i], s5_b_re[i], s5_b_im[i],
                           s5_c_re[i], s5_c_im[i], s5_d[i], s5_w_glu[i], s5_b_glu[i])
            y_2 = nsa_mixer(h, bsz, seq, NSA_GATE_COL_BLOCK, nsa_pe_k[i], nsa_pe_v[i], nsa_ck_w1[i], nsa_ck_b1[i],
                            nsa_ck_w2[i], nsa_cv_w1[i], nsa_cv_b1[i], nsa_cv_w2[i])
            w_out = ab_w_out[i]
        else:
            h = project(x_in, cd_w_in[i].astype(BF16), PROJ_TM)
            y_1 = rwkv7_mixer(h, bsz, seq, rwkv_mu[i], rwkv_w0[i], rwkv_w_up[i], rwkv_a0[i], rwkv_a_up[i],
                              rwkv_g_up[i], rwkv_k_k[i], rwkv_k_a[i], rwkv_r_k[i], rwkv_ln_g[i], rwkv_ln_b[i])
            y_2 = retention_mixer(h, bsz, seq, RWKV_COLS, ret_ln_g[i], ret_ln_b[i])
            w_out = cd_w_out[i]
        xf = out_proj_ln(y_1, y_2, w_out, xf, ln1_g[layer], ln1_b[layer])
        xf, x_in = moe_block(xf, moe_router[layer], moe_bias[layer], moe_w1[layer], moe_w3[layer],
                             moe_w2[layer], sh_w1[layer], sh_w3[layer], sh_w2[layer], ln2_g[layer], ln2_b[layer])
    return xf.reshape(bsz, seq, d)
```

```python
import functools
import math

import jax
import jax.numpy as jnp
import numpy as np
from jax import lax
from jax.experimental import pallas as pl
from jax.experimental.pallas import tpu as pltpu

F32 = jnp.float32
BF16 = jnp.bfloat16
HIGHEST = lax.Precision.HIGHEST
FP8 = jnp.float8_e4m3fn
FP8_MAX = 448.0
FP8_TINY = 1e-30

VMEM_LIMIT_BYTES = 52 * 1024 * 1024
LANES = 128

LN_EPS = 1e-5
DEPTH = 2
ALPHA = (2 * DEPTH) ** 0.25

S5_GROUPS, S5_GROUP_CH, S5_STATE = 32, 16, 64
S5_WIDTH = S5_GROUPS * S5_GROUP_CH
S5_CHUNK = 16
S5_PACK = 8
NSA_HEADS, NSA_KV_GROUPS, NSA_HEAD_DIM = 8, 2, 64
NSA_HPG = NSA_HEADS // NSA_KV_GROUPS
NSA_WIDTH = NSA_HEADS * NSA_HEAD_DIM
NSA_ROT_DIM = NSA_HEAD_DIM // 4
ROPE_THETA = 500000.0
CMP_BLOCK, CMP_STRIDE, CMP_HIDDEN = 32, 16, 128
SLC_BLOCK, N_SLC, WINDOW, Q_BLOCK = 64, 16, 512, 128
FORCE_SCORE = 1e6
MASK_VALUE = -1e30
RWKV_HEADS, RWKV_HEAD_DIM = 8, 64
RWKV_WIDTH = RWKV_HEADS * RWKV_HEAD_DIM
RWKV_LORA_W, RWKV_LORA_A, RWKV_LORA_G = 64, 64, 128
RWKV_COLS = 3 * RWKV_WIDTH + RWKV_LORA_W + RWKV_LORA_A + RWKV_LORA_G
RWKV_GN_EPS = 64e-5
RWKV_CHUNK = 64
RWKV_PACK = 4
RET_HEADS, RET_DK, RET_DV, RET_CHUNK = 4, 64, 128, 128
RET_THETA = 10000.0
RET_GN_EPS = 1e-5
N_EXPERTS, TOP_K, EXPERT_FF = 64, 8, 256
N_EXPERT_GROUPS, TOPK_GROUPS = 8, 4
EXPERTS_PER_GROUP = N_EXPERTS // N_EXPERT_GROUPS
ROUTED_SCALE = 2.5
MOE_EXPERTS_PER_STEP = 4


def _params(*sem):
    return pltpu.CompilerParams(dimension_semantics=sem, vmem_limit_bytes=VMEM_LIMIT_BYTES)


def _dot(a, b, **kw):
    return jnp.dot(a, b, preferred_element_type=F32, **kw)


def _dot_nt(a, b, **kw):
    return lax.dot_general(a, b, (((1,), (1,)), ((), ())), preferred_element_type=F32, **kw)


def _dot_tn(a, b, **kw):
    return lax.dot_general(a, b, (((0,), (0,)), ((), ())), preferred_element_type=F32, **kw)


def _dot_split(a, b, split, parts):
    rest = a if split == 'a' else b
    acc = None
    for _ in range(parts):
        piece = rest.astype(BF16)
        term = _dot(piece, b) if split == 'a' else _dot(a, piece)
        acc = term if acc is None else acc + term
        rest = rest - piece.astype(F32)
    return acc


def _run_interleaved(gens):
    results = [None] * len(gens)
    live = list(range(len(gens)))
    while live:
        for i in list(live):
            try:
                next(gens[i])
            except StopIteration as done:
                results[i] = done.value
                live.remove(i)
    return results


def _gelu(x):
    return 0.5 * x * (1.0 + jnp.tanh(math.sqrt(2.0 / math.pi) * (x + 0.044715 * (x * x * x))))


def _sigmoid(x):
    return 1.0 / (1.0 + jnp.exp(-x))


def _layer_norm_rows(z, g, b):
    mu = jnp.mean(z, axis=-1, keepdims=True)
    zc = z - mu
    var = jnp.mean(zc * zc, axis=-1, keepdims=True)
    return zc * lax.rsqrt(var + LN_EPS) * g + b


def _proj_kernel(x_ref, w_ref, o_ref, *chunked_ref):
    y = _dot(x_ref[...].astype(BF16), w_ref[...])
    o_ref[...] = y
    for c_ref in chunked_ref:
        rows, t, w = c_ref.shape
        c_ref[...] = y[:, :w].reshape(rows, t, w)


def project(x, w_bf16, tm, chunked=None):
    m, k = x.shape
    n = w_bf16.shape[1]
    out_specs = [pl.BlockSpec((tm, n), lambda i: (i, 0))]
    out_shape = [jax.ShapeDtypeStruct((m, n), F32)]
    if chunked is not None:
        t, w = chunked
        out_specs.append(pl.BlockSpec((tm // t, t, w), lambda i: (i, 0, 0)))
        out_shape.append(jax.ShapeDtypeStruct((m // t, t, w), F32))
    out = pl.pallas_call(
        _proj_kernel,
        grid=(m // tm,),
        in_specs=[pl.BlockSpec((tm, k), lambda i: (i, 0)), pl.BlockSpec((k, n), lambda i: (0, 0))],
        out_specs=out_specs,
        out_shape=out_shape,
        compiler_params=_params("parallel"),
        name="project",
    )(x, w_bf16)
    return out if chunked is not None else out[0]


def _out_proj_ln_kernel(ya_ref, yb_ref, wa_ref, wb_ref, x_ref, g_ref, b_ref, o_ref):
    mix = _dot(ya_ref[...], wa_ref[...]) + _dot(yb_ref[...], wb_ref[...])
    o_ref[...] = _layer_norm_rows(ALPHA * x_ref[...] + mix, g_ref[...], b_ref[...])


def out_proj_ln(ya, yb, w_out, x, g, b, tm=512):
    n, d = x.shape
    ka, kb = ya.shape[1], yb.shape[1]
    wa = w_out[:ka].astype(BF16)
    wb = w_out[ka:].astype(BF16)
    row = lambda i: (i, 0)
    fixed = lambda i: (0, 0)
    return pl.pallas_call(
        _out_proj_ln_kernel,
        grid=(n // tm,),
        in_specs=[pl.BlockSpec((tm, ka), row), pl.BlockSpec((tm, kb), row),
                  pl.BlockSpec((ka, d), fixed), pl.BlockSpec((kb, d), fixed),
                  pl.BlockSpec((tm, d), row), pl.BlockSpec((1, d), fixed), pl.BlockSpec((1, d), fixed)],
        out_specs=pl.BlockSpec((tm, d), row),
        out_shape=jax.ShapeDtypeStruct((n, d), F32),
        compiler_params=_params("parallel"),
        name="out_proj_ln",
    )(ya, yb, wa, wb, x, g.reshape(1, d), b.reshape(1, d))


def _router_kernel(x_ref, rt_ref, bias_ref, o_ref):
    tr = x_ref.shape[0]
    scores = _sigmoid(_dot_nt(rt_ref[...], x_ref[...], precision=HIGHEST))
    biased = scores + bias_ref[...]
    grp = biased.reshape(N_EXPERT_GROUPS, EXPERTS_PER_GROUP, tr)
    pos = lax.broadcasted_iota(jnp.int32, grp.shape, 1)
    m1 = jnp.max(grp, axis=1, keepdims=True)
    first = jnp.min(jnp.where(grp == m1, pos, EXPERTS_PER_GROUP), axis=1, keepdims=True)
    m2 = jnp.max(jnp.where(pos == first, -jnp.inf, grp), axis=1, keepdims=True)
    gscore = (m1 + m2).reshape(N_EXPERT_GROUPS, tr)
    gidx = lax.broadcasted_iota(jnp.int32, gscore.shape, 0)
    grank = jnp.zeros(gscore.shape, F32)
    for j in range(N_EXPERT_GROUPS):
        row = gscore[j:j + 1, :]
        grank = grank + jnp.where(gidx > j, jnp.where(row >= gscore, 1.0, 0.0), jnp.where(row > gscore, 1.0, 0.0))
    gkeep = jnp.where(grank < TOPK_GROUPS, 1.0, 0.0)
    keep = jnp.broadcast_to(gkeep[:, None, :], grp.shape).reshape(N_EXPERTS, tr)
    masked = jnp.where(keep > 0.5, biased, -jnp.inf)
    eidx = lax.broadcasted_iota(jnp.int32, masked.shape, 0)
    rank = jnp.zeros(masked.shape, F32)
    for j in range(N_EXPERTS):
        row = masked[j:j + 1, :]
        rank = rank + jnp.where(eidx > j, jnp.where(row >= masked, 1.0, 0.0), jnp.where(row > masked, 1.0, 0.0))
    gate = jnp.where(rank < TOP_K, scores, 0.0)
    gate = gate / jnp.sum(gate, axis=0, keepdims=True) * ROUTED_SCALE
    o_ref[...] = jnp.concatenate([gate, jnp.zeros((LANES - N_EXPERTS, tr), F32)], axis=0).T


def moe_router(x, router, bias, tr=512):
    n, d = x.shape
    return pl.pallas_call(
        _router_kernel,
        grid=(n // tr,),
        in_specs=[pl.BlockSpec((tr, d), lambda i: (i, 0)),
                  pl.BlockSpec((N_EXPERTS, d), lambda i: (0, 0)),
                  pl.BlockSpec((N_EXPERTS, 1), lambda i: (0, 0))],
        out_specs=pl.BlockSpec((tr, LANES), lambda i: (i, 0)),
        out_shape=jax.ShapeDtypeStruct((n, LANES), F32),
        compiler_params=_params("parallel"),
        name="moe_router",
    )(x, router.T, bias.reshape(N_EXPERTS, 1))


def _quantize_fp8(a, axes):
    amax = jnp.max(jnp.abs(a), axis=axes, keepdims=True)
    scale = jnp.maximum(amax, FP8_TINY) * (1.0 / FP8_MAX)
    return (a * (1.0 / scale)).astype(FP8), scale


def _swiglu_hidden(xq, x_scale, w1q, w3q, w_scale, gate=None):
    col1 = x_scale * w_scale[0:1, 0:1]
    col3 = x_scale * w_scale[1:2, 0:1]
    if gate is not None:
        col3 = col3 * gate
    h1 = _dot(xq, w1q) * col1
    return h1 * _sigmoid(h1) * (_dot(xq, w3q) * col3)


def _experts_ln_kernel(x_ref, gate_ref, w1_ref, w3_ref, ws_ref, w2_ref, sw1_ref, sw3_ref, sws_ref, sw2_ref,
                       g_ref, b_ref, o_ref, obf_ref, acc_ref, xq_ref, xs_ref):
    step = pl.program_id(1)
    per_step = w1_ref.shape[0]

    @pl.when(step == 0)
    def _():
        xq, xs = _quantize_fp8(x_ref[...], (1,))
        xq_ref[...] = xq
        xs_ref[...] = xs
        h = _swiglu_hidden(xq, xs, sw1_ref[...], sw3_ref[...], sws_ref[...])
        acc_ref[...] = _dot(h.astype(BF16), sw2_ref[...].astype(BF16))

    lane = lax.broadcasted_iota(jnp.int32, gate_ref.shape, 1)
    gates = gate_ref[...]
    xq, xs = xq_ref[...], xs_ref[...]
    hidden = []
    for j in range(per_step):
        gcol = jnp.sum(jnp.where(lane == step * per_step + j, gates, 0.0), axis=1, keepdims=True)
        hidden.append(_swiglu_hidden(xq, xs, w1_ref[j], w3_ref[j], ws_ref[j], gcol).astype(BF16))
    w2 = w2_ref[...].astype(BF16)
    acc_ref[...] += _dot(jnp.concatenate(hidden, axis=1), w2.reshape(per_step * w2.shape[1], w2.shape[2]))

    @pl.when(step == pl.num_programs(1) - 1)
    def _():
        y = _layer_norm_rows(ALPHA * x_ref[...] + acc_ref[...], g_ref[...], b_ref[...])
        o_ref[...] = y
        obf_ref[...] = y.astype(BF16)


def _quantize_expert_weights(w1, w3):
    w1q, s1 = _quantize_fp8(w1, (-2, -1))
    w3q, s3 = _quantize_fp8(w3, (-2, -1))
    scales = jnp.broadcast_to(jnp.concatenate([s1, s3], axis=-2), s1.shape[:-2] + (2, w1.shape[-1]))
    return w1q, w3q, scales


def moe_experts_ln(x, gates, w1, w3, w2, sw1, sw3, sw2, g, b, tm=1024):
    n, d = x.shape
    ne = w1.shape[0]
    w1q, w3q, ws = _quantize_expert_weights(w1, w3)
    sw1q, sw3q, sws = _quantize_expert_weights(sw1, sw3)
    tok = lambda i, e: (i, 0)
    fixed = lambda i, e: (0, 0)
    per_expert = lambda *blk: pl.BlockSpec((MOE_EXPERTS_PER_STEP,) + blk, lambda i, e: (e, 0, 0))
    return pl.pallas_call(
        _experts_ln_kernel,
        grid=(n // tm, ne // MOE_EXPERTS_PER_STEP),
        in_specs=[pl.BlockSpec((tm, d), tok), pl.BlockSpec((tm, LANES), tok),
                  per_expert(d, EXPERT_FF), per_expert(d, EXPERT_FF), per_expert(2, EXPERT_FF),
                  per_expert(EXPERT_FF, d),
                  pl.BlockSpec((d, EXPERT_FF), fixed), pl.BlockSpec((d, EXPERT_FF), fixed),
                  pl.BlockSpec((2, EXPERT_FF), fixed), pl.BlockSpec((EXPERT_FF, d), fixed),
                  pl.BlockSpec((1, d), fixed), pl.BlockSpec((1, d), fixed)],
        out_specs=[pl.BlockSpec((tm, d), tok), pl.BlockSpec((tm, d), tok)],
        out_shape=[jax.ShapeDtypeStruct((n, d), F32), jax.ShapeDtypeStruct((n, d), BF16)],
        scratch_shapes=[pltpu.VMEM((tm, d), F32), pltpu.VMEM((tm, d), FP8), pltpu.VMEM((tm, 1), F32)],
        compiler_params=_params("parallel", "arbitrary"),
        name="moe_experts_ln",
    )(x, gates, w1q, w3q, ws, w2, sw1q, sw3q, sws, sw2, g.reshape(1, d), b.reshape(1, d))


def moe_block(x, router, bias, w1, w3, w2, sw1, sw3, sw2, g, b):
    gates = moe_router(x, router, bias)
    return moe_experts_ln(x, gates, w1, w3, w2, sw1, sw3, sw2, g, b)


def _s5_tables(lam_re, lam_im, log_dt, b_re, b_im, c_re, c_im, n_chunk):
    t, h, p = S5_CHUNK, S5_GROUP_CH, S5_STATE
    dt = jnp.exp(log_dt.astype(F32))[:, None]
    den = lam_re ** 2 + lam_im ** 2

    def lam_pow(k):
        k = jnp.asarray(k, F32)[..., None, None]
        mag = jnp.exp(lam_re * dt * k)
        return mag * jnp.cos(lam_im * dt * k), mag * jnp.sin(lam_im * dt * k)

    lb_re, lb_im = lam_pow(1.0)
    f_re = ((lb_re - 1.0) * lam_re + lb_im * lam_im) / den
    f_im = (lb_im * lam_re - (lb_re - 1.0) * lam_im) / den
    bb_re = f_re[..., None] * b_re - f_im[..., None] * b_im
    bb_im = f_re[..., None] * b_im + f_im[..., None] * b_re
    pr, pi = lam_pow(jnp.arange(t))
    cl_re = c_re[None] * pr[:, :, None, :] - c_im[None] * pi[:, :, None, :]
    cl_im = c_re[None] * pi[:, :, None, :] + c_im[None] * pr[:, :, None, :]
    klag = jnp.einsum('tgop,gpi->tgoi', cl_re, bb_re) - jnp.einsum('tgop,gpi->tgoi', cl_im, bb_im)
    nb = S5_GROUPS // S5_PACK
    split = lambda a, axis: a.reshape(a.shape[:axis] + (nb, S5_PACK) + a.shape[axis + 1:])
    eye = jnp.eye(S5_PACK, dtype=F32)
    lag_t = jnp.transpose(split(klag, 1), (1, 0, 2, 4, 3))
    lag_t = (lag_t[:, :, :, :, None, :] * eye[None, None, :, None, :, None]).reshape(nb, t, LANES, LANES)
    qr, qi = lam_pow(t - 1 - jnp.arange(t))
    st_re = qr[..., None] * bb_re[None] - qi[..., None] * bb_im[None]
    st_im = qr[..., None] * bb_im[None] + qi[..., None] * bb_re[None]
    st = jnp.stack([st_re, st_im], axis=0)
    st_t = jnp.transpose(split(st, 2), (2, 1, 3, 5, 0, 4)).reshape(nb, t, LANES, 2 * p)
    er, ei = lam_pow(jnp.arange(t) + 1)
    x_re = c_re[None] * er[:, :, None, :] - c_im[None] * ei[:, :, None, :]
    x_im = c_re[None] * ei[:, :, None, :] + c_im[None] * er[:, :, None, :]
    cr = jnp.stack([x_re, -x_im], axis=0)
    cr_t = jnp.transpose(split(cr, 2), (2, 1, 0, 5, 3, 4)).reshape(nb, t, 2 * p, LANES)
    levels = max(1, int(math.log2(n_chunk)))
    sr, si = lam_pow(t * (2.0 ** jnp.arange(levels)))
    sr = sr.reshape(levels, nb, S5_PACK * p)
    si = si.reshape(levels, nb, S5_PACK * p)
    a1 = jnp.concatenate([sr, sr], axis=-1)
    a2 = jnp.concatenate([-si, si], axis=-1)
    scan = jnp.transpose(jnp.stack([a1, a2], axis=1), (2, 0, 1, 3))
    return lag_t, st_t, cr_t, scan.astype(F32)


def _s5_build_tables(lag_ref, st_ref, cr_ref, wtoe_ref, wstate_ref, wcross_ref):
    t = lag_ref.shape[1]
    p = S5_STATE
    kp = S5_PACK * p
    wtoe_ref[...] = jnp.zeros_like(wtoe_ref)
    for d in range(t):
        tile = lag_ref[0, d].astype(BF16)
        for j in range(t - d):
            wtoe_ref[j * LANES:(j + 1) * LANES, (j + d) * LANES:(j + d + 1) * LANES] = tile
    lane = lax.broadcasted_iota(jnp.int32, (LANES, LANES), 1)
    row_g = lax.broadcasted_iota(jnp.int32, (LANES, kp), 0) // S5_GROUP_CH
    same_s = row_g == lax.broadcasted_iota(jnp.int32, (LANES, kp), 1) // p
    for j in range(t):
        a = st_ref[0, j]
        swapped = pltpu.roll(a, p, 1)
        for c, both in enumerate((jnp.where(lane < p, a, swapped), jnp.where(lane < p, swapped, a))):
            wide = jnp.concatenate([both] * (kp // LANES), axis=1)
            wstate_ref[j * LANES:(j + 1) * LANES, c * kp:(c + 1) * kp] = jnp.where(same_s, wide, 0.0).astype(BF16)
    same_c = (lax.broadcasted_iota(jnp.int32, (kp, LANES), 0) // p
              == lax.broadcasted_iota(jnp.int32, (kp, LANES), 1) // S5_GROUP_CH)
    for i in range(t):
        a = cr_ref[0, i]
        for c in range(2):
            tall = jnp.concatenate([a[c * p:(c + 1) * p]] * S5_PACK, axis=0)
            wcross_ref[c * kp:(c + 1) * kp, i * LANES:(i + 1) * LANES] = jnp.where(same_c, tall, 0.0).astype(BF16)


def _s5_kernel(u_ref, lag_ref, st_ref, cr_ref, scan_ref, o_ref, wtoe_ref, wstate_ref, wcross_ref):
    @pl.when(pl.program_id(1) == 0)
    def _():
        _s5_build_tables(lag_ref, st_ref, cr_ref, wtoe_ref, wstate_ref, wcross_ref)

    n_chunk, t, _ = u_ref.shape
    x = jnp.concatenate([u_ref[:, j, :] for j in range(t)], axis=1).astype(BF16)
    local = _dot(x, wtoe_ref[...])
    state = _dot(x, wstate_ref[...])
    row = lax.broadcasted_iota(jnp.int32, state.shape, 0)
    s = jnp.where(row >= 1, pltpu.roll(state, 1, 0), 0.0)
    half = state.shape[1] // 2
    level = 0
    d = 1
    while d < n_chunk:
        mult = scan_ref[0, level]
        prev = jnp.where(row >= d, pltpu.roll(s, d, 0), 0.0)
        s = s + mult[0:1, :] * prev + mult[1:2, :] * pltpu.roll(prev, half, 1)
        d *= 2
        level += 1
    y = local + _dot(s.astype(BF16), wcross_ref[...])
    for i in range(t):
        o_ref[:, i, :] = y[:, i * LANES:(i + 1) * LANES]


def s5_scan(u3, bsz, lag_t, st_t, cr_t, scan):
    rows, t, w = u3.shape
    n_chunk = rows // bsz
    kp2 = 2 * S5_PACK * S5_STATE
    table = lambda a: pl.BlockSpec((1,) + a.shape[1:], lambda j, b: (j, 0, 0, 0))
    return pl.pallas_call(
        _s5_kernel,
        grid=(w // LANES, bsz),
        in_specs=[pl.BlockSpec((n_chunk, t, LANES), lambda j, b: (b, 0, j)),
                  table(lag_t), table(st_t), table(cr_t), table(scan)],
        out_specs=pl.BlockSpec((n_chunk, t, LANES), lambda j, b: (b, 0, j)),
        out_shape=jax.ShapeDtypeStruct(u3.shape, F32),
        scratch_shapes=[pltpu.VMEM((t * LANES, t * LANES), BF16), pltpu.VMEM((t * LANES, kp2), BF16),
                        pltpu.VMEM((kp2, t * LANES), BF16)],
        compiler_params=_params("arbitrary", "arbitrary"),
        name="s5_scan",
    )(u3, lag_t, st_t, cr_t, scan)


def _s5_post_kernel(y_ref, u_ref, d_ref, w_ref, b_ref, o_ref):
    u = u_ref[...]
    y = _gelu(y_ref[...].reshape(u.shape) + d_ref[...] * u)
    o_ref[...] = (y * _sigmoid(_dot(y.astype(BF16), w_ref[...]) + b_ref[...])).astype(o_ref.dtype)


def s5_post(y3, h, d_skip, w_glu, b_glu, tm=1024):
    rows, t, w = y3.shape
    n = rows * t
    tm = min(tm, n)
    row = lambda i: (i, 0)
    fixed = lambda i: (0, 0)
    return pl.pallas_call(
        _s5_post_kernel,
        grid=(n // tm,),
        in_specs=[pl.BlockSpec((tm // t, t, w), lambda i: (i, 0, 0)), pl.BlockSpec((tm, w), row),
                  pl.BlockSpec((1, w), fixed), pl.BlockSpec((w, w), fixed), pl.BlockSpec((1, w), fixed)],
        out_specs=pl.BlockSpec((tm, w), row),
        out_shape=jax.ShapeDtypeStruct((n, w), BF16),
        compiler_params=_params("parallel"),
        name="s5_post",
    )(y3, h, d_skip.reshape(1, w), w_glu.astype(BF16), b_glu.reshape(1, w))


def s5_mixer(h, u3, bsz, seq, lam_re, lam_im, log_dt, b_re, b_im, c_re, c_im, d_skip, w_glu, b_glu):
    tables = _s5_tables(lam_re, lam_im, log_dt, b_re, b_im, c_re, c_im, seq // S5_CHUNK)
    return s5_post(s5_scan(u3, bsz, *tables), h, d_skip, w_glu, b_glu)


def _rope_tables(pos, rot_dim, theta, head_dim, n_heads):
    half = rot_dim // 2
    f32 = np.float32
    inv_freq = f32(theta) ** (-np.arange(half, dtype=f32) / f32(half))
    ang = (pos.astype(f32)[:, None] * inv_freq[None, :]).astype(np.float64)
    cos, sin = np.cos(ang), np.sin(ang)
    rest = head_dim - rot_dim
    n = pos.shape[0]
    c = np.concatenate([cos, cos, np.ones((n, rest))], axis=1)
    s_up = np.concatenate([-sin, np.zeros((n, half + rest))], axis=1)
    s_dn = np.concatenate([np.zeros((n, half)), sin, np.zeros((n, rest))], axis=1)
    tile = lambda a: jnp.asarray(np.tile(a, (1, n_heads)), F32)
    return tile(c), tile(s_up), tile(s_dn)


def _rope_apply(x, c, s_up, s_dn, half):
    return x * c + pltpu.roll(x, LANES - half, 1) * s_up + pltpu.roll(x, half, 1) * s_dn


def _retention_tables():
    c = RET_CHUNK
    log_gamma = np.log(1.0 - 2.0 ** (-5.0 - np.arange(RET_HEADS, dtype=np.float64)))
    i = np.arange(c, dtype=np.float64)
    diff = i[:, None] - i[None, :]
    decay = np.where(diff >= 0, np.exp(diff[None] * log_gamma[:, None, None]), 0.0)
    qdec = np.repeat(np.exp((i + 1.0)[:, None] * log_gamma[None, :]), RET_DK, axis=1)
    kdec = np.repeat(np.exp((c - 1.0 - i)[:, None] * log_gamma[None, :]), RET_DK, axis=1)
    chunk_decay = [float(v) for v in np.exp(c * log_gamma)]
    return jnp.asarray(decay, F32), jnp.asarray(qdec, F32), jnp.asarray(kdec, F32), chunk_decay


def _retention_kernel(chunk_decay, q_ref, k_ref, v0_ref, v1_ref, g0_ref, g1_ref, c_ref, su_ref, sd_ref,
                      dec_ref, qdec_ref, kdec_ref, lng_ref, lnb_ref, o_ref, state_ref):
    @pl.when(pl.program_id(1) == 0)
    def _():
        state_ref[...] = jnp.zeros_like(state_ref)

    half = RET_DK // 2
    tabs = (c_ref[...], su_ref[...], sd_ref[...])
    q = jnp.concatenate([_rope_apply(q_ref[:, s:s + LANES], *tabs, half) for s in (0, LANES)], axis=1)
    k = jnp.concatenate([_rope_apply(k_ref[:, s:s + LANES], *tabs, half) for s in (0, LANES)], axis=1)
    k = k * (RET_DK ** -0.5)
    q_dec = q * qdec_ref[...]
    k_dec = k * kdec_ref[...]
    v = jnp.concatenate([v0_ref[...], v1_ref[...]], axis=1)
    gate = jnp.concatenate([g0_ref[...], g1_ref[...]], axis=1)
    states = [state_ref[h] for h in range(RET_HEADS)]

    def head(h):
        ks = slice(h * RET_DK, (h + 1) * RET_DK)
        vh = v[:, h * RET_DV:(h + 1) * RET_DV].astype(BF16)
        scores = _dot_nt(q[:, ks].astype(BF16), k[:, ks].astype(BF16)) * dec_ref[h]
        yield
        y = _dot(scores.astype(BF16), vh) + _dot(q_dec[:, ks].astype(BF16), states[h].astype(BF16))
        yield
        new_state = states[h] * chunk_decay[h] + _dot_tn(k_dec[:, ks].astype(BF16), vh)
        yield
        mu = jnp.mean(y, axis=-1, keepdims=True)
        yc = y - mu
        var = jnp.mean(yc * yc, axis=-1, keepdims=True)
        return yc * lax.rsqrt(var + RET_GN_EPS), new_state

    results = _run_interleaved([head(h) for h in range(RET_HEADS)])
    outs = [r[0] for r in results]
    for h, (_, new_state) in enumerate(results):
        state_ref[h] = new_state
    yn = jnp.concatenate(outs, axis=1) * lng_ref[...] + lnb_ref[...]
    o_ref[...] = (gate * _sigmoid(gate) * yn).astype(o_ref.dtype)


def retention_mixer(h, bsz, seq, col0, ln_g, ln_b):
    c = RET_CHUNK
    n_chunk = seq // c
    qk_w = RET_HEADS * RET_DK
    v_w = RET_HEADS * RET_DV
    assert col0 % qk_w == 0 and qk_w == 2 * LANES and v_w == 2 * qk_w
    cb = col0 // qk_w
    rc, rsu, rsd = _rope_tables(np.arange(seq), RET_DK, RET_THETA, RET_DK, 2)
    dec, qdec, kdec, chunk_decay = _retention_tables()
    row = lambda j: (lambda b, n: (b * n_chunk + n, j))
    pos = lambda b, n: (n, 0)
    fixed2 = lambda b, n: (0, 0)
    kern = functools.partial(_retention_kernel, chunk_decay)
    return pl.pallas_call(
        kern,
        grid=(bsz, n_chunk),
        in_specs=[pl.BlockSpec((c, qk_w), row(cb)), pl.BlockSpec((c, qk_w), row(cb + 1)),
                  pl.BlockSpec((c, qk_w), row(cb + 2)), pl.BlockSpec((c, qk_w), row(cb + 3)),
                  pl.BlockSpec((c, qk_w), row(cb + 4)), pl.BlockSpec((c, qk_w), row(cb + 5)),
                  pl.BlockSpec((c, LANES), pos), pl.BlockSpec((c, LANES), pos), pl.BlockSpec((c, LANES), pos),
                  pl.BlockSpec((RET_HEADS, c, c), lambda b, n: (0, 0, 0)),
                  pl.BlockSpec((c, qk_w), fixed2), pl.BlockSpec((c, qk_w), fixed2),
                  pl.BlockSpec((1, v_w), fixed2), pl.BlockSpec((1, v_w), fixed2)],
        out_specs=pl.BlockSpec((c, v_w), lambda b, n: (b * n_chunk + n, 0)),
        out_shape=jax.ShapeDtypeStruct((bsz * seq, v_w), BF16),
        scratch_shapes=[pltpu.VMEM((RET_HEADS, RET_DK, RET_DV), F32)],
        compiler_params=_params("parallel", "arbitrary"),
        name="retention",
    )(h, h, h, h, h, h, rc, rsu, rsd, dec, qdec, kdec, ln_g.reshape(1, v_w), ln_b.reshape(1, v_w))


NSA_KV_W = NSA_KV_GROUPS * NSA_HEAD_DIM
NSA_GATE_COLS = 3 * NSA_HEADS


def _nsa_prep_kernel(q_ref, kvc_ref, kvs_ref, kvw_ref, c_ref, su_ref, sd_ref,
                     qo_ref, kc_ref, vc_ref, ks_ref, vs_ref, kw_ref, vw_ref):
    half = NSA_ROT_DIM // 2
    tabs = (c_ref[...], su_ref[...], sd_ref[...])
    scale = NSA_HEAD_DIM ** -0.5 * math.log2(math.e)
    q = jnp.concatenate(
        [_rope_apply(q_ref[:, s:s + LANES], *tabs, half) * scale for s in range(0, NSA_WIDTH, LANES)], axis=1)
    qo_ref[0] = q.T.astype(qo_ref.dtype)

    def split(x, o_ref):
        for g in range(NSA_KV_GROUPS):
            o_ref[0, g] = x[:, g * NSA_HEAD_DIM:(g + 1) * NSA_HEAD_DIM].astype(o_ref.dtype)

    def split_t(x, o_ref):
        xt = x.T
        for g in range(NSA_KV_GROUPS):
            o_ref[0, g] = xt[g * NSA_HEAD_DIM:(g + 1) * NSA_HEAD_DIM, :].astype(o_ref.dtype)

    split(kvc_ref[:, :NSA_KV_W], kc_ref)
    split(kvc_ref[:, NSA_KV_W:], vc_ref)
    split(_rope_apply(kvs_ref[:, :NSA_KV_W], *tabs, half), ks_ref)
    split_t(kvs_ref[:, NSA_KV_W:], vs_ref)
    split(_rope_apply(kvw_ref[:, :NSA_KV_W], *tabs, half), kw_ref)
    split_t(kvw_ref[:, NSA_KV_W:], vw_ref)


def nsa_prep(h, bsz, seq, tl=512):
    tl = min(tl, seq)
    nl = seq // tl
    rc, rsu, rsd = _rope_tables(np.arange(seq), NSA_ROT_DIM, ROPE_THETA, NSA_HEAD_DIM, LANES // NSA_HEAD_DIM)
    row = lambda j: (lambda b, l: (b * nl + l, j))
    pos = lambda b, l: (l, 0)
    kv_out = pl.BlockSpec((1, NSA_KV_GROUPS, tl, NSA_HEAD_DIM), lambda b, l: (b, 0, l, 0))
    kv_shape = lambda dt: jax.ShapeDtypeStruct((bsz, NSA_KV_GROUPS, seq, NSA_HEAD_DIM), dt)
    vt_out = pl.BlockSpec((1, NSA_KV_GROUPS, NSA_HEAD_DIM, tl), lambda b, l: (b, 0, 0, l))
    vt_shape = jax.ShapeDtypeStruct((bsz, NSA_KV_GROUPS, NSA_HEAD_DIM, seq), BF16)
    two = 2 * NSA_KV_W
    return pl.pallas_call(
        _nsa_prep_kernel,
        grid=(bsz, nl),
        in_specs=[pl.BlockSpec((tl, NSA_WIDTH), row(1)),
                  pl.BlockSpec((tl, two), row(4)), pl.BlockSpec((tl, two), row(5)), pl.BlockSpec((tl, two), row(6)),
                  pl.BlockSpec((tl, LANES), pos), pl.BlockSpec((tl, LANES), pos), pl.BlockSpec((tl, LANES), pos)],
        out_specs=[pl.BlockSpec((1, NSA_WIDTH, tl), lambda b, l: (b, 0, l)),
                   kv_out, kv_out, kv_out, vt_out, kv_out, vt_out],
        out_shape=[jax.ShapeDtypeStruct((bsz, NSA_WIDTH, seq), BF16),
                   kv_shape(F32), kv_shape(F32), kv_shape(BF16), vt_shape, kv_shape(BF16), vt_shape],
        compiler_params=_params("parallel", "parallel"),
        name="nsa_prep",
    )(h, h, h, h, rc, rsu, rsd)


def _nsa_compress_kernel(hk_ref, hv_ref, pek_ref, pev_ref, kw1_ref, kb1_ref, kw2_ref, vw1_ref, vb1_ref, vw2_ref,
                         c_ref, su_ref, sd_ref, ko_ref, vo_ref):
    def mlp(h_ref, pe_ref, w1_ref, b1_ref, w2_ref):
        hb = h_ref[0, 0]
        rows = hb.shape[0]
        first = _dot((hb + pe_ref[0:1, :]).astype(BF16), w1_ref[0])
        second = _dot((hb + pe_ref[1:2, :]).astype(BF16), w1_ref[1])
        hid = _gelu(first + pltpu.roll(second, rows - 1, 0) + b1_ref[...])
        return _dot(hid.astype(BF16), w2_ref[...])

    kc = _rope_apply(mlp(hk_ref, pek_ref, kw1_ref, kb1_ref, kw2_ref), c_ref[...], su_ref[...], sd_ref[...],
                     NSA_ROT_DIM // 2)
    vc = mlp(hv_ref, pev_ref, vw1_ref, vb1_ref, vw2_ref)
    ko_ref[0, 0] = kc[:, :NSA_HEAD_DIM].astype(ko_ref.dtype)
    vo_ref[0, 0] = vc.T[:NSA_HEAD_DIM, :].astype(vo_ref.dtype)


def nsa_compress(kc, vc, pe_k, pe_v, ck_w1, ck_b1, ck_w2, cv_w1, cv_b1, cv_w2):
    bsz, grp, seq, d = kc.shape
    n_rows = seq // CMP_STRIDE
    flat = CMP_STRIDE * d
    cmp_end = np.arange(n_rows) * CMP_STRIDE + CMP_BLOCK - 1
    rc, rsu, rsd = _rope_tables(cmp_end, NSA_ROT_DIM, ROPE_THETA, NSA_HEAD_DIM, LANES // NSA_HEAD_DIM)
    pad_w2 = lambda w: jnp.pad(w, ((0, 0), (0, LANES - d))).astype(BF16)
    blk = pl.BlockSpec((1, 1, n_rows, flat), lambda b, g: (b, g, 0, 0))
    f2 = lambda b, g: (0, 0)
    f3 = lambda b, g: (0, 0, 0)
    w_specs = [pl.BlockSpec((2, flat, CMP_HIDDEN), f3), pl.BlockSpec((1, CMP_HIDDEN), f2),
               pl.BlockSpec((CMP_HIDDEN, LANES), f2)]
    return pl.pallas_call(
        _nsa_compress_kernel,
        grid=(bsz, grp),
        in_specs=[blk, blk, pl.BlockSpec((2, flat), f2), pl.BlockSpec((2, flat), f2)] + w_specs + w_specs
                 + [pl.BlockSpec((n_rows, LANES), f2)] * 3,
        out_specs=[pl.BlockSpec((1, 1, n_rows, d), lambda b, g: (b, g, 0, 0)),
                   pl.BlockSpec((1, 1, d, n_rows), lambda b, g: (b, g, 0, 0))],
        out_shape=[jax.ShapeDtypeStruct((bsz, grp, n_rows, d), BF16),
                   jax.ShapeDtypeStruct((bsz, grp, d, n_rows), BF16)],
        compiler_params=_params("parallel", "parallel"),
        name="nsa_compress",
    )(kc.reshape(bsz, grp, n_rows, flat), vc.reshape(bsz, grp, n_rows, flat),
      pe_k.reshape(2, flat), pe_v.reshape(2, flat),
      ck_w1.reshape(2, flat, CMP_HIDDEN).astype(BF16), ck_b1.reshape(1, CMP_HIDDEN), pad_w2(ck_w2),
      cv_w1.reshape(2, flat, CMP_HIDDEN).astype(BF16), cv_b1.reshape(1, CMP_HIDDEN), pad_w2(cv_w2),
      rc, rsu, rsd)


def _per_head(x):
    return jnp.concatenate([x] * NSA_HPG, axis=1)


def _nsa_attn_kernel(seq, tk, qt_ref, gate_ref, kc_ref, vct_ref, ks_ref, vst_ref, kw_ref, vwt_ref, mmapt_ref,
                     o_ref, sel_ref):
    n_blk = seq // SLC_BLOCK
    n_sel = min(N_SLC, n_blk)
    hd = NSA_HEAD_DIM
    w = NSA_HPG * hd
    groups = range(NSA_KV_GROUPS)
    q0 = pl.program_id(1) * Q_BLOCK
    t_l = q0 + lax.broadcasted_iota(jnp.int32, (1, Q_BLOCK), 1)

    def select(g):
        qg = qt_ref[0, g * w:(g + 1) * w, :]
        qst = jnp.concatenate([qg[h * hd:(h + 1) * hd, :] for h in range(NSA_HPG)], axis=1)
        kc = kc_ref[0, g]
        n_cmp = kc.shape[0]
        cmp_end = lax.broadcasted_iota(jnp.int32, (n_cmp, 1), 0) * CMP_STRIDE + (CMP_BLOCK - 1)
        s = _dot(kc, qst) + _per_head(jnp.where(cmp_end <= t_l, 0.0, MASK_VALUE))
        yield
        p = jnp.exp2(s - jnp.max(s, axis=0, keepdims=True))
        any_key = _per_head(jnp.where(t_l >= CMP_BLOCK - 1, 1.0, 0.0))
        inv_l = any_key / jnp.sum(p, axis=0, keepdims=True)
        o_cmp = _dot(vct_ref[0, g], p.astype(BF16)) * inv_l
        yield
        p = p * inv_l
        imp = p[:, 0:Q_BLOCK]
        for h in range(1, NSA_HPG):
            imp = imp + p[:, h * Q_BLOCK:(h + 1) * Q_BLOCK]
        imp_slc = _dot_split(mmapt_ref[...], imp, 'b', 3)
        yield
        blk = lax.broadcasted_iota(jnp.int32, (n_blk, 1), 0)
        cur = t_l // SLC_BLOCK
        score = jnp.where(blk == 0, FORCE_SCORE,
                          jnp.where(blk == cur, FORCE_SCORE, jnp.where(blk == cur - 1, FORCE_SCORE, imp_slc)))
        score = jnp.where(blk * SLC_BLOCK <= t_l, score, -FORCE_SCORE)
        sel = jnp.zeros((n_blk, Q_BLOCK), F32)
        for _ in range(n_sel):
            best = jnp.max(score, axis=0, keepdims=True)
            idx = jnp.min(jnp.where(score == best, blk, n_blk), axis=0, keepdims=True)
            pick = blk == idx
            sel = jnp.where(pick, 1.0, sel)
            score = jnp.where(pick, -jnp.inf, score)
            yield
        sel_ref[g] = sel
        return qst, o_cmp

    selected = _run_interleaved([select(g) for g in groups])
    qst = [r[0] for r in selected]
    o_cmp = [r[1] for r in selected]

    blocks_per_tile = tk // SLC_BLOCK

    def slc_tile(kt, carry):
        k0 = pl.multiple_of(kt * tk, tk)
        causal = k0 + lax.broadcasted_iota(jnp.int32, (tk, 1), 0) <= t_l
        scores = [_dot(ks_ref[0, g, pl.ds(k0, tk), :], qst[g]) for g in groups]
        out = []
        for g in groups:
            m, l, acc = carry[g]
            chosen = jnp.concatenate(
                [jnp.broadcast_to(sel_ref[g, pl.ds(kt * blocks_per_tile + j, 1), :], (SLC_BLOCK, Q_BLOCK))
                 for j in range(blocks_per_tile)], axis=0)
            bias = jnp.where(causal, jnp.where(chosen > 0.5, 0.0, MASK_VALUE), MASK_VALUE)
            s = scores[g] + _per_head(bias)
            m_new = jnp.maximum(m, jnp.max(s, axis=0, keepdims=True))
            alpha = jnp.exp2(m - m_new)
            p = jnp.exp2(s - m_new)
            l = alpha * l + jnp.sum(p, axis=0, keepdims=True)
            acc = alpha * acc + _dot(vst_ref[0, g, :, pl.ds(k0, tk)], p.astype(BF16))
            out.append((m_new, l, acc))
        return tuple(out)

    n_tiles = (q0 + Q_BLOCK + tk - 1) // tk
    cols = NSA_HPG * Q_BLOCK
    init = tuple((jnp.full((1, cols), MASK_VALUE, F32), jnp.zeros((1, cols), F32), jnp.zeros((hd, cols), F32))
                 for _ in groups)
    slc = lax.fori_loop(0, n_tiles, slc_tile, init)

    band = WINDOW + Q_BLOCK
    w0 = pl.multiple_of(jnp.maximum(q0 - WINDOW, 0), Q_BLOCK)
    kpos = w0 + lax.broadcasted_iota(jnp.int32, (band, 1), 0)
    win_bias = _per_head(jnp.where(kpos <= t_l, jnp.where(kpos > t_l - WINDOW, 0.0, MASK_VALUE), MASK_VALUE))
    sig_t = _sigmoid(gate_ref[...]).T

    def finish(g):
        s = _dot(kw_ref[0, g, pl.ds(w0, band), :], qst[g]) + win_bias
        yield
        p = jnp.exp2(s - jnp.max(s, axis=0, keepdims=True))
        o_win = _dot(vwt_ref[0, g, :, pl.ds(w0, band)], p.astype(BF16)) / jnp.sum(p, axis=0, keepdims=True)
        yield
        _, l_slc, acc_slc = slc[g]

        def gate(branch):
            first = (g * NSA_HPG) * 3 + branch
            return jnp.concatenate([sig_t[first + 3 * h:first + 3 * h + 1, :] for h in range(NSA_HPG)], axis=1)

        out_t = gate(0) * o_cmp[g] + gate(1) * (acc_slc / l_slc) + gate(2) * o_win
        pairs = []
        for h in range(0, NSA_HPG, 2):
            two = jnp.concatenate([out_t[:, h * Q_BLOCK:(h + 1) * Q_BLOCK],
                                   out_t[:, (h + 1) * Q_BLOCK:(h + 2) * Q_BLOCK]], axis=0)
            pairs.append(two.T)
        o_ref[:, g * w:(g + 1) * w] = jnp.concatenate(pairs, axis=1).astype(o_ref.dtype)

    _run_interleaved([finish(g) for g in groups])


def _nsa_pool_matrix(seq):
    n_blk = seq // SLC_BLOCK
    n_rows = seq // CMP_STRIDE
    per_stride = SLC_BLOCK // CMP_STRIDE
    span = CMP_BLOCK // CMP_STRIDE
    pool = np.zeros((n_blk, n_rows), np.float32)
    for j in range(n_blk):
        for m in range(per_stride):
            for n in range(span):
                c = per_stride * j + m + n - (span - 1)
                if 0 <= c < n_rows - 1:
                    pool[j, c] += 1.0
    return jnp.asarray(pool, BF16)


def nsa_attention(qt, h, gate_col_block, k_cmp, v_cmp_t, ks, vs_t, kw, vw_t, bsz, seq, tk=512):
    tk = min(tk, seq)
    nq = seq // Q_BLOCK
    pool = _nsa_pool_matrix(seq)
    n_rows = k_cmp.shape[2]
    d = NSA_HEAD_DIM
    qblk = lambda b, i: (b * nq + i, 0)
    whole = lambda *shape: pl.BlockSpec((1, NSA_KV_GROUPS) + shape, lambda b, i: (b, 0, 0, 0))
    kern = functools.partial(_nsa_attn_kernel, seq, tk)
    return pl.pallas_call(
        kern,
        grid=(bsz, nq),
        in_specs=[pl.BlockSpec((1, NSA_WIDTH, Q_BLOCK), lambda b, i: (b, 0, i)),
                  pl.BlockSpec((Q_BLOCK, LANES), lambda b, i: (b * nq + i, gate_col_block)),
                  whole(n_rows, d), whole(d, n_rows), whole(seq, d), whole(d, seq), whole(seq, d), whole(d, seq),
                  pl.BlockSpec(pool.shape, lambda b, i: (0, 0))],
        out_specs=pl.BlockSpec((Q_BLOCK, NSA_WIDTH), qblk),
        out_shape=jax.ShapeDtypeStruct((bsz * seq, NSA_WIDTH), BF16),
        scratch_shapes=[pltpu.VMEM((NSA_KV_GROUPS, seq // SLC_BLOCK, Q_BLOCK), F32)],
        compiler_params=_params("parallel", "arbitrary"),
        name="nsa_attention",
    )(qt, h, k_cmp, v_cmp_t, ks, vs_t, kw, vw_t, pool)


def nsa_mixer(h, bsz, seq, gate_col_block, pe_k, pe_v, ck_w1, ck_b1, ck_w2, cv_w1, cv_b1, cv_w2):
    qt, kc, vc, ks, vs_t, kw, vw_t = nsa_prep(h, bsz, seq)
    k_cmp, v_cmp_t = nsa_compress(kc, vc, pe_k, pe_v, ck_w1, ck_b1, ck_w2, cv_w1, cv_b1, cv_w2)
    return nsa_attention(qt, h, gate_col_block, k_cmp, v_cmp_t, ks, vs_t, kw, vw_t, bsz, seq)


def _head_ones(width, head_dim):
    idx = np.arange(width) // head_dim
    return jnp.asarray(idx[:, None] == idx[None, :], BF16)


def _softplus(x):
    return jnp.maximum(x, 0.0) + jnp.log(1.0 + jnp.exp(-jnp.abs(x)))


def _rwkv_pre_kernel(p_ref, prev_ref, mu_ref, w0_ref, wup_ref, a0_ref, aup_ref, gup_ref, kk_ref, ka_ref, rk_ref,
                     ones_ref, r_o, k_o, v_o, kk_o, b_o, ld_o, g_o, bonus_o):
    w = RWKV_WIDTH
    p = p_ref[...]
    first_row = jnp.where(pl.program_id(1) == 0, 0.0, prev_ref[7:8, :])
    is_row0 = lax.broadcasted_iota(jnp.int32, p.shape, 0) == 0
    prev = jnp.where(is_row0, first_row, pltpu.roll(p, 1, 0))
    ps = p + (prev - p) * mu_ref[...]
    r, k, v = ps[:, 0:w], ps[:, w:2 * w], ps[:, 2 * w:3 * w]
    o = 3 * w
    w_lo = ps[:, o:o + RWKV_LORA_W]
    a_lo = ps[:, o + RWKV_LORA_W:o + RWKV_LORA_W + RWKV_LORA_A]
    g_lo = ps[:, o + RWKV_LORA_W + RWKV_LORA_A:]
    wlog = -_softplus(-(w0_ref[...] + _dot(jnp.tanh(w_lo).astype(BF16), wup_ref[...]))) - 0.5
    a = _sigmoid(a0_ref[...] + _dot(a_lo.astype(BF16), aup_ref[...]))
    g = _dot(_sigmoid(g_lo).astype(BF16), gup_ref[...])
    kk = k * kk_ref[...]
    norm = jnp.sqrt(_dot_split(kk * kk, ones_ref[...], 'a', 2))
    kk = kk / jnp.maximum(norm, 1e-12)
    k2 = k * (1.0 + (a - 1.0) * ka_ref[...])
    r_o[...] = r
    k_o[...] = k2
    v_o[...] = v
    kk_o[...] = kk
    b_o[...] = kk * a
    ld_o[...] = -jnp.exp(wlog)
    g_o[...] = g
    bonus_o[...] = _dot_split(r * k2 * rk_ref[...], ones_ref[...], 'a', 2) * v


def rwkv_pre(h, bsz, seq, mu, w0, w_up, a0, a_up, g_up, k_k, k_a, r_k, tl=512):
    tl = min(tl, seq)
    nl = seq // tl
    w = RWKV_WIDTH
    cols = RWKV_COLS
    ones = _head_ones(w, RWKV_HEAD_DIM)
    f2 = lambda b, l: (0, 0)
    vec = pl.BlockSpec((1, w), f2)
    out_spec = pl.BlockSpec((tl, w), lambda b, l: (b * nl + l, 0))
    out_shape = jax.ShapeDtypeStruct((bsz * seq, w), F32)
    return pl.pallas_call(
        _rwkv_pre_kernel,
        grid=(bsz, nl),
        in_specs=[pl.BlockSpec((tl, cols), lambda b, l: (b * nl + l, 0)),
                  pl.BlockSpec((8, cols), lambda b, l: (jnp.maximum((b * seq + l * tl) // 8 - 1, 0), 0)),
                  pl.BlockSpec((1, cols), f2), vec, pl.BlockSpec((RWKV_LORA_W, w), f2),
                  vec, pl.BlockSpec((RWKV_LORA_A, w), f2), pl.BlockSpec((RWKV_LORA_G, w), f2),
                  vec, vec, vec, pl.BlockSpec((w, w), f2)],
        out_specs=[out_spec] * 8,
        out_shape=[out_shape] * 8,
        compiler_params=_params("parallel", "parallel"),
        name="rwkv_pre",
    )(h, h, mu.reshape(1, cols), w0.reshape(1, w), w_up.astype(BF16), a0.reshape(1, w), a_up.astype(BF16),
      g_up.astype(BF16), k_k.reshape(1, w), k_a.reshape(1, w), r_k.reshape(1, w), ones)


def _rwkv_masks():
    t, pk = RWKV_CHUNK, RWKV_PACK
    n = t * pk
    ri = np.arange(n)
    same = (ri[:, None] // t) == (ri[None, :] // t)
    tt, ss = ri[:, None] % t, ri[None, :] % t
    levels = []
    k = 1
    while k < t:
        levels.append(same & (tt // (2 * k) == ss // (2 * k)) & ((tt // k) % 2 == 1) & ((ss // k) % 2 == 0))
        k *= 2
    lvl = np.stack(levels).astype(np.float32)
    tri = (np.arange(t)[:, None] >= np.arange(t)[None, :]).astype(np.float32)
    head_lane = ((ri[:, None] // t) == (np.arange(pk * RWKV_HEAD_DIM)[None, :] // RWKV_HEAD_DIM)).astype(np.float32)
    return (jnp.asarray(tri, BF16), jnp.asarray(head_lane), jnp.asarray(same.astype(np.float32)), jnp.asarray(lvl))


def _rwkv_chain(r, k, v, kk, b, ld, st, tri, hl, bd, lvl_ref):
    t, pk = RWKV_CHUNK, RWKV_PACK
    n = t * pk
    c = _dot_split(tri, ld, 'b', 3)
    yield
    c_end = c[t - 1:t, :]
    e_neg = jnp.exp(-c)
    e_end = jnp.exp(c_end - c)
    kkd = (kk * jnp.exp(c - ld)).astype(BF16)
    rd = (r * jnp.exp(c)).astype(BF16)

    def big(x):
        return (jnp.concatenate([x] * pk, axis=0) * hl).astype(BF16)

    st_b = st.astype(BF16)
    v_big = big(v)
    a_all = _dot_nt(jnp.concatenate([kkd, rd], axis=0),
                    jnp.concatenate([big(k * e_neg), big(b * e_neg)], axis=0))
    yield
    ti = lax.broadcasted_iota(jnp.int32, (t, n), 0)
    si = lax.broadcasted_iota(jnp.int32, (t, n), 1) % t
    strict = ti > si
    incl = ti >= si
    a_kk = jnp.where(strict, a_all[:t, :n], 0.0)
    a_kb = jnp.where(strict, a_all[:t, n:], 0.0)
    a_rk = jnp.where(incl, a_all[t:, :n], 0.0)
    a_rb = jnp.where(incl, a_all[t:, n:], 0.0)
    rhs = _dot(kkd, st_b) + _dot(a_kk.astype(BF16), v_big)
    yield
    a_bd = jnp.concatenate([a_kb] * pk, axis=0) * bd
    m = jnp.where(lax.broadcasted_iota(jnp.int32, (n, n), 0) == lax.broadcasted_iota(jnp.int32, (n, n), 1), 1.0, 0.0)
    for lv in range(lvl_ref.shape[0]):
        mb = m.astype(BF16)
        ma = _dot(mb, (a_bd * lvl_ref[lv]).astype(BF16)).astype(BF16)
        yield
        m = m - _dot(ma, mb)
        yield
    u_big = _dot(m.astype(BF16), big(rhs))
    yield
    u = u_big[0:t]
    for h in range(1, pk):
        u = u + u_big[h * t:(h + 1) * t]
    y = _dot(rd, st_b) + _dot(a_rk.astype(BF16), v_big) - _dot(a_rb.astype(BF16), big(u))
    yield
    decay_col = jnp.broadcast_to(jnp.exp(c_end), st.shape).T
    kb_end = jnp.concatenate([k * e_end, -(b * e_end)], axis=0).astype(BF16)
    vu = jnp.concatenate([v, u], axis=0).astype(BF16)
    return y, decay_col * st + bd * _dot_tn(kb_end, vu)


def _rwkv_chunk_kernel(r_ref, k_ref, v_ref, kk_ref, b_ref, ld_ref, tri_ref, hl_ref, bd_ref, lvl_ref, y_ref, st_ref):
    @pl.when(pl.program_id(0) == 0)
    def _():
        st_ref[...] = jnp.zeros_like(st_ref)

    wp = RWKV_PACK * RWKV_HEAD_DIM
    tri, hl, bd = tri_ref[...], hl_ref[...], bd_ref[...]
    n_pack = r_ref.shape[2] // wp
    where = [(bi, slice(g * wp, (g + 1) * wp)) for bi in range(r_ref.shape[0]) for g in range(n_pack)]
    loaded = [tuple(ref[bi, :, cols] for ref in (r_ref, k_ref, v_ref, kk_ref, b_ref, ld_ref)) + (st_ref[i],)
              for i, (bi, cols) in enumerate(where)]
    results = _run_interleaved([_rwkv_chain(*args, tri, hl, bd, lvl_ref) for args in loaded])
    for i, ((bi, cols), (y, st_new)) in enumerate(zip(where, results)):
        y_ref[bi, :, cols] = y
        st_ref[i] = st_new


def rwkv_chunk(r, k, v, kk, b, ld, bsz, seq):
    t, pk = RWKV_CHUNK, RWKV_PACK
    n_chunk = seq // t
    w = RWKV_WIDTH
    wp = pk * RWKV_HEAD_DIM
    assert t == RWKV_HEAD_DIM
    tri, hl, bd, lvl = _rwkv_masks()
    blk = pl.BlockSpec((bsz, t, w), lambda c: (0, c, 0))
    f2 = lambda c: (0, 0)
    shaped = lambda a: a.reshape(bsz, seq, w)
    y = pl.pallas_call(
        _rwkv_chunk_kernel,
        grid=(n_chunk,),
        in_specs=[blk] * 6 + [pl.BlockSpec(tri.shape, f2), pl.BlockSpec(hl.shape, f2), pl.BlockSpec(bd.shape, f2),
                              pl.BlockSpec(lvl.shape, lambda c: (0, 0, 0))],
        out_specs=blk,
        out_shape=jax.ShapeDtypeStruct((bsz, seq, w), F32),
        scratch_shapes=[pltpu.VMEM((bsz * (w // wp), wp, wp), F32)],
        compiler_params=_params("arbitrary"),
        name="rwkv_chunk",
    )(shaped(r), shaped(k), shaped(v), shaped(kk), shaped(b), shaped(ld), tri, hl, bd, lvl)
    return y.reshape(bsz * seq, w)


def _rwkv_post_kernel(y_ref, bonus_ref, g_ref, lng_ref, lnb_ref, ones_ref, o_ref):
    y = y_ref[...]
    inv = 1.0 / RWKV_HEAD_DIM
    mu = _dot_split(y, ones_ref[...], 'a', 2) * inv
    yc = y - mu
    var = _dot_split(yc * yc, ones_ref[...], 'a', 2) * inv
    yn = yc * lax.rsqrt(var + RWKV_GN_EPS) * lng_ref[...] + lnb_ref[...]
    o_ref[...] = ((yn + bonus_ref[...]) * g_ref[...]).astype(o_ref.dtype)


def rwkv_post(y, bonus, g, ln_g, ln_b, tm=1024):
    n, w = y.shape
    tm = min(tm, n)
    row = pl.BlockSpec((tm, w), lambda i: (i, 0))
    vec = pl.BlockSpec((1, w), lambda i: (0, 0))
    return pl.pallas_call(
        _rwkv_post_kernel,
        grid=(n // tm,),
        in_specs=[row, row, row, vec, vec, pl.BlockSpec((w, w), lambda i: (0, 0))],
        out_specs=row,
        out_shape=jax.ShapeDtypeStruct((n, w), BF16),
        compiler_params=_params("parallel"),
        name="rwkv_post",
    )(y, bonus, g, ln_g.reshape(1, w), ln_b.reshape(1, w), _head_ones(w, RWKV_HEAD_DIM))


def rwkv7_mixer(h, bsz, seq, mu, w0, w_up, a0, a_up, g_up, k_k, k_a, r_k, ln_g, ln_b):
    r, k, v, kk, b, ld, g, bonus = rwkv_pre(h, bsz, seq, mu, w0, w_up, a0, a_up, g_up, k_k, k_a, r_k)
    y = rwkv_chunk(r, k, v, kk, b, ld, bsz, seq)
    return rwkv_post(y, bonus, g, ln_g, ln_b)


AB_IN = S5_WIDTH + NSA_WIDTH + 6 * NSA_KV_W + NSA_GATE_COLS
AB_IN_PADDED = -(-AB_IN // LANES) * LANES
NSA_GATE_COL_BLOCK = (AB_IN - NSA_GATE_COLS) // LANES
PROJ_TM = 512


def kernel(x, ab_w_in, ab_w_out, s5_lam_re, s5_lam_im, s5_log_dt, s5_b_re, s5_b_im, s5_c_re, s5_c_im, s5_d, s5_w_glu, s5_b_glu, nsa_pe_k, nsa_pe_v, nsa_ck_w1, nsa_ck_b1, nsa_ck_w2, nsa_cv_w1, nsa_cv_b1, nsa_cv_w2, cd_w_in, cd_w_out, rwkv_mu, rwkv_w0, rwkv_w_up, rwkv_a0, rwkv_a_up, rwkv_g_up, rwkv_k_k, rwkv_k_a, rwkv_r_k, rwkv_ln_g, rwkv_ln_b, ret_ln_g, ret_ln_b, ln1_g, ln1_b, ln2_g, ln2_b, moe_router, moe_bias, moe_w1, moe_w3, moe_w2, sh_w1, sh_w3, sh_w2):
    bsz, seq, d = x.shape
    assert (AB_IN - NSA_GATE_COLS) % LANES == 0
    xf = x.reshape(bsz * seq, d)
    x_in = xf
    for layer in range(DEPTH):
        i = layer // 2
        if layer % 2 == 0:
            w_in = jnp.pad(ab_w_in[i], ((0, 0), (0, AB_IN_PADDED - AB_IN))).astype(BF16)
            h, u3 = project(x_in, w_in, PROJ_TM, chunked=(S5_CHUNK, S5_WIDTH))
            y_1 = s5_mixer(h, u3, bsz, seq, s5_lam_re[i], s5_lam_im[i], s5_log_dt[i], s5_b_re[i], s5_b_im[i],
                           s5_c_re[i], s5_c_im[i], s5_d[i], s5_w_glu[i], s5_b_glu[i])
            y_2 = nsa_mixer(h, bsz, seq, NSA_GATE_COL_BLOCK, nsa_pe_k[i], nsa_pe_v[i], nsa_ck_w1[i], nsa_ck_b1[i],
                            nsa_ck_w2[i], nsa_cv_w1[i], nsa_cv_b1[i], nsa_cv_w2[i])
            w_out = ab_w_out[i]
        else:
            h = project(x_in, cd_w_in[i].astype(BF16), PROJ_TM)
            y_1 = rwkv7_mixer(h, bsz, seq, rwkv_mu[i], rwkv_w0[i], rwkv_w_up[i], rwkv_a0[i], rwkv_a_up[i],
                              rwkv_g_up[i], rwkv_k_k[i], rwkv_k_a[i], rwkv_r_k[i], rwkv_ln_g[i], rwkv_ln_b[i])
            y_2 = retention_mixer(h, bsz, seq, RWKV_COLS, ret_ln_g[i], ret_ln_b[i])
            w_out = cd_w_out[i]
        xf = out_proj_ln(y_1, y_2, w_out, xf, ln1_g[layer], ln1_b[layer])
        xf, x_in = moe_block(xf, moe_router[layer], moe_bias[layer], moe_w1[layer], moe_w3[layer],
                             moe_w2[layer], sh_w1[layer], sh_w3[layer], sh_w2[layer], ln2_g[layer], ln2_b[layer])
    return xf.reshape(bsz, seq, d)
```

```python
import functools
import math

import jax
import jax.numpy as jnp
import numpy as np
from jax import lax
from jax.experimental import pallas as pl
from jax.experimental.pallas import tpu as pltpu

F32 = jnp.float32
BF16 = jnp.bfloat16
HIGHEST = lax.Precision.HIGHEST
FP8 = jnp.float8_e4m3fn
FP8_MAX = 448.0
FP8_TINY = 1e-30

VMEM_LIMIT_BYTES = 52 * 1024 * 1024
LANES = 128

LN_EPS = 1e-5
DEPTH = 2
ALPHA = (2 * DEPTH) ** 0.25

S5_GROUPS, S5_GROUP_CH, S5_STATE = 32, 16, 64
S5_WIDTH = S5_GROUPS * S5_GROUP_CH
S5_CHUNK = 16
S5_PACK = 8
NSA_HEADS, NSA_KV_GROUPS, NSA_HEAD_DIM = 8, 2, 64
NSA_HPG = NSA_HEADS // NSA_KV_GROUPS
NSA_WIDTH = NSA_HEADS * NSA_HEAD_DIM
NSA_ROT_DIM = NSA_HEAD_DIM // 4
ROPE_THETA = 500000.0
CMP_BLOCK, CMP_STRIDE, CMP_HIDDEN = 32, 16, 128
SLC_BLOCK, N_SLC, WINDOW, Q_BLOCK = 64, 16, 512, 128
FORCE_SCORE = 1e6
MASK_VALUE = -1e30
RWKV_HEADS, RWKV_HEAD_DIM = 8, 64
RWKV_WIDTH = RWKV_HEADS * RWKV_HEAD_DIM
RWKV_LORA_W, RWKV_LORA_A, RWKV_LORA_G = 64, 64, 128
RWKV_COLS = 3 * RWKV_WIDTH + RWKV_LORA_W + RWKV_LORA_A + RWKV_LORA_G
RWKV_GN_EPS = 64e-5
RWKV_CHUNK = 64
RWKV_PACK = 4
RET_HEADS, RET_DK, RET_DV, RET_CHUNK = 4, 64, 128, 128
RET_THETA = 10000.0
RET_GN_EPS = 1e-5
N_EXPERTS, TOP_K, EXPERT_FF = 64, 8, 256
N_EXPERT_GROUPS, TOPK_GROUPS = 8, 4
EXPERTS_PER_GROUP = N_EXPERTS // N_EXPERT_GROUPS
ROUTED_SCALE = 2.5
MOE_EXPERTS_PER_STEP = 4


def _params(*sem):
    return pltpu.CompilerParams(dimension_semantics=sem, vmem_limit_bytes=VMEM_LIMIT_BYTES)


def _dot(a, b, **kw):
    return jnp.dot(a, b, preferred_element_type=F32, **kw)


def _dot_nt(a, b, **kw):
    return lax.dot_general(a, b, (((1,), (1,)), ((), ())), preferred_element_type=F32, **kw)


def _dot_tn(a, b, **kw):
    return lax.dot_general(a, b, (((0,), (0,)), ((), ())), preferred_element_type=F32, **kw)


def _dot_split(a, b, split, parts):
    rest = a if split == 'a' else b
    acc = None
    for _ in range(parts):
        piece = rest.astype(BF16)
        term = _dot(piece, b) if split == 'a' else _dot(a, piece)
        acc = term if acc is None else acc + term
        rest = rest - piece.astype(F32)
    return acc


def _run_interleaved(gens):
    results = [None] * len(gens)
    live = list(range(len(gens)))
    while live:
        for i in list(live):
            try:
                next(gens[i])
            except StopIteration as done:
                results[i] = done.value
                live.remove(i)
    return results


def _gelu(x):
    return 0.5 * x * (1.0 + jnp.tanh(math.sqrt(2.0 / math.pi) * (x + 0.044715 * (x * x * x))))


def _sigmoid(x):
    return 1.0 / (1.0 + jnp.exp(-x))


def _layer_norm_rows(z, g, b):
    mu = jnp.mean(z, axis=-1, keepdims=True)
    zc = z - mu
    var = jnp.mean(zc * zc, axis=-1, keepdims=True)
    return zc * lax.rsqrt(var + LN_EPS) * g + b


def _proj_kernel(x_ref, w_ref, o_ref, *chunked_ref):
    y = _dot(x_ref[...].astype(BF16), w_ref[...])
    o_ref[...] = y
    for c_ref in chunked_ref:
        rows, t, w = c_ref.shape
        c_ref[...] = y[:, :w].reshape(rows, t, w)


def project(x, w_bf16, tm, chunked=None):
    m, k = x.shape
    n = w_bf16.shape[1]
    out_specs = [pl.BlockSpec((tm, n), lambda i: (i, 0))]
    out_shape = [jax.ShapeDtypeStruct((m, n), F32)]
    if chunked is not None:
        t, w = chunked
        out_specs.append(pl.BlockSpec((tm // t, t, w), lambda i: (i, 0, 0)))
        out_shape.append(jax.ShapeDtypeStruct((m // t, t, w), F32))
    out = pl.pallas_call(
        _proj_kernel,
        grid=(m // tm,),
        in_specs=[pl.BlockSpec((tm, k), lambda i: (i, 0)), pl.BlockSpec((k, n), lambda i: (0, 0))],
        out_specs=out_specs,
        out_shape=out_shape,
        compiler_params=_params("parallel"),
        name="project",
    )(x, w_bf16)
    return out if chunked is not None else out[0]


def _out_proj_ln_kernel(ya_ref, yb_ref, wa_ref, wb_ref, x_ref, g_ref, b_ref, o_ref):
    mix = _dot(ya_ref[...], wa_ref[...]) + _dot(yb_ref[...], wb_ref[...])
    o_ref[...] = _layer_norm_rows(ALPHA * x_ref[...] + mix, g_ref[...], b_ref[...])


def out_proj_ln(ya, yb, w_out, x, g, b, tm=512):
    n, d = x.shape
    ka, kb = ya.shape[1], yb.shape[1]
    wa = w_out[:ka].astype(BF16)
    wb = w_out[ka:].astype(BF16)
    row = lambda i: (i, 0)
    fixed = lambda i: (0, 0)
    return pl.pallas_call(
        _out_proj_ln_kernel,
        grid=(n // tm,),
        in_specs=[pl.BlockSpec((tm, ka), row), pl.BlockSpec((tm, kb), row),
                  pl.BlockSpec((ka, d), fixed), pl.BlockSpec((kb, d), fixed),
                  pl.BlockSpec((tm, d), row), pl.BlockSpec((1, d), fixed), pl.BlockSpec((1, d), fixed)],
        out_specs=pl.BlockSpec((tm, d), row),
        out_shape=jax.ShapeDtypeStruct((n, d), F32),
        compiler_params=_params("parallel"),
        name="out_proj_ln",
    )(ya, yb, wa, wb, x, g.reshape(1, d), b.reshape(1, d))


def _router_kernel(x_ref, rt_ref, bias_ref, o_ref):
    tr = x_ref.shape[0]
    scores = _sigmoid(_dot_nt(rt_ref[...], x_ref[...], precision=HIGHEST))
    biased = scores + bias_ref[...]
    grp = biased.reshape(N_EXPERT_GROUPS, EXPERTS_PER_GROUP, tr)
    pos = lax.broadcasted_iota(jnp.int32, grp.shape, 1)
    m1 = jnp.max(grp, axis=1, keepdims=True)
    first = jnp.min(jnp.where(grp == m1, pos, EXPERTS_PER_GROUP), axis=1, keepdims=True)
    m2 = jnp.max(jnp.where(pos == first, -jnp.inf, grp), axis=1, keepdims=True)
    gscore = (m1 + m2).reshape(N_EXPERT_GROUPS, tr)
    gidx = lax.broadcasted_iota(jnp.int32, gscore.shape, 0)
    grank = jnp.zeros(gscore.shape, F32)
    for j in range(N_EXPERT_GROUPS):
        row = gscore[j:j + 1, :]
        grank = grank + jnp.where(gidx > j, jnp.where(row >= gscore, 1.0, 0.0), jnp.where(row > gscore, 1.0, 0.0))
    gkeep = jnp.where(grank < TOPK_GROUPS, 1.0, 0.0)
    keep = jnp.broadcast_to(gkeep[:, None, :], grp.shape).reshape(N_EXPERTS, tr)
    masked = jnp.where(keep > 0.5, biased, -jnp.inf)
    eidx = lax.broadcasted_iota(jnp.int32, masked.shape, 0)
    rank = jnp.zeros(masked.shape, F32)
    for j in range(N_EXPERTS):
        row = masked[j:j + 1, :]
        rank = rank + jnp.where(eidx > j, jnp.where(row >= masked, 1.0, 0.0), jnp.where(row > masked, 1.0, 0.0))
    gate = jnp.where(rank < TOP_K, scores, 0.0)
    gate = gate / jnp.sum(gate, axis=0, keepdims=True) * ROUTED_SCALE
    o_ref[...] = jnp.concatenate([gate, jnp.zeros((LANES - N_EXPERTS, tr), F32)], axis=0).T


def moe_router(x, router, bias, tr=512):
    n, d = x.shape
    return pl.pallas_call(
        _router_kernel,
        grid=(n // tr,),
        in_specs=[pl.BlockSpec((tr, d), lambda i: (i, 0)),
                  pl.BlockSpec((N_EXPERTS, d), lambda i: (0, 0)),
                  pl.BlockSpec((N_EXPERTS, 1), lambda i: (0, 0))],
        out_specs=pl.BlockSpec((tr, LANES), lambda i: (i, 0)),
        out_shape=jax.ShapeDtypeStruct((n, LANES), F32),
        compiler_params=_params("parallel"),
        name="moe_router",
    )(x, router.T, bias.reshape(N_EXPERTS, 1))


def _quantize_fp8(a, axes):
    amax = jnp.max(jnp.abs(a), axis=axes, keepdims=True)
    scale = jnp.maximum(amax, FP8_TINY) * (1.0 / FP8_MAX)
    return (a * (1.0 / scale)).astype(FP8), scale


def _swiglu_hidden(xq, x_scale, w1q, w3q, w_scale, gate=None):
    col1 = x_scale * w_scale[0:1, 0:1]
    col3 = x_scale * w_scale[1:2, 0:1]
    if gate is not None:
        col3 = col3 * gate
    h1 = _dot(xq, w1q) * col1
    return h1 * _sigmoid(h1) * (_dot(xq, w3q) * col3)


def _experts_ln_kernel(x_ref, gate_ref, w1_ref, w3_ref, ws_ref, w2_ref, sw1_ref, sw3_ref, sws_ref, sw2_ref,
                       g_ref, b_ref, o_ref, obf_ref, acc_ref, xq_ref, xs_ref):
    step = pl.program_id(1)
    per_step = w1_ref.shape[0]

    @pl.when(step == 0)
    def _():
        xq, xs = _quantize_fp8(x_ref[...], (1,))
        xq_ref[...] = xq
        xs_ref[...] = xs
        h = _swiglu_hidden(xq, xs, sw1_ref[...], sw3_ref[...], sws_ref[...])
        acc_ref[...] = _dot(h.astype(BF16), sw2_ref[...].astype(BF16))

    lane = lax.broadcasted_iota(jnp.int32, gate_ref.shape, 1)
    gates = gate_ref[...]
    xq, xs = xq_ref[...], xs_ref[...]
    hidden = []
    for j in range(per_step):
        gcol = jnp.sum(jnp.where(lane == step * per_step + j, gates, 0.0), axis=1, keepdims=True)
        hidden.append(_swiglu_hidden(xq, xs, w1_ref[j], w3_ref[j], ws_ref[j], gcol).astype(BF16))
    w2 = w2_ref[0].astype(BF16)
    acc_ref[...] += _dot(jnp.concatenate(hidden, axis=1), w2.reshape(per_step * w2.shape[1], w2.shape[2]))

    @pl.when(step == pl.num_programs(1) - 1)
    def _():
        y = _layer_norm_rows(ALPHA * x_ref[...] + acc_ref[...], g_ref[...], b_ref[...])
        o_ref[...] = y
        obf_ref[...] = y.astype(BF16)


def _quantize_expert_weights(w1, w3):
    w1q, s1 = _quantize_fp8(w1, (-2, -1))
    w3q, s3 = _quantize_fp8(w3, (-2, -1))
    scales = jnp.broadcast_to(jnp.concatenate([s1, s3], axis=-2), s1.shape[:-2] + (2, w1.shape[-1]))
    return w1q, w3q, scales


def moe_experts_ln(x, gates, w1, w3, w2_layers, layer, sw1, sw3, sw2, g, b, tm=1024):
    n, d = x.shape
    ne = w1.shape[0]
    w1q, w3q, ws = _quantize_expert_weights(w1, w3)
    sw1q, sw3q, sws = _quantize_expert_weights(sw1, sw3)
    tok = lambda i, e: (i, 0)
    fixed = lambda i, e: (0, 0)
    per_expert = lambda *blk: pl.BlockSpec((MOE_EXPERTS_PER_STEP,) + blk, lambda i, e: (e, 0, 0))
    return pl.pallas_call(
        _experts_ln_kernel,
        grid=(n // tm, ne // MOE_EXPERTS_PER_STEP),
        in_specs=[pl.BlockSpec((tm, d), tok), pl.BlockSpec((tm, LANES), tok),
                  per_expert(d, EXPERT_FF), per_expert(d, EXPERT_FF), per_expert(2, EXPERT_FF),
                  pl.BlockSpec((1, MOE_EXPERTS_PER_STEP, EXPERT_FF, d), lambda i, e: (layer, e, 0, 0)),
                  pl.BlockSpec((d, EXPERT_FF), fixed), pl.BlockSpec((d, EXPERT_FF), fixed),
                  pl.BlockSpec((2, EXPERT_FF), fixed), pl.BlockSpec((EXPERT_FF, d), fixed),
                  pl.BlockSpec((1, d), fixed), pl.BlockSpec((1, d), fixed)],
        out_specs=[pl.BlockSpec((tm, d), tok), pl.BlockSpec((tm, d), tok)],
        out_shape=[jax.ShapeDtypeStruct((n, d), F32), jax.ShapeDtypeStruct((n, d), BF16)],
        scratch_shapes=[pltpu.VMEM((tm, d), F32), pltpu.VMEM((tm, d), FP8), pltpu.VMEM((tm, 1), F32)],
        compiler_params=_params("parallel", "arbitrary"),
        name="moe_experts_ln",
    )(x, gates, w1q, w3q, ws, w2_layers, sw1q, sw3q, sws, sw2, g.reshape(1, d), b.reshape(1, d))


def moe_block(x, router, bias, w1, w3, w2_layers, layer, sw1, sw3, sw2, g, b):
    gates = moe_router(x, router, bias)
    return moe_experts_ln(x, gates, w1, w3, w2_layers, layer, sw1, sw3, sw2, g, b)


def _s5_tables(lam_re, lam_im, log_dt, b_re, b_im, c_re, c_im, n_chunk):
    t, h, p = S5_CHUNK, S5_GROUP_CH, S5_STATE
    dt = jnp.exp(log_dt.astype(F32))[:, None]
    den = lam_re ** 2 + lam_im ** 2

    def lam_pow(k):
        k = jnp.asarray(k, F32)[..., None, None]
        mag = jnp.exp(lam_re * dt * k)
        return mag * jnp.cos(lam_im * dt * k), mag * jnp.sin(lam_im * dt * k)

    lb_re, lb_im = lam_pow(1.0)
    f_re = ((lb_re - 1.0) * lam_re + lb_im * lam_im) / den
    f_im = (lb_im * lam_re - (lb_re - 1.0) * lam_im) / den
    bb_re = f_re[..., None] * b_re - f_im[..., None] * b_im
    bb_im = f_re[..., None] * b_im + f_im[..., None] * b_re
    pr, pi = lam_pow(jnp.arange(t))
    cl_re = c_re[None] * pr[:, :, None, :] - c_im[None] * pi[:, :, None, :]
    cl_im = c_re[None] * pi[:, :, None, :] + c_im[None] * pr[:, :, None, :]
    klag = jnp.einsum('tgop,gpi->tgoi', cl_re, bb_re) - jnp.einsum('tgop,gpi->tgoi', cl_im, bb_im)
    nb = S5_GROUPS // S5_PACK
    split = lambda a, axis: a.reshape(a.shape[:axis] + (nb, S5_PACK) + a.shape[axis + 1:])
    eye = jnp.eye(S5_PACK, dtype=F32)
    lag_t = jnp.transpose(split(klag, 1), (1, 0, 2, 4, 3))
    lag_t = (lag_t[:, :, :, :, None, :] * eye[None, None, :, None, :, None]).reshape(nb, t, LANES, LANES)
    qr, qi = lam_pow(t - 1 - jnp.arange(t))
    st_re = qr[..., None] * bb_re[None] - qi[..., None] * bb_im[None]
    st_im = qr[..., None] * bb_im[None] + qi[..., None] * bb_re[None]
    st = jnp.stack([st_re, st_im], axis=0)
    st_t = jnp.transpose(split(st, 2), (2, 1, 3, 5, 0, 4)).reshape(nb, t, LANES, 2 * p)
    er, ei = lam_pow(jnp.arange(t) + 1)
    x_re = c_re[None] * er[:, :, None, :] - c_im[None] * ei[:, :, None, :]
    x_im = c_re[None] * ei[:, :, None, :] + c_im[None] * er[:, :, None, :]
    cr = jnp.stack([x_re, -x_im], axis=0)
    cr_t = jnp.transpose(split(cr, 2), (2, 1, 0, 5, 3, 4)).reshape(nb, t, 2 * p, LANES)
    levels = max(1, int(math.log2(n_chunk)))
    sr, si = lam_pow(t * (2.0 ** jnp.arange(levels)))
    sr = sr.reshape(levels, nb, S5_PACK * p)
    si = si.reshape(levels, nb, S5_PACK * p)
    a1 = jnp.concatenate([sr, sr], axis=-1)
    a2 = jnp.concatenate([-si, si], axis=-1)
    scan = jnp.transpose(jnp.stack([a1, a2], axis=1), (2, 0, 1, 3))
    return lag_t, st_t, cr_t, scan.astype(F32)


def _s5_build_tables(lag_ref, st_ref, cr_ref, wtoe_ref, wstate_ref, wcross_ref):
    t = lag_ref.shape[1]
    p = S5_STATE
    kp = S5_PACK * p
    wtoe_ref[...] = jnp.zeros_like(wtoe_ref)
    for d in range(t):
        tile = lag_ref[0, d].astype(BF16)
        for j in range(t - d):
            wtoe_ref[j * LANES:(j + 1) * LANES, (j + d) * LANES:(j + d + 1) * LANES] = tile
    lane = lax.broadcasted_iota(jnp.int32, (LANES, LANES), 1)
    row_g = lax.broadcasted_iota(jnp.int32, (LANES, kp), 0) // S5_GROUP_CH
    same_s = row_g == lax.broadcasted_iota(jnp.int32, (LANES, kp), 1) // p
    for j in range(t):
        a = st_ref[0, j]
        swapped = pltpu.roll(a, p, 1)
        for c, both in enumerate((jnp.where(lane < p, a, swapped), jnp.where(lane < p, swapped, a))):
            wide = jnp.concatenate([both] * (kp // LANES), axis=1)
            wstate_ref[j * LANES:(j + 1) * LANES, c * kp:(c + 1) * kp] = jnp.where(same_s, wide, 0.0).astype(BF16)
    same_c = (lax.broadcasted_iota(jnp.int32, (kp, LANES), 0) // p
              == lax.broadcasted_iota(jnp.int32, (kp, LANES), 1) // S5_GROUP_CH)
    for i in range(t):
        a = cr_ref[0, i]
        for c in range(2):
            tall = jnp.concatenate([a[c * p:(c + 1) * p]] * S5_PACK, axis=0)
            wcross_ref[c * kp:(c + 1) * kp, i * LANES:(i + 1) * LANES] = jnp.where(same_c, tall, 0.0).astype(BF16)


def _s5_kernel(u_ref, lag_ref, st_ref, cr_ref, scan_ref, o_ref, wtoe_ref, wstate_ref, wcross_ref):
    @pl.when(pl.program_id(1) == 0)
    def _():
        _s5_build_tables(lag_ref, st_ref, cr_ref, wtoe_ref, wstate_ref, wcross_ref)

    n_chunk, t, _ = u_ref.shape
    x = jnp.concatenate([u_ref[:, j, :] for j in range(t)], axis=1).astype(BF16)
    local = _dot(x, wtoe_ref[...])
    state = _dot(x, wstate_ref[...])
    row = lax.broadcasted_iota(jnp.int32, state.shape, 0)
    s = jnp.where(row >= 1, pltpu.roll(state, 1, 0), 0.0)
    half = state.shape[1] // 2
    level = 0
    d = 1
    while d < n_chunk:
        mult = scan_ref[0, level]
        prev = jnp.where(row >= d, pltpu.roll(s, d, 0), 0.0)
        s = s + mult[0:1, :] * prev + mult[1:2, :] * pltpu.roll(prev, half, 1)
        d *= 2
        level += 1
    y = local + _dot(s.astype(BF16), wcross_ref[...])
    for i in range(t):
        o_ref[:, i, :] = y[:, i * LANES:(i + 1) * LANES]


def s5_scan(u3, bsz, lag_t, st_t, cr_t, scan):
    rows, t, w = u3.shape
    n_chunk = rows // bsz
    kp2 = 2 * S5_PACK * S5_STATE
    table = lambda a: pl.BlockSpec((1,) + a.shape[1:], lambda j, b: (j, 0, 0, 0))
    return pl.pallas_call(
        _s5_kernel,
        grid=(w // LANES, bsz),
        in_specs=[pl.BlockSpec((n_chunk, t, LANES), lambda j, b: (b, 0, j)),
                  table(lag_t), table(st_t), table(cr_t), table(scan)],
        out_specs=pl.BlockSpec((n_chunk, t, LANES), lambda j, b: (b, 0, j)),
        out_shape=jax.ShapeDtypeStruct(u3.shape, F32),
        scratch_shapes=[pltpu.VMEM((t * LANES, t * LANES), BF16), pltpu.VMEM((t * LANES, kp2), BF16),
                        pltpu.VMEM((kp2, t * LANES), BF16)],
        compiler_params=_params("arbitrary", "arbitrary"),
        name="s5_scan",
    )(u3, lag_t, st_t, cr_t, scan)


def _s5_post_kernel(y_ref, u_ref, d_ref, w_ref, b_ref, o_ref):
    u = u_ref[...]
    y = _gelu(y_ref[...].reshape(u.shape) + d_ref[...] * u)
    o_ref[...] = (y * _sigmoid(_dot(y.astype(BF16), w_ref[...]) + b_ref[...])).astype(o_ref.dtype)


def s5_post(y3, h, d_skip, w_glu, b_glu, tm=1024):
    rows, t, w = y3.shape
    n = rows * t
    tm = min(tm, n)
    row = lambda i: (i, 0)
    fixed = lambda i: (0, 0)
    return pl.pallas_call(
        _s5_post_kernel,
        grid=(n // tm,),
        in_specs=[pl.BlockSpec((tm // t, t, w), lambda i: (i, 0, 0)), pl.BlockSpec((tm, w), row),
                  pl.BlockSpec((1, w), fixed), pl.BlockSpec((w, w), fixed), pl.BlockSpec((1, w), fixed)],
        out_specs=pl.BlockSpec((tm, w), row),
        out_shape=jax.ShapeDtypeStruct((n, w), BF16),
        compiler_params=_params("parallel"),
        name="s5_post",
    )(y3, h, d_skip.reshape(1, w), w_glu.astype(BF16), b_glu.reshape(1, w))


def s5_mixer(h, u3, bsz, seq, lam_re, lam_im, log_dt, b_re, b_im, c_re, c_im, d_skip, w_glu, b_glu):
    tables = _s5_tables(lam_re, lam_im, log_dt, b_re, b_im, c_re, c_im, seq // S5_CHUNK)
    return s5_post(s5_scan(u3, bsz, *tables), h, d_skip, w_glu, b_glu)


def _rope_tables(pos, rot_dim, theta, head_dim, n_heads):
    half = rot_dim // 2
    f32 = np.float32
    inv_freq = f32(theta) ** (-np.arange(half, dtype=f32) / f32(half))
    ang = (pos.astype(f32)[:, None] * inv_freq[None, :]).astype(np.float64)
    cos, sin = np.cos(ang), np.sin(ang)
    rest = head_dim - rot_dim
    n = pos.shape[0]
    c = np.concatenate([cos, cos, np.ones((n, rest))], axis=1)
    s_up = np.concatenate([-sin, np.zeros((n, half + rest))], axis=1)
    s_dn = np.concatenate([np.zeros((n, half)), sin, np.zeros((n, rest))], axis=1)
    tile = lambda a: jnp.asarray(np.tile(a, (1, n_heads)), F32)
    return tile(c), tile(s_up), tile(s_dn)


def _rope_apply(x, c, s_up, s_dn, half):
    return x * c + pltpu.roll(x, LANES - half, 1) * s_up + pltpu.roll(x, half, 1) * s_dn


def _retention_tables():
    c = RET_CHUNK
    log_gamma = np.log(1.0 - 2.0 ** (-5.0 - np.arange(RET_HEADS, dtype=np.float64)))
    i = np.arange(c, dtype=np.float64)
    diff = i[:, None] - i[None, :]
    decay = np.where(diff >= 0, np.exp(diff[None] * log_gamma[:, None, None]), 0.0)
    qdec = np.repeat(np.exp((i + 1.0)[:, None] * log_gamma[None, :]), RET_DK, axis=1)
    kdec = np.repeat(np.exp((c - 1.0 - i)[:, None] * log_gamma[None, :]), RET_DK, axis=1)
    chunk_decay = [float(v) for v in np.exp(c * log_gamma)]
    return jnp.asarray(decay, F32), jnp.asarray(qdec, F32), jnp.asarray(kdec, F32), chunk_decay


def _retention_kernel(chunk_decay, q_ref, k_ref, v0_ref, v1_ref, g0_ref, g1_ref, c_ref, su_ref, sd_ref,
                      dec_ref, qdec_ref, kdec_ref, lng_ref, lnb_ref, o_ref, state_ref):
    @pl.when(pl.program_id(1) == 0)
    def _():
        state_ref[...] = jnp.zeros_like(state_ref)

    half = RET_DK // 2
    tabs = (c_ref[...], su_ref[...], sd_ref[...])
    q = jnp.concatenate([_rope_apply(q_ref[:, s:s + LANES], *tabs, half) for s in (0, LANES)], axis=1)
    k = jnp.concatenate([_rope_apply(k_ref[:, s:s + LANES], *tabs, half) for s in (0, LANES)], axis=1)
    k = k * (RET_DK ** -0.5)
    q_dec = q * qdec_ref[...]
    k_dec = k * kdec_ref[...]
    v = jnp.concatenate([v0_ref[...], v1_ref[...]], axis=1)
    gate = jnp.concatenate([g0_ref[...], g1_ref[...]], axis=1)
    states = [state_ref[h] for h in range(RET_HEADS)]

    def head(h):
        ks = slice(h * RET_DK, (h + 1) * RET_DK)
        vh = v[:, h * RET_DV:(h + 1) * RET_DV].astype(BF16)
        scores = _dot_nt(q[:, ks].astype(BF16), k[:, ks].astype(BF16)) * dec_ref[h]
        yield
        y = _dot(scores.astype(BF16), vh) + _dot(q_dec[:, ks].astype(BF16), states[h].astype(BF16))
        yield
        new_state = states[h] * chunk_decay[h] + _dot_tn(k_dec[:, ks].astype(BF16), vh)
        yield
        mu = jnp.mean(y, axis=-1, keepdims=True)
        yc = y - mu
        var = jnp.mean(yc * yc, axis=-1, keepdims=True)
        return yc * lax.rsqrt(var + RET_GN_EPS), new_state

    results = _run_interleaved([head(h) for h in range(RET_HEADS)])
    outs = [r[0] for r in results]
    for h, (_, new_state) in enumerate(results):
        state_ref[h] = new_state
    yn = jnp.concatenate(outs, axis=1) * lng_ref[...] + lnb_ref[...]
    o_ref[...] = (gate * _sigmoid(gate) * yn).astype(o_ref.dtype)


def retention_mixer(h, bsz, seq, col0, ln_g, ln_b):
    c = RET_CHUNK
    n_chunk = seq // c
    qk_w = RET_HEADS * RET_DK
    v_w = RET_HEADS * RET_DV
    assert col0 % qk_w == 0 and qk_w == 2 * LANES and v_w == 2 * qk_w
    cb = col0 // qk_w
    rc, rsu, rsd = _rope_tables(np.arange(seq), RET_DK, RET_THETA, RET_DK, 2)
    dec, qdec, kdec, chunk_decay = _retention_tables()
    row = lambda j: (lambda b, n: (b * n_chunk + n, j))
    pos = lambda b, n: (n, 0)
    fixed2 = lambda b, n: (0, 0)
    kern = functools.partial(_retention_kernel, chunk_decay)
    return pl.pallas_call(
        kern,
        grid=(bsz, n_chunk),
        in_specs=[pl.BlockSpec((c, qk_w), row(cb)), pl.BlockSpec((c, qk_w), row(cb + 1)),
                  pl.BlockSpec((c, qk_w), row(cb + 2)), pl.BlockSpec((c, qk_w), row(cb + 3)),
                  pl.BlockSpec((c, qk_w), row(cb + 4)), pl.BlockSpec((c, qk_w), row(cb + 5)),
                  pl.BlockSpec((c, LANES), pos), pl.BlockSpec((c, LANES), pos), pl.BlockSpec((c, LANES), pos),
                  pl.BlockSpec((RET_HEADS, c, c), lambda b, n: (0, 0, 0)),
                  pl.BlockSpec((c, qk_w), fixed2), pl.BlockSpec((c, qk_w), fixed2),
                  pl.BlockSpec((1, v_w), fixed2), pl.BlockSpec((1, v_w), fixed2)],
        out_specs=pl.BlockSpec((c, v_w), lambda b, n: (b * n_chunk + n, 0)),
        out_shape=jax.ShapeDtypeStruct((bsz * seq, v_w), BF16),
        scratch_shapes=[pltpu.VMEM((RET_HEADS, RET_DK, RET_DV), F32)],
        compiler_params=_params("parallel", "arbitrary"),
        name="retention",
    )(h, h, h, h, h, h, rc, rsu, rsd, dec, qdec, kdec, ln_g.reshape(1, v_w), ln_b.reshape(1, v_w))


NSA_KV_W = NSA_KV_GROUPS * NSA_HEAD_DIM
NSA_GATE_COLS = 3 * NSA_HEADS


def _nsa_prep_kernel(q_ref, kvc_ref, kvs_ref, kvw_ref, c_ref, su_ref, sd_ref,
                     qo_ref, kc_ref, vc_ref, ks_ref, vs_ref, kw_ref, vw_ref):
    half = NSA_ROT_DIM // 2
    tabs = (c_ref[...], su_ref[...], sd_ref[...])
    scale = NSA_HEAD_DIM ** -0.5 * math.log2(math.e)
    q = jnp.concatenate(
        [_rope_apply(q_ref[:, s:s + LANES], *tabs, half) * scale for s in range(0, NSA_WIDTH, LANES)], axis=1)
    qo_ref[0] = q.T.astype(qo_ref.dtype)

    def split(x, o_ref):
        for g in range(NSA_KV_GROUPS):
            o_ref[0, g] = x[:, g * NSA_HEAD_DIM:(g + 1) * NSA_HEAD_DIM].astype(o_ref.dtype)

    def split_t(x, o_ref):
        xt = x.T
        for g in range(NSA_KV_GROUPS):
            o_ref[0, g] = xt[g * NSA_HEAD_DIM:(g + 1) * NSA_HEAD_DIM, :].astype(o_ref.dtype)

    split(kvc_ref[:, :NSA_KV_W], kc_ref)
    split(kvc_ref[:, NSA_KV_W:], vc_ref)
    split(_rope_apply(kvs_ref[:, :NSA_KV_W], *tabs, half), ks_ref)
    split_t(kvs_ref[:, NSA_KV_W:], vs_ref)
    split(_rope_apply(kvw_ref[:, :NSA_KV_W], *tabs, half), kw_ref)
    split_t(kvw_ref[:, NSA_KV_W:], vw_ref)


def nsa_prep(h, bsz, seq, tl=512):
    tl = min(tl, seq)
    nl = seq // tl
    rc, rsu, rsd = _rope_tables(np.arange(seq), NSA_ROT_DIM, ROPE_THETA, NSA_HEAD_DIM, LANES // NSA_HEAD_DIM)
    row = lambda j: (lambda b, l: (b * nl + l, j))
    pos = lambda b, l: (l, 0)
    kv_out = pl.BlockSpec((1, NSA_KV_GROUPS, tl, NSA_HEAD_DIM), lambda b, l: (b, 0, l, 0))
    kv_shape = lambda dt: jax.ShapeDtypeStruct((bsz, NSA_KV_GROUPS, seq, NSA_HEAD_DIM), dt)
    vt_out = pl.BlockSpec((1, NSA_KV_GROUPS, NSA_HEAD_DIM, tl), lambda b, l: (b, 0, 0, l))
    vt_shape = jax.ShapeDtypeStruct((bsz, NSA_KV_GROUPS, NSA_HEAD_DIM, seq), BF16)
    two = 2 * NSA_KV_W
    return pl.pallas_call(
        _nsa_prep_kernel,
        grid=(bsz, nl),
        in_specs=[pl.BlockSpec((tl, NSA_WIDTH), row(1)),
                  pl.BlockSpec((tl, two), row(4)), pl.BlockSpec((tl, two), row(5)), pl.BlockSpec((tl, two), row(6)),
                  pl.BlockSpec((tl, LANES), pos), pl.BlockSpec((tl, LANES), pos), pl.BlockSpec((tl, LANES), pos)],
        out_specs=[pl.BlockSpec((1, NSA_WIDTH, tl), lambda b, l: (b, 0, l)),
                   kv_out, kv_out, kv_out, vt_out, kv_out, vt_out],
        out_shape=[jax.ShapeDtypeStruct((bsz, NSA_WIDTH, seq), BF16),
                   kv_shape(F32), kv_shape(F32), kv_shape(BF16), vt_shape, kv_shape(BF16), vt_shape],
        compiler_params=_params("parallel", "parallel"),
        name="nsa_prep",
    )(h, h, h, h, rc, rsu, rsd)


def _nsa_compress_kernel(hk_ref, hv_ref, pek_ref, pev_ref, kw1_ref, kb1_ref, kw2_ref, vw1_ref, vb1_ref, vw2_ref,
                         c_ref, su_ref, sd_ref, ko_ref, vo_ref):
    def mlp(h_ref, pe_ref, w1_ref, b1_ref, w2_ref):
        hb = h_ref[0, 0]
        rows = hb.shape[0]
        first = _dot((hb + pe_ref[0:1, :]).astype(BF16), w1_ref[0])
        second = _dot((hb + pe_ref[1:2, :]).astype(BF16), w1_ref[1])
        hid = _gelu(first + pltpu.roll(second, rows - 1, 0) + b1_ref[...])
        return _dot(hid.astype(BF16), w2_ref[...])

    kc = _rope_apply(mlp(hk_ref, pek_ref, kw1_ref, kb1_ref, kw2_ref), c_ref[...], su_ref[...], sd_ref[...],
                     NSA_ROT_DIM // 2)
    vc = mlp(hv_ref, pev_ref, vw1_ref, vb1_ref, vw2_ref)
    ko_ref[0, 0] = kc[:, :NSA_HEAD_DIM].astype(ko_ref.dtype)
    vo_ref[0, 0] = vc.T[:NSA_HEAD_DIM, :].astype(vo_ref.dtype)


def nsa_compress(kc, vc, pe_k, pe_v, ck_w1, ck_b1, ck_w2, cv_w1, cv_b1, cv_w2):
    bsz, grp, seq, d = kc.shape
    n_rows = seq // CMP_STRIDE
    flat = CMP_STRIDE * d
    cmp_end = np.arange(n_rows) * CMP_STRIDE + CMP_BLOCK - 1
    rc, rsu, rsd = _rope_tables(cmp_end, NSA_ROT_DIM, ROPE_THETA, NSA_HEAD_DIM, LANES // NSA_HEAD_DIM)
    pad_w2 = lambda w: jnp.pad(w, ((0, 0), (0, LANES - d))).astype(BF16)
    blk = pl.BlockSpec((1, 1, n_rows, flat), lambda b, g: (b, g, 0, 0))
    f2 = lambda b, g: (0, 0)
    f3 = lambda b, g: (0, 0, 0)
    w_specs = [pl.BlockSpec((2, flat, CMP_HIDDEN), f3), pl.BlockSpec((1, CMP_HIDDEN), f2),
               pl.BlockSpec((CMP_HIDDEN, LANES), f2)]
    return pl.pallas_call(
        _nsa_compress_kernel,
        grid=(bsz, grp),
        in_specs=[blk, blk, pl.BlockSpec((2, flat), f2), pl.BlockSpec((2, flat), f2)] + w_specs + w_specs
                 + [pl.BlockSpec((n_rows, LANES), f2)] * 3,
        out_specs=[pl.BlockSpec((1, 1, n_rows, d), lambda b, g: (b, g, 0, 0)),
                   pl.BlockSpec((1, 1, d, n_rows), lambda b, g: (b, g, 0, 0))],
        out_shape=[jax.ShapeDtypeStruct((bsz, grp, n_rows, d), BF16),
                   jax.ShapeDtypeStruct((bsz, grp, d, n_rows), BF16)],
        compiler_params=_params("parallel", "parallel"),
        name="nsa_compress",
    )(kc.reshape(bsz, grp, n_rows, flat), vc.reshape(bsz, grp, n_rows, flat),
      pe_k.reshape(2, flat), pe_v.reshape(2, flat),
      ck_w1.reshape(2, flat, CMP_HIDDEN).astype(BF16), ck_b1.reshape(1, CMP_HIDDEN), pad_w2(ck_w2),
      cv_w1.reshape(2, flat, CMP_HIDDEN).astype(BF16), cv_b1.reshape(1, CMP_HIDDEN), pad_w2(cv_w2),
      rc, rsu, rsd)


def _per_head(x):
    return jnp.concatenate([x] * NSA_HPG, axis=1)


def _nsa_attn_kernel(seq, tk, qt_ref, gate_ref, kc_ref, vct_ref, ks_ref, vst_ref, kw_ref, vwt_ref, mmapt_ref,
                     o_ref, sel_ref):
    n_blk = seq // SLC_BLOCK
    n_sel = min(N_SLC, n_blk)
    hd = NSA_HEAD_DIM
    w = NSA_HPG * hd
    groups = range(NSA_KV_GROUPS)
    q0 = pl.program_id(1) * Q_BLOCK
    t_l = q0 + lax.broadcasted_iota(jnp.int32, (1, Q_BLOCK), 1)

    def select(g):
        qg = qt_ref[0, g * w:(g + 1) * w, :]
        qst = jnp.concatenate([qg[h * hd:(h + 1) * hd, :] for h in range(NSA_HPG)], axis=1)
        kc = kc_ref[0, g]
        n_cmp = kc.shape[0]
        cmp_end = lax.broadcasted_iota(jnp.int32, (n_cmp, 1), 0) * CMP_STRIDE + (CMP_BLOCK - 1)
        s = _dot(kc, qst) + _per_head(jnp.where(cmp_end <= t_l, 0.0, MASK_VALUE))
        yield
        p = jnp.exp2(s - jnp.max(s, axis=0, keepdims=True))
        any_key = _per_head(jnp.where(t_l >= CMP_BLOCK - 1, 1.0, 0.0))
        inv_l = any_key / jnp.sum(p, axis=0, keepdims=True)
        o_cmp = _dot(vct_ref[0, g], p.astype(BF16)) * inv_l
        yield
        p = p * inv_l
        imp = p[:, 0:Q_BLOCK]
        for h in range(1, NSA_HPG):
            imp = imp + p[:, h * Q_BLOCK:(h + 1) * Q_BLOCK]
        imp_slc = _dot_split(mmapt_ref[...], imp, 'b', 3)
        yield
        blk = lax.broadcasted_iota(jnp.int32, (n_blk, 1), 0)
        cur = t_l // SLC_BLOCK
        score = jnp.where(blk == 0, FORCE_SCORE,
                          jnp.where(blk == cur, FORCE_SCORE, jnp.where(blk == cur - 1, FORCE_SCORE, imp_slc)))
        score = jnp.where(blk * SLC_BLOCK <= t_l, score, -FORCE_SCORE)
        sel = jnp.zeros((n_blk, Q_BLOCK), F32)
        for _ in range(n_sel):
            best = jnp.max(score, axis=0, keepdims=True)
            idx = jnp.min(jnp.where(score == best, blk, n_blk), axis=0, keepdims=True)
            pick = blk == idx
            sel = jnp.where(pick, 1.0, sel)
            score = jnp.where(pick, -jnp.inf, score)
            yield
        sel_ref[g] = sel
        return qst, o_cmp

    selected = _run_interleaved([select(g) for g in groups])
    qst = [r[0] for r in selected]
    o_cmp = [r[1] for r in selected]

    blocks_per_tile = tk // SLC_BLOCK

    def slc_tile(kt, carry):
        k0 = pl.multiple_of(kt * tk, tk)
        causal = k0 + lax.broadcasted_iota(jnp.int32, (tk, 1), 0) <= t_l
        scores = [_dot(ks_ref[0, g, pl.ds(k0, tk), :], qst[g]) for g in groups]
        out = []
        for g in groups:
            m, l, acc = carry[g]
            chosen = jnp.concatenate(
                [jnp.broadcast_to(sel_ref[g, pl.ds(kt * blocks_per_tile + j, 1), :], (SLC_BLOCK, Q_BLOCK))
                 for j in range(blocks_per_tile)], axis=0)
            bias = jnp.where(causal, jnp.where(chosen > 0.5, 0.0, MASK_VALUE), MASK_VALUE)
            s = scores[g] + _per_head(bias)
            m_new = jnp.maximum(m, jnp.max(s, axis=0, keepdims=True))
            alpha = jnp.exp2(m - m_new)
            p = jnp.exp2(s - m_new)
            l = alpha * l + jnp.sum(p, axis=0, keepdims=True)
            acc = alpha * acc + _dot(vst_ref[0, g, :, pl.ds(k0, tk)], p.astype(BF16))
            out.append((m_new, l, acc))
        return tuple(out)

    n_tiles = (q0 + Q_BLOCK + tk - 1) // tk
    cols = NSA_HPG * Q_BLOCK
    init = tuple((jnp.full((1, cols), MASK_VALUE, F32), jnp.zeros((1, cols), F32), jnp.zeros((hd, cols), F32))
                 for _ in groups)
    slc = lax.fori_loop(0, n_tiles, slc_tile, init)

    band = WINDOW + Q_BLOCK
    w0 = pl.multiple_of(jnp.maximum(q0 - WINDOW, 0), Q_BLOCK)
    kpos = w0 + lax.broadcasted_iota(jnp.int32, (band, 1), 0)
    win_bias = _per_head(jnp.where(kpos <= t_l, jnp.where(kpos > t_l - WINDOW, 0.0, MASK_VALUE), MASK_VALUE))
    sig_t = _sigmoid(gate_ref[...]).T

    def finish(g):
        s = _dot(kw_ref[0, g, pl.ds(w0, band), :], qst[g]) + win_bias
        yield
        p = jnp.exp2(s - jnp.max(s, axis=0, keepdims=True))
        o_win = _dot(vwt_ref[0, g, :, pl.ds(w0, band)], p.astype(BF16)) / jnp.sum(p, axis=0, keepdims=True)
        yield
        _, l_slc, acc_slc = slc[g]

        def gate(branch):
            first = (g * NSA_HPG) * 3 + branch
            return jnp.concatenate([sig_t[first + 3 * h:first + 3 * h + 1, :] for h in range(NSA_HPG)], axis=1)

        out_t = gate(0) * o_cmp[g] + gate(1) * (acc_slc / l_slc) + gate(2) * o_win
        pairs = []
        for h in range(0, NSA_HPG, 2):
            two = jnp.concatenate([out_t[:, h * Q_BLOCK:(h + 1) * Q_BLOCK],
                                   out_t[:, (h + 1) * Q_BLOCK:(h + 2) * Q_BLOCK]], axis=0)
            pairs.append(two.T)
        o_ref[:, g * w:(g + 1) * w] = jnp.concatenate(pairs, axis=1).astype(o_ref.dtype)

    _run_interleaved([finish(g) for g in groups])


def _nsa_pool_matrix(seq):
    n_blk = seq // SLC_BLOCK
    n_rows = seq // CMP_STRIDE
    per_stride = SLC_BLOCK // CMP_STRIDE
    span = CMP_BLOCK // CMP_STRIDE
    pool = np.zeros((n_blk, n_rows), np.float32)
    for j in range(n_blk):
        for m in range(per_stride):
            for n in range(span):
                c = per_stride * j + m + n - (span - 1)
                if 0 <= c < n_rows - 1:
                    pool[j, c] += 1.0
    return jnp.asarray(pool, BF16)


def nsa_attention(qt, h, gate_col_block, k_cmp, v_cmp_t, ks, vs_t, kw, vw_t, bsz, seq, tk=1024):
    tk = min(tk, seq)
    nq = seq // Q_BLOCK
    pool = _nsa_pool_matrix(seq)
    n_rows = k_cmp.shape[2]
    d = NSA_HEAD_DIM
    qblk = lambda b, i: (b * nq + i, 0)
    whole = lambda *shape: pl.BlockSpec((1, NSA_KV_GROUPS) + shape, lambda b, i: (b, 0, 0, 0))
    kern = functools.partial(_nsa_attn_kernel, seq, tk)
    return pl.pallas_call(
        kern,
        grid=(bsz, nq),
        in_specs=[pl.BlockSpec((1, NSA_WIDTH, Q_BLOCK), lambda b, i: (b, 0, i)),
                  pl.BlockSpec((Q_BLOCK, LANES), lambda b, i: (b * nq + i, gate_col_block)),
                  whole(n_rows, d), whole(d, n_rows), whole(seq, d), whole(d, seq), whole(seq, d), whole(d, seq),
                  pl.BlockSpec(pool.shape, lambda b, i: (0, 0))],
        out_specs=pl.BlockSpec((Q_BLOCK, NSA_WIDTH), qblk),
        out_shape=jax.ShapeDtypeStruct((bsz * seq, NSA_WIDTH), BF16),
        scratch_shapes=[pltpu.VMEM((NSA_KV_GROUPS, seq // SLC_BLOCK, Q_BLOCK), F32)],
        compiler_params=_params("parallel", "arbitrary"),
        name="nsa_attention",
    )(qt, h, k_cmp, v_cmp_t, ks, vs_t, kw, vw_t, pool)


def nsa_mixer(h, bsz, seq, gate_col_block, pe_k, pe_v, ck_w1, ck_b1, ck_w2, cv_w1, cv_b1, cv_w2):
    qt, kc, vc, ks, vs_t, kw, vw_t = nsa_prep(h, bsz, seq)
    k_cmp, v_cmp_t = nsa_compress(kc, vc, pe_k, pe_v, ck_w1, ck_b1, ck_w2, cv_w1, cv_b1, cv_w2)
    return nsa_attention(qt, h, gate_col_block, k_cmp, v_cmp_t, ks, vs_t, kw, vw_t, bsz, seq)


def _head_ones(width, head_dim):
    idx = np.arange(width) // head_dim
    return jnp.asarray(idx[:, None] == idx[None, :], BF16)


def _softplus(x):
    return jnp.maximum(x, 0.0) + jnp.log(1.0 + jnp.exp(-jnp.abs(x)))


def _rwkv_pre_kernel(p_ref, prev_ref, mu_ref, w0_ref, wup_ref, a0_ref, aup_ref, gup_ref, kk_ref, ka_ref, rk_ref,
                     ones_ref, r_o, k_o, v_o, kk_o, b_o, ld_o, g_o, bonus_o):
    w = RWKV_WIDTH
    p = p_ref[...]
    first_row = jnp.where(pl.program_id(1) == 0, 0.0, prev_ref[7:8, :])
    is_row0 = lax.broadcasted_iota(jnp.int32, p.shape, 0) == 0
    prev = jnp.where(is_row0, first_row, pltpu.roll(p, 1, 0))
    ps = p + (prev - p) * mu_ref[...]
    r, k, v = ps[:, 0:w], ps[:, w:2 * w], ps[:, 2 * w:3 * w]
    o = 3 * w
    w_lo = ps[:, o:o + RWKV_LORA_W]
    a_lo = ps[:, o + RWKV_LORA_W:o + RWKV_LORA_W + RWKV_LORA_A]
    g_lo = ps[:, o + RWKV_LORA_W + RWKV_LORA_A:]
    wlog = -_softplus(-(w0_ref[...] + _dot(jnp.tanh(w_lo).astype(BF16), wup_ref[...]))) - 0.5
    a = _sigmoid(a0_ref[...] + _dot(a_lo.astype(BF16), aup_ref[...]))
    g = _dot(_sigmoid(g_lo).astype(BF16), gup_ref[...])
    kk = k * kk_ref[...]
    norm = jnp.sqrt(_dot_split(kk * kk, ones_ref[...], 'a', 2))
    kk = kk / jnp.maximum(norm, 1e-12)
    k2 = k * (1.0 + (a - 1.0) * ka_ref[...])
    r_o[...] = r
    k_o[...] = k2
    v_o[...] = v
    kk_o[...] = kk
    b_o[...] = kk * a
    ld_o[...] = -jnp.exp(wlog)
    g_o[...] = g
    bonus_o[...] = _dot_split(r * k2 * rk_ref[...], ones_ref[...], 'a', 2) * v


def rwkv_pre(h, bsz, seq, mu, w0, w_up, a0, a_up, g_up, k_k, k_a, r_k, tl=512):
    tl = min(tl, seq)
    nl = seq // tl
    w = RWKV_WIDTH
    cols = RWKV_COLS
    ones = _head_ones(w, RWKV_HEAD_DIM)
    f2 = lambda b, l: (0, 0)
    vec = pl.BlockSpec((1, w), f2)
    out_spec = pl.BlockSpec((tl, w), lambda b, l: (b * nl + l, 0))
    out_shape = jax.ShapeDtypeStruct((bsz * seq, w), F32)
    return pl.pallas_call(
        _rwkv_pre_kernel,
        grid=(bsz, nl),
        in_specs=[pl.BlockSpec((tl, cols), lambda b, l: (b * nl + l, 0)),
                  pl.BlockSpec((8, cols), lambda b, l: (jnp.maximum((b * seq + l * tl) // 8 - 1, 0), 0)),
                  pl.BlockSpec((1, cols), f2), vec, pl.BlockSpec((RWKV_LORA_W, w), f2),
                  vec, pl.BlockSpec((RWKV_LORA_A, w), f2), pl.BlockSpec((RWKV_LORA_G, w), f2),
                  vec, vec, vec, pl.BlockSpec((w, w), f2)],
        out_specs=[out_spec] * 8,
        out_shape=[out_shape] * 8,
        compiler_params=_params("parallel", "parallel"),
        name="rwkv_pre",
    )(h, h, mu.reshape(1, cols), w0.reshape(1, w), w_up.astype(BF16), a0.reshape(1, w), a_up.astype(BF16),
      g_up.astype(BF16), k_k.reshape(1, w), k_a.reshape(1, w), r_k.reshape(1, w), ones)


def _rwkv_masks():
    t, pk = RWKV_CHUNK, RWKV_PACK
    n = t * pk
    ri = np.arange(n)
    same = (ri[:, None] // t) == (ri[None, :] // t)
    tt, ss = ri[:, None] % t, ri[None, :] % t
    levels = []
    k = 1
    while k < t:
        levels.append(same & (tt // (2 * k) == ss // (2 * k)) & ((tt // k) % 2 == 1) & ((ss // k) % 2 == 0))
        k *= 2
    lvl = np.stack(levels).astype(np.float32)
    tri = (np.arange(t)[:, None] >= np.arange(t)[None, :]).astype(np.float32)
    head_lane = ((ri[:, None] // t) == (np.arange(pk * RWKV_HEAD_DIM)[None, :] // RWKV_HEAD_DIM)).astype(np.float32)
    return (jnp.asarray(tri, BF16), jnp.asarray(head_lane), jnp.asarray(same.astype(np.float32)), jnp.asarray(lvl))


def _rwkv_chain(r, k, v, kk, b, ld, st, tri, hl, bd, lvl_ref):
    t, pk = RWKV_CHUNK, RWKV_PACK
    n = t * pk
    c = _dot_split(tri, ld, 'b', 3)
    yield
    c_end = c[t - 1:t, :]
    e_neg = jnp.exp(-c)
    e_end = jnp.exp(c_end - c)
    kkd = (kk * jnp.exp(c - ld)).astype(BF16)
    rd = (r * jnp.exp(c)).astype(BF16)

    def big(x):
        return (jnp.concatenate([x] * pk, axis=0) * hl).astype(BF16)

    st_b = st.astype(BF16)
    v_big = big(v)
    a_all = _dot_nt(jnp.concatenate([kkd, rd], axis=0),
                    jnp.concatenate([big(k * e_neg), big(b * e_neg)], axis=0))
    yield
    ti = lax.broadcasted_iota(jnp.int32, (t, n), 0)
    si = lax.broadcasted_iota(jnp.int32, (t, n), 1) % t
    strict = ti > si
    incl = ti >= si
    a_kk = jnp.where(strict, a_all[:t, :n], 0.0)
    a_kb = jnp.where(strict, a_all[:t, n:], 0.0)
    a_rk = jnp.where(incl, a_all[t:, :n], 0.0)
    a_rb = jnp.where(incl, a_all[t:, n:], 0.0)
    rhs = _dot(kkd, st_b) + _dot(a_kk.astype(BF16), v_big)
    yield
    a_bd = jnp.concatenate([a_kb] * pk, axis=0) * bd
    m = jnp.where(lax.broadcasted_iota(jnp.int32, (n, n), 0) == lax.broadcasted_iota(jnp.int32, (n, n), 1), 1.0, 0.0)
    for lv in range(lvl_ref.shape[0]):
        mb = m.astype(BF16)
        ma = _dot(mb, (a_bd * lvl_ref[lv]).astype(BF16)).astype(BF16)
        yield
        m = m - _dot(ma, mb)
        yield
    u_big = _dot(m.astype(BF16), big(rhs))
    yield
    u = u_big[0:t]
    for h in range(1, pk):
        u = u + u_big[h * t:(h + 1) * t]
    y = _dot(rd, st_b) + _dot(a_rk.astype(BF16), v_big) - _dot(a_rb.astype(BF16), big(u))
    yield
    decay_col = jnp.broadcast_to(jnp.exp(c_end), st.shape).T
    kb_end = jnp.concatenate([k * e_end, -(b * e_end)], axis=0).astype(BF16)
    vu = jnp.concatenate([v, u], axis=0).astype(BF16)
    return y, decay_col * st + bd * _dot_tn(kb_end, vu)


def _rwkv_chunk_kernel(r_ref, k_ref, v_ref, kk_ref, b_ref, ld_ref, tri_ref, hl_ref, bd_ref, lvl_ref, y_ref, st_ref):
    @pl.when(pl.program_id(0) == 0)
    def _():
        st_ref[...] = jnp.zeros_like(st_ref)

    wp = RWKV_PACK * RWKV_HEAD_DIM
    tri, hl, bd = tri_ref[...], hl_ref[...], bd_ref[...]
    n_pack = r_ref.shape[2] // wp
    where = [(bi, slice(g * wp, (g + 1) * wp)) for bi in range(r_ref.shape[0]) for g in range(n_pack)]
    loaded = [tuple(ref[bi, :, cols] for ref in (r_ref, k_ref, v_ref, kk_ref, b_ref, ld_ref)) + (st_ref[i],)
              for i, (bi, cols) in enumerate(where)]
    results = _run_interleaved([_rwkv_chain(*args, tri, hl, bd, lvl_ref) for args in loaded])
    for i, ((bi, cols), (y, st_new)) in enumerate(zip(where, results)):
        y_ref[bi, :, cols] = y
        st_ref[i] = st_new


def rwkv_chunk(r, k, v, kk, b, ld, bsz, seq):
    t, pk = RWKV_CHUNK, RWKV_PACK
    n_chunk = seq // t
    w = RWKV_WIDTH
    wp = pk * RWKV_HEAD_DIM
    assert t == RWKV_HEAD_DIM
    tri, hl, bd, lvl = _rwkv_masks()
    blk = pl.BlockSpec((bsz, t, w), lambda c: (0, c, 0))
    f2 = lambda c: (0, 0)
    shaped = lambda a: a.reshape(bsz, seq, w)
    y = pl.pallas_call(
        _rwkv_chunk_kernel,
        grid=(n_chunk,),
        in_specs=[blk] * 6 + [pl.BlockSpec(tri.shape, f2), pl.BlockSpec(hl.shape, f2), pl.BlockSpec(bd.shape, f2),
                              pl.BlockSpec(lvl.shape, lambda c: (0, 0, 0))],
        out_specs=blk,
        out_shape=jax.ShapeDtypeStruct((bsz, seq, w), F32),
        scratch_shapes=[pltpu.VMEM((bsz * (w // wp), wp, wp), F32)],
        compiler_params=_params("arbitrary"),
        name="rwkv_chunk",
    )(shaped(r), shaped(k), shaped(v), shaped(kk), shaped(b), shaped(ld), tri, hl, bd, lvl)
    return y.reshape(bsz * seq, w)


def _rwkv_post_kernel(y_ref, bonus_ref, g_ref, lng_ref, lnb_ref, ones_ref, o_ref):
    y = y_ref[...]
    inv = 1.0 / RWKV_HEAD_DIM
    mu = _dot_split(y, ones_ref[...], 'a', 2) * inv
    yc = y - mu
    var = _dot_split(yc * yc, ones_ref[...], 'a', 2) * inv
    yn = yc * lax.rsqrt(var + RWKV_GN_EPS) * lng_ref[...] + lnb_ref[...]
    o_ref[...] = ((yn + bonus_ref[...]) * g_ref[...]).astype(o_ref.dtype)


def rwkv_post(y, bonus, g, ln_g, ln_b, tm=1024):
    n, w = y.shape
    tm = min(tm, n)
    row = pl.BlockSpec((tm, w), lambda i: (i, 0))
    vec = pl.BlockSpec((1, w), lambda i: (0, 0))
    return pl.pallas_call(
        _rwkv_post_kernel,
        grid=(n // tm,),
        in_specs=[row, row, row, vec, vec, pl.BlockSpec((w, w), lambda i: (0, 0))],
        out_specs=row,
        out_shape=jax.ShapeDtypeStruct((n, w), BF16),
        compiler_params=_params("parallel"),
        name="rwkv_post",
    )(y, bonus, g, ln_g.reshape(1, w), ln_b.reshape(1, w), _head_ones(w, RWKV_HEAD_DIM))


def rwkv7_mixer(h, bsz, seq, mu, w0, w_up, a0, a_up, g_up, k_k, k_a, r_k, ln_g, ln_b):
    r, k, v, kk, b, ld, g, bonus = rwkv_pre(h, bsz, seq, mu, w0, w_up, a0, a_up, g_up, k_k, k_a, r_k)
    y = rwkv_chunk(r, k, v, kk, b, ld, bsz, seq)
    return rwkv_post(y, bonus, g, ln_g, ln_b)


AB_IN = S5_WIDTH + NSA_WIDTH + 6 * NSA_KV_W + NSA_GATE_COLS
AB_IN_PADDED = -(-AB_IN // LANES) * LANES
NSA_GATE_COL_BLOCK = (AB_IN - NSA_GATE_COLS) // LANES
PROJ_TM = 512


def kernel(x, ab_w_in, ab_w_out, s5_lam_re, s5_lam_im, s5_log_dt, s5_b_re, s5_b_im, s5_c_re, s5_c_im, s5_d, s5_w_glu, s5_b_glu, nsa_pe_k, nsa_pe_v, nsa_ck_w1, nsa_ck_b1, nsa_ck_w2, nsa_cv_w1, nsa_cv_b1, nsa_cv_w2, cd_w_in, cd_w_out, rwkv_mu, rwkv_w0, rwkv_w_up, rwkv_a0, rwkv_a_up, rwkv_g_up, rwkv_k_k, rwkv_k_a, rwkv_r_k, rwkv_ln_g, rwkv_ln_b, ret_ln_g, ret_ln_b, ln1_g, ln1_b, ln2_g, ln2_b, moe_router, moe_bias, moe_w1, moe_w3, moe_w2, sh_w1, sh_w3, sh_w2):
    bsz, seq, d = x.shape
    assert (AB_IN - NSA_GATE_COLS) % LANES == 0
    xf = x.reshape(bsz * seq, d)
    x_in = xf
    for layer in range(DEPTH):
        i = layer // 2
        if layer % 2 == 0:
            w_in = jnp.pad(ab_w_in[i], ((0, 0), (0, AB_IN_PADDED - AB_IN))).astype(BF16)
            h, u3 = project(x_in, w_in, PROJ_TM, chunked=(S5_CHUNK, S5_WIDTH))
            y_1 = s5_mixer(h, u3, bsz, seq, s5_lam_re[i], s5_lam_im[i], s5_log_dt[i], s5_b_re[i], s5_b_im[i],
                           s5_c_re[i], s5_c_im[i], s5_d[i], s5_w_glu[i], s5_b_glu[i])
            y_2 = nsa_mixer(h, bsz, seq, NSA_GATE_COL_BLOCK, nsa_pe_k[i], nsa_pe_v[i], nsa_ck_w1[i], nsa_ck_b1[i],
                            nsa_ck_w2[i], nsa_cv_w1[i], nsa_cv_b1[i], nsa_cv_w2[i])
            w_out = ab_w_out[i]
        else:
            h = project(x_in, cd_w_in[i].astype(BF16), PROJ_TM)
            y_1 = rwkv7_mixer(h, bsz, seq, rwkv_mu[i], rwkv_w0[i], rwkv_w_up[i], rwkv_a0[i], rwkv_a_up[i],
                              rwkv_g_up[i], rwkv_k_k[i], rwkv_k_a[i], rwkv_r_k[i], rwkv_ln_g[i], rwkv_ln_b[i])
            y_2 = retention_mixer(h, bsz, seq, RWKV_COLS, ret_ln_g[i], ret_ln_b[i])
            w_out = cd_w_out[i]
        xf = out_proj_ln(y_1, y_2, w_out, xf, ln1_g[layer], ln1_b[layer])
        xf, x_in = moe_block(xf, moe_router[layer], moe_bias[layer], moe_w1[layer], moe_w3[layer],
                             moe_w2, layer, sh_w1[layer], sh_w3[layer], sh_w2[layer], ln2_g[layer], ln2_b[layer])
    return xf.reshape(bsz, seq, d)
```

```python
import functools
import math

import jax
import jax.numpy as jnp
import numpy as np
from jax import lax
from jax.experimental import pallas as pl
from jax.experimental.pallas import tpu as pltpu

F32 = jnp.float32
BF16 = jnp.bfloat16
HIGHEST = lax.Precision.HIGHEST
FP8 = jnp.float8_e4m3fn
FP8_MAX = 448.0
FP8_TINY = 1e-30

VMEM_LIMIT_BYTES = 52 * 1024 * 1024
LANES = 128

LN_EPS = 1e-5
DEPTH = 2
ALPHA = (2 * DEPTH) ** 0.25

S5_GROUPS, S5_GROUP_CH, S5_STATE = 32, 16, 64
S5_WIDTH = S5_GROUPS * S5_GROUP_CH
S5_CHUNK = 16
S5_PACK = 8
NSA_HEADS, NSA_KV_GROUPS, NSA_HEAD_DIM = 8, 2, 64
NSA_HPG = NSA_HEADS // NSA_KV_GROUPS
NSA_WIDTH = NSA_HEADS * NSA_HEAD_DIM
NSA_ROT_DIM = NSA_HEAD_DIM // 4
ROPE_THETA = 500000.0
CMP_BLOCK, CMP_STRIDE, CMP_HIDDEN = 32, 16, 128
SLC_BLOCK, N_SLC, WINDOW = 64, 16, 512
Q_BLOCK = 256
FORCE_SCORE = 1e6
MASK_VALUE = -1e30
RWKV_HEADS, RWKV_HEAD_DIM = 8, 64
RWKV_WIDTH = RWKV_HEADS * RWKV_HEAD_DIM
RWKV_LORA_W, RWKV_LORA_A, RWKV_LORA_G = 64, 64, 128
RWKV_COLS = 3 * RWKV_WIDTH + RWKV_LORA_W + RWKV_LORA_A + RWKV_LORA_G
RWKV_GN_EPS = 64e-5
RWKV_CHUNK = 64
RWKV_PACK = 4
RET_HEADS, RET_DK, RET_DV, RET_CHUNK = 4, 64, 128, 128
RET_THETA = 10000.0
RET_GN_EPS = 1e-5
N_EXPERTS, TOP_K, EXPERT_FF = 64, 8, 256
N_EXPERT_GROUPS, TOPK_GROUPS = 8, 4
EXPERTS_PER_GROUP = N_EXPERTS // N_EXPERT_GROUPS
ROUTED_SCALE = 2.5
MOE_EXPERTS_PER_STEP = 4


def _params(*sem):
    return pltpu.CompilerParams(dimension_semantics=sem, vmem_limit_bytes=VMEM_LIMIT_BYTES)


def _dot(a, b, **kw):
    return jnp.dot(a, b, preferred_element_type=F32, **kw)


def _dot_nt(a, b, **kw):
    return lax.dot_general(a, b, (((1,), (1,)), ((), ())), preferred_element_type=F32, **kw)


def _dot_tn(a, b, **kw):
    return lax.dot_general(a, b, (((0,), (0,)), ((), ())), preferred_element_type=F32, **kw)


def _dot_split(a, b, split, parts):
    rest = a if split == 'a' else b
    acc = None
    for _ in range(parts):
        piece = rest.astype(BF16)
        term = _dot(piece, b) if split == 'a' else _dot(a, piece)
        acc = term if acc is None else acc + term
        rest = rest - piece.astype(F32)
    return acc


def _run_interleaved(gens):
    results = [None] * len(gens)
    live = list(range(len(gens)))
    while live:
        for i in list(live):
            try:
                next(gens[i])
            except StopIteration as done:
                results[i] = done.value
                live.remove(i)
    return results


def _gelu(x):
    return 0.5 * x * (1.0 + jnp.tanh(math.sqrt(2.0 / math.pi) * (x + 0.044715 * (x * x * x))))


def _sigmoid(x):
    return 1.0 / (1.0 + jnp.exp(-x))


def _layer_norm_rows(z, g, b):
    mu = jnp.mean(z, axis=-1, keepdims=True)
    zc = z - mu
    var = jnp.mean(zc * zc, axis=-1, keepdims=True)
    return zc * lax.rsqrt(var + LN_EPS) * g + b


def _proj_kernel(x_ref, w_ref, o_ref, *chunked_ref):
    y = _dot(x_ref[...].astype(BF16), w_ref[...])
    o_ref[...] = y
    for c_ref in chunked_ref:
        rows, t, w = c_ref.shape
        c_ref[...] = y[:, :w].reshape(rows, t, w)


def project(x, w_bf16, tm, chunked=None):
    m, k = x.shape
    n = w_bf16.shape[1]
    out_specs = [pl.BlockSpec((tm, n), lambda i: (i, 0))]
    out_shape = [jax.ShapeDtypeStruct((m, n), F32)]
    if chunked is not None:
        t, w = chunked
        out_specs.append(pl.BlockSpec((tm // t, t, w), lambda i: (i, 0, 0)))
        out_shape.append(jax.ShapeDtypeStruct((m // t, t, w), F32))
    out = pl.pallas_call(
        _proj_kernel,
        grid=(m // tm,),
        in_specs=[pl.BlockSpec((tm, k), lambda i: (i, 0)), pl.BlockSpec((k, n), lambda i: (0, 0))],
        out_specs=out_specs,
        out_shape=out_shape,
        compiler_params=_params("parallel"),
        name="project",
    )(x, w_bf16)
    return out if chunked is not None else out[0]


def _out_proj_ln_kernel(ya_ref, yb_ref, wa_ref, wb_ref, x_ref, g_ref, b_ref, o_ref):
    mix = _dot(ya_ref[...], wa_ref[...]) + _dot(yb_ref[...], wb_ref[...])
    o_ref[...] = _layer_norm_rows(ALPHA * x_ref[...] + mix, g_ref[...], b_ref[...])


def out_proj_ln(ya, yb, w_out, x, g, b, tm=512):
    n, d = x.shape
    ka, kb = ya.shape[1], yb.shape[1]
    wa = w_out[:ka].astype(BF16)
    wb = w_out[ka:].astype(BF16)
    row = lambda i: (i, 0)
    fixed = lambda i: (0, 0)
    return pl.pallas_call(
        _out_proj_ln_kernel,
        grid=(n // tm,),
        in_specs=[pl.BlockSpec((tm, ka), row), pl.BlockSpec((tm, kb), row),
                  pl.BlockSpec((ka, d), fixed), pl.BlockSpec((kb, d), fixed),
                  pl.BlockSpec((tm, d), row), pl.BlockSpec((1, d), fixed), pl.BlockSpec((1, d), fixed)],
        out_specs=pl.BlockSpec((tm, d), row),
        out_shape=jax.ShapeDtypeStruct((n, d), F32),
        compiler_params=_params("parallel"),
        name="out_proj_ln",
    )(ya, yb, wa, wb, x, g.reshape(1, d), b.reshape(1, d))


def _router_kernel(x_ref, rt_ref, bias_ref, o_ref):
    tr = x_ref.shape[0]
    scores = _sigmoid(_dot_nt(rt_ref[...], x_ref[...], precision=HIGHEST))
    biased = scores + bias_ref[...]
    grp = biased.reshape(N_EXPERT_GROUPS, EXPERTS_PER_GROUP, tr)
    pos = lax.broadcasted_iota(jnp.int32, grp.shape, 1)
    m1 = jnp.max(grp, axis=1, keepdims=True)
    first = jnp.min(jnp.where(grp == m1, pos, EXPERTS_PER_GROUP), axis=1, keepdims=True)
    m2 = jnp.max(jnp.where(pos == first, -jnp.inf, grp), axis=1, keepdims=True)
    gscore = (m1 + m2).reshape(N_EXPERT_GROUPS, tr)
    gidx = lax.broadcasted_iota(jnp.int32, gscore.shape, 0)
    grank = jnp.zeros(gscore.shape, F32)
    for j in range(N_EXPERT_GROUPS):
        row = gscore[j:j + 1, :]
        grank = grank + jnp.where(gidx > j, jnp.where(row >= gscore, 1.0, 0.0), jnp.where(row > gscore, 1.0, 0.0))
    gkeep = jnp.where(grank < TOPK_GROUPS, 1.0, 0.0)
    keep = jnp.broadcast_to(gkeep[:, None, :], grp.shape).reshape(N_EXPERTS, tr)
    masked = jnp.where(keep > 0.5, biased, -jnp.inf)
    eidx = lax.broadcasted_iota(jnp.int32, masked.shape, 0)
    rank = jnp.zeros(masked.shape, F32)
    for j in range(N_EXPERTS):
        row = masked[j:j + 1, :]
        rank = rank + jnp.where(eidx > j, jnp.where(row >= masked, 1.0, 0.0), jnp.where(row > masked, 1.0, 0.0))
    gate = jnp.where(rank < TOP_K, scores, 0.0)
    gate = gate / jnp.sum(gate, axis=0, keepdims=True) * ROUTED_SCALE
    o_ref[...] = jnp.concatenate([gate, jnp.zeros((LANES - N_EXPERTS, tr), F32)], axis=0).T


def moe_router(x, router, bias, tr=512):
    n, d = x.shape
    return pl.pallas_call(
        _router_kernel,
        grid=(n // tr,),
        in_specs=[pl.BlockSpec((tr, d), lambda i: (i, 0)),
                  pl.BlockSpec((N_EXPERTS, d), lambda i: (0, 0)),
                  pl.BlockSpec((N_EXPERTS, 1), lambda i: (0, 0))],
        out_specs=pl.BlockSpec((tr, LANES), lambda i: (i, 0)),
        out_shape=jax.ShapeDtypeStruct((n, LANES), F32),
        compiler_params=_params("parallel"),
        name="moe_router",
    )(x, router.T, bias.reshape(N_EXPERTS, 1))


def _quantize_fp8(a, axes):
    amax = jnp.max(jnp.abs(a), axis=axes, keepdims=True)
    scale = jnp.maximum(amax, FP8_TINY) * (1.0 / FP8_MAX)
    return (a * (1.0 / scale)).astype(FP8), scale


def _swiglu_hidden(xq, x_scale, w1q, w3q, w_scale, gate=None):
    col1 = x_scale * w_scale[0:1, 0:1]
    col3 = x_scale * w_scale[1:2, 0:1]
    if gate is not None:
        col3 = col3 * gate
    h1 = _dot(xq, w1q) * col1
    return h1 * _sigmoid(h1) * (_dot(xq, w3q) * col3)


def _experts_ln_kernel(x_ref, gate_ref, w1_ref, w3_ref, ws_ref, w2_ref, sw1_ref, sw3_ref, sws_ref, sw2_ref,
                       g_ref, b_ref, o_ref, obf_ref, acc_ref, xq_ref, xs_ref):
    step = pl.program_id(1)
    per_step = w1_ref.shape[0]

    @pl.when(step == 0)
    def _():
        xq, xs = _quantize_fp8(x_ref[...], (1,))
        xq_ref[...] = xq
        xs_ref[...] = xs
        h = _swiglu_hidden(xq, xs, sw1_ref[...], sw3_ref[...], sws_ref[...])
        acc_ref[...] = _dot(h.astype(BF16), sw2_ref[...].astype(BF16))

    lane = lax.broadcasted_iota(jnp.int32, gate_ref.shape, 1)
    gates = gate_ref[...]
    xq, xs = xq_ref[...], xs_ref[...]
    hidden = []
    for j in range(per_step):
        gcol = jnp.sum(jnp.where(lane == step * per_step + j, gates, 0.0), axis=1, keepdims=True)
        hidden.append(_swiglu_hidden(xq, xs, w1_ref[j], w3_ref[j], ws_ref[j], gcol).astype(BF16))
    w2 = w2_ref[0].astype(BF16)
    acc_ref[...] += _dot(jnp.concatenate(hidden, axis=1), w2.reshape(per_step * w2.shape[1], w2.shape[2]))

    @pl.when(step == pl.num_programs(1) - 1)
    def _():
        y = _layer_norm_rows(ALPHA * x_ref[...] + acc_ref[...], g_ref[...], b_ref[...])
        o_ref[...] = y
        obf_ref[...] = y.astype(BF16)


def _quantize_expert_weights(w1, w3):
    w1q, s1 = _quantize_fp8(w1, (-2, -1))
    w3q, s3 = _quantize_fp8(w3, (-2, -1))
    scales = jnp.broadcast_to(jnp.concatenate([s1, s3], axis=-2), s1.shape[:-2] + (2, w1.shape[-1]))
    return w1q, w3q, scales


def moe_experts_ln(x, gates, w1, w3, w2_layers, layer, sw1, sw3, sw2, g, b, tm=1024):
    n, d = x.shape
    ne = w1.shape[0]
    w1q, w3q, ws = _quantize_expert_weights(w1, w3)
    sw1q, sw3q, sws = _quantize_expert_weights(sw1, sw3)
    tok = lambda i, e: (i, 0)
    fixed = lambda i, e: (0, 0)
    per_expert = lambda *blk: pl.BlockSpec((MOE_EXPERTS_PER_STEP,) + blk, lambda i, e: (e, 0, 0))
    return pl.pallas_call(
        _experts_ln_kernel,
        grid=(n // tm, ne // MOE_EXPERTS_PER_STEP),
        in_specs=[pl.BlockSpec((tm, d), tok), pl.BlockSpec((tm, LANES), tok),
                  per_expert(d, EXPERT_FF), per_expert(d, EXPERT_FF), per_expert(2, EXPERT_FF),
                  pl.BlockSpec((1, MOE_EXPERTS_PER_STEP, EXPERT_FF, d), lambda i, e: (layer, e, 0, 0)),
                  pl.BlockSpec((d, EXPERT_FF), fixed), pl.BlockSpec((d, EXPERT_FF), fixed),
                  pl.BlockSpec((2, EXPERT_FF), fixed), pl.BlockSpec((EXPERT_FF, d), fixed),
                  pl.BlockSpec((1, d), fixed), pl.BlockSpec((1, d), fixed)],
        out_specs=[pl.BlockSpec((tm, d), tok), pl.BlockSpec((tm, d), tok)],
        out_shape=[jax.ShapeDtypeStruct((n, d), F32), jax.ShapeDtypeStruct((n, d), BF16)],
        scratch_shapes=[pltpu.VMEM((tm, d), F32), pltpu.VMEM((tm, d), FP8), pltpu.VMEM((tm, 1), F32)],
        compiler_params=_params("parallel", "arbitrary"),
        name="moe_experts_ln",
    )(x, gates, w1q, w3q, ws, w2_layers, sw1q, sw3q, sws, sw2, g.reshape(1, d), b.reshape(1, d))


def moe_block(x, router, bias, w1, w3, w2_layers, layer, sw1, sw3, sw2, g, b):
    gates = moe_router(x, router, bias)
    return moe_experts_ln(x, gates, w1, w3, w2_layers, layer, sw1, sw3, sw2, g, b)


def _s5_tables(lam_re, lam_im, log_dt, b_re, b_im, c_re, c_im, n_chunk):
    t, h, p = S5_CHUNK, S5_GROUP_CH, S5_STATE
    dt = jnp.exp(log_dt.astype(F32))[:, None]
    den = lam_re ** 2 + lam_im ** 2

    def lam_pow(k):
        k = jnp.asarray(k, F32)[..., None, None]
        mag = jnp.exp(lam_re * dt * k)
        return mag * jnp.cos(lam_im * dt * k), mag * jnp.sin(lam_im * dt * k)

    lb_re, lb_im = lam_pow(1.0)
    f_re = ((lb_re - 1.0) * lam_re + lb_im * lam_im) / den
    f_im = (lb_im * lam_re - (lb_re - 1.0) * lam_im) / den
    bb_re = f_re[..., None] * b_re - f_im[..., None] * b_im
    bb_im = f_re[..., None] * b_im + f_im[..., None] * b_re
    pr, pi = lam_pow(jnp.arange(t))
    cl_re = c_re[None] * pr[:, :, None, :] - c_im[None] * pi[:, :, None, :]
    cl_im = c_re[None] * pi[:, :, None, :] + c_im[None] * pr[:, :, None, :]
    klag = jnp.einsum('tgop,gpi->tgoi', cl_re, bb_re) - jnp.einsum('tgop,gpi->tgoi', cl_im, bb_im)
    nb = S5_GROUPS // S5_PACK
    split = lambda a, axis: a.reshape(a.shape[:axis] + (nb, S5_PACK) + a.shape[axis + 1:])
    eye = jnp.eye(S5_PACK, dtype=F32)
    lag_t = jnp.transpose(split(klag, 1), (1, 0, 2, 4, 3))
    lag_t = (lag_t[:, :, :, :, None, :] * eye[None, None, :, None, :, None]).reshape(nb, t, LANES, LANES)
    qr, qi = lam_pow(t - 1 - jnp.arange(t))
    st_re = qr[..., None] * bb_re[None] - qi[..., None] * bb_im[None]
    st_im = qr[..., None] * bb_im[None] + qi[..., None] * bb_re[None]
    st = jnp.stack([st_re, st_im], axis=0)
    st_t = jnp.transpose(split(st, 2), (2, 1, 3, 5, 0, 4)).reshape(nb, t, LANES, 2 * p)
    er, ei = lam_pow(jnp.arange(t) + 1)
    x_re = c_re[None] * er[:, :, None, :] - c_im[None] * ei[:, :, None, :]
    x_im = c_re[None] * ei[:, :, None, :] + c_im[None] * er[:, :, None, :]
    cr = jnp.stack([x_re, -x_im], axis=0)
    cr_t = jnp.transpose(split(cr, 2), (2, 1, 0, 5, 3, 4)).reshape(nb, t, 2 * p, LANES)
    levels = max(1, int(math.log2(n_chunk)))
    sr, si = lam_pow(t * (2.0 ** jnp.arange(levels)))
    sr = sr.reshape(levels, nb, S5_PACK * p)
    si = si.reshape(levels, nb, S5_PACK * p)
    a1 = jnp.concatenate([sr, sr], axis=-1)
    a2 = jnp.concatenate([-si, si], axis=-1)
    scan = jnp.transpose(jnp.stack([a1, a2], axis=1), (2, 0, 1, 3))
    return lag_t, st_t, cr_t, scan.astype(F32)


def _s5_build_tables(lag_ref, st_ref, cr_ref, wtoe_ref, wstate_ref, wcross_ref):
    t = lag_ref.shape[1]
    p = S5_STATE
    kp = S5_PACK * p
    wtoe_ref[...] = jnp.zeros_like(wtoe_ref)
    for d in range(t):
        tile = lag_ref[0, d].astype(BF16)
        for j in range(t - d):
            wtoe_ref[j * LANES:(j + 1) * LANES, (j + d) * LANES:(j + d + 1) * LANES] = tile
    lane = lax.broadcasted_iota(jnp.int32, (LANES, LANES), 1)
    row_g = lax.broadcasted_iota(jnp.int32, (LANES, kp), 0) // S5_GROUP_CH
    same_s = row_g == lax.broadcasted_iota(jnp.int32, (LANES, kp), 1) // p
    for j in range(t):
        a = st_ref[0, j]
        swapped = pltpu.roll(a, p, 1)
        for c, both in enumerate((jnp.where(lane < p, a, swapped), jnp.where(lane < p, swapped, a))):
            wide = jnp.concatenate([both] * (kp // LANES), axis=1)
            wstate_ref[j * LANES:(j + 1) * LANES, c * kp:(c + 1) * kp] = jnp.where(same_s, wide, 0.0).astype(BF16)
    same_c = (lax.broadcasted_iota(jnp.int32, (kp, LANES), 0) // p
              == lax.broadcasted_iota(jnp.int32, (kp, LANES), 1) // S5_GROUP_CH)
    for i in range(t):
        a = cr_ref[0, i]
        for c in range(2):
            tall = jnp.concatenate([a[c * p:(c + 1) * p]] * S5_PACK, axis=0)
            wcross_ref[c * kp:(c + 1) * kp, i * LANES:(i + 1) * LANES] = jnp.where(same_c, tall, 0.0).astype(BF16)


def _s5_kernel(u_ref, lag_ref, st_ref, cr_ref, scan_ref, o_ref, wtoe_ref, wstate_ref, wcross_ref):
    @pl.when(pl.program_id(1) == 0)
    def _():
        _s5_build_tables(lag_ref, st_ref, cr_ref, wtoe_ref, wstate_ref, wcross_ref)

    n_chunk, t, _ = u_ref.shape
    x = jnp.concatenate([u_ref[:, j, :] for j in range(t)], axis=1).astype(BF16)
    local = _dot(x, wtoe_ref[...])
    state = _dot(x, wstate_ref[...])
    row = lax.broadcasted_iota(jnp.int32, state.shape, 0)
    s = jnp.where(row >= 1, pltpu.roll(state, 1, 0), 0.0)
    half = state.shape[1] // 2
    level = 0
    d = 1
    while d < n_chunk:
        mult = scan_ref[0, level]
        prev = jnp.where(row >= d, pltpu.roll(s, d, 0), 0.0)
        s = s + mult[0:1, :] * prev + mult[1:2, :] * pltpu.roll(prev, half, 1)
        d *= 2
        level += 1
    y = local + _dot(s.astype(BF16), wcross_ref[...])
    for i in range(t):
        o_ref[:, i, :] = y[:, i * LANES:(i + 1) * LANES]


def s5_scan(u3, bsz, lag_t, st_t, cr_t, scan):
    rows, t, w = u3.shape
    n_chunk = rows // bsz
    kp2 = 2 * S5_PACK * S5_STATE
    table = lambda a: pl.BlockSpec((1,) + a.shape[1:], lambda j, b: (j, 0, 0, 0))
    return pl.pallas_call(
        _s5_kernel,
        grid=(w // LANES, bsz),
        in_specs=[pl.BlockSpec((n_chunk, t, LANES), lambda j, b: (b, 0, j)),
                  table(lag_t), table(st_t), table(cr_t), table(scan)],
        out_specs=pl.BlockSpec((n_chunk, t, LANES), lambda j, b: (b, 0, j)),
        out_shape=jax.ShapeDtypeStruct(u3.shape, F32),
        scratch_shapes=[pltpu.VMEM((t * LANES, t * LANES), BF16), pltpu.VMEM((t * LANES, kp2), BF16),
                        pltpu.VMEM((kp2, t * LANES), BF16)],
        compiler_params=_params("arbitrary", "arbitrary"),
        name="s5_scan",
    )(u3, lag_t, st_t, cr_t, scan)


def _s5_post_kernel(y_ref, u_ref, d_ref, w_ref, b_ref, o_ref):
    u = u_ref[...]
    y = _gelu(y_ref[...].reshape(u.shape) + d_ref[...] * u)
    o_ref[...] = (y * _sigmoid(_dot(y.astype(BF16), w_ref[...]) + b_ref[...])).astype(o_ref.dtype)


def s5_post(y3, h, d_skip, w_glu, b_glu, tm=1024):
    rows, t, w = y3.shape
    n = rows * t
    tm = min(tm, n)
    row = lambda i: (i, 0)
    fixed = lambda i: (0, 0)
    return pl.pallas_call(
        _s5_post_kernel,
        grid=(n // tm,),
        in_specs=[pl.BlockSpec((tm // t, t, w), lambda i: (i, 0, 0)), pl.BlockSpec((tm, w), row),
                  pl.BlockSpec((1, w), fixed), pl.BlockSpec((w, w), fixed), pl.BlockSpec((1, w), fixed)],
        out_specs=pl.BlockSpec((tm, w), row),
        out_shape=jax.ShapeDtypeStruct((n, w), BF16),
        compiler_params=_params("parallel"),
        name="s5_post",
    )(y3, h, d_skip.reshape(1, w), w_glu.astype(BF16), b_glu.reshape(1, w))


def s5_mixer(h, u3, bsz, seq, lam_re, lam_im, log_dt, b_re, b_im, c_re, c_im, d_skip, w_glu, b_glu):
    tables = _s5_tables(lam_re, lam_im, log_dt, b_re, b_im, c_re, c_im, seq // S5_CHUNK)
    return s5_post(s5_scan(u3, bsz, *tables), h, d_skip, w_glu, b_glu)


def _rope_tables(pos, rot_dim, theta, head_dim, n_heads):
    half = rot_dim // 2
    f32 = np.float32
    inv_freq = f32(theta) ** (-np.arange(half, dtype=f32) / f32(half))
    ang = (pos.astype(f32)[:, None] * inv_freq[None, :]).astype(np.float64)
    cos, sin = np.cos(ang), np.sin(ang)
    rest = head_dim - rot_dim
    n = pos.shape[0]
    c = np.concatenate([cos, cos, np.ones((n, rest))], axis=1)
    s_up = np.concatenate([-sin, np.zeros((n, half + rest))], axis=1)
    s_dn = np.concatenate([np.zeros((n, half)), sin, np.zeros((n, rest))], axis=1)
    tile = lambda a: jnp.asarray(np.tile(a, (1, n_heads)), F32)
    return tile(c), tile(s_up), tile(s_dn)


def _rope_apply(x, c, s_up, s_dn, half):
    return x * c + pltpu.roll(x, LANES - half, 1) * s_up + pltpu.roll(x, half, 1) * s_dn


def _retention_tables():
    c = RET_CHUNK
    log_gamma = np.log(1.0 - 2.0 ** (-5.0 - np.arange(RET_HEADS, dtype=np.float64)))
    i = np.arange(c, dtype=np.float64)
    diff = i[:, None] - i[None, :]
    decay = np.where(diff >= 0, np.exp(diff[None] * log_gamma[:, None, None]), 0.0)
    qdec = np.repeat(np.exp((i + 1.0)[:, None] * log_gamma[None, :]), RET_DK, axis=1)
    kdec = np.repeat(np.exp((c - 1.0 - i)[:, None] * log_gamma[None, :]), RET_DK, axis=1)
    chunk_decay = [float(v) for v in np.exp(c * log_gamma)]
    return jnp.asarray(decay, F32), jnp.asarray(qdec, F32), jnp.asarray(kdec, F32), chunk_decay


def _retention_kernel(chunk_decay, q_ref, k_ref, v0_ref, v1_ref, g0_ref, g1_ref, c_ref, su_ref, sd_ref,
                      dec_ref, qdec_ref, kdec_ref, lng_ref, lnb_ref, o_ref, state_ref):
    @pl.when(pl.program_id(1) == 0)
    def _():
        state_ref[...] = jnp.zeros_like(state_ref)

    half = RET_DK // 2
    tabs = (c_ref[...], su_ref[...], sd_ref[...])
    q = jnp.concatenate([_rope_apply(q_ref[:, s:s + LANES], *tabs, half) for s in (0, LANES)], axis=1)
    k = jnp.concatenate([_rope_apply(k_ref[:, s:s + LANES], *tabs, half) for s in (0, LANES)], axis=1)
    k = k * (RET_DK ** -0.5)
    q_dec = q * qdec_ref[...]
    k_dec = k * kdec_ref[...]
    v = jnp.concatenate([v0_ref[...], v1_ref[...]], axis=1)
    gate = jnp.concatenate([g0_ref[...], g1_ref[...]], axis=1)
    states = [state_ref[h] for h in range(RET_HEADS)]

    def head(h):
        ks = slice(h * RET_DK, (h + 1) * RET_DK)
        vh = v[:, h * RET_DV:(h + 1) * RET_DV].astype(BF16)
        scores = _dot_nt(q[:, ks].astype(BF16), k[:, ks].astype(BF16)) * dec_ref[h]
        yield
        y = _dot(scores.astype(BF16), vh) + _dot(q_dec[:, ks].astype(BF16), states[h].astype(BF16))
        yield
        new_state = states[h] * chunk_decay[h] + _dot_tn(k_dec[:, ks].astype(BF16), vh)
        yield
        mu = jnp.mean(y, axis=-1, keepdims=True)
        yc = y - mu
        var = jnp.mean(yc * yc, axis=-1, keepdims=True)
        return yc * lax.rsqrt(var + RET_GN_EPS), new_state

    results = _run_interleaved([head(h) for h in range(RET_HEADS)])
    outs = [r[0] for r in results]
    for h, (_, new_state) in enumerate(results):
        state_ref[h] = new_state
    yn = jnp.concatenate(outs, axis=1) * lng_ref[...] + lnb_ref[...]
    o_ref[...] = (gate * _sigmoid(gate) * yn).astype(o_ref.dtype)


def retention_mixer(h, bsz, seq, col0, ln_g, ln_b):
    c = RET_CHUNK
    n_chunk = seq // c
    qk_w = RET_HEADS * RET_DK
    v_w = RET_HEADS * RET_DV
    assert col0 % qk_w == 0 and qk_w == 2 * LANES and v_w == 2 * qk_w
    cb = col0 // qk_w
    rc, rsu, rsd = _rope_tables(np.arange(seq), RET_DK, RET_THETA, RET_DK, 2)
    dec, qdec, kdec, chunk_decay = _retention_tables()
    row = lambda j: (lambda b, n: (b * n_chunk + n, j))
    pos = lambda b, n: (n, 0)
    fixed2 = lambda b, n: (0, 0)
    kern = functools.partial(_retention_kernel, chunk_decay)
    return pl.pallas_call(
        kern,
        grid=(bsz, n_chunk),
        in_specs=[pl.BlockSpec((c, qk_w), row(cb)), pl.BlockSpec((c, qk_w), row(cb + 1)),
                  pl.BlockSpec((c, qk_w), row(cb + 2)), pl.BlockSpec((c, qk_w), row(cb + 3)),
                  pl.BlockSpec((c, qk_w), row(cb + 4)), pl.BlockSpec((c, qk_w), row(cb + 5)),
                  pl.BlockSpec((c, LANES), pos), pl.BlockSpec((c, LANES), pos), pl.BlockSpec((c, LANES), pos),
                  pl.BlockSpec((RET_HEADS, c, c), lambda b, n: (0, 0, 0)),
                  pl.BlockSpec((c, qk_w), fixed2), pl.BlockSpec((c, qk_w), fixed2),
                  pl.BlockSpec((1, v_w), fixed2), pl.BlockSpec((1, v_w), fixed2)],
        out_specs=pl.BlockSpec((c, v_w), lambda b, n: (b * n_chunk + n, 0)),
        out_shape=jax.ShapeDtypeStruct((bsz * seq, v_w), BF16),
        scratch_shapes=[pltpu.VMEM((RET_HEADS, RET_DK, RET_DV), F32)],
        compiler_params=_params("parallel", "arbitrary"),
        name="retention",
    )(h, h, h, h, h, h, rc, rsu, rsd, dec, qdec, kdec, ln_g.reshape(1, v_w), ln_b.reshape(1, v_w))


NSA_KV_W = NSA_KV_GROUPS * NSA_HEAD_DIM
NSA_GATE_COLS = 3 * NSA_HEADS


def _nsa_prep_kernel(q_ref, kvc_ref, kvs_ref, kvw_ref, c_ref, su_ref, sd_ref,
                     qo_ref, kc_ref, vc_ref, ks_ref, vs_ref, kw_ref, vw_ref):
    half = NSA_ROT_DIM // 2
    tabs = (c_ref[...], su_ref[...], sd_ref[...])
    scale = NSA_HEAD_DIM ** -0.5 * math.log2(math.e)
    q = jnp.concatenate(
        [_rope_apply(q_ref[:, s:s + LANES], *tabs, half) * scale for s in range(0, NSA_WIDTH, LANES)], axis=1)
    qo_ref[0] = q.T.astype(qo_ref.dtype)

    def split(x, o_ref):
        for g in range(NSA_KV_GROUPS):
            o_ref[0, g] = x[:, g * NSA_HEAD_DIM:(g + 1) * NSA_HEAD_DIM].astype(o_ref.dtype)

    def split_t(x, o_ref):
        xt = x.T
        for g in range(NSA_KV_GROUPS):
            o_ref[0, g] = xt[g * NSA_HEAD_DIM:(g + 1) * NSA_HEAD_DIM, :].astype(o_ref.dtype)

    split(kvc_ref[:, :NSA_KV_W], kc_ref)
    split(kvc_ref[:, NSA_KV_W:], vc_ref)
    split(_rope_apply(kvs_ref[:, :NSA_KV_W], *tabs, half), ks_ref)
    split_t(kvs_ref[:, NSA_KV_W:], vs_ref)
    split(_rope_apply(kvw_ref[:, :NSA_KV_W], *tabs, half), kw_ref)
    split_t(kvw_ref[:, NSA_KV_W:], vw_ref)


def nsa_prep(h, bsz, seq, tl=512):
    tl = min(tl, seq)
    nl = seq // tl
    rc, rsu, rsd = _rope_tables(np.arange(seq), NSA_ROT_DIM, ROPE_THETA, NSA_HEAD_DIM, LANES // NSA_HEAD_DIM)
    row = lambda j: (lambda b, l: (b * nl + l, j))
    pos = lambda b, l: (l, 0)
    kv_out = pl.BlockSpec((1, NSA_KV_GROUPS, tl, NSA_HEAD_DIM), lambda b, l: (b, 0, l, 0))
    kv_shape = lambda dt: jax.ShapeDtypeStruct((bsz, NSA_KV_GROUPS, seq, NSA_HEAD_DIM), dt)
    vt_out = pl.BlockSpec((1, NSA_KV_GROUPS, NSA_HEAD_DIM, tl), lambda b, l: (b, 0, 0, l))
    vt_shape = jax.ShapeDtypeStruct((bsz, NSA_KV_GROUPS, NSA_HEAD_DIM, seq), BF16)
    two = 2 * NSA_KV_W
    return pl.pallas_call(
        _nsa_prep_kernel,
        grid=(bsz, nl),
        in_specs=[pl.BlockSpec((tl, NSA_WIDTH), row(1)),
                  pl.BlockSpec((tl, two), row(4)), pl.BlockSpec((tl, two), row(5)), pl.BlockSpec((tl, two), row(6)),
                  pl.BlockSpec((tl, LANES), pos), pl.BlockSpec((tl, LANES), pos), pl.BlockSpec((tl, LANES), pos)],
        out_specs=[pl.BlockSpec((1, NSA_WIDTH, tl), lambda b, l: (b, 0, l)),
                   kv_out, kv_out, kv_out, vt_out, kv_out, vt_out],
        out_shape=[jax.ShapeDtypeStruct((bsz, NSA_WIDTH, seq), BF16),
                   kv_shape(F32), kv_shape(F32), kv_shape(BF16), vt_shape, kv_shape(BF16), vt_shape],
        compiler_params=_params("parallel", "parallel"),
        name="nsa_prep",
    )(h, h, h, h, rc, rsu, rsd)


def _nsa_compress_kernel(hk_ref, hv_ref, pek_ref, pev_ref, kw1_ref, kb1_ref, kw2_ref, vw1_ref, vb1_ref, vw2_ref,
                         c_ref, su_ref, sd_ref, ko_ref, vo_ref):
    def mlp(h_ref, pe_ref, w1_ref, b1_ref, w2_ref):
        hb = h_ref[0, 0]
        rows = hb.shape[0]
        first = _dot((hb + pe_ref[0:1, :]).astype(BF16), w1_ref[0])
        second = _dot((hb + pe_ref[1:2, :]).astype(BF16), w1_ref[1])
        hid = _gelu(first + pltpu.roll(second, rows - 1, 0) + b1_ref[...])
        return _dot(hid.astype(BF16), w2_ref[...])

    kc = _rope_apply(mlp(hk_ref, pek_ref, kw1_ref, kb1_ref, kw2_ref), c_ref[...], su_ref[...], sd_ref[...],
                     NSA_ROT_DIM // 2)
    vc = mlp(hv_ref, pev_ref, vw1_ref, vb1_ref, vw2_ref)
    ko_ref[0, 0] = kc[:, :NSA_HEAD_DIM].astype(ko_ref.dtype)
    vo_ref[0, 0] = vc.T[:NSA_HEAD_DIM, :].astype(vo_ref.dtype)


def nsa_compress(kc, vc, pe_k, pe_v, ck_w1, ck_b1, ck_w2, cv_w1, cv_b1, cv_w2):
    bsz, grp, seq, d = kc.shape
    n_rows = seq // CMP_STRIDE
    flat = CMP_STRIDE * d
    cmp_end = np.arange(n_rows) * CMP_STRIDE + CMP_BLOCK - 1
    rc, rsu, rsd = _rope_tables(cmp_end, NSA_ROT_DIM, ROPE_THETA, NSA_HEAD_DIM, LANES // NSA_HEAD_DIM)
    pad_w2 = lambda w: jnp.pad(w, ((0, 0), (0, LANES - d))).astype(BF16)
    blk = pl.BlockSpec((1, 1, n_rows, flat), lambda b, g: (b, g, 0, 0))
    f2 = lambda b, g: (0, 0)
    f3 = lambda b, g: (0, 0, 0)
    w_specs = [pl.BlockSpec((2, flat, CMP_HIDDEN), f3), pl.BlockSpec((1, CMP_HIDDEN), f2),
               pl.BlockSpec((CMP_HIDDEN, LANES), f2)]
    return pl.pallas_call(
        _nsa_compress_kernel,
        grid=(bsz, grp),
        in_specs=[blk, blk, pl.BlockSpec((2, flat), f2), pl.BlockSpec((2, flat), f2)] + w_specs + w_specs
                 + [pl.BlockSpec((n_rows, LANES), f2)] * 3,
        out_specs=[pl.BlockSpec((1, 1, n_rows, d), lambda b, g: (b, g, 0, 0)),
                   pl.BlockSpec((1, 1, d, n_rows), lambda b, g: (b, g, 0, 0))],
        out_shape=[jax.ShapeDtypeStruct((bsz, grp, n_rows, d), BF16),
                   jax.ShapeDtypeStruct((bsz, grp, d, n_rows), BF16)],
        compiler_params=_params("parallel", "parallel"),
        name="nsa_compress",
    )(kc.reshape(bsz, grp, n_rows, flat), vc.reshape(bsz, grp, n_rows, flat),
      pe_k.reshape(2, flat), pe_v.reshape(2, flat),
      ck_w1.reshape(2, flat, CMP_HIDDEN).astype(BF16), ck_b1.reshape(1, CMP_HIDDEN), pad_w2(ck_w2),
      cv_w1.reshape(2, flat, CMP_HIDDEN).astype(BF16), cv_b1.reshape(1, CMP_HIDDEN), pad_w2(cv_w2),
      rc, rsu, rsd)


def _per_head(x):
    return jnp.concatenate([x] * NSA_HPG, axis=1)


def _nsa_attn_kernel(seq, tk, qt_ref, gate_ref, kc_ref, vct_ref, ks_ref, vst_ref, kw_ref, vwt_ref, mmapt_ref,
                     o_ref, sel_ref):
    n_blk = seq // SLC_BLOCK
    n_sel = min(N_SLC, n_blk)
    hd = NSA_HEAD_DIM
    w = NSA_HPG * hd
    groups = range(NSA_KV_GROUPS)
    q0 = pl.program_id(1) * Q_BLOCK
    t_l = q0 + lax.broadcasted_iota(jnp.int32, (1, Q_BLOCK), 1)

    def select(g):
        qg = qt_ref[0, g * w:(g + 1) * w, :]
        qst = jnp.concatenate([qg[h * hd:(h + 1) * hd, :] for h in range(NSA_HPG)], axis=1)
        kc = kc_ref[0, g]
        n_cmp = kc.shape[0]
        cmp_end = lax.broadcasted_iota(jnp.int32, (n_cmp, 1), 0) * CMP_STRIDE + (CMP_BLOCK - 1)
        s = _dot(kc, qst) + _per_head(jnp.where(cmp_end <= t_l, 0.0, MASK_VALUE))
        yield
        p = jnp.exp2(s - jnp.max(s, axis=0, keepdims=True))
        any_key = _per_head(jnp.where(t_l >= CMP_BLOCK - 1, 1.0, 0.0))
        inv_l = any_key / jnp.sum(p, axis=0, keepdims=True)
        o_cmp = _dot(vct_ref[0, g], p.astype(BF16)) * inv_l
        yield
        p = p * inv_l
        imp = p[:, 0:Q_BLOCK]
        for h in range(1, NSA_HPG):
            imp = imp + p[:, h * Q_BLOCK:(h + 1) * Q_BLOCK]
        imp_slc = _dot_split(mmapt_ref[...], imp, 'b', 3)
        yield
        blk = lax.broadcasted_iota(jnp.int32, (n_blk, 1), 0)
        cur = t_l // SLC_BLOCK
        score = jnp.where(blk == 0, FORCE_SCORE,
                          jnp.where(blk == cur, FORCE_SCORE, jnp.where(blk == cur - 1, FORCE_SCORE, imp_slc)))
        score = jnp.where(blk * SLC_BLOCK <= t_l, score, -FORCE_SCORE)
        sel = jnp.zeros((n_blk, Q_BLOCK), F32)
        for _ in range(n_sel):
            best = jnp.max(score, axis=0, keepdims=True)
            idx = jnp.min(jnp.where(score == best, blk, n_blk), axis=0, keepdims=True)
            pick = blk == idx
            sel = jnp.where(pick, 1.0, sel)
            score = jnp.where(pick, -jnp.inf, score)
            yield
        sel_ref[g] = sel
        return qst, o_cmp

    selected = _run_interleaved([select(g) for g in groups])
    qst = [r[0] for r in selected]
    o_cmp = [r[1] for r in selected]

    blocks_per_tile = tk // SLC_BLOCK

    def slc_tile(kt, carry):
        k0 = pl.multiple_of(kt * tk, tk)
        causal = k0 + lax.broadcasted_iota(jnp.int32, (tk, 1), 0) <= t_l
        scores = [_dot(ks_ref[0, g, pl.ds(k0, tk), :], qst[g]) for g in groups]
        out = []
        for g in groups:
            m, l, acc = carry[g]
            chosen = jnp.concatenate(
                [jnp.broadcast_to(sel_ref[g, pl.ds(kt * blocks_per_tile + j, 1), :], (SLC_BLOCK, Q_BLOCK))
                 for j in range(blocks_per_tile)], axis=0)
            bias = jnp.where(causal, jnp.where(chosen > 0.5, 0.0, MASK_VALUE), MASK_VALUE)
            s = scores[g] + _per_head(bias)
            m_new = jnp.maximum(m, jnp.max(s, axis=0, keepdims=True))
            alpha = jnp.exp2(m - m_new)
            p = jnp.exp2(s - m_new)
            l = alpha * l + jnp.sum(p, axis=0, keepdims=True)
            acc = alpha * acc + _dot(vst_ref[0, g, :, pl.ds(k0, tk)], p.astype(BF16))
            out.append((m_new, l, acc))
        return tuple(out)

    n_tiles = (q0 + Q_BLOCK + tk - 1) // tk
    cols = NSA_HPG * Q_BLOCK
    init = tuple((jnp.full((1, cols), MASK_VALUE, F32), jnp.zeros((1, cols), F32), jnp.zeros((hd, cols), F32))
                 for _ in groups)
    slc = lax.fori_loop(0, n_tiles, slc_tile, init)

    band = WINDOW + Q_BLOCK
    w0 = pl.multiple_of(jnp.maximum(q0 - WINDOW, 0), Q_BLOCK)
    kpos = w0 + lax.broadcasted_iota(jnp.int32, (band, 1), 0)
    win_bias = _per_head(jnp.where(kpos <= t_l, jnp.where(kpos > t_l - WINDOW, 0.0, MASK_VALUE), MASK_VALUE))
    sig_t = _sigmoid(gate_ref[...]).T

    def finish(g):
        s = _dot(kw_ref[0, g, pl.ds(w0, band), :], qst[g]) + win_bias
        yield
        p = jnp.exp2(s - jnp.max(s, axis=0, keepdims=True))
        o_win = _dot(vwt_ref[0, g, :, pl.ds(w0, band)], p.astype(BF16)) / jnp.sum(p, axis=0, keepdims=True)
        yield
        _, l_slc, acc_slc = slc[g]

        def gate(branch):
            first = (g * NSA_HPG) * 3 + branch
            return jnp.concatenate([sig_t[first + 3 * h:first + 3 * h + 1, :] for h in range(NSA_HPG)], axis=1)

        out_t = gate(0) * o_cmp[g] + gate(1) * (acc_slc / l_slc) + gate(2) * o_win
        pairs = []
        for h in range(0, NSA_HPG, 2):
            two = jnp.concatenate([out_t[:, h * Q_BLOCK:(h + 1) * Q_BLOCK],
                                   out_t[:, (h + 1) * Q_BLOCK:(h + 2) * Q_BLOCK]], axis=0)
            pairs.append(two.T)
        o_ref[:, g * w:(g + 1) * w] = jnp.concatenate(pairs, axis=1).astype(o_ref.dtype)

    _run_interleaved([finish(g) for g in groups])


def _nsa_pool_matrix(seq):
    n_blk = seq // SLC_BLOCK
    n_rows = seq // CMP_STRIDE
    per_stride = SLC_BLOCK // CMP_STRIDE
    span = CMP_BLOCK // CMP_STRIDE
    pool = np.zeros((n_blk, n_rows), np.float32)
    for j in range(n_blk):
        for m in range(per_stride):
            for n in range(span):
                c = per_stride * j + m + n - (span - 1)
                if 0 <= c < n_rows - 1:
                    pool[j, c] += 1.0
    return jnp.asarray(pool, BF16)


def nsa_attention(qt, h, gate_col_block, k_cmp, v_cmp_t, ks, vs_t, kw, vw_t, bsz, seq, tk=1024):
    tk = min(tk, seq)
    nq = seq // Q_BLOCK
    pool = _nsa_pool_matrix(seq)
    n_rows = k_cmp.shape[2]
    d = NSA_HEAD_DIM
    qblk = lambda b, i: (b * nq + i, 0)
    whole = lambda *shape: pl.BlockSpec((1, NSA_KV_GROUPS) + shape, lambda b, i: (b, 0, 0, 0))
    kern = functools.partial(_nsa_attn_kernel, seq, tk)
    return pl.pallas_call(
        kern,
        grid=(bsz, nq),
        in_specs=[pl.BlockSpec((1, NSA_WIDTH, Q_BLOCK), lambda b, i: (b, 0, i)),
                  pl.BlockSpec((Q_BLOCK, LANES), lambda b, i: (b * nq + i, gate_col_block)),
                  whole(n_rows, d), whole(d, n_rows), whole(seq, d), whole(d, seq), whole(seq, d), whole(d, seq),
                  pl.BlockSpec(pool.shape, lambda b, i: (0, 0))],
        out_specs=pl.BlockSpec((Q_BLOCK, NSA_WIDTH), qblk),
        out_shape=jax.ShapeDtypeStruct((bsz * seq, NSA_WIDTH), BF16),
        scratch_shapes=[pltpu.VMEM((NSA_KV_GROUPS, seq // SLC_BLOCK, Q_BLOCK), F32)],
        compiler_params=_params("parallel", "arbitrary"),
        name="nsa_attention",
    )(qt, h, k_cmp, v_cmp_t, ks, vs_t, kw, vw_t, pool)


def nsa_mixer(h, bsz, seq, gate_col_block, pe_k, pe_v, ck_w1, ck_b1, ck_w2, cv_w1, cv_b1, cv_w2):
    qt, kc, vc, ks, vs_t, kw, vw_t = nsa_prep(h, bsz, seq)
    k_cmp, v_cmp_t = nsa_compress(kc, vc, pe_k, pe_v, ck_w1, ck_b1, ck_w2, cv_w1, cv_b1, cv_w2)
    return nsa_attention(qt, h, gate_col_block, k_cmp, v_cmp_t, ks, vs_t, kw, vw_t, bsz, seq)


def _head_ones(width, head_dim):
    idx = np.arange(width) // head_dim
    return jnp.asarray(idx[:, None] == idx[None, :], BF16)


def _softplus(x):
    return jnp.maximum(x, 0.0) + jnp.log(1.0 + jnp.exp(-jnp.abs(x)))


def _rwkv_pre_kernel(p_ref, prev_ref, mu_ref, w0_ref, wup_ref, a0_ref, aup_ref, gup_ref, kk_ref, ka_ref, rk_ref,
                     ones_ref, r_o, k_o, v_o, kk_o, b_o, ld_o, g_o, bonus_o):
    w = RWKV_WIDTH
    p = p_ref[...]
    first_row = jnp.where(pl.program_id(1) == 0, 0.0, prev_ref[7:8, :])
    is_row0 = lax.broadcasted_iota(jnp.int32, p.shape, 0) == 0
    prev = jnp.where(is_row0, first_row, pltpu.roll(p, 1, 0))
    ps = p + (prev - p) * mu_ref[...]
    r, k, v = ps[:, 0:w], ps[:, w:2 * w], ps[:, 2 * w:3 * w]
    o = 3 * w
    w_lo = ps[:, o:o + RWKV_LORA_W]
    a_lo = ps[:, o + RWKV_LORA_W:o + RWKV_LORA_W + RWKV_LORA_A]
    g_lo = ps[:, o + RWKV_LORA_W + RWKV_LORA_A:]
    wlog = -_softplus(-(w0_ref[...] + _dot(jnp.tanh(w_lo).astype(BF16), wup_ref[...]))) - 0.5
    a = _sigmoid(a0_ref[...] + _dot(a_lo.astype(BF16), aup_ref[...]))
    g = _dot(_sigmoid(g_lo).astype(BF16), gup_ref[...])
    kk = k * kk_ref[...]
    norm = jnp.sqrt(_dot_split(kk * kk, ones_ref[...], 'a', 2))
    kk = kk / jnp.maximum(norm, 1e-12)
    k2 = k * (1.0 + (a - 1.0) * ka_ref[...])
    r_o[...] = r
    k_o[...] = k2
    v_o[...] = v
    kk_o[...] = kk
    b_o[...] = kk * a
    ld_o[...] = -jnp.exp(wlog)
    g_o[...] = g
    bonus_o[...] = _dot_split(r * k2 * rk_ref[...], ones_ref[...], 'a', 2) * v


def rwkv_pre(h, bsz, seq, mu, w0, w_up, a0, a_up, g_up, k_k, k_a, r_k, tl=512):
    tl = min(tl, seq)
    nl = seq // tl
    w = RWKV_WIDTH
    cols = RWKV_COLS
    ones = _head_ones(w, RWKV_HEAD_DIM)
    f2 = lambda b, l: (0, 0)
    vec = pl.BlockSpec((1, w), f2)
    out_spec = pl.BlockSpec((tl, w), lambda b, l: (b * nl + l, 0))
    out_shape = jax.ShapeDtypeStruct((bsz * seq, w), F32)
    return pl.pallas_call(
        _rwkv_pre_kernel,
        grid=(bsz, nl),
        in_specs=[pl.BlockSpec((tl, cols), lambda b, l: (b * nl + l, 0)),
                  pl.BlockSpec((8, cols), lambda b, l: (jnp.maximum((b * seq + l * tl) // 8 - 1, 0), 0)),
                  pl.BlockSpec((1, cols), f2), vec, pl.BlockSpec((RWKV_LORA_W, w), f2),
                  vec, pl.BlockSpec((RWKV_LORA_A, w), f2), pl.BlockSpec((RWKV_LORA_G, w), f2),
                  vec, vec, vec, pl.BlockSpec((w, w), f2)],
        out_specs=[out_spec] * 8,
        out_shape=[out_shape] * 8,
        compiler_params=_params("parallel", "parallel"),
        name="rwkv_pre",
    )(h, h, mu.reshape(1, cols), w0.reshape(1, w), w_up.astype(BF16), a0.reshape(1, w), a_up.astype(BF16),
      g_up.astype(BF16), k_k.reshape(1, w), k_a.reshape(1, w), r_k.reshape(1, w), ones)


def _rwkv_masks():
    t, pk = RWKV_CHUNK, RWKV_PACK
    n = t * pk
    ri = np.arange(n)
    same = (ri[:, None] // t) == (ri[None, :] // t)
    tt, ss = ri[:, None] % t, ri[None, :] % t
    levels = []
    k = 1
    while k < t:
        levels.append(same & (tt // (2 * k) == ss // (2 * k)) & ((tt // k) % 2 == 1) & ((ss // k) % 2 == 0))
        k *= 2
    lvl = np.stack(levels).astype(np.float32)
    tri = (np.arange(t)[:, None] >= np.arange(t)[None, :]).astype(np.float32)
    head_lane = ((ri[:, None] // t) == (np.arange(pk * RWKV_HEAD_DIM)[None, :] // RWKV_HEAD_DIM)).astype(np.float32)
    return (jnp.asarray(tri, BF16), jnp.asarray(head_lane), jnp.asarray(same.astype(np.float32)), jnp.asarray(lvl))


def _rwkv_chain(r, k, v, kk, b, ld, st, tri, hl, bd, lvl_ref):
    t, pk = RWKV_CHUNK, RWKV_PACK
    n = t * pk
    c = _dot_split(tri, ld, 'b', 3)
    yield
    c_end = c[t - 1:t, :]
    e_neg = jnp.exp(-c)
    e_end = jnp.exp(c_end - c)
    kkd = (kk * jnp.exp(c - ld)).astype(BF16)
    rd = (r * jnp.exp(c)).astype(BF16)

    def big(x):
        return (jnp.concatenate([x] * pk, axis=0) * hl).astype(BF16)

    st_b = st.astype(BF16)
    v_big = big(v)
    a_all = _dot_nt(jnp.concatenate([kkd, rd], axis=0),
                    jnp.concatenate([big(k * e_neg), big(b * e_neg)], axis=0))
    yield
    ti = lax.broadcasted_iota(jnp.int32, (t, n), 0)
    si = lax.broadcasted_iota(jnp.int32, (t, n), 1) % t
    strict = ti > si
    incl = ti >= si
    a_kk = jnp.where(strict, a_all[:t, :n], 0.0)
    a_kb = jnp.where(strict, a_all[:t, n:], 0.0)
    a_rk = jnp.where(incl, a_all[t:, :n], 0.0)
    a_rb = jnp.where(incl, a_all[t:, n:], 0.0)
    rhs = _dot(kkd, st_b) + _dot(a_kk.astype(BF16), v_big)
    yield
    a_bd = jnp.concatenate([a_kb] * pk, axis=0) * bd
    m = jnp.where(lax.broadcasted_iota(jnp.int32, (n, n), 0) == lax.broadcasted_iota(jnp.int32, (n, n), 1), 1.0, 0.0)
    for lv in range(lvl_ref.shape[0]):
        mb = m.astype(BF16)
        ma = _dot(mb, (a_bd * lvl_ref[lv]).astype(BF16)).astype(BF16)
        yield
        m = m - _dot(ma, mb)
        yield
    u_big = _dot(m.astype(BF16), big(rhs))
    yield
    u = u_big[0:t]
    for h in range(1, pk):
        u = u + u_big[h * t:(h + 1) * t]
    y = _dot(rd, st_b) + _dot(a_rk.astype(BF16), v_big) - _dot(a_rb.astype(BF16), big(u))
    yield
    decay_col = jnp.broadcast_to(jnp.exp(c_end), st.shape).T
    kb_end = jnp.concatenate([k * e_end, -(b * e_end)], axis=0).astype(BF16)
    vu = jnp.concatenate([v, u], axis=0).astype(BF16)
    return y, decay_col * st + bd * _dot_tn(kb_end, vu)


def _rwkv_chunk_kernel(r_ref, k_ref, v_ref, kk_ref, b_ref, ld_ref, tri_ref, hl_ref, bd_ref, lvl_ref, y_ref, st_ref):
    @pl.when(pl.program_id(0) == 0)
    def _():
        st_ref[...] = jnp.zeros_like(st_ref)

    wp = RWKV_PACK * RWKV_HEAD_DIM
    tri, hl, bd = tri_ref[...], hl_ref[...], bd_ref[...]
    n_pack = r_ref.shape[2] // wp
    where = [(bi, slice(g * wp, (g + 1) * wp)) for bi in range(r_ref.shape[0]) for g in range(n_pack)]
    loaded = [tuple(ref[bi, :, cols] for ref in (r_ref, k_ref, v_ref, kk_ref, b_ref, ld_ref)) + (st_ref[i],)
              for i, (bi, cols) in enumerate(where)]
    results = _run_interleaved([_rwkv_chain(*args, tri, hl, bd, lvl_ref) for args in loaded])
    for i, ((bi, cols), (y, st_new)) in enumerate(zip(where, results)):
        y_ref[bi, :, cols] = y
        st_ref[i] = st_new


def rwkv_chunk(r, k, v, kk, b, ld, bsz, seq):
    t, pk = RWKV_CHUNK, RWKV_PACK
    n_chunk = seq // t
    w = RWKV_WIDTH
    wp = pk * RWKV_HEAD_DIM
    assert t == RWKV_HEAD_DIM
    tri, hl, bd, lvl = _rwkv_masks()
    blk = pl.BlockSpec((bsz, t, w), lambda c: (0, c, 0))
    f2 = lambda c: (0, 0)
    shaped = lambda a: a.reshape(bsz, seq, w)
    y = pl.pallas_call(
        _rwkv_chunk_kernel,
        grid=(n_chunk,),
        in_specs=[blk] * 6 + [pl.BlockSpec(tri.shape, f2), pl.BlockSpec(hl.shape, f2), pl.BlockSpec(bd.shape, f2),
                              pl.BlockSpec(lvl.shape, lambda c: (0, 0, 0))],
        out_specs=blk,
        out_shape=jax.ShapeDtypeStruct((bsz, seq, w), F32),
        scratch_shapes=[pltpu.VMEM((bsz * (w // wp), wp, wp), F32)],
        compiler_params=_params("arbitrary"),
        name="rwkv_chunk",
    )(shaped(r), shaped(k), shaped(v), shaped(kk), shaped(b), shaped(ld), tri, hl, bd, lvl)
    return y.reshape(bsz * seq, w)


def _rwkv_post_kernel(y_ref, bonus_ref, g_ref, lng_ref, lnb_ref, ones_ref, o_ref):
    y = y_ref[...]
    inv = 1.0 / RWKV_HEAD_DIM
    mu = _dot_split(y, ones_ref[...], 'a', 2) * inv
    yc = y - mu
    var = _dot_split(yc * yc, ones_ref[...], 'a', 2) * inv
    yn = yc * lax.rsqrt(var + RWKV_GN_EPS) * lng_ref[...] + lnb_ref[...]
    o_ref[...] = ((yn + bonus_ref[...]) * g_ref[...]).astype(o_ref.dtype)


def rwkv_post(y, bonus, g, ln_g, ln_b, tm=1024):
    n, w = y.shape
    tm = min(tm, n)
    row = pl.BlockSpec((tm, w), lambda i: (i, 0))
    vec = pl.BlockSpec((1, w), lambda i: (0, 0))
    return pl.pallas_call(
        _rwkv_post_kernel,
        grid=(n // tm,),
        in_specs=[row, row, row, vec, vec, pl.BlockSpec((w, w), lambda i: (0, 0))],
        out_specs=row,
        out_shape=jax.ShapeDtypeStruct((n, w), BF16),
        compiler_params=_params("parallel"),
        name="rwkv_post",
    )(y, bonus, g, ln_g.reshape(1, w), ln_b.reshape(1, w), _head_ones(w, RWKV_HEAD_DIM))


def rwkv7_mixer(h, bsz, seq, mu, w0, w_up, a0, a_up, g_up, k_k, k_a, r_k, ln_g, ln_b):
    r, k, v, kk, b, ld, g, bonus = rwkv_pre(h, bsz, seq, mu, w0, w_up, a0, a_up, g_up, k_k, k_a, r_k)
    y = rwkv_chunk(r, k, v, kk, b, ld, bsz, seq)
    return rwkv_post(y, bonus, g, ln_g, ln_b)


AB_IN = S5_WIDTH + NSA_WIDTH + 6 * NSA_KV_W + NSA_GATE_COLS
AB_IN_PADDED = -(-AB_IN // LANES) * LANES
NSA_GATE_COL_BLOCK = (AB_IN - NSA_GATE_COLS) // LANES
PROJ_TM = 512


def kernel(x, ab_w_in, ab_w_out, s5_lam_re, s5_lam_im, s5_log_dt, s5_b_re, s5_b_im, s5_c_re, s5_c_im, s5_d, s5_w_glu, s5_b_glu, nsa_pe_k, nsa_pe_v, nsa_ck_w1, nsa_ck_b1, nsa_ck_w2, nsa_cv_w1, nsa_cv_b1, nsa_cv_w2, cd_w_in, cd_w_out, rwkv_mu, rwkv_w0, rwkv_w_up, rwkv_a0, rwkv_a_up, rwkv_g_up, rwkv_k_k, rwkv_k_a, rwkv_r_k, rwkv_ln_g, rwkv_ln_b, ret_ln_g, ret_ln_b, ln1_g, ln1_b, ln2_g, ln2_b, moe_router, moe_bias, moe_w1, moe_w3, moe_w2, sh_w1, sh_w3, sh_w2):
    bsz, seq, d = x.shape
    assert (AB_IN - NSA_GATE_COLS) % LANES == 0
    xf = x.reshape(bsz * seq, d)
    x_in = xf
    for layer in range(DEPTH):
        i = layer // 2
        if layer % 2 == 0:
            w_in = jnp.pad(ab_w_in[i], ((0, 0), (0, AB_IN_PADDED - AB_IN))).astype(BF16)
            h, u3 = project(x_in, w_in, PROJ_TM, chunked=(S5_CHUNK, S5_WIDTH))
            y_1 = s5_mixer(h, u3, bsz, seq, s5_lam_re[i], s5_lam_im[i], s5_log_dt[i], s5_b_re[i], s5_b_im[i],
                           s5_c_re[i], s5_c_im[i], s5_d[i], s5_w_glu[i], s5_b_glu[i])
            y_2 = nsa_mixer(h, bsz, seq, NSA_GATE_COL_BLOCK, nsa_pe_k[i], nsa_pe_v[i], nsa_ck_w1[i], nsa_ck_b1[i],
                            nsa_ck_w2[i], nsa_cv_w1[i], nsa_cv_b1[i], nsa_cv_w2[i])
            w_out = ab_w_out[i]
        else:
            h = project(x_in, cd_w_in[i].astype(BF16), PROJ_TM)
            y_1 = rwkv7_mixer(h, bsz, seq, rwkv_mu[i], rwkv_w0[i], rwkv_w_up[i], rwkv_a0[i], rwkv_a_up[i],
                              rwkv_g_up[i], rwkv_k_k[i], rwkv_k_a[i], rwkv_r_k[i], rwkv_ln_g[i], rwkv_ln_b[i])
            y_2 = retention_mixer(h, bsz, seq, RWKV_COLS, ret_ln_g[i], ret_ln_b[i])
            w_out = cd_w_out[i]
        xf = out_proj_ln(y_1, y_2, w_out, xf, ln1_g[layer], ln1_b[layer])
        xf, x_in = moe_block(xf, moe_router[layer], moe_bias[layer], moe_w1[layer], moe_w3[layer],
                             moe_w2, layer, sh_w1[layer], sh_w3[layer], sh_w2[layer], ln2_g[layer], ln2_b[layer])
    return xf.reshape(bsz, seq, d)
```

```python
import functools
import math

import jax
import jax.numpy as jnp
import numpy as np
from jax import lax
from jax.experimental import pallas as pl
from jax.experimental.pallas import tpu as pltpu

F32 = jnp.float32
BF16 = jnp.bfloat16
HIGHEST = lax.Precision.HIGHEST
FP8 = jnp.float8_e4m3fn
FP8_MAX = 448.0
FP8_TINY = 1e-30

VMEM_LIMIT_BYTES = 52 * 1024 * 1024
LANES = 128

LN_EPS = 1e-5
DEPTH = 2
ALPHA = (2 * DEPTH) ** 0.25

S5_GROUPS, S5_GROUP_CH, S5_STATE = 32, 16, 64
S5_WIDTH = S5_GROUPS * S5_GROUP_CH
S5_CHUNK = 16
S5_PACK = 8
NSA_HEADS, NSA_KV_GROUPS, NSA_HEAD_DIM = 8, 2, 64
NSA_HPG = NSA_HEADS // NSA_KV_GROUPS
NSA_WIDTH = NSA_HEADS * NSA_HEAD_DIM
NSA_ROT_DIM = NSA_HEAD_DIM // 4
ROPE_THETA = 500000.0
CMP_BLOCK, CMP_STRIDE, CMP_HIDDEN = 32, 16, 128
SLC_BLOCK, N_SLC, WINDOW = 64, 16, 512
Q_BLOCK = 256
FORCE_SCORE = 1e6
MASK_VALUE = -1e30
RWKV_HEADS, RWKV_HEAD_DIM = 8, 64
RWKV_WIDTH = RWKV_HEADS * RWKV_HEAD_DIM
RWKV_LORA_W, RWKV_LORA_A, RWKV_LORA_G = 64, 64, 128
RWKV_COLS = 3 * RWKV_WIDTH + RWKV_LORA_W + RWKV_LORA_A + RWKV_LORA_G
RWKV_GN_EPS = 64e-5
RWKV_CHUNK = 64
RWKV_PACK = 4
RET_HEADS, RET_DK, RET_DV, RET_CHUNK = 4, 64, 128, 128
RET_THETA = 10000.0
RET_GN_EPS = 1e-5
N_EXPERTS, TOP_K, EXPERT_FF = 64, 8, 256
N_EXPERT_GROUPS, TOPK_GROUPS = 8, 4
EXPERTS_PER_GROUP = N_EXPERTS // N_EXPERT_GROUPS
ROUTED_SCALE = 2.5
MOE_EXPERTS_PER_STEP = 4


def _params(*sem):
    return pltpu.CompilerParams(dimension_semantics=sem, vmem_limit_bytes=VMEM_LIMIT_BYTES)


def _dot(a, b, **kw):
    return jnp.dot(a, b, preferred_element_type=F32, **kw)


def _dot_nt(a, b, **kw):
    return lax.dot_general(a, b, (((1,), (1,)), ((), ())), preferred_element_type=F32, **kw)


def _dot_tn(a, b, **kw):
    return lax.dot_general(a, b, (((0,), (0,)), ((), ())), preferred_element_type=F32, **kw)


def _dot_split(a, b, split, parts):
    rest = a if split == 'a' else b
    acc = None
    for _ in range(parts):
        piece = rest.astype(BF16)
        term = _dot(piece, b) if split == 'a' else _dot(a, piece)
        acc = term if acc is None else acc + term
        rest = rest - piece.astype(F32)
    return acc


def _run_interleaved(gens):
    results = [None] * len(gens)
    live = list(range(len(gens)))
    while live:
        for i in list(live):
            try:
                next(gens[i])
            except StopIteration as done:
                results[i] = done.value
                live.remove(i)
    return results


def _gelu(x):
    return 0.5 * x * (1.0 + jnp.tanh(math.sqrt(2.0 / math.pi) * (x + 0.044715 * (x * x * x))))


def _sigmoid(x):
    return 1.0 / (1.0 + jnp.exp(-x))


def _layer_norm_rows(z, g, b):
    mu = jnp.mean(z, axis=-1, keepdims=True)
    zc = z - mu
    var = jnp.mean(zc * zc, axis=-1, keepdims=True)
    return zc * lax.rsqrt(var + LN_EPS) * g + b


def _proj_kernel(x_ref, w_ref, o_ref, *chunked_ref):
    y = _dot(x_ref[...].astype(BF16), w_ref[...])
    o_ref[...] = y
    for c_ref in chunked_ref:
        rows, t, w = c_ref.shape
        c_ref[...] = y[:, :w].reshape(rows, t, w)


def project(x, w_bf16, tm, chunked=None):
    m, k = x.shape
    n = w_bf16.shape[1]
    out_specs = [pl.BlockSpec((tm, n), lambda i: (i, 0))]
    out_shape = [jax.ShapeDtypeStruct((m, n), F32)]
    if chunked is not None:
        t, w = chunked
        out_specs.append(pl.BlockSpec((tm // t, t, w), lambda i: (i, 0, 0)))
        out_shape.append(jax.ShapeDtypeStruct((m // t, t, w), F32))
    out = pl.pallas_call(
        _proj_kernel,
        grid=(m // tm,),
        in_specs=[pl.BlockSpec((tm, k), lambda i: (i, 0)), pl.BlockSpec((k, n), lambda i: (0, 0))],
        out_specs=out_specs,
        out_shape=out_shape,
        compiler_params=_params("parallel"),
        name="project",
    )(x, w_bf16)
    return out if chunked is not None else out[0]


def _out_proj_ln_kernel(ya_ref, yb_ref, wa_ref, wb_ref, x_ref, g_ref, b_ref, o_ref):
    mix = _dot(ya_ref[...], wa_ref[...]) + _dot(yb_ref[...], wb_ref[...])
    o_ref[...] = _layer_norm_rows(ALPHA * x_ref[...] + mix, g_ref[...], b_ref[...])


def out_proj_ln(ya, yb, w_out, x, g, b, tm=512):
    n, d = x.shape
    ka, kb = ya.shape[1], yb.shape[1]
    wa = w_out[:ka].astype(BF16)
    wb = w_out[ka:].astype(BF16)
    row = lambda i: (i, 0)
    fixed = lambda i: (0, 0)
    return pl.pallas_call(
        _out_proj_ln_kernel,
        grid=(n // tm,),
        in_specs=[pl.BlockSpec((tm, ka), row), pl.BlockSpec((tm, kb), row),
                  pl.BlockSpec((ka, d), fixed), pl.BlockSpec((kb, d), fixed),
                  pl.BlockSpec((tm, d), row), pl.BlockSpec((1, d), fixed), pl.BlockSpec((1, d), fixed)],
        out_specs=pl.BlockSpec((tm, d), row),
        out_shape=jax.ShapeDtypeStruct((n, d), F32),
        compiler_params=_params("parallel"),
        name="out_proj_ln",
    )(ya, yb, wa, wb, x, g.reshape(1, d), b.reshape(1, d))


def _router_kernel(x_ref, rt_ref, bias_ref, o_ref):
    tr = x_ref.shape[0]
    scores = _sigmoid(_dot_nt(rt_ref[...], x_ref[...], precision=HIGHEST))
    biased = scores + bias_ref[...]
    grp = biased.reshape(N_EXPERT_GROUPS, EXPERTS_PER_GROUP, tr)
    pos = lax.broadcasted_iota(jnp.int32, grp.shape, 1)
    m1 = jnp.max(grp, axis=1, keepdims=True)
    first = jnp.min(jnp.where(grp == m1, pos, EXPERTS_PER_GROUP), axis=1, keepdims=True)
    m2 = jnp.max(jnp.where(pos == first, -jnp.inf, grp), axis=1, keepdims=True)
    gscore = (m1 + m2).reshape(N_EXPERT_GROUPS, tr)
    gidx = lax.broadcasted_iota(jnp.int32, gscore.shape, 0)
    grank = jnp.zeros(gscore.shape, F32)
    for j in range(N_EXPERT_GROUPS):
        row = gscore[j:j + 1, :]
        grank = grank + jnp.where(gidx > j, jnp.where(row >= gscore, 1.0, 0.0), jnp.where(row > gscore, 1.0, 0.0))
    gkeep = jnp.where(grank < TOPK_GROUPS, 1.0, 0.0)
    keep = jnp.broadcast_to(gkeep[:, None, :], grp.shape).reshape(N_EXPERTS, tr)
    masked = jnp.where(keep > 0.5, biased, -jnp.inf)
    eidx = lax.broadcasted_iota(jnp.int32, masked.shape, 0)
    rank = jnp.zeros(masked.shape, F32)
    for j in range(N_EXPERTS):
        row = masked[j:j + 1, :]
        rank = rank + jnp.where(eidx > j, jnp.where(row >= masked, 1.0, 0.0), jnp.where(row > masked, 1.0, 0.0))
    gate = jnp.where(rank < TOP_K, scores, 0.0)
    gate = gate / jnp.sum(gate, axis=0, keepdims=True) * ROUTED_SCALE
    o_ref[...] = jnp.concatenate([gate, jnp.zeros((LANES - N_EXPERTS, tr), F32)], axis=0).T


def moe_router(x, router, bias, tr=512):
    n, d = x.shape
    return pl.pallas_call(
        _router_kernel,
        grid=(n // tr,),
        in_specs=[pl.BlockSpec((tr, d), lambda i: (i, 0)),
                  pl.BlockSpec((N_EXPERTS, d), lambda i: (0, 0)),
                  pl.BlockSpec((N_EXPERTS, 1), lambda i: (0, 0))],
        out_specs=pl.BlockSpec((tr, LANES), lambda i: (i, 0)),
        out_shape=jax.ShapeDtypeStruct((n, LANES), F32),
        compiler_params=_params("parallel"),
        name="moe_router",
    )(x, router.T, bias.reshape(N_EXPERTS, 1))


def _quantize_fp8(a, axes):
    amax = jnp.max(jnp.abs(a), axis=axes, keepdims=True)
    scale = jnp.maximum(amax, FP8_TINY) * (1.0 / FP8_MAX)
    return (a * (1.0 / scale)).astype(FP8), scale


def _swiglu_hidden(xq, x_scale, w1q, w3q, w_scale, gate=None):
    col1 = x_scale * w_scale[0:1, 0:1]
    col3 = x_scale * w_scale[1:2, 0:1]
    if gate is not None:
        col3 = col3 * gate
    h1 = _dot(xq, w1q) * col1
    return h1 * _sigmoid(h1) * (_dot(xq, w3q) * col3)


def _experts_ln_kernel(x_ref, gate_ref, w1_ref, w3_ref, ws_ref, w2_ref, sw1_ref, sw3_ref, sws_ref, sw2_ref,
                       g_ref, b_ref, o_ref, obf_ref, acc_ref, xq_ref, xs_ref):
    step = pl.program_id(1)
    per_step = w1_ref.shape[0]

    @pl.when(step == 0)
    def _():
        xq, xs = _quantize_fp8(x_ref[...], (1,))
        xq_ref[...] = xq
        xs_ref[...] = xs
        h = _swiglu_hidden(xq, xs, sw1_ref[...], sw3_ref[...], sws_ref[...])
        acc_ref[...] = _dot(h.astype(BF16), sw2_ref[...].astype(BF16))

    lane = lax.broadcasted_iota(jnp.int32, gate_ref.shape, 1)
    gates = gate_ref[...]
    xq, xs = xq_ref[...], xs_ref[...]
    hidden = []
    for j in range(per_step):
        gcol = jnp.sum(jnp.where(lane == step * per_step + j, gates, 0.0), axis=1, keepdims=True)
        hidden.append(_swiglu_hidden(xq, xs, w1_ref[j], w3_ref[j], ws_ref[j], gcol).astype(BF16))
    w2 = w2_ref[...]
    acc_ref[...] += _dot(jnp.concatenate(hidden, axis=1), w2.reshape(per_step * w2.shape[1], w2.shape[2]))

    @pl.when(step == pl.num_programs(1) - 1)
    def _():
        y = _layer_norm_rows(ALPHA * x_ref[...] + acc_ref[...], g_ref[...], b_ref[...])
        o_ref[...] = y
        obf_ref[...] = y.astype(BF16)


def _quantize_expert_weights(w1, w3):
    w1q, s1 = _quantize_fp8(w1, (-2, -1))
    w3q, s3 = _quantize_fp8(w3, (-2, -1))
    scales = jnp.broadcast_to(jnp.concatenate([s1, s3], axis=-2), s1.shape[:-2] + (2, w1.shape[-1]))
    return w1q, w3q, scales


def moe_experts_ln(x, gates, w1, w3, w2_layers, layer, sw1, sw3, sw2, g, b, tm=1024):
    n, d = x.shape
    ne = w1.shape[0]
    w1q, w3q, ws = _quantize_expert_weights(w1, w3)
    sw1q, sw3q, sws = _quantize_expert_weights(sw1, sw3)
    tok = lambda i, e: (i, 0)
    fixed = lambda i, e: (0, 0)
    per_expert = lambda *blk: pl.BlockSpec((MOE_EXPERTS_PER_STEP,) + blk, lambda i, e: (e, 0, 0))
    return pl.pallas_call(
        _experts_ln_kernel,
        grid=(n // tm, ne // MOE_EXPERTS_PER_STEP),
        in_specs=[pl.BlockSpec((tm, d), tok), pl.BlockSpec((tm, LANES), tok),
                  per_expert(d, EXPERT_FF), per_expert(d, EXPERT_FF), per_expert(2, EXPERT_FF),
                  per_expert(EXPERT_FF, d),
                  pl.BlockSpec((d, EXPERT_FF), fixed), pl.BlockSpec((d, EXPERT_FF), fixed),
                  pl.BlockSpec((2, EXPERT_FF), fixed), pl.BlockSpec((EXPERT_FF, d), fixed),
                  pl.BlockSpec((1, d), fixed), pl.BlockSpec((1, d), fixed)],
        out_specs=[pl.BlockSpec((tm, d), tok), pl.BlockSpec((tm, d), tok)],
        out_shape=[jax.ShapeDtypeStruct((n, d), F32), jax.ShapeDtypeStruct((n, d), BF16)],
        scratch_shapes=[pltpu.VMEM((tm, d), F32), pltpu.VMEM((tm, d), FP8), pltpu.VMEM((tm, 1), F32)],
        compiler_params=_params("parallel", "arbitrary"),
        name="moe_experts_ln",
    )(x, gates, w1q, w3q, ws, w2_layers[layer].astype(BF16), sw1q, sw3q, sws, sw2, g.reshape(1, d), b.reshape(1, d))


def moe_block(x, router, bias, w1, w3, w2_layers, layer, sw1, sw3, sw2, g, b):
    gates = moe_router(x, router, bias)
    return moe_experts_ln(x, gates, w1, w3, w2_layers, layer, sw1, sw3, sw2, g, b)


def _s5_tables(lam_re, lam_im, log_dt, b_re, b_im, c_re, c_im, n_chunk):
    t, h, p = S5_CHUNK, S5_GROUP_CH, S5_STATE
    dt = jnp.exp(log_dt.astype(F32))[:, None]
    den = lam_re ** 2 + lam_im ** 2

    def lam_pow(k):
        k = jnp.asarray(k, F32)[..., None, None]
        mag = jnp.exp(lam_re * dt * k)
        return mag * jnp.cos(lam_im * dt * k), mag * jnp.sin(lam_im * dt * k)

    lb_re, lb_im = lam_pow(1.0)
    f_re = ((lb_re - 1.0) * lam_re + lb_im * lam_im) / den
    f_im = (lb_im * lam_re - (lb_re - 1.0) * lam_im) / den
    bb_re = f_re[..., None] * b_re - f_im[..., None] * b_im
    bb_im = f_re[..., None] * b_im + f_im[..., None] * b_re
    pr, pi = lam_pow(jnp.arange(t))
    cl_re = c_re[None] * pr[:, :, None, :] - c_im[None] * pi[:, :, None, :]
    cl_im = c_re[None] * pi[:, :, None, :] + c_im[None] * pr[:, :, None, :]
    klag = jnp.einsum('tgop,gpi->tgoi', cl_re, bb_re) - jnp.einsum('tgop,gpi->tgoi', cl_im, bb_im)
    nb = S5_GROUPS // S5_PACK
    split = lambda a, axis: a.reshape(a.shape[:axis] + (nb, S5_PACK) + a.shape[axis + 1:])
    eye = jnp.eye(S5_PACK, dtype=F32)
    lag_t = jnp.transpose(split(klag, 1), (1, 0, 2, 4, 3))
    lag_t = (lag_t[:, :, :, :, None, :] * eye[None, None, :, None, :, None]).reshape(nb, t, LANES, LANES)
    qr, qi = lam_pow(t - 1 - jnp.arange(t))
    st_re = qr[..., None] * bb_re[None] - qi[..., None] * bb_im[None]
    st_im = qr[..., None] * bb_im[None] + qi[..., None] * bb_re[None]
    st = jnp.stack([st_re, st_im], axis=0)
    st_t = jnp.transpose(split(st, 2), (2, 1, 3, 5, 0, 4)).reshape(nb, t, LANES, 2 * p)
    er, ei = lam_pow(jnp.arange(t) + 1)
    x_re = c_re[None] * er[:, :, None, :] - c_im[None] * ei[:, :, None, :]
    x_im = c_re[None] * ei[:, :, None, :] + c_im[None] * er[:, :, None, :]
    cr = jnp.stack([x_re, -x_im], axis=0)
    cr_t = jnp.transpose(split(cr, 2), (2, 1, 0, 5, 3, 4)).reshape(nb, t, 2 * p, LANES)
    levels = max(1, int(math.log2(n_chunk)))
    sr, si = lam_pow(t * (2.0 ** jnp.arange(levels)))
    sr = sr.reshape(levels, nb, S5_PACK * p)
    si = si.reshape(levels, nb, S5_PACK * p)
    a1 = jnp.concatenate([sr, sr], axis=-1)
    a2 = jnp.concatenate([-si, si], axis=-1)
    scan = jnp.transpose(jnp.stack([a1, a2], axis=1), (2, 0, 1, 3))
    return lag_t, st_t, cr_t, scan.astype(F32)


def _s5_build_tables(lag_ref, st_ref, cr_ref, wtoe_ref, wstate_ref, wcross_ref):
    t = lag_ref.shape[1]
    p = S5_STATE
    kp = S5_PACK * p
    wtoe_ref[...] = jnp.zeros_like(wtoe_ref)
    for d in range(t):
        tile = lag_ref[0, d].astype(BF16)
        for j in range(t - d):
            wtoe_ref[j * LANES:(j + 1) * LANES, (j + d) * LANES:(j + d + 1) * LANES] = tile
    lane = lax.broadcasted_iota(jnp.int32, (LANES, LANES), 1)
    row_g = lax.broadcasted_iota(jnp.int32, (LANES, kp), 0) // S5_GROUP_CH
    same_s = row_g == lax.broadcasted_iota(jnp.int32, (LANES, kp), 1) // p
    for j in range(t):
        a = st_ref[0, j]
        swapped = pltpu.roll(a, p, 1)
        for c, both in enumerate((jnp.where(lane < p, a, swapped), jnp.where(lane < p, swapped, a))):
            wide = jnp.concatenate([both] * (kp // LANES), axis=1)
            wstate_ref[j * LANES:(j + 1) * LANES, c * kp:(c + 1) * kp] = jnp.where(same_s, wide, 0.0).astype(BF16)
    same_c = (lax.broadcasted_iota(jnp.int32, (kp, LANES), 0) // p
              == lax.broadcasted_iota(jnp.int32, (kp, LANES), 1) // S5_GROUP_CH)
    for i in range(t):
        a = cr_ref[0, i]
        for c in range(2):
            tall = jnp.concatenate([a[c * p:(c + 1) * p]] * S5_PACK, axis=0)
            wcross_ref[c * kp:(c + 1) * kp, i * LANES:(i + 1) * LANES] = jnp.where(same_c, tall, 0.0).astype(BF16)


def _s5_kernel(u_ref, lag_ref, st_ref, cr_ref, scan_ref, o_ref, wtoe_ref, wstate_ref, wcross_ref):
    @pl.when(pl.program_id(1) == 0)
    def _():
        _s5_build_tables(lag_ref, st_ref, cr_ref, wtoe_ref, wstate_ref, wcross_ref)

    n_chunk, t, _ = u_ref.shape
    x = jnp.concatenate([u_ref[:, j, :] for j in range(t)], axis=1).astype(BF16)
    local = _dot(x, wtoe_ref[...])
    state = _dot(x, wstate_ref[...])
    row = lax.broadcasted_iota(jnp.int32, state.shape, 0)
    s = jnp.where(row >= 1, pltpu.roll(state, 1, 0), 0.0)
    half = state.shape[1] // 2
    level = 0
    d = 1
    while d < n_chunk:
        mult = scan_ref[0, level]
        prev = jnp.where(row >= d, pltpu.roll(s, d, 0), 0.0)
        s = s + mult[0:1, :] * prev + mult[1:2, :] * pltpu.roll(prev, half, 1)
        d *= 2
        level += 1
    y = local + _dot(s.astype(BF16), wcross_ref[...])
    for i in range(t):
        o_ref[:, i, :] = y[:, i * LANES:(i + 1) * LANES]


def s5_scan(u3, bsz, lag_t, st_t, cr_t, scan):
    rows, t, w = u3.shape
    n_chunk = rows // bsz
    kp2 = 2 * S5_PACK * S5_STATE
    table = lambda a: pl.BlockSpec((1,) + a.shape[1:], lambda j, b: (j, 0, 0, 0))
    return pl.pallas_call(
        _s5_kernel,
        grid=(w // LANES, bsz),
        in_specs=[pl.BlockSpec((n_chunk, t, LANES), lambda j, b: (b, 0, j)),
                  table(lag_t), table(st_t), table(cr_t), table(scan)],
        out_specs=pl.BlockSpec((n_chunk, t, LANES), lambda j, b: (b, 0, j)),
        out_shape=jax.ShapeDtypeStruct(u3.shape, F32),
        scratch_shapes=[pltpu.VMEM((t * LANES, t * LANES), BF16), pltpu.VMEM((t * LANES, kp2), BF16),
                        pltpu.VMEM((kp2, t * LANES), BF16)],
        compiler_params=_params("arbitrary", "arbitrary"),
        name="s5_scan",
    )(u3, lag_t, st_t, cr_t, scan)


def _s5_post_kernel(y_ref, u_ref, d_ref, w_ref, b_ref, o_ref):
    u = u_ref[...]
    y = _gelu(y_ref[...].reshape(u.shape) + d_ref[...] * u)
    o_ref[...] = (y * _sigmoid(_dot(y.astype(BF16), w_ref[...]) + b_ref[...])).astype(o_ref.dtype)


def s5_post(y3, h, d_skip, w_glu, b_glu, tm=1024):
    rows, t, w = y3.shape
    n = rows * t
    tm = min(tm, n)
    row = lambda i: (i, 0)
    fixed = lambda i: (0, 0)
    return pl.pallas_call(
        _s5_post_kernel,
        grid=(n // tm,),
        in_specs=[pl.BlockSpec((tm // t, t, w), lambda i: (i, 0, 0)), pl.BlockSpec((tm, w), row),
                  pl.BlockSpec((1, w), fixed), pl.BlockSpec((w, w), fixed), pl.BlockSpec((1, w), fixed)],
        out_specs=pl.BlockSpec((tm, w), row),
        out_shape=jax.ShapeDtypeStruct((n, w), BF16),
        compiler_params=_params("parallel"),
        name="s5_post",
    )(y3, h, d_skip.reshape(1, w), w_glu.astype(BF16), b_glu.reshape(1, w))


def s5_mixer(h, u3, bsz, seq, lam_re, lam_im, log_dt, b_re, b_im, c_re, c_im, d_skip, w_glu, b_glu):
    tables = _s5_tables(lam_re, lam_im, log_dt, b_re, b_im, c_re, c_im, seq // S5_CHUNK)
    return s5_post(s5_scan(u3, bsz, *tables), h, d_skip, w_glu, b_glu)


def _rope_tables(pos, rot_dim, theta, head_dim, n_heads):
    half = rot_dim // 2
    f32 = np.float32
    inv_freq = f32(theta) ** (-np.arange(half, dtype=f32) / f32(half))
    ang = (pos.astype(f32)[:, None] * inv_freq[None, :]).astype(np.float64)
    cos, sin = np.cos(ang), np.sin(ang)
    rest = head_dim - rot_dim
    n = pos.shape[0]
    c = np.concatenate([cos, cos, np.ones((n, rest))], axis=1)
    s_up = np.concatenate([-sin, np.zeros((n, half + rest))], axis=1)
    s_dn = np.concatenate([np.zeros((n, half)), sin, np.zeros((n, rest))], axis=1)
    tile = lambda a: jnp.asarray(np.tile(a, (1, n_heads)), F32)
    return tile(c), tile(s_up), tile(s_dn)


def _rope_apply(x, c, s_up, s_dn, half):
    return x * c + pltpu.roll(x, LANES - half, 1) * s_up + pltpu.roll(x, half, 1) * s_dn


def _retention_tables():
    c = RET_CHUNK
    log_gamma = np.log(1.0 - 2.0 ** (-5.0 - np.arange(RET_HEADS, dtype=np.float64)))
    i = np.arange(c, dtype=np.float64)
    diff = i[:, None] - i[None, :]
    decay = np.where(diff >= 0, np.exp(diff[None] * log_gamma[:, None, None]), 0.0)
    qdec = np.repeat(np.exp((i + 1.0)[:, None] * log_gamma[None, :]), RET_DK, axis=1)
    kdec = np.repeat(np.exp((c - 1.0 - i)[:, None] * log_gamma[None, :]), RET_DK, axis=1)
    chunk_decay = [float(v) for v in np.exp(c * log_gamma)]
    return jnp.asarray(decay, F32), jnp.asarray(qdec, F32), jnp.asarray(kdec, F32), chunk_decay


def _retention_kernel(chunk_decay, q_ref, k_ref, v0_ref, v1_ref, g0_ref, g1_ref, c_ref, su_ref, sd_ref,
                      dec_ref, qdec_ref, kdec_ref, lng_ref, lnb_ref, o_ref, state_ref):
    @pl.when(pl.program_id(1) == 0)
    def _():
        state_ref[...] = jnp.zeros_like(state_ref)

    half = RET_DK // 2
    tabs = (c_ref[...], su_ref[...], sd_ref[...])
    q = jnp.concatenate([_rope_apply(q_ref[:, s:s + LANES], *tabs, half) for s in (0, LANES)], axis=1)
    k = jnp.concatenate([_rope_apply(k_ref[:, s:s + LANES], *tabs, half) for s in (0, LANES)], axis=1)
    k = k * (RET_DK ** -0.5)
    q_dec = q * qdec_ref[...]
    k_dec = k * kdec_ref[...]
    v = jnp.concatenate([v0_ref[...], v1_ref[...]], axis=1)
    gate = jnp.concatenate([g0_ref[...], g1_ref[...]], axis=1)
    states = [state_ref[h] for h in range(RET_HEADS)]

    def head(h):
        ks = slice(h * RET_DK, (h + 1) * RET_DK)
        vh = v[:, h * RET_DV:(h + 1) * RET_DV].astype(BF16)
        scores = _dot_nt(q[:, ks].astype(BF16), k[:, ks].astype(BF16)) * dec_ref[h]
        yield
        y = _dot(scores.astype(BF16), vh) + _dot(q_dec[:, ks].astype(BF16), states[h].astype(BF16))
        yield
        new_state = states[h] * chunk_decay[h] + _dot_tn(k_dec[:, ks].astype(BF16), vh)
        yield
        mu = jnp.mean(y, axis=-1, keepdims=True)
        yc = y - mu
        var = jnp.mean(yc * yc, axis=-1, keepdims=True)
        return yc * lax.rsqrt(var + RET_GN_EPS), new_state

    results = _run_interleaved([head(h) for h in range(RET_HEADS)])
    outs = [r[0] for r in results]
    for h, (_, new_state) in enumerate(results):
        state_ref[h] = new_state
    yn = jnp.concatenate(outs, axis=1) * lng_ref[...] + lnb_ref[...]
    o_ref[...] = (gate * _sigmoid(gate) * yn).astype(o_ref.dtype)


def retention_mixer(h, bsz, seq, col0, ln_g, ln_b):
    c = RET_CHUNK
    n_chunk = seq // c
    qk_w = RET_HEADS * RET_DK
    v_w = RET_HEADS * RET_DV
    assert col0 % qk_w == 0 and qk_w == 2 * LANES and v_w == 2 * qk_w
    cb = col0 // qk_w
    rc, rsu, rsd = _rope_tables(np.arange(seq), RET_DK, RET_THETA, RET_DK, 2)
    dec, qdec, kdec, chunk_decay = _retention_tables()
    row = lambda j: (lambda b, n: (b * n_chunk + n, j))
    pos = lambda b, n: (n, 0)
    fixed2 = lambda b, n: (0, 0)
    kern = functools.partial(_retention_kernel, chunk_decay)
    return pl.pallas_call(
        kern,
        grid=(bsz, n_chunk),
        in_specs=[pl.BlockSpec((c, qk_w), row(cb)), pl.BlockSpec((c, qk_w), row(cb + 1)),
                  pl.BlockSpec((c, qk_w), row(cb + 2)), pl.BlockSpec((c, qk_w), row(cb + 3)),
                  pl.BlockSpec((c, qk_w), row(cb + 4)), pl.BlockSpec((c, qk_w), row(cb + 5)),
                  pl.BlockSpec((c, LANES), pos), pl.BlockSpec((c, LANES), pos), pl.BlockSpec((c, LANES), pos),
                  pl.BlockSpec((RET_HEADS, c, c), lambda b, n: (0, 0, 0)),
                  pl.BlockSpec((c, qk_w), fixed2), pl.BlockSpec((c, qk_w), fixed2),
                  pl.BlockSpec((1, v_w), fixed2), pl.BlockSpec((1, v_w), fixed2)],
        out_specs=pl.BlockSpec((c, v_w), lambda b, n: (b * n_chunk + n, 0)),
        out_shape=jax.ShapeDtypeStruct((bsz * seq, v_w), BF16),
        scratch_shapes=[pltpu.VMEM((RET_HEADS, RET_DK, RET_DV), F32)],
        compiler_params=_params("parallel", "arbitrary"),
        name="retention",
    )(h, h, h, h, h, h, rc, rsu, rsd, dec, qdec, kdec, ln_g.reshape(1, v_w), ln_b.reshape(1, v_w))


NSA_KV_W = NSA_KV_GROUPS * NSA_HEAD_DIM
NSA_GATE_COLS = 3 * NSA_HEADS


def _nsa_prep_kernel(q_ref, kvc_ref, kvs_ref, kvw_ref, c_ref, su_ref, sd_ref,
                     qo_ref, kc_ref, vc_ref, ks_ref, vs_ref, kw_ref, vw_ref):
    half = NSA_ROT_DIM // 2
    tabs = (c_ref[...], su_ref[...], sd_ref[...])
    scale = NSA_HEAD_DIM ** -0.5 * math.log2(math.e)
    q = jnp.concatenate(
        [_rope_apply(q_ref[:, s:s + LANES], *tabs, half) * scale for s in range(0, NSA_WIDTH, LANES)], axis=1)
    qo_ref[0] = q.T.astype(qo_ref.dtype)

    def split(x, o_ref):
        for g in range(NSA_KV_GROUPS):
            o_ref[0, g] = x[:, g * NSA_HEAD_DIM:(g + 1) * NSA_HEAD_DIM].astype(o_ref.dtype)

    def split_t(x, o_ref):
        xt = x.T
        for g in range(NSA_KV_GROUPS):
            o_ref[0, g] = xt[g * NSA_HEAD_DIM:(g + 1) * NSA_HEAD_DIM, :].astype(o_ref.dtype)

    split(kvc_ref[:, :NSA_KV_W], kc_ref)
    split(kvc_ref[:, NSA_KV_W:], vc_ref)
    split(_rope_apply(kvs_ref[:, :NSA_KV_W], *tabs, half), ks_ref)
    split_t(kvs_ref[:, NSA_KV_W:], vs_ref)
    split(_rope_apply(kvw_ref[:, :NSA_KV_W], *tabs, half), kw_ref)
    split_t(kvw_ref[:, NSA_KV_W:], vw_ref)


def nsa_prep(h, bsz, seq, tl=512):
    tl = min(tl, seq)
    nl = seq // tl
    rc, rsu, rsd = _rope_tables(np.arange(seq), NSA_ROT_DIM, ROPE_THETA, NSA_HEAD_DIM, LANES // NSA_HEAD_DIM)
    row = lambda j: (lambda b, l: (b * nl + l, j))
    pos = lambda b, l: (l, 0)
    kv_out = pl.BlockSpec((1, NSA_KV_GROUPS, tl, NSA_HEAD_DIM), lambda b, l: (b, 0, l, 0))
    kv_shape = lambda dt: jax.ShapeDtypeStruct((bsz, NSA_KV_GROUPS, seq, NSA_HEAD_DIM), dt)
    vt_out = pl.BlockSpec((1, NSA_KV_GROUPS, NSA_HEAD_DIM, tl), lambda b, l: (b, 0, 0, l))
    vt_shape = jax.ShapeDtypeStruct((bsz, NSA_KV_GROUPS, NSA_HEAD_DIM, seq), BF16)
    two = 2 * NSA_KV_W
    return pl.pallas_call(
        _nsa_prep_kernel,
        grid=(bsz, nl),
        in_specs=[pl.BlockSpec((tl, NSA_WIDTH), row(1)),
                  pl.BlockSpec((tl, two), row(4)), pl.BlockSpec((tl, two), row(5)), pl.BlockSpec((tl, two), row(6)),
                  pl.BlockSpec((tl, LANES), pos), pl.BlockSpec((tl, LANES), pos), pl.BlockSpec((tl, LANES), pos)],
        out_specs=[pl.BlockSpec((1, NSA_WIDTH, tl), lambda b, l: (b, 0, l)),
                   kv_out, kv_out, kv_out, vt_out, kv_out, vt_out],
        out_shape=[jax.ShapeDtypeStruct((bsz, NSA_WIDTH, seq), BF16),
                   kv_shape(F32), kv_shape(F32), kv_shape(BF16), vt_shape, kv_shape(BF16), vt_shape],
        compiler_params=_params("parallel", "parallel"),
        name="nsa_prep",
    )(h, h, h, h, rc, rsu, rsd)


def _nsa_compress_kernel(hk_ref, hv_ref, pek_ref, pev_ref, kw1_ref, kb1_ref, kw2_ref, vw1_ref, vb1_ref, vw2_ref,
                         c_ref, su_ref, sd_ref, ko_ref, vo_ref):
    def mlp(h_ref, pe_ref, w1_ref, b1_ref, w2_ref):
        hb = h_ref[0, 0]
        rows = hb.shape[0]
        first = _dot((hb + pe_ref[0:1, :]).astype(BF16), w1_ref[0])
        second = _dot((hb + pe_ref[1:2, :]).astype(BF16), w1_ref[1])
        hid = _gelu(first + pltpu.roll(second, rows - 1, 0) + b1_ref[...])
        return _dot(hid.astype(BF16), w2_ref[...])

    kc = _rope_apply(mlp(hk_ref, pek_ref, kw1_ref, kb1_ref, kw2_ref), c_ref[...], su_ref[...], sd_ref[...],
                     NSA_ROT_DIM // 2)
    vc = mlp(hv_ref, pev_ref, vw1_ref, vb1_ref, vw2_ref)
    ko_ref[0, 0] = kc[:, :NSA_HEAD_DIM].astype(ko_ref.dtype)
    vo_ref[0, 0] = vc.T[:NSA_HEAD_DIM, :].astype(vo_ref.dtype)


def nsa_compress(kc, vc, pe_k, pe_v, ck_w1, ck_b1, ck_w2, cv_w1, cv_b1, cv_w2):
    bsz, grp, seq, d = kc.shape
    n_rows = seq // CMP_STRIDE
    flat = CMP_STRIDE * d
    cmp_end = np.arange(n_rows) * CMP_STRIDE + CMP_BLOCK - 1
    rc, rsu, rsd = _rope_tables(cmp_end, NSA_ROT_DIM, ROPE_THETA, NSA_HEAD_DIM, LANES // NSA_HEAD_DIM)
    pad_w2 = lambda w: jnp.pad(w, ((0, 0), (0, LANES - d))).astype(BF16)
    blk = pl.BlockSpec((1, 1, n_rows, flat), lambda b, g: (b, g, 0, 0))
    f2 = lambda b, g: (0, 0)
    f3 = lambda b, g: (0, 0, 0)
    w_specs = [pl.BlockSpec((2, flat, CMP_HIDDEN), f3), pl.BlockSpec((1, CMP_HIDDEN), f2),
               pl.BlockSpec((CMP_HIDDEN, LANES), f2)]
    return pl.pallas_call(
        _nsa_compress_kernel,
        grid=(bsz, grp),
        in_specs=[blk, blk, pl.BlockSpec((2, flat), f2), pl.BlockSpec((2, flat), f2)] + w_specs + w_specs
                 + [pl.BlockSpec((n_rows, LANES), f2)] * 3,
        out_specs=[pl.BlockSpec((1, 1, n_rows, d), lambda b, g: (b, g, 0, 0)),
                   pl.BlockSpec((1, 1, d, n_rows), lambda b, g: (b, g, 0, 0))],
        out_shape=[jax.ShapeDtypeStruct((bsz, grp, n_rows, d), BF16),
                   jax.ShapeDtypeStruct((bsz, grp, d, n_rows), BF16)],
        compiler_params=_params("parallel", "parallel"),
        name="nsa_compress",
    )(kc.reshape(bsz, grp, n_rows, flat), vc.reshape(bsz, grp, n_rows, flat),
      pe_k.reshape(2, flat), pe_v.reshape(2, flat),
      ck_w1.reshape(2, flat, CMP_HIDDEN).astype(BF16), ck_b1.reshape(1, CMP_HIDDEN), pad_w2(ck_w2),
      cv_w1.reshape(2, flat, CMP_HIDDEN).astype(BF16), cv_b1.reshape(1, CMP_HIDDEN), pad_w2(cv_w2),
      rc, rsu, rsd)


def _per_head(x):
    return jnp.concatenate([x] * NSA_HPG, axis=1)


def _nsa_attn_kernel(seq, tk, qt_ref, gate_ref, kc_ref, vct_ref, ks_ref, vst_ref, kw_ref, vwt_ref, mmapt_ref,
                     o_ref, sel_ref):
    n_blk = seq // SLC_BLOCK
    n_sel = min(N_SLC, n_blk)
    hd = NSA_HEAD_DIM
    w = NSA_HPG * hd
    groups = range(NSA_KV_GROUPS)
    q0 = pl.program_id(1) * Q_BLOCK
    t_l = q0 + lax.broadcasted_iota(jnp.int32, (1, Q_BLOCK), 1)

    def select(g):
        qg = qt_ref[0, g * w:(g + 1) * w, :]
        qst = jnp.concatenate([qg[h * hd:(h + 1) * hd, :] for h in range(NSA_HPG)], axis=1)
        kc = kc_ref[0, g]
        n_cmp = kc.shape[0]
        cmp_end = lax.broadcasted_iota(jnp.int32, (n_cmp, 1), 0) * CMP_STRIDE + (CMP_BLOCK - 1)
        s = _dot(kc, qst) + _per_head(jnp.where(cmp_end <= t_l, 0.0, MASK_VALUE))
        yield
        p = jnp.exp2(s - jnp.max(s, axis=0, keepdims=True))
        any_key = _per_head(jnp.where(t_l >= CMP_BLOCK - 1, 1.0, 0.0))
        inv_l = any_key / jnp.sum(p, axis=0, keepdims=True)
        o_cmp = _dot(vct_ref[0, g], p.astype(BF16)) * inv_l
        yield
        p = p * inv_l
        imp = p[:, 0:Q_BLOCK]
        for h in range(1, NSA_HPG):
            imp = imp + p[:, h * Q_BLOCK:(h + 1) * Q_BLOCK]
        imp_slc = _dot_split(mmapt_ref[...], imp, 'b', 3)
        yield
        blk = lax.broadcasted_iota(jnp.int32, (n_blk, 1), 0)
        cur = t_l // SLC_BLOCK
        score = jnp.where(blk == 0, FORCE_SCORE,
                          jnp.where(blk == cur, FORCE_SCORE, jnp.where(blk == cur - 1, FORCE_SCORE, imp_slc)))
        score = jnp.where(blk * SLC_BLOCK <= t_l, score, -FORCE_SCORE)
        sel = jnp.zeros((n_blk, Q_BLOCK), F32)
        for _ in range(n_sel):
            best = jnp.max(score, axis=0, keepdims=True)
            idx = jnp.min(jnp.where(score == best, blk, n_blk), axis=0, keepdims=True)
            pick = blk == idx
            sel = jnp.where(pick, 1.0, sel)
            score = jnp.where(pick, -jnp.inf, score)
            yield
        sel_ref[g] = sel
        return qst, o_cmp

    selected = _run_interleaved([select(g) for g in groups])
    qst = [r[0] for r in selected]
    o_cmp = [r[1] for r in selected]

    blocks_per_tile = tk // SLC_BLOCK

    def slc_tile(kt, carry):
        k0 = pl.multiple_of(kt * tk, tk)
        causal = k0 + lax.broadcasted_iota(jnp.int32, (tk, 1), 0) <= t_l
        scores = [_dot(ks_ref[0, g, pl.ds(k0, tk), :], qst[g]) for g in groups]
        out = []
        for g in groups:
            m, l, acc = carry[g]
            chosen = jnp.concatenate(
                [jnp.broadcast_to(sel_ref[g, pl.ds(kt * blocks_per_tile + j, 1), :], (SLC_BLOCK, Q_BLOCK))
                 for j in range(blocks_per_tile)], axis=0)
            bias = jnp.where(causal, jnp.where(chosen > 0.5, 0.0, MASK_VALUE), MASK_VALUE)
            s = scores[g] + _per_head(bias)
            m_new = jnp.maximum(m, jnp.max(s, axis=0, keepdims=True))
            alpha = jnp.exp2(m - m_new)
            p = jnp.exp2(s - m_new)
            l = alpha * l + jnp.sum(p, axis=0, keepdims=True)
            acc = alpha * acc + _dot(vst_ref[0, g, :, pl.ds(k0, tk)], p.astype(BF16))
            out.append((m_new, l, acc))
        return tuple(out)

    n_tiles = (q0 + Q_BLOCK + tk - 1) // tk
    cols = NSA_HPG * Q_BLOCK
    init = tuple((jnp.full((1, cols), MASK_VALUE, F32), jnp.zeros((1, cols), F32), jnp.zeros((hd, cols), F32))
                 for _ in groups)
    slc = lax.fori_loop(0, n_tiles, slc_tile, init)

    band = WINDOW + Q_BLOCK
    w0 = pl.multiple_of(jnp.maximum(q0 - WINDOW, 0), Q_BLOCK)
    kpos = w0 + lax.broadcasted_iota(jnp.int32, (band, 1), 0)
    win_bias = _per_head(jnp.where(kpos <= t_l, jnp.where(kpos > t_l - WINDOW, 0.0, MASK_VALUE), MASK_VALUE))
    sig_t = _sigmoid(gate_ref[...]).T

    def finish(g):
        s = _dot(kw_ref[0, g, pl.ds(w0, band), :], qst[g]) + win_bias
        yield
        p = jnp.exp2(s - jnp.max(s, axis=0, keepdims=True))
        o_win = _dot(vwt_ref[0, g, :, pl.ds(w0, band)], p.astype(BF16)) / jnp.sum(p, axis=0, keepdims=True)
        yield
        _, l_slc, acc_slc = slc[g]

        def gate(branch):
            first = (g * NSA_HPG) * 3 + branch
            return jnp.concatenate([sig_t[first + 3 * h:first + 3 * h + 1, :] for h in range(NSA_HPG)], axis=1)

        out_t = gate(0) * o_cmp[g] + gate(1) * (acc_slc / l_slc) + gate(2) * o_win
        pairs = []
        for h in range(0, NSA_HPG, 2):
            two = jnp.concatenate([out_t[:, h * Q_BLOCK:(h + 1) * Q_BLOCK],
                                   out_t[:, (h + 1) * Q_BLOCK:(h + 2) * Q_BLOCK]], axis=0)
            pairs.append(two.T)
        o_ref[:, g * w:(g + 1) * w] = jnp.concatenate(pairs, axis=1).astype(o_ref.dtype)

    _run_interleaved([finish(g) for g in groups])


def _nsa_pool_matrix(seq):
    n_blk = seq // SLC_BLOCK
    n_rows = seq // CMP_STRIDE
    per_stride = SLC_BLOCK // CMP_STRIDE
    span = CMP_BLOCK // CMP_STRIDE
    pool = np.zeros((n_blk, n_rows), np.float32)
    for j in range(n_blk):
        for m in range(per_stride):
            for n in range(span):
                c = per_stride * j + m + n - (span - 1)
                if 0 <= c < n_rows - 1:
                    pool[j, c] += 1.0
    return jnp.asarray(pool, BF16)


def nsa_attention(qt, h, gate_col_block, k_cmp, v_cmp_t, ks, vs_t, kw, vw_t, bsz, seq, tk=1024):
    tk = min(tk, seq)
    nq = seq // Q_BLOCK
    pool = _nsa_pool_matrix(seq)
    n_rows = k_cmp.shape[2]
    d = NSA_HEAD_DIM
    qblk = lambda b, i: (b * nq + i, 0)
    whole = lambda *shape: pl.BlockSpec((1, NSA_KV_GROUPS) + shape, lambda b, i: (b, 0, 0, 0))
    kern = functools.partial(_nsa_attn_kernel, seq, tk)
    return pl.pallas_call(
        kern,
        grid=(bsz, nq),
        in_specs=[pl.BlockSpec((1, NSA_WIDTH, Q_BLOCK), lambda b, i: (b, 0, i)),
                  pl.BlockSpec((Q_BLOCK, LANES), lambda b, i: (b * nq + i, gate_col_block)),
                  whole(n_rows, d), whole(d, n_rows), whole(seq, d), whole(d, seq), whole(seq, d), whole(d, seq),
                  pl.BlockSpec(pool.shape, lambda b, i: (0, 0))],
        out_specs=pl.BlockSpec((Q_BLOCK, NSA_WIDTH), qblk),
        out_shape=jax.ShapeDtypeStruct((bsz * seq, NSA_WIDTH), BF16),
        scratch_shapes=[pltpu.VMEM((NSA_KV_GROUPS, seq // SLC_BLOCK, Q_BLOCK), F32)],
        compiler_params=_params("parallel", "arbitrary"),
        name="nsa_attention",
    )(qt, h, k_cmp, v_cmp_t, ks, vs_t, kw, vw_t, pool)


def nsa_mixer(h, bsz, seq, gate_col_block, pe_k, pe_v, ck_w1, ck_b1, ck_w2, cv_w1, cv_b1, cv_w2):
    qt, kc, vc, ks, vs_t, kw, vw_t = nsa_prep(h, bsz, seq)
    k_cmp, v_cmp_t = nsa_compress(kc, vc, pe_k, pe_v, ck_w1, ck_b1, ck_w2, cv_w1, cv_b1, cv_w2)
    return nsa_attention(qt, h, gate_col_block, k_cmp, v_cmp_t, ks, vs_t, kw, vw_t, bsz, seq)


def _head_ones(width, head_dim):
    idx = np.arange(width) // head_dim
    return jnp.asarray(idx[:, None] == idx[None, :], BF16)


def _softplus(x):
    return jnp.maximum(x, 0.0) + jnp.log(1.0 + jnp.exp(-jnp.abs(x)))


def _rwkv_pre_kernel(p_ref, prev_ref, mu_ref, w0_ref, wup_ref, a0_ref, aup_ref, gup_ref, kk_ref, ka_ref, rk_ref,
                     ones_ref, r_o, k_o, v_o, kk_o, b_o, ld_o, g_o, bonus_o):
    w = RWKV_WIDTH
    p = p_ref[...]
    first_row = jnp.where(pl.program_id(1) == 0, 0.0, prev_ref[7:8, :])
    is_row0 = lax.broadcasted_iota(jnp.int32, p.shape, 0) == 0
    prev = jnp.where(is_row0, first_row, pltpu.roll(p, 1, 0))
    ps = p + (prev - p) * mu_ref[...]
    r, k, v = ps[:, 0:w], ps[:, w:2 * w], ps[:, 2 * w:3 * w]
    o = 3 * w
    w_lo = ps[:, o:o + RWKV_LORA_W]
    a_lo = ps[:, o + RWKV_LORA_W:o + RWKV_LORA_W + RWKV_LORA_A]
    g_lo = ps[:, o + RWKV_LORA_W + RWKV_LORA_A:]
    wlog = -_softplus(-(w0_ref[...] + _dot(jnp.tanh(w_lo).astype(BF16), wup_ref[...]))) - 0.5
    a = _sigmoid(a0_ref[...] + _dot(a_lo.astype(BF16), aup_ref[...]))
    g = _dot(_sigmoid(g_lo).astype(BF16), gup_ref[...])
    kk = k * kk_ref[...]
    norm = jnp.sqrt(_dot_split(kk * kk, ones_ref[...], 'a', 2))
    kk = kk / jnp.maximum(norm, 1e-12)
    k2 = k * (1.0 + (a - 1.0) * ka_ref[...])
    r_o[...] = r
    k_o[...] = k2
    v_o[...] = v
    kk_o[...] = kk
    b_o[...] = kk * a
    ld_o[...] = -jnp.exp(wlog)
    g_o[...] = g
    bonus_o[...] = _dot_split(r * k2 * rk_ref[...], ones_ref[...], 'a', 2) * v


def rwkv_pre(h, bsz, seq, mu, w0, w_up, a0, a_up, g_up, k_k, k_a, r_k, tl=512):
    tl = min(tl, seq)
    nl = seq // tl
    w = RWKV_WIDTH
    cols = RWKV_COLS
    ones = _head_ones(w, RWKV_HEAD_DIM)
    f2 = lambda b, l: (0, 0)
    vec = pl.BlockSpec((1, w), f2)
    out_spec = pl.BlockSpec((tl, w), lambda b, l: (b * nl + l, 0))
    out_shape = jax.ShapeDtypeStruct((bsz * seq, w), F32)
    return pl.pallas_call(
        _rwkv_pre_kernel,
        grid=(bsz, nl),
        in_specs=[pl.BlockSpec((tl, cols), lambda b, l: (b * nl + l, 0)),
                  pl.BlockSpec((8, cols), lambda b, l: (jnp.maximum((b * seq + l * tl) // 8 - 1, 0), 0)),
                  pl.BlockSpec((1, cols), f2), vec, pl.BlockSpec((RWKV_LORA_W, w), f2),
                  vec, pl.BlockSpec((RWKV_LORA_A, w), f2), pl.BlockSpec((RWKV_LORA_G, w), f2),
                  vec, vec, vec, pl.BlockSpec((w, w), f2)],
        out_specs=[out_spec] * 8,
        out_shape=[out_shape] * 8,
        compiler_params=_params("parallel", "parallel"),
        name="rwkv_pre",
    )(h, h, mu.reshape(1, cols), w0.reshape(1, w), w_up.astype(BF16), a0.reshape(1, w), a_up.astype(BF16),
      g_up.astype(BF16), k_k.reshape(1, w), k_a.reshape(1, w), r_k.reshape(1, w), ones)


def _rwkv_masks():
    t, pk = RWKV_CHUNK, RWKV_PACK
    n = t * pk
    ri = np.arange(n)
    same = (ri[:, None] // t) == (ri[None, :] // t)
    tt, ss = ri[:, None] % t, ri[None, :] % t
    levels = []
    k = 1
    while k < t:
        levels.append(same & (tt // (2 * k) == ss // (2 * k)) & ((tt // k) % 2 == 1) & ((ss // k) % 2 == 0))
        k *= 2
    lvl = np.stack(levels).astype(np.float32)
    tri = (np.arange(t)[:, None] >= np.arange(t)[None, :]).astype(np.float32)
    head_lane = ((ri[:, None] // t) == (np.arange(pk * RWKV_HEAD_DIM)[None, :] // RWKV_HEAD_DIM)).astype(np.float32)
    return (jnp.asarray(tri, BF16), jnp.asarray(head_lane), jnp.asarray(same.astype(np.float32)), jnp.asarray(lvl))


def _rwkv_chain(r, k, v, kk, b, ld, st, tri, hl, bd, lvl_ref):
    t, pk = RWKV_CHUNK, RWKV_PACK
    n = t * pk
    c = _dot_split(tri, ld, 'b', 3)
    yield
    c_end = c[t - 1:t, :]
    e_neg = jnp.exp(-c)
    e_end = jnp.exp(c_end - c)
    kkd = (kk * jnp.exp(c - ld)).astype(BF16)
    rd = (r * jnp.exp(c)).astype(BF16)

    def big(x):
        return (jnp.concatenate([x] * pk, axis=0) * hl).astype(BF16)

    st_b = st.astype(BF16)
    v_big = big(v)
    a_all = _dot_nt(jnp.concatenate([kkd, rd], axis=0),
                    jnp.concatenate([big(k * e_neg), big(b * e_neg)], axis=0))
    yield
    ti = lax.broadcasted_iota(jnp.int32, (t, n), 0)
    si = lax.broadcasted_iota(jnp.int32, (t, n), 1) % t
    strict = ti > si
    incl = ti >= si
    a_kk = jnp.where(strict, a_all[:t, :n], 0.0)
    a_kb = jnp.where(strict, a_all[:t, n:], 0.0)
    a_rk = jnp.where(incl, a_all[t:, :n], 0.0)
    a_rb = jnp.where(incl, a_all[t:, n:], 0.0)
    rhs = _dot(kkd, st_b) + _dot(a_kk.astype(BF16), v_big)
    yield
    a_bd = jnp.concatenate([a_kb] * pk, axis=0) * bd
    m = jnp.where(lax.broadcasted_iota(jnp.int32, (n, n), 0) == lax.broadcasted_iota(jnp.int32, (n, n), 1), 1.0, 0.0)
    for lv in range(lvl_ref.shape[0]):
        mb = m.astype(BF16)
        ma = _dot(mb, (a_bd * lvl_ref[lv]).astype(BF16)).astype(BF16)
        yield
        m = m - _dot(ma, mb)
        yield
    u_big = _dot(m.astype(BF16), big(rhs))
    yield
    u = u_big[0:t]
    for h in range(1, pk):
        u = u + u_big[h * t:(h + 1) * t]
    y = _dot(rd, st_b) + _dot(a_rk.astype(BF16), v_big) - _dot(a_rb.astype(BF16), big(u))
    yield
    decay_col = jnp.broadcast_to(jnp.exp(c_end), st.shape).T
    kb_end = jnp.concatenate([k * e_end, -(b * e_end)], axis=0).astype(BF16)
    vu = jnp.concatenate([v, u], axis=0).astype(BF16)
    return y, decay_col * st + bd * _dot_tn(kb_end, vu)


def _rwkv_chunk_kernel(r_ref, k_ref, v_ref, kk_ref, b_ref, ld_ref, tri_ref, hl_ref, bd_ref, lvl_ref, y_ref, st_ref):
    @pl.when(pl.program_id(0) == 0)
    def _():
        st_ref[...] = jnp.zeros_like(st_ref)

    wp = RWKV_PACK * RWKV_HEAD_DIM
    tri, hl, bd = tri_ref[...], hl_ref[...], bd_ref[...]
    n_pack = r_ref.shape[2] // wp
    where = [(bi, slice(g * wp, (g + 1) * wp)) for bi in range(r_ref.shape[0]) for g in range(n_pack)]
    loaded = [tuple(ref[bi, :, cols] for ref in (r_ref, k_ref, v_ref, kk_ref, b_ref, ld_ref)) + (st_ref[i],)
              for i, (bi, cols) in enumerate(where)]
    results = _run_interleaved([_rwkv_chain(*args, tri, hl, bd, lvl_ref) for args in loaded])
    for i, ((bi, cols), (y, st_new)) in enumerate(zip(where, results)):
        y_ref[bi, :, cols] = y
        st_ref[i] = st_new


def rwkv_chunk(r, k, v, kk, b, ld, bsz, seq):
    t, pk = RWKV_CHUNK, RWKV_PACK
    n_chunk = seq // t
    w = RWKV_WIDTH
    wp = pk * RWKV_HEAD_DIM
    assert t == RWKV_HEAD_DIM
    tri, hl, bd, lvl = _rwkv_masks()
    blk = pl.BlockSpec((bsz, t, w), lambda c: (0, c, 0))
    f2 = lambda c: (0, 0)
    shaped = lambda a: a.reshape(bsz, seq, w)
    y = pl.pallas_call(
        _rwkv_chunk_kernel,
        grid=(n_chunk,),
        in_specs=[blk] * 6 + [pl.BlockSpec(tri.shape, f2), pl.BlockSpec(hl.shape, f2), pl.BlockSpec(bd.shape, f2),
                              pl.BlockSpec(lvl.shape, lambda c: (0, 0, 0))],
        out_specs=blk,
        out_shape=jax.ShapeDtypeStruct((bsz, seq, w), F32),
        scratch_shapes=[pltpu.VMEM((bsz * (w // wp), wp, wp), F32)],
        compiler_params=_params("arbitrary"),
        name="rwkv_chunk",
    )(shaped(r), shaped(k), shaped(v), shaped(kk), shaped(b), shaped(ld), tri, hl, bd, lvl)
    return y.reshape(bsz * seq, w)


def _rwkv_post_kernel(y_ref, bonus_ref, g_ref, lng_ref, lnb_ref, ones_ref, o_ref):
    y = y_ref[...]
    inv = 1.0 / RWKV_HEAD_DIM
    mu = _dot_split(y, ones_ref[...], 'a', 2) * inv
    yc = y - mu
    var = _dot_split(yc * yc, ones_ref[...], 'a', 2) * inv
    yn = yc * lax.rsqrt(var + RWKV_GN_EPS) * lng_ref[...] + lnb_ref[...]
    o_ref[...] = ((yn + bonus_ref[...]) * g_ref[...]).astype(o_ref.dtype)


def rwkv_post(y, bonus, g, ln_g, ln_b, tm=1024):
    n, w = y.shape
    tm = min(tm, n)
    row = pl.BlockSpec((tm, w), lambda i: (i, 0))
    vec = pl.BlockSpec((1, w), lambda i: (0, 0))
    return pl.pallas_call(
        _rwkv_post_kernel,
        grid=(n // tm,),
        in_specs=[row, row, row, vec, vec, pl.BlockSpec((w, w), lambda i: (0, 0))],
        out_specs=row,
        out_shape=jax.ShapeDtypeStruct((n, w), BF16),
        compiler_params=_params("parallel"),
        name="rwkv_post",
    )(y, bonus, g, ln_g.reshape(1, w), ln_b.reshape(1, w), _head_ones(w, RWKV_HEAD_DIM))


def rwkv7_mixer(h, bsz, seq, mu, w0, w_up, a0, a_up, g_up, k_k, k_a, r_k, ln_g, ln_b):
    r, k, v, kk, b, ld, g, bonus = rwkv_pre(h, bsz, seq, mu, w0, w_up, a0, a_up, g_up, k_k, k_a, r_k)
    y = rwkv_chunk(r, k, v, kk, b, ld, bsz, seq)
    return rwkv_post(y, bonus, g, ln_g, ln_b)


AB_IN = S5_WIDTH + NSA_WIDTH + 6 * NSA_KV_W + NSA_GATE_COLS
AB_IN_PADDED = -(-AB_IN // LANES) * LANES
NSA_GATE_COL_BLOCK = (AB_IN - NSA_GATE_COLS) // LANES
PROJ_TM = 512


def kernel(x, ab_w_in, ab_w_out, s5_lam_re, s5_lam_im, s5_log_dt, s5_b_re, s5_b_im, s5_c_re, s5_c_im, s5_d, s5_w_glu, s5_b_glu, nsa_pe_k, nsa_pe_v, nsa_ck_w1, nsa_ck_b1, nsa_ck_w2, nsa_cv_w1, nsa_cv_b1, nsa_cv_w2, cd_w_in, cd_w_out, rwkv_mu, rwkv_w0, rwkv_w_up, rwkv_a0, rwkv_a_up, rwkv_g_up, rwkv_k_k, rwkv_k_a, rwkv_r_k, rwkv_ln_g, rwkv_ln_b, ret_ln_g, ret_ln_b, ln1_g, ln1_b, ln2_g, ln2_b, moe_router, moe_bias, moe_w1, moe_w3, moe_w2, sh_w1, sh_w3, sh_w2):
    bsz, seq, d = x.shape
    assert (AB_IN - NSA_GATE_COLS) % LANES == 0
    xf = x.reshape(bsz * seq, d)
    x_in = xf
    for layer in range(DEPTH):
        i = layer // 2
        if layer % 2 == 0:
            w_in = jnp.pad(ab_w_in[i], ((0, 0), (0, AB_IN_PADDED - AB_IN))).astype(BF16)
            h, u3 = project(x_in, w_in, PROJ_TM, chunked=(S5_CHUNK, S5_WIDTH))
            y_1 = s5_mixer(h, u3, bsz, seq, s5_lam_re[i], s5_lam_im[i], s5_log_dt[i], s5_b_re[i], s5_b_im[i],
                           s5_c_re[i], s5_c_im[i], s5_d[i], s5_w_glu[i], s5_b_glu[i])
            y_2 = nsa_mixer(h, bsz, seq, NSA_GATE_COL_BLOCK, nsa_pe_k[i], nsa_pe_v[i], nsa_ck_w1[i], nsa_ck_b1[i],
                            nsa_ck_w2[i], nsa_cv_w1[i], nsa_cv_b1[i], nsa_cv_w2[i])
            w_out = ab_w_out[i]
        else:
            h = project(x_in, cd_w_in[i].astype(BF16), PROJ_TM)
            y_1 = rwkv7_mixer(h, bsz, seq, rwkv_mu[i], rwkv_w0[i], rwkv_w_up[i], rwkv_a0[i], rwkv_a_up[i],
                              rwkv_g_up[i], rwkv_k_k[i], rwkv_k_a[i], rwkv_r_k[i], rwkv_ln_g[i], rwkv_ln_b[i])
            y_2 = retention_mixer(h, bsz, seq, RWKV_COLS, ret_ln_g[i], ret_ln_b[i])
            w_out = cd_w_out[i]
        xf = out_proj_ln(y_1, y_2, w_out, xf, ln1_g[layer], ln1_b[layer])
        xf, x_in = moe_block(xf, moe_router[layer], moe_bias[layer], moe_w1[layer], moe_w3[layer],
                             moe_w2, layer, sh_w1[layer], sh_w3[layer], sh_w2[layer], ln2_g[layer], ln2_b[layer])
    return xf.reshape(bsz, seq, d)
```

```python
import functools
import math

import jax
import jax.numpy as jnp
import numpy as np
from jax import lax
from jax.experimental import pallas as pl
from jax.experimental.pallas import tpu as pltpu

F32 = jnp.float32
BF16 = jnp.bfloat16
HIGHEST = lax.Precision.HIGHEST
FP8 = jnp.float8_e4m3fn
FP8_MAX = 448.0
FP8_TINY = 1e-30

VMEM_LIMIT_BYTES = 52 * 1024 * 1024
LANES = 128

LN_EPS = 1e-5
DEPTH = 2
ALPHA = (2 * DEPTH) ** 0.25

S5_GROUPS, S5_GROUP_CH, S5_STATE = 32, 16, 64
S5_WIDTH = S5_GROUPS * S5_GROUP_CH
S5_CHUNK = 16
S5_PACK = 8
NSA_HEADS, NSA_KV_GROUPS, NSA_HEAD_DIM = 8, 2, 64
NSA_HPG = NSA_HEADS // NSA_KV_GROUPS
NSA_WIDTH = NSA_HEADS * NSA_HEAD_DIM
NSA_ROT_DIM = NSA_HEAD_DIM // 4
ROPE_THETA = 500000.0
CMP_BLOCK, CMP_STRIDE, CMP_HIDDEN = 32, 16, 128
SLC_BLOCK, N_SLC, WINDOW = 64, 16, 512
Q_BLOCK = 256
FORCE_SCORE = 1e6
MASK_VALUE = -1e30
RWKV_HEADS, RWKV_HEAD_DIM = 8, 64
RWKV_WIDTH = RWKV_HEADS * RWKV_HEAD_DIM
RWKV_LORA_W, RWKV_LORA_A, RWKV_LORA_G = 64, 64, 128
RWKV_COLS = 3 * RWKV_WIDTH + RWKV_LORA_W + RWKV_LORA_A + RWKV_LORA_G
RWKV_GN_EPS = 64e-5
RWKV_CHUNK = 64
RWKV_PACK = 4
RET_HEADS, RET_DK, RET_DV, RET_CHUNK = 4, 64, 128, 128
RET_THETA = 10000.0
RET_GN_EPS = 1e-5
N_EXPERTS, TOP_K, EXPERT_FF = 64, 8, 256
N_EXPERT_GROUPS, TOPK_GROUPS = 8, 4
EXPERTS_PER_GROUP = N_EXPERTS // N_EXPERT_GROUPS
ROUTED_SCALE = 2.5
MOE_EXPERTS_PER_STEP = 4


def _params(*sem):
    return pltpu.CompilerParams(dimension_semantics=sem, vmem_limit_bytes=VMEM_LIMIT_BYTES)


def _dot(a, b, **kw):
    return jnp.dot(a, b, preferred_element_type=F32, **kw)


def _dot_nt(a, b, **kw):
    return lax.dot_general(a, b, (((1,), (1,)), ((), ())), preferred_element_type=F32, **kw)


def _dot_tn(a, b, **kw):
    return lax.dot_general(a, b, (((0,), (0,)), ((), ())), preferred_element_type=F32, **kw)


def _dot_split(a, b, split, parts):
    rest = a if split == 'a' else b
    acc = None
    for _ in range(parts):
        piece = rest.astype(BF16)
        term = _dot(piece, b) if split == 'a' else _dot(a, piece)
        acc = term if acc is None else acc + term
        rest = rest - piece.astype(F32)
    return acc


def _run_interleaved(gens):
    results = [None] * len(gens)
    live = list(range(len(gens)))
    while live:
        for i in list(live):
            try:
                next(gens[i])
            except StopIteration as done:
                results[i] = done.value
                live.remove(i)
    return results


def _gelu(x):
    return 0.5 * x * (1.0 + jnp.tanh(math.sqrt(2.0 / math.pi) * (x + 0.044715 * (x * x * x))))


def _sigmoid(x):
    return 1.0 / (1.0 + jnp.exp(-x))


def _layer_norm_rows(z, g, b):
    mu = jnp.mean(z, axis=-1, keepdims=True)
    zc = z - mu
    var = jnp.mean(zc * zc, axis=-1, keepdims=True)
    return zc * lax.rsqrt(var + LN_EPS) * g + b


def _proj_kernel(x_ref, w_ref, o_ref, *chunked_ref):
    y = _dot(x_ref[...].astype(BF16), w_ref[...])
    o_ref[...] = y
    for c_ref in chunked_ref:
        rows, t, w = c_ref.shape
        c_ref[...] = y[:, :w].reshape(rows, t, w)


def project(x, w_bf16, tm, chunked=None):
    m, k = x.shape
    n = w_bf16.shape[1]
    out_specs = [pl.BlockSpec((tm, n), lambda i: (i, 0))]
    out_shape = [jax.ShapeDtypeStruct((m, n), F32)]
    if chunked is not None:
        t, w = chunked
        out_specs.append(pl.BlockSpec((tm // t, t, w), lambda i: (i, 0, 0)))
        out_shape.append(jax.ShapeDtypeStruct((m // t, t, w), F32))
    out = pl.pallas_call(
        _proj_kernel,
        grid=(m // tm,),
        in_specs=[pl.BlockSpec((tm, k), lambda i: (i, 0)), pl.BlockSpec((k, n), lambda i: (0, 0))],
        out_specs=out_specs,
        out_shape=out_shape,
        compiler_params=_params("parallel"),
        name="project",
    )(x, w_bf16)
    return out if chunked is not None else out[0]


def _out_proj_ln_kernel(ya_ref, yb_ref, wa_ref, wb_ref, x_ref, g_ref, b_ref, o_ref):
    mix = _dot(ya_ref[...], wa_ref[...]) + _dot(yb_ref[...], wb_ref[...])
    o_ref[...] = _layer_norm_rows(ALPHA * x_ref[...] + mix, g_ref[...], b_ref[...])


def out_proj_ln(ya, yb, w_out, x, g, b, tm=512):
    n, d = x.shape
    ka, kb = ya.shape[1], yb.shape[1]
    wa = w_out[:ka].astype(BF16)
    wb = w_out[ka:].astype(BF16)
    row = lambda i: (i, 0)
    fixed = lambda i: (0, 0)
    return pl.pallas_call(
        _out_proj_ln_kernel,
        grid=(n // tm,),
        in_specs=[pl.BlockSpec((tm, ka), row), pl.BlockSpec((tm, kb), row),
                  pl.BlockSpec((ka, d), fixed), pl.BlockSpec((kb, d), fixed),
                  pl.BlockSpec((tm, d), row), pl.BlockSpec((1, d), fixed), pl.BlockSpec((1, d), fixed)],
        out_specs=pl.BlockSpec((tm, d), row),
        out_shape=jax.ShapeDtypeStruct((n, d), F32),
        compiler_params=_params("parallel"),
        name="out_proj_ln",
    )(ya, yb, wa, wb, x, g.reshape(1, d), b.reshape(1, d))


def _router_kernel(x_ref, rt_ref, bias_ref, o_ref):
    tr = x_ref.shape[0]
    scores = _sigmoid(_dot_nt(rt_ref[...], x_ref[...], precision=HIGHEST))
    biased = scores + bias_ref[...]
    grp = biased.reshape(N_EXPERT_GROUPS, EXPERTS_PER_GROUP, tr)
    pos = lax.broadcasted_iota(jnp.int32, grp.shape, 1)
    m1 = jnp.max(grp, axis=1, keepdims=True)
    first = jnp.min(jnp.where(grp == m1, pos, EXPERTS_PER_GROUP), axis=1, keepdims=True)
    m2 = jnp.max(jnp.where(pos == first, -jnp.inf, grp), axis=1, keepdims=True)
    gscore = (m1 + m2).reshape(N_EXPERT_GROUPS, tr)
    gidx = lax.broadcasted_iota(jnp.int32, gscore.shape, 0)
    grank = jnp.zeros(gscore.shape, F32)
    for j in range(N_EXPERT_GROUPS):
        row = gscore[j:j + 1, :]
        grank = grank + jnp.where(gidx > j, jnp.where(row >= gscore, 1.0, 0.0), jnp.where(row > gscore, 1.0, 0.0))
    gkeep = jnp.where(grank < TOPK_GROUPS, 1.0, 0.0)
    keep = jnp.broadcast_to(gkeep[:, None, :], grp.shape).reshape(N_EXPERTS, tr)
    masked = jnp.where(keep > 0.5, biased, -jnp.inf)
    eidx = lax.broadcasted_iota(jnp.int32, masked.shape, 0)
    rank = jnp.zeros(masked.shape, F32)
    for j in range(N_EXPERTS):
        row = masked[j:j + 1, :]
        rank = rank + jnp.where(eidx > j, jnp.where(row >= masked, 1.0, 0.0), jnp.where(row > masked, 1.0, 0.0))
    gate = jnp.where(rank < TOP_K, scores, 0.0)
    gate = gate / jnp.sum(gate, axis=0, keepdims=True) * ROUTED_SCALE
    o_ref[...] = jnp.concatenate([gate, jnp.zeros((LANES - N_EXPERTS, tr), F32)], axis=0).T


def moe_router(x, router, bias, tr=512):
    n, d = x.shape
    return pl.pallas_call(
        _router_kernel,
        grid=(n // tr,),
        in_specs=[pl.BlockSpec((tr, d), lambda i: (i, 0)),
                  pl.BlockSpec((N_EXPERTS, d), lambda i: (0, 0)),
                  pl.BlockSpec((N_EXPERTS, 1), lambda i: (0, 0))],
        out_specs=pl.BlockSpec((tr, LANES), lambda i: (i, 0)),
        out_shape=jax.ShapeDtypeStruct((n, LANES), F32),
        compiler_params=_params("parallel"),
        name="moe_router",
    )(x, router.T, bias.reshape(N_EXPERTS, 1))


def _quantize_fp8(a, axes):
    amax = jnp.max(jnp.abs(a), axis=axes, keepdims=True)
    scale = jnp.maximum(amax, FP8_TINY) * (1.0 / FP8_MAX)
    return (a * (1.0 / scale)).astype(FP8), scale


def _swiglu_hidden(xq, x_scale, w1q, w3q, w_scale, gate=None):
    col1 = x_scale * w_scale[0:1, 0:1]
    col3 = x_scale * w_scale[1:2, 0:1]
    if gate is not None:
        col3 = col3 * gate
    h1 = _dot(xq, w1q) * col1
    return h1 * _sigmoid(h1) * (_dot(xq, w3q) * col3)


def _experts_ln_kernel(x_ref, gate_ref, w1_ref, w3_ref, ws_ref, w2_ref, sw1_ref, sw3_ref, sws_ref, sw2_ref,
                       g_ref, b_ref, o_ref, obf_ref, acc_ref, xq_ref, xs_ref):
    step = pl.program_id(1)
    per_step = w1_ref.shape[0]

    @pl.when(step == 0)
    def _():
        xq, xs = _quantize_fp8(x_ref[...], (1,))
        xq_ref[...] = xq
        xs_ref[...] = xs
        h = _swiglu_hidden(xq, xs, sw1_ref[...], sw3_ref[...], sws_ref[...])
        acc_ref[...] = _dot(h.astype(BF16), sw2_ref[...].astype(BF16))

    lane = lax.broadcasted_iota(jnp.int32, gate_ref.shape, 1)
    gates = gate_ref[...]
    xq, xs = xq_ref[...], xs_ref[...]
    hidden = []
    for j in range(per_step):
        gcol = jnp.sum(jnp.where(lane == step * per_step + j, gates, 0.0), axis=1, keepdims=True)
        hidden.append(_swiglu_hidden(xq, xs, w1_ref[j], w3_ref[j], ws_ref[j], gcol).astype(BF16))
    w2 = w2_ref[0].astype(BF16)
    acc_ref[...] += _dot(jnp.concatenate(hidden, axis=1), w2.reshape(per_step * w2.shape[1], w2.shape[2]))

    @pl.when(step == pl.num_programs(1) - 1)
    def _():
        y = _layer_norm_rows(ALPHA * x_ref[...] + acc_ref[...], g_ref[...], b_ref[...])
        o_ref[...] = y
        obf_ref[...] = y.astype(BF16)


def _quantize_expert_weights(w1, w3):
    w1q, s1 = _quantize_fp8(w1, (-2, -1))
    w3q, s3 = _quantize_fp8(w3, (-2, -1))
    scales = jnp.broadcast_to(jnp.concatenate([s1, s3], axis=-2), s1.shape[:-2] + (2, w1.shape[-1]))
    return w1q, w3q, scales


def moe_experts_ln(x, gates, w1, w3, w2_layers, layer, sw1, sw3, sw2, g, b, tm=1024):
    n, d = x.shape
    ne = w1.shape[0]
    w1q, w3q, ws = _quantize_expert_weights(w1, w3)
    sw1q, sw3q, sws = _quantize_expert_weights(sw1, sw3)
    tok = lambda i, e: (i, 0)
    fixed = lambda i, e: (0, 0)
    per_expert = lambda *blk: pl.BlockSpec((MOE_EXPERTS_PER_STEP,) + blk, lambda i, e: (e, 0, 0))
    return pl.pallas_call(
        _experts_ln_kernel,
        grid=(n // tm, ne // MOE_EXPERTS_PER_STEP),
        in_specs=[pl.BlockSpec((tm, d), tok), pl.BlockSpec((tm, LANES), tok),
                  per_expert(d, EXPERT_FF), per_expert(d, EXPERT_FF), per_expert(2, EXPERT_FF),
                  pl.BlockSpec((1, MOE_EXPERTS_PER_STEP, EXPERT_FF, d), lambda i, e: (layer, e, 0, 0)),
                  pl.BlockSpec((d, EXPERT_FF), fixed), pl.BlockSpec((d, EXPERT_FF), fixed),
                  pl.BlockSpec((2, EXPERT_FF), fixed), pl.BlockSpec((EXPERT_FF, d), fixed),
                  pl.BlockSpec((1, d), fixed), pl.BlockSpec((1, d), fixed)],
        out_specs=[pl.BlockSpec((tm, d), tok), pl.BlockSpec((tm, d), tok)],
        out_shape=[jax.ShapeDtypeStruct((n, d), F32), jax.ShapeDtypeStruct((n, d), BF16)],
        scratch_shapes=[pltpu.VMEM((tm, d), F32), pltpu.VMEM((tm, d), FP8), pltpu.VMEM((tm, 1), F32)],
        compiler_params=_params("parallel", "arbitrary"),
        name="moe_experts_ln",
    )(x, gates, w1q, w3q, ws, w2_layers, sw1q, sw3q, sws, sw2, g.reshape(1, d), b.reshape(1, d))


def moe_block(x, router, bias, w1, w3, w2_layers, layer, sw1, sw3, sw2, g, b):
    gates = moe_router(x, router, bias)
    return moe_experts_ln(x, gates, w1, w3, w2_layers, layer, sw1, sw3, sw2, g, b)


def _s5_tables(lam_re, lam_im, log_dt, b_re, b_im, c_re, c_im, n_chunk):
    t, h, p = S5_CHUNK, S5_GROUP_CH, S5_STATE
    dt = jnp.exp(log_dt.astype(F32))[:, None]
    den = lam_re ** 2 + lam_im ** 2

    def lam_pow(k):
        k = jnp.asarray(k, F32)[..., None, None]
        mag = jnp.exp(lam_re * dt * k)
        return mag * jnp.cos(lam_im * dt * k), mag * jnp.sin(lam_im * dt * k)

    lb_re, lb_im = lam_pow(1.0)
    f_re = ((lb_re - 1.0) * lam_re + lb_im * lam_im) / den
    f_im = (lb_im * lam_re - (lb_re - 1.0) * lam_im) / den
    bb_re = f_re[..., None] * b_re - f_im[..., None] * b_im
    bb_im = f_re[..., None] * b_im + f_im[..., None] * b_re
    pr, pi = lam_pow(jnp.arange(t))
    cl_re = c_re[None] * pr[:, :, None, :] - c_im[None] * pi[:, :, None, :]
    cl_im = c_re[None] * pi[:, :, None, :] + c_im[None] * pr[:, :, None, :]
    klag = jnp.einsum('tgop,gpi->tgoi', cl_re, bb_re) - jnp.einsum('tgop,gpi->tgoi', cl_im, bb_im)
    nb = S5_GROUPS // S5_PACK
    split = lambda a, axis: a.reshape(a.shape[:axis] + (nb, S5_PACK) + a.shape[axis + 1:])
    eye = jnp.eye(S5_PACK, dtype=F32)
    lag_t = jnp.transpose(split(klag, 1), (1, 0, 2, 4, 3))
    lag_t = (lag_t[:, :, :, :, None, :] * eye[None, None, :, None, :, None]).reshape(nb, t, LANES, LANES)
    qr, qi = lam_pow(t - 1 - jnp.arange(t))
    st_re = qr[..., None] * bb_re[None] - qi[..., None] * bb_im[None]
    st_im = qr[..., None] * bb_im[None] + qi[..., None] * bb_re[None]
    st = jnp.stack([st_re, st_im], axis=0)
    st_t = jnp.transpose(split(st, 2), (2, 1, 3, 5, 0, 4)).reshape(nb, t, LANES, 2 * p)
    er, ei = lam_pow(jnp.arange(t) + 1)
    x_re = c_re[None] * er[:, :, None, :] - c_im[None] * ei[:, :, None, :]
    x_im = c_re[None] * ei[:, :, None, :] + c_im[None] * er[:, :, None, :]
    cr = jnp.stack([x_re, -x_im], axis=0)
    cr_t = jnp.transpose(split(cr, 2), (2, 1, 0, 5, 3, 4)).reshape(nb, t, 2 * p, LANES)
    levels = max(1, int(math.log2(n_chunk)))
    sr, si = lam_pow(t * (2.0 ** jnp.arange(levels)))
    sr = sr.reshape(levels, nb, S5_PACK * p)
    si = si.reshape(levels, nb, S5_PACK * p)
    a1 = jnp.concatenate([sr, sr], axis=-1)
    a2 = jnp.concatenate([-si, si], axis=-1)
    scan = jnp.transpose(jnp.stack([a1, a2], axis=1), (2, 0, 1, 3))
    return lag_t, st_t, cr_t, scan.astype(F32)


def _s5_build_tables(lag_ref, st_ref, cr_ref, wtoe_ref, wstate_ref, wcross_ref):
    t = lag_ref.shape[1]
    p = S5_STATE
    kp = S5_PACK * p
    wtoe_ref[...] = jnp.zeros_like(wtoe_ref)
    for d in range(t):
        tile = lag_ref[0, d].astype(BF16)
        for j in range(t - d):
            wtoe_ref[j * LANES:(j + 1) * LANES, (j + d) * LANES:(j + d + 1) * LANES] = tile
    lane = lax.broadcasted_iota(jnp.int32, (LANES, LANES), 1)
    row_g = lax.broadcasted_iota(jnp.int32, (LANES, kp), 0) // S5_GROUP_CH
    same_s = row_g == lax.broadcasted_iota(jnp.int32, (LANES, kp), 1) // p
    for j in range(t):
        a = st_ref[0, j]
        swapped = pltpu.roll(a, p, 1)
        for c, both in enumerate((jnp.where(lane < p, a, swapped), jnp.where(lane < p, swapped, a))):
            wide = jnp.concatenate([both] * (kp // LANES), axis=1)
            wstate_ref[j * LANES:(j + 1) * LANES, c * kp:(c + 1) * kp] = jnp.where(same_s, wide, 0.0).astype(BF16)
    same_c = (lax.broadcasted_iota(jnp.int32, (kp, LANES), 0) // p
              == lax.broadcasted_iota(jnp.int32, (kp, LANES), 1) // S5_GROUP_CH)
    for i in range(t):
        a = cr_ref[0, i]
        for c in range(2):
            tall = jnp.concatenate([a[c * p:(c + 1) * p]] * S5_PACK, axis=0)
            wcross_ref[c * kp:(c + 1) * kp, i * LANES:(i + 1) * LANES] = jnp.where(same_c, tall, 0.0).astype(BF16)


def _s5_kernel(u_ref, lag_ref, st_ref, cr_ref, scan_ref, o_ref, wtoe_ref, wstate_ref, wcross_ref):
    @pl.when(pl.program_id(1) == 0)
    def _():
        _s5_build_tables(lag_ref, st_ref, cr_ref, wtoe_ref, wstate_ref, wcross_ref)

    n_chunk, t, _ = u_ref.shape
    x = jnp.concatenate([u_ref[:, j, :] for j in range(t)], axis=1).astype(BF16)
    local = _dot(x, wtoe_ref[...])
    state = _dot(x, wstate_ref[...])
    row = lax.broadcasted_iota(jnp.int32, state.shape, 0)
    s = jnp.where(row >= 1, pltpu.roll(state, 1, 0), 0.0)
    half = state.shape[1] // 2
    level = 0
    d = 1
    while d < n_chunk:
        mult = scan_ref[0, level]
        prev = jnp.where(row >= d, pltpu.roll(s, d, 0), 0.0)
        s = s + mult[0:1, :] * prev + mult[1:2, :] * pltpu.roll(prev, half, 1)
        d *= 2
        level += 1
    y = local + _dot(s.astype(BF16), wcross_ref[...])
    for i in range(t):
        o_ref[:, i, :] = y[:, i * LANES:(i + 1) * LANES]


def s5_scan(u3, bsz, lag_t, st_t, cr_t, scan):
    rows, t, w = u3.shape
    n_chunk = rows // bsz
    kp2 = 2 * S5_PACK * S5_STATE
    table = lambda a: pl.BlockSpec((1,) + a.shape[1:], lambda j, b: (j, 0, 0, 0))
    return pl.pallas_call(
        _s5_kernel,
        grid=(w // LANES, bsz),
        in_specs=[pl.BlockSpec((n_chunk, t, LANES), lambda j, b: (b, 0, j)),
                  table(lag_t), table(st_t), table(cr_t), table(scan)],
        out_specs=pl.BlockSpec((n_chunk, t, LANES), lambda j, b: (b, 0, j)),
        out_shape=jax.ShapeDtypeStruct(u3.shape, F32),
        scratch_shapes=[pltpu.VMEM((t * LANES, t * LANES), BF16), pltpu.VMEM((t * LANES, kp2), BF16),
                        pltpu.VMEM((kp2, t * LANES), BF16)],
        compiler_params=_params("arbitrary", "arbitrary"),
        name="s5_scan",
    )(u3, lag_t, st_t, cr_t, scan)


def _s5_post_kernel(y_ref, u_ref, d_ref, w_ref, b_ref, o_ref):
    u = u_ref[...]
    y = _gelu(y_ref[...].reshape(u.shape) + d_ref[...] * u)
    o_ref[...] = (y * _sigmoid(_dot(y.astype(BF16), w_ref[...]) + b_ref[...])).astype(o_ref.dtype)


def s5_post(y3, h, d_skip, w_glu, b_glu, tm=1024):
    rows, t, w = y3.shape
    n = rows * t
    tm = min(tm, n)
    row = lambda i: (i, 0)
    fixed = lambda i: (0, 0)
    return pl.pallas_call(
        _s5_post_kernel,
        grid=(n // tm,),
        in_specs=[pl.BlockSpec((tm // t, t, w), lambda i: (i, 0, 0)), pl.BlockSpec((tm, w), row),
                  pl.BlockSpec((1, w), fixed), pl.BlockSpec((w, w), fixed), pl.BlockSpec((1, w), fixed)],
        out_specs=pl.BlockSpec((tm, w), row),
        out_shape=jax.ShapeDtypeStruct((n, w), BF16),
        compiler_params=_params("parallel"),
        name="s5_post",
    )(y3, h, d_skip.reshape(1, w), w_glu.astype(BF16), b_glu.reshape(1, w))


def s5_mixer(h, u3, bsz, seq, lam_re, lam_im, log_dt, b_re, b_im, c_re, c_im, d_skip, w_glu, b_glu):
    tables = _s5_tables(lam_re, lam_im, log_dt, b_re, b_im, c_re, c_im, seq // S5_CHUNK)
    return s5_post(s5_scan(u3, bsz, *tables), h, d_skip, w_glu, b_glu)


def _rope_tables(pos, rot_dim, theta, head_dim, n_heads):
    half = rot_dim // 2
    f32 = np.float32
    inv_freq = f32(theta) ** (-np.arange(half, dtype=f32) / f32(half))
    ang = (pos.astype(f32)[:, None] * inv_freq[None, :]).astype(np.float64)
    cos, sin = np.cos(ang), np.sin(ang)
    rest = head_dim - rot_dim
    n = pos.shape[0]
    c = np.concatenate([cos, cos, np.ones((n, rest))], axis=1)
    s_up = np.concatenate([-sin, np.zeros((n, half + rest))], axis=1)
    s_dn = np.concatenate([np.zeros((n, half)), sin, np.zeros((n, rest))], axis=1)
    tile = lambda a: jnp.asarray(np.tile(a, (1, n_heads)), F32)
    return tile(c), tile(s_up), tile(s_dn)


def _rope_apply(x, c, s_up, s_dn, half):
    return x * c + pltpu.roll(x, LANES - half, 1) * s_up + pltpu.roll(x, half, 1) * s_dn


def _retention_tables():
    c = RET_CHUNK
    log_gamma = np.log(1.0 - 2.0 ** (-5.0 - np.arange(RET_HEADS, dtype=np.float64)))
    i = np.arange(c, dtype=np.float64)
    diff = i[:, None] - i[None, :]
    decay = np.where(diff >= 0, np.exp(diff[None] * log_gamma[:, None, None]), 0.0)
    qdec = np.repeat(np.exp((i + 1.0)[:, None] * log_gamma[None, :]), RET_DK, axis=1)
    kdec = np.repeat(np.exp((c - 1.0 - i)[:, None] * log_gamma[None, :]), RET_DK, axis=1)
    chunk_decay = [float(v) for v in np.exp(c * log_gamma)]
    return jnp.asarray(decay, F32), jnp.asarray(qdec, F32), jnp.asarray(kdec, F32), chunk_decay


def _retention_kernel(chunk_decay, q_ref, k_ref, v0_ref, v1_ref, g0_ref, g1_ref, c_ref, su_ref, sd_ref,
                      dec_ref, qdec_ref, kdec_ref, lng_ref, lnb_ref, o_ref, state_ref):
    @pl.when(pl.program_id(1) == 0)
    def _():
        state_ref[...] = jnp.zeros_like(state_ref)

    half = RET_DK // 2
    tabs = (c_ref[...], su_ref[...], sd_ref[...])
    q = jnp.concatenate([_rope_apply(q_ref[:, s:s + LANES], *tabs, half) for s in (0, LANES)], axis=1)
    k = jnp.concatenate([_rope_apply(k_ref[:, s:s + LANES], *tabs, half) for s in (0, LANES)], axis=1)
    k = k * (RET_DK ** -0.5)
    q_dec = q * qdec_ref[...]
    k_dec = k * kdec_ref[...]
    v = jnp.concatenate([v0_ref[...], v1_ref[...]], axis=1)
    gate = jnp.concatenate([g0_ref[...], g1_ref[...]], axis=1)
    states = [state_ref[h] for h in range(RET_HEADS)]

    def head(h):
        ks = slice(h * RET_DK, (h + 1) * RET_DK)
        vh = v[:, h * RET_DV:(h + 1) * RET_DV].astype(BF16)
        scores = _dot_nt(q[:, ks].astype(BF16), k[:, ks].astype(BF16)) * dec_ref[h]
        yield
        y = _dot(scores.astype(BF16), vh) + _dot(q_dec[:, ks].astype(BF16), states[h].astype(BF16))
        yield
        new_state = states[h] * chunk_decay[h] + _dot_tn(k_dec[:, ks].astype(BF16), vh)
        yield
        mu = jnp.mean(y, axis=-1, keepdims=True)
        yc = y - mu
        var = jnp.mean(yc * yc, axis=-1, keepdims=True)
        return yc * lax.rsqrt(var + RET_GN_EPS), new_state

    results = _run_interleaved([head(h) for h in range(RET_HEADS)])
    outs = [r[0] for r in results]
    for h, (_, new_state) in enumerate(results):
        state_ref[h] = new_state
    yn = jnp.concatenate(outs, axis=1) * lng_ref[...] + lnb_ref[...]
    o_ref[...] = (gate * _sigmoid(gate) * yn).astype(o_ref.dtype)


def retention_mixer(h, bsz, seq, col0, ln_g, ln_b):
    c = RET_CHUNK
    n_chunk = seq // c
    qk_w = RET_HEADS * RET_DK
    v_w = RET_HEADS * RET_DV
    assert col0 % qk_w == 0 and qk_w == 2 * LANES and v_w == 2 * qk_w
    cb = col0 // qk_w
    rc, rsu, rsd = _rope_tables(np.arange(seq), RET_DK, RET_THETA, RET_DK, 2)
    dec, qdec, kdec, chunk_decay = _retention_tables()
    row = lambda j: (lambda b, n: (b * n_chunk + n, j))
    pos = lambda b, n: (n, 0)
    fixed2 = lambda b, n: (0, 0)
    kern = functools.partial(_retention_kernel, chunk_decay)
    return pl.pallas_call(
        kern,
        grid=(bsz, n_chunk),
        in_specs=[pl.BlockSpec((c, qk_w), row(cb)), pl.BlockSpec((c, qk_w), row(cb + 1)),
                  pl.BlockSpec((c, qk_w), row(cb + 2)), pl.BlockSpec((c, qk_w), row(cb + 3)),
                  pl.BlockSpec((c, qk_w), row(cb + 4)), pl.BlockSpec((c, qk_w), row(cb + 5)),
                  pl.BlockSpec((c, LANES), pos), pl.BlockSpec((c, LANES), pos), pl.BlockSpec((c, LANES), pos),
                  pl.BlockSpec((RET_HEADS, c, c), lambda b, n: (0, 0, 0)),
                  pl.BlockSpec((c, qk_w), fixed2), pl.BlockSpec((c, qk_w), fixed2),
                  pl.BlockSpec((1, v_w), fixed2), pl.BlockSpec((1, v_w), fixed2)],
        out_specs=pl.BlockSpec((c, v_w), lambda b, n: (b * n_chunk + n, 0)),
        out_shape=jax.ShapeDtypeStruct((bsz * seq, v_w), BF16),
        scratch_shapes=[pltpu.VMEM((RET_HEADS, RET_DK, RET_DV), F32)],
        compiler_params=_params("parallel", "arbitrary"),
        name="retention",
    )(h, h, h, h, h, h, rc, rsu, rsd, dec, qdec, kdec, ln_g.reshape(1, v_w), ln_b.reshape(1, v_w))


NSA_KV_W = NSA_KV_GROUPS * NSA_HEAD_DIM
NSA_VT_ROWS = NSA_HEAD_DIM + 16
NSA_GATE_COLS = 3 * NSA_HEADS


def _nsa_prep_kernel(q_ref, kvc_ref, kvs_ref, kvw_ref, c_ref, su_ref, sd_ref,
                     qo_ref, kc_ref, vc_ref, ks_ref, vs_ref, kw_ref, vw_ref):
    half = NSA_ROT_DIM // 2
    tabs = (c_ref[...], su_ref[...], sd_ref[...])
    scale = NSA_HEAD_DIM ** -0.5 * math.log2(math.e)
    q = jnp.concatenate(
        [_rope_apply(q_ref[:, s:s + LANES], *tabs, half) * scale for s in range(0, NSA_WIDTH, LANES)], axis=1)
    qo_ref[0] = q.T.astype(qo_ref.dtype)

    def split(x, o_ref):
        for g in range(NSA_KV_GROUPS):
            o_ref[0, g] = x[:, g * NSA_HEAD_DIM:(g + 1) * NSA_HEAD_DIM].astype(o_ref.dtype)

    def split_t(x, o_ref):
        xt = x.T
        for g in range(NSA_KV_GROUPS):
            o_ref[0, g, :NSA_HEAD_DIM] = xt[g * NSA_HEAD_DIM:(g + 1) * NSA_HEAD_DIM, :].astype(o_ref.dtype)
            o_ref[0, g, NSA_HEAD_DIM:] = jnp.ones((NSA_VT_ROWS - NSA_HEAD_DIM, xt.shape[1]), o_ref.dtype)

    split(kvc_ref[:, :NSA_KV_W], kc_ref)
    split(kvc_ref[:, NSA_KV_W:], vc_ref)
    split(_rope_apply(kvs_ref[:, :NSA_KV_W], *tabs, half), ks_ref)
    split_t(kvs_ref[:, NSA_KV_W:], vs_ref)
    split(_rope_apply(kvw_ref[:, :NSA_KV_W], *tabs, half), kw_ref)
    split_t(kvw_ref[:, NSA_KV_W:], vw_ref)


def nsa_prep(h, bsz, seq, tl=512):
    tl = min(tl, seq)
    nl = seq // tl
    rc, rsu, rsd = _rope_tables(np.arange(seq), NSA_ROT_DIM, ROPE_THETA, NSA_HEAD_DIM, LANES // NSA_HEAD_DIM)
    row = lambda j: (lambda b, l: (b * nl + l, j))
    pos = lambda b, l: (l, 0)
    kv_out = pl.BlockSpec((1, NSA_KV_GROUPS, tl, NSA_HEAD_DIM), lambda b, l: (b, 0, l, 0))
    kv_shape = lambda dt: jax.ShapeDtypeStruct((bsz, NSA_KV_GROUPS, seq, NSA_HEAD_DIM), dt)
    vt_out = pl.BlockSpec((1, NSA_KV_GROUPS, NSA_VT_ROWS, tl), lambda b, l: (b, 0, 0, l))
    vt_shape = jax.ShapeDtypeStruct((bsz, NSA_KV_GROUPS, NSA_VT_ROWS, seq), BF16)
    two = 2 * NSA_KV_W
    return pl.pallas_call(
        _nsa_prep_kernel,
        grid=(bsz, nl),
        in_specs=[pl.BlockSpec((tl, NSA_WIDTH), row(1)),
                  pl.BlockSpec((tl, two), row(4)), pl.BlockSpec((tl, two), row(5)), pl.BlockSpec((tl, two), row(6)),
                  pl.BlockSpec((tl, LANES), pos), pl.BlockSpec((tl, LANES), pos), pl.BlockSpec((tl, LANES), pos)],
        out_specs=[pl.BlockSpec((1, NSA_WIDTH, tl), lambda b, l: (b, 0, l)),
                   kv_out, kv_out, kv_out, vt_out, kv_out, vt_out],
        out_shape=[jax.ShapeDtypeStruct((bsz, NSA_WIDTH, seq), BF16),
                   kv_shape(F32), kv_shape(F32), kv_shape(BF16), vt_shape, kv_shape(BF16), vt_shape],
        compiler_params=_params("parallel", "parallel"),
        name="nsa_prep",
    )(h, h, h, h, rc, rsu, rsd)


def _nsa_compress_kernel(hk_ref, hv_ref, pek_ref, pev_ref, kw1_ref, kb1_ref, kw2_ref, vw1_ref, vb1_ref, vw2_ref,
                         c_ref, su_ref, sd_ref, ko_ref, vo_ref):
    def mlp(h_ref, pe_ref, w1_ref, b1_ref, w2_ref):
        hb = h_ref[0, 0]
        rows = hb.shape[0]
        first = _dot((hb + pe_ref[0:1, :]).astype(BF16), w1_ref[0])
        second = _dot((hb + pe_ref[1:2, :]).astype(BF16), w1_ref[1])
        hid = _gelu(first + pltpu.roll(second, rows - 1, 0) + b1_ref[...])
        return _dot(hid.astype(BF16), w2_ref[...])

    kc = _rope_apply(mlp(hk_ref, pek_ref, kw1_ref, kb1_ref, kw2_ref), c_ref[...], su_ref[...], sd_ref[...],
                     NSA_ROT_DIM // 2)
    vc = mlp(hv_ref, pev_ref, vw1_ref, vb1_ref, vw2_ref)
    ko_ref[0, 0] = kc[:, :NSA_HEAD_DIM].astype(ko_ref.dtype)
    vo_ref[0, 0, :NSA_HEAD_DIM] = vc.T[:NSA_HEAD_DIM, :].astype(vo_ref.dtype)
    vo_ref[0, 0, NSA_HEAD_DIM:] = jnp.ones((NSA_VT_ROWS - NSA_HEAD_DIM, vc.shape[0]), vo_ref.dtype)


def nsa_compress(kc, vc, pe_k, pe_v, ck_w1, ck_b1, ck_w2, cv_w1, cv_b1, cv_w2):
    bsz, grp, seq, d = kc.shape
    n_rows = seq // CMP_STRIDE
    flat = CMP_STRIDE * d
    cmp_end = np.arange(n_rows) * CMP_STRIDE + CMP_BLOCK - 1
    rc, rsu, rsd = _rope_tables(cmp_end, NSA_ROT_DIM, ROPE_THETA, NSA_HEAD_DIM, LANES // NSA_HEAD_DIM)
    pad_w2 = lambda w: jnp.pad(w, ((0, 0), (0, LANES - d))).astype(BF16)
    blk = pl.BlockSpec((1, 1, n_rows, flat), lambda b, g: (b, g, 0, 0))
    f2 = lambda b, g: (0, 0)
    f3 = lambda b, g: (0, 0, 0)
    w_specs = [pl.BlockSpec((2, flat, CMP_HIDDEN), f3), pl.BlockSpec((1, CMP_HIDDEN), f2),
               pl.BlockSpec((CMP_HIDDEN, LANES), f2)]
    return pl.pallas_call(
        _nsa_compress_kernel,
        grid=(bsz, grp),
        in_specs=[blk, blk, pl.BlockSpec((2, flat), f2), pl.BlockSpec((2, flat), f2)] + w_specs + w_specs
                 + [pl.BlockSpec((n_rows, LANES), f2)] * 3,
        out_specs=[pl.BlockSpec((1, 1, n_rows, d), lambda b, g: (b, g, 0, 0)),
                   pl.BlockSpec((1, 1, NSA_VT_ROWS, n_rows), lambda b, g: (b, g, 0, 0))],
        out_shape=[jax.ShapeDtypeStruct((bsz, grp, n_rows, d), BF16),
                   jax.ShapeDtypeStruct((bsz, grp, NSA_VT_ROWS, n_rows), BF16)],
        compiler_params=_params("parallel", "parallel"),
        name="nsa_compress",
    )(kc.reshape(bsz, grp, n_rows, flat), vc.reshape(bsz, grp, n_rows, flat),
      pe_k.reshape(2, flat), pe_v.reshape(2, flat),
      ck_w1.reshape(2, flat, CMP_HIDDEN).astype(BF16), ck_b1.reshape(1, CMP_HIDDEN), pad_w2(ck_w2),
      cv_w1.reshape(2, flat, CMP_HIDDEN).astype(BF16), cv_b1.reshape(1, CMP_HIDDEN), pad_w2(cv_w2),
      rc, rsu, rsd)


def _per_head(x):
    return jnp.concatenate([x] * NSA_HPG, axis=1)


def _nsa_attn_kernel(seq, tk, qt_ref, gate_ref, kc_ref, vct_ref, ks_ref, vst_ref, kw_ref, vwt_ref, mmapt_ref,
                     o_ref, sel_ref):
    n_blk = seq // SLC_BLOCK
    n_sel = min(N_SLC, n_blk)
    hd = NSA_HEAD_DIM
    w = NSA_HPG * hd
    groups = range(NSA_KV_GROUPS)
    q0 = pl.program_id(1) * Q_BLOCK
    t_l = q0 + lax.broadcasted_iota(jnp.int32, (1, Q_BLOCK), 1)

    def select(g):
        qg = qt_ref[0, g * w:(g + 1) * w, :]
        qst = jnp.concatenate([qg[h * hd:(h + 1) * hd, :] for h in range(NSA_HPG)], axis=1)
        kc = kc_ref[0, g]
        n_cmp = kc.shape[0]
        cmp_end = lax.broadcasted_iota(jnp.int32, (n_cmp, 1), 0) * CMP_STRIDE + (CMP_BLOCK - 1)
        s = _dot(kc, qst) + _per_head(jnp.where(cmp_end <= t_l, 0.0, MASK_VALUE))
        yield
        p = jnp.exp2(s - jnp.max(s, axis=0, keepdims=True))
        any_key = _per_head(jnp.where(t_l >= CMP_BLOCK - 1, 1.0, 0.0))
        pv = _dot(vct_ref[0, g], p.astype(BF16))
        inv_l = any_key / pv[hd:hd + 1]
        o_cmp = pv[:hd] * inv_l
        yield
        p = p * inv_l
        imp = p[:, 0:Q_BLOCK]
        for h in range(1, NSA_HPG):
            imp = imp + p[:, h * Q_BLOCK:(h + 1) * Q_BLOCK]
        imp_slc = _dot_split(mmapt_ref[...], imp, 'b', 3)
        yield
        blk = lax.broadcasted_iota(jnp.int32, (n_blk, 1), 0)
        cur = t_l // SLC_BLOCK
        score = jnp.where(blk == 0, FORCE_SCORE,
                          jnp.where(blk == cur, FORCE_SCORE, jnp.where(blk == cur - 1, FORCE_SCORE, imp_slc)))
        score = jnp.where(blk * SLC_BLOCK <= t_l, score, -FORCE_SCORE)
        sel = jnp.zeros((n_blk, Q_BLOCK), F32)
        for _ in range(n_sel):
            best = jnp.max(score, axis=0, keepdims=True)
            idx = jnp.min(jnp.where(score == best, blk, n_blk), axis=0, keepdims=True)
            pick = blk == idx
            sel = jnp.where(pick, 1.0, sel)
            score = jnp.where(pick, -jnp.inf, score)
            yield
        sel_ref[g] = sel
        return qst, o_cmp

    selected = _run_interleaved([select(g) for g in groups])
    qst = [r[0] for r in selected]
    o_cmp = [r[1] for r in selected]

    blocks_per_tile = tk // SLC_BLOCK

    def slc_tile(kt, carry):
        k0 = pl.multiple_of(kt * tk, tk)
        causal = k0 + lax.broadcasted_iota(jnp.int32, (tk, 1), 0) <= t_l
        out = []
        for g in groups:
            m, acc = carry[g]
            score = _dot(ks_ref[0, g, pl.ds(k0, tk), :], qst[g])
            chosen = jnp.concatenate(
                [jnp.broadcast_to(sel_ref[g, pl.ds(kt * blocks_per_tile + j, 1), :], (SLC_BLOCK, Q_BLOCK))
                 for j in range(blocks_per_tile)], axis=0)
            bias = jnp.where(causal, jnp.where(chosen > 0.5, 0.0, MASK_VALUE), MASK_VALUE)
            s = score + _per_head(bias)
            m_new = jnp.maximum(m, jnp.max(s, axis=0, keepdims=True))
            alpha = jnp.exp2(m - m_new)
            p = jnp.exp2(s - m_new)
            acc = alpha * acc + _dot(vst_ref[0, g, :, pl.ds(k0, tk)], p.astype(BF16))
            out.append((m_new, acc))
        return tuple(out)

    n_tiles = (q0 + Q_BLOCK + tk - 1) // tk
    cols = NSA_HPG * Q_BLOCK
    init = tuple((jnp.full((1, cols), MASK_VALUE, F32), jnp.zeros((NSA_VT_ROWS, cols), F32)) for _ in groups)
    slc = lax.fori_loop(0, n_tiles, slc_tile, init)

    band = WINDOW + Q_BLOCK
    w0 = pl.multiple_of(jnp.maximum(q0 - WINDOW, 0), Q_BLOCK)
    kpos = w0 + lax.broadcasted_iota(jnp.int32, (band, 1), 0)
    win_bias = _per_head(jnp.where(kpos <= t_l, jnp.where(kpos > t_l - WINDOW, 0.0, MASK_VALUE), MASK_VALUE))
    sig_t = _sigmoid(gate_ref[...]).T

    def finish(g):
        s = _dot(kw_ref[0, g, pl.ds(w0, band), :], qst[g]) + win_bias
        yield
        p = jnp.exp2(s - jnp.max(s, axis=0, keepdims=True))
        pv = _dot(vwt_ref[0, g, :, pl.ds(w0, band)], p.astype(BF16))
        o_win = pv[:hd] / pv[hd:hd + 1]
        yield
        acc = slc[g][1]
        acc_slc, l_slc = acc[:hd], acc[hd:hd + 1]

        def gate(branch):
            first = (g * NSA_HPG) * 3 + branch
            return jnp.concatenate([sig_t[first + 3 * h:first + 3 * h + 1, :] for h in range(NSA_HPG)], axis=1)

        out_t = gate(0) * o_cmp[g] + gate(1) * (acc_slc / l_slc) + gate(2) * o_win
        pairs = []
        for h in range(0, NSA_HPG, 2):
            two = jnp.concatenate([out_t[:, h * Q_BLOCK:(h + 1) * Q_BLOCK],
                                   out_t[:, (h + 1) * Q_BLOCK:(h + 2) * Q_BLOCK]], axis=0)
            pairs.append(two.T)
        o_ref[:, g * w:(g + 1) * w] = jnp.concatenate(pairs, axis=1).astype(o_ref.dtype)

    _run_interleaved([finish(g) for g in groups])


def _nsa_pool_matrix(seq):
    n_blk = seq // SLC_BLOCK
    n_rows = seq // CMP_STRIDE
    per_stride = SLC_BLOCK // CMP_STRIDE
    span = CMP_BLOCK // CMP_STRIDE
    pool = np.zeros((n_blk, n_rows), np.float32)
    for j in range(n_blk):
        for m in range(per_stride):
            for n in range(span):
                c = per_stride * j + m + n - (span - 1)
                if 0 <= c < n_rows - 1:
                    pool[j, c] += 1.0
    return jnp.asarray(pool, BF16)


def nsa_attention(qt, h, gate_col_block, k_cmp, v_cmp_t, ks, vs_t, kw, vw_t, bsz, seq, tk=1024):
    tk = min(tk, seq)
    nq = seq // Q_BLOCK
    pool = _nsa_pool_matrix(seq)
    n_rows = k_cmp.shape[2]
    d = NSA_HEAD_DIM
    qblk = lambda b, i: (b * nq + i, 0)
    whole = lambda *shape: pl.BlockSpec((1, NSA_KV_GROUPS) + shape, lambda b, i: (b, 0, 0, 0))
    kern = functools.partial(_nsa_attn_kernel, seq, tk)
    return pl.pallas_call(
        kern,
        grid=(bsz, nq),
        in_specs=[pl.BlockSpec((1, NSA_WIDTH, Q_BLOCK), lambda b, i: (b, 0, i)),
                  pl.BlockSpec((Q_BLOCK, LANES), lambda b, i: (b * nq + i, gate_col_block)),
                  whole(n_rows, d), whole(NSA_VT_ROWS, n_rows), whole(seq, d), whole(NSA_VT_ROWS, seq),
                  whole(seq, d), whole(NSA_VT_ROWS, seq),
                  pl.BlockSpec(pool.shape, lambda b, i: (0, 0))],
        out_specs=pl.BlockSpec((Q_BLOCK, NSA_WIDTH), qblk),
        out_shape=jax.ShapeDtypeStruct((bsz * seq, NSA_WIDTH), BF16),
        scratch_shapes=[pltpu.VMEM((NSA_KV_GROUPS, seq // SLC_BLOCK, Q_BLOCK), F32)],
        compiler_params=_params("parallel", "arbitrary"),
        name="nsa_attention",
    )(qt, h, k_cmp, v_cmp_t, ks, vs_t, kw, vw_t, pool)


def nsa_mixer(h, bsz, seq, gate_col_block, pe_k, pe_v, ck_w1, ck_b1, ck_w2, cv_w1, cv_b1, cv_w2):
    qt, kc, vc, ks, vs_t, kw, vw_t = nsa_prep(h, bsz, seq)
    k_cmp, v_cmp_t = nsa_compress(kc, vc, pe_k, pe_v, ck_w1, ck_b1, ck_w2, cv_w1, cv_b1, cv_w2)
    return nsa_attention(qt, h, gate_col_block, k_cmp, v_cmp_t, ks, vs_t, kw, vw_t, bsz, seq)


def _head_ones(width, head_dim):
    idx = np.arange(width) // head_dim
    return jnp.asarray(idx[:, None] == idx[None, :], BF16)


def _softplus(x):
    return jnp.maximum(x, 0.0) + jnp.log(1.0 + jnp.exp(-jnp.abs(x)))


def _rwkv_pre_kernel(p_ref, prev_ref, mu_ref, w0_ref, wup_ref, a0_ref, aup_ref, gup_ref, kk_ref, ka_ref, rk_ref,
                     ones_ref, r_o, k_o, v_o, kk_o, b_o, ld_o, g_o, bonus_o):
    w = RWKV_WIDTH
    p = p_ref[...]
    first_row = jnp.where(pl.program_id(1) == 0, 0.0, prev_ref[7:8, :])
    is_row0 = lax.broadcasted_iota(jnp.int32, p.shape, 0) == 0
    prev = jnp.where(is_row0, first_row, pltpu.roll(p, 1, 0))
    ps = p + (prev - p) * mu_ref[...]
    r, k, v = ps[:, 0:w], ps[:, w:2 * w], ps[:, 2 * w:3 * w]
    o = 3 * w
    w_lo = ps[:, o:o + RWKV_LORA_W]
    a_lo = ps[:, o + RWKV_LORA_W:o + RWKV_LORA_W + RWKV_LORA_A]
    g_lo = ps[:, o + RWKV_LORA_W + RWKV_LORA_A:]
    wlog = -_softplus(-(w0_ref[...] + _dot(jnp.tanh(w_lo).astype(BF16), wup_ref[...]))) - 0.5
    a = _sigmoid(a0_ref[...] + _dot(a_lo.astype(BF16), aup_ref[...]))
    g = _dot(_sigmoid(g_lo).astype(BF16), gup_ref[...])
    kk = k * kk_ref[...]
    norm = jnp.sqrt(_dot_split(kk * kk, ones_ref[...], 'a', 2))
    kk = kk / jnp.maximum(norm, 1e-12)
    k2 = k * (1.0 + (a - 1.0) * ka_ref[...])
    r_o[...] = r
    k_o[...] = k2
    v_o[...] = v
    kk_o[...] = kk
    b_o[...] = kk * a
    ld_o[...] = -jnp.exp(wlog)
    g_o[...] = g
    bonus_o[...] = _dot_split(r * k2 * rk_ref[...], ones_ref[...], 'a', 2) * v


def rwkv_pre(h, bsz, seq, mu, w0, w_up, a0, a_up, g_up, k_k, k_a, r_k, tl=512):
    tl = min(tl, seq)
    nl = seq // tl
    w = RWKV_WIDTH
    cols = RWKV_COLS
    ones = _head_ones(w, RWKV_HEAD_DIM)
    f2 = lambda b, l: (0, 0)
    vec = pl.BlockSpec((1, w), f2)
    out_spec = pl.BlockSpec((tl, w), lambda b, l: (b * nl + l, 0))
    out_shape = jax.ShapeDtypeStruct((bsz * seq, w), F32)
    return pl.pallas_call(
        _rwkv_pre_kernel,
        grid=(bsz, nl),
        in_specs=[pl.BlockSpec((tl, cols), lambda b, l: (b * nl + l, 0)),
                  pl.BlockSpec((8, cols), lambda b, l: (jnp.maximum((b * seq + l * tl) // 8 - 1, 0), 0)),
                  pl.BlockSpec((1, cols), f2), vec, pl.BlockSpec((RWKV_LORA_W, w), f2),
                  vec, pl.BlockSpec((RWKV_LORA_A, w), f2), pl.BlockSpec((RWKV_LORA_G, w), f2),
                  vec, vec, vec, pl.BlockSpec((w, w), f2)],
        out_specs=[out_spec] * 8,
        out_shape=[out_shape] * 8,
        compiler_params=_params("parallel", "parallel"),
        name="rwkv_pre",
    )(h, h, mu.reshape(1, cols), w0.reshape(1, w), w_up.astype(BF16), a0.reshape(1, w), a_up.astype(BF16),
      g_up.astype(BF16), k_k.reshape(1, w), k_a.reshape(1, w), r_k.reshape(1, w), ones)


def _rwkv_masks():
    t, pk = RWKV_CHUNK, RWKV_PACK
    n = t * pk
    ri = np.arange(n)
    same = (ri[:, None] // t) == (ri[None, :] // t)
    tt, ss = ri[:, None] % t, ri[None, :] % t
    levels = []
    k = 1
    while k < t:
        levels.append(same & (tt // (2 * k) == ss // (2 * k)) & ((tt // k) % 2 == 1) & ((ss // k) % 2 == 0))
        k *= 2
    lvl = np.stack(levels).astype(np.float32)
    tri = (np.arange(t)[:, None] >= np.arange(t)[None, :]).astype(np.float32)
    head_lane = ((ri[:, None] // t) == (np.arange(pk * RWKV_HEAD_DIM)[None, :] // RWKV_HEAD_DIM)).astype(np.float32)
    return (jnp.asarray(tri, BF16), jnp.asarray(head_lane), jnp.asarray(same.astype(np.float32)), jnp.asarray(lvl))


def _rwkv_chain(r, k, v, kk, b, ld, st, tri, hl, bd, lvl_ref):
    t, pk = RWKV_CHUNK, RWKV_PACK
    n = t * pk
    c = _dot_split(tri, ld, 'b', 3)
    yield
    c_end = c[t - 1:t, :]
    e_neg = jnp.exp(-c)
    e_end = jnp.exp(c_end - c)
    kkd = (kk * jnp.exp(c - ld)).astype(BF16)
    rd = (r * jnp.exp(c)).astype(BF16)

    def big(x):
        return (jnp.concatenate([x] * pk, axis=0) * hl).astype(BF16)

    st_b = st.astype(BF16)
    v_big = big(v)
    a_all = _dot_nt(jnp.concatenate([kkd, rd], axis=0),
                    jnp.concatenate([big(k * e_neg), big(b * e_neg)], axis=0))
    yield
    ti = lax.broadcasted_iota(jnp.int32, (t, n), 0)
    si = lax.broadcasted_iota(jnp.int32, (t, n), 1) % t
    strict = ti > si
    incl = ti >= si
    a_kk = jnp.where(strict, a_all[:t, :n], 0.0)
    a_kb = jnp.where(strict, a_all[:t, n:], 0.0)
    a_rk = jnp.where(incl, a_all[t:, :n], 0.0)
    a_rb = jnp.where(incl, a_all[t:, n:], 0.0)
    rhs = _dot(kkd, st_b) + _dot(a_kk.astype(BF16), v_big)
    yield
    a_bd = jnp.concatenate([a_kb] * pk, axis=0) * bd
    m = jnp.where(lax.broadcasted_iota(jnp.int32, (n, n), 0) == lax.broadcasted_iota(jnp.int32, (n, n), 1), 1.0, 0.0)
    for lv in range(lvl_ref.shape[0]):
        mb = m.astype(BF16)
        ma = _dot(mb, (a_bd * lvl_ref[lv]).astype(BF16)).astype(BF16)
        yield
        m = m - _dot(ma, mb)
        yield
    u_big = _dot(m.astype(BF16), big(rhs))
    yield
    u = u_big[0:t]
    for h in range(1, pk):
        u = u + u_big[h * t:(h + 1) * t]
    y = _dot(rd, st_b) + _dot(a_rk.astype(BF16), v_big) - _dot(a_rb.astype(BF16), big(u))
    yield
    decay_col = jnp.broadcast_to(jnp.exp(c_end), st.shape).T
    kb_end = jnp.concatenate([k * e_end, -(b * e_end)], axis=0).astype(BF16)
    vu = jnp.concatenate([v, u], axis=0).astype(BF16)
    return y, decay_col * st + bd * _dot_tn(kb_end, vu)


def _rwkv_chunk_kernel(r_ref, k_ref, v_ref, kk_ref, b_ref, ld_ref, tri_ref, hl_ref, bd_ref, lvl_ref, y_ref, st_ref):
    @pl.when(pl.program_id(0) == 0)
    def _():
        st_ref[...] = jnp.zeros_like(st_ref)

    wp = RWKV_PACK * RWKV_HEAD_DIM
    tri, hl, bd = tri_ref[...], hl_ref[...], bd_ref[...]
    n_pack = r_ref.shape[2] // wp
    where = [(bi, slice(g * wp, (g + 1) * wp)) for bi in range(r_ref.shape[0]) for g in range(n_pack)]
    loaded = [tuple(ref[bi, :, cols] for ref in (r_ref, k_ref, v_ref, kk_ref, b_ref, ld_ref)) + (st_ref[i],)
              for i, (bi, cols) in enumerate(where)]
    results = _run_interleaved([_rwkv_chain(*args, tri, hl, bd, lvl_ref) for args in loaded])
    for i, ((bi, cols), (y, st_new)) in enumerate(zip(where, results)):
        y_ref[bi, :, cols] = y
        st_ref[i] = st_new


def rwkv_chunk(r, k, v, kk, b, ld, bsz, seq):
    t, pk = RWKV_CHUNK, RWKV_PACK
    n_chunk = seq // t
    w = RWKV_WIDTH
    wp = pk * RWKV_HEAD_DIM
    assert t == RWKV_HEAD_DIM
    tri, hl, bd, lvl = _rwkv_masks()
    blk = pl.BlockSpec((bsz, t, w), lambda c: (0, c, 0))
    f2 = lambda c: (0, 0)
    shaped = lambda a: a.reshape(bsz, seq, w)
    y = pl.pallas_call(
        _rwkv_chunk_kernel,
        grid=(n_chunk,),
        in_specs=[blk] * 6 + [pl.BlockSpec(tri.shape, f2), pl.BlockSpec(hl.shape, f2), pl.BlockSpec(bd.shape, f2),
                              pl.BlockSpec(lvl.shape, lambda c: (0, 0, 0))],
        out_specs=blk,
        out_shape=jax.ShapeDtypeStruct((bsz, seq, w), F32),
        scratch_shapes=[pltpu.VMEM((bsz * (w // wp), wp, wp), F32)],
        compiler_params=_params("arbitrary"),
        name="rwkv_chunk",
    )(shaped(r), shaped(k), shaped(v), shaped(kk), shaped(b), shaped(ld), tri, hl, bd, lvl)
    return y.reshape(bsz * seq, w)


def _rwkv_post_kernel(y_ref, bonus_ref, g_ref, lng_ref, lnb_ref, ones_ref, o_ref):
    y = y_ref[...]
    inv = 1.0 / RWKV_HEAD_DIM
    mu = _dot_split(y, ones_ref[...], 'a', 2) * inv
    yc = y - mu
    var = _dot_split(yc * yc, ones_ref[...], 'a', 2) * inv
    yn = yc * lax.rsqrt(var + RWKV_GN_EPS) * lng_ref[...] + lnb_ref[...]
    o_ref[...] = ((yn + bonus_ref[...]) * g_ref[...]).astype(o_ref.dtype)


def rwkv_post(y, bonus, g, ln_g, ln_b, tm=1024):
    n, w = y.shape
    tm = min(tm, n)
    row = pl.BlockSpec((tm, w), lambda i: (i, 0))
    vec = pl.BlockSpec((1, w), lambda i: (0, 0))
    return pl.pallas_call(
        _rwkv_post_kernel,
        grid=(n // tm,),
        in_specs=[row, row, row, vec, vec, pl.BlockSpec((w, w), lambda i: (0, 0))],
        out_specs=row,
        out_shape=jax.ShapeDtypeStruct((n, w), BF16),
        compiler_params=_params("parallel"),
        name="rwkv_post",
    )(y, bonus, g, ln_g.reshape(1, w), ln_b.reshape(1, w), _head_ones(w, RWKV_HEAD_DIM))


def rwkv7_mixer(h, bsz, seq, mu, w0, w_up, a0, a_up, g_up, k_k, k_a, r_k, ln_g, ln_b):
    r, k, v, kk, b, ld, g, bonus = rwkv_pre(h, bsz, seq, mu, w0, w_up, a0, a_up, g_up, k_k, k_a, r_k)
    y = rwkv_chunk(r, k, v, kk, b, ld, bsz, seq)
    return rwkv_post(y, bonus, g, ln_g, ln_b)


AB_IN = S5_WIDTH + NSA_WIDTH + 6 * NSA_KV_W + NSA_GATE_COLS
AB_IN_PADDED = -(-AB_IN // LANES) * LANES
NSA_GATE_COL_BLOCK = (AB_IN - NSA_GATE_COLS) // LANES
PROJ_TM = 512


def kernel(x, ab_w_in, ab_w_out, s5_lam_re, s5_lam_im, s5_log_dt, s5_b_re, s5_b_im, s5_c_re, s5_c_im, s5_d, s5_w_glu, s5_b_glu, nsa_pe_k, nsa_pe_v, nsa_ck_w1, nsa_ck_b1, nsa_ck_w2, nsa_cv_w1, nsa_cv_b1, nsa_cv_w2, cd_w_in, cd_w_out, rwkv_mu, rwkv_w0, rwkv_w_up, rwkv_a0, rwkv_a_up, rwkv_g_up, rwkv_k_k, rwkv_k_a, rwkv_r_k, rwkv_ln_g, rwkv_ln_b, ret_ln_g, ret_ln_b, ln1_g, ln1_b, ln2_g, ln2_b, moe_router, moe_bias, moe_w1, moe_w3, moe_w2, sh_w1, sh_w3, sh_w2):
    bsz, seq, d = x.shape
    assert (AB_IN - NSA_GATE_COLS) % LANES == 0
    xf = x.reshape(bsz * seq, d)
    x_in = xf
    for layer in range(DEPTH):
        i = layer // 2
        if layer % 2 == 0:
            w_in = jnp.pad(ab_w_in[i], ((0, 0), (0, AB_IN_PADDED - AB_IN))).astype(BF16)
            h, u3 = project(x_in, w_in, PROJ_TM, chunked=(S5_CHUNK, S5_WIDTH))
            y_1 = s5_mixer(h, u3, bsz, seq, s5_lam_re[i], s5_lam_im[i], s5_log_dt[i], s5_b_re[i], s5_b_im[i],
                           s5_c_re[i], s5_c_im[i], s5_d[i], s5_w_glu[i], s5_b_glu[i])
            y_2 = nsa_mixer(h, bsz, seq, NSA_GATE_COL_BLOCK, nsa_pe_k[i], nsa_pe_v[i], nsa_ck_w1[i], nsa_ck_b1[i],
                            nsa_ck_w2[i], nsa_cv_w1[i], nsa_cv_b1[i], nsa_cv_w2[i])
            w_out = ab_w_out[i]
        else:
            h = project(x_in, cd_w_in[i].astype(BF16), PROJ_TM)
            y_1 = rwkv7_mixer(h, bsz, seq, rwkv_mu[i], rwkv_w0[i], rwkv_w_up[i], rwkv_a0[i], rwkv_a_up[i],
                              rwkv_g_up[i], rwkv_k_k[i], rwkv_k_a[i], rwkv_r_k[i], rwkv_ln_g[i], rwkv_ln_b[i])
            y_2 = retention_mixer(h, bsz, seq, RWKV_COLS, ret_ln_g[i], ret_ln_b[i])
            w_out = cd_w_out[i]
        xf = out_proj_ln(y_1, y_2, w_out, xf, ln1_g[layer], ln1_b[layer])
        xf, x_in = moe_block(xf, moe_router[layer], moe_bias[layer], moe_w1[layer], moe_w3[layer],
                             moe_w2, layer, sh_w1[layer], sh_w3[layer], sh_w2[layer], ln2_g[layer], ln2_b[layer])
    return xf.reshape(bsz, seq, d)
```

```python
import functools
import math

import jax
import jax.numpy as jnp
import numpy as np
from jax import lax
from jax.experimental import pallas as pl
from jax.experimental.pallas import tpu as pltpu

F32 = jnp.float32
BF16 = jnp.bfloat16
HIGHEST = lax.Precision.HIGHEST
FP8 = jnp.float8_e4m3fn
FP8_MAX = 448.0
FP8_TINY = 1e-30

VMEM_LIMIT_BYTES = 52 * 1024 * 1024
LANES = 128

LN_EPS = 1e-5
DEPTH = 2
ALPHA = (2 * DEPTH) ** 0.25

S5_GROUPS, S5_GROUP_CH, S5_STATE = 32, 16, 64
S5_WIDTH = S5_GROUPS * S5_GROUP_CH
S5_CHUNK = 16
S5_PACK = 8
NSA_HEADS, NSA_KV_GROUPS, NSA_HEAD_DIM = 8, 2, 64
NSA_HPG = NSA_HEADS // NSA_KV_GROUPS
NSA_WIDTH = NSA_HEADS * NSA_HEAD_DIM
NSA_ROT_DIM = NSA_HEAD_DIM // 4
ROPE_THETA = 500000.0
CMP_BLOCK, CMP_STRIDE, CMP_HIDDEN = 32, 16, 128
SLC_BLOCK, N_SLC, WINDOW = 64, 16, 512
Q_BLOCK = 256
NSA_KEY_TILE = 1024
FORCE_SCORE = 1e6
MASK_VALUE = -1e30
RWKV_HEADS, RWKV_HEAD_DIM = 8, 64
RWKV_WIDTH = RWKV_HEADS * RWKV_HEAD_DIM
RWKV_LORA_W, RWKV_LORA_A, RWKV_LORA_G = 64, 64, 128
RWKV_COLS = 3 * RWKV_WIDTH + RWKV_LORA_W + RWKV_LORA_A + RWKV_LORA_G
RWKV_GN_EPS = 64e-5
RWKV_CHUNK = 64
RWKV_PACK = 4
RET_HEADS, RET_DK, RET_DV, RET_CHUNK = 4, 64, 128, 128
RET_THETA = 10000.0
RET_GN_EPS = 1e-5
N_EXPERTS, TOP_K, EXPERT_FF = 64, 8, 256
N_EXPERT_GROUPS, TOPK_GROUPS = 8, 4
EXPERTS_PER_GROUP = N_EXPERTS // N_EXPERT_GROUPS
ROUTED_SCALE = 2.5
MOE_EXPERTS_PER_STEP = 4


def _params(*sem):
    return pltpu.CompilerParams(dimension_semantics=sem, vmem_limit_bytes=VMEM_LIMIT_BYTES)


def _dot(a, b, **kw):
    return jnp.dot(a, b, preferred_element_type=F32, **kw)


def _dot_nt(a, b, **kw):
    return lax.dot_general(a, b, (((1,), (1,)), ((), ())), preferred_element_type=F32, **kw)


def _dot_tn(a, b, **kw):
    return lax.dot_general(a, b, (((0,), (0,)), ((), ())), preferred_element_type=F32, **kw)


def _dot_split(a, b, split, parts):
    rest = a if split == 'a' else b
    acc = None
    for _ in range(parts):
        piece = rest.astype(BF16)
        term = _dot(piece, b) if split == 'a' else _dot(a, piece)
        acc = term if acc is None else acc + term
        rest = rest - piece.astype(F32)
    return acc


def _run_interleaved(gens):
    results = [None] * len(gens)
    live = list(range(len(gens)))
    while live:
        for i in list(live):
            try:
                next(gens[i])
            except StopIteration as done:
                results[i] = done.value
                live.remove(i)
    return results


def _gelu(x):
    return 0.5 * x * (1.0 + jnp.tanh(math.sqrt(2.0 / math.pi) * (x + 0.044715 * (x * x * x))))


def _sigmoid(x):
    return 1.0 / (1.0 + jnp.exp(-x))


def _layer_norm_rows(z, g, b):
    mu = jnp.mean(z, axis=-1, keepdims=True)
    zc = z - mu
    var = jnp.mean(zc * zc, axis=-1, keepdims=True)
    return zc * lax.rsqrt(var + LN_EPS) * g + b


def _proj_kernel(x_ref, w_ref, o_ref, *chunked_ref):
    y = _dot(x_ref[...].astype(BF16), w_ref[...])
    o_ref[...] = y
    for c_ref in chunked_ref:
        rows, t, w = c_ref.shape
        c_ref[...] = y[:, :w].reshape(rows, t, w)


def project(x, w_bf16, tm, chunked=None):
    m, k = x.shape
    n = w_bf16.shape[1]
    out_specs = [pl.BlockSpec((tm, n), lambda i: (i, 0))]
    out_shape = [jax.ShapeDtypeStruct((m, n), F32)]
    if chunked is not None:
        t, w = chunked
        out_specs.append(pl.BlockSpec((tm // t, t, w), lambda i: (i, 0, 0)))
        out_shape.append(jax.ShapeDtypeStruct((m // t, t, w), F32))
    out = pl.pallas_call(
        _proj_kernel,
        grid=(m // tm,),
        in_specs=[pl.BlockSpec((tm, k), lambda i: (i, 0)), pl.BlockSpec((k, n), lambda i: (0, 0))],
        out_specs=out_specs,
        out_shape=out_shape,
        compiler_params=_params("parallel"),
        name="project",
    )(x, w_bf16)
    return out if chunked is not None else out[0]


def _out_proj_ln_kernel(ya_ref, yb_ref, wa_ref, wb_ref, x_ref, g_ref, b_ref, o_ref):
    mix = _dot(ya_ref[...], wa_ref[...]) + _dot(yb_ref[...], wb_ref[...])
    o_ref[...] = _layer_norm_rows(ALPHA * x_ref[...] + mix, g_ref[...], b_ref[...])


def out_proj_ln(ya, yb, w_out, x, g, b, tm=512):
    n, d = x.shape
    ka, kb = ya.shape[1], yb.shape[1]
    wa = w_out[:ka].astype(BF16)
    wb = w_out[ka:].astype(BF16)
    row = lambda i: (i, 0)
    fixed = lambda i: (0, 0)
    return pl.pallas_call(
        _out_proj_ln_kernel,
        grid=(n // tm,),
        in_specs=[pl.BlockSpec((tm, ka), row), pl.BlockSpec((tm, kb), row),
                  pl.BlockSpec((ka, d), fixed), pl.BlockSpec((kb, d), fixed),
                  pl.BlockSpec((tm, d), row), pl.BlockSpec((1, d), fixed), pl.BlockSpec((1, d), fixed)],
        out_specs=pl.BlockSpec((tm, d), row),
        out_shape=jax.ShapeDtypeStruct((n, d), F32),
        compiler_params=_params("parallel"),
        name="out_proj_ln",
    )(ya, yb, wa, wb, x, g.reshape(1, d), b.reshape(1, d))


def _router_kernel(x_ref, rt_ref, bias_ref, o_ref):
    tr = x_ref.shape[0]
    scores = _sigmoid(_dot_nt(rt_ref[...], x_ref[...], precision=HIGHEST))
    biased = scores + bias_ref[...]
    grp = biased.reshape(N_EXPERT_GROUPS, EXPERTS_PER_GROUP, tr)
    pos = lax.broadcasted_iota(jnp.int32, grp.shape, 1)
    m1 = jnp.max(grp, axis=1, keepdims=True)
    first = jnp.min(jnp.where(grp == m1, pos, EXPERTS_PER_GROUP), axis=1, keepdims=True)
    m2 = jnp.max(jnp.where(pos == first, -jnp.inf, grp), axis=1, keepdims=True)
    gscore = (m1 + m2).reshape(N_EXPERT_GROUPS, tr)
    gidx = lax.broadcasted_iota(jnp.int32, gscore.shape, 0)
    grank = jnp.zeros(gscore.shape, F32)
    for j in range(N_EXPERT_GROUPS):
        row = gscore[j:j + 1, :]
        grank = grank + jnp.where(gidx > j, jnp.where(row >= gscore, 1.0, 0.0), jnp.where(row > gscore, 1.0, 0.0))
    gkeep = jnp.where(grank < TOPK_GROUPS, 1.0, 0.0)
    keep = jnp.broadcast_to(gkeep[:, None, :], grp.shape).reshape(N_EXPERTS, tr)
    masked = jnp.where(keep > 0.5, biased, -jnp.inf)
    eidx = lax.broadcasted_iota(jnp.int32, masked.shape, 0)
    rank = jnp.zeros(masked.shape, F32)
    for j in range(N_EXPERTS):
        row = masked[j:j + 1, :]
        rank = rank + jnp.where(eidx > j, jnp.where(row >= masked, 1.0, 0.0), jnp.where(row > masked, 1.0, 0.0))
    gate = jnp.where(rank < TOP_K, scores, 0.0)
    gate = gate / jnp.sum(gate, axis=0, keepdims=True) * ROUTED_SCALE
    o_ref[...] = jnp.concatenate([gate, jnp.zeros((LANES - N_EXPERTS, tr), F32)], axis=0).T


def moe_router(x, router, bias, tr=512):
    n, d = x.shape
    return pl.pallas_call(
        _router_kernel,
        grid=(n // tr,),
        in_specs=[pl.BlockSpec((tr, d), lambda i: (i, 0)),
                  pl.BlockSpec((N_EXPERTS, d), lambda i: (0, 0)),
                  pl.BlockSpec((N_EXPERTS, 1), lambda i: (0, 0))],
        out_specs=pl.BlockSpec((tr, LANES), lambda i: (i, 0)),
        out_shape=jax.ShapeDtypeStruct((n, LANES), F32),
        compiler_params=_params("parallel"),
        name="moe_router",
    )(x, router.T, bias.reshape(N_EXPERTS, 1))


def _quantize_fp8(a, axes):
    amax = jnp.max(jnp.abs(a), axis=axes, keepdims=True)
    scale = jnp.maximum(amax, FP8_TINY) * (1.0 / FP8_MAX)
    return (a * (1.0 / scale)).astype(FP8), scale


def _swiglu_hidden(xq, x_scale, w1q, w3q, w_scale, gate=None):
    col1 = x_scale * w_scale[0:1, 0:1]
    col3 = x_scale * w_scale[1:2, 0:1]
    if gate is not None:
        col3 = col3 * gate
    h1 = _dot(xq, w1q) * col1
    return h1 * _sigmoid(h1) * (_dot(xq, w3q) * col3)


def _experts_ln_kernel(x_ref, gate_ref, w1_ref, w3_ref, ws_ref, w2_ref, sw1_ref, sw3_ref, sws_ref, sw2_ref,
                       g_ref, b_ref, o_ref, obf_ref, acc_ref, xq_ref, xs_ref):
    step = pl.program_id(1)
    per_step = w1_ref.shape[0]

    @pl.when(step == 0)
    def _():
        xq, xs = _quantize_fp8(x_ref[...], (1,))
        xq_ref[...] = xq
        xs_ref[...] = xs
        h = _swiglu_hidden(xq, xs, sw1_ref[...], sw3_ref[...], sws_ref[...])
        acc_ref[...] = _dot(h.astype(BF16), sw2_ref[...].astype(BF16))

    lane = lax.broadcasted_iota(jnp.int32, gate_ref.shape, 1)
    gates = gate_ref[...]
    xq, xs = xq_ref[...], xs_ref[...]
    hidden = []
    for j in range(per_step):
        gcol = jnp.sum(jnp.where(lane == step * per_step + j, gates, 0.0), axis=1, keepdims=True)
        hidden.append(_swiglu_hidden(xq, xs, w1_ref[j], w3_ref[j], ws_ref[j], gcol).astype(BF16))
    w2 = w2_ref[0].astype(BF16)
    acc_ref[...] += _dot(jnp.concatenate(hidden, axis=1), w2.reshape(per_step * w2.shape[1], w2.shape[2]))

    @pl.when(step == pl.num_programs(1) - 1)
    def _():
        y = _layer_norm_rows(ALPHA * x_ref[...] + acc_ref[...], g_ref[...], b_ref[...])
        o_ref[...] = y
        obf_ref[...] = y.astype(BF16)


def _quantize_expert_weights(w1, w3):
    w1q, s1 = _quantize_fp8(w1, (-2, -1))
    w3q, s3 = _quantize_fp8(w3, (-2, -1))
    scales = jnp.broadcast_to(jnp.concatenate([s1, s3], axis=-2), s1.shape[:-2] + (2, w1.shape[-1]))
    return w1q, w3q, scales


def moe_experts_ln(x, gates, w1, w3, w2_layers, layer, sw1, sw3, sw2, g, b, tm=1024):
    n, d = x.shape
    ne = w1.shape[0]
    w1q, w3q, ws = _quantize_expert_weights(w1, w3)
    sw1q, sw3q, sws = _quantize_expert_weights(sw1, sw3)
    tok = lambda i, e: (i, 0)
    fixed = lambda i, e: (0, 0)
    per_expert = lambda *blk: pl.BlockSpec((MOE_EXPERTS_PER_STEP,) + blk, lambda i, e: (e, 0, 0))
    return pl.pallas_call(
        _experts_ln_kernel,
        grid=(n // tm, ne // MOE_EXPERTS_PER_STEP),
        in_specs=[pl.BlockSpec((tm, d), tok), pl.BlockSpec((tm, LANES), tok),
                  per_expert(d, EXPERT_FF), per_expert(d, EXPERT_FF), per_expert(2, EXPERT_FF),
                  pl.BlockSpec((1, MOE_EXPERTS_PER_STEP, EXPERT_FF, d), lambda i, e: (layer, e, 0, 0)),
                  pl.BlockSpec((d, EXPERT_FF), fixed), pl.BlockSpec((d, EXPERT_FF), fixed),
                  pl.BlockSpec((2, EXPERT_FF), fixed), pl.BlockSpec((EXPERT_FF, d), fixed),
                  pl.BlockSpec((1, d), fixed), pl.BlockSpec((1, d), fixed)],
        out_specs=[pl.BlockSpec((tm, d), tok), pl.BlockSpec((tm, d), tok)],
        out_shape=[jax.ShapeDtypeStruct((n, d), F32), jax.ShapeDtypeStruct((n, d), BF16)],
        scratch_shapes=[pltpu.VMEM((tm, d), F32), pltpu.VMEM((tm, d), FP8), pltpu.VMEM((tm, 1), F32)],
        compiler_params=_params("parallel", "arbitrary"),
        name="moe_experts_ln",
    )(x, gates, w1q, w3q, ws, w2_layers, sw1q, sw3q, sws, sw2, g.reshape(1, d), b.reshape(1, d))


def moe_block(x, router, bias, w1, w3, w2_layers, layer, sw1, sw3, sw2, g, b):
    gates = moe_router(x, router, bias)
    return moe_experts_ln(x, gates, w1, w3, w2_layers, layer, sw1, sw3, sw2, g, b)


def _s5_tables(lam_re, lam_im, log_dt, b_re, b_im, c_re, c_im, n_chunk):
    t, h, p = S5_CHUNK, S5_GROUP_CH, S5_STATE
    dt = jnp.exp(log_dt.astype(F32))[:, None]
    den = lam_re ** 2 + lam_im ** 2

    def lam_pow(k):
        k = jnp.asarray(k, F32)[..., None, None]
        mag = jnp.exp(lam_re * dt * k)
        return mag * jnp.cos(lam_im * dt * k), mag * jnp.sin(lam_im * dt * k)

    lb_re, lb_im = lam_pow(1.0)
    f_re = ((lb_re - 1.0) * lam_re + lb_im * lam_im) / den
    f_im = (lb_im * lam_re - (lb_re - 1.0) * lam_im) / den
    bb_re = f_re[..., None] * b_re - f_im[..., None] * b_im
    bb_im = f_re[..., None] * b_im + f_im[..., None] * b_re
    pr, pi = lam_pow(jnp.arange(t))
    cl_re = c_re[None] * pr[:, :, None, :] - c_im[None] * pi[:, :, None, :]
    cl_im = c_re[None] * pi[:, :, None, :] + c_im[None] * pr[:, :, None, :]
    klag = jnp.einsum('tgop,gpi->tgoi', cl_re, bb_re) - jnp.einsum('tgop,gpi->tgoi', cl_im, bb_im)
    nb = S5_GROUPS // S5_PACK
    split = lambda a, axis: a.reshape(a.shape[:axis] + (nb, S5_PACK) + a.shape[axis + 1:])
    eye = jnp.eye(S5_PACK, dtype=F32)
    lag_t = jnp.transpose(split(klag, 1), (1, 0, 2, 4, 3))
    lag_t = (lag_t[:, :, :, :, None, :] * eye[None, None, :, None, :, None]).reshape(nb, t, LANES, LANES)
    qr, qi = lam_pow(t - 1 - jnp.arange(t))
    st_re = qr[..., None] * bb_re[None] - qi[..., None] * bb_im[None]
    st_im = qr[..., None] * bb_im[None] + qi[..., None] * bb_re[None]
    st = jnp.stack([st_re, st_im], axis=0)
    st_t = jnp.transpose(split(st, 2), (2, 1, 3, 5, 0, 4)).reshape(nb, t, LANES, 2 * p)
    er, ei = lam_pow(jnp.arange(t) + 1)
    x_re = c_re[None] * er[:, :, None, :] - c_im[None] * ei[:, :, None, :]
    x_im = c_re[None] * ei[:, :, None, :] + c_im[None] * er[:, :, None, :]
    cr = jnp.stack([x_re, -x_im], axis=0)
    cr_t = jnp.transpose(split(cr, 2), (2, 1, 0, 5, 3, 4)).reshape(nb, t, 2 * p, LANES)
    levels = max(1, int(math.log2(n_chunk)))
    sr, si = lam_pow(t * (2.0 ** jnp.arange(levels)))
    sr = sr.reshape(levels, nb, S5_PACK * p)
    si = si.reshape(levels, nb, S5_PACK * p)
    a1 = jnp.concatenate([sr, sr], axis=-1)
    a2 = jnp.concatenate([-si, si], axis=-1)
    scan = jnp.transpose(jnp.stack([a1, a2], axis=1), (2, 0, 1, 3))
    return lag_t, st_t, cr_t, scan.astype(F32)


def _s5_build_tables(lag_ref, st_ref, cr_ref, wtoe_ref, wstate_ref, wcross_ref):
    t = lag_ref.shape[1]
    p = S5_STATE
    kp = S5_PACK * p
    wtoe_ref[...] = jnp.zeros_like(wtoe_ref)
    for d in range(t):
        tile = lag_ref[0, d].astype(BF16)
        for j in range(t - d):
            wtoe_ref[j * LANES:(j + 1) * LANES, (j + d) * LANES:(j + d + 1) * LANES] = tile
    lane = lax.broadcasted_iota(jnp.int32, (LANES, LANES), 1)
    row_g = lax.broadcasted_iota(jnp.int32, (LANES, kp), 0) // S5_GROUP_CH
    same_s = row_g == lax.broadcasted_iota(jnp.int32, (LANES, kp), 1) // p
    for j in range(t):
        a = st_ref[0, j]
        swapped = pltpu.roll(a, p, 1)
        for c, both in enumerate((jnp.where(lane < p, a, swapped), jnp.where(lane < p, swapped, a))):
            wide = jnp.concatenate([both] * (kp // LANES), axis=1)
            wstate_ref[j * LANES:(j + 1) * LANES, c * kp:(c + 1) * kp] = jnp.where(same_s, wide, 0.0).astype(BF16)
    same_c = (lax.broadcasted_iota(jnp.int32, (kp, LANES), 0) // p
              == lax.broadcasted_iota(jnp.int32, (kp, LANES), 1) // S5_GROUP_CH)
    for i in range(t):
        a = cr_ref[0, i]
        for c in range(2):
            tall = jnp.concatenate([a[c * p:(c + 1) * p]] * S5_PACK, axis=0)
            wcross_ref[c * kp:(c + 1) * kp, i * LANES:(i + 1) * LANES] = jnp.where(same_c, tall, 0.0).astype(BF16)


def _s5_kernel(u_ref, lag_ref, st_ref, cr_ref, scan_ref, o_ref, wtoe_ref, wstate_ref, wcross_ref):
    @pl.when(pl.program_id(1) == 0)
    def _():
        _s5_build_tables(lag_ref, st_ref, cr_ref, wtoe_ref, wstate_ref, wcross_ref)

    n_chunk, t, _ = u_ref.shape
    x = jnp.concatenate([u_ref[:, j, :] for j in range(t)], axis=1).astype(BF16)
    local = _dot(x, wtoe_ref[...])
    state = _dot(x, wstate_ref[...])
    row = lax.broadcasted_iota(jnp.int32, state.shape, 0)
    s = jnp.where(row >= 1, pltpu.roll(state, 1, 0), 0.0)
    half = state.shape[1] // 2
    level = 0
    d = 1
    while d < n_chunk:
        mult = scan_ref[0, level]
        prev = jnp.where(row >= d, pltpu.roll(s, d, 0), 0.0)
        s = s + mult[0:1, :] * prev + mult[1:2, :] * pltpu.roll(prev, half, 1)
        d *= 2
        level += 1
    y = local + _dot(s.astype(BF16), wcross_ref[...])
    for i in range(t):
        o_ref[:, i, :] = y[:, i * LANES:(i + 1) * LANES]


def s5_scan(u3, bsz, lag_t, st_t, cr_t, scan):
    rows, t, w = u3.shape
    n_chunk = rows // bsz
    kp2 = 2 * S5_PACK * S5_STATE
    table = lambda a: pl.BlockSpec((1,) + a.shape[1:], lambda j, b: (j, 0, 0, 0))
    return pl.pallas_call(
        _s5_kernel,
        grid=(w // LANES, bsz),
        in_specs=[pl.BlockSpec((n_chunk, t, LANES), lambda j, b: (b, 0, j)),
                  table(lag_t), table(st_t), table(cr_t), table(scan)],
        out_specs=pl.BlockSpec((n_chunk, t, LANES), lambda j, b: (b, 0, j)),
        out_shape=jax.ShapeDtypeStruct(u3.shape, F32),
        scratch_shapes=[pltpu.VMEM((t * LANES, t * LANES), BF16), pltpu.VMEM((t * LANES, kp2), BF16),
                        pltpu.VMEM((kp2, t * LANES), BF16)],
        compiler_params=_params("arbitrary", "arbitrary"),
        name="s5_scan",
    )(u3, lag_t, st_t, cr_t, scan)


def _s5_post_kernel(y_ref, u_ref, d_ref, w_ref, b_ref, o_ref):
    u = u_ref[...]
    y = _gelu(y_ref[...].reshape(u.shape) + d_ref[...] * u)
    o_ref[...] = (y * _sigmoid(_dot(y.astype(BF16), w_ref[...]) + b_ref[...])).astype(o_ref.dtype)


def s5_post(y3, h, d_skip, w_glu, b_glu, tm=1024):
    rows, t, w = y3.shape
    n = rows * t
    tm = min(tm, n)
    row = lambda i: (i, 0)
    fixed = lambda i: (0, 0)
    return pl.pallas_call(
        _s5_post_kernel,
        grid=(n // tm,),
        in_specs=[pl.BlockSpec((tm // t, t, w), lambda i: (i, 0, 0)), pl.BlockSpec((tm, w), row),
                  pl.BlockSpec((1, w), fixed), pl.BlockSpec((w, w), fixed), pl.BlockSpec((1, w), fixed)],
        out_specs=pl.BlockSpec((tm, w), row),
        out_shape=jax.ShapeDtypeStruct((n, w), BF16),
        compiler_params=_params("parallel"),
        name="s5_post",
    )(y3, h, d_skip.reshape(1, w), w_glu.astype(BF16), b_glu.reshape(1, w))


def s5_mixer(h, u3, bsz, seq, lam_re, lam_im, log_dt, b_re, b_im, c_re, c_im, d_skip, w_glu, b_glu):
    tables = _s5_tables(lam_re, lam_im, log_dt, b_re, b_im, c_re, c_im, seq // S5_CHUNK)
    return s5_post(s5_scan(u3, bsz, *tables), h, d_skip, w_glu, b_glu)


def _rope_tables(pos, rot_dim, theta, head_dim, n_heads):
    half = rot_dim // 2
    f32 = np.float32
    inv_freq = f32(theta) ** (-np.arange(half, dtype=f32) / f32(half))
    ang = (pos.astype(f32)[:, None] * inv_freq[None, :]).astype(np.float64)
    cos, sin = np.cos(ang), np.sin(ang)
    rest = head_dim - rot_dim
    n = pos.shape[0]
    c = np.concatenate([cos, cos, np.ones((n, rest))], axis=1)
    s_up = np.concatenate([-sin, np.zeros((n, half + rest))], axis=1)
    s_dn = np.concatenate([np.zeros((n, half)), sin, np.zeros((n, rest))], axis=1)
    tile = lambda a: jnp.asarray(np.tile(a, (1, n_heads)), F32)
    return tile(c), tile(s_up), tile(s_dn)


def _rope_apply(x, c, s_up, s_dn, half):
    return x * c + pltpu.roll(x, LANES - half, 1) * s_up + pltpu.roll(x, half, 1) * s_dn


def _retention_tables():
    c = RET_CHUNK
    log_gamma = np.log(1.0 - 2.0 ** (-5.0 - np.arange(RET_HEADS, dtype=np.float64)))
    i = np.arange(c, dtype=np.float64)
    diff = i[:, None] - i[None, :]
    decay = np.where(diff >= 0, np.exp(diff[None] * log_gamma[:, None, None]), 0.0)
    qdec = np.repeat(np.exp((i + 1.0)[:, None] * log_gamma[None, :]), RET_DK, axis=1)
    kdec = np.repeat(np.exp((c - 1.0 - i)[:, None] * log_gamma[None, :]), RET_DK, axis=1)
    chunk_decay = [float(v) for v in np.exp(c * log_gamma)]
    return jnp.asarray(decay, F32), jnp.asarray(qdec, F32), jnp.asarray(kdec, F32), chunk_decay


def _retention_kernel(chunk_decay, q_ref, k_ref, v0_ref, v1_ref, g0_ref, g1_ref, c_ref, su_ref, sd_ref,
                      dec_ref, qdec_ref, kdec_ref, lng_ref, lnb_ref, o_ref, state_ref):
    @pl.when(pl.program_id(1) == 0)
    def _():
        state_ref[...] = jnp.zeros_like(state_ref)

    half = RET_DK // 2
    tabs = (c_ref[...], su_ref[...], sd_ref[...])
    q = jnp.concatenate([_rope_apply(q_ref[:, s:s + LANES], *tabs, half) for s in (0, LANES)], axis=1)
    k = jnp.concatenate([_rope_apply(k_ref[:, s:s + LANES], *tabs, half) for s in (0, LANES)], axis=1)
    k = k * (RET_DK ** -0.5)
    q_dec = q * qdec_ref[...]
    k_dec = k * kdec_ref[...]
    v = jnp.concatenate([v0_ref[...], v1_ref[...]], axis=1)
    gate = jnp.concatenate([g0_ref[...], g1_ref[...]], axis=1)
    states = [state_ref[h] for h in range(RET_HEADS)]

    def head(h):
        ks = slice(h * RET_DK, (h + 1) * RET_DK)
        vh = v[:, h * RET_DV:(h + 1) * RET_DV].astype(BF16)
        scores = _dot_nt(q[:, ks].astype(BF16), k[:, ks].astype(BF16)) * dec_ref[h]
        yield
        y = _dot(scores.astype(BF16), vh) + _dot(q_dec[:, ks].astype(BF16), states[h].astype(BF16))
        yield
        new_state = states[h] * chunk_decay[h] + _dot_tn(k_dec[:, ks].astype(BF16), vh)
        yield
        mu = jnp.mean(y, axis=-1, keepdims=True)
        yc = y - mu
        var = jnp.mean(yc * yc, axis=-1, keepdims=True)
        return yc * lax.rsqrt(var + RET_GN_EPS), new_state

    results = _run_interleaved([head(h) for h in range(RET_HEADS)])
    outs = [r[0] for r in results]
    for h, (_, new_state) in enumerate(results):
        state_ref[h] = new_state
    yn = jnp.concatenate(outs, axis=1) * lng_ref[...] + lnb_ref[...]
    o_ref[...] = (gate * _sigmoid(gate) * yn).astype(o_ref.dtype)


def retention_mixer(h, bsz, seq, col0, ln_g, ln_b):
    c = RET_CHUNK
    n_chunk = seq // c
    qk_w = RET_HEADS * RET_DK
    v_w = RET_HEADS * RET_DV
    assert col0 % qk_w == 0 and qk_w == 2 * LANES and v_w == 2 * qk_w
    cb = col0 // qk_w
    rc, rsu, rsd = _rope_tables(np.arange(seq), RET_DK, RET_THETA, RET_DK, 2)
    dec, qdec, kdec, chunk_decay = _retention_tables()
    row = lambda j: (lambda b, n: (b * n_chunk + n, j))
    pos = lambda b, n: (n, 0)
    fixed2 = lambda b, n: (0, 0)
    kern = functools.partial(_retention_kernel, chunk_decay)
    return pl.pallas_call(
        kern,
        grid=(bsz, n_chunk),
        in_specs=[pl.BlockSpec((c, qk_w), row(cb)), pl.BlockSpec((c, qk_w), row(cb + 1)),
                  pl.BlockSpec((c, qk_w), row(cb + 2)), pl.BlockSpec((c, qk_w), row(cb + 3)),
                  pl.BlockSpec((c, qk_w), row(cb + 4)), pl.BlockSpec((c, qk_w), row(cb + 5)),
                  pl.BlockSpec((c, LANES), pos), pl.BlockSpec((c, LANES), pos), pl.BlockSpec((c, LANES), pos),
                  pl.BlockSpec((RET_HEADS, c, c), lambda b, n: (0, 0, 0)),
                  pl.BlockSpec((c, qk_w), fixed2), pl.BlockSpec((c, qk_w), fixed2),
                  pl.BlockSpec((1, v_w), fixed2), pl.BlockSpec((1, v_w), fixed2)],
        out_specs=pl.BlockSpec((c, v_w), lambda b, n: (b * n_chunk + n, 0)),
        out_shape=jax.ShapeDtypeStruct((bsz * seq, v_w), BF16),
        scratch_shapes=[pltpu.VMEM((RET_HEADS, RET_DK, RET_DV), F32)],
        compiler_params=_params("parallel", "arbitrary"),
        name="retention",
    )(h, h, h, h, h, h, rc, rsu, rsd, dec, qdec, kdec, ln_g.reshape(1, v_w), ln_b.reshape(1, v_w))


NSA_KV_W = NSA_KV_GROUPS * NSA_HEAD_DIM
NSA_VT_ROWS = NSA_HEAD_DIM + 16
NSA_GATE_COLS = 3 * NSA_HEADS


def _nsa_prep_kernel(q_ref, kvc_ref, kvs_ref, kvw_ref, c_ref, su_ref, sd_ref,
                     qo_ref, kc_ref, vc_ref, ks_ref, vs_ref, kw_ref, vw_ref):
    half = NSA_ROT_DIM // 2
    tabs = (c_ref[...], su_ref[...], sd_ref[...])
    scale = NSA_HEAD_DIM ** -0.5 * math.log2(math.e)
    q = jnp.concatenate(
        [_rope_apply(q_ref[:, s:s + LANES], *tabs, half) * scale for s in range(0, NSA_WIDTH, LANES)], axis=1)
    qo_ref[0] = q.T.astype(qo_ref.dtype)

    def split(x, o_ref):
        for g in range(NSA_KV_GROUPS):
            o_ref[0, g] = x[:, g * NSA_HEAD_DIM:(g + 1) * NSA_HEAD_DIM].astype(o_ref.dtype)

    def split_t(x, o_ref):
        xt = x.T
        for g in range(NSA_KV_GROUPS):
            o_ref[0, g, :NSA_HEAD_DIM] = xt[g * NSA_HEAD_DIM:(g + 1) * NSA_HEAD_DIM, :].astype(o_ref.dtype)
            o_ref[0, g, NSA_HEAD_DIM:] = jnp.ones((NSA_VT_ROWS - NSA_HEAD_DIM, xt.shape[1]), o_ref.dtype)

    split(kvc_ref[:, :NSA_KV_W], kc_ref)
    split(kvc_ref[:, NSA_KV_W:], vc_ref)
    n_hot = ks_ref.shape[3] - NSA_HEAD_DIM
    tl = q_ref.shape[0]
    blk = (pl.program_id(1) * tl + lax.broadcasted_iota(jnp.int32, (tl, n_hot), 0)) // SLC_BLOCK
    one_hot = jnp.where(blk % n_hot == lax.broadcasted_iota(jnp.int32, (tl, n_hot), 1), 1.0, 0.0)
    ks = _rope_apply(kvs_ref[:, :NSA_KV_W], *tabs, half)
    for g in range(NSA_KV_GROUPS):
        ks_ref[0, g] = jnp.concatenate([ks[:, g * NSA_HEAD_DIM:(g + 1) * NSA_HEAD_DIM], one_hot],
                                       axis=1).astype(ks_ref.dtype)
    split_t(kvs_ref[:, NSA_KV_W:], vs_ref)
    split(_rope_apply(kvw_ref[:, :NSA_KV_W], *tabs, half), kw_ref)
    split_t(kvw_ref[:, NSA_KV_W:], vw_ref)


def nsa_prep(h, bsz, seq, tl=512):
    tl = min(tl, seq)
    nl = seq // tl
    rc, rsu, rsd = _rope_tables(np.arange(seq), NSA_ROT_DIM, ROPE_THETA, NSA_HEAD_DIM, LANES // NSA_HEAD_DIM)
    row = lambda j: (lambda b, l: (b * nl + l, j))
    pos = lambda b, l: (l, 0)
    kv_out = pl.BlockSpec((1, NSA_KV_GROUPS, tl, NSA_HEAD_DIM), lambda b, l: (b, 0, l, 0))
    kv_shape = lambda dt: jax.ShapeDtypeStruct((bsz, NSA_KV_GROUPS, seq, NSA_HEAD_DIM), dt)
    vt_out = pl.BlockSpec((1, NSA_KV_GROUPS, NSA_VT_ROWS, tl), lambda b, l: (b, 0, 0, l))
    vt_shape = jax.ShapeDtypeStruct((bsz, NSA_KV_GROUPS, NSA_VT_ROWS, seq), BF16)
    ks_w = NSA_HEAD_DIM + min(NSA_KEY_TILE, seq) // SLC_BLOCK
    ks_out = pl.BlockSpec((1, NSA_KV_GROUPS, tl, ks_w), lambda b, l: (b, 0, l, 0))
    ks_shape = jax.ShapeDtypeStruct((bsz, NSA_KV_GROUPS, seq, ks_w), BF16)
    two = 2 * NSA_KV_W
    return pl.pallas_call(
        _nsa_prep_kernel,
        grid=(bsz, nl),
        in_specs=[pl.BlockSpec((tl, NSA_WIDTH), row(1)),
                  pl.BlockSpec((tl, two), row(4)), pl.BlockSpec((tl, two), row(5)), pl.BlockSpec((tl, two), row(6)),
                  pl.BlockSpec((tl, LANES), pos), pl.BlockSpec((tl, LANES), pos), pl.BlockSpec((tl, LANES), pos)],
        out_specs=[pl.BlockSpec((1, NSA_WIDTH, tl), lambda b, l: (b, 0, l)),
                   kv_out, kv_out, ks_out, vt_out, kv_out, vt_out],
        out_shape=[jax.ShapeDtypeStruct((bsz, NSA_WIDTH, seq), BF16),
                   kv_shape(F32), kv_shape(F32), ks_shape, vt_shape, kv_shape(BF16), vt_shape],
        compiler_params=_params("parallel", "parallel"),
        name="nsa_prep",
    )(h, h, h, h, rc, rsu, rsd)


def _nsa_compress_kernel(hk_ref, hv_ref, pek_ref, pev_ref, kw1_ref, kb1_ref, kw2_ref, vw1_ref, vb1_ref, vw2_ref,
                         c_ref, su_ref, sd_ref, ko_ref, vo_ref):
    def mlp(h_ref, pe_ref, w1_ref, b1_ref, w2_ref):
        hb = h_ref[0, 0]
        rows = hb.shape[0]
        first = _dot((hb + pe_ref[0:1, :]).astype(BF16), w1_ref[0])
        second = _dot((hb + pe_ref[1:2, :]).astype(BF16), w1_ref[1])
        hid = _gelu(first + pltpu.roll(second, rows - 1, 0) + b1_ref[...])
        return _dot(hid.astype(BF16), w2_ref[...])

    kc = _rope_apply(mlp(hk_ref, pek_ref, kw1_ref, kb1_ref, kw2_ref), c_ref[...], su_ref[...], sd_ref[...],
                     NSA_ROT_DIM // 2)
    vc = mlp(hv_ref, pev_ref, vw1_ref, vb1_ref, vw2_ref)
    ko_ref[0, 0] = kc[:, :NSA_HEAD_DIM].astype(ko_ref.dtype)
    vo_ref[0, 0, :NSA_HEAD_DIM] = vc.T[:NSA_HEAD_DIM, :].astype(vo_ref.dtype)
    vo_ref[0, 0, NSA_HEAD_DIM:] = jnp.ones((NSA_VT_ROWS - NSA_HEAD_DIM, vc.shape[0]), vo_ref.dtype)


def nsa_compress(kc, vc, pe_k, pe_v, ck_w1, ck_b1, ck_w2, cv_w1, cv_b1, cv_w2):
    bsz, grp, seq, d = kc.shape
    n_rows = seq // CMP_STRIDE
    flat = CMP_STRIDE * d
    cmp_end = np.arange(n_rows) * CMP_STRIDE + CMP_BLOCK - 1
    rc, rsu, rsd = _rope_tables(cmp_end, NSA_ROT_DIM, ROPE_THETA, NSA_HEAD_DIM, LANES // NSA_HEAD_DIM)
    pad_w2 = lambda w: jnp.pad(w, ((0, 0), (0, LANES - d))).astype(BF16)
    blk = pl.BlockSpec((1, 1, n_rows, flat), lambda b, g: (b, g, 0, 0))
    f2 = lambda b, g: (0, 0)
    f3 = lambda b, g: (0, 0, 0)
    w_specs = [pl.BlockSpec((2, flat, CMP_HIDDEN), f3), pl.BlockSpec((1, CMP_HIDDEN), f2),
               pl.BlockSpec((CMP_HIDDEN, LANES), f2)]
    return pl.pallas_call(
        _nsa_compress_kernel,
        grid=(bsz, grp),
        in_specs=[blk, blk, pl.BlockSpec((2, flat), f2), pl.BlockSpec((2, flat), f2)] + w_specs + w_specs
                 + [pl.BlockSpec((n_rows, LANES), f2)] * 3,
        out_specs=[pl.BlockSpec((1, 1, n_rows, d), lambda b, g: (b, g, 0, 0)),
                   pl.BlockSpec((1, 1, NSA_VT_ROWS, n_rows), lambda b, g: (b, g, 0, 0))],
        out_shape=[jax.ShapeDtypeStruct((bsz, grp, n_rows, d), BF16),
                   jax.ShapeDtypeStruct((bsz, grp, NSA_VT_ROWS, n_rows), BF16)],
        compiler_params=_params("parallel", "parallel"),
        name="nsa_compress",
    )(kc.reshape(bsz, grp, n_rows, flat), vc.reshape(bsz, grp, n_rows, flat),
      pe_k.reshape(2, flat), pe_v.reshape(2, flat),
      ck_w1.reshape(2, flat, CMP_HIDDEN).astype(BF16), ck_b1.reshape(1, CMP_HIDDEN), pad_w2(ck_w2),
      cv_w1.reshape(2, flat, CMP_HIDDEN).astype(BF16), cv_b1.reshape(1, CMP_HIDDEN), pad_w2(cv_w2),
      rc, rsu, rsd)


def _per_head(x):
    return jnp.concatenate([x] * NSA_HPG, axis=1)


def _nsa_attn_kernel(seq, tk, qt_ref, gate_ref, kc_ref, vct_ref, ks_ref, vst_ref, kw_ref, vwt_ref, mmapt_ref,
                     o_ref, sel_ref):
    n_blk = seq // SLC_BLOCK
    n_sel = min(N_SLC, n_blk)
    hd = NSA_HEAD_DIM
    w = NSA_HPG * hd
    groups = range(NSA_KV_GROUPS)
    q0 = pl.program_id(1) * Q_BLOCK
    t_l = q0 + lax.broadcasted_iota(jnp.int32, (1, Q_BLOCK), 1)

    def select(g):
        qg = qt_ref[0, g * w:(g + 1) * w, :]
        qst = jnp.concatenate([qg[h * hd:(h + 1) * hd, :] for h in range(NSA_HPG)], axis=1)
        kc = kc_ref[0, g]
        n_cmp = kc.shape[0]
        cmp_end = lax.broadcasted_iota(jnp.int32, (n_cmp, 1), 0) * CMP_STRIDE + (CMP_BLOCK - 1)
        s = _dot(kc, qst) + _per_head(jnp.where(cmp_end <= t_l, 0.0, MASK_VALUE))
        yield
        p = jnp.exp2(s - jnp.max(s, axis=0, keepdims=True))
        any_key = _per_head(jnp.where(t_l >= CMP_BLOCK - 1, 1.0, 0.0))
        pv = _dot(vct_ref[0, g], p.astype(BF16))
        inv_l = any_key / pv[hd:hd + 1]
        o_cmp = pv[:hd] * inv_l
        yield
        p = p * inv_l
        imp = p[:, 0:Q_BLOCK]
        for h in range(1, NSA_HPG):
            imp = imp + p[:, h * Q_BLOCK:(h + 1) * Q_BLOCK]
        imp_slc = _dot_split(mmapt_ref[...], imp, 'b', 3)
        yield
        blk = lax.broadcasted_iota(jnp.int32, (n_blk, 1), 0)
        cur = t_l // SLC_BLOCK
        score = jnp.where(blk == 0, FORCE_SCORE,
                          jnp.where(blk == cur, FORCE_SCORE, jnp.where(blk == cur - 1, FORCE_SCORE, imp_slc)))
        score = jnp.where(blk * SLC_BLOCK <= t_l, score, -FORCE_SCORE)
        sel = jnp.zeros((n_blk, Q_BLOCK), F32)
        for _ in range(n_sel):
            best = jnp.max(score, axis=0, keepdims=True)
            idx = jnp.min(jnp.where(score == best, blk, n_blk), axis=0, keepdims=True)
            pick = blk == idx
            sel = jnp.where(pick, 1.0, sel)
            score = jnp.where(pick, -jnp.inf, score)
            yield
        sel_ref[g] = sel
        return qst, o_cmp

    selected = _run_interleaved([select(g) for g in groups])
    qst = [r[0] for r in selected]
    o_cmp = [r[1] for r in selected]

    blocks_per_tile = tk // SLC_BLOCK
    assert ks_ref.shape[3] == hd + blocks_per_tile and Q_BLOCK <= tk and tk % Q_BLOCK == 0

    def slc_tile(kt, carry, causal_bias=None):
        k0 = pl.multiple_of(kt * tk, tk)
        out = []
        for g in groups:
            m, acc = carry[g]
            sel_rows = sel_ref[g, pl.ds(pl.multiple_of(kt * blocks_per_tile, blocks_per_tile), blocks_per_tile), :]
            q_aug = jnp.concatenate([qst[g], _per_head((sel_rows - 1.0) * -MASK_VALUE).astype(BF16)], axis=0)
            s = _dot(ks_ref[0, g, pl.ds(k0, tk), :], q_aug)
            if causal_bias is not None:
                s = s + causal_bias
            m_new = jnp.maximum(m, jnp.max(s, axis=0, keepdims=True))
            alpha = jnp.exp2(m - m_new)
            p = jnp.exp2(s - m_new)
            acc = alpha * acc + _dot(vst_ref[0, g, :, pl.ds(k0, tk)], p.astype(BF16))
            out.append((m_new, acc))
        return tuple(out)

    n_full = q0 // tk
    cols = NSA_HPG * Q_BLOCK
    init = tuple((jnp.full((1, cols), MASK_VALUE, F32), jnp.zeros((NSA_VT_ROWS, cols), F32)) for _ in groups)
    slc = lax.fori_loop(0, n_full, slc_tile, init)
    kpos = n_full * tk + lax.broadcasted_iota(jnp.int32, (tk, 1), 0)
    slc = slc_tile(n_full, slc, _per_head(jnp.where(kpos <= t_l, 0.0, MASK_VALUE)))

    band = WINDOW + Q_BLOCK
    w0 = pl.multiple_of(jnp.maximum(q0 - WINDOW, 0), Q_BLOCK)
    kpos = w0 + lax.broadcasted_iota(jnp.int32, (band, 1), 0)
    win_bias = _per_head(jnp.where(kpos <= t_l, jnp.where(kpos > t_l - WINDOW, 0.0, MASK_VALUE), MASK_VALUE))
    sig_t = _sigmoid(gate_ref[...]).T

    def finish(g):
        s = _dot(kw_ref[0, g, pl.ds(w0, band), :], qst[g]) + win_bias
        yield
        p = jnp.exp2(s - jnp.max(s, axis=0, keepdims=True))
        pv = _dot(vwt_ref[0, g, :, pl.ds(w0, band)], p.astype(BF16))
        o_win = pv[:hd] / pv[hd:hd + 1]
        yield
        acc = slc[g][1]
        acc_slc, l_slc = acc[:hd], acc[hd:hd + 1]

        def gate(branch):
            first = (g * NSA_HPG) * 3 + branch
            return jnp.concatenate([sig_t[first + 3 * h:first + 3 * h + 1, :] for h in range(NSA_HPG)], axis=1)

        out_t = gate(0) * o_cmp[g] + gate(1) * (acc_slc / l_slc) + gate(2) * o_win
        pairs = []
        for h in range(0, NSA_HPG, 2):
            two = jnp.concatenate([out_t[:, h * Q_BLOCK:(h + 1) * Q_BLOCK],
                                   out_t[:, (h + 1) * Q_BLOCK:(h + 2) * Q_BLOCK]], axis=0)
            pairs.append(two.T)
        o_ref[:, g * w:(g + 1) * w] = jnp.concatenate(pairs, axis=1).astype(o_ref.dtype)

    _run_interleaved([finish(g) for g in groups])


def _nsa_pool_matrix(seq):
    n_blk = seq // SLC_BLOCK
    n_rows = seq // CMP_STRIDE
    per_stride = SLC_BLOCK // CMP_STRIDE
    span = CMP_BLOCK // CMP_STRIDE
    pool = np.zeros((n_blk, n_rows), np.float32)
    for j in range(n_blk):
        for m in range(per_stride):
            for n in range(span):
                c = per_stride * j + m + n - (span - 1)
                if 0 <= c < n_rows - 1:
                    pool[j, c] += 1.0
    return jnp.asarray(pool, BF16)


def nsa_attention(qt, h, gate_col_block, k_cmp, v_cmp_t, ks, vs_t, kw, vw_t, bsz, seq):
    tk = min(NSA_KEY_TILE, seq)
    nq = seq // Q_BLOCK
    pool = _nsa_pool_matrix(seq)
    n_rows = k_cmp.shape[2]
    d = NSA_HEAD_DIM
    qblk = lambda b, i: (b * nq + i, 0)
    whole = lambda *shape: pl.BlockSpec((1, NSA_KV_GROUPS) + shape, lambda b, i: (b, 0, 0, 0))
    kern = functools.partial(_nsa_attn_kernel, seq, tk)
    return pl.pallas_call(
        kern,
        grid=(bsz, nq),
        in_specs=[pl.BlockSpec((1, NSA_WIDTH, Q_BLOCK), lambda b, i: (b, 0, i)),
                  pl.BlockSpec((Q_BLOCK, LANES), lambda b, i: (b * nq + i, gate_col_block)),
                  whole(n_rows, d), whole(NSA_VT_ROWS, n_rows), whole(seq, ks.shape[3]), whole(NSA_VT_ROWS, seq),
                  whole(seq, d), whole(NSA_VT_ROWS, seq),
                  pl.BlockSpec(pool.shape, lambda b, i: (0, 0))],
        out_specs=pl.BlockSpec((Q_BLOCK, NSA_WIDTH), qblk),
        out_shape=jax.ShapeDtypeStruct((bsz * seq, NSA_WIDTH), BF16),
        scratch_shapes=[pltpu.VMEM((NSA_KV_GROUPS, seq // SLC_BLOCK, Q_BLOCK), F32)],
        compiler_params=_params("parallel", "arbitrary"),
        name="nsa_attention",
    )(qt, h, k_cmp, v_cmp_t, ks, vs_t, kw, vw_t, pool)


def nsa_mixer(h, bsz, seq, gate_col_block, pe_k, pe_v, ck_w1, ck_b1, ck_w2, cv_w1, cv_b1, cv_w2):
    qt, kc, vc, ks, vs_t, kw, vw_t = nsa_prep(h, bsz, seq)
    k_cmp, v_cmp_t = nsa_compress(kc, vc, pe_k, pe_v, ck_w1, ck_b1, ck_w2, cv_w1, cv_b1, cv_w2)
    return nsa_attention(qt, h, gate_col_block, k_cmp, v_cmp_t, ks, vs_t, kw, vw_t, bsz, seq)


def _head_ones(width, head_dim):
    idx = np.arange(width) // head_dim
    return jnp.asarray(idx[:, None] == idx[None, :], BF16)


def _softplus(x):
    return jnp.maximum(x, 0.0) + jnp.log(1.0 + jnp.exp(-jnp.abs(x)))


def _rwkv_pre_kernel(p_ref, prev_ref, mu_ref, w0_ref, wup_ref, a0_ref, aup_ref, gup_ref, kk_ref, ka_ref, rk_ref,
                     ones_ref, r_o, k_o, v_o, kk_o, b_o, ld_o, g_o, bonus_o):
    w = RWKV_WIDTH
    p = p_ref[...]
    first_row = jnp.where(pl.program_id(1) == 0, 0.0, prev_ref[7:8, :])
    is_row0 = lax.broadcasted_iota(jnp.int32, p.shape, 0) == 0
    prev = jnp.where(is_row0, first_row, pltpu.roll(p, 1, 0))
    ps = p + (prev - p) * mu_ref[...]
    r, k, v = ps[:, 0:w], ps[:, w:2 * w], ps[:, 2 * w:3 * w]
    o = 3 * w
    w_lo = ps[:, o:o + RWKV_LORA_W]
    a_lo = ps[:, o + RWKV_LORA_W:o + RWKV_LORA_W + RWKV_LORA_A]
    g_lo = ps[:, o + RWKV_LORA_W + RWKV_LORA_A:]
    wlog = -_softplus(-(w0_ref[...] + _dot(jnp.tanh(w_lo).astype(BF16), wup_ref[...]))) - 0.5
    a = _sigmoid(a0_ref[...] + _dot(a_lo.astype(BF16), aup_ref[...]))
    g = _dot(_sigmoid(g_lo).astype(BF16), gup_ref[...])
    kk = k * kk_ref[...]
    norm = jnp.sqrt(_dot_split(kk * kk, ones_ref[...], 'a', 2))
    kk = kk / jnp.maximum(norm, 1e-12)
    k2 = k * (1.0 + (a - 1.0) * ka_ref[...])
    r_o[...] = r
    k_o[...] = k2
    v_o[...] = v
    kk_o[...] = kk
    b_o[...] = kk * a
    ld_o[...] = -jnp.exp(wlog)
    g_o[...] = g
    bonus_o[...] = _dot_split(r * k2 * rk_ref[...], ones_ref[...], 'a', 2) * v


def rwkv_pre(h, bsz, seq, mu, w0, w_up, a0, a_up, g_up, k_k, k_a, r_k, tl=512):
    tl = min(tl, seq)
    nl = seq // tl
    w = RWKV_WIDTH
    cols = RWKV_COLS
    ones = _head_ones(w, RWKV_HEAD_DIM)
    f2 = lambda b, l: (0, 0)
    vec = pl.BlockSpec((1, w), f2)
    out_spec = pl.BlockSpec((tl, w), lambda b, l: (b * nl + l, 0))
    out_shape = jax.ShapeDtypeStruct((bsz * seq, w), F32)
    return pl.pallas_call(
        _rwkv_pre_kernel,
        grid=(bsz, nl),
        in_specs=[pl.BlockSpec((tl, cols), lambda b, l: (b * nl + l, 0)),
                  pl.BlockSpec((8, cols), lambda b, l: (jnp.maximum((b * seq + l * tl) // 8 - 1, 0), 0)),
                  pl.BlockSpec((1, cols), f2), vec, pl.BlockSpec((RWKV_LORA_W, w), f2),
                  vec, pl.BlockSpec((RWKV_LORA_A, w), f2), pl.BlockSpec((RWKV_LORA_G, w), f2),
                  vec, vec, vec, pl.BlockSpec((w, w), f2)],
        out_specs=[out_spec] * 8,
        out_shape=[out_shape] * 8,
        compiler_params=_params("parallel", "parallel"),
        name="rwkv_pre",
    )(h, h, mu.reshape(1, cols), w0.reshape(1, w), w_up.astype(BF16), a0.reshape(1, w), a_up.astype(BF16),
      g_up.astype(BF16), k_k.reshape(1, w), k_a.reshape(1, w), r_k.reshape(1, w), ones)


def _rwkv_masks():
    t, pk = RWKV_CHUNK, RWKV_PACK
    n = t * pk
    ri = np.arange(n)
    same = (ri[:, None] // t) == (ri[None, :] // t)
    tt, ss = ri[:, None] % t, ri[None, :] % t
    levels = []
    k = 1
    while k < t:
        levels.append(same & (tt // (2 * k) == ss // (2 * k)) & ((tt // k) % 2 == 1) & ((ss // k) % 2 == 0))
        k *= 2
    lvl = np.stack(levels).astype(np.float32)
    tri = (np.arange(t)[:, None] >= np.arange(t)[None, :]).astype(np.float32)
    head_lane = ((ri[:, None] // t) == (np.arange(pk * RWKV_HEAD_DIM)[None, :] // RWKV_HEAD_DIM)).astype(np.float32)
    return (jnp.asarray(tri, BF16), jnp.asarray(head_lane), jnp.asarray(same.astype(np.float32)), jnp.asarray(lvl))


def _rwkv_chain(r, k, v, kk, b, ld, st, tri, hl, bd, lvl_ref):
    t, pk = RWKV_CHUNK, RWKV_PACK
    n = t * pk
    c = _dot_split(tri, ld, 'b', 3)
    yield
    c_end = c[t - 1:t, :]
    e_neg = jnp.exp(-c)
    e_end = jnp.exp(c_end - c)
    kkd = (kk * jnp.exp(c - ld)).astype(BF16)
    rd = (r * jnp.exp(c)).astype(BF16)

    def big(x):
        return (jnp.concatenate([x] * pk, axis=0) * hl).astype(BF16)

    st_b = st.astype(BF16)
    v_big = big(v)
    a_all = _dot_nt(jnp.concatenate([kkd, rd], axis=0),
                    jnp.concatenate([big(k * e_neg), big(b * e_neg)], axis=0))
    yield
    ti = lax.broadcasted_iota(jnp.int32, (t, n), 0)
    si = lax.broadcasted_iota(jnp.int32, (t, n), 1) % t
    strict = ti > si
    incl = ti >= si
    a_kk = jnp.where(strict, a_all[:t, :n], 0.0)
    a_kb = jnp.where(strict, a_all[:t, n:], 0.0)
    a_rk = jnp.where(incl, a_all[t:, :n], 0.0)
    a_rb = jnp.where(incl, a_all[t:, n:], 0.0)
    rhs = _dot(kkd, st_b) + _dot(a_kk.astype(BF16), v_big)
    yield
    a_bd = jnp.concatenate([a_kb] * pk, axis=0) * bd
    m = jnp.where(lax.broadcasted_iota(jnp.int32, (n, n), 0) == lax.broadcasted_iota(jnp.int32, (n, n), 1), 1.0, 0.0)
    for lv in range(lvl_ref.shape[0]):
        mb = m.astype(BF16)
        ma = _dot(mb, (a_bd * lvl_ref[lv]).astype(BF16)).astype(BF16)
        yield
        m = m - _dot(ma, mb)
        yield
    u_big = _dot(m.astype(BF16), big(rhs))
    yield
    u = u_big[0:t]
    for h in range(1, pk):
        u = u + u_big[h * t:(h + 1) * t]
    y = _dot(rd, st_b) + _dot(a_rk.astype(BF16), v_big) - _dot(a_rb.astype(BF16), big(u))
    yield
    decay_col = jnp.broadcast_to(jnp.exp(c_end), st.shape).T
    kb_end = jnp.concatenate([k * e_end, -(b * e_end)], axis=0).astype(BF16)
    vu = jnp.concatenate([v, u], axis=0).astype(BF16)
    return y, decay_col * st + bd * _dot_tn(kb_end, vu)


def _rwkv_chunk_kernel(r_ref, k_ref, v_ref, kk_ref, b_ref, ld_ref, tri_ref, hl_ref, bd_ref, lvl_ref, y_ref, st_ref):
    @pl.when(pl.program_id(0) == 0)
    def _():
        st_ref[...] = jnp.zeros_like(st_ref)

    wp = RWKV_PACK * RWKV_HEAD_DIM
    tri, hl, bd = tri_ref[...], hl_ref[...], bd_ref[...]
    n_pack = r_ref.shape[2] // wp
    where = [(bi, slice(g * wp, (g + 1) * wp)) for bi in range(r_ref.shape[0]) for g in range(n_pack)]
    loaded = [tuple(ref[bi, :, cols] for ref in (r_ref, k_ref, v_ref, kk_ref, b_ref, ld_ref)) + (st_ref[i],)
              for i, (bi, cols) in enumerate(where)]
    results = _run_interleaved([_rwkv_chain(*args, tri, hl, bd, lvl_ref) for args in loaded])
    for i, ((bi, cols), (y, st_new)) in enumerate(zip(where, results)):
        y_ref[bi, :, cols] = y
        st_ref[i] = st_new


def rwkv_chunk(r, k, v, kk, b, ld, bsz, seq):
    t, pk = RWKV_CHUNK, RWKV_PACK
    n_chunk = seq // t
    w = RWKV_WIDTH
    wp = pk * RWKV_HEAD_DIM
    assert t == RWKV_HEAD_DIM
    tri, hl, bd, lvl = _rwkv_masks()
    blk = pl.BlockSpec((bsz, t, w), lambda c: (0, c, 0))
    f2 = lambda c: (0, 0)
    shaped = lambda a: a.reshape(bsz, seq, w)
    y = pl.pallas_call(
        _rwkv_chunk_kernel,
        grid=(n_chunk,),
        in_specs=[blk] * 6 + [pl.BlockSpec(tri.shape, f2), pl.BlockSpec(hl.shape, f2), pl.BlockSpec(bd.shape, f2),
                              pl.BlockSpec(lvl.shape, lambda c: (0, 0, 0))],
        out_specs=blk,
        out_shape=jax.ShapeDtypeStruct((bsz, seq, w), F32),
        scratch_shapes=[pltpu.VMEM((bsz * (w // wp), wp, wp), F32)],
        compiler_params=_params("arbitrary"),
        name="rwkv_chunk",
    )(shaped(r), shaped(k), shaped(v), shaped(kk), shaped(b), shaped(ld), tri, hl, bd, lvl)
    return y.reshape(bsz * seq, w)


def _rwkv_post_kernel(y_ref, bonus_ref, g_ref, lng_ref, lnb_ref, ones_ref, o_ref):
    y = y_ref[...]
    inv = 1.0 / RWKV_HEAD_DIM
    mu = _dot_split(y, ones_ref[...], 'a', 2) * inv
    yc = y - mu
    var = _dot_split(yc * yc, ones_ref[...], 'a', 2) * inv
    yn = yc * lax.rsqrt(var + RWKV_GN_EPS) * lng_ref[...] + lnb_ref[...]
    o_ref[...] = ((yn + bonus_ref[...]) * g_ref[...]).astype(o_ref.dtype)


def rwkv_post(y, bonus, g, ln_g, ln_b, tm=1024):
    n, w = y.shape
    tm = min(tm, n)
    row = pl.BlockSpec((tm, w), lambda i: (i, 0))
    vec = pl.BlockSpec((1, w), lambda i: (0, 0))
    return pl.pallas_call(
        _rwkv_post_kernel,
        grid=(n // tm,),
        in_specs=[row, row, row, vec, vec, pl.BlockSpec((w, w), lambda i: (0, 0))],
        out_specs=row,
        out_shape=jax.ShapeDtypeStruct((n, w), BF16),
        compiler_params=_params("parallel"),
        name="rwkv_post",
    )(y, bonus, g, ln_g.reshape(1, w), ln_b.reshape(1, w), _head_ones(w, RWKV_HEAD_DIM))


def rwkv7_mixer(h, bsz, seq, mu, w0, w_up, a0, a_up, g_up, k_k, k_a, r_k, ln_g, ln_b):
    r, k, v, kk, b, ld, g, bonus = rwkv_pre(h, bsz, seq, mu, w0, w_up, a0, a_up, g_up, k_k, k_a, r_k)
    y = rwkv_chunk(r, k, v, kk, b, ld, bsz, seq)
    return rwkv_post(y, bonus, g, ln_g, ln_b)


AB_IN = S5_WIDTH + NSA_WIDTH + 6 * NSA_KV_W + NSA_GATE_COLS
AB_IN_PADDED = -(-AB_IN // LANES) * LANES
NSA_GATE_COL_BLOCK = (AB_IN - NSA_GATE_COLS) // LANES
PROJ_TM = 512


def kernel(x, ab_w_in, ab_w_out, s5_lam_re, s5_lam_im, s5_log_dt, s5_b_re, s5_b_im, s5_c_re, s5_c_im, s5_d, s5_w_glu, s5_b_glu, nsa_pe_k, nsa_pe_v, nsa_ck_w1, nsa_ck_b1, nsa_ck_w2, nsa_cv_w1, nsa_cv_b1, nsa_cv_w2, cd_w_in, cd_w_out, rwkv_mu, rwkv_w0, rwkv_w_up, rwkv_a0, rwkv_a_up, rwkv_g_up, rwkv_k_k, rwkv_k_a, rwkv_r_k, rwkv_ln_g, rwkv_ln_b, ret_ln_g, ret_ln_b, ln1_g, ln1_b, ln2_g, ln2_b, moe_router, moe_bias, moe_w1, moe_w3, moe_w2, sh_w1, sh_w3, sh_w2):
    bsz, seq, d = x.shape
    assert (AB_IN - NSA_GATE_COLS) % LANES == 0
    xf = x.reshape(bsz * seq, d)
    x_in = xf
    for layer in range(DEPTH):
        i = layer // 2
        if layer % 2 == 0:
            w_in = jnp.pad(ab_w_in[i], ((0, 0), (0, AB_IN_PADDED - AB_IN))).astype(BF16)
            h, u3 = project(x_in, w_in, PROJ_TM, chunked=(S5_CHUNK, S5_WIDTH))
            y_1 = s5_mixer(h, u3, bsz, seq, s5_lam_re[i], s5_lam_im[i], s5_log_dt[i], s5_b_re[i], s5_b_im[i],
                           s5_c_re[i], s5_c_im[i], s5_d[i], s5_w_glu[i], s5_b_glu[i])
            y_2 = nsa_mixer(h, bsz, seq, NSA_GATE_COL_BLOCK, nsa_pe_k[i], nsa_pe_v[i], nsa_ck_w1[i], nsa_ck_b1[i],
                            nsa_ck_w2[i], nsa_cv_w1[i], nsa_cv_b1[i], nsa_cv_w2[i])
            w_out = ab_w_out[i]
        else:
            h = project(x_in, cd_w_in[i].astype(BF16), PROJ_TM)
            y_1 = rwkv7_mixer(h, bsz, seq, rwkv_mu[i], rwkv_w0[i], rwkv_w_up[i], rwkv_a0[i], rwkv_a_up[i],
                              rwkv_g_up[i], rwkv_k_k[i], rwkv_k_a[i], rwkv_r_k[i], rwkv_ln_g[i], rwkv_ln_b[i])
            y_2 = retention_mixer(h, bsz, seq, RWKV_COLS, ret_ln_g[i], ret_ln_b[i])
            w_out = cd_w_out[i]
        xf = out_proj_ln(y_1, y_2, w_out, xf, ln1_g[layer], ln1_b[layer])
        xf, x_in = moe_block(xf, moe_router[layer], moe_bias[layer], moe_w1[layer], moe_w3[layer],
                             moe_w2, layer, sh_w1[layer], sh_w3[layer], sh_w2[layer], ln2_g[layer], ln2_b[layer])
    return xf.reshape(bsz, seq, d)
```

```python
import functools
import math

import jax
import jax.numpy as jnp
import numpy as np
from jax import lax
from jax.experimental import pallas as pl
from jax.experimental.pallas import tpu as pltpu

F32 = jnp.float32
BF16 = jnp.bfloat16
HIGHEST = lax.Precision.HIGHEST
FP8 = jnp.float8_e4m3fn
FP8_MAX = 448.0
FP8_TINY = 1e-30

VMEM_LIMIT_BYTES = 52 * 1024 * 1024
LANES = 128

LN_EPS = 1e-5
DEPTH = 2
ALPHA = (2 * DEPTH) ** 0.25

S5_GROUPS, S5_GROUP_CH, S5_STATE = 32, 16, 64
S5_WIDTH = S5_GROUPS * S5_GROUP_CH
S5_CHUNK = 16
S5_PACK = 8
NSA_HEADS, NSA_KV_GROUPS, NSA_HEAD_DIM = 8, 2, 64
NSA_HPG = NSA_HEADS // NSA_KV_GROUPS
NSA_WIDTH = NSA_HEADS * NSA_HEAD_DIM
NSA_ROT_DIM = NSA_HEAD_DIM // 4
ROPE_THETA = 500000.0
CMP_BLOCK, CMP_STRIDE, CMP_HIDDEN = 32, 16, 128
SLC_BLOCK, N_SLC, WINDOW = 64, 16, 512
Q_BLOCK = 256
NSA_KEY_TILE = 1024
FORCE_SCORE = 1e6
MASK_VALUE = -1e30
RWKV_HEADS, RWKV_HEAD_DIM = 8, 64
RWKV_WIDTH = RWKV_HEADS * RWKV_HEAD_DIM
RWKV_LORA_W, RWKV_LORA_A, RWKV_LORA_G = 64, 64, 128
RWKV_COLS = 3 * RWKV_WIDTH + RWKV_LORA_W + RWKV_LORA_A + RWKV_LORA_G
RWKV_GN_EPS = 64e-5
RWKV_CHUNK = 64
RWKV_PACK = 4
RET_HEADS, RET_DK, RET_DV, RET_CHUNK = 4, 64, 128, 128
RET_THETA = 10000.0
RET_GN_EPS = 1e-5
N_EXPERTS, TOP_K, EXPERT_FF = 64, 8, 256
N_EXPERT_GROUPS, TOPK_GROUPS = 8, 4
EXPERTS_PER_GROUP = N_EXPERTS // N_EXPERT_GROUPS
ROUTED_SCALE = 2.5
MOE_EXPERTS_PER_STEP = 4


def _params(*sem):
    return pltpu.CompilerParams(dimension_semantics=sem, vmem_limit_bytes=VMEM_LIMIT_BYTES)


def _dot(a, b, **kw):
    return jnp.dot(a, b, preferred_element_type=F32, **kw)


def _dot_nt(a, b, **kw):
    return lax.dot_general(a, b, (((1,), (1,)), ((), ())), preferred_element_type=F32, **kw)


def _dot_tn(a, b, **kw):
    return lax.dot_general(a, b, (((0,), (0,)), ((), ())), preferred_element_type=F32, **kw)


def _dot_split(a, b, split, parts):
    rest = a if split == 'a' else b
    acc = None
    for _ in range(parts):
        piece = rest.astype(BF16)
        term = _dot(piece, b) if split == 'a' else _dot(a, piece)
        acc = term if acc is None else acc + term
        rest = rest - piece.astype(F32)
    return acc


def _run_interleaved(gens):
    results = [None] * len(gens)
    live = list(range(len(gens)))
    while live:
        for i in list(live):
            try:
                next(gens[i])
            except StopIteration as done:
                results[i] = done.value
                live.remove(i)
    return results


def _gelu(x):
    return 0.5 * x * (1.0 + jnp.tanh(math.sqrt(2.0 / math.pi) * (x + 0.044715 * (x * x * x))))


def _sigmoid(x):
    return 1.0 / (1.0 + jnp.exp(-x))


def _layer_norm_rows(z, g, b):
    mu = jnp.mean(z, axis=-1, keepdims=True)
    zc = z - mu
    var = jnp.mean(zc * zc, axis=-1, keepdims=True)
    return zc * lax.rsqrt(var + LN_EPS) * g + b


def _proj_kernel(x_ref, w_ref, o_ref, *chunked_ref):
    y = _dot(x_ref[...].astype(BF16), w_ref[...])
    o_ref[...] = y
    for c_ref in chunked_ref:
        rows, t, w = c_ref.shape
        c_ref[...] = y[:, :w].reshape(rows, t, w)


def project(x, w_bf16, tm, chunked=None):
    m, k = x.shape
    n = w_bf16.shape[1]
    out_specs = [pl.BlockSpec((tm, n), lambda i: (i, 0))]
    out_shape = [jax.ShapeDtypeStruct((m, n), F32)]
    if chunked is not None:
        t, w = chunked
        out_specs.append(pl.BlockSpec((tm // t, t, w), lambda i: (i, 0, 0)))
        out_shape.append(jax.ShapeDtypeStruct((m // t, t, w), F32))
    out = pl.pallas_call(
        _proj_kernel,
        grid=(m // tm,),
        in_specs=[pl.BlockSpec((tm, k), lambda i: (i, 0)), pl.BlockSpec((k, n), lambda i: (0, 0))],
        out_specs=out_specs,
        out_shape=out_shape,
        compiler_params=_params("parallel"),
        name="project",
    )(x, w_bf16)
    return out if chunked is not None else out[0]


def _out_proj_ln_kernel(ya_ref, yb_ref, wa_ref, wb_ref, x_ref, g_ref, b_ref, rt_ref, rbias_ref, o_ref, gate_ref):
    mix = _dot(ya_ref[...], wa_ref[...]) + _dot(yb_ref[...], wb_ref[...])
    y = _layer_norm_rows(ALPHA * x_ref[...] + mix, g_ref[...], b_ref[...])
    o_ref[...] = y
    gate_ref[...] = _route(y, rt_ref[...], rbias_ref[...])


def out_proj_ln_route(ya, yb, w_out, x, g, b, router, router_bias, tm=512):
    n, d = x.shape
    ka, kb = ya.shape[1], yb.shape[1]
    wa = w_out[:ka].astype(BF16)
    wb = w_out[ka:].astype(BF16)
    row = lambda i: (i, 0)
    fixed = lambda i: (0, 0)
    return pl.pallas_call(
        _out_proj_ln_kernel,
        grid=(n // tm,),
        in_specs=[pl.BlockSpec((tm, ka), row), pl.BlockSpec((tm, kb), row),
                  pl.BlockSpec((ka, d), fixed), pl.BlockSpec((kb, d), fixed),
                  pl.BlockSpec((tm, d), row), pl.BlockSpec((1, d), fixed), pl.BlockSpec((1, d), fixed),
                  pl.BlockSpec((N_EXPERTS, d), fixed), pl.BlockSpec((N_EXPERTS, 1), fixed)],
        out_specs=[pl.BlockSpec((tm, d), row), pl.BlockSpec((tm, LANES), row)],
        out_shape=[jax.ShapeDtypeStruct((n, d), F32), jax.ShapeDtypeStruct((n, LANES), F32)],
        compiler_params=_params("parallel"),
        name="out_proj_ln_route",
    )(ya, yb, wa, wb, x, g.reshape(1, d), b.reshape(1, d), router.T, router_bias.reshape(N_EXPERTS, 1))


def _route(x, rt, bias):
    tr = x.shape[0]
    scores = _sigmoid(_dot_nt(rt, x, precision=HIGHEST))
    biased = scores + bias
    grp = biased.reshape(N_EXPERT_GROUPS, EXPERTS_PER_GROUP, tr)
    pos = lax.broadcasted_iota(jnp.int32, grp.shape, 1)
    m1 = jnp.max(grp, axis=1, keepdims=True)
    first = jnp.min(jnp.where(grp == m1, pos, EXPERTS_PER_GROUP), axis=1, keepdims=True)
    m2 = jnp.max(jnp.where(pos == first, -jnp.inf, grp), axis=1, keepdims=True)
    gscore = (m1 + m2).reshape(N_EXPERT_GROUPS, tr)
    gidx = lax.broadcasted_iota(jnp.int32, gscore.shape, 0)
    grank = jnp.zeros(gscore.shape, F32)
    for j in range(N_EXPERT_GROUPS):
        row = gscore[j:j + 1, :]
        grank = grank + jnp.where(gidx > j, jnp.where(row >= gscore, 1.0, 0.0), jnp.where(row > gscore, 1.0, 0.0))
    gkeep = jnp.where(grank < TOPK_GROUPS, 1.0, 0.0)
    keep = jnp.broadcast_to(gkeep[:, None, :], grp.shape).reshape(N_EXPERTS, tr)
    masked = jnp.where(keep > 0.5, biased, -jnp.inf)
    eidx = lax.broadcasted_iota(jnp.int32, masked.shape, 0)
    rank = jnp.zeros(masked.shape, F32)
    for j in range(N_EXPERTS):
        row = masked[j:j + 1, :]
        rank = rank + jnp.where(eidx > j, jnp.where(row >= masked, 1.0, 0.0), jnp.where(row > masked, 1.0, 0.0))
    gate = jnp.where(rank < TOP_K, scores, 0.0)
    gate = gate / jnp.sum(gate, axis=0, keepdims=True) * ROUTED_SCALE
    return jnp.concatenate([gate, jnp.zeros((LANES - N_EXPERTS, tr), F32)], axis=0).T


def _quantize_fp8(a, axes):
    amax = jnp.max(jnp.abs(a), axis=axes, keepdims=True)
    scale = jnp.maximum(amax, FP8_TINY) * (1.0 / FP8_MAX)
    return (a * (1.0 / scale)).astype(FP8), scale


def _swiglu_hidden(xq, x_scale, w1q, w3q, w_scale, gate=None):
    col1 = x_scale * w_scale[0:1, 0:1]
    col3 = x_scale * w_scale[1:2, 0:1]
    if gate is not None:
        col3 = col3 * gate
    h1 = _dot(xq, w1q) * col1
    return h1 * _sigmoid(h1) * (_dot(xq, w3q) * col3)


def _experts_ln_kernel(x_ref, gate_ref, w1_ref, w3_ref, ws_ref, w2_ref, sw1_ref, sw3_ref, sws_ref, sw2_ref,
                       g_ref, b_ref, o_ref, obf_ref, acc_ref, xq_ref, xs_ref):
    step = pl.program_id(1)
    per_step = w1_ref.shape[0]

    @pl.when(step == 0)
    def _():
        xq, xs = _quantize_fp8(x_ref[...], (1,))
        xq_ref[...] = xq
        xs_ref[...] = xs
        h = _swiglu_hidden(xq, xs, sw1_ref[...], sw3_ref[...], sws_ref[...])
        acc_ref[...] = _dot(h.astype(BF16), sw2_ref[...].astype(BF16))

    lane = lax.broadcasted_iota(jnp.int32, gate_ref.shape, 1)
    gates = gate_ref[...]
    xq, xs = xq_ref[...], xs_ref[...]
    hidden = []
    for j in range(per_step):
        gcol = jnp.sum(jnp.where(lane == step * per_step + j, gates, 0.0), axis=1, keepdims=True)
        hidden.append(_swiglu_hidden(xq, xs, w1_ref[j], w3_ref[j], ws_ref[j], gcol).astype(BF16))
    w2 = w2_ref[0].astype(BF16)
    acc_ref[...] += _dot(jnp.concatenate(hidden, axis=1), w2.reshape(per_step * w2.shape[1], w2.shape[2]))

    @pl.when(step == pl.num_programs(1) - 1)
    def _():
        y = _layer_norm_rows(ALPHA * x_ref[...] + acc_ref[...], g_ref[...], b_ref[...])
        o_ref[...] = y
        obf_ref[...] = y.astype(BF16)


def _quantize_expert_weights(w1, w3):
    w1q, s1 = _quantize_fp8(w1, (-2, -1))
    w3q, s3 = _quantize_fp8(w3, (-2, -1))
    scales = jnp.broadcast_to(jnp.concatenate([s1, s3], axis=-2), s1.shape[:-2] + (2, w1.shape[-1]))
    return w1q, w3q, scales


def moe_experts_ln(x, gates, w1, w3, w2_layers, layer, sw1, sw3, sw2, g, b, tm=1024):
    n, d = x.shape
    ne = w1.shape[0]
    w1q, w3q, ws = _quantize_expert_weights(w1, w3)
    sw1q, sw3q, sws = _quantize_expert_weights(sw1, sw3)
    tok = lambda i, e: (i, 0)
    fixed = lambda i, e: (0, 0)
    per_expert = lambda *blk: pl.BlockSpec((MOE_EXPERTS_PER_STEP,) + blk, lambda i, e: (e, 0, 0))
    return pl.pallas_call(
        _experts_ln_kernel,
        grid=(n // tm, ne // MOE_EXPERTS_PER_STEP),
        in_specs=[pl.BlockSpec((tm, d), tok), pl.BlockSpec((tm, LANES), tok),
                  per_expert(d, EXPERT_FF), per_expert(d, EXPERT_FF), per_expert(2, EXPERT_FF),
                  pl.BlockSpec((1, MOE_EXPERTS_PER_STEP, EXPERT_FF, d), lambda i, e: (layer, e, 0, 0)),
                  pl.BlockSpec((d, EXPERT_FF), fixed), pl.BlockSpec((d, EXPERT_FF), fixed),
                  pl.BlockSpec((2, EXPERT_FF), fixed), pl.BlockSpec((EXPERT_FF, d), fixed),
                  pl.BlockSpec((1, d), fixed), pl.BlockSpec((1, d), fixed)],
        out_specs=[pl.BlockSpec((tm, d), tok), pl.BlockSpec((tm, d), tok)],
        out_shape=[jax.ShapeDtypeStruct((n, d), F32), jax.ShapeDtypeStruct((n, d), BF16)],
        scratch_shapes=[pltpu.VMEM((tm, d), F32), pltpu.VMEM((tm, d), FP8), pltpu.VMEM((tm, 1), F32)],
        compiler_params=_params("parallel", "arbitrary"),
        name="moe_experts_ln",
    )(x, gates, w1q, w3q, ws, w2_layers, sw1q, sw3q, sws, sw2, g.reshape(1, d), b.reshape(1, d))


def _s5_tables(lam_re, lam_im, log_dt, b_re, b_im, c_re, c_im, n_chunk):
    t, h, p = S5_CHUNK, S5_GROUP_CH, S5_STATE
    dt = jnp.exp(log_dt.astype(F32))[:, None]
    den = lam_re ** 2 + lam_im ** 2

    def lam_pow(k):
        k = jnp.asarray(k, F32)[..., None, None]
        mag = jnp.exp(lam_re * dt * k)
        return mag * jnp.cos(lam_im * dt * k), mag * jnp.sin(lam_im * dt * k)

    lb_re, lb_im = lam_pow(1.0)
    f_re = ((lb_re - 1.0) * lam_re + lb_im * lam_im) / den
    f_im = (lb_im * lam_re - (lb_re - 1.0) * lam_im) / den
    bb_re = f_re[..., None] * b_re - f_im[..., None] * b_im
    bb_im = f_re[..., None] * b_im + f_im[..., None] * b_re
    pr, pi = lam_pow(jnp.arange(t))
    cl_re = c_re[None] * pr[:, :, None, :] - c_im[None] * pi[:, :, None, :]
    cl_im = c_re[None] * pi[:, :, None, :] + c_im[None] * pr[:, :, None, :]
    klag = jnp.einsum('tgop,gpi->tgoi', cl_re, bb_re) - jnp.einsum('tgop,gpi->tgoi', cl_im, bb_im)
    nb = S5_GROUPS // S5_PACK
    split = lambda a, axis: a.reshape(a.shape[:axis] + (nb, S5_PACK) + a.shape[axis + 1:])
    eye = jnp.eye(S5_PACK, dtype=F32)
    lag_t = jnp.transpose(split(klag, 1), (1, 0, 2, 4, 3))
    lag_t = (lag_t[:, :, :, :, None, :] * eye[None, None, :, None, :, None]).reshape(nb, t, LANES, LANES)
    qr, qi = lam_pow(t - 1 - jnp.arange(t))
    st_re = qr[..., None] * bb_re[None] - qi[..., None] * bb_im[None]
    st_im = qr[..., None] * bb_im[None] + qi[..., None] * bb_re[None]
    st = jnp.stack([st_re, st_im], axis=0)
    st_t = jnp.transpose(split(st, 2), (2, 1, 3, 5, 0, 4)).reshape(nb, t, LANES, 2 * p)
    er, ei = lam_pow(jnp.arange(t) + 1)
    x_re = c_re[None] * er[:, :, None, :] - c_im[None] * ei[:, :, None, :]
    x_im = c_re[None] * ei[:, :, None, :] + c_im[None] * er[:, :, None, :]
    cr = jnp.stack([x_re, -x_im], axis=0)
    cr_t = jnp.transpose(split(cr, 2), (2, 1, 0, 5, 3, 4)).reshape(nb, t, 2 * p, LANES)
    levels = max(1, int(math.log2(n_chunk)))
    sr, si = lam_pow(t * (2.0 ** jnp.arange(levels)))
    sr = sr.reshape(levels, nb, S5_PACK * p)
    si = si.reshape(levels, nb, S5_PACK * p)
    a1 = jnp.concatenate([sr, sr], axis=-1)
    a2 = jnp.concatenate([-si, si], axis=-1)
    scan = jnp.transpose(jnp.stack([a1, a2], axis=1), (2, 0, 1, 3))
    return lag_t, st_t, cr_t, scan.astype(F32)


def _s5_build_tables(lag_ref, st_ref, cr_ref, wtoe_ref, wstate_ref, wcross_ref):
    t = lag_ref.shape[1]
    p = S5_STATE
    kp = S5_PACK * p
    wtoe_ref[...] = jnp.zeros_like(wtoe_ref)
    for d in range(t):
        tile = lag_ref[0, d].astype(BF16)
        for j in range(t - d):
            wtoe_ref[j * LANES:(j + 1) * LANES, (j + d) * LANES:(j + d + 1) * LANES] = tile
    lane = lax.broadcasted_iota(jnp.int32, (LANES, LANES), 1)
    row_g = lax.broadcasted_iota(jnp.int32, (LANES, kp), 0) // S5_GROUP_CH
    same_s = row_g == lax.broadcasted_iota(jnp.int32, (LANES, kp), 1) // p
    for j in range(t):
        a = st_ref[0, j]
        swapped = pltpu.roll(a, p, 1)
        for c, both in enumerate((jnp.where(lane < p, a, swapped), jnp.where(lane < p, swapped, a))):
            wide = jnp.concatenate([both] * (kp // LANES), axis=1)
            wstate_ref[j * LANES:(j + 1) * LANES, c * kp:(c + 1) * kp] = jnp.where(same_s, wide, 0.0).astype(BF16)
    same_c = (lax.broadcasted_iota(jnp.int32, (kp, LANES), 0) // p
              == lax.broadcasted_iota(jnp.int32, (kp, LANES), 1) // S5_GROUP_CH)
    for i in range(t):
        a = cr_ref[0, i]
        for c in range(2):
            tall = jnp.concatenate([a[c * p:(c + 1) * p]] * S5_PACK, axis=0)
            wcross_ref[c * kp:(c + 1) * kp, i * LANES:(i + 1) * LANES] = jnp.where(same_c, tall, 0.0).astype(BF16)


def _s5_kernel(u_ref, lag_ref, st_ref, cr_ref, scan_ref, o_ref, wtoe_ref, wstate_ref, wcross_ref):
    @pl.when(pl.program_id(1) == 0)
    def _():
        _s5_build_tables(lag_ref, st_ref, cr_ref, wtoe_ref, wstate_ref, wcross_ref)

    n_chunk, t, _ = u_ref.shape
    x = jnp.concatenate([u_ref[:, j, :] for j in range(t)], axis=1).astype(BF16)
    local = _dot(x, wtoe_ref[...])
    state = _dot(x, wstate_ref[...])
    row = lax.broadcasted_iota(jnp.int32, state.shape, 0)
    s = jnp.where(row >= 1, pltpu.roll(state, 1, 0), 0.0)
    half = state.shape[1] // 2
    level = 0
    d = 1
    while d < n_chunk:
        mult = scan_ref[0, level]
        prev = jnp.where(row >= d, pltpu.roll(s, d, 0), 0.0)
        s = s + mult[0:1, :] * prev + mult[1:2, :] * pltpu.roll(prev, half, 1)
        d *= 2
        level += 1
    y = local + _dot(s.astype(BF16), wcross_ref[...])
    for i in range(t):
        o_ref[:, i, :] = y[:, i * LANES:(i + 1) * LANES]


def s5_scan(u3, bsz, lag_t, st_t, cr_t, scan):
    rows, t, w = u3.shape
    n_chunk = rows // bsz
    kp2 = 2 * S5_PACK * S5_STATE
    table = lambda a: pl.BlockSpec((1,) + a.shape[1:], lambda j, b: (j, 0, 0, 0))
    return pl.pallas_call(
        _s5_kernel,
        grid=(w // LANES, bsz),
        in_specs=[pl.BlockSpec((n_chunk, t, LANES), lambda j, b: (b, 0, j)),
                  table(lag_t), table(st_t), table(cr_t), table(scan)],
        out_specs=pl.BlockSpec((n_chunk, t, LANES), lambda j, b: (b, 0, j)),
        out_shape=jax.ShapeDtypeStruct(u3.shape, F32),
        scratch_shapes=[pltpu.VMEM((t * LANES, t * LANES), BF16), pltpu.VMEM((t * LANES, kp2), BF16),
                        pltpu.VMEM((kp2, t * LANES), BF16)],
        compiler_params=_params("arbitrary", "arbitrary"),
        name="s5_scan",
    )(u3, lag_t, st_t, cr_t, scan)


def _s5_post_kernel(y_ref, u_ref, d_ref, w_ref, b_ref, o_ref):
    u = u_ref[...]
    y = _gelu(y_ref[...].reshape(u.shape) + d_ref[...] * u)
    o_ref[...] = (y * _sigmoid(_dot(y.astype(BF16), w_ref[...]) + b_ref[...])).astype(o_ref.dtype)


def s5_post(y3, h, d_skip, w_glu, b_glu, tm=1024):
    rows, t, w = y3.shape
    n = rows * t
    tm = min(tm, n)
    row = lambda i: (i, 0)
    fixed = lambda i: (0, 0)
    return pl.pallas_call(
        _s5_post_kernel,
        grid=(n // tm,),
        in_specs=[pl.BlockSpec((tm // t, t, w), lambda i: (i, 0, 0)), pl.BlockSpec((tm, w), row),
                  pl.BlockSpec((1, w), fixed), pl.BlockSpec((w, w), fixed), pl.BlockSpec((1, w), fixed)],
        out_specs=pl.BlockSpec((tm, w), row),
        out_shape=jax.ShapeDtypeStruct((n, w), BF16),
        compiler_params=_params("parallel"),
        name="s5_post",
    )(y3, h, d_skip.reshape(1, w), w_glu.astype(BF16), b_glu.reshape(1, w))


def s5_mixer(h, u3, bsz, seq, lam_re, lam_im, log_dt, b_re, b_im, c_re, c_im, d_skip, w_glu, b_glu):
    tables = _s5_tables(lam_re, lam_im, log_dt, b_re, b_im, c_re, c_im, seq // S5_CHUNK)
    return s5_post(s5_scan(u3, bsz, *tables), h, d_skip, w_glu, b_glu)


def _rope_tables(pos, rot_dim, theta, head_dim, n_heads):
    half = rot_dim // 2
    f32 = np.float32
    inv_freq = f32(theta) ** (-np.arange(half, dtype=f32) / f32(half))
    ang = (pos.astype(f32)[:, None] * inv_freq[None, :]).astype(np.float64)
    cos, sin = np.cos(ang), np.sin(ang)
    rest = head_dim - rot_dim
    n = pos.shape[0]
    c = np.concatenate([cos, cos, np.ones((n, rest))], axis=1)
    s_up = np.concatenate([-sin, np.zeros((n, half + rest))], axis=1)
    s_dn = np.concatenate([np.zeros((n, half)), sin, np.zeros((n, rest))], axis=1)
    tile = lambda a: jnp.asarray(np.tile(a, (1, n_heads)), F32)
    return tile(c), tile(s_up), tile(s_dn)


def _rope_apply(x, c, s_up, s_dn, half):
    return x * c + pltpu.roll(x, LANES - half, 1) * s_up + pltpu.roll(x, half, 1) * s_dn


def _retention_tables():
    c = RET_CHUNK
    log_gamma = np.log(1.0 - 2.0 ** (-5.0 - np.arange(RET_HEADS, dtype=np.float64)))
    i = np.arange(c, dtype=np.float64)
    diff = i[:, None] - i[None, :]
    decay = np.where(diff >= 0, np.exp(diff[None] * log_gamma[:, None, None]), 0.0)
    qdec = np.repeat(np.exp((i + 1.0)[:, None] * log_gamma[None, :]), RET_DK, axis=1)
    kdec = np.repeat(np.exp((c - 1.0 - i)[:, None] * log_gamma[None, :]), RET_DK, axis=1)
    chunk_decay = [float(v) for v in np.exp(c * log_gamma)]
    return jnp.asarray(decay, F32), jnp.asarray(qdec, F32), jnp.asarray(kdec, F32), chunk_decay


def _retention_kernel(chunk_decay, q_ref, k_ref, v0_ref, v1_ref, g0_ref, g1_ref, c_ref, su_ref, sd_ref,
                      dec_ref, qdec_ref, kdec_ref, lng_ref, lnb_ref, o_ref, state_ref):
    @pl.when(pl.program_id(1) == 0)
    def _():
        state_ref[...] = jnp.zeros_like(state_ref)

    half = RET_DK // 2
    tabs = (c_ref[...], su_ref[...], sd_ref[...])
    q = jnp.concatenate([_rope_apply(q_ref[:, s:s + LANES], *tabs, half) for s in (0, LANES)], axis=1)
    k = jnp.concatenate([_rope_apply(k_ref[:, s:s + LANES], *tabs, half) for s in (0, LANES)], axis=1)
    k = k * (RET_DK ** -0.5)
    q_dec = q * qdec_ref[...]
    k_dec = k * kdec_ref[...]
    v = jnp.concatenate([v0_ref[...], v1_ref[...]], axis=1)
    gate = jnp.concatenate([g0_ref[...], g1_ref[...]], axis=1)
    states = [state_ref[h] for h in range(RET_HEADS)]

    def head(h):
        ks = slice(h * RET_DK, (h + 1) * RET_DK)
        vh = v[:, h * RET_DV:(h + 1) * RET_DV].astype(BF16)
        scores = _dot_nt(q[:, ks].astype(BF16), k[:, ks].astype(BF16)) * dec_ref[h]
        yield
        y = _dot(scores.astype(BF16), vh) + _dot(q_dec[:, ks].astype(BF16), states[h].astype(BF16))
        yield
        new_state = states[h] * chunk_decay[h] + _dot_tn(k_dec[:, ks].astype(BF16), vh)
        yield
        mu = jnp.mean(y, axis=-1, keepdims=True)
        yc = y - mu
        var = jnp.mean(yc * yc, axis=-1, keepdims=True)
        return yc * lax.rsqrt(var + RET_GN_EPS), new_state

    results = _run_interleaved([head(h) for h in range(RET_HEADS)])
    outs = [r[0] for r in results]
    for h, (_, new_state) in enumerate(results):
        state_ref[h] = new_state
    yn = jnp.concatenate(outs, axis=1) * lng_ref[...] + lnb_ref[...]
    o_ref[...] = (gate * _sigmoid(gate) * yn).astype(o_ref.dtype)


def retention_mixer(h, bsz, seq, col0, ln_g, ln_b):
    c = RET_CHUNK
    n_chunk = seq // c
    qk_w = RET_HEADS * RET_DK
    v_w = RET_HEADS * RET_DV
    assert col0 % qk_w == 0 and qk_w == 2 * LANES and v_w == 2 * qk_w
    cb = col0 // qk_w
    rc, rsu, rsd = _rope_tables(np.arange(seq), RET_DK, RET_THETA, RET_DK, 2)
    dec, qdec, kdec, chunk_decay = _retention_tables()
    row = lambda j: (lambda b, n: (b * n_chunk + n, j))
    pos = lambda b, n: (n, 0)
    fixed2 = lambda b, n: (0, 0)
    kern = functools.partial(_retention_kernel, chunk_decay)
    return pl.pallas_call(
        kern,
        grid=(bsz, n_chunk),
        in_specs=[pl.BlockSpec((c, qk_w), row(cb)), pl.BlockSpec((c, qk_w), row(cb + 1)),
                  pl.BlockSpec((c, qk_w), row(cb + 2)), pl.BlockSpec((c, qk_w), row(cb + 3)),
                  pl.BlockSpec((c, qk_w), row(cb + 4)), pl.BlockSpec((c, qk_w), row(cb + 5)),
                  pl.BlockSpec((c, LANES), pos), pl.BlockSpec((c, LANES), pos), pl.BlockSpec((c, LANES), pos),
                  pl.BlockSpec((RET_HEADS, c, c), lambda b, n: (0, 0, 0)),
                  pl.BlockSpec((c, qk_w), fixed2), pl.BlockSpec((c, qk_w), fixed2),
                  pl.BlockSpec((1, v_w), fixed2), pl.BlockSpec((1, v_w), fixed2)],
        out_specs=pl.BlockSpec((c, v_w), lambda b, n: (b * n_chunk + n, 0)),
        out_shape=jax.ShapeDtypeStruct((bsz * seq, v_w), BF16),
        scratch_shapes=[pltpu.VMEM((RET_HEADS, RET_DK, RET_DV), F32)],
        compiler_params=_params("parallel", "arbitrary"),
        name="retention",
    )(h, h, h, h, h, h, rc, rsu, rsd, dec, qdec, kdec, ln_g.reshape(1, v_w), ln_b.reshape(1, v_w))


NSA_KV_W = NSA_KV_GROUPS * NSA_HEAD_DIM
NSA_VT_ROWS = NSA_HEAD_DIM + 16
NSA_GATE_COLS = 3 * NSA_HEADS


def _nsa_prep_kernel(q_ref, kvc_ref, kvs_ref, kvw_ref, c_ref, su_ref, sd_ref,
                     qo_ref, kc_ref, vc_ref, ks_ref, vs_ref, kw_ref, vw_ref):
    half = NSA_ROT_DIM // 2
    tabs = (c_ref[...], su_ref[...], sd_ref[...])
    scale = NSA_HEAD_DIM ** -0.5 * math.log2(math.e)
    q = jnp.concatenate(
        [_rope_apply(q_ref[:, s:s + LANES], *tabs, half) * scale for s in range(0, NSA_WIDTH, LANES)], axis=1)
    qo_ref[0] = q.T.astype(qo_ref.dtype)

    def split(x, o_ref):
        for g in range(NSA_KV_GROUPS):
            o_ref[0, g] = x[:, g * NSA_HEAD_DIM:(g + 1) * NSA_HEAD_DIM].astype(o_ref.dtype)

    def split_t(x, o_ref):
        xt = x.T
        for g in range(NSA_KV_GROUPS):
            o_ref[0, g, :NSA_HEAD_DIM] = xt[g * NSA_HEAD_DIM:(g + 1) * NSA_HEAD_DIM, :].astype(o_ref.dtype)
            o_ref[0, g, NSA_HEAD_DIM:] = jnp.ones((NSA_VT_ROWS - NSA_HEAD_DIM, xt.shape[1]), o_ref.dtype)

    split(kvc_ref[:, :NSA_KV_W], kc_ref)
    split(kvc_ref[:, NSA_KV_W:], vc_ref)
    n_hot = ks_ref.shape[3] - NSA_HEAD_DIM
    tl = q_ref.shape[0]
    blk = (pl.program_id(1) * tl + lax.broadcasted_iota(jnp.int32, (tl, n_hot), 0)) // SLC_BLOCK
    one_hot = jnp.where(blk % n_hot == lax.broadcasted_iota(jnp.int32, (tl, n_hot), 1), 1.0, 0.0)
    ks = _rope_apply(kvs_ref[:, :NSA_KV_W], *tabs, half)
    for g in range(NSA_KV_GROUPS):
        ks_ref[0, g] = jnp.concatenate([ks[:, g * NSA_HEAD_DIM:(g + 1) * NSA_HEAD_DIM], one_hot],
                                       axis=1).astype(ks_ref.dtype)
    split_t(kvs_ref[:, NSA_KV_W:], vs_ref)
    split(_rope_apply(kvw_ref[:, :NSA_KV_W], *tabs, half), kw_ref)
    split_t(kvw_ref[:, NSA_KV_W:], vw_ref)


def nsa_prep(h, bsz, seq, tl=512):
    tl = min(tl, seq)
    nl = seq // tl
    rc, rsu, rsd = _rope_tables(np.arange(seq), NSA_ROT_DIM, ROPE_THETA, NSA_HEAD_DIM, LANES // NSA_HEAD_DIM)
    row = lambda j: (lambda b, l: (b * nl + l, j))
    pos = lambda b, l: (l, 0)
    kv_out = pl.BlockSpec((1, NSA_KV_GROUPS, tl, NSA_HEAD_DIM), lambda b, l: (b, 0, l, 0))
    kv_shape = lambda dt: jax.ShapeDtypeStruct((bsz, NSA_KV_GROUPS, seq, NSA_HEAD_DIM), dt)
    vt_out = pl.BlockSpec((1, NSA_KV_GROUPS, NSA_VT_ROWS, tl), lambda b, l: (b, 0, 0, l))
    vt_shape = jax.ShapeDtypeStruct((bsz, NSA_KV_GROUPS, NSA_VT_ROWS, seq), BF16)
    ks_w = NSA_HEAD_DIM + min(NSA_KEY_TILE, seq) // SLC_BLOCK
    ks_out = pl.BlockSpec((1, NSA_KV_GROUPS, tl, ks_w), lambda b, l: (b, 0, l, 0))
    ks_shape = jax.ShapeDtypeStruct((bsz, NSA_KV_GROUPS, seq, ks_w), BF16)
    two = 2 * NSA_KV_W
    return pl.pallas_call(
        _nsa_prep_kernel,
        grid=(bsz, nl),
        in_specs=[pl.BlockSpec((tl, NSA_WIDTH), row(1)),
                  pl.BlockSpec((tl, two), row(4)), pl.BlockSpec((tl, two), row(5)), pl.BlockSpec((tl, two), row(6)),
                  pl.BlockSpec((tl, LANES), pos), pl.BlockSpec((tl, LANES), pos), pl.BlockSpec((tl, LANES), pos)],
        out_specs=[pl.BlockSpec((1, NSA_WIDTH, tl), lambda b, l: (b, 0, l)),
                   kv_out, kv_out, ks_out, vt_out, kv_out, vt_out],
        out_shape=[jax.ShapeDtypeStruct((bsz, NSA_WIDTH, seq), BF16),
                   kv_shape(F32), kv_shape(F32), ks_shape, vt_shape, kv_shape(BF16), vt_shape],
        compiler_params=_params("parallel", "parallel"),
        name="nsa_prep",
    )(h, h, h, h, rc, rsu, rsd)


def _nsa_compress_kernel(hk_ref, hv_ref, pek_ref, pev_ref, kw1_ref, kb1_ref, kw2_ref, vw1_ref, vb1_ref, vw2_ref,
                         c_ref, su_ref, sd_ref, ko_ref, vo_ref):
    def mlp(h_ref, pe_ref, w1_ref, b1_ref, w2_ref):
        hb = h_ref[0, 0]
        rows = hb.shape[0]
        first = _dot((hb + pe_ref[0:1, :]).astype(BF16), w1_ref[0])
        second = _dot((hb + pe_ref[1:2, :]).astype(BF16), w1_ref[1])
        hid = _gelu(first + pltpu.roll(second, rows - 1, 0) + b1_ref[...])
        return _dot(hid.astype(BF16), w2_ref[...])

    kc = _rope_apply(mlp(hk_ref, pek_ref, kw1_ref, kb1_ref, kw2_ref), c_ref[...], su_ref[...], sd_ref[...],
                     NSA_ROT_DIM // 2)
    vc = mlp(hv_ref, pev_ref, vw1_ref, vb1_ref, vw2_ref)
    ko_ref[0, 0] = kc[:, :NSA_HEAD_DIM].astype(ko_ref.dtype)
    vo_ref[0, 0, :NSA_HEAD_DIM] = vc.T[:NSA_HEAD_DIM, :].astype(vo_ref.dtype)
    vo_ref[0, 0, NSA_HEAD_DIM:] = jnp.ones((NSA_VT_ROWS - NSA_HEAD_DIM, vc.shape[0]), vo_ref.dtype)


def nsa_compress(kc, vc, pe_k, pe_v, ck_w1, ck_b1, ck_w2, cv_w1, cv_b1, cv_w2):
    bsz, grp, seq, d = kc.shape
    n_rows = seq // CMP_STRIDE
    flat = CMP_STRIDE * d
    cmp_end = np.arange(n_rows) * CMP_STRIDE + CMP_BLOCK - 1
    rc, rsu, rsd = _rope_tables(cmp_end, NSA_ROT_DIM, ROPE_THETA, NSA_HEAD_DIM, LANES // NSA_HEAD_DIM)
    pad_w2 = lambda w: jnp.pad(w, ((0, 0), (0, LANES - d))).astype(BF16)
    blk = pl.BlockSpec((1, 1, n_rows, flat), lambda b, g: (b, g, 0, 0))
    f2 = lambda b, g: (0, 0)
    f3 = lambda b, g: (0, 0, 0)
    w_specs = [pl.BlockSpec((2, flat, CMP_HIDDEN), f3), pl.BlockSpec((1, CMP_HIDDEN), f2),
               pl.BlockSpec((CMP_HIDDEN, LANES), f2)]
    return pl.pallas_call(
        _nsa_compress_kernel,
        grid=(bsz, grp),
        in_specs=[blk, blk, pl.BlockSpec((2, flat), f2), pl.BlockSpec((2, flat), f2)] + w_specs + w_specs
                 + [pl.BlockSpec((n_rows, LANES), f2)] * 3,
        out_specs=[pl.BlockSpec((1, 1, n_rows, d), lambda b, g: (b, g, 0, 0)),
                   pl.BlockSpec((1, 1, NSA_VT_ROWS, n_rows), lambda b, g: (b, g, 0, 0))],
        out_shape=[jax.ShapeDtypeStruct((bsz, grp, n_rows, d), BF16),
                   jax.ShapeDtypeStruct((bsz, grp, NSA_VT_ROWS, n_rows), BF16)],
        compiler_params=_params("parallel", "parallel"),
        name="nsa_compress",
    )(kc.reshape(bsz, grp, n_rows, flat), vc.reshape(bsz, grp, n_rows, flat),
      pe_k.reshape(2, flat), pe_v.reshape(2, flat),
      ck_w1.reshape(2, flat, CMP_HIDDEN).astype(BF16), ck_b1.reshape(1, CMP_HIDDEN), pad_w2(ck_w2),
      cv_w1.reshape(2, flat, CMP_HIDDEN).astype(BF16), cv_b1.reshape(1, CMP_HIDDEN), pad_w2(cv_w2),
      rc, rsu, rsd)


def _per_head(x):
    return jnp.concatenate([x] * NSA_HPG, axis=1)


def _nsa_attn_kernel(seq, tk, qt_ref, gate_ref, kc_ref, vct_ref, ks_ref, vst_ref, kw_ref, vwt_ref, mmapt_ref,
                     o_ref, sel_ref):
    n_blk = seq // SLC_BLOCK
    n_sel = min(N_SLC, n_blk)
    hd = NSA_HEAD_DIM
    w = NSA_HPG * hd
    groups = range(NSA_KV_GROUPS)
    q0 = pl.program_id(1) * Q_BLOCK
    t_l = q0 + lax.broadcasted_iota(jnp.int32, (1, Q_BLOCK), 1)

    def select(g):
        qg = qt_ref[0, g * w:(g + 1) * w, :]
        qst = jnp.concatenate([qg[h * hd:(h + 1) * hd, :] for h in range(NSA_HPG)], axis=1)
        kc = kc_ref[0, g]
        n_cmp = kc.shape[0]
        cmp_end = lax.broadcasted_iota(jnp.int32, (n_cmp, 1), 0) * CMP_STRIDE + (CMP_BLOCK - 1)
        s = _dot(kc, qst) + _per_head(jnp.where(cmp_end <= t_l, 0.0, MASK_VALUE))
        yield
        p = jnp.exp2(s - jnp.max(s, axis=0, keepdims=True))
        any_key = _per_head(jnp.where(t_l >= CMP_BLOCK - 1, 1.0, 0.0))
        pv = _dot(vct_ref[0, g], p.astype(BF16))
        inv_l = any_key / pv[hd:hd + 1]
        o_cmp = pv[:hd] * inv_l
        yield
        p = p * inv_l
        imp = p[:, 0:Q_BLOCK]
        for h in range(1, NSA_HPG):
            imp = imp + p[:, h * Q_BLOCK:(h + 1) * Q_BLOCK]
        imp_slc = _dot_split(mmapt_ref[...], imp, 'b', 3)
        yield
        blk = lax.broadcasted_iota(jnp.int32, (n_blk, 1), 0)
        cur = t_l // SLC_BLOCK
        score = jnp.where(blk == 0, FORCE_SCORE,
                          jnp.where(blk == cur, FORCE_SCORE, jnp.where(blk == cur - 1, FORCE_SCORE, imp_slc)))
        score = jnp.where(blk * SLC_BLOCK <= t_l, score, -FORCE_SCORE)
        sel = jnp.zeros((n_blk, Q_BLOCK), F32)
        for _ in range(n_sel):
            best = jnp.max(score, axis=0, keepdims=True)
            idx = jnp.min(jnp.where(score == best, blk, n_blk), axis=0, keepdims=True)
            pick = blk == idx
            sel = jnp.where(pick, 1.0, sel)
            score = jnp.where(pick, -jnp.inf, score)
            yield
        sel_ref[g] = sel
        return qst, o_cmp

    selected = _run_interleaved([select(g) for g in groups])
    qst = [r[0] for r in selected]
    o_cmp = [r[1] for r in selected]

    blocks_per_tile = tk // SLC_BLOCK
    assert ks_ref.shape[3] == hd + blocks_per_tile and Q_BLOCK <= tk and tk % Q_BLOCK == 0

    def slc_tile(kt, carry, causal_bias=None):
        k0 = pl.multiple_of(kt * tk, tk)
        out = []
        for g in groups:
            m, acc = carry[g]
            sel_rows = sel_ref[g, pl.ds(pl.multiple_of(kt * blocks_per_tile, blocks_per_tile), blocks_per_tile), :]
            q_aug = jnp.concatenate([qst[g], _per_head((sel_rows - 1.0) * -MASK_VALUE).astype(BF16)], axis=0)
            s = _dot(ks_ref[0, g, pl.ds(k0, tk), :], q_aug)
            if causal_bias is not None:
                s = s + causal_bias
            m_new = jnp.maximum(m, jnp.max(s, axis=0, keepdims=True))
            alpha = jnp.exp2(m - m_new)
            p = jnp.exp2(s - m_new)
            acc = alpha * acc + _dot(vst_ref[0, g, :, pl.ds(k0, tk)], p.astype(BF16))
            out.append((m_new, acc))
        return tuple(out)

    n_full = q0 // tk
    cols = NSA_HPG * Q_BLOCK
    init = tuple((jnp.full((1, cols), MASK_VALUE, F32), jnp.zeros((NSA_VT_ROWS, cols), F32)) for _ in groups)
    slc = lax.fori_loop(0, n_full, slc_tile, init)
    kpos = n_full * tk + lax.broadcasted_iota(jnp.int32, (tk, 1), 0)
    slc = slc_tile(n_full, slc, _per_head(jnp.where(kpos <= t_l, 0.0, MASK_VALUE)))

    band = WINDOW + Q_BLOCK
    w0 = pl.multiple_of(jnp.maximum(q0 - WINDOW, 0), Q_BLOCK)
    kpos = w0 + lax.broadcasted_iota(jnp.int32, (band, 1), 0)
    win_bias = _per_head(jnp.where(kpos <= t_l, jnp.where(kpos > t_l - WINDOW, 0.0, MASK_VALUE), MASK_VALUE))
    sig_t = _sigmoid(gate_ref[...]).T

    def finish(g):
        s = _dot(kw_ref[0, g, pl.ds(w0, band), :], qst[g]) + win_bias
        yield
        p = jnp.exp2(s - jnp.max(s, axis=0, keepdims=True))
        pv = _dot(vwt_ref[0, g, :, pl.ds(w0, band)], p.astype(BF16))
        o_win = pv[:hd] / pv[hd:hd + 1]
        yield
        acc = slc[g][1]
        acc_slc, l_slc = acc[:hd], acc[hd:hd + 1]

        def gate(branch):
            first = (g * NSA_HPG) * 3 + branch
            return jnp.concatenate([sig_t[first + 3 * h:first + 3 * h + 1, :] for h in range(NSA_HPG)], axis=1)

        out_t = gate(0) * o_cmp[g] + gate(1) * (acc_slc / l_slc) + gate(2) * o_win
        pairs = []
        for h in range(0, NSA_HPG, 2):
            two = jnp.concatenate([out_t[:, h * Q_BLOCK:(h + 1) * Q_BLOCK],
                                   out_t[:, (h + 1) * Q_BLOCK:(h + 2) * Q_BLOCK]], axis=0)
            pairs.append(two.T)
        o_ref[:, g * w:(g + 1) * w] = jnp.concatenate(pairs, axis=1).astype(o_ref.dtype)

    _run_interleaved([finish(g) for g in groups])


def _nsa_pool_matrix(seq):
    n_blk = seq // SLC_BLOCK
    n_rows = seq // CMP_STRIDE
    per_stride = SLC_BLOCK // CMP_STRIDE
    span = CMP_BLOCK // CMP_STRIDE
    pool = np.zeros((n_blk, n_rows), np.float32)
    for j in range(n_blk):
        for m in range(per_stride):
            for n in range(span):
                c = per_stride * j + m + n - (span - 1)
                if 0 <= c < n_rows - 1:
                    pool[j, c] += 1.0
    return jnp.asarray(pool, BF16)


def nsa_attention(qt, h, gate_col_block, k_cmp, v_cmp_t, ks, vs_t, kw, vw_t, bsz, seq):
    tk = min(NSA_KEY_TILE, seq)
    nq = seq // Q_BLOCK
    pool = _nsa_pool_matrix(seq)
    n_rows = k_cmp.shape[2]
    d = NSA_HEAD_DIM
    qblk = lambda b, i: (b * nq + i, 0)
    whole = lambda *shape: pl.BlockSpec((1, NSA_KV_GROUPS) + shape, lambda b, i: (b, 0, 0, 0))
    kern = functools.partial(_nsa_attn_kernel, seq, tk)
    return pl.pallas_call(
        kern,
        grid=(bsz, nq),
        in_specs=[pl.BlockSpec((1, NSA_WIDTH, Q_BLOCK), lambda b, i: (b, 0, i)),
                  pl.BlockSpec((Q_BLOCK, LANES), lambda b, i: (b * nq + i, gate_col_block)),
                  whole(n_rows, d), whole(NSA_VT_ROWS, n_rows), whole(seq, ks.shape[3]), whole(NSA_VT_ROWS, seq),
                  whole(seq, d), whole(NSA_VT_ROWS, seq),
                  pl.BlockSpec(pool.shape, lambda b, i: (0, 0))],
        out_specs=pl.BlockSpec((Q_BLOCK, NSA_WIDTH), qblk),
        out_shape=jax.ShapeDtypeStruct((bsz * seq, NSA_WIDTH), BF16),
        scratch_shapes=[pltpu.VMEM((NSA_KV_GROUPS, seq // SLC_BLOCK, Q_BLOCK), F32)],
        compiler_params=_params("parallel", "arbitrary"),
        name="nsa_attention",
    )(qt, h, k_cmp, v_cmp_t, ks, vs_t, kw, vw_t, pool)


def nsa_mixer(h, bsz, seq, gate_col_block, pe_k, pe_v, ck_w1, ck_b1, ck_w2, cv_w1, cv_b1, cv_w2):
    qt, kc, vc, ks, vs_t, kw, vw_t = nsa_prep(h, bsz, seq)
    k_cmp, v_cmp_t = nsa_compress(kc, vc, pe_k, pe_v, ck_w1, ck_b1, ck_w2, cv_w1, cv_b1, cv_w2)
    return nsa_attention(qt, h, gate_col_block, k_cmp, v_cmp_t, ks, vs_t, kw, vw_t, bsz, seq)


def _head_ones(width, head_dim):
    idx = np.arange(width) // head_dim
    return jnp.asarray(idx[:, None] == idx[None, :], BF16)


def _softplus(x):
    return jnp.maximum(x, 0.0) + jnp.log(1.0 + jnp.exp(-jnp.abs(x)))


def _rwkv_pre_kernel(p_ref, prev_ref, mu_ref, w0_ref, wup_ref, a0_ref, aup_ref, gup_ref, kk_ref, ka_ref, rk_ref,
                     ones_ref, r_o, k_o, v_o, kk_o, b_o, ld_o, g_o, bonus_o):
    w = RWKV_WIDTH
    p = p_ref[...]
    first_row = jnp.where(pl.program_id(1) == 0, 0.0, prev_ref[7:8, :])
    is_row0 = lax.broadcasted_iota(jnp.int32, p.shape, 0) == 0
    prev = jnp.where(is_row0, first_row, pltpu.roll(p, 1, 0))
    ps = p + (prev - p) * mu_ref[...]
    r, k, v = ps[:, 0:w], ps[:, w:2 * w], ps[:, 2 * w:3 * w]
    o = 3 * w
    w_lo = ps[:, o:o + RWKV_LORA_W]
    a_lo = ps[:, o + RWKV_LORA_W:o + RWKV_LORA_W + RWKV_LORA_A]
    g_lo = ps[:, o + RWKV_LORA_W + RWKV_LORA_A:]
    wlog = -_softplus(-(w0_ref[...] + _dot(jnp.tanh(w_lo).astype(BF16), wup_ref[...]))) - 0.5
    a = _sigmoid(a0_ref[...] + _dot(a_lo.astype(BF16), aup_ref[...]))
    g = _dot(_sigmoid(g_lo).astype(BF16), gup_ref[...])
    kk = k * kk_ref[...]
    norm = jnp.sqrt(_dot_split(kk * kk, ones_ref[...], 'a', 2))
    kk = kk / jnp.maximum(norm, 1e-12)
    k2 = k * (1.0 + (a - 1.0) * ka_ref[...])
    r_o[...] = r
    k_o[...] = k2
    v_o[...] = v
    kk_o[...] = kk
    b_o[...] = kk * a
    ld_o[...] = -jnp.exp(wlog)
    g_o[...] = g
    bonus_o[...] = _dot_split(r * k2 * rk_ref[...], ones_ref[...], 'a', 2) * v


def rwkv_pre(h, bsz, seq, mu, w0, w_up, a0, a_up, g_up, k_k, k_a, r_k, tl=512):
    tl = min(tl, seq)
    nl = seq // tl
    w = RWKV_WIDTH
    cols = RWKV_COLS
    ones = _head_ones(w, RWKV_HEAD_DIM)
    f2 = lambda b, l: (0, 0)
    vec = pl.BlockSpec((1, w), f2)
    out_spec = pl.BlockSpec((tl, w), lambda b, l: (b * nl + l, 0))
    out_shape = jax.ShapeDtypeStruct((bsz * seq, w), F32)
    return pl.pallas_call(
        _rwkv_pre_kernel,
        grid=(bsz, nl),
        in_specs=[pl.BlockSpec((tl, cols), lambda b, l: (b * nl + l, 0)),
                  pl.BlockSpec((8, cols), lambda b, l: (jnp.maximum((b * seq + l * tl) // 8 - 1, 0), 0)),
                  pl.BlockSpec((1, cols), f2), vec, pl.BlockSpec((RWKV_LORA_W, w), f2),
                  vec, pl.BlockSpec((RWKV_LORA_A, w), f2), pl.BlockSpec((RWKV_LORA_G, w), f2),
                  vec, vec, vec, pl.BlockSpec((w, w), f2)],
        out_specs=[out_spec] * 8,
        out_shape=[out_shape] * 8,
        compiler_params=_params("parallel", "parallel"),
        name="rwkv_pre",
    )(h, h, mu.reshape(1, cols), w0.reshape(1, w), w_up.astype(BF16), a0.reshape(1, w), a_up.astype(BF16),
      g_up.astype(BF16), k_k.reshape(1, w), k_a.reshape(1, w), r_k.reshape(1, w), ones)


def _rwkv_masks():
    t, pk = RWKV_CHUNK, RWKV_PACK
    n = t * pk
    ri = np.arange(n)
    same = (ri[:, None] // t) == (ri[None, :] // t)
    tt, ss = ri[:, None] % t, ri[None, :] % t
    levels = []
    k = 1
    while k < t:
        levels.append(same & (tt // (2 * k) == ss // (2 * k)) & ((tt // k) % 2 == 1) & ((ss // k) % 2 == 0))
        k *= 2
    lvl = np.stack(levels).astype(np.float32)
    tri = (np.arange(t)[:, None] >= np.arange(t)[None, :]).astype(np.float32)
    head_lane = ((ri[:, None] // t) == (np.arange(pk * RWKV_HEAD_DIM)[None, :] // RWKV_HEAD_DIM)).astype(np.float32)
    return (jnp.asarray(tri, BF16), jnp.asarray(head_lane), jnp.asarray(same.astype(np.float32)), jnp.asarray(lvl))


def _rwkv_chain(r, k, v, kk, b, ld, st, tri, hl, bd, lvl_ref):
    t, pk = RWKV_CHUNK, RWKV_PACK
    n = t * pk
    c = _dot_split(tri, ld, 'b', 3)
    yield
    c_end = c[t - 1:t, :]
    e_neg = jnp.exp(-c)
    e_end = jnp.exp(c_end - c)
    kkd = (kk * jnp.exp(c - ld)).astype(BF16)
    rd = (r * jnp.exp(c)).astype(BF16)

    def big(x):
        return (jnp.concatenate([x] * pk, axis=0) * hl).astype(BF16)

    st_b = st.astype(BF16)
    v_big = big(v)
    a_all = _dot_nt(jnp.concatenate([kkd, rd], axis=0),
                    jnp.concatenate([big(k * e_neg), big(b * e_neg)], axis=0))
    yield
    ti = lax.broadcasted_iota(jnp.int32, (t, n), 0)
    si = lax.broadcasted_iota(jnp.int32, (t, n), 1) % t
    strict = ti > si
    incl = ti >= si
    a_kk = jnp.where(strict, a_all[:t, :n], 0.0)
    a_kb = jnp.where(strict, a_all[:t, n:], 0.0)
    a_rk = jnp.where(incl, a_all[t:, :n], 0.0)
    a_rb = jnp.where(incl, a_all[t:, n:], 0.0)
    rhs = _dot(kkd, st_b) + _dot(a_kk.astype(BF16), v_big)
    yield
    a_bd = jnp.concatenate([a_kb] * pk, axis=0) * bd
    m = jnp.where(lax.broadcasted_iota(jnp.int32, (n, n), 0) == lax.broadcasted_iota(jnp.int32, (n, n), 1), 1.0, 0.0)
    for lv in range(lvl_ref.shape[0]):
        mb = m.astype(BF16)
        ma = _dot(mb, (a_bd * lvl_ref[lv]).astype(BF16)).astype(BF16)
        yield
        m = m - _dot(ma, mb)
        yield
    u_big = _dot(m.astype(BF16), big(rhs))
    yield
    u = u_big[0:t]
    for h in range(1, pk):
        u = u + u_big[h * t:(h + 1) * t]
    y = _dot(rd, st_b) + _dot(a_rk.astype(BF16), v_big) - _dot(a_rb.astype(BF16), big(u))
    yield
    decay_col = jnp.broadcast_to(jnp.exp(c_end), st.shape).T
    kb_end = jnp.concatenate([k * e_end, -(b * e_end)], axis=0).astype(BF16)
    vu = jnp.concatenate([v, u], axis=0).astype(BF16)
    return y, decay_col * st + bd * _dot_tn(kb_end, vu)


def _rwkv_chunk_kernel(r_ref, k_ref, v_ref, kk_ref, b_ref, ld_ref, tri_ref, hl_ref, bd_ref, lvl_ref, y_ref, st_ref):
    @pl.when(pl.program_id(0) == 0)
    def _():
        st_ref[...] = jnp.zeros_like(st_ref)

    wp = RWKV_PACK * RWKV_HEAD_DIM
    tri, hl, bd = tri_ref[...], hl_ref[...], bd_ref[...]
    n_pack = r_ref.shape[2] // wp
    where = [(bi, slice(g * wp, (g + 1) * wp)) for bi in range(r_ref.shape[0]) for g in range(n_pack)]
    loaded = [tuple(ref[bi, :, cols] for ref in (r_ref, k_ref, v_ref, kk_ref, b_ref, ld_ref)) + (st_ref[i],)
              for i, (bi, cols) in enumerate(where)]
    results = _run_interleaved([_rwkv_chain(*args, tri, hl, bd, lvl_ref) for args in loaded])
    for i, ((bi, cols), (y, st_new)) in enumerate(zip(where, results)):
        y_ref[bi, :, cols] = y
        st_ref[i] = st_new


def rwkv_chunk(r, k, v, kk, b, ld, bsz, seq):
    t, pk = RWKV_CHUNK, RWKV_PACK
    n_chunk = seq // t
    w = RWKV_WIDTH
    wp = pk * RWKV_HEAD_DIM
    assert t == RWKV_HEAD_DIM
    tri, hl, bd, lvl = _rwkv_masks()
    blk = pl.BlockSpec((bsz, t, w), lambda c: (0, c, 0))
    f2 = lambda c: (0, 0)
    shaped = lambda a: a.reshape(bsz, seq, w)
    y = pl.pallas_call(
        _rwkv_chunk_kernel,
        grid=(n_chunk,),
        in_specs=[blk] * 6 + [pl.BlockSpec(tri.shape, f2), pl.BlockSpec(hl.shape, f2), pl.BlockSpec(bd.shape, f2),
                              pl.BlockSpec(lvl.shape, lambda c: (0, 0, 0))],
        out_specs=blk,
        out_shape=jax.ShapeDtypeStruct((bsz, seq, w), F32),
        scratch_shapes=[pltpu.VMEM((bsz * (w // wp), wp, wp), F32)],
        compiler_params=_params("arbitrary"),
        name="rwkv_chunk",
    )(shaped(r), shaped(k), shaped(v), shaped(kk), shaped(b), shaped(ld), tri, hl, bd, lvl)
    return y.reshape(bsz * seq, w)


def _rwkv_post_kernel(y_ref, bonus_ref, g_ref, lng_ref, lnb_ref, ones_ref, o_ref):
    y = y_ref[...]
    inv = 1.0 / RWKV_HEAD_DIM
    mu = _dot_split(y, ones_ref[...], 'a', 2) * inv
    yc = y - mu
    var = _dot_split(yc * yc, ones_ref[...], 'a', 2) * inv
    yn = yc * lax.rsqrt(var + RWKV_GN_EPS) * lng_ref[...] + lnb_ref[...]
    o_ref[...] = ((yn + bonus_ref[...]) * g_ref[...]).astype(o_ref.dtype)


def rwkv_post(y, bonus, g, ln_g, ln_b, tm=1024):
    n, w = y.shape
    tm = min(tm, n)
    row = pl.BlockSpec((tm, w), lambda i: (i, 0))
    vec = pl.BlockSpec((1, w), lambda i: (0, 0))
    return pl.pallas_call(
        _rwkv_post_kernel,
        grid=(n // tm,),
        in_specs=[row, row, row, vec, vec, pl.BlockSpec((w, w), lambda i: (0, 0))],
        out_specs=row,
        out_shape=jax.ShapeDtypeStruct((n, w), BF16),
        compiler_params=_params("parallel"),
        name="rwkv_post",
    )(y, bonus, g, ln_g.reshape(1, w), ln_b.reshape(1, w), _head_ones(w, RWKV_HEAD_DIM))


def rwkv7_mixer(h, bsz, seq, mu, w0, w_up, a0, a_up, g_up, k_k, k_a, r_k, ln_g, ln_b):
    r, k, v, kk, b, ld, g, bonus = rwkv_pre(h, bsz, seq, mu, w0, w_up, a0, a_up, g_up, k_k, k_a, r_k)
    y = rwkv_chunk(r, k, v, kk, b, ld, bsz, seq)
    return rwkv_post(y, bonus, g, ln_g, ln_b)


AB_IN = S5_WIDTH + NSA_WIDTH + 6 * NSA_KV_W + NSA_GATE_COLS
AB_IN_PADDED = -(-AB_IN // LANES) * LANES
NSA_GATE_COL_BLOCK = (AB_IN - NSA_GATE_COLS) // LANES
PROJ_TM = 512


def kernel(x, ab_w_in, ab_w_out, s5_lam_re, s5_lam_im, s5_log_dt, s5_b_re, s5_b_im, s5_c_re, s5_c_im, s5_d, s5_w_glu, s5_b_glu, nsa_pe_k, nsa_pe_v, nsa_ck_w1, nsa_ck_b1, nsa_ck_w2, nsa_cv_w1, nsa_cv_b1, nsa_cv_w2, cd_w_in, cd_w_out, rwkv_mu, rwkv_w0, rwkv_w_up, rwkv_a0, rwkv_a_up, rwkv_g_up, rwkv_k_k, rwkv_k_a, rwkv_r_k, rwkv_ln_g, rwkv_ln_b, ret_ln_g, ret_ln_b, ln1_g, ln1_b, ln2_g, ln2_b, moe_router, moe_bias, moe_w1, moe_w3, moe_w2, sh_w1, sh_w3, sh_w2):
    bsz, seq, d = x.shape
    assert (AB_IN - NSA_GATE_COLS) % LANES == 0
    xf = x.reshape(bsz * seq, d)
    x_in = xf
    for layer in range(DEPTH):
        i = layer // 2
        if layer % 2 == 0:
            w_in = jnp.pad(ab_w_in[i], ((0, 0), (0, AB_IN_PADDED - AB_IN))).astype(BF16)
            h, u3 = project(x_in, w_in, PROJ_TM, chunked=(S5_CHUNK, S5_WIDTH))
            y_1 = s5_mixer(h, u3, bsz, seq, s5_lam_re[i], s5_lam_im[i], s5_log_dt[i], s5_b_re[i], s5_b_im[i],
                           s5_c_re[i], s5_c_im[i], s5_d[i], s5_w_glu[i], s5_b_glu[i])
            y_2 = nsa_mixer(h, bsz, seq, NSA_GATE_COL_BLOCK, nsa_pe_k[i], nsa_pe_v[i], nsa_ck_w1[i], nsa_ck_b1[i],
                            nsa_ck_w2[i], nsa_cv_w1[i], nsa_cv_b1[i], nsa_cv_w2[i])
            w_out = ab_w_out[i]
        else:
            h = project(x_in, cd_w_in[i].astype(BF16), PROJ_TM)
            y_1 = rwkv7_mixer(h, bsz, seq, rwkv_mu[i], rwkv_w0[i], rwkv_w_up[i], rwkv_a0[i], rwkv_a_up[i],
                              rwkv_g_up[i], rwkv_k_k[i], rwkv_k_a[i], rwkv_r_k[i], rwkv_ln_g[i], rwkv_ln_b[i])
            y_2 = retention_mixer(h, bsz, seq, RWKV_COLS, ret_ln_g[i], ret_ln_b[i])
            w_out = cd_w_out[i]
        xf, gates = out_proj_ln_route(y_1, y_2, w_out, xf, ln1_g[layer], ln1_b[layer],
                                      moe_router[layer], moe_bias[layer])
        xf, x_in = moe_experts_ln(xf, gates, moe_w1[layer], moe_w3[layer], moe_w2, layer,
                                  sh_w1[layer], sh_w3[layer], sh_w2[layer], ln2_g[layer], ln2_b[layer])
    return xf.reshape(bsz, seq, d)
```

```python
import functools
import math

import jax
import jax.numpy as jnp
import numpy as np
from jax import lax
from jax.experimental import pallas as pl
from jax.experimental.pallas import tpu as pltpu

F32 = jnp.float32
BF16 = jnp.bfloat16
HIGHEST = lax.Precision.HIGHEST
FP8 = jnp.float8_e4m3fn
FP8_MAX = 448.0
FP8_TINY = 1e-30

VMEM_LIMIT_BYTES = 52 * 1024 * 1024
LANES = 128

LN_EPS = 1e-5
DEPTH = 2
ALPHA = (2 * DEPTH) ** 0.25

S5_GROUPS, S5_GROUP_CH, S5_STATE = 32, 16, 64
S5_WIDTH = S5_GROUPS * S5_GROUP_CH
S5_CHUNK = 16
S5_PACK = 8
NSA_HEADS, NSA_KV_GROUPS, NSA_HEAD_DIM = 8, 2, 64
NSA_HPG = NSA_HEADS // NSA_KV_GROUPS
NSA_WIDTH = NSA_HEADS * NSA_HEAD_DIM
NSA_ROT_DIM = NSA_HEAD_DIM // 4
ROPE_THETA = 500000.0
CMP_BLOCK, CMP_STRIDE, CMP_HIDDEN = 32, 16, 128
SLC_BLOCK, N_SLC, WINDOW = 64, 16, 512
Q_BLOCK = 256
NSA_KEY_TILE = 1024
FORCE_SCORE = 1e6
MASK_VALUE = -1e30
RWKV_HEADS, RWKV_HEAD_DIM = 8, 64
RWKV_WIDTH = RWKV_HEADS * RWKV_HEAD_DIM
RWKV_LORA_W, RWKV_LORA_A, RWKV_LORA_G = 64, 64, 128
RWKV_COLS = 3 * RWKV_WIDTH + RWKV_LORA_W + RWKV_LORA_A + RWKV_LORA_G
RWKV_GN_EPS = 64e-5
RWKV_CHUNK = 64
RWKV_PACK = 4
RET_HEADS, RET_DK, RET_DV, RET_CHUNK = 4, 64, 128, 128
RET_THETA = 10000.0
RET_GN_EPS = 1e-5
N_EXPERTS, TOP_K, EXPERT_FF = 64, 8, 256
N_EXPERT_GROUPS, TOPK_GROUPS = 8, 4
EXPERTS_PER_GROUP = N_EXPERTS // N_EXPERT_GROUPS
ROUTED_SCALE = 2.5
MOE_EXPERTS_PER_STEP = 4


def _params(*sem):
    return pltpu.CompilerParams(dimension_semantics=sem, vmem_limit_bytes=VMEM_LIMIT_BYTES)


def _dot(a, b, **kw):
    return jnp.dot(a, b, preferred_element_type=F32, **kw)


def _dot_nt(a, b, **kw):
    return lax.dot_general(a, b, (((1,), (1,)), ((), ())), preferred_element_type=F32, **kw)


def _dot_tn(a, b, **kw):
    return lax.dot_general(a, b, (((0,), (0,)), ((), ())), preferred_element_type=F32, **kw)


def _dot_split(a, b, split, parts):
    rest = a if split == 'a' else b
    acc = None
    for _ in range(parts):
        piece = rest.astype(BF16)
        term = _dot(piece, b) if split == 'a' else _dot(a, piece)
        acc = term if acc is None else acc + term
        rest = rest - piece.astype(F32)
    return acc


def _run_interleaved(gens):
    results = [None] * len(gens)
    live = list(range(len(gens)))
    while live:
        for i in list(live):
            try:
                next(gens[i])
            except StopIteration as done:
                results[i] = done.value
                live.remove(i)
    return results


def _gelu(x):
    return 0.5 * x * (1.0 + jnp.tanh(math.sqrt(2.0 / math.pi) * (x + 0.044715 * (x * x * x))))


def _sigmoid(x):
    return 1.0 / (1.0 + jnp.exp(-x))


def _layer_norm_rows(z, g, b):
    mu = jnp.mean(z, axis=-1, keepdims=True)
    zc = z - mu
    var = jnp.mean(zc * zc, axis=-1, keepdims=True)
    return zc * lax.rsqrt(var + LN_EPS) * g + b


def _proj_kernel(x_ref, w_ref, o_ref, *chunked_ref):
    y = _dot(x_ref[...].astype(BF16), w_ref[...])
    o_ref[...] = y
    for c_ref in chunked_ref:
        rows, t, w = c_ref.shape
        c_ref[...] = y[:, :w].reshape(rows, t, w)


def project(x, w_bf16, tm, chunked=None):
    m, k = x.shape
    n = w_bf16.shape[1]
    out_specs = [pl.BlockSpec((tm, n), lambda i: (i, 0))]
    out_shape = [jax.ShapeDtypeStruct((m, n), F32)]
    if chunked is not None:
        t, w = chunked
        out_specs.append(pl.BlockSpec((tm // t, t, w), lambda i: (i, 0, 0)))
        out_shape.append(jax.ShapeDtypeStruct((m // t, t, w), F32))
    out = pl.pallas_call(
        _proj_kernel,
        grid=(m // tm,),
        in_specs=[pl.BlockSpec((tm, k), lambda i: (i, 0)), pl.BlockSpec((k, n), lambda i: (0, 0))],
        out_specs=out_specs,
        out_shape=out_shape,
        compiler_params=_params("parallel"),
        name="project",
    )(x, w_bf16)
    return out if chunked is not None else out[0]


def _out_proj_ln_kernel(ya_ref, yb_ref, wa_ref, wb_ref, x_ref, g_ref, b_ref, rt_ref, rbias_ref, o_ref, gate_ref):
    mix = _dot(ya_ref[...], wa_ref[...]) + _dot(yb_ref[...], wb_ref[...])
    y = _layer_norm_rows(ALPHA * x_ref[...] + mix, g_ref[...], b_ref[...])
    o_ref[...] = y
    gate_ref[...] = _route(y, rt_ref[...], rbias_ref[...])


def out_proj_ln_route(ya, yb, w_out, x, g, b, router, router_bias, tm=512):
    n, d = x.shape
    ka, kb = ya.shape[1], yb.shape[1]
    wa = w_out[:ka].astype(BF16)
    wb = w_out[ka:].astype(BF16)
    row = lambda i: (i, 0)
    fixed = lambda i: (0, 0)
    return pl.pallas_call(
        _out_proj_ln_kernel,
        grid=(n // tm,),
        in_specs=[pl.BlockSpec((tm, ka), row), pl.BlockSpec((tm, kb), row),
                  pl.BlockSpec((ka, d), fixed), pl.BlockSpec((kb, d), fixed),
                  pl.BlockSpec((tm, d), row), pl.BlockSpec((1, d), fixed), pl.BlockSpec((1, d), fixed),
                  pl.BlockSpec((N_EXPERTS, d), fixed), pl.BlockSpec((N_EXPERTS, 1), fixed)],
        out_specs=[pl.BlockSpec((tm, d), row), pl.BlockSpec((tm, LANES), row)],
        out_shape=[jax.ShapeDtypeStruct((n, d), F32), jax.ShapeDtypeStruct((n, LANES), F32)],
        compiler_params=_params("parallel"),
        name="out_proj_ln_route",
    )(ya, yb, wa, wb, x, g.reshape(1, d), b.reshape(1, d), router.T, router_bias.reshape(N_EXPERTS, 1))


def _route(x, rt, bias):
    tr = x.shape[0]
    scores = _sigmoid(_dot_nt(rt, x, precision=HIGHEST))
    biased = scores + bias
    grp = biased.reshape(N_EXPERT_GROUPS, EXPERTS_PER_GROUP, tr)
    pos = lax.broadcasted_iota(jnp.int32, grp.shape, 1)
    m1 = jnp.max(grp, axis=1, keepdims=True)
    first = jnp.min(jnp.where(grp == m1, pos, EXPERTS_PER_GROUP), axis=1, keepdims=True)
    m2 = jnp.max(jnp.where(pos == first, -jnp.inf, grp), axis=1, keepdims=True)
    gscore = (m1 + m2).reshape(N_EXPERT_GROUPS, tr)
    gidx = lax.broadcasted_iota(jnp.int32, gscore.shape, 0)
    grank = jnp.zeros(gscore.shape, F32)
    for j in range(N_EXPERT_GROUPS):
        row = gscore[j:j + 1, :]
        grank = grank + jnp.where(gidx > j, jnp.where(row >= gscore, 1.0, 0.0), jnp.where(row > gscore, 1.0, 0.0))
    gkeep = jnp.where(grank < TOPK_GROUPS, 1.0, 0.0)
    keep = jnp.broadcast_to(gkeep[:, None, :], grp.shape).reshape(N_EXPERTS, tr)
    masked = jnp.where(keep > 0.5, biased, -jnp.inf)
    eidx = lax.broadcasted_iota(jnp.int32, masked.shape, 0)
    rank = jnp.zeros(masked.shape, F32)
    for j in range(N_EXPERTS):
        row = masked[j:j + 1, :]
        rank = rank + jnp.where(eidx > j, jnp.where(row >= masked, 1.0, 0.0), jnp.where(row > masked, 1.0, 0.0))
    gate = jnp.where(rank < TOP_K, scores, 0.0)
    gate = gate / jnp.sum(gate, axis=0, keepdims=True) * ROUTED_SCALE
    return jnp.concatenate([gate, jnp.zeros((LANES - N_EXPERTS, tr), F32)], axis=0).T


def _quantize_fp8(a, axes):
    amax = jnp.max(jnp.abs(a), axis=axes, keepdims=True)
    scale = jnp.maximum(amax, FP8_TINY) * (1.0 / FP8_MAX)
    return (a * (1.0 / scale)).astype(FP8), scale


def _swiglu_hidden(xq, x_scale, w1q, w3q, w_scale, gate=None):
    col1 = x_scale * w_scale[0:1, 0:1]
    col3 = x_scale * w_scale[1:2, 0:1]
    if gate is not None:
        col3 = col3 * gate
    h1 = _dot(xq, w1q) * col1
    return h1 * _sigmoid(h1) * (_dot(xq, w3q) * col3)


def _experts_ln_kernel(x_ref, gate_ref, w1_ref, w3_ref, ws_ref, w2_ref, sw1_ref, sw3_ref, sws_ref, sw2_ref,
                       g_ref, b_ref, o_ref, obf_ref, acc_ref, xq_ref, xs_ref):
    step = pl.program_id(1)
    per_step = w1_ref.shape[0]

    @pl.when(step == 0)
    def _():
        xq, xs = _quantize_fp8(x_ref[...], (1,))
        xq_ref[...] = xq
        xs_ref[...] = xs
        h = _swiglu_hidden(xq, xs, sw1_ref[...], sw3_ref[...], sws_ref[...])
        acc_ref[...] = _dot(h.astype(BF16), sw2_ref[...].astype(BF16))

    lane = lax.broadcasted_iota(jnp.int32, gate_ref.shape, 1)
    gates = gate_ref[...]
    xq, xs = xq_ref[...], xs_ref[...]
    hidden = []
    for j in range(per_step):
        gcol = jnp.sum(jnp.where(lane == step * per_step + j, gates, 0.0), axis=1, keepdims=True)
        hidden.append(_swiglu_hidden(xq, xs, w1_ref[j], w3_ref[j], ws_ref[j], gcol).astype(BF16))
    w2 = w2_ref[0].astype(BF16)
    acc_ref[...] += _dot(jnp.concatenate(hidden, axis=1), w2.reshape(per_step * w2.shape[1], w2.shape[2]))

    @pl.when(step == pl.num_programs(1) - 1)
    def _():
        y = _layer_norm_rows(ALPHA * x_ref[...] + acc_ref[...], g_ref[...], b_ref[...])
        o_ref[...] = y
        obf_ref[...] = y.astype(BF16)


def _quantize_weights_kernel(w1_ref, w3_ref, q1_ref, q3_ref, s_ref):
    for e in range(w1_ref.shape[0]):
        q1, s1 = _quantize_fp8(w1_ref[e], (0, 1))
        q3, s3 = _quantize_fp8(w3_ref[e], (0, 1))
        q1_ref[e] = q1
        q3_ref[e] = q3
        ff = s_ref.shape[2]
        s_ref[e] = jnp.concatenate([jnp.broadcast_to(s1, (1, ff)), jnp.broadcast_to(s3, (1, ff))], axis=0)


def _quantize_expert_weights(w1, w3):
    ne, d, ff = w1.shape
    per_step = math.gcd(ne, MOE_EXPERTS_PER_STEP)
    blk = lambda *shape: pl.BlockSpec((per_step,) + shape, lambda i: (i, 0, 0))
    return pl.pallas_call(
        _quantize_weights_kernel,
        grid=(ne // per_step,),
        in_specs=[blk(d, ff), blk(d, ff)],
        out_specs=[blk(d, ff), blk(d, ff), blk(2, ff)],
        out_shape=[jax.ShapeDtypeStruct((ne, d, ff), FP8), jax.ShapeDtypeStruct((ne, d, ff), FP8),
                   jax.ShapeDtypeStruct((ne, 2, ff), F32)],
        compiler_params=_params("parallel"),
        name="quantize_expert_weights",
    )(w1, w3)


def moe_experts_ln(x, gates, w1, w3, w2_layers, layer, sw1, sw3, sw2, g, b, tm=1024):
    n, d = x.shape
    ne = w1.shape[0]
    w1q, w3q, ws = _quantize_expert_weights(w1, w3)
    sw1q, sw3q, sws = (a[0] for a in _quantize_expert_weights(sw1[None], sw3[None]))
    tok = lambda i, e: (i, 0)
    fixed = lambda i, e: (0, 0)
    per_expert = lambda *blk: pl.BlockSpec((MOE_EXPERTS_PER_STEP,) + blk, lambda i, e: (e, 0, 0))
    return pl.pallas_call(
        _experts_ln_kernel,
        grid=(n // tm, ne // MOE_EXPERTS_PER_STEP),
        in_specs=[pl.BlockSpec((tm, d), tok), pl.BlockSpec((tm, LANES), tok),
                  per_expert(d, EXPERT_FF), per_expert(d, EXPERT_FF), per_expert(2, EXPERT_FF),
                  pl.BlockSpec((1, MOE_EXPERTS_PER_STEP, EXPERT_FF, d), lambda i, e: (layer, e, 0, 0)),
                  pl.BlockSpec((d, EXPERT_FF), fixed), pl.BlockSpec((d, EXPERT_FF), fixed),
                  pl.BlockSpec((2, EXPERT_FF), fixed), pl.BlockSpec((EXPERT_FF, d), fixed),
                  pl.BlockSpec((1, d), fixed), pl.BlockSpec((1, d), fixed)],
        out_specs=[pl.BlockSpec((tm, d), tok), pl.BlockSpec((tm, d), tok)],
        out_shape=[jax.ShapeDtypeStruct((n, d), F32), jax.ShapeDtypeStruct((n, d), BF16)],
        scratch_shapes=[pltpu.VMEM((tm, d), F32), pltpu.VMEM((tm, d), FP8), pltpu.VMEM((tm, 1), F32)],
        compiler_params=_params("parallel", "arbitrary"),
        name="moe_experts_ln",
    )(x, gates, w1q, w3q, ws, w2_layers, sw1q, sw3q, sws, sw2, g.reshape(1, d), b.reshape(1, d))


def _s5_tables(lam_re, lam_im, log_dt, b_re, b_im, c_re, c_im, n_chunk):
    t, h, p = S5_CHUNK, S5_GROUP_CH, S5_STATE
    dt = jnp.exp(log_dt.astype(F32))[:, None]
    den = lam_re ** 2 + lam_im ** 2

    def lam_pow(k):
        k = jnp.asarray(k, F32)[..., None, None]
        mag = jnp.exp(lam_re * dt * k)
        return mag * jnp.cos(lam_im * dt * k), mag * jnp.sin(lam_im * dt * k)

    lb_re, lb_im = lam_pow(1.0)
    f_re = ((lb_re - 1.0) * lam_re + lb_im * lam_im) / den
    f_im = (lb_im * lam_re - (lb_re - 1.0) * lam_im) / den
    bb_re = f_re[..., None] * b_re - f_im[..., None] * b_im
    bb_im = f_re[..., None] * b_im + f_im[..., None] * b_re
    pr, pi = lam_pow(jnp.arange(t))
    cl_re = c_re[None] * pr[:, :, None, :] - c_im[None] * pi[:, :, None, :]
    cl_im = c_re[None] * pi[:, :, None, :] + c_im[None] * pr[:, :, None, :]
    klag = jnp.einsum('tgop,gpi->tgoi', cl_re, bb_re) - jnp.einsum('tgop,gpi->tgoi', cl_im, bb_im)
    nb = S5_GROUPS // S5_PACK
    split = lambda a, axis: a.reshape(a.shape[:axis] + (nb, S5_PACK) + a.shape[axis + 1:])
    eye = jnp.eye(S5_PACK, dtype=F32)
    lag_t = jnp.transpose(split(klag, 1), (1, 0, 2, 4, 3))
    lag_t = (lag_t[:, :, :, :, None, :] * eye[None, None, :, None, :, None]).reshape(nb, t, LANES, LANES)
    qr, qi = lam_pow(t - 1 - jnp.arange(t))
    st_re = qr[..., None] * bb_re[None] - qi[..., None] * bb_im[None]
    st_im = qr[..., None] * bb_im[None] + qi[..., None] * bb_re[None]
    st = jnp.stack([st_re, st_im], axis=0)
    st_t = jnp.transpose(split(st, 2), (2, 1, 3, 5, 0, 4)).reshape(nb, t, LANES, 2 * p)
    er, ei = lam_pow(jnp.arange(t) + 1)
    x_re = c_re[None] * er[:, :, None, :] - c_im[None] * ei[:, :, None, :]
    x_im = c_re[None] * ei[:, :, None, :] + c_im[None] * er[:, :, None, :]
    cr = jnp.stack([x_re, -x_im], axis=0)
    cr_t = jnp.transpose(split(cr, 2), (2, 1, 0, 5, 3, 4)).reshape(nb, t, 2 * p, LANES)
    levels = max(1, int(math.log2(n_chunk)))
    sr, si = lam_pow(t * (2.0 ** jnp.arange(levels)))
    sr = sr.reshape(levels, nb, S5_PACK * p)
    si = si.reshape(levels, nb, S5_PACK * p)
    a1 = jnp.concatenate([sr, sr], axis=-1)
    a2 = jnp.concatenate([-si, si], axis=-1)
    scan = jnp.transpose(jnp.stack([a1, a2], axis=1), (2, 0, 1, 3))
    return lag_t, st_t, cr_t, scan.astype(F32)


def _s5_build_tables(lag_ref, st_ref, cr_ref, wtoe_ref, wstate_ref, wcross_ref):
    t = lag_ref.shape[1]
    p = S5_STATE
    kp = S5_PACK * p
    wtoe_ref[...] = jnp.zeros_like(wtoe_ref)
    for d in range(t):
        tile = lag_ref[0, d].astype(BF16)
        for j in range(t - d):
            wtoe_ref[j * LANES:(j + 1) * LANES, (j + d) * LANES:(j + d + 1) * LANES] = tile
    lane = lax.broadcasted_iota(jnp.int32, (LANES, LANES), 1)
    row_g = lax.broadcasted_iota(jnp.int32, (LANES, kp), 0) // S5_GROUP_CH
    same_s = row_g == lax.broadcasted_iota(jnp.int32, (LANES, kp), 1) // p
    for j in range(t):
        a = st_ref[0, j]
        swapped = pltpu.roll(a, p, 1)
        for c, both in enumerate((jnp.where(lane < p, a, swapped), jnp.where(lane < p, swapped, a))):
            wide = jnp.concatenate([both] * (kp // LANES), axis=1)
            wstate_ref[j * LANES:(j + 1) * LANES, c * kp:(c + 1) * kp] = jnp.where(same_s, wide, 0.0).astype(BF16)
    same_c = (lax.broadcasted_iota(jnp.int32, (kp, LANES), 0) // p
              == lax.broadcasted_iota(jnp.int32, (kp, LANES), 1) // S5_GROUP_CH)
    for i in range(t):
        a = cr_ref[0, i]
        for c in range(2):
            tall = jnp.concatenate([a[c * p:(c + 1) * p]] * S5_PACK, axis=0)
            wcross_ref[c * kp:(c + 1) * kp, i * LANES:(i + 1) * LANES] = jnp.where(same_c, tall, 0.0).astype(BF16)


def _s5_kernel(u_ref, lag_ref, st_ref, cr_ref, scan_ref, o_ref, wtoe_ref, wstate_ref, wcross_ref):
    @pl.when(pl.program_id(1) == 0)
    def _():
        _s5_build_tables(lag_ref, st_ref, cr_ref, wtoe_ref, wstate_ref, wcross_ref)

    n_chunk, t, _ = u_ref.shape
    x = jnp.concatenate([u_ref[:, j, :] for j in range(t)], axis=1).astype(BF16)
    local = _dot(x, wtoe_ref[...])
    state = _dot(x, wstate_ref[...])
    row = lax.broadcasted_iota(jnp.int32, state.shape, 0)
    s = jnp.where(row >= 1, pltpu.roll(state, 1, 0), 0.0)
    half = state.shape[1] // 2
    level = 0
    d = 1
    while d < n_chunk:
        mult = scan_ref[0, level]
        prev = jnp.where(row >= d, pltpu.roll(s, d, 0), 0.0)
        s = s + mult[0:1, :] * prev + mult[1:2, :] * pltpu.roll(prev, half, 1)
        d *= 2
        level += 1
    y = local + _dot(s.astype(BF16), wcross_ref[...])
    for i in range(t):
        o_ref[:, i, :] = y[:, i * LANES:(i + 1) * LANES]


def s5_scan(u3, bsz, lag_t, st_t, cr_t, scan):
    rows, t, w = u3.shape
    n_chunk = rows // bsz
    kp2 = 2 * S5_PACK * S5_STATE
    table = lambda a: pl.BlockSpec((1,) + a.shape[1:], lambda j, b: (j, 0, 0, 0))
    return pl.pallas_call(
        _s5_kernel,
        grid=(w // LANES, bsz),
        in_specs=[pl.BlockSpec((n_chunk, t, LANES), lambda j, b: (b, 0, j)),
                  table(lag_t), table(st_t), table(cr_t), table(scan)],
        out_specs=pl.BlockSpec((n_chunk, t, LANES), lambda j, b: (b, 0, j)),
        out_shape=jax.ShapeDtypeStruct(u3.shape, F32),
        scratch_shapes=[pltpu.VMEM((t * LANES, t * LANES), BF16), pltpu.VMEM((t * LANES, kp2), BF16),
                        pltpu.VMEM((kp2, t * LANES), BF16)],
        compiler_params=_params("arbitrary", "arbitrary"),
        name="s5_scan",
    )(u3, lag_t, st_t, cr_t, scan)


def _s5_post_kernel(y_ref, u_ref, d_ref, w_ref, b_ref, o_ref):
    u = u_ref[...]
    y = _gelu(y_ref[...].reshape(u.shape) + d_ref[...] * u)
    o_ref[...] = (y * _sigmoid(_dot(y.astype(BF16), w_ref[...]) + b_ref[...])).astype(o_ref.dtype)


def s5_post(y3, h, d_skip, w_glu, b_glu, tm=1024):
    rows, t, w = y3.shape
    n = rows * t
    tm = min(tm, n)
    row = lambda i: (i, 0)
    fixed = lambda i: (0, 0)
    return pl.pallas_call(
        _s5_post_kernel,
        grid=(n // tm,),
        in_specs=[pl.BlockSpec((tm // t, t, w), lambda i: (i, 0, 0)), pl.BlockSpec((tm, w), row),
                  pl.BlockSpec((1, w), fixed), pl.BlockSpec((w, w), fixed), pl.BlockSpec((1, w), fixed)],
        out_specs=pl.BlockSpec((tm, w), row),
        out_shape=jax.ShapeDtypeStruct((n, w), BF16),
        compiler_params=_params("parallel"),
        name="s5_post",
    )(y3, h, d_skip.reshape(1, w), w_glu.astype(BF16), b_glu.reshape(1, w))


def s5_mixer(h, u3, bsz, seq, lam_re, lam_im, log_dt, b_re, b_im, c_re, c_im, d_skip, w_glu, b_glu):
    tables = _s5_tables(lam_re, lam_im, log_dt, b_re, b_im, c_re, c_im, seq // S5_CHUNK)
    return s5_post(s5_scan(u3, bsz, *tables), h, d_skip, w_glu, b_glu)


def _rope_tables(pos, rot_dim, theta, head_dim, n_heads):
    half = rot_dim // 2
    f32 = np.float32
    inv_freq = f32(theta) ** (-np.arange(half, dtype=f32) / f32(half))
    ang = (pos.astype(f32)[:, None] * inv_freq[None, :]).astype(np.float64)
    cos, sin = np.cos(ang), np.sin(ang)
    rest = head_dim - rot_dim
    n = pos.shape[0]
    c = np.concatenate([cos, cos, np.ones((n, rest))], axis=1)
    s_up = np.concatenate([-sin, np.zeros((n, half + rest))], axis=1)
    s_dn = np.concatenate([np.zeros((n, half)), sin, np.zeros((n, rest))], axis=1)
    tile = lambda a: jnp.asarray(np.tile(a, (1, n_heads)), F32)
    return tile(c), tile(s_up), tile(s_dn)


def _rope_apply(x, c, s_up, s_dn, half):
    return x * c + pltpu.roll(x, LANES - half, 1) * s_up + pltpu.roll(x, half, 1) * s_dn


def _retention_tables():
    c = RET_CHUNK
    log_gamma = np.log(1.0 - 2.0 ** (-5.0 - np.arange(RET_HEADS, dtype=np.float64)))
    i = np.arange(c, dtype=np.float64)
    diff = i[:, None] - i[None, :]
    decay = np.where(diff >= 0, np.exp(diff[None] * log_gamma[:, None, None]), 0.0)
    qdec = np.repeat(np.exp((i + 1.0)[:, None] * log_gamma[None, :]), RET_DK, axis=1)
    kdec = np.repeat(np.exp((c - 1.0 - i)[:, None] * log_gamma[None, :]), RET_DK, axis=1)
    chunk_decay = [float(v) for v in np.exp(c * log_gamma)]
    return jnp.asarray(decay, F32), jnp.asarray(qdec, F32), jnp.asarray(kdec, F32), chunk_decay


def _retention_kernel(chunk_decay, q_ref, k_ref, v0_ref, v1_ref, g0_ref, g1_ref, c_ref, su_ref, sd_ref,
                      dec_ref, qdec_ref, kdec_ref, lng_ref, lnb_ref, o_ref, state_ref):
    @pl.when(pl.program_id(1) == 0)
    def _():
        state_ref[...] = jnp.zeros_like(state_ref)

    half = RET_DK // 2
    tabs = (c_ref[...], su_ref[...], sd_ref[...])
    q = jnp.concatenate([_rope_apply(q_ref[:, s:s + LANES], *tabs, half) for s in (0, LANES)], axis=1)
    k = jnp.concatenate([_rope_apply(k_ref[:, s:s + LANES], *tabs, half) for s in (0, LANES)], axis=1)
    k = k * (RET_DK ** -0.5)
    q_dec = q * qdec_ref[...]
    k_dec = k * kdec_ref[...]
    v = jnp.concatenate([v0_ref[...], v1_ref[...]], axis=1)
    gate = jnp.concatenate([g0_ref[...], g1_ref[...]], axis=1)
    states = [state_ref[h] for h in range(RET_HEADS)]

    def head(h):
        ks = slice(h * RET_DK, (h + 1) * RET_DK)
        vh = v[:, h * RET_DV:(h + 1) * RET_DV].astype(BF16)
        scores = _dot_nt(q[:, ks].astype(BF16), k[:, ks].astype(BF16)) * dec_ref[h]
        yield
        y = _dot(scores.astype(BF16), vh) + _dot(q_dec[:, ks].astype(BF16), states[h].astype(BF16))
        yield
        new_state = states[h] * chunk_decay[h] + _dot_tn(k_dec[:, ks].astype(BF16), vh)
        yield
        mu = jnp.mean(y, axis=-1, keepdims=True)
        yc = y - mu
        var = jnp.mean(yc * yc, axis=-1, keepdims=True)
        return yc * lax.rsqrt(var + RET_GN_EPS), new_state

    results = _run_interleaved([head(h) for h in range(RET_HEADS)])
    outs = [r[0] for r in results]
    for h, (_, new_state) in enumerate(results):
        state_ref[h] = new_state
    yn = jnp.concatenate(outs, axis=1) * lng_ref[...] + lnb_ref[...]
    o_ref[...] = (gate * _sigmoid(gate) * yn).astype(o_ref.dtype)


def retention_mixer(h, bsz, seq, col0, ln_g, ln_b):
    c = RET_CHUNK
    n_chunk = seq // c
    qk_w = RET_HEADS * RET_DK
    v_w = RET_HEADS * RET_DV
    assert col0 % qk_w == 0 and qk_w == 2 * LANES and v_w == 2 * qk_w
    cb = col0 // qk_w
    rc, rsu, rsd = _rope_tables(np.arange(seq), RET_DK, RET_THETA, RET_DK, 2)
    dec, qdec, kdec, chunk_decay = _retention_tables()
    row = lambda j: (lambda b, n: (b * n_chunk + n, j))
    pos = lambda b, n: (n, 0)
    fixed2 = lambda b, n: (0, 0)
    kern = functools.partial(_retention_kernel, chunk_decay)
    return pl.pallas_call(
        kern,
        grid=(bsz, n_chunk),
        in_specs=[pl.BlockSpec((c, qk_w), row(cb)), pl.BlockSpec((c, qk_w), row(cb + 1)),
                  pl.BlockSpec((c, qk_w), row(cb + 2)), pl.BlockSpec((c, qk_w), row(cb + 3)),
                  pl.BlockSpec((c, qk_w), row(cb + 4)), pl.BlockSpec((c, qk_w), row(cb + 5)),
                  pl.BlockSpec((c, LANES), pos), pl.BlockSpec((c, LANES), pos), pl.BlockSpec((c, LANES), pos),
                  pl.BlockSpec((RET_HEADS, c, c), lambda b, n: (0, 0, 0)),
                  pl.BlockSpec((c, qk_w), fixed2), pl.BlockSpec((c, qk_w), fixed2),
                  pl.BlockSpec((1, v_w), fixed2), pl.BlockSpec((1, v_w), fixed2)],
        out_specs=pl.BlockSpec((c, v_w), lambda b, n: (b * n_chunk + n, 0)),
        out_shape=jax.ShapeDtypeStruct((bsz * seq, v_w), BF16),
        scratch_shapes=[pltpu.VMEM((RET_HEADS, RET_DK, RET_DV), F32)],
        compiler_params=_params("parallel", "arbitrary"),
        name="retention",
    )(h, h, h, h, h, h, rc, rsu, rsd, dec, qdec, kdec, ln_g.reshape(1, v_w), ln_b.reshape(1, v_w))


NSA_KV_W = NSA_KV_GROUPS * NSA_HEAD_DIM
NSA_VT_ROWS = NSA_HEAD_DIM + 16
NSA_GATE_COLS = 3 * NSA_HEADS


def _nsa_prep_kernel(q_ref, kvc_ref, kvs_ref, kvw_ref, c_ref, su_ref, sd_ref,
                     qo_ref, kc_ref, vc_ref, ks_ref, vs_ref, kw_ref, vw_ref):
    half = NSA_ROT_DIM // 2
    tabs = (c_ref[...], su_ref[...], sd_ref[...])
    scale = NSA_HEAD_DIM ** -0.5 * math.log2(math.e)
    q = jnp.concatenate(
        [_rope_apply(q_ref[:, s:s + LANES], *tabs, half) * scale for s in range(0, NSA_WIDTH, LANES)], axis=1)
    qo_ref[0] = q.T.astype(qo_ref.dtype)

    def split(x, o_ref):
        for g in range(NSA_KV_GROUPS):
            o_ref[0, g] = x[:, g * NSA_HEAD_DIM:(g + 1) * NSA_HEAD_DIM].astype(o_ref.dtype)

    def split_t(x, o_ref):
        xt = x.T
        for g in range(NSA_KV_GROUPS):
            o_ref[0, g, :NSA_HEAD_DIM] = xt[g * NSA_HEAD_DIM:(g + 1) * NSA_HEAD_DIM, :].astype(o_ref.dtype)
            o_ref[0, g, NSA_HEAD_DIM:] = jnp.ones((NSA_VT_ROWS - NSA_HEAD_DIM, xt.shape[1]), o_ref.dtype)

    split(kvc_ref[:, :NSA_KV_W], kc_ref)
    split(kvc_ref[:, NSA_KV_W:], vc_ref)
    n_hot = ks_ref.shape[3] - NSA_HEAD_DIM
    tl = q_ref.shape[0]
    blk = (pl.program_id(1) * tl + lax.broadcasted_iota(jnp.int32, (tl, n_hot), 0)) // SLC_BLOCK
    one_hot = jnp.where(blk % n_hot == lax.broadcasted_iota(jnp.int32, (tl, n_hot), 1), 1.0, 0.0)
    ks = _rope_apply(kvs_ref[:, :NSA_KV_W], *tabs, half)
    for g in range(NSA_KV_GROUPS):
        ks_ref[0, g] = jnp.concatenate([ks[:, g * NSA_HEAD_DIM:(g + 1) * NSA_HEAD_DIM], one_hot],
                                       axis=1).astype(ks_ref.dtype)
    split_t(kvs_ref[:, NSA_KV_W:], vs_ref)
    split(_rope_apply(kvw_ref[:, :NSA_KV_W], *tabs, half), kw_ref)
    split_t(kvw_ref[:, NSA_KV_W:], vw_ref)


def nsa_prep(h, bsz, seq, tl=512):
    tl = min(tl, seq)
    nl = seq // tl
    rc, rsu, rsd = _rope_tables(np.arange(seq), NSA_ROT_DIM, ROPE_THETA, NSA_HEAD_DIM, LANES // NSA_HEAD_DIM)
    row = lambda j: (lambda b, l: (b * nl + l, j))
    pos = lambda b, l: (l, 0)
    kv_out = pl.BlockSpec((1, NSA_KV_GROUPS, tl, NSA_HEAD_DIM), lambda b, l: (b, 0, l, 0))
    kv_shape = lambda dt: jax.ShapeDtypeStruct((bsz, NSA_KV_GROUPS, seq, NSA_HEAD_DIM), dt)
    vt_out = pl.BlockSpec((1, NSA_KV_GROUPS, NSA_VT_ROWS, tl), lambda b, l: (b, 0, 0, l))
    vt_shape = jax.ShapeDtypeStruct((bsz, NSA_KV_GROUPS, NSA_VT_ROWS, seq), BF16)
    ks_w = NSA_HEAD_DIM + min(NSA_KEY_TILE, seq) // SLC_BLOCK
    ks_out = pl.BlockSpec((1, NSA_KV_GROUPS, tl, ks_w), lambda b, l: (b, 0, l, 0))
    ks_shape = jax.ShapeDtypeStruct((bsz, NSA_KV_GROUPS, seq, ks_w), BF16)
    two = 2 * NSA_KV_W
    return pl.pallas_call(
        _nsa_prep_kernel,
        grid=(bsz, nl),
        in_specs=[pl.BlockSpec((tl, NSA_WIDTH), row(1)),
                  pl.BlockSpec((tl, two), row(4)), pl.BlockSpec((tl, two), row(5)), pl.BlockSpec((tl, two), row(6)),
                  pl.BlockSpec((tl, LANES), pos), pl.BlockSpec((tl, LANES), pos), pl.BlockSpec((tl, LANES), pos)],
        out_specs=[pl.BlockSpec((1, NSA_WIDTH, tl), lambda b, l: (b, 0, l)),
                   kv_out, kv_out, ks_out, vt_out, kv_out, vt_out],
        out_shape=[jax.ShapeDtypeStruct((bsz, NSA_WIDTH, seq), BF16),
                   kv_shape(F32), kv_shape(F32), ks_shape, vt_shape, kv_shape(BF16), vt_shape],
        compiler_params=_params("parallel", "parallel"),
        name="nsa_prep",
    )(h, h, h, h, rc, rsu, rsd)


def _nsa_compress_kernel(hk_ref, hv_ref, pek_ref, pev_ref, kw1_ref, kb1_ref, kw2_ref, vw1_ref, vb1_ref, vw2_ref,
                         c_ref, su_ref, sd_ref, ko_ref, vo_ref):
    def mlp(h_ref, pe_ref, w1_ref, b1_ref, w2_ref):
        hb = h_ref[0, 0]
        rows = hb.shape[0]
        first = _dot((hb + pe_ref[0:1, :]).astype(BF16), w1_ref[0])
        second = _dot((hb + pe_ref[1:2, :]).astype(BF16), w1_ref[1])
        hid = _gelu(first + pltpu.roll(second, rows - 1, 0) + b1_ref[...])
        return _dot(hid.astype(BF16), w2_ref[...])

    kc = _rope_apply(mlp(hk_ref, pek_ref, kw1_ref, kb1_ref, kw2_ref), c_ref[...], su_ref[...], sd_ref[...],
                     NSA_ROT_DIM // 2)
    vc = mlp(hv_ref, pev_ref, vw1_ref, vb1_ref, vw2_ref)
    ko_ref[0, 0] = kc[:, :NSA_HEAD_DIM].astype(ko_ref.dtype)
    vo_ref[0, 0, :NSA_HEAD_DIM] = vc.T[:NSA_HEAD_DIM, :].astype(vo_ref.dtype)
    vo_ref[0, 0, NSA_HEAD_DIM:] = jnp.ones((NSA_VT_ROWS - NSA_HEAD_DIM, vc.shape[0]), vo_ref.dtype)


def nsa_compress(kc, vc, pe_k, pe_v, ck_w1, ck_b1, ck_w2, cv_w1, cv_b1, cv_w2):
    bsz, grp, seq, d = kc.shape
    n_rows = seq // CMP_STRIDE
    flat = CMP_STRIDE * d
    cmp_end = np.arange(n_rows) * CMP_STRIDE + CMP_BLOCK - 1
    rc, rsu, rsd = _rope_tables(cmp_end, NSA_ROT_DIM, ROPE_THETA, NSA_HEAD_DIM, LANES // NSA_HEAD_DIM)
    pad_w2 = lambda w: jnp.pad(w, ((0, 0), (0, LANES - d))).astype(BF16)
    blk = pl.BlockSpec((1, 1, n_rows, flat), lambda b, g: (b, g, 0, 0))
    f2 = lambda b, g: (0, 0)
    f3 = lambda b, g: (0, 0, 0)
    w_specs = [pl.BlockSpec((2, flat, CMP_HIDDEN), f3), pl.BlockSpec((1, CMP_HIDDEN), f2),
               pl.BlockSpec((CMP_HIDDEN, LANES), f2)]
    return pl.pallas_call(
        _nsa_compress_kernel,
        grid=(bsz, grp),
        in_specs=[blk, blk, pl.BlockSpec((2, flat), f2), pl.BlockSpec((2, flat), f2)] + w_specs + w_specs
                 + [pl.BlockSpec((n_rows, LANES), f2)] * 3,
        out_specs=[pl.BlockSpec((1, 1, n_rows, d), lambda b, g: (b, g, 0, 0)),
                   pl.BlockSpec((1, 1, NSA_VT_ROWS, n_rows), lambda b, g: (b, g, 0, 0))],
        out_shape=[jax.ShapeDtypeStruct((bsz, grp, n_rows, d), BF16),
                   jax.ShapeDtypeStruct((bsz, grp, NSA_VT_ROWS, n_rows), BF16)],
        compiler_params=_params("parallel", "parallel"),
        name="nsa_compress",
    )(kc.reshape(bsz, grp, n_rows, flat), vc.reshape(bsz, grp, n_rows, flat),
      pe_k.reshape(2, flat), pe_v.reshape(2, flat),
      ck_w1.reshape(2, flat, CMP_HIDDEN).astype(BF16), ck_b1.reshape(1, CMP_HIDDEN), pad_w2(ck_w2),
      cv_w1.reshape(2, flat, CMP_HIDDEN).astype(BF16), cv_b1.reshape(1, CMP_HIDDEN), pad_w2(cv_w2),
      rc, rsu, rsd)


def _per_head(x):
    return jnp.concatenate([x] * NSA_HPG, axis=1)


def _nsa_attn_kernel(seq, tk, qt_ref, gate_ref, kc_ref, vct_ref, ks_ref, vst_ref, kw_ref, vwt_ref, mmapt_ref,
                     o_ref, sel_ref):
    n_blk = seq // SLC_BLOCK
    n_sel = min(N_SLC, n_blk)
    hd = NSA_HEAD_DIM
    w = NSA_HPG * hd
    groups = range(NSA_KV_GROUPS)
    q0 = pl.program_id(1) * Q_BLOCK
    t_l = q0 + lax.broadcasted_iota(jnp.int32, (1, Q_BLOCK), 1)

    def select(g):
        qg = qt_ref[0, g * w:(g + 1) * w, :]
        qst = jnp.concatenate([qg[h * hd:(h + 1) * hd, :] for h in range(NSA_HPG)], axis=1)
        kc = kc_ref[0, g]
        n_cmp = kc.shape[0]
        cmp_end = lax.broadcasted_iota(jnp.int32, (n_cmp, 1), 0) * CMP_STRIDE + (CMP_BLOCK - 1)
        s = _dot(kc, qst) + _per_head(jnp.where(cmp_end <= t_l, 0.0, MASK_VALUE))
        yield
        p = jnp.exp2(s - jnp.max(s, axis=0, keepdims=True))
        any_key = _per_head(jnp.where(t_l >= CMP_BLOCK - 1, 1.0, 0.0))
        pv = _dot(vct_ref[0, g], p.astype(BF16))
        inv_l = any_key / pv[hd:hd + 1]
        o_cmp = pv[:hd] * inv_l
        yield
        p = p * inv_l
        imp = p[:, 0:Q_BLOCK]
        for h in range(1, NSA_HPG):
            imp = imp + p[:, h * Q_BLOCK:(h + 1) * Q_BLOCK]
        imp_slc = _dot_split(mmapt_ref[...], imp, 'b', 3)
        yield
        blk = lax.broadcasted_iota(jnp.int32, (n_blk, 1), 0)
        cur = t_l // SLC_BLOCK
        score = jnp.where(blk == 0, FORCE_SCORE,
                          jnp.where(blk == cur, FORCE_SCORE, jnp.where(blk == cur - 1, FORCE_SCORE, imp_slc)))
        score = jnp.where(blk * SLC_BLOCK <= t_l, score, -FORCE_SCORE)
        sel = jnp.zeros((n_blk, Q_BLOCK), F32)
        for _ in range(n_sel):
            best = jnp.max(score, axis=0, keepdims=True)
            idx = jnp.min(jnp.where(score == best, blk, n_blk), axis=0, keepdims=True)
            pick = blk == idx
            sel = jnp.where(pick, 1.0, sel)
            score = jnp.where(pick, -jnp.inf, score)
            yield
        sel_ref[g] = sel
        return qst, o_cmp

    selected = _run_interleaved([select(g) for g in groups])
    qst = [r[0] for r in selected]
    o_cmp = [r[1] for r in selected]

    blocks_per_tile = tk // SLC_BLOCK
    assert ks_ref.shape[3] == hd + blocks_per_tile and Q_BLOCK <= tk and tk % Q_BLOCK == 0

    def slc_tile(kt, carry, causal_bias=None):
        k0 = pl.multiple_of(kt * tk, tk)
        out = []
        for g in groups:
            m, acc = carry[g]
            sel_rows = sel_ref[g, pl.ds(pl.multiple_of(kt * blocks_per_tile, blocks_per_tile), blocks_per_tile), :]
            q_aug = jnp.concatenate([qst[g], _per_head((sel_rows - 1.0) * -MASK_VALUE).astype(BF16)], axis=0)
            s = _dot(ks_ref[0, g, pl.ds(k0, tk), :], q_aug)
            if causal_bias is not None:
                s = s + causal_bias
            m_new = jnp.maximum(m, jnp.max(s, axis=0, keepdims=True))
            alpha = jnp.exp2(m - m_new)
            p = jnp.exp2(s - m_new)
            acc = alpha * acc + _dot(vst_ref[0, g, :, pl.ds(k0, tk)], p.astype(BF16))
            out.append((m_new, acc))
        return tuple(out)

    n_full = q0 // tk
    cols = NSA_HPG * Q_BLOCK
    init = tuple((jnp.full((1, cols), MASK_VALUE, F32), jnp.zeros((NSA_VT_ROWS, cols), F32)) for _ in groups)
    slc = lax.fori_loop(0, n_full, slc_tile, init)
    kpos = n_full * tk + lax.broadcasted_iota(jnp.int32, (tk, 1), 0)
    slc = slc_tile(n_full, slc, _per_head(jnp.where(kpos <= t_l, 0.0, MASK_VALUE)))

    band = WINDOW + Q_BLOCK
    w0 = pl.multiple_of(jnp.maximum(q0 - WINDOW, 0), Q_BLOCK)
    kpos = w0 + lax.broadcasted_iota(jnp.int32, (band, 1), 0)
    win_bias = _per_head(jnp.where(kpos <= t_l, jnp.where(kpos > t_l - WINDOW, 0.0, MASK_VALUE), MASK_VALUE))
    sig_t = _sigmoid(gate_ref[...]).T

    def finish(g):
        s = _dot(kw_ref[0, g, pl.ds(w0, band), :], qst[g]) + win_bias
        yield
        p = jnp.exp2(s - jnp.max(s, axis=0, keepdims=True))
        pv = _dot(vwt_ref[0, g, :, pl.ds(w0, band)], p.astype(BF16))
        o_win = pv[:hd] / pv[hd:hd + 1]
        yield
        acc = slc[g][1]
        acc_slc, l_slc = acc[:hd], acc[hd:hd + 1]

        def gate(branch):
            first = (g * NSA_HPG) * 3 + branch
            return jnp.concatenate([sig_t[first + 3 * h:first + 3 * h + 1, :] for h in range(NSA_HPG)], axis=1)

        out_t = gate(0) * o_cmp[g] + gate(1) * (acc_slc / l_slc) + gate(2) * o_win
        pairs = []
        for h in range(0, NSA_HPG, 2):
            two = jnp.concatenate([out_t[:, h * Q_BLOCK:(h + 1) * Q_BLOCK],
                                   out_t[:, (h + 1) * Q_BLOCK:(h + 2) * Q_BLOCK]], axis=0)
            pairs.append(two.T)
        o_ref[:, g * w:(g + 1) * w] = jnp.concatenate(pairs, axis=1).astype(o_ref.dtype)

    _run_interleaved([finish(g) for g in groups])


def _nsa_pool_matrix(seq):
    n_blk = seq // SLC_BLOCK
    n_rows = seq // CMP_STRIDE
    per_stride = SLC_BLOCK // CMP_STRIDE
    span = CMP_BLOCK // CMP_STRIDE
    pool = np.zeros((n_blk, n_rows), np.float32)
    for j in range(n_blk):
        for m in range(per_stride):
            for n in range(span):
                c = per_stride * j + m + n - (span - 1)
                if 0 <= c < n_rows - 1:
                    pool[j, c] += 1.0
    return jnp.asarray(pool, BF16)


def nsa_attention(qt, h, gate_col_block, k_cmp, v_cmp_t, ks, vs_t, kw, vw_t, bsz, seq):
    tk = min(NSA_KEY_TILE, seq)
    nq = seq // Q_BLOCK
    pool = _nsa_pool_matrix(seq)
    n_rows = k_cmp.shape[2]
    d = NSA_HEAD_DIM
    qblk = lambda b, i: (b * nq + i, 0)
    whole = lambda *shape: pl.BlockSpec((1, NSA_KV_GROUPS) + shape, lambda b, i: (b, 0, 0, 0))
    kern = functools.partial(_nsa_attn_kernel, seq, tk)
    return pl.pallas_call(
        kern,
        grid=(bsz, nq),
        in_specs=[pl.BlockSpec((1, NSA_WIDTH, Q_BLOCK), lambda b, i: (b, 0, i)),
                  pl.BlockSpec((Q_BLOCK, LANES), lambda b, i: (b * nq + i, gate_col_block)),
                  whole(n_rows, d), whole(NSA_VT_ROWS, n_rows), whole(seq, ks.shape[3]), whole(NSA_VT_ROWS, seq),
                  whole(seq, d), whole(NSA_VT_ROWS, seq),
                  pl.BlockSpec(pool.shape, lambda b, i: (0, 0))],
        out_specs=pl.BlockSpec((Q_BLOCK, NSA_WIDTH), qblk),
        out_shape=jax.ShapeDtypeStruct((bsz * seq, NSA_WIDTH), BF16),
        scratch_shapes=[pltpu.VMEM((NSA_KV_GROUPS, seq // SLC_BLOCK, Q_BLOCK), F32)],
        compiler_params=_params("parallel", "arbitrary"),
        name="nsa_attention",
    )(qt, h, k_cmp, v_cmp_t, ks, vs_t, kw, vw_t, pool)


def nsa_mixer(h, bsz, seq, gate_col_block, pe_k, pe_v, ck_w1, ck_b1, ck_w2, cv_w1, cv_b1, cv_w2):
    qt, kc, vc, ks, vs_t, kw, vw_t = nsa_prep(h, bsz, seq)
    k_cmp, v_cmp_t = nsa_compress(kc, vc, pe_k, pe_v, ck_w1, ck_b1, ck_w2, cv_w1, cv_b1, cv_w2)
    return nsa_attention(qt, h, gate_col_block, k_cmp, v_cmp_t, ks, vs_t, kw, vw_t, bsz, seq)


def _head_ones(width, head_dim):
    idx = np.arange(width) // head_dim
    return jnp.asarray(idx[:, None] == idx[None, :], BF16)


def _softplus(x):
    return jnp.maximum(x, 0.0) + jnp.log(1.0 + jnp.exp(-jnp.abs(x)))


def _rwkv_pre_kernel(p_ref, prev_ref, mu_ref, w0_ref, wup_ref, a0_ref, aup_ref, gup_ref, kk_ref, ka_ref, rk_ref,
                     ones_ref, r_o, k_o, v_o, kk_o, b_o, ld_o, g_o, bonus_o):
    w = RWKV_WIDTH
    p = p_ref[...]
    first_row = jnp.where(pl.program_id(1) == 0, 0.0, prev_ref[7:8, :])
    is_row0 = lax.broadcasted_iota(jnp.int32, p.shape, 0) == 0
    prev = jnp.where(is_row0, first_row, pltpu.roll(p, 1, 0))
    ps = p + (prev - p) * mu_ref[...]
    r, k, v = ps[:, 0:w], ps[:, w:2 * w], ps[:, 2 * w:3 * w]
    o = 3 * w
    w_lo = ps[:, o:o + RWKV_LORA_W]
    a_lo = ps[:, o + RWKV_LORA_W:o + RWKV_LORA_W + RWKV_LORA_A]
    g_lo = ps[:, o + RWKV_LORA_W + RWKV_LORA_A:]
    wlog = -_softplus(-(w0_ref[...] + _dot(jnp.tanh(w_lo).astype(BF16), wup_ref[...]))) - 0.5
    a = _sigmoid(a0_ref[...] + _dot(a_lo.astype(BF16), aup_ref[...]))
    g = _dot(_sigmoid(g_lo).astype(BF16), gup_ref[...])
    kk = k * kk_ref[...]
    norm = jnp.sqrt(_dot_split(kk * kk, ones_ref[...], 'a', 2))
    kk = kk / jnp.maximum(norm, 1e-12)
    k2 = k * (1.0 + (a - 1.0) * ka_ref[...])
    r_o[...] = r
    k_o[...] = k2
    v_o[...] = v
    kk_o[...] = kk
    b_o[...] = kk * a
    ld_o[...] = -jnp.exp(wlog)
    g_o[...] = g
    bonus_o[...] = _dot_split(r * k2 * rk_ref[...], ones_ref[...], 'a', 2) * v


def rwkv_pre(h, bsz, seq, mu, w0, w_up, a0, a_up, g_up, k_k, k_a, r_k, tl=512):
    tl = min(tl, seq)
    nl = seq // tl
    w = RWKV_WIDTH
    cols = RWKV_COLS
    ones = _head_ones(w, RWKV_HEAD_DIM)
    f2 = lambda b, l: (0, 0)
    vec = pl.BlockSpec((1, w), f2)
    out_spec = pl.BlockSpec((tl, w), lambda b, l: (b * nl + l, 0))
    out_shape = jax.ShapeDtypeStruct((bsz * seq, w), F32)
    return pl.pallas_call(
        _rwkv_pre_kernel,
        grid=(bsz, nl),
        in_specs=[pl.BlockSpec((tl, cols), lambda b, l: (b * nl + l, 0)),
                  pl.BlockSpec((8, cols), lambda b, l: (jnp.maximum((b * seq + l * tl) // 8 - 1, 0), 0)),
                  pl.BlockSpec((1, cols), f2), vec, pl.BlockSpec((RWKV_LORA_W, w), f2),
                  vec, pl.BlockSpec((RWKV_LORA_A, w), f2), pl.BlockSpec((RWKV_LORA_G, w), f2),
                  vec, vec, vec, pl.BlockSpec((w, w), f2)],
        out_specs=[out_spec] * 8,
        out_shape=[out_shape] * 8,
        compiler_params=_params("parallel", "parallel"),
        name="rwkv_pre",
    )(h, h, mu.reshape(1, cols), w0.reshape(1, w), w_up.astype(BF16), a0.reshape(1, w), a_up.astype(BF16),
      g_up.astype(BF16), k_k.reshape(1, w), k_a.reshape(1, w), r_k.reshape(1, w), ones)


def _rwkv_masks():
    t, pk = RWKV_CHUNK, RWKV_PACK
    n = t * pk
    ri = np.arange(n)
    same = (ri[:, None] // t) == (ri[None, :] // t)
    tt, ss = ri[:, None] % t, ri[None, :] % t
    levels = []
    k = 1
    while k < t:
        levels.append(same & (tt // (2 * k) == ss // (2 * k)) & ((tt // k) % 2 == 1) & ((ss // k) % 2 == 0))
        k *= 2
    lvl = np.stack(levels).astype(np.float32)
    tri = (np.arange(t)[:, None] >= np.arange(t)[None, :]).astype(np.float32)
    head_lane = ((ri[:, None] // t) == (np.arange(pk * RWKV_HEAD_DIM)[None, :] // RWKV_HEAD_DIM)).astype(np.float32)
    return (jnp.asarray(tri, BF16), jnp.asarray(head_lane), jnp.asarray(same.astype(np.float32)), jnp.asarray(lvl))


def _rwkv_chain(r, k, v, kk, b, ld, st, tri, hl, bd, lvl_ref):
    t, pk = RWKV_CHUNK, RWKV_PACK
    n = t * pk
    c = _dot_split(tri, ld, 'b', 3)
    yield
    c_end = c[t - 1:t, :]
    e_neg = jnp.exp(-c)
    e_end = jnp.exp(c_end - c)
    kkd = (kk * jnp.exp(c - ld)).astype(BF16)
    rd = (r * jnp.exp(c)).astype(BF16)

    def big(x):
        return (jnp.concatenate([x] * pk, axis=0) * hl).astype(BF16)

    st_b = st.astype(BF16)
    v_big = big(v)
    a_all = _dot_nt(jnp.concatenate([kkd, rd], axis=0),
                    jnp.concatenate([big(k * e_neg), big(b * e_neg)], axis=0))
    yield
    ti = lax.broadcasted_iota(jnp.int32, (t, n), 0)
    si = lax.broadcasted_iota(jnp.int32, (t, n), 1) % t
    strict = ti > si
    incl = ti >= si
    a_kk = jnp.where(strict, a_all[:t, :n], 0.0)
    a_kb = jnp.where(strict, a_all[:t, n:], 0.0)
    a_rk = jnp.where(incl, a_all[t:, :n], 0.0)
    a_rb = jnp.where(incl, a_all[t:, n:], 0.0)
    rhs = _dot(kkd, st_b) + _dot(a_kk.astype(BF16), v_big)
    yield
    a_bd = jnp.concatenate([a_kb] * pk, axis=0) * bd
    m = jnp.where(lax.broadcasted_iota(jnp.int32, (n, n), 0) == lax.broadcasted_iota(jnp.int32, (n, n), 1), 1.0, 0.0)
    for lv in range(lvl_ref.shape[0]):
        mb = m.astype(BF16)
        ma = _dot(mb, (a_bd * lvl_ref[lv]).astype(BF16)).astype(BF16)
        yield
        m = m - _dot(ma, mb)
        yield
    u_big = _dot(m.astype(BF16), big(rhs))
    yield
    u = u_big[0:t]
    for h in range(1, pk):
        u = u + u_big[h * t:(h + 1) * t]
    y = _dot(rd, st_b) + _dot(a_rk.astype(BF16), v_big) - _dot(a_rb.astype(BF16), big(u))
    yield
    decay_col = jnp.broadcast_to(jnp.exp(c_end), st.shape).T
    kb_end = jnp.concatenate([k * e_end, -(b * e_end)], axis=0).astype(BF16)
    vu = jnp.concatenate([v, u], axis=0).astype(BF16)
    return y, decay_col * st + bd * _dot_tn(kb_end, vu)


def _rwkv_chunk_kernel(r_ref, k_ref, v_ref, kk_ref, b_ref, ld_ref, tri_ref, hl_ref, bd_ref, lvl_ref, y_ref, st_ref):
    @pl.when(pl.program_id(0) == 0)
    def _():
        st_ref[...] = jnp.zeros_like(st_ref)

    wp = RWKV_PACK * RWKV_HEAD_DIM
    tri, hl, bd = tri_ref[...], hl_ref[...], bd_ref[...]
    n_pack = r_ref.shape[2] // wp
    where = [(bi, slice(g * wp, (g + 1) * wp)) for bi in range(r_ref.shape[0]) for g in range(n_pack)]
    loaded = [tuple(ref[bi, :, cols] for ref in (r_ref, k_ref, v_ref, kk_ref, b_ref, ld_ref)) + (st_ref[i],)
              for i, (bi, cols) in enumerate(where)]
    results = _run_interleaved([_rwkv_chain(*args, tri, hl, bd, lvl_ref) for args in loaded])
    for i, ((bi, cols), (y, st_new)) in enumerate(zip(where, results)):
        y_ref[bi, :, cols] = y
        st_ref[i] = st_new


def rwkv_chunk(r, k, v, kk, b, ld, bsz, seq):
    t, pk = RWKV_CHUNK, RWKV_PACK
    n_chunk = seq // t
    w = RWKV_WIDTH
    wp = pk * RWKV_HEAD_DIM
    assert t == RWKV_HEAD_DIM
    tri, hl, bd, lvl = _rwkv_masks()
    blk = pl.BlockSpec((bsz, t, w), lambda c: (0, c, 0))
    f2 = lambda c: (0, 0)
    shaped = lambda a: a.reshape(bsz, seq, w)
    y = pl.pallas_call(
        _rwkv_chunk_kernel,
        grid=(n_chunk,),
        in_specs=[blk] * 6 + [pl.BlockSpec(tri.shape, f2), pl.BlockSpec(hl.shape, f2), pl.BlockSpec(bd.shape, f2),
                              pl.BlockSpec(lvl.shape, lambda c: (0, 0, 0))],
        out_specs=blk,
        out_shape=jax.ShapeDtypeStruct((bsz, seq, w), F32),
        scratch_shapes=[pltpu.VMEM((bsz * (w // wp), wp, wp), F32)],
        compiler_params=_params("arbitrary"),
        name="rwkv_chunk",
    )(shaped(r), shaped(k), shaped(v), shaped(kk), shaped(b), shaped(ld), tri, hl, bd, lvl)
    return y.reshape(bsz * seq, w)


def _rwkv_post_kernel(y_ref, bonus_ref, g_ref, lng_ref, lnb_ref, ones_ref, o_ref):
    y = y_ref[...]
    inv = 1.0 / RWKV_HEAD_DIM
    mu = _dot_split(y, ones_ref[...], 'a', 2) * inv
    yc = y - mu
    var = _dot_split(yc * yc, ones_ref[...], 'a', 2) * inv
    yn = yc * lax.rsqrt(var + RWKV_GN_EPS) * lng_ref[...] + lnb_ref[...]
    o_ref[...] = ((yn + bonus_ref[...]) * g_ref[...]).astype(o_ref.dtype)


def rwkv_post(y, bonus, g, ln_g, ln_b, tm=1024):
    n, w = y.shape
    tm = min(tm, n)
    row = pl.BlockSpec((tm, w), lambda i: (i, 0))
    vec = pl.BlockSpec((1, w), lambda i: (0, 0))
    return pl.pallas_call(
        _rwkv_post_kernel,
        grid=(n // tm,),
        in_specs=[row, row, row, vec, vec, pl.BlockSpec((w, w), lambda i: (0, 0))],
        out_specs=row,
        out_shape=jax.ShapeDtypeStruct((n, w), BF16),
        compiler_params=_params("parallel"),
        name="rwkv_post",
    )(y, bonus, g, ln_g.reshape(1, w), ln_b.reshape(1, w), _head_ones(w, RWKV_HEAD_DIM))


def rwkv7_mixer(h, bsz, seq, mu, w0, w_up, a0, a_up, g_up, k_k, k_a, r_k, ln_g, ln_b):
    r, k, v, kk, b, ld, g, bonus = rwkv_pre(h, bsz, seq, mu, w0, w_up, a0, a_up, g_up, k_k, k_a, r_k)
    y = rwkv_chunk(r, k, v, kk, b, ld, bsz, seq)
    return rwkv_post(y, bonus, g, ln_g, ln_b)


AB_IN = S5_WIDTH + NSA_WIDTH + 6 * NSA_KV_W + NSA_GATE_COLS
AB_IN_PADDED = -(-AB_IN // LANES) * LANES
NSA_GATE_COL_BLOCK = (AB_IN - NSA_GATE_COLS) // LANES
PROJ_TM = 512


def kernel(x, ab_w_in, ab_w_out, s5_lam_re, s5_lam_im, s5_log_dt, s5_b_re, s5_b_im, s5_c_re, s5_c_im, s5_d, s5_w_glu, s5_b_glu, nsa_pe_k, nsa_pe_v, nsa_ck_w1, nsa_ck_b1, nsa_ck_w2, nsa_cv_w1, nsa_cv_b1, nsa_cv_w2, cd_w_in, cd_w_out, rwkv_mu, rwkv_w0, rwkv_w_up, rwkv_a0, rwkv_a_up, rwkv_g_up, rwkv_k_k, rwkv_k_a, rwkv_r_k, rwkv_ln_g, rwkv_ln_b, ret_ln_g, ret_ln_b, ln1_g, ln1_b, ln2_g, ln2_b, moe_router, moe_bias, moe_w1, moe_w3, moe_w2, sh_w1, sh_w3, sh_w2):
    bsz, seq, d = x.shape
    assert (AB_IN - NSA_GATE_COLS) % LANES == 0
    xf = x.reshape(bsz * seq, d)
    x_in = xf
    for layer in range(DEPTH):
        i = layer // 2
        if layer % 2 == 0:
            w_in = jnp.pad(ab_w_in[i], ((0, 0), (0, AB_IN_PADDED - AB_IN))).astype(BF16)
            h, u3 = project(x_in, w_in, PROJ_TM, chunked=(S5_CHUNK, S5_WIDTH))
            y_1 = s5_mixer(h, u3, bsz, seq, s5_lam_re[i], s5_lam_im[i], s5_log_dt[i], s5_b_re[i], s5_b_im[i],
                           s5_c_re[i], s5_c_im[i], s5_d[i], s5_w_glu[i], s5_b_glu[i])
            y_2 = nsa_mixer(h, bsz, seq, NSA_GATE_COL_BLOCK, nsa_pe_k[i], nsa_pe_v[i], nsa_ck_w1[i], nsa_ck_b1[i],
                            nsa_ck_w2[i], nsa_cv_w1[i], nsa_cv_b1[i], nsa_cv_w2[i])
            w_out = ab_w_out[i]
        else:
            h = project(x_in, cd_w_in[i].astype(BF16), PROJ_TM)
            y_1 = rwkv7_mixer(h, bsz, seq, rwkv_mu[i], rwkv_w0[i], rwkv_w_up[i], rwkv_a0[i], rwkv_a_up[i],
                              rwkv_g_up[i], rwkv_k_k[i], rwkv_k_a[i], rwkv_r_k[i], rwkv_ln_g[i], rwkv_ln_b[i])
            y_2 = retention_mixer(h, bsz, seq, RWKV_COLS, ret_ln_g[i], ret_ln_b[i])
            w_out = cd_w_out[i]
        xf, gates = out_proj_ln_route(y_1, y_2, w_out, xf, ln1_g[layer], ln1_b[layer],
                                      moe_router[layer], moe_bias[layer])
        xf, x_in = moe_experts_ln(xf, gates, moe_w1[layer], moe_w3[layer], moe_w2, layer,
                                  sh_w1[layer], sh_w3[layer], sh_w2[layer], ln2_g[layer], ln2_b[layer])
    return xf.reshape(bsz, seq, d)
```

```python
import functools
import math

import jax
import jax.numpy as jnp
import numpy as np
from jax import lax
from jax.experimental import pallas as pl
from jax.experimental.pallas import tpu as pltpu

F32 = jnp.float32
BF16 = jnp.bfloat16
HIGHEST = lax.Precision.HIGHEST
FP8 = jnp.float8_e4m3fn
FP8_MAX = 448.0
FP8_TINY = 1e-30

VMEM_LIMIT_BYTES = 52 * 1024 * 1024
LANES = 128

LN_EPS = 1e-5
DEPTH = 2
ALPHA = (2 * DEPTH) ** 0.25

S5_GROUPS, S5_GROUP_CH, S5_STATE = 32, 16, 64
S5_WIDTH = S5_GROUPS * S5_GROUP_CH
S5_CHUNK = 16
S5_PACK = 8
NSA_HEADS, NSA_KV_GROUPS, NSA_HEAD_DIM = 8, 2, 64
NSA_HPG = NSA_HEADS // NSA_KV_GROUPS
NSA_WIDTH = NSA_HEADS * NSA_HEAD_DIM
NSA_ROT_DIM = NSA_HEAD_DIM // 4
ROPE_THETA = 500000.0
CMP_BLOCK, CMP_STRIDE, CMP_HIDDEN = 32, 16, 128
SLC_BLOCK, N_SLC, WINDOW = 64, 16, 512
Q_BLOCK = 256
NSA_KEY_TILE = 1024
FORCE_SCORE = 1e6
MASK_VALUE = -1e30
RWKV_HEADS, RWKV_HEAD_DIM = 8, 64
RWKV_WIDTH = RWKV_HEADS * RWKV_HEAD_DIM
RWKV_LORA_W, RWKV_LORA_A, RWKV_LORA_G = 64, 64, 128
RWKV_COLS = 3 * RWKV_WIDTH + RWKV_LORA_W + RWKV_LORA_A + RWKV_LORA_G
RWKV_GN_EPS = 64e-5
RWKV_CHUNK = 64
RWKV_PACK = 4
RET_HEADS, RET_DK, RET_DV, RET_CHUNK = 4, 64, 128, 128
RET_THETA = 10000.0
RET_GN_EPS = 1e-5
N_EXPERTS, TOP_K, EXPERT_FF = 64, 8, 256
N_EXPERT_GROUPS, TOPK_GROUPS = 8, 4
EXPERTS_PER_GROUP = N_EXPERTS // N_EXPERT_GROUPS
ROUTED_SCALE = 2.5
MOE_EXPERTS_PER_STEP = 4


def _params(*sem):
    return pltpu.CompilerParams(dimension_semantics=sem, vmem_limit_bytes=VMEM_LIMIT_BYTES)


def _dot(a, b, **kw):
    return jnp.dot(a, b, preferred_element_type=F32, **kw)


def _dot_nt(a, b, **kw):
    return lax.dot_general(a, b, (((1,), (1,)), ((), ())), preferred_element_type=F32, **kw)


def _dot_tn(a, b, **kw):
    return lax.dot_general(a, b, (((0,), (0,)), ((), ())), preferred_element_type=F32, **kw)


def _dot_split(a, b, split, parts):
    rest = a if split == 'a' else b
    acc = None
    for _ in range(parts):
        piece = rest.astype(BF16)
        term = _dot(piece, b) if split == 'a' else _dot(a, piece)
        acc = term if acc is None else acc + term
        rest = rest - piece.astype(F32)
    return acc


def _run_interleaved(gens):
    results = [None] * len(gens)
    live = list(range(len(gens)))
    while live:
        for i in list(live):
            try:
                next(gens[i])
            except StopIteration as done:
                results[i] = done.value
                live.remove(i)
    return results


def _gelu(x):
    return 0.5 * x * (1.0 + jnp.tanh(math.sqrt(2.0 / math.pi) * (x + 0.044715 * (x * x * x))))


def _sigmoid(x):
    return 1.0 / (1.0 + jnp.exp(-x))


def _layer_norm_rows(z, g, b):
    mu = jnp.mean(z, axis=-1, keepdims=True)
    zc = z - mu
    var = jnp.mean(zc * zc, axis=-1, keepdims=True)
    return zc * lax.rsqrt(var + LN_EPS) * g + b


def _proj_kernel(x_ref, w_ref, o_ref, *chunked_ref):
    y = _dot(x_ref[...].astype(BF16), w_ref[...])
    o_ref[...] = y
    for c_ref in chunked_ref:
        rows, t, w = c_ref.shape
        c_ref[...] = y[:, :w].reshape(rows, t, w)


def project(x, w_bf16, tm, chunked=None):
    m, k = x.shape
    n = w_bf16.shape[1]
    out_specs = [pl.BlockSpec((tm, n), lambda i: (i, 0))]
    out_shape = [jax.ShapeDtypeStruct((m, n), F32)]
    if chunked is not None:
        t, w = chunked
        out_specs.append(pl.BlockSpec((tm // t, t, w), lambda i: (i, 0, 0)))
        out_shape.append(jax.ShapeDtypeStruct((m // t, t, w), F32))
    out = pl.pallas_call(
        _proj_kernel,
        grid=(m // tm,),
        in_specs=[pl.BlockSpec((tm, k), lambda i: (i, 0)), pl.BlockSpec((k, n), lambda i: (0, 0))],
        out_specs=out_specs,
        out_shape=out_shape,
        compiler_params=_params("parallel"),
        name="project",
    )(x, w_bf16)
    return out if chunked is not None else out[0]


def _out_proj_ln_kernel(ya_ref, yb_ref, wa_ref, wb_ref, x_ref, g_ref, b_ref, rt_ref, rbias_ref, o_ref, gate_ref):
    mix = _dot(ya_ref[...], wa_ref[...]) + _dot(yb_ref[...], wb_ref[...])
    y = _layer_norm_rows(ALPHA * x_ref[...] + mix, g_ref[...], b_ref[...])
    o_ref[...] = y
    gate_ref[...] = _route(y, rt_ref[...], rbias_ref[...])


def out_proj_ln_route(ya, yb, w_out, x, g, b, router, router_bias, tm=512):
    n, d = x.shape
    ka, kb = ya.shape[1], yb.shape[1]
    wa = w_out[:ka].astype(BF16)
    wb = w_out[ka:].astype(BF16)
    row = lambda i: (i, 0)
    fixed = lambda i: (0, 0)
    return pl.pallas_call(
        _out_proj_ln_kernel,
        grid=(n // tm,),
        in_specs=[pl.BlockSpec((tm, ka), row), pl.BlockSpec((tm, kb), row),
                  pl.BlockSpec((ka, d), fixed), pl.BlockSpec((kb, d), fixed),
                  pl.BlockSpec((tm, d), row), pl.BlockSpec((1, d), fixed), pl.BlockSpec((1, d), fixed),
                  pl.BlockSpec((N_EXPERTS, d), fixed), pl.BlockSpec((N_EXPERTS, 1), fixed)],
        out_specs=[pl.BlockSpec((tm, d), row), pl.BlockSpec((tm, LANES), row)],
        out_shape=[jax.ShapeDtypeStruct((n, d), F32), jax.ShapeDtypeStruct((n, LANES), F32)],
        compiler_params=_params("parallel"),
        name="out_proj_ln_route",
    )(ya, yb, wa, wb, x, g.reshape(1, d), b.reshape(1, d), router.T, router_bias.reshape(N_EXPERTS, 1))


def _route(x, rt, bias):
    tr = x.shape[0]
    scores = _sigmoid(_dot_nt(rt, x, precision=HIGHEST))
    biased = scores + bias
    grp = biased.reshape(N_EXPERT_GROUPS, EXPERTS_PER_GROUP, tr)
    pos = lax.broadcasted_iota(jnp.int32, grp.shape, 1)
    m1 = jnp.max(grp, axis=1, keepdims=True)
    first = jnp.min(jnp.where(grp == m1, pos, EXPERTS_PER_GROUP), axis=1, keepdims=True)
    m2 = jnp.max(jnp.where(pos == first, -jnp.inf, grp), axis=1, keepdims=True)
    gscore = (m1 + m2).reshape(N_EXPERT_GROUPS, tr)
    gidx = lax.broadcasted_iota(jnp.int32, gscore.shape, 0)
    grank = jnp.zeros(gscore.shape, F32)
    for j in range(N_EXPERT_GROUPS):
        row = gscore[j:j + 1, :]
        grank = grank + jnp.where(gidx > j, jnp.where(row >= gscore, 1.0, 0.0), jnp.where(row > gscore, 1.0, 0.0))
    gkeep = jnp.where(grank < TOPK_GROUPS, 1.0, 0.0)
    keep = jnp.broadcast_to(gkeep[:, None, :], grp.shape).reshape(N_EXPERTS, tr)
    masked = jnp.where(keep > 0.5, biased, -jnp.inf)
    eidx = lax.broadcasted_iota(jnp.int32, masked.shape, 0)
    rank = jnp.zeros(masked.shape, F32)
    for j in range(N_EXPERTS):
        row = masked[j:j + 1, :]
        rank = rank + jnp.where(eidx > j, jnp.where(row >= masked, 1.0, 0.0), jnp.where(row > masked, 1.0, 0.0))
    gate = jnp.where(rank < TOP_K, scores, 0.0)
    gate = gate / jnp.sum(gate, axis=0, keepdims=True) * ROUTED_SCALE
    return jnp.concatenate([gate, jnp.zeros((LANES - N_EXPERTS, tr), F32)], axis=0).T


def _quantize_fp8(a, axes):
    amax = jnp.max(jnp.abs(a), axis=axes, keepdims=True)
    scale = jnp.maximum(amax, FP8_TINY) * (1.0 / FP8_MAX)
    return (a * (1.0 / scale)).astype(FP8), scale


def _swiglu_hidden(xq, x_scale, w1q, w3q, w_scale, gate=None):
    col1 = x_scale * w_scale[0:1, 0:1]
    col3 = x_scale * w_scale[1:2, 0:1]
    if gate is not None:
        col3 = col3 * gate
    h1 = _dot(xq, w1q) * col1
    return h1 * _sigmoid(h1) * (_dot(xq, w3q) * col3)


def _experts_ln_kernel(x_ref, gate_ref, w1_ref, w3_ref, ws_ref, w2_ref, sw1_ref, sw3_ref, sws_ref, sw2_ref,
                       g_ref, b_ref, o_ref, obf_ref, acc_ref, xq_ref, xs_ref):
    step = pl.program_id(1)
    per_step = w1_ref.shape[0]

    @pl.when(step == 0)
    def _():
        xq, xs = _quantize_fp8(x_ref[...], (1,))
        xq_ref[...] = xq
        xs_ref[...] = xs
        h = _swiglu_hidden(xq, xs, sw1_ref[...], sw3_ref[...], sws_ref[...])
        acc_ref[...] = _dot(h.astype(BF16), sw2_ref[...].astype(BF16))

    lane = lax.broadcasted_iota(jnp.int32, gate_ref.shape, 1)
    gates = gate_ref[...]
    xq, xs = xq_ref[...], xs_ref[...]
    hidden = []
    for j in range(per_step):
        gcol = jnp.sum(jnp.where(lane == step * per_step + j, gates, 0.0), axis=1, keepdims=True)
        hidden.append(_swiglu_hidden(xq, xs, w1_ref[j], w3_ref[j], ws_ref[j], gcol).astype(BF16))
    w2 = w2_ref[0].astype(BF16)
    acc_ref[...] += _dot(jnp.concatenate(hidden, axis=1), w2.reshape(per_step * w2.shape[1], w2.shape[2]))

    @pl.when(step == pl.num_programs(1) - 1)
    def _():
        y = _layer_norm_rows(ALPHA * x_ref[...] + acc_ref[...], g_ref[...], b_ref[...])
        o_ref[...] = y
        obf_ref[...] = y.astype(BF16)


def _quantize_weights_kernel(w1_ref, w3_ref, q1_ref, q3_ref, s_ref):
    for e in range(q1_ref.shape[0]):
        q1, s1 = _quantize_fp8(w1_ref[0, e], (0, 1))
        q3, s3 = _quantize_fp8(w3_ref[0, e], (0, 1))
        q1_ref[e] = q1
        q3_ref[e] = q3
        ff = s_ref.shape[2]
        s_ref[e] = jnp.concatenate([jnp.broadcast_to(s1, (1, ff)), jnp.broadcast_to(s3, (1, ff))], axis=0)


def _quantize_expert_weights(w1_layers, w3_layers, layer):
    _, ne, d, ff = w1_layers.shape
    per_step = math.gcd(ne, MOE_EXPERTS_PER_STEP)
    blk = lambda *shape: pl.BlockSpec((per_step,) + shape, lambda i: (i, 0, 0))
    src = pl.BlockSpec((1, per_step, d, ff), lambda i: (layer, i, 0, 0))
    return pl.pallas_call(
        _quantize_weights_kernel,
        grid=(ne // per_step,),
        in_specs=[src, src],
        out_specs=[blk(d, ff), blk(d, ff), blk(2, ff)],
        out_shape=[jax.ShapeDtypeStruct((ne, d, ff), FP8), jax.ShapeDtypeStruct((ne, d, ff), FP8),
                   jax.ShapeDtypeStruct((ne, 2, ff), F32)],
        compiler_params=_params("parallel"),
        name="quantize_expert_weights",
    )(w1_layers, w3_layers)


def moe_experts_ln(x, gates, w1_layers, w3_layers, w2_layers, layer, sw1_layers, sw3_layers, sw2, g, b, tm=1024):
    n, d = x.shape
    ne = w1_layers.shape[1]
    w1q, w3q, ws = _quantize_expert_weights(w1_layers, w3_layers, layer)
    sw1q, sw3q, sws = (a[0] for a in _quantize_expert_weights(sw1_layers[:, None], sw3_layers[:, None], layer))
    tok = lambda i, e: (i, 0)
    fixed = lambda i, e: (0, 0)
    per_expert = lambda *blk: pl.BlockSpec((MOE_EXPERTS_PER_STEP,) + blk, lambda i, e: (e, 0, 0))
    return pl.pallas_call(
        _experts_ln_kernel,
        grid=(n // tm, ne // MOE_EXPERTS_PER_STEP),
        in_specs=[pl.BlockSpec((tm, d), tok), pl.BlockSpec((tm, LANES), tok),
                  per_expert(d, EXPERT_FF), per_expert(d, EXPERT_FF), per_expert(2, EXPERT_FF),
                  pl.BlockSpec((1, MOE_EXPERTS_PER_STEP, EXPERT_FF, d), lambda i, e: (layer, e, 0, 0)),
                  pl.BlockSpec((d, EXPERT_FF), fixed), pl.BlockSpec((d, EXPERT_FF), fixed),
                  pl.BlockSpec((2, EXPERT_FF), fixed), pl.BlockSpec((EXPERT_FF, d), fixed),
                  pl.BlockSpec((1, d), fixed), pl.BlockSpec((1, d), fixed)],
        out_specs=[pl.BlockSpec((tm, d), tok), pl.BlockSpec((tm, d), tok)],
        out_shape=[jax.ShapeDtypeStruct((n, d), F32), jax.ShapeDtypeStruct((n, d), BF16)],
        scratch_shapes=[pltpu.VMEM((tm, d), F32), pltpu.VMEM((tm, d), FP8), pltpu.VMEM((tm, 1), F32)],
        compiler_params=_params("parallel", "arbitrary"),
        name="moe_experts_ln",
    )(x, gates, w1q, w3q, ws, w2_layers, sw1q, sw3q, sws, sw2, g.reshape(1, d), b.reshape(1, d))


def _s5_tables(lam_re, lam_im, log_dt, b_re, b_im, c_re, c_im, n_chunk):
    t, h, p = S5_CHUNK, S5_GROUP_CH, S5_STATE
    dt = jnp.exp(log_dt.astype(F32))[:, None]
    den = lam_re ** 2 + lam_im ** 2

    def lam_pow(k):
        k = jnp.asarray(k, F32)[..., None, None]
        mag = jnp.exp(lam_re * dt * k)
        return mag * jnp.cos(lam_im * dt * k), mag * jnp.sin(lam_im * dt * k)

    lb_re, lb_im = lam_pow(1.0)
    f_re = ((lb_re - 1.0) * lam_re + lb_im * lam_im) / den
    f_im = (lb_im * lam_re - (lb_re - 1.0) * lam_im) / den
    bb_re = f_re[..., None] * b_re - f_im[..., None] * b_im
    bb_im = f_re[..., None] * b_im + f_im[..., None] * b_re
    pr, pi = lam_pow(jnp.arange(t))
    cl_re = c_re[None] * pr[:, :, None, :] - c_im[None] * pi[:, :, None, :]
    cl_im = c_re[None] * pi[:, :, None, :] + c_im[None] * pr[:, :, None, :]
    klag = jnp.einsum('tgop,gpi->tgoi', cl_re, bb_re) - jnp.einsum('tgop,gpi->tgoi', cl_im, bb_im)
    nb = S5_GROUPS // S5_PACK
    split = lambda a, axis: a.reshape(a.shape[:axis] + (nb, S5_PACK) + a.shape[axis + 1:])
    eye = jnp.eye(S5_PACK, dtype=F32)
    lag_t = jnp.transpose(split(klag, 1), (1, 0, 2, 4, 3))
    lag_t = (lag_t[:, :, :, :, None, :] * eye[None, None, :, None, :, None]).reshape(nb, t, LANES, LANES)
    qr, qi = lam_pow(t - 1 - jnp.arange(t))
    st_re = qr[..., None] * bb_re[None] - qi[..., None] * bb_im[None]
    st_im = qr[..., None] * bb_im[None] + qi[..., None] * bb_re[None]
    st = jnp.stack([st_re, st_im], axis=0)
    st_t = jnp.transpose(split(st, 2), (2, 1, 3, 5, 0, 4)).reshape(nb, t, LANES, 2 * p)
    er, ei = lam_pow(jnp.arange(t) + 1)
    x_re = c_re[None] * er[:, :, None, :] - c_im[None] * ei[:, :, None, :]
    x_im = c_re[None] * ei[:, :, None, :] + c_im[None] * er[:, :, None, :]
    cr = jnp.stack([x_re, -x_im], axis=0)
    cr_t = jnp.transpose(split(cr, 2), (2, 1, 0, 5, 3, 4)).reshape(nb, t, 2 * p, LANES)
    levels = max(1, int(math.log2(n_chunk)))
    sr, si = lam_pow(t * (2.0 ** jnp.arange(levels)))
    sr = sr.reshape(levels, nb, S5_PACK * p)
    si = si.reshape(levels, nb, S5_PACK * p)
    a1 = jnp.concatenate([sr, sr], axis=-1)
    a2 = jnp.concatenate([-si, si], axis=-1)
    scan = jnp.transpose(jnp.stack([a1, a2], axis=1), (2, 0, 1, 3))
    return lag_t, st_t, cr_t, scan.astype(F32)


def _s5_build_tables(lag_ref, st_ref, cr_ref, wtoe_ref, wstate_ref, wcross_ref):
    t = lag_ref.shape[1]
    p = S5_STATE
    kp = S5_PACK * p
    wtoe_ref[...] = jnp.zeros_like(wtoe_ref)
    for d in range(t):
        tile = lag_ref[0, d].astype(BF16)
        for j in range(t - d):
            wtoe_ref[j * LANES:(j + 1) * LANES, (j + d) * LANES:(j + d + 1) * LANES] = tile
    lane = lax.broadcasted_iota(jnp.int32, (LANES, LANES), 1)
    row_g = lax.broadcasted_iota(jnp.int32, (LANES, kp), 0) // S5_GROUP_CH
    same_s = row_g == lax.broadcasted_iota(jnp.int32, (LANES, kp), 1) // p
    for j in range(t):
        a = st_ref[0, j]
        swapped = pltpu.roll(a, p, 1)
        for c, both in enumerate((jnp.where(lane < p, a, swapped), jnp.where(lane < p, swapped, a))):
            wide = jnp.concatenate([both] * (kp // LANES), axis=1)
            wstate_ref[j * LANES:(j + 1) * LANES, c * kp:(c + 1) * kp] = jnp.where(same_s, wide, 0.0).astype(BF16)
    same_c = (lax.broadcasted_iota(jnp.int32, (kp, LANES), 0) // p
              == lax.broadcasted_iota(jnp.int32, (kp, LANES), 1) // S5_GROUP_CH)
    for i in range(t):
        a = cr_ref[0, i]
        for c in range(2):
            tall = jnp.concatenate([a[c * p:(c + 1) * p]] * S5_PACK, axis=0)
            wcross_ref[c * kp:(c + 1) * kp, i * LANES:(i + 1) * LANES] = jnp.where(same_c, tall, 0.0).astype(BF16)


def _s5_kernel(u_ref, lag_ref, st_ref, cr_ref, scan_ref, o_ref, wtoe_ref, wstate_ref, wcross_ref):
    @pl.when(pl.program_id(1) == 0)
    def _():
        _s5_build_tables(lag_ref, st_ref, cr_ref, wtoe_ref, wstate_ref, wcross_ref)

    n_chunk, t, _ = u_ref.shape
    x = jnp.concatenate([u_ref[:, j, :] for j in range(t)], axis=1).astype(BF16)
    local = _dot(x, wtoe_ref[...])
    state = _dot(x, wstate_ref[...])
    row = lax.broadcasted_iota(jnp.int32, state.shape, 0)
    s = jnp.where(row >= 1, pltpu.roll(state, 1, 0), 0.0)
    half = state.shape[1] // 2
    level = 0
    d = 1
    while d < n_chunk:
        mult = scan_ref[0, level]
        prev = jnp.where(row >= d, pltpu.roll(s, d, 0), 0.0)
        s = s + mult[0:1, :] * prev + mult[1:2, :] * pltpu.roll(prev, half, 1)
        d *= 2
        level += 1
    y = local + _dot(s.astype(BF16), wcross_ref[...])
    for i in range(t):
        o_ref[:, i, :] = y[:, i * LANES:(i + 1) * LANES]


def s5_scan(u3, bsz, lag_t, st_t, cr_t, scan):
    rows, t, w = u3.shape
    n_chunk = rows // bsz
    kp2 = 2 * S5_PACK * S5_STATE
    table = lambda a: pl.BlockSpec((1,) + a.shape[1:], lambda j, b: (j, 0, 0, 0))
    return pl.pallas_call(
        _s5_kernel,
        grid=(w // LANES, bsz),
        in_specs=[pl.BlockSpec((n_chunk, t, LANES), lambda j, b: (b, 0, j)),
                  table(lag_t), table(st_t), table(cr_t), table(scan)],
        out_specs=pl.BlockSpec((n_chunk, t, LANES), lambda j, b: (b, 0, j)),
        out_shape=jax.ShapeDtypeStruct(u3.shape, F32),
        scratch_shapes=[pltpu.VMEM((t * LANES, t * LANES), BF16), pltpu.VMEM((t * LANES, kp2), BF16),
                        pltpu.VMEM((kp2, t * LANES), BF16)],
        compiler_params=_params("arbitrary", "arbitrary"),
        name="s5_scan",
    )(u3, lag_t, st_t, cr_t, scan)


def _s5_post_kernel(y_ref, u_ref, d_ref, w_ref, b_ref, o_ref):
    u = u_ref[...]
    y = _gelu(y_ref[...].reshape(u.shape) + d_ref[...] * u)
    o_ref[...] = (y * _sigmoid(_dot(y.astype(BF16), w_ref[...]) + b_ref[...])).astype(o_ref.dtype)


def s5_post(y3, h, d_skip, w_glu, b_glu, tm=1024):
    rows, t, w = y3.shape
    n = rows * t
    tm = min(tm, n)
    row = lambda i: (i, 0)
    fixed = lambda i: (0, 0)
    return pl.pallas_call(
        _s5_post_kernel,
        grid=(n // tm,),
        in_specs=[pl.BlockSpec((tm // t, t, w), lambda i: (i, 0, 0)), pl.BlockSpec((tm, w), row),
                  pl.BlockSpec((1, w), fixed), pl.BlockSpec((w, w), fixed), pl.BlockSpec((1, w), fixed)],
        out_specs=pl.BlockSpec((tm, w), row),
        out_shape=jax.ShapeDtypeStruct((n, w), BF16),
        compiler_params=_params("parallel"),
        name="s5_post",
    )(y3, h, d_skip.reshape(1, w), w_glu.astype(BF16), b_glu.reshape(1, w))


def s5_mixer(h, u3, bsz, seq, lam_re, lam_im, log_dt, b_re, b_im, c_re, c_im, d_skip, w_glu, b_glu):
    tables = _s5_tables(lam_re, lam_im, log_dt, b_re, b_im, c_re, c_im, seq // S5_CHUNK)
    return s5_post(s5_scan(u3, bsz, *tables), h, d_skip, w_glu, b_glu)


def _rope_tables(pos, rot_dim, theta, head_dim, n_heads):
    half = rot_dim // 2
    f32 = np.float32
    inv_freq = f32(theta) ** (-np.arange(half, dtype=f32) / f32(half))
    ang = (pos.astype(f32)[:, None] * inv_freq[None, :]).astype(np.float64)
    cos, sin = np.cos(ang), np.sin(ang)
    rest = head_dim - rot_dim
    n = pos.shape[0]
    c = np.concatenate([cos, cos, np.ones((n, rest))], axis=1)
    s_up = np.concatenate([-sin, np.zeros((n, half + rest))], axis=1)
    s_dn = np.concatenate([np.zeros((n, half)), sin, np.zeros((n, rest))], axis=1)
    tile = lambda a: jnp.asarray(np.tile(a, (1, n_heads)), F32)
    return tile(c), tile(s_up), tile(s_dn)


def _rope_apply(x, c, s_up, s_dn, half):
    return x * c + pltpu.roll(x, LANES - half, 1) * s_up + pltpu.roll(x, half, 1) * s_dn


def _retention_tables():
    c = RET_CHUNK
    log_gamma = np.log(1.0 - 2.0 ** (-5.0 - np.arange(RET_HEADS, dtype=np.float64)))
    i = np.arange(c, dtype=np.float64)
    diff = i[:, None] - i[None, :]
    decay = np.where(diff >= 0, np.exp(diff[None] * log_gamma[:, None, None]), 0.0)
    qdec = np.repeat(np.exp((i + 1.0)[:, None] * log_gamma[None, :]), RET_DK, axis=1)
    kdec = np.repeat(np.exp((c - 1.0 - i)[:, None] * log_gamma[None, :]), RET_DK, axis=1)
    chunk_decay = [float(v) for v in np.exp(c * log_gamma)]
    return jnp.asarray(decay, F32), jnp.asarray(qdec, F32), jnp.asarray(kdec, F32), chunk_decay


def _retention_kernel(chunk_decay, q_ref, k_ref, v0_ref, v1_ref, g0_ref, g1_ref, c_ref, su_ref, sd_ref,
                      dec_ref, qdec_ref, kdec_ref, lng_ref, lnb_ref, o_ref, state_ref):
    @pl.when(pl.program_id(1) == 0)
    def _():
        state_ref[...] = jnp.zeros_like(state_ref)

    half = RET_DK // 2
    tabs = (c_ref[...], su_ref[...], sd_ref[...])
    q = jnp.concatenate([_rope_apply(q_ref[:, s:s + LANES], *tabs, half) for s in (0, LANES)], axis=1)
    k = jnp.concatenate([_rope_apply(k_ref[:, s:s + LANES], *tabs, half) for s in (0, LANES)], axis=1)
    k = k * (RET_DK ** -0.5)
    q_dec = q * qdec_ref[...]
    k_dec = k * kdec_ref[...]
    v = jnp.concatenate([v0_ref[...], v1_ref[...]], axis=1)
    gate = jnp.concatenate([g0_ref[...], g1_ref[...]], axis=1)
    states = [state_ref[h] for h in range(RET_HEADS)]

    def head(h):
        ks = slice(h * RET_DK, (h + 1) * RET_DK)
        vh = v[:, h * RET_DV:(h + 1) * RET_DV].astype(BF16)
        scores = _dot_nt(q[:, ks].astype(BF16), k[:, ks].astype(BF16)) * dec_ref[h]
        yield
        y = _dot(scores.astype(BF16), vh) + _dot(q_dec[:, ks].astype(BF16), states[h].astype(BF16))
        yield
        new_state = states[h] * chunk_decay[h] + _dot_tn(k_dec[:, ks].astype(BF16), vh)
        yield
        mu = jnp.mean(y, axis=-1, keepdims=True)
        yc = y - mu
        var = jnp.mean(yc * yc, axis=-1, keepdims=True)
        return yc * lax.rsqrt(var + RET_GN_EPS), new_state

    results = _run_interleaved([head(h) for h in range(RET_HEADS)])
    outs = [r[0] for r in results]
    for h, (_, new_state) in enumerate(results):
        state_ref[h] = new_state
    yn = jnp.concatenate(outs, axis=1) * lng_ref[...] + lnb_ref[...]
    o_ref[...] = (gate * _sigmoid(gate) * yn).astype(o_ref.dtype)


def retention_mixer(h, bsz, seq, col0, ln_g, ln_b):
    c = RET_CHUNK
    n_chunk = seq // c
    qk_w = RET_HEADS * RET_DK
    v_w = RET_HEADS * RET_DV
    assert col0 % qk_w == 0 and qk_w == 2 * LANES and v_w == 2 * qk_w
    cb = col0 // qk_w
    rc, rsu, rsd = _rope_tables(np.arange(seq), RET_DK, RET_THETA, RET_DK, 2)
    dec, qdec, kdec, chunk_decay = _retention_tables()
    row = lambda j: (lambda b, n: (b * n_chunk + n, j))
    pos = lambda b, n: (n, 0)
    fixed2 = lambda b, n: (0, 0)
    kern = functools.partial(_retention_kernel, chunk_decay)
    return pl.pallas_call(
        kern,
        grid=(bsz, n_chunk),
        in_specs=[pl.BlockSpec((c, qk_w), row(cb)), pl.BlockSpec((c, qk_w), row(cb + 1)),
                  pl.BlockSpec((c, qk_w), row(cb + 2)), pl.BlockSpec((c, qk_w), row(cb + 3)),
                  pl.BlockSpec((c, qk_w), row(cb + 4)), pl.BlockSpec((c, qk_w), row(cb + 5)),
                  pl.BlockSpec((c, LANES), pos), pl.BlockSpec((c, LANES), pos), pl.BlockSpec((c, LANES), pos),
                  pl.BlockSpec((RET_HEADS, c, c), lambda b, n: (0, 0, 0)),
                  pl.BlockSpec((c, qk_w), fixed2), pl.BlockSpec((c, qk_w), fixed2),
                  pl.BlockSpec((1, v_w), fixed2), pl.BlockSpec((1, v_w), fixed2)],
        out_specs=pl.BlockSpec((c, v_w), lambda b, n: (b * n_chunk + n, 0)),
        out_shape=jax.ShapeDtypeStruct((bsz * seq, v_w), BF16),
        scratch_shapes=[pltpu.VMEM((RET_HEADS, RET_DK, RET_DV), F32)],
        compiler_params=_params("parallel", "arbitrary"),
        name="retention",
    )(h, h, h, h, h, h, rc, rsu, rsd, dec, qdec, kdec, ln_g.reshape(1, v_w), ln_b.reshape(1, v_w))


NSA_KV_W = NSA_KV_GROUPS * NSA_HEAD_DIM
NSA_VT_ROWS = NSA_HEAD_DIM + 16
NSA_GATE_COLS = 3 * NSA_HEADS


def _nsa_prep_kernel(q_ref, kvc_ref, kvs_ref, kvw_ref, c_ref, su_ref, sd_ref,
                     qo_ref, kc_ref, vc_ref, ks_ref, vs_ref, kw_ref, vw_ref):
    half = NSA_ROT_DIM // 2
    tabs = (c_ref[...], su_ref[...], sd_ref[...])
    scale = NSA_HEAD_DIM ** -0.5 * math.log2(math.e)
    q = jnp.concatenate(
        [_rope_apply(q_ref[:, s:s + LANES], *tabs, half) * scale for s in range(0, NSA_WIDTH, LANES)], axis=1)
    qo_ref[0] = q.T.astype(qo_ref.dtype)

    def split(x, o_ref):
        for g in range(NSA_KV_GROUPS):
            o_ref[0, g] = x[:, g * NSA_HEAD_DIM:(g + 1) * NSA_HEAD_DIM].astype(o_ref.dtype)

    def split_t(x, o_ref):
        xt = x.T
        for g in range(NSA_KV_GROUPS):
            o_ref[0, g, :NSA_HEAD_DIM] = xt[g * NSA_HEAD_DIM:(g + 1) * NSA_HEAD_DIM, :].astype(o_ref.dtype)
            o_ref[0, g, NSA_HEAD_DIM:] = jnp.ones((NSA_VT_ROWS - NSA_HEAD_DIM, xt.shape[1]), o_ref.dtype)

    split(kvc_ref[:, :NSA_KV_W], kc_ref)
    split(kvc_ref[:, NSA_KV_W:], vc_ref)
    n_hot = ks_ref.shape[3] - NSA_HEAD_DIM
    tl = q_ref.shape[0]
    blk = (pl.program_id(1) * tl + lax.broadcasted_iota(jnp.int32, (tl, n_hot), 0)) // SLC_BLOCK
    one_hot = jnp.where(blk % n_hot == lax.broadcasted_iota(jnp.int32, (tl, n_hot), 1), 1.0, 0.0)
    ks = _rope_apply(kvs_ref[:, :NSA_KV_W], *tabs, half)
    for g in range(NSA_KV_GROUPS):
        ks_ref[0, g] = jnp.concatenate([ks[:, g * NSA_HEAD_DIM:(g + 1) * NSA_HEAD_DIM], one_hot],
                                       axis=1).astype(ks_ref.dtype)
    split_t(kvs_ref[:, NSA_KV_W:], vs_ref)
    split(_rope_apply(kvw_ref[:, :NSA_KV_W], *tabs, half), kw_ref)
    split_t(kvw_ref[:, NSA_KV_W:], vw_ref)


def nsa_prep(h, bsz, seq, tl=512):
    tl = min(tl, seq)
    nl = seq // tl
    rc, rsu, rsd = _rope_tables(np.arange(seq), NSA_ROT_DIM, ROPE_THETA, NSA_HEAD_DIM, LANES // NSA_HEAD_DIM)
    row = lambda j: (lambda b, l: (b * nl + l, j))
    pos = lambda b, l: (l, 0)
    kv_out = pl.BlockSpec((1, NSA_KV_GROUPS, tl, NSA_HEAD_DIM), lambda b, l: (b, 0, l, 0))
    kv_shape = lambda dt: jax.ShapeDtypeStruct((bsz, NSA_KV_GROUPS, seq, NSA_HEAD_DIM), dt)
    vt_out = pl.BlockSpec((1, NSA_KV_GROUPS, NSA_VT_ROWS, tl), lambda b, l: (b, 0, 0, l))
    vt_shape = jax.ShapeDtypeStruct((bsz, NSA_KV_GROUPS, NSA_VT_ROWS, seq), BF16)
    ks_w = NSA_HEAD_DIM + min(NSA_KEY_TILE, seq) // SLC_BLOCK
    ks_out = pl.BlockSpec((1, NSA_KV_GROUPS, tl, ks_w), lambda b, l: (b, 0, l, 0))
    ks_shape = jax.ShapeDtypeStruct((bsz, NSA_KV_GROUPS, seq, ks_w), BF16)
    two = 2 * NSA_KV_W
    return pl.pallas_call(
        _nsa_prep_kernel,
        grid=(bsz, nl),
        in_specs=[pl.BlockSpec((tl, NSA_WIDTH), row(1)),
                  pl.BlockSpec((tl, two), row(4)), pl.BlockSpec((tl, two), row(5)), pl.BlockSpec((tl, two), row(6)),
                  pl.BlockSpec((tl, LANES), pos), pl.BlockSpec((tl, LANES), pos), pl.BlockSpec((tl, LANES), pos)],
        out_specs=[pl.BlockSpec((1, NSA_WIDTH, tl), lambda b, l: (b, 0, l)),
                   kv_out, kv_out, ks_out, vt_out, kv_out, vt_out],
        out_shape=[jax.ShapeDtypeStruct((bsz, NSA_WIDTH, seq), BF16),
                   kv_shape(F32), kv_shape(F32), ks_shape, vt_shape, kv_shape(BF16), vt_shape],
        compiler_params=_params("parallel", "parallel"),
        name="nsa_prep",
    )(h, h, h, h, rc, rsu, rsd)


def _nsa_compress_kernel(hk_ref, hv_ref, pek_ref, pev_ref, kw1_ref, kb1_ref, kw2_ref, vw1_ref, vb1_ref, vw2_ref,
                         c_ref, su_ref, sd_ref, ko_ref, vo_ref):
    def mlp(h_ref, pe_ref, w1_ref, b1_ref, w2_ref):
        hb = h_ref[0, 0]
        rows = hb.shape[0]
        first = _dot((hb + pe_ref[0:1, :]).astype(BF16), w1_ref[0])
        second = _dot((hb + pe_ref[1:2, :]).astype(BF16), w1_ref[1])
        hid = _gelu(first + pltpu.roll(second, rows - 1, 0) + b1_ref[...])
        return _dot(hid.astype(BF16), w2_ref[...])

    kc = _rope_apply(mlp(hk_ref, pek_ref, kw1_ref, kb1_ref, kw2_ref), c_ref[...], su_ref[...], sd_ref[...],
                     NSA_ROT_DIM // 2)
    vc = mlp(hv_ref, pev_ref, vw1_ref, vb1_ref, vw2_ref)
    ko_ref[0, 0] = kc[:, :NSA_HEAD_DIM].astype(ko_ref.dtype)
    vo_ref[0, 0, :NSA_HEAD_DIM] = vc.T[:NSA_HEAD_DIM, :].astype(vo_ref.dtype)
    vo_ref[0, 0, NSA_HEAD_DIM:] = jnp.ones((NSA_VT_ROWS - NSA_HEAD_DIM, vc.shape[0]), vo_ref.dtype)


def nsa_compress(kc, vc, pe_k, pe_v, ck_w1, ck_b1, ck_w2, cv_w1, cv_b1, cv_w2):
    bsz, grp, seq, d = kc.shape
    n_rows = seq // CMP_STRIDE
    flat = CMP_STRIDE * d
    cmp_end = np.arange(n_rows) * CMP_STRIDE + CMP_BLOCK - 1
    rc, rsu, rsd = _rope_tables(cmp_end, NSA_ROT_DIM, ROPE_THETA, NSA_HEAD_DIM, LANES // NSA_HEAD_DIM)
    pad_w2 = lambda w: jnp.pad(w, ((0, 0), (0, LANES - d))).astype(BF16)
    blk = pl.BlockSpec((1, 1, n_rows, flat), lambda b, g: (b, g, 0, 0))
    f2 = lambda b, g: (0, 0)
    f3 = lambda b, g: (0, 0, 0)
    w_specs = [pl.BlockSpec((2, flat, CMP_HIDDEN), f3), pl.BlockSpec((1, CMP_HIDDEN), f2),
               pl.BlockSpec((CMP_HIDDEN, LANES), f2)]
    return pl.pallas_call(
        _nsa_compress_kernel,
        grid=(bsz, grp),
        in_specs=[blk, blk, pl.BlockSpec((2, flat), f2), pl.BlockSpec((2, flat), f2)] + w_specs + w_specs
                 + [pl.BlockSpec((n_rows, LANES), f2)] * 3,
        out_specs=[pl.BlockSpec((1, 1, n_rows, d), lambda b, g: (b, g, 0, 0)),
                   pl.BlockSpec((1, 1, NSA_VT_ROWS, n_rows), lambda b, g: (b, g, 0, 0))],
        out_shape=[jax.ShapeDtypeStruct((bsz, grp, n_rows, d), BF16),
                   jax.ShapeDtypeStruct((bsz, grp, NSA_VT_ROWS, n_rows), BF16)],
        compiler_params=_params("parallel", "parallel"),
        name="nsa_compress",
    )(kc.reshape(bsz, grp, n_rows, flat), vc.reshape(bsz, grp, n_rows, flat),
      pe_k.reshape(2, flat), pe_v.reshape(2, flat),
      ck_w1.reshape(2, flat, CMP_HIDDEN).astype(BF16), ck_b1.reshape(1, CMP_HIDDEN), pad_w2(ck_w2),
      cv_w1.reshape(2, flat, CMP_HIDDEN).astype(BF16), cv_b1.reshape(1, CMP_HIDDEN), pad_w2(cv_w2),
      rc, rsu, rsd)


def _per_head(x):
    return jnp.concatenate([x] * NSA_HPG, axis=1)


def _nsa_attn_kernel(seq, tk, qt_ref, gate_ref, kc_ref, vct_ref, ks_ref, vst_ref, kw_ref, vwt_ref, mmapt_ref,
                     o_ref, sel_ref):
    n_blk = seq // SLC_BLOCK
    n_sel = min(N_SLC, n_blk)
    hd = NSA_HEAD_DIM
    w = NSA_HPG * hd
    groups = range(NSA_KV_GROUPS)
    q0 = pl.program_id(1) * Q_BLOCK
    t_l = q0 + lax.broadcasted_iota(jnp.int32, (1, Q_BLOCK), 1)

    def select(g):
        qg = qt_ref[0, g * w:(g + 1) * w, :]
        qst = jnp.concatenate([qg[h * hd:(h + 1) * hd, :] for h in range(NSA_HPG)], axis=1)
        kc = kc_ref[0, g]
        n_cmp = kc.shape[0]
        cmp_end = lax.broadcasted_iota(jnp.int32, (n_cmp, 1), 0) * CMP_STRIDE + (CMP_BLOCK - 1)
        s = _dot(kc, qst) + _per_head(jnp.where(cmp_end <= t_l, 0.0, MASK_VALUE))
        yield
        p = jnp.exp2(s - jnp.max(s, axis=0, keepdims=True))
        any_key = _per_head(jnp.where(t_l >= CMP_BLOCK - 1, 1.0, 0.0))
        pv = _dot(vct_ref[0, g], p.astype(BF16))
        inv_l = any_key / pv[hd:hd + 1]
        o_cmp = pv[:hd] * inv_l
        yield
        p = p * inv_l
        imp = p[:, 0:Q_BLOCK]
        for h in range(1, NSA_HPG):
            imp = imp + p[:, h * Q_BLOCK:(h + 1) * Q_BLOCK]
        imp_slc = _dot_split(mmapt_ref[...], imp, 'b', 3)
        yield
        blk = lax.broadcasted_iota(jnp.int32, (n_blk, 1), 0)
        cur = t_l // SLC_BLOCK
        score = jnp.where(blk == 0, FORCE_SCORE,
                          jnp.where(blk == cur, FORCE_SCORE, jnp.where(blk == cur - 1, FORCE_SCORE, imp_slc)))
        score = jnp.where(blk * SLC_BLOCK <= t_l, score, -FORCE_SCORE)
        sel = jnp.zeros((n_blk, Q_BLOCK), F32)
        for _ in range(n_sel):
            best = jnp.max(score, axis=0, keepdims=True)
            idx = jnp.min(jnp.where(score == best, blk, n_blk), axis=0, keepdims=True)
            pick = blk == idx
            sel = jnp.where(pick, 1.0, sel)
            score = jnp.where(pick, -jnp.inf, score)
            yield
        sel_ref[g] = sel
        return qst, o_cmp

    selected = _run_interleaved([select(g) for g in groups])
    qst = [r[0] for r in selected]
    o_cmp = [r[1] for r in selected]

    blocks_per_tile = tk // SLC_BLOCK
    assert ks_ref.shape[3] == hd + blocks_per_tile and Q_BLOCK <= tk and tk % Q_BLOCK == 0

    def slc_tile(kt, carry, causal_bias=None):
        k0 = pl.multiple_of(kt * tk, tk)
        out = []
        for g in groups:
            m, acc = carry[g]
            sel_rows = sel_ref[g, pl.ds(pl.multiple_of(kt * blocks_per_tile, blocks_per_tile), blocks_per_tile), :]
            q_aug = jnp.concatenate([qst[g], _per_head((sel_rows - 1.0) * -MASK_VALUE).astype(BF16)], axis=0)
            s = _dot(ks_ref[0, g, pl.ds(k0, tk), :], q_aug)
            if causal_bias is not None:
                s = s + causal_bias
            m_new = jnp.maximum(m, jnp.max(s, axis=0, keepdims=True))
            alpha = jnp.exp2(m - m_new)
            p = jnp.exp2(s - m_new)
            acc = alpha * acc + _dot(vst_ref[0, g, :, pl.ds(k0, tk)], p.astype(BF16))
            out.append((m_new, acc))
        return tuple(out)

    n_full = q0 // tk
    cols = NSA_HPG * Q_BLOCK
    init = tuple((jnp.full((1, cols), MASK_VALUE, F32), jnp.zeros((NSA_VT_ROWS, cols), F32)) for _ in groups)
    slc = lax.fori_loop(0, n_full, slc_tile, init)
    kpos = n_full * tk + lax.broadcasted_iota(jnp.int32, (tk, 1), 0)
    slc = slc_tile(n_full, slc, _per_head(jnp.where(kpos <= t_l, 0.0, MASK_VALUE)))

    band = WINDOW + Q_BLOCK
    w0 = pl.multiple_of(jnp.maximum(q0 - WINDOW, 0), Q_BLOCK)
    kpos = w0 + lax.broadcasted_iota(jnp.int32, (band, 1), 0)
    win_bias = _per_head(jnp.where(kpos <= t_l, jnp.where(kpos > t_l - WINDOW, 0.0, MASK_VALUE), MASK_VALUE))
    sig_t = _sigmoid(gate_ref[...]).T

    def finish(g):
        s = _dot(kw_ref[0, g, pl.ds(w0, band), :], qst[g]) + win_bias
        yield
        p = jnp.exp2(s - jnp.max(s, axis=0, keepdims=True))
        pv = _dot(vwt_ref[0, g, :, pl.ds(w0, band)], p.astype(BF16))
        o_win = pv[:hd] / pv[hd:hd + 1]
        yield
        acc = slc[g][1]
        acc_slc, l_slc = acc[:hd], acc[hd:hd + 1]

        def gate(branch):
            first = (g * NSA_HPG) * 3 + branch
            return jnp.concatenate([sig_t[first + 3 * h:first + 3 * h + 1, :] for h in range(NSA_HPG)], axis=1)

        out_t = gate(0) * o_cmp[g] + gate(1) * (acc_slc / l_slc) + gate(2) * o_win
        pairs = []
        for h in range(0, NSA_HPG, 2):
            two = jnp.concatenate([out_t[:, h * Q_BLOCK:(h + 1) * Q_BLOCK],
                                   out_t[:, (h + 1) * Q_BLOCK:(h + 2) * Q_BLOCK]], axis=0)
            pairs.append(two.T)
        o_ref[:, g * w:(g + 1) * w] = jnp.concatenate(pairs, axis=1).astype(o_ref.dtype)

    _run_interleaved([finish(g) for g in groups])


def _nsa_pool_matrix(seq):
    n_blk = seq // SLC_BLOCK
    n_rows = seq // CMP_STRIDE
    per_stride = SLC_BLOCK // CMP_STRIDE
    span = CMP_BLOCK // CMP_STRIDE
    pool = np.zeros((n_blk, n_rows), np.float32)
    for j in range(n_blk):
        for m in range(per_stride):
            for n in range(span):
                c = per_stride * j + m + n - (span - 1)
                if 0 <= c < n_rows - 1:
                    pool[j, c] += 1.0
    return jnp.asarray(pool, BF16)


def nsa_attention(qt, h, gate_col_block, k_cmp, v_cmp_t, ks, vs_t, kw, vw_t, bsz, seq):
    tk = min(NSA_KEY_TILE, seq)
    nq = seq // Q_BLOCK
    pool = _nsa_pool_matrix(seq)
    n_rows = k_cmp.shape[2]
    d = NSA_HEAD_DIM
    qblk = lambda b, i: (b * nq + i, 0)
    whole = lambda *shape: pl.BlockSpec((1, NSA_KV_GROUPS) + shape, lambda b, i: (b, 0, 0, 0))
    kern = functools.partial(_nsa_attn_kernel, seq, tk)
    return pl.pallas_call(
        kern,
        grid=(bsz, nq),
        in_specs=[pl.BlockSpec((1, NSA_WIDTH, Q_BLOCK), lambda b, i: (b, 0, i)),
                  pl.BlockSpec((Q_BLOCK, LANES), lambda b, i: (b * nq + i, gate_col_block)),
                  whole(n_rows, d), whole(NSA_VT_ROWS, n_rows), whole(seq, ks.shape[3]), whole(NSA_VT_ROWS, seq),
                  whole(seq, d), whole(NSA_VT_ROWS, seq),
                  pl.BlockSpec(pool.shape, lambda b, i: (0, 0))],
        out_specs=pl.BlockSpec((Q_BLOCK, NSA_WIDTH), qblk),
        out_shape=jax.ShapeDtypeStruct((bsz * seq, NSA_WIDTH), BF16),
        scratch_shapes=[pltpu.VMEM((NSA_KV_GROUPS, seq // SLC_BLOCK, Q_BLOCK), F32)],
        compiler_params=_params("parallel", "arbitrary"),
        name="nsa_attention",
    )(qt, h, k_cmp, v_cmp_t, ks, vs_t, kw, vw_t, pool)


def nsa_mixer(h, bsz, seq, gate_col_block, pe_k, pe_v, ck_w1, ck_b1, ck_w2, cv_w1, cv_b1, cv_w2):
    qt, kc, vc, ks, vs_t, kw, vw_t = nsa_prep(h, bsz, seq)
    k_cmp, v_cmp_t = nsa_compress(kc, vc, pe_k, pe_v, ck_w1, ck_b1, ck_w2, cv_w1, cv_b1, cv_w2)
    return nsa_attention(qt, h, gate_col_block, k_cmp, v_cmp_t, ks, vs_t, kw, vw_t, bsz, seq)


def _head_ones(width, head_dim):
    idx = np.arange(width) // head_dim
    return jnp.asarray(idx[:, None] == idx[None, :], BF16)


def _softplus(x):
    return jnp.maximum(x, 0.0) + jnp.log(1.0 + jnp.exp(-jnp.abs(x)))


def _rwkv_pre_kernel(p_ref, prev_ref, mu_ref, w0_ref, wup_ref, a0_ref, aup_ref, gup_ref, kk_ref, ka_ref, rk_ref,
                     ones_ref, r_o, k_o, v_o, kk_o, b_o, ld_o, g_o, bonus_o):
    w = RWKV_WIDTH
    p = p_ref[...]
    first_row = jnp.where(pl.program_id(1) == 0, 0.0, prev_ref[7:8, :])
    is_row0 = lax.broadcasted_iota(jnp.int32, p.shape, 0) == 0
    prev = jnp.where(is_row0, first_row, pltpu.roll(p, 1, 0))
    ps = p + (prev - p) * mu_ref[...]
    r, k, v = ps[:, 0:w], ps[:, w:2 * w], ps[:, 2 * w:3 * w]
    o = 3 * w
    w_lo = ps[:, o:o + RWKV_LORA_W]
    a_lo = ps[:, o + RWKV_LORA_W:o + RWKV_LORA_W + RWKV_LORA_A]
    g_lo = ps[:, o + RWKV_LORA_W + RWKV_LORA_A:]
    wlog = -_softplus(-(w0_ref[...] + _dot(jnp.tanh(w_lo).astype(BF16), wup_ref[...]))) - 0.5
    a = _sigmoid(a0_ref[...] + _dot(a_lo.astype(BF16), aup_ref[...]))
    g = _dot(_sigmoid(g_lo).astype(BF16), gup_ref[...])
    kk = k * kk_ref[...]
    norm = jnp.sqrt(_dot_split(kk * kk, ones_ref[...], 'a', 2))
    kk = kk / jnp.maximum(norm, 1e-12)
    k2 = k * (1.0 + (a - 1.0) * ka_ref[...])
    r_o[...] = r
    k_o[...] = k2
    v_o[...] = v
    kk_o[...] = kk
    b_o[...] = kk * a
    ld_o[...] = -jnp.exp(wlog)
    g_o[...] = g
    bonus_o[...] = _dot_split(r * k2 * rk_ref[...], ones_ref[...], 'a', 2) * v


def rwkv_pre(h, bsz, seq, mu, w0, w_up, a0, a_up, g_up, k_k, k_a, r_k, tl=512):
    tl = min(tl, seq)
    nl = seq // tl
    w = RWKV_WIDTH
    cols = RWKV_COLS
    ones = _head_ones(w, RWKV_HEAD_DIM)
    f2 = lambda b, l: (0, 0)
    vec = pl.BlockSpec((1, w), f2)
    out_spec = pl.BlockSpec((tl, w), lambda b, l: (b * nl + l, 0))
    out_shape = jax.ShapeDtypeStruct((bsz * seq, w), F32)
    return pl.pallas_call(
        _rwkv_pre_kernel,
        grid=(bsz, nl),
        in_specs=[pl.BlockSpec((tl, cols), lambda b, l: (b * nl + l, 0)),
                  pl.BlockSpec((8, cols), lambda b, l: (jnp.maximum((b * seq + l * tl) // 8 - 1, 0), 0)),
                  pl.BlockSpec((1, cols), f2), vec, pl.BlockSpec((RWKV_LORA_W, w), f2),
                  vec, pl.BlockSpec((RWKV_LORA_A, w), f2), pl.BlockSpec((RWKV_LORA_G, w), f2),
                  vec, vec, vec, pl.BlockSpec((w, w), f2)],
        out_specs=[out_spec] * 8,
        out_shape=[out_shape] * 8,
        compiler_params=_params("parallel", "parallel"),
        name="rwkv_pre",
    )(h, h, mu.reshape(1, cols), w0.reshape(1, w), w_up.astype(BF16), a0.reshape(1, w), a_up.astype(BF16),
      g_up.astype(BF16), k_k.reshape(1, w), k_a.reshape(1, w), r_k.reshape(1, w), ones)


def _rwkv_masks():
    t, pk = RWKV_CHUNK, RWKV_PACK
    n = t * pk
    ri = np.arange(n)
    same = (ri[:, None] // t) == (ri[None, :] // t)
    tt, ss = ri[:, None] % t, ri[None, :] % t
    levels = []
    k = 1
    while k < t:
        levels.append(same & (tt // (2 * k) == ss // (2 * k)) & ((tt // k) % 2 == 1) & ((ss // k) % 2 == 0))
        k *= 2
    lvl = np.stack(levels).astype(np.float32)
    tri = (np.arange(t)[:, None] >= np.arange(t)[None, :]).astype(np.float32)
    head_lane = ((ri[:, None] // t) == (np.arange(pk * RWKV_HEAD_DIM)[None, :] // RWKV_HEAD_DIM)).astype(np.float32)
    return (jnp.asarray(tri, BF16), jnp.asarray(head_lane), jnp.asarray(same.astype(np.float32)), jnp.asarray(lvl))


def _rwkv_chain(r, k, v, kk, b, ld, st, tri, hl, bd, lvl_ref):
    t, pk = RWKV_CHUNK, RWKV_PACK
    n = t * pk
    c = _dot_split(tri, ld, 'b', 3)
    yield
    c_end = c[t - 1:t, :]
    e_neg = jnp.exp(-c)
    e_end = jnp.exp(c_end - c)
    kkd = (kk * jnp.exp(c - ld)).astype(BF16)
    rd = (r * jnp.exp(c)).astype(BF16)

    def big(x):
        return (jnp.concatenate([x] * pk, axis=0) * hl).astype(BF16)

    st_b = st.astype(BF16)
    v_big = big(v)
    a_all = _dot_nt(jnp.concatenate([kkd, rd], axis=0),
                    jnp.concatenate([big(k * e_neg), big(b * e_neg)], axis=0))
    yield
    ti = lax.broadcasted_iota(jnp.int32, (t, n), 0)
    si = lax.broadcasted_iota(jnp.int32, (t, n), 1) % t
    strict = ti > si
    incl = ti >= si
    a_kk = jnp.where(strict, a_all[:t, :n], 0.0)
    a_kb = jnp.where(strict, a_all[:t, n:], 0.0)
    a_rk = jnp.where(incl, a_all[t:, :n], 0.0)
    a_rb = jnp.where(incl, a_all[t:, n:], 0.0)
    rhs = _dot(kkd, st_b) + _dot(a_kk.astype(BF16), v_big)
    yield
    a_bd = jnp.concatenate([a_kb] * pk, axis=0) * bd
    m = jnp.where(lax.broadcasted_iota(jnp.int32, (n, n), 0) == lax.broadcasted_iota(jnp.int32, (n, n), 1), 1.0, 0.0)
    for lv in range(lvl_ref.shape[0]):
        mb = m.astype(BF16)
        ma = _dot(mb, (a_bd * lvl_ref[lv]).astype(BF16)).astype(BF16)
        yield
        m = m - _dot(ma, mb)
        yield
    u_big = _dot(m.astype(BF16), big(rhs))
    yield
    u = u_big[0:t]
    for h in range(1, pk):
        u = u + u_big[h * t:(h + 1) * t]
    y = _dot(rd, st_b) + _dot(a_rk.astype(BF16), v_big) - _dot(a_rb.astype(BF16), big(u))
    yield
    decay_col = jnp.broadcast_to(jnp.exp(c_end), st.shape).T
    kb_end = jnp.concatenate([k * e_end, -(b * e_end)], axis=0).astype(BF16)
    vu = jnp.concatenate([v, u], axis=0).astype(BF16)
    return y, decay_col * st + bd * _dot_tn(kb_end, vu)


def _rwkv_chunk_kernel(r_ref, k_ref, v_ref, kk_ref, b_ref, ld_ref, tri_ref, hl_ref, bd_ref, lvl_ref, y_ref, st_ref):
    @pl.when(pl.program_id(0) == 0)
    def _():
        st_ref[...] = jnp.zeros_like(st_ref)

    wp = RWKV_PACK * RWKV_HEAD_DIM
    tri, hl, bd = tri_ref[...], hl_ref[...], bd_ref[...]
    n_pack = r_ref.shape[2] // wp
    where = [(bi, slice(g * wp, (g + 1) * wp)) for bi in range(r_ref.shape[0]) for g in range(n_pack)]
    loaded = [tuple(ref[bi, :, cols] for ref in (r_ref, k_ref, v_ref, kk_ref, b_ref, ld_ref)) + (st_ref[i],)
              for i, (bi, cols) in enumerate(where)]
    results = _run_interleaved([_rwkv_chain(*args, tri, hl, bd, lvl_ref) for args in loaded])
    for i, ((bi, cols), (y, st_new)) in enumerate(zip(where, results)):
        y_ref[bi, :, cols] = y
        st_ref[i] = st_new


def rwkv_chunk(r, k, v, kk, b, ld, bsz, seq):
    t, pk = RWKV_CHUNK, RWKV_PACK
    n_chunk = seq // t
    w = RWKV_WIDTH
    wp = pk * RWKV_HEAD_DIM
    assert t == RWKV_HEAD_DIM
    tri, hl, bd, lvl = _rwkv_masks()
    blk = pl.BlockSpec((bsz, t, w), lambda c: (0, c, 0))
    f2 = lambda c: (0, 0)
    shaped = lambda a: a.reshape(bsz, seq, w)
    y = pl.pallas_call(
        _rwkv_chunk_kernel,
        grid=(n_chunk,),
        in_specs=[blk] * 6 + [pl.BlockSpec(tri.shape, f2), pl.BlockSpec(hl.shape, f2), pl.BlockSpec(bd.shape, f2),
                              pl.BlockSpec(lvl.shape, lambda c: (0, 0, 0))],
        out_specs=blk,
        out_shape=jax.ShapeDtypeStruct((bsz, seq, w), F32),
        scratch_shapes=[pltpu.VMEM((bsz * (w // wp), wp, wp), F32)],
        compiler_params=_params("arbitrary"),
        name="rwkv_chunk",
    )(shaped(r), shaped(k), shaped(v), shaped(kk), shaped(b), shaped(ld), tri, hl, bd, lvl)
    return y.reshape(bsz * seq, w)


def _rwkv_post_kernel(y_ref, bonus_ref, g_ref, lng_ref, lnb_ref, ones_ref, o_ref):
    y = y_ref[...]
    inv = 1.0 / RWKV_HEAD_DIM
    mu = _dot_split(y, ones_ref[...], 'a', 2) * inv
    yc = y - mu
    var = _dot_split(yc * yc, ones_ref[...], 'a', 2) * inv
    yn = yc * lax.rsqrt(var + RWKV_GN_EPS) * lng_ref[...] + lnb_ref[...]
    o_ref[...] = ((yn + bonus_ref[...]) * g_ref[...]).astype(o_ref.dtype)


def rwkv_post(y, bonus, g, ln_g, ln_b, tm=1024):
    n, w = y.shape
    tm = min(tm, n)
    row = pl.BlockSpec((tm, w), lambda i: (i, 0))
    vec = pl.BlockSpec((1, w), lambda i: (0, 0))
    return pl.pallas_call(
        _rwkv_post_kernel,
        grid=(n // tm,),
        in_specs=[row, row, row, vec, vec, pl.BlockSpec((w, w), lambda i: (0, 0))],
        out_specs=row,
        out_shape=jax.ShapeDtypeStruct((n, w), BF16),
        compiler_params=_params("parallel"),
        name="rwkv_post",
    )(y, bonus, g, ln_g.reshape(1, w), ln_b.reshape(1, w), _head_ones(w, RWKV_HEAD_DIM))


def rwkv7_mixer(h, bsz, seq, mu, w0, w_up, a0, a_up, g_up, k_k, k_a, r_k, ln_g, ln_b):
    r, k, v, kk, b, ld, g, bonus = rwkv_pre(h, bsz, seq, mu, w0, w_up, a0, a_up, g_up, k_k, k_a, r_k)
    y = rwkv_chunk(r, k, v, kk, b, ld, bsz, seq)
    return rwkv_post(y, bonus, g, ln_g, ln_b)


AB_IN = S5_WIDTH + NSA_WIDTH + 6 * NSA_KV_W + NSA_GATE_COLS
AB_IN_PADDED = -(-AB_IN // LANES) * LANES
NSA_GATE_COL_BLOCK = (AB_IN - NSA_GATE_COLS) // LANES
PROJ_TM = 512


def kernel(x, ab_w_in, ab_w_out, s5_lam_re, s5_lam_im, s5_log_dt, s5_b_re, s5_b_im, s5_c_re, s5_c_im, s5_d, s5_w_glu, s5_b_glu, nsa_pe_k, nsa_pe_v, nsa_ck_w1, nsa_ck_b1, nsa_ck_w2, nsa_cv_w1, nsa_cv_b1, nsa_cv_w2, cd_w_in, cd_w_out, rwkv_mu, rwkv_w0, rwkv_w_up, rwkv_a0, rwkv_a_up, rwkv_g_up, rwkv_k_k, rwkv_k_a, rwkv_r_k, rwkv_ln_g, rwkv_ln_b, ret_ln_g, ret_ln_b, ln1_g, ln1_b, ln2_g, ln2_b, moe_router, moe_bias, moe_w1, moe_w3, moe_w2, sh_w1, sh_w3, sh_w2):
    bsz, seq, d = x.shape
    assert (AB_IN - NSA_GATE_COLS) % LANES == 0
    xf = x.reshape(bsz * seq, d)
    x_in = xf
    for layer in range(DEPTH):
        i = layer // 2
        if layer % 2 == 0:
            w_in = jnp.pad(ab_w_in[i], ((0, 0), (0, AB_IN_PADDED - AB_IN))).astype(BF16)
            h, u3 = project(x_in, w_in, PROJ_TM, chunked=(S5_CHUNK, S5_WIDTH))
            y_1 = s5_mixer(h, u3, bsz, seq, s5_lam_re[i], s5_lam_im[i], s5_log_dt[i], s5_b_re[i], s5_b_im[i],
                           s5_c_re[i], s5_c_im[i], s5_d[i], s5_w_glu[i], s5_b_glu[i])
            y_2 = nsa_mixer(h, bsz, seq, NSA_GATE_COL_BLOCK, nsa_pe_k[i], nsa_pe_v[i], nsa_ck_w1[i], nsa_ck_b1[i],
                            nsa_ck_w2[i], nsa_cv_w1[i], nsa_cv_b1[i], nsa_cv_w2[i])
            w_out = ab_w_out[i]
        else:
            h = project(x_in, cd_w_in[i].astype(BF16), PROJ_TM)
            y_1 = rwkv7_mixer(h, bsz, seq, rwkv_mu[i], rwkv_w0[i], rwkv_w_up[i], rwkv_a0[i], rwkv_a_up[i],
                              rwkv_g_up[i], rwkv_k_k[i], rwkv_k_a[i], rwkv_r_k[i], rwkv_ln_g[i], rwkv_ln_b[i])
            y_2 = retention_mixer(h, bsz, seq, RWKV_COLS, ret_ln_g[i], ret_ln_b[i])
            w_out = cd_w_out[i]
        xf, gates = out_proj_ln_route(y_1, y_2, w_out, xf, ln1_g[layer], ln1_b[layer],
                                      moe_router[layer], moe_bias[layer])
        xf, x_in = moe_experts_ln(xf, gates, moe_w1, moe_w3, moe_w2, layer,
                                  sh_w1, sh_w3, sh_w2[layer], ln2_g[layer], ln2_b[layer])
    return xf.reshape(bsz, seq, d)
```

```python
import functools
import math

import jax
import jax.numpy as jnp
import numpy as np
from jax import lax
from jax.experimental import pallas as pl
from jax.experimental.pallas import tpu as pltpu

F32 = jnp.float32
BF16 = jnp.bfloat16
FP8 = jnp.float8_e4m3fn
FP8_MAX = 448.0
FP8_TINY = 1e-30

VMEM_LIMIT_BYTES = 52 * 1024 * 1024
LANES = 128

LN_EPS = 1e-5
DEPTH = 2
ALPHA = (2 * DEPTH) ** 0.25

S5_GROUPS, S5_GROUP_CH, S5_STATE = 32, 16, 64
S5_WIDTH = S5_GROUPS * S5_GROUP_CH
S5_CHUNK = 16
S5_PACK = 8
NSA_HEADS, NSA_KV_GROUPS, NSA_HEAD_DIM = 8, 2, 64
NSA_HPG = NSA_HEADS // NSA_KV_GROUPS
NSA_WIDTH = NSA_HEADS * NSA_HEAD_DIM
NSA_ROT_DIM = NSA_HEAD_DIM // 4
ROPE_THETA = 500000.0
CMP_BLOCK, CMP_STRIDE, CMP_HIDDEN = 32, 16, 128
SLC_BLOCK, N_SLC, WINDOW = 64, 16, 512
Q_BLOCK = 256
NSA_KEY_TILE = 1024
FORCE_SCORE = 1e6
MASK_VALUE = -1e30
RWKV_HEADS, RWKV_HEAD_DIM = 8, 64
RWKV_WIDTH = RWKV_HEADS * RWKV_HEAD_DIM
RWKV_LORA_W, RWKV_LORA_A, RWKV_LORA_G = 64, 64, 128
RWKV_COLS = 3 * RWKV_WIDTH + RWKV_LORA_W + RWKV_LORA_A + RWKV_LORA_G
RWKV_GN_EPS = 64e-5
RWKV_CHUNK = 64
RWKV_PACK = 4
RET_HEADS, RET_DK, RET_DV, RET_CHUNK = 4, 64, 128, 128
RET_THETA = 10000.0
RET_GN_EPS = 1e-5
N_EXPERTS, TOP_K, EXPERT_FF = 64, 8, 256
N_EXPERT_GROUPS, TOPK_GROUPS = 8, 4
EXPERTS_PER_GROUP = N_EXPERTS // N_EXPERT_GROUPS
ROUTED_SCALE = 2.5
MOE_EXPERTS_PER_STEP = 4


def _params(*sem):
    return pltpu.CompilerParams(dimension_semantics=sem, vmem_limit_bytes=VMEM_LIMIT_BYTES)


def _dot(a, b, **kw):
    return jnp.dot(a, b, preferred_element_type=F32, **kw)


def _dot_nt(a, b, **kw):
    return lax.dot_general(a, b, (((1,), (1,)), ((), ())), preferred_element_type=F32, **kw)


def _dot_tn(a, b, **kw):
    return lax.dot_general(a, b, (((0,), (0,)), ((), ())), preferred_element_type=F32, **kw)


def _dot_split(a, b, split, parts):
    rest = a if split == 'a' else b
    acc = None
    for _ in range(parts):
        piece = rest.astype(BF16)
        term = _dot(piece, b) if split == 'a' else _dot(a, piece)
        acc = term if acc is None else acc + term
        rest = rest - piece.astype(F32)
    return acc


def _run_interleaved(gens):
    results = [None] * len(gens)
    live = list(range(len(gens)))
    while live:
        for i in list(live):
            try:
                next(gens[i])
            except StopIteration as done:
                results[i] = done.value
                live.remove(i)
    return results


def _gelu(x):
    return 0.5 * x * (1.0 + jnp.tanh(math.sqrt(2.0 / math.pi) * (x + 0.044715 * (x * x * x))))


def _sigmoid(x):
    return 1.0 / (1.0 + jnp.exp(-x))


def _layer_norm_rows(z, g, b):
    mu = jnp.mean(z, axis=-1, keepdims=True)
    zc = z - mu
    var = jnp.mean(zc * zc, axis=-1, keepdims=True)
    return zc * lax.rsqrt(var + LN_EPS) * g + b


def _proj_kernel(x_ref, w_ref, o_ref, *chunked_ref):
    y = _dot(x_ref[...].astype(BF16), w_ref[...])
    o_ref[...] = y
    for c_ref in chunked_ref:
        rows, t, w = c_ref.shape
        c_ref[...] = y[:, :w].reshape(rows, t, w)


def project(x, w_bf16, tm, chunked=None):
    m, k = x.shape
    n = w_bf16.shape[1]
    out_specs = [pl.BlockSpec((tm, n), lambda i: (i, 0))]
    out_shape = [jax.ShapeDtypeStruct((m, n), F32)]
    if chunked is not None:
        t, w = chunked
        out_specs.append(pl.BlockSpec((tm // t, t, w), lambda i: (i, 0, 0)))
        out_shape.append(jax.ShapeDtypeStruct((m // t, t, w), F32))
    out = pl.pallas_call(
        _proj_kernel,
        grid=(m // tm,),
        in_specs=[pl.BlockSpec((tm, k), lambda i: (i, 0)), pl.BlockSpec((k, n), lambda i: (0, 0))],
        out_specs=out_specs,
        out_shape=out_shape,
        compiler_params=_params("parallel"),
        name="project",
    )(x, w_bf16)
    return out if chunked is not None else out[0]


def _out_proj_ln_kernel(ya_ref, yb_ref, wa_ref, wb_ref, x_ref, g_ref, b_ref, rt_ref, rbias_ref, o_ref, gate_ref):
    mix = _dot(ya_ref[...], wa_ref[...]) + _dot(yb_ref[...], wb_ref[...])
    y = _layer_norm_rows(ALPHA * x_ref[...] + mix, g_ref[...], b_ref[...])
    o_ref[...] = y
    gate_ref[...] = _route(y, rt_ref[...], rbias_ref[...])


def out_proj_ln_route(ya, yb, w_out, x, g, b, router, router_bias, tm=512):
    n, d = x.shape
    ka, kb = ya.shape[1], yb.shape[1]
    wa = w_out[:ka].astype(BF16)
    wb = w_out[ka:].astype(BF16)
    row = lambda i: (i, 0)
    fixed = lambda i: (0, 0)
    return pl.pallas_call(
        _out_proj_ln_kernel,
        grid=(n // tm,),
        in_specs=[pl.BlockSpec((tm, ka), row), pl.BlockSpec((tm, kb), row),
                  pl.BlockSpec((ka, d), fixed), pl.BlockSpec((kb, d), fixed),
                  pl.BlockSpec((tm, d), row), pl.BlockSpec((1, d), fixed), pl.BlockSpec((1, d), fixed),
                  pl.BlockSpec((N_EXPERTS, d), fixed), pl.BlockSpec((N_EXPERTS, 1), fixed)],
        out_specs=[pl.BlockSpec((tm, d), row), pl.BlockSpec((tm, LANES), row)],
        out_shape=[jax.ShapeDtypeStruct((n, d), F32), jax.ShapeDtypeStruct((n, LANES), F32)],
        compiler_params=_params("parallel"),
        name="out_proj_ln_route",
    )(ya, yb, wa, wb, x, g.reshape(1, d), b.reshape(1, d), router.T, router_bias.reshape(N_EXPERTS, 1))


def _route(x, rt, bias):
    tr = x.shape[0]
    x_hi, rt_hi = x.astype(BF16), rt.astype(BF16)
    x_lo, rt_lo = (x - x_hi.astype(F32)).astype(BF16), (rt - rt_hi.astype(F32)).astype(BF16)
    scores = _sigmoid(_dot_nt(rt_hi, x_hi) + _dot_nt(rt_hi, x_lo) + _dot_nt(rt_lo, x_hi))
    biased = scores + bias
    grp = biased.reshape(N_EXPERT_GROUPS, EXPERTS_PER_GROUP, tr)
    pos = lax.broadcasted_iota(jnp.int32, grp.shape, 1)
    m1 = jnp.max(grp, axis=1, keepdims=True)
    first = jnp.min(jnp.where(grp == m1, pos, EXPERTS_PER_GROUP), axis=1, keepdims=True)
    m2 = jnp.max(jnp.where(pos == first, -jnp.inf, grp), axis=1, keepdims=True)
    gscore = (m1 + m2).reshape(N_EXPERT_GROUPS, tr)
    gidx = lax.broadcasted_iota(jnp.int32, gscore.shape, 0)
    grank = jnp.zeros(gscore.shape, F32)
    for j in range(N_EXPERT_GROUPS):
        row = gscore[j:j + 1, :]
        grank = grank + jnp.where(gidx > j, jnp.where(row >= gscore, 1.0, 0.0), jnp.where(row > gscore, 1.0, 0.0))
    gkeep = jnp.where(grank < TOPK_GROUPS, 1.0, 0.0)
    keep = jnp.broadcast_to(gkeep[:, None, :], grp.shape).reshape(N_EXPERTS, tr)
    masked = jnp.where(keep > 0.5, biased, -jnp.inf)
    eidx = lax.broadcasted_iota(jnp.int32, masked.shape, 0)
    rank = jnp.zeros(masked.shape, F32)
    for j in range(N_EXPERTS):
        row = masked[j:j + 1, :]
        rank = rank + jnp.where(eidx > j, jnp.where(row >= masked, 1.0, 0.0), jnp.where(row > masked, 1.0, 0.0))
    gate = jnp.where(rank < TOP_K, scores, 0.0)
    gate = gate / jnp.sum(gate, axis=0, keepdims=True) * ROUTED_SCALE
    return jnp.concatenate([gate, jnp.zeros((LANES - N_EXPERTS, tr), F32)], axis=0).T


def _quantize_fp8(a, axes):
    amax = jnp.max(jnp.abs(a), axis=axes, keepdims=True)
    scale = jnp.maximum(amax, FP8_TINY) * (1.0 / FP8_MAX)
    return (a * (1.0 / scale)).astype(FP8), scale


def _swiglu_hidden(xq, x_scale, w1q, w3q, w_scale, gate=None):
    col1 = x_scale * w_scale[0:1, 0:1]
    col3 = x_scale * w_scale[1:2, 0:1]
    if gate is not None:
        col3 = col3 * gate
    h1 = _dot(xq, w1q) * col1
    return h1 * _sigmoid(h1) * (_dot(xq, w3q) * col3)


def _experts_ln_kernel(x_ref, gate_ref, w1_ref, w3_ref, ws_ref, w2_ref, sw1_ref, sw3_ref, sws_ref, sw2_ref,
                       g_ref, b_ref, o_ref, obf_ref, acc_ref, xq_ref, xs_ref):
    step = pl.program_id(1)
    per_step = w1_ref.shape[0]

    @pl.when(step == 0)
    def _():
        xq, xs = _quantize_fp8(x_ref[...], (1,))
        xq_ref[...] = xq
        xs_ref[...] = xs
        h = _swiglu_hidden(xq, xs, sw1_ref[...], sw3_ref[...], sws_ref[...])
        acc_ref[...] = _dot(h.astype(BF16), sw2_ref[...].astype(BF16))

    lane = lax.broadcasted_iota(jnp.int32, gate_ref.shape, 1)
    gates = gate_ref[...]
    xq, xs = xq_ref[...], xs_ref[...]
    hidden = []
    for j in range(per_step):
        gcol = jnp.sum(jnp.where(lane == step * per_step + j, gates, 0.0), axis=1, keepdims=True)
        hidden.append(_swiglu_hidden(xq, xs, w1_ref[j], w3_ref[j], ws_ref[j], gcol).astype(BF16))
    w2 = w2_ref[0].astype(BF16)
    acc_ref[...] += _dot(jnp.concatenate(hidden, axis=1), w2.reshape(per_step * w2.shape[1], w2.shape[2]))

    @pl.when(step == pl.num_programs(1) - 1)
    def _():
        y = _layer_norm_rows(ALPHA * x_ref[...] + acc_ref[...], g_ref[...], b_ref[...])
        o_ref[...] = y
        obf_ref[...] = y.astype(BF16)


def _quantize_weights_kernel(w1_ref, w3_ref, q1_ref, q3_ref, s_ref):
    for e in range(q1_ref.shape[0]):
        q1, s1 = _quantize_fp8(w1_ref[0, e], (0, 1))
        q3, s3 = _quantize_fp8(w3_ref[0, e], (0, 1))
        q1_ref[e] = q1
        q3_ref[e] = q3
        ff = s_ref.shape[2]
        s_ref[e] = jnp.concatenate([jnp.broadcast_to(s1, (1, ff)), jnp.broadcast_to(s3, (1, ff))], axis=0)


def _quantize_expert_weights(w1_layers, w3_layers, layer):
    _, ne, d, ff = w1_layers.shape
    per_step = math.gcd(ne, MOE_EXPERTS_PER_STEP)
    blk = lambda *shape: pl.BlockSpec((per_step,) + shape, lambda i: (i, 0, 0))
    src = pl.BlockSpec((1, per_step, d, ff), lambda i: (layer, i, 0, 0))
    return pl.pallas_call(
        _quantize_weights_kernel,
        grid=(ne // per_step,),
        in_specs=[src, src],
        out_specs=[blk(d, ff), blk(d, ff), blk(2, ff)],
        out_shape=[jax.ShapeDtypeStruct((ne, d, ff), FP8), jax.ShapeDtypeStruct((ne, d, ff), FP8),
                   jax.ShapeDtypeStruct((ne, 2, ff), F32)],
        compiler_params=_params("parallel"),
        name="quantize_expert_weights",
    )(w1_layers, w3_layers)


def moe_experts_ln(x, gates, w1_layers, w3_layers, w2_layers, layer, sw1_layers, sw3_layers, sw2, g, b, tm=1024):
    n, d = x.shape
    ne = w1_layers.shape[1]
    w1q, w3q, ws = _quantize_expert_weights(w1_layers, w3_layers, layer)
    sw1q, sw3q, sws = (a[0] for a in _quantize_expert_weights(sw1_layers[:, None], sw3_layers[:, None], layer))
    tok = lambda i, e: (i, 0)
    fixed = lambda i, e: (0, 0)
    per_expert = lambda *blk: pl.BlockSpec((MOE_EXPERTS_PER_STEP,) + blk, lambda i, e: (e, 0, 0))
    return pl.pallas_call(
        _experts_ln_kernel,
        grid=(n // tm, ne // MOE_EXPERTS_PER_STEP),
        in_specs=[pl.BlockSpec((tm, d), tok), pl.BlockSpec((tm, LANES), tok),
                  per_expert(d, EXPERT_FF), per_expert(d, EXPERT_FF), per_expert(2, EXPERT_FF),
                  pl.BlockSpec((1, MOE_EXPERTS_PER_STEP, EXPERT_FF, d), lambda i, e: (layer, e, 0, 0)),
                  pl.BlockSpec((d, EXPERT_FF), fixed), pl.BlockSpec((d, EXPERT_FF), fixed),
                  pl.BlockSpec((2, EXPERT_FF), fixed), pl.BlockSpec((EXPERT_FF, d), fixed),
                  pl.BlockSpec((1, d), fixed), pl.BlockSpec((1, d), fixed)],
        out_specs=[pl.BlockSpec((tm, d), tok), pl.BlockSpec((tm, d), tok)],
        out_shape=[jax.ShapeDtypeStruct((n, d), F32), jax.ShapeDtypeStruct((n, d), BF16)],
        scratch_shapes=[pltpu.VMEM((tm, d), F32), pltpu.VMEM((tm, d), FP8), pltpu.VMEM((tm, 1), F32)],
        compiler_params=_params("parallel", "arbitrary"),
        name="moe_experts_ln",
    )(x, gates, w1q, w3q, ws, w2_layers, sw1q, sw3q, sws, sw2, g.reshape(1, d), b.reshape(1, d))


def _s5_tables(lam_re, lam_im, log_dt, b_re, b_im, c_re, c_im, n_chunk):
    t, h, p = S5_CHUNK, S5_GROUP_CH, S5_STATE
    dt = jnp.exp(log_dt.astype(F32))[:, None]
    den = lam_re ** 2 + lam_im ** 2

    def lam_pow(k):
        k = jnp.asarray(k, F32)[..., None, None]
        mag = jnp.exp(lam_re * dt * k)
        return mag * jnp.cos(lam_im * dt * k), mag * jnp.sin(lam_im * dt * k)

    lb_re, lb_im = lam_pow(1.0)
    f_re = ((lb_re - 1.0) * lam_re + lb_im * lam_im) / den
    f_im = (lb_im * lam_re - (lb_re - 1.0) * lam_im) / den
    bb_re = f_re[..., None] * b_re - f_im[..., None] * b_im
    bb_im = f_re[..., None] * b_im + f_im[..., None] * b_re
    pr, pi = lam_pow(jnp.arange(t))
    cl_re = c_re[None] * pr[:, :, None, :] - c_im[None] * pi[:, :, None, :]
    cl_im = c_re[None] * pi[:, :, None, :] + c_im[None] * pr[:, :, None, :]
    klag = jnp.einsum('tgop,gpi->tgoi', cl_re, bb_re) - jnp.einsum('tgop,gpi->tgoi', cl_im, bb_im)
    nb = S5_GROUPS // S5_PACK
    split = lambda a, axis: a.reshape(a.shape[:axis] + (nb, S5_PACK) + a.shape[axis + 1:])
    eye = jnp.eye(S5_PACK, dtype=F32)
    lag_t = jnp.transpose(split(klag, 1), (1, 0, 2, 4, 3))
    lag_t = (lag_t[:, :, :, :, None, :] * eye[None, None, :, None, :, None]).reshape(nb, t, LANES, LANES)
    qr, qi = lam_pow(t - 1 - jnp.arange(t))
    st_re = qr[..., None] * bb_re[None] - qi[..., None] * bb_im[None]
    st_im = qr[..., None] * bb_im[None] + qi[..., None] * bb_re[None]
    st = jnp.stack([st_re, st_im], axis=0)
    st_t = jnp.transpose(split(st, 2), (2, 1, 3, 5, 0, 4)).reshape(nb, t, LANES, 2 * p)
    er, ei = lam_pow(jnp.arange(t) + 1)
    x_re = c_re[None] * er[:, :, None, :] - c_im[None] * ei[:, :, None, :]
    x_im = c_re[None] * ei[:, :, None, :] + c_im[None] * er[:, :, None, :]
    cr = jnp.stack([x_re, -x_im], axis=0)
    cr_t = jnp.transpose(split(cr, 2), (2, 1, 0, 5, 3, 4)).reshape(nb, t, 2 * p, LANES)
    levels = max(1, int(math.log2(n_chunk)))
    sr, si = lam_pow(t * (2.0 ** jnp.arange(levels)))
    sr = sr.reshape(levels, nb, S5_PACK * p)
    si = si.reshape(levels, nb, S5_PACK * p)
    a1 = jnp.concatenate([sr, sr], axis=-1)
    a2 = jnp.concatenate([-si, si], axis=-1)
    scan = jnp.transpose(jnp.stack([a1, a2], axis=1), (2, 0, 1, 3))
    return lag_t, st_t, cr_t, scan.astype(F32)


def _s5_build_tables(lag_ref, st_ref, cr_ref, wtoe_ref, wstate_ref, wcross_ref):
    t = lag_ref.shape[1]
    p = S5_STATE
    kp = S5_PACK * p
    wtoe_ref[...] = jnp.zeros_like(wtoe_ref)
    for d in range(t):
        tile = lag_ref[0, d].astype(BF16)
        for j in range(t - d):
            wtoe_ref[j * LANES:(j + 1) * LANES, (j + d) * LANES:(j + d + 1) * LANES] = tile
    lane = lax.broadcasted_iota(jnp.int32, (LANES, LANES), 1)
    row_g = lax.broadcasted_iota(jnp.int32, (LANES, kp), 0) // S5_GROUP_CH
    same_s = row_g == lax.broadcasted_iota(jnp.int32, (LANES, kp), 1) // p
    for j in range(t):
        a = st_ref[0, j]
        swapped = pltpu.roll(a, p, 1)
        for c, both in enumerate((jnp.where(lane < p, a, swapped), jnp.where(lane < p, swapped, a))):
            wide = jnp.concatenate([both] * (kp // LANES), axis=1)
            wstate_ref[j * LANES:(j + 1) * LANES, c * kp:(c + 1) * kp] = jnp.where(same_s, wide, 0.0).astype(BF16)
    same_c = (lax.broadcasted_iota(jnp.int32, (kp, LANES), 0) // p
              == lax.broadcasted_iota(jnp.int32, (kp, LANES), 1) // S5_GROUP_CH)
    for i in range(t):
        a = cr_ref[0, i]
        for c in range(2):
            tall = jnp.concatenate([a[c * p:(c + 1) * p]] * S5_PACK, axis=0)
            wcross_ref[c * kp:(c + 1) * kp, i * LANES:(i + 1) * LANES] = jnp.where(same_c, tall, 0.0).astype(BF16)


def _s5_kernel(u_ref, lag_ref, st_ref, cr_ref, scan_ref, o_ref, wtoe_ref, wstate_ref, wcross_ref):
    @pl.when(pl.program_id(1) == 0)
    def _():
        _s5_build_tables(lag_ref, st_ref, cr_ref, wtoe_ref, wstate_ref, wcross_ref)

    n_chunk, t, _ = u_ref.shape
    x = jnp.concatenate([u_ref[:, j, :] for j in range(t)], axis=1).astype(BF16)
    local = _dot(x, wtoe_ref[...])
    state = _dot(x, wstate_ref[...])
    row = lax.broadcasted_iota(jnp.int32, state.shape, 0)
    s = jnp.where(row >= 1, pltpu.roll(state, 1, 0), 0.0)
    half = state.shape[1] // 2
    level = 0
    d = 1
    while d < n_chunk:
        mult = scan_ref[0, level]
        prev = jnp.where(row >= d, pltpu.roll(s, d, 0), 0.0)
        s = s + mult[0:1, :] * prev + mult[1:2, :] * pltpu.roll(prev, half, 1)
        d *= 2
        level += 1
    y = local + _dot(s.astype(BF16), wcross_ref[...])
    for i in range(t):
        o_ref[:, i, :] = y[:, i * LANES:(i + 1) * LANES]


def s5_scan(u3, bsz, lag_t, st_t, cr_t, scan):
    rows, t, w = u3.shape
    n_chunk = rows // bsz
    kp2 = 2 * S5_PACK * S5_STATE
    table = lambda a: pl.BlockSpec((1,) + a.shape[1:], lambda j, b: (j, 0, 0, 0))
    return pl.pallas_call(
        _s5_kernel,
        grid=(w // LANES, bsz),
        in_specs=[pl.BlockSpec((n_chunk, t, LANES), lambda j, b: (b, 0, j)),
                  table(lag_t), table(st_t), table(cr_t), table(scan)],
        out_specs=pl.BlockSpec((n_chunk, t, LANES), lambda j, b: (b, 0, j)),
        out_shape=jax.ShapeDtypeStruct(u3.shape, F32),
        scratch_shapes=[pltpu.VMEM((t * LANES, t * LANES), BF16), pltpu.VMEM((t * LANES, kp2), BF16),
                        pltpu.VMEM((kp2, t * LANES), BF16)],
        compiler_params=_params("arbitrary", "arbitrary"),
        name="s5_scan",
    )(u3, lag_t, st_t, cr_t, scan)


def _s5_post_kernel(y_ref, u_ref, d_ref, w_ref, b_ref, o_ref):
    u = u_ref[...]
    y = _gelu(y_ref[...].reshape(u.shape) + d_ref[...] * u)
    o_ref[...] = (y * _sigmoid(_dot(y.astype(BF16), w_ref[...]) + b_ref[...])).astype(o_ref.dtype)


def s5_post(y3, h, d_skip, w_glu, b_glu, tm=1024):
    rows, t, w = y3.shape
    n = rows * t
    tm = min(tm, n)
    row = lambda i: (i, 0)
    fixed = lambda i: (0, 0)
    return pl.pallas_call(
        _s5_post_kernel,
        grid=(n // tm,),
        in_specs=[pl.BlockSpec((tm // t, t, w), lambda i: (i, 0, 0)), pl.BlockSpec((tm, w), row),
                  pl.BlockSpec((1, w), fixed), pl.BlockSpec((w, w), fixed), pl.BlockSpec((1, w), fixed)],
        out_specs=pl.BlockSpec((tm, w), row),
        out_shape=jax.ShapeDtypeStruct((n, w), BF16),
        compiler_params=_params("parallel"),
        name="s5_post",
    )(y3, h, d_skip.reshape(1, w), w_glu.astype(BF16), b_glu.reshape(1, w))


def s5_mixer(h, u3, bsz, seq, lam_re, lam_im, log_dt, b_re, b_im, c_re, c_im, d_skip, w_glu, b_glu):
    tables = _s5_tables(lam_re, lam_im, log_dt, b_re, b_im, c_re, c_im, seq // S5_CHUNK)
    return s5_post(s5_scan(u3, bsz, *tables), h, d_skip, w_glu, b_glu)


def _rope_tables(pos, rot_dim, theta, head_dim, n_heads):
    half = rot_dim // 2
    f32 = np.float32
    inv_freq = f32(theta) ** (-np.arange(half, dtype=f32) / f32(half))
    ang = (pos.astype(f32)[:, None] * inv_freq[None, :]).astype(np.float64)
    cos, sin = np.cos(ang), np.sin(ang)
    rest = head_dim - rot_dim
    n = pos.shape[0]
    c = np.concatenate([cos, cos, np.ones((n, rest))], axis=1)
    s_up = np.concatenate([-sin, np.zeros((n, half + rest))], axis=1)
    s_dn = np.concatenate([np.zeros((n, half)), sin, np.zeros((n, rest))], axis=1)
    tile = lambda a: jnp.asarray(np.tile(a, (1, n_heads)), F32)
    return tile(c), tile(s_up), tile(s_dn)


def _rope_apply(x, c, s_up, s_dn, half):
    return x * c + pltpu.roll(x, LANES - half, 1) * s_up + pltpu.roll(x, half, 1) * s_dn


def _retention_tables():
    c = RET_CHUNK
    log_gamma = np.log(1.0 - 2.0 ** (-5.0 - np.arange(RET_HEADS, dtype=np.float64)))
    i = np.arange(c, dtype=np.float64)
    diff = i[:, None] - i[None, :]
    decay = np.where(diff >= 0, np.exp(diff[None] * log_gamma[:, None, None]), 0.0)
    qdec = np.repeat(np.exp((i + 1.0)[:, None] * log_gamma[None, :]), RET_DK, axis=1)
    kdec = np.repeat(np.exp((c - 1.0 - i)[:, None] * log_gamma[None, :]), RET_DK, axis=1)
    chunk_decay = [float(v) for v in np.exp(c * log_gamma)]
    return jnp.asarray(decay, F32), jnp.asarray(qdec, F32), jnp.asarray(kdec, F32), chunk_decay


def _retention_kernel(chunk_decay, q_ref, k_ref, v0_ref, v1_ref, g0_ref, g1_ref, c_ref, su_ref, sd_ref,
                      dec_ref, qdec_ref, kdec_ref, lng_ref, lnb_ref, o_ref, state_ref):
    @pl.when(pl.program_id(1) == 0)
    def _():
        state_ref[...] = jnp.zeros_like(state_ref)

    half = RET_DK // 2
    tabs = (c_ref[...], su_ref[...], sd_ref[...])
    q = jnp.concatenate([_rope_apply(q_ref[:, s:s + LANES], *tabs, half) for s in (0, LANES)], axis=1)
    k = jnp.concatenate([_rope_apply(k_ref[:, s:s + LANES], *tabs, half) for s in (0, LANES)], axis=1)
    k = k * (RET_DK ** -0.5)
    q_dec = q * qdec_ref[...]
    k_dec = k * kdec_ref[...]
    v = jnp.concatenate([v0_ref[...], v1_ref[...]], axis=1)
    gate = jnp.concatenate([g0_ref[...], g1_ref[...]], axis=1)
    states = [state_ref[h] for h in range(RET_HEADS)]

    def head(h):
        ks = slice(h * RET_DK, (h + 1) * RET_DK)
        vh = v[:, h * RET_DV:(h + 1) * RET_DV].astype(BF16)
        scores = _dot_nt(q[:, ks].astype(BF16), k[:, ks].astype(BF16)) * dec_ref[h]
        yield
        y = _dot(scores.astype(BF16), vh) + _dot(q_dec[:, ks].astype(BF16), states[h].astype(BF16))
        yield
        new_state = states[h] * chunk_decay[h] + _dot_tn(k_dec[:, ks].astype(BF16), vh)
        yield
        mu = jnp.mean(y, axis=-1, keepdims=True)
        yc = y - mu
        var = jnp.mean(yc * yc, axis=-1, keepdims=True)
        return yc * lax.rsqrt(var + RET_GN_EPS), new_state

    results = _run_interleaved([head(h) for h in range(RET_HEADS)])
    outs = [r[0] for r in results]
    for h, (_, new_state) in enumerate(results):
        state_ref[h] = new_state
    yn = jnp.concatenate(outs, axis=1) * lng_ref[...] + lnb_ref[...]
    o_ref[...] = (gate * _sigmoid(gate) * yn).astype(o_ref.dtype)


def retention_mixer(h, bsz, seq, col0, ln_g, ln_b):
    c = RET_CHUNK
    n_chunk = seq // c
    qk_w = RET_HEADS * RET_DK
    v_w = RET_HEADS * RET_DV
    assert col0 % qk_w == 0 and qk_w == 2 * LANES and v_w == 2 * qk_w
    cb = col0 // qk_w
    rc, rsu, rsd = _rope_tables(np.arange(seq), RET_DK, RET_THETA, RET_DK, 2)
    dec, qdec, kdec, chunk_decay = _retention_tables()
    row = lambda j: (lambda b, n: (b * n_chunk + n, j))
    pos = lambda b, n: (n, 0)
    fixed2 = lambda b, n: (0, 0)
    kern = functools.partial(_retention_kernel, chunk_decay)
    return pl.pallas_call(
        kern,
        grid=(bsz, n_chunk),
        in_specs=[pl.BlockSpec((c, qk_w), row(cb)), pl.BlockSpec((c, qk_w), row(cb + 1)),
                  pl.BlockSpec((c, qk_w), row(cb + 2)), pl.BlockSpec((c, qk_w), row(cb + 3)),
                  pl.BlockSpec((c, qk_w), row(cb + 4)), pl.BlockSpec((c, qk_w), row(cb + 5)),
                  pl.BlockSpec((c, LANES), pos), pl.BlockSpec((c, LANES), pos), pl.BlockSpec((c, LANES), pos),
                  pl.BlockSpec((RET_HEADS, c, c), lambda b, n: (0, 0, 0)),
                  pl.BlockSpec((c, qk_w), fixed2), pl.BlockSpec((c, qk_w), fixed2),
                  pl.BlockSpec((1, v_w), fixed2), pl.BlockSpec((1, v_w), fixed2)],
        out_specs=pl.BlockSpec((c, v_w), lambda b, n: (b * n_chunk + n, 0)),
        out_shape=jax.ShapeDtypeStruct((bsz * seq, v_w), BF16),
        scratch_shapes=[pltpu.VMEM((RET_HEADS, RET_DK, RET_DV), F32)],
        compiler_params=_params("parallel", "arbitrary"),
        name="retention",
    )(h, h, h, h, h, h, rc, rsu, rsd, dec, qdec, kdec, ln_g.reshape(1, v_w), ln_b.reshape(1, v_w))


NSA_KV_W = NSA_KV_GROUPS * NSA_HEAD_DIM
NSA_VT_ROWS = NSA_HEAD_DIM + 16
NSA_GATE_COLS = 3 * NSA_HEADS


def _nsa_prep_kernel(q_ref, kvc_ref, kvs_ref, kvw_ref, c_ref, su_ref, sd_ref,
                     qo_ref, kc_ref, vc_ref, ks_ref, vs_ref, kw_ref, vw_ref):
    half = NSA_ROT_DIM // 2
    tabs = (c_ref[...], su_ref[...], sd_ref[...])
    scale = NSA_HEAD_DIM ** -0.5 * math.log2(math.e)
    q = jnp.concatenate(
        [_rope_apply(q_ref[:, s:s + LANES], *tabs, half) * scale for s in range(0, NSA_WIDTH, LANES)], axis=1)
    qo_ref[0] = q.T.astype(qo_ref.dtype)

    def split(x, o_ref):
        for g in range(NSA_KV_GROUPS):
            o_ref[0, g] = x[:, g * NSA_HEAD_DIM:(g + 1) * NSA_HEAD_DIM].astype(o_ref.dtype)

    def split_t(x, o_ref):
        xt = x.T
        for g in range(NSA_KV_GROUPS):
            o_ref[0, g, :NSA_HEAD_DIM] = xt[g * NSA_HEAD_DIM:(g + 1) * NSA_HEAD_DIM, :].astype(o_ref.dtype)
            o_ref[0, g, NSA_HEAD_DIM:] = jnp.ones((NSA_VT_ROWS - NSA_HEAD_DIM, xt.shape[1]), o_ref.dtype)

    split(kvc_ref[:, :NSA_KV_W], kc_ref)
    split(kvc_ref[:, NSA_KV_W:], vc_ref)
    n_hot = ks_ref.shape[3] - NSA_HEAD_DIM
    tl = q_ref.shape[0]
    blk = (pl.program_id(1) * tl + lax.broadcasted_iota(jnp.int32, (tl, n_hot), 0)) // SLC_BLOCK
    one_hot = jnp.where(blk % n_hot == lax.broadcasted_iota(jnp.int32, (tl, n_hot), 1), 1.0, 0.0)
    ks = _rope_apply(kvs_ref[:, :NSA_KV_W], *tabs, half)
    for g in range(NSA_KV_GROUPS):
        ks_ref[0, g] = jnp.concatenate([ks[:, g * NSA_HEAD_DIM:(g + 1) * NSA_HEAD_DIM], one_hot],
                                       axis=1).astype(ks_ref.dtype)
    split_t(kvs_ref[:, NSA_KV_W:], vs_ref)
    split(_rope_apply(kvw_ref[:, :NSA_KV_W], *tabs, half), kw_ref)
    split_t(kvw_ref[:, NSA_KV_W:], vw_ref)


def nsa_prep(h, bsz, seq, tl=512):
    tl = min(tl, seq)
    nl = seq // tl
    rc, rsu, rsd = _rope_tables(np.arange(seq), NSA_ROT_DIM, ROPE_THETA, NSA_HEAD_DIM, LANES // NSA_HEAD_DIM)
    row = lambda j: (lambda b, l: (b * nl + l, j))
    pos = lambda b, l: (l, 0)
    kv_out = pl.BlockSpec((1, NSA_KV_GROUPS, tl, NSA_HEAD_DIM), lambda b, l: (b, 0, l, 0))
    kv_shape = lambda dt: jax.ShapeDtypeStruct((bsz, NSA_KV_GROUPS, seq, NSA_HEAD_DIM), dt)
    vt_out = pl.BlockSpec((1, NSA_KV_GROUPS, NSA_VT_ROWS, tl), lambda b, l: (b, 0, 0, l))
    vt_shape = jax.ShapeDtypeStruct((bsz, NSA_KV_GROUPS, NSA_VT_ROWS, seq), BF16)
    ks_w = NSA_HEAD_DIM + min(NSA_KEY_TILE, seq) // SLC_BLOCK
    ks_out = pl.BlockSpec((1, NSA_KV_GROUPS, tl, ks_w), lambda b, l: (b, 0, l, 0))
    ks_shape = jax.ShapeDtypeStruct((bsz, NSA_KV_GROUPS, seq, ks_w), BF16)
    two = 2 * NSA_KV_W
    return pl.pallas_call(
        _nsa_prep_kernel,
        grid=(bsz, nl),
        in_specs=[pl.BlockSpec((tl, NSA_WIDTH), row(1)),
                  pl.BlockSpec((tl, two), row(4)), pl.BlockSpec((tl, two), row(5)), pl.BlockSpec((tl, two), row(6)),
                  pl.BlockSpec((tl, LANES), pos), pl.BlockSpec((tl, LANES), pos), pl.BlockSpec((tl, LANES), pos)],
        out_specs=[pl.BlockSpec((1, NSA_WIDTH, tl), lambda b, l: (b, 0, l)),
                   kv_out, kv_out, ks_out, vt_out, kv_out, vt_out],
        out_shape=[jax.ShapeDtypeStruct((bsz, NSA_WIDTH, seq), BF16),
                   kv_shape(F32), kv_shape(F32), ks_shape, vt_shape, kv_shape(BF16), vt_shape],
        compiler_params=_params("parallel", "parallel"),
        name="nsa_prep",
    )(h, h, h, h, rc, rsu, rsd)


def _nsa_compress_kernel(hk_ref, hv_ref, pek_ref, pev_ref, kw1_ref, kb1_ref, kw2_ref, vw1_ref, vb1_ref, vw2_ref,
                         c_ref, su_ref, sd_ref, ko_ref, vo_ref):
    def mlp(h_ref, pe_ref, w1_ref, b1_ref, w2_ref):
        hb = h_ref[0, 0]
        rows = hb.shape[0]
        first = _dot((hb + pe_ref[0:1, :]).astype(BF16), w1_ref[0])
        second = _dot((hb + pe_ref[1:2, :]).astype(BF16), w1_ref[1])
        hid = _gelu(first + pltpu.roll(second, rows - 1, 0) + b1_ref[...])
        return _dot(hid.astype(BF16), w2_ref[...])

    kc = _rope_apply(mlp(hk_ref, pek_ref, kw1_ref, kb1_ref, kw2_ref), c_ref[...], su_ref[...], sd_ref[...],
                     NSA_ROT_DIM // 2)
    vc = mlp(hv_ref, pev_ref, vw1_ref, vb1_ref, vw2_ref)
    ko_ref[0, 0] = kc[:, :NSA_HEAD_DIM].astype(ko_ref.dtype)
    vo_ref[0, 0, :NSA_HEAD_DIM] = vc.T[:NSA_HEAD_DIM, :].astype(vo_ref.dtype)
    vo_ref[0, 0, NSA_HEAD_DIM:] = jnp.ones((NSA_VT_ROWS - NSA_HEAD_DIM, vc.shape[0]), vo_ref.dtype)


def nsa_compress(kc, vc, pe_k, pe_v, ck_w1, ck_b1, ck_w2, cv_w1, cv_b1, cv_w2):
    bsz, grp, seq, d = kc.shape
    n_rows = seq // CMP_STRIDE
    flat = CMP_STRIDE * d
    cmp_end = np.arange(n_rows) * CMP_STRIDE + CMP_BLOCK - 1
    rc, rsu, rsd = _rope_tables(cmp_end, NSA_ROT_DIM, ROPE_THETA, NSA_HEAD_DIM, LANES // NSA_HEAD_DIM)
    pad_w2 = lambda w: jnp.pad(w, ((0, 0), (0, LANES - d))).astype(BF16)
    blk = pl.BlockSpec((1, 1, n_rows, flat), lambda b, g: (b, g, 0, 0))
    f2 = lambda b, g: (0, 0)
    f3 = lambda b, g: (0, 0, 0)
    w_specs = [pl.BlockSpec((2, flat, CMP_HIDDEN), f3), pl.BlockSpec((1, CMP_HIDDEN), f2),
               pl.BlockSpec((CMP_HIDDEN, LANES), f2)]
    return pl.pallas_call(
        _nsa_compress_kernel,
        grid=(bsz, grp),
        in_specs=[blk, blk, pl.BlockSpec((2, flat), f2), pl.BlockSpec((2, flat), f2)] + w_specs + w_specs
                 + [pl.BlockSpec((n_rows, LANES), f2)] * 3,
        out_specs=[pl.BlockSpec((1, 1, n_rows, d), lambda b, g: (b, g, 0, 0)),
                   pl.BlockSpec((1, 1, NSA_VT_ROWS, n_rows), lambda b, g: (b, g, 0, 0))],
        out_shape=[jax.ShapeDtypeStruct((bsz, grp, n_rows, d), BF16),
                   jax.ShapeDtypeStruct((bsz, grp, NSA_VT_ROWS, n_rows), BF16)],
        compiler_params=_params("parallel", "parallel"),
        name="nsa_compress",
    )(kc.reshape(bsz, grp, n_rows, flat), vc.reshape(bsz, grp, n_rows, flat),
      pe_k.reshape(2, flat), pe_v.reshape(2, flat),
      ck_w1.reshape(2, flat, CMP_HIDDEN).astype(BF16), ck_b1.reshape(1, CMP_HIDDEN), pad_w2(ck_w2),
      cv_w1.reshape(2, flat, CMP_HIDDEN).astype(BF16), cv_b1.reshape(1, CMP_HIDDEN), pad_w2(cv_w2),
      rc, rsu, rsd)


def _per_head(x):
    return jnp.concatenate([x] * NSA_HPG, axis=1)


def _nsa_attn_kernel(seq, tk, qt_ref, gate_ref, kc_ref, vct_ref, ks_ref, vst_ref, kw_ref, vwt_ref, mmapt_ref,
                     o_ref, sel_ref):
    n_blk = seq // SLC_BLOCK
    n_sel = min(N_SLC, n_blk)
    hd = NSA_HEAD_DIM
    w = NSA_HPG * hd
    groups = range(NSA_KV_GROUPS)
    q0 = pl.program_id(1) * Q_BLOCK
    t_l = q0 + lax.broadcasted_iota(jnp.int32, (1, Q_BLOCK), 1)

    def select(g):
        qg = qt_ref[0, g * w:(g + 1) * w, :]
        qst = jnp.concatenate([qg[h * hd:(h + 1) * hd, :] for h in range(NSA_HPG)], axis=1)
        kc = kc_ref[0, g]
        n_cmp = kc.shape[0]
        cmp_end = lax.broadcasted_iota(jnp.int32, (n_cmp, 1), 0) * CMP_STRIDE + (CMP_BLOCK - 1)
        s = _dot(kc, qst) + _per_head(jnp.where(cmp_end <= t_l, 0.0, MASK_VALUE))
        yield
        p = jnp.exp2(s - jnp.max(s, axis=0, keepdims=True))
        any_key = _per_head(jnp.where(t_l >= CMP_BLOCK - 1, 1.0, 0.0))
        pv = _dot(vct_ref[0, g], p.astype(BF16))
        inv_l = any_key / pv[hd:hd + 1]
        o_cmp = pv[:hd] * inv_l
        yield
        p = p * inv_l
        imp = p[:, 0:Q_BLOCK]
        for h in range(1, NSA_HPG):
            imp = imp + p[:, h * Q_BLOCK:(h + 1) * Q_BLOCK]
        imp_slc = _dot_split(mmapt_ref[...], imp, 'b', 3)
        yield
        blk = lax.broadcasted_iota(jnp.int32, (n_blk, 1), 0)
        cur = t_l // SLC_BLOCK
        score = jnp.where(blk == 0, FORCE_SCORE,
                          jnp.where(blk == cur, FORCE_SCORE, jnp.where(blk == cur - 1, FORCE_SCORE, imp_slc)))
        score = jnp.where(blk * SLC_BLOCK <= t_l, score, -FORCE_SCORE)
        sel = jnp.zeros((n_blk, Q_BLOCK), F32)
        for _ in range(n_sel):
            best = jnp.max(score, axis=0, keepdims=True)
            idx = jnp.min(jnp.where(score == best, blk, n_blk), axis=0, keepdims=True)
            pick = blk == idx
            sel = jnp.where(pick, 1.0, sel)
            score = jnp.where(pick, -jnp.inf, score)
            yield
        sel_ref[g] = sel
        return qst, o_cmp

    selected = _run_interleaved([select(g) for g in groups])
    qst = [r[0] for r in selected]
    o_cmp = [r[1] for r in selected]

    blocks_per_tile = tk // SLC_BLOCK
    assert ks_ref.shape[3] == hd + blocks_per_tile and Q_BLOCK <= tk and tk % Q_BLOCK == 0

    def slc_tile(kt, carry, causal_bias=None):
        k0 = pl.multiple_of(kt * tk, tk)
        out = []
        for g in groups:
            m, acc = carry[g]
            sel_rows = sel_ref[g, pl.ds(pl.multiple_of(kt * blocks_per_tile, blocks_per_tile), blocks_per_tile), :]
            q_aug = jnp.concatenate([qst[g], _per_head((sel_rows - 1.0) * -MASK_VALUE).astype(BF16)], axis=0)
            s = _dot(ks_ref[0, g, pl.ds(k0, tk), :], q_aug)
            if causal_bias is not None:
                s = s + causal_bias
            m_new = jnp.maximum(m, jnp.max(s, axis=0, keepdims=True))
            alpha = jnp.exp2(m - m_new)
            p = jnp.exp2(s - m_new)
            acc = alpha * acc + _dot(vst_ref[0, g, :, pl.ds(k0, tk)], p.astype(BF16))
            out.append((m_new, acc))
        return tuple(out)

    n_full = q0 // tk
    cols = NSA_HPG * Q_BLOCK
    init = tuple((jnp.full((1, cols), MASK_VALUE, F32), jnp.zeros((NSA_VT_ROWS, cols), F32)) for _ in groups)
    slc = lax.fori_loop(0, n_full, slc_tile, init)
    kpos = n_full * tk + lax.broadcasted_iota(jnp.int32, (tk, 1), 0)
    slc = slc_tile(n_full, slc, _per_head(jnp.where(kpos <= t_l, 0.0, MASK_VALUE)))

    band = WINDOW + Q_BLOCK
    w0 = pl.multiple_of(jnp.maximum(q0 - WINDOW, 0), Q_BLOCK)
    kpos = w0 + lax.broadcasted_iota(jnp.int32, (band, 1), 0)
    win_bias = _per_head(jnp.where(kpos <= t_l, jnp.where(kpos > t_l - WINDOW, 0.0, MASK_VALUE), MASK_VALUE))
    sig_t = _sigmoid(gate_ref[...]).T

    def finish(g):
        s = _dot(kw_ref[0, g, pl.ds(w0, band), :], qst[g]) + win_bias
        yield
        p = jnp.exp2(s - jnp.max(s, axis=0, keepdims=True))
        pv = _dot(vwt_ref[0, g, :, pl.ds(w0, band)], p.astype(BF16))
        o_win = pv[:hd] / pv[hd:hd + 1]
        yield
        acc = slc[g][1]
        acc_slc, l_slc = acc[:hd], acc[hd:hd + 1]

        def gate(branch):
            first = (g * NSA_HPG) * 3 + branch
            return jnp.concatenate([sig_t[first + 3 * h:first + 3 * h + 1, :] for h in range(NSA_HPG)], axis=1)

        out_t = gate(0) * o_cmp[g] + gate(1) * (acc_slc / l_slc) + gate(2) * o_win
        pairs = []
        for h in range(0, NSA_HPG, 2):
            two = jnp.concatenate([out_t[:, h * Q_BLOCK:(h + 1) * Q_BLOCK],
                                   out_t[:, (h + 1) * Q_BLOCK:(h + 2) * Q_BLOCK]], axis=0)
            pairs.append(two.T)
        o_ref[:, g * w:(g + 1) * w] = jnp.concatenate(pairs, axis=1).astype(o_ref.dtype)

    _run_interleaved([finish(g) for g in groups])


def _nsa_pool_matrix(seq):
    n_blk = seq // SLC_BLOCK
    n_rows = seq // CMP_STRIDE
    per_stride = SLC_BLOCK // CMP_STRIDE
    span = CMP_BLOCK // CMP_STRIDE
    pool = np.zeros((n_blk, n_rows), np.float32)
    for j in range(n_blk):
        for m in range(per_stride):
            for n in range(span):
                c = per_stride * j + m + n - (span - 1)
                if 0 <= c < n_rows - 1:
                    pool[j, c] += 1.0
    return jnp.asarray(pool, BF16)


def nsa_attention(qt, h, gate_col_block, k_cmp, v_cmp_t, ks, vs_t, kw, vw_t, bsz, seq):
    tk = min(NSA_KEY_TILE, seq)
    nq = seq // Q_BLOCK
    pool = _nsa_pool_matrix(seq)
    n_rows = k_cmp.shape[2]
    d = NSA_HEAD_DIM
    qblk = lambda b, i: (b * nq + i, 0)
    whole = lambda *shape: pl.BlockSpec((1, NSA_KV_GROUPS) + shape, lambda b, i: (b, 0, 0, 0))
    kern = functools.partial(_nsa_attn_kernel, seq, tk)
    return pl.pallas_call(
        kern,
        grid=(bsz, nq),
        in_specs=[pl.BlockSpec((1, NSA_WIDTH, Q_BLOCK), lambda b, i: (b, 0, i)),
                  pl.BlockSpec((Q_BLOCK, LANES), lambda b, i: (b * nq + i, gate_col_block)),
                  whole(n_rows, d), whole(NSA_VT_ROWS, n_rows), whole(seq, ks.shape[3]), whole(NSA_VT_ROWS, seq),
                  whole(seq, d), whole(NSA_VT_ROWS, seq),
                  pl.BlockSpec(pool.shape, lambda b, i: (0, 0))],
        out_specs=pl.BlockSpec((Q_BLOCK, NSA_WIDTH), qblk),
        out_shape=jax.ShapeDtypeStruct((bsz * seq, NSA_WIDTH), BF16),
        scratch_shapes=[pltpu.VMEM((NSA_KV_GROUPS, seq // SLC_BLOCK, Q_BLOCK), F32)],
        compiler_params=_params("parallel", "arbitrary"),
        name="nsa_attention",
    )(qt, h, k_cmp, v_cmp_t, ks, vs_t, kw, vw_t, pool)


def nsa_mixer(h, bsz, seq, gate_col_block, pe_k, pe_v, ck_w1, ck_b1, ck_w2, cv_w1, cv_b1, cv_w2):
    qt, kc, vc, ks, vs_t, kw, vw_t = nsa_prep(h, bsz, seq)
    k_cmp, v_cmp_t = nsa_compress(kc, vc, pe_k, pe_v, ck_w1, ck_b1, ck_w2, cv_w1, cv_b1, cv_w2)
    return nsa_attention(qt, h, gate_col_block, k_cmp, v_cmp_t, ks, vs_t, kw, vw_t, bsz, seq)


def _head_ones(width, head_dim):
    idx = np.arange(width) // head_dim
    return jnp.asarray(idx[:, None] == idx[None, :], BF16)


def _softplus(x):
    return jnp.maximum(x, 0.0) + jnp.log(1.0 + jnp.exp(-jnp.abs(x)))


def _rwkv_pre_kernel(p_ref, prev_ref, mu_ref, w0_ref, wup_ref, a0_ref, aup_ref, gup_ref, kk_ref, ka_ref, rk_ref,
                     ones_ref, r_o, k_o, v_o, kk_o, b_o, ld_o, g_o, bonus_o):
    w = RWKV_WIDTH
    p = p_ref[...]
    first_row = jnp.where(pl.program_id(1) == 0, 0.0, prev_ref[7:8, :])
    is_row0 = lax.broadcasted_iota(jnp.int32, p.shape, 0) == 0
    prev = jnp.where(is_row0, first_row, pltpu.roll(p, 1, 0))
    ps = p + (prev - p) * mu_ref[...]
    r, k, v = ps[:, 0:w], ps[:, w:2 * w], ps[:, 2 * w:3 * w]
    o = 3 * w
    w_lo = ps[:, o:o + RWKV_LORA_W]
    a_lo = ps[:, o + RWKV_LORA_W:o + RWKV_LORA_W + RWKV_LORA_A]
    g_lo = ps[:, o + RWKV_LORA_W + RWKV_LORA_A:]
    wlog = -_softplus(-(w0_ref[...] + _dot(jnp.tanh(w_lo).astype(BF16), wup_ref[...]))) - 0.5
    a = _sigmoid(a0_ref[...] + _dot(a_lo.astype(BF16), aup_ref[...]))
    g = _dot(_sigmoid(g_lo).astype(BF16), gup_ref[...])
    kk = k * kk_ref[...]
    norm = jnp.sqrt(_dot_split(kk * kk, ones_ref[...], 'a', 2))
    kk = kk / jnp.maximum(norm, 1e-12)
    k2 = k * (1.0 + (a - 1.0) * ka_ref[...])
    r_o[...] = r
    k_o[...] = k2
    v_o[...] = v
    kk_o[...] = kk
    b_o[...] = kk * a
    ld_o[...] = -jnp.exp(wlog)
    g_o[...] = g
    bonus_o[...] = _dot_split(r * k2 * rk_ref[...], ones_ref[...], 'a', 2) * v


def rwkv_pre(h, bsz, seq, mu, w0, w_up, a0, a_up, g_up, k_k, k_a, r_k, tl=512):
    tl = min(tl, seq)
    nl = seq // tl
    w = RWKV_WIDTH
    cols = RWKV_COLS
    ones = _head_ones(w, RWKV_HEAD_DIM)
    f2 = lambda b, l: (0, 0)
    vec = pl.BlockSpec((1, w), f2)
    out_spec = pl.BlockSpec((tl, w), lambda b, l: (b * nl + l, 0))
    out_shape = jax.ShapeDtypeStruct((bsz * seq, w), F32)
    return pl.pallas_call(
        _rwkv_pre_kernel,
        grid=(bsz, nl),
        in_specs=[pl.BlockSpec((tl, cols), lambda b, l: (b * nl + l, 0)),
                  pl.BlockSpec((8, cols), lambda b, l: (jnp.maximum((b * seq + l * tl) // 8 - 1, 0), 0)),
                  pl.BlockSpec((1, cols), f2), vec, pl.BlockSpec((RWKV_LORA_W, w), f2),
                  vec, pl.BlockSpec((RWKV_LORA_A, w), f2), pl.BlockSpec((RWKV_LORA_G, w), f2),
                  vec, vec, vec, pl.BlockSpec((w, w), f2)],
        out_specs=[out_spec] * 8,
        out_shape=[out_shape] * 8,
        compiler_params=_params("parallel", "parallel"),
        name="rwkv_pre",
    )(h, h, mu.reshape(1, cols), w0.reshape(1, w), w_up.astype(BF16), a0.reshape(1, w), a_up.astype(BF16),
      g_up.astype(BF16), k_k.reshape(1, w), k_a.reshape(1, w), r_k.reshape(1, w), ones)


def _rwkv_masks():
    t, pk = RWKV_CHUNK, RWKV_PACK
    n = t * pk
    ri = np.arange(n)
    same = (ri[:, None] // t) == (ri[None, :] // t)
    tt, ss = ri[:, None] % t, ri[None, :] % t
    levels = []
    k = 1
    while k < t:
        levels.append(same & (tt // (2 * k) == ss // (2 * k)) & ((tt // k) % 2 == 1) & ((ss // k) % 2 == 0))
        k *= 2
    lvl = np.stack(levels).astype(np.float32)
    tri = (np.arange(t)[:, None] >= np.arange(t)[None, :]).astype(np.float32)
    head_lane = ((ri[:, None] // t) == (np.arange(pk * RWKV_HEAD_DIM)[None, :] // RWKV_HEAD_DIM)).astype(np.float32)
    return (jnp.asarray(tri, BF16), jnp.asarray(head_lane), jnp.asarray(same.astype(np.float32)), jnp.asarray(lvl))


def _rwkv_chain(r, k, v, kk, b, ld, st, tri, hl, bd, lvl_ref):
    t, pk = RWKV_CHUNK, RWKV_PACK
    n = t * pk
    c = _dot_split(tri, ld, 'b', 3)
    yield
    c_end = c[t - 1:t, :]
    e_neg = jnp.exp(-c)
    e_end = jnp.exp(c_end - c)
    kkd = (kk * jnp.exp(c - ld)).astype(BF16)
    rd = (r * jnp.exp(c)).astype(BF16)

    def big(x):
        return (jnp.concatenate([x] * pk, axis=0) * hl).astype(BF16)

    st_b = st.astype(BF16)
    v_big = big(v)
    a_all = _dot_nt(jnp.concatenate([kkd, rd], axis=0),
                    jnp.concatenate([big(k * e_neg), big(b * e_neg)], axis=0))
    yield
    ti = lax.broadcasted_iota(jnp.int32, (t, n), 0)
    si = lax.broadcasted_iota(jnp.int32, (t, n), 1) % t
    strict = ti > si
    incl = ti >= si
    a_kk = jnp.where(strict, a_all[:t, :n], 0.0)
    a_kb = jnp.where(strict, a_all[:t, n:], 0.0)
    a_rk = jnp.where(incl, a_all[t:, :n], 0.0)
    a_rb = jnp.where(incl, a_all[t:, n:], 0.0)
    rhs = _dot(kkd, st_b) + _dot(a_kk.astype(BF16), v_big)
    yield
    a_bd = jnp.concatenate([a_kb] * pk, axis=0) * bd
    m = jnp.where(lax.broadcasted_iota(jnp.int32, (n, n), 0) == lax.broadcasted_iota(jnp.int32, (n, n), 1), 1.0, 0.0)
    for lv in range(lvl_ref.shape[0]):
        mb = m.astype(BF16)
        ma = _dot(mb, (a_bd * lvl_ref[lv]).astype(BF16)).astype(BF16)
        yield
        m = m - _dot(ma, mb)
        yield
    u_big = _dot(m.astype(BF16), big(rhs))
    yield
    u = u_big[0:t]
    for h in range(1, pk):
        u = u + u_big[h * t:(h + 1) * t]
    y = _dot(rd, st_b) + _dot(a_rk.astype(BF16), v_big) - _dot(a_rb.astype(BF16), big(u))
    yield
    decay_col = jnp.broadcast_to(jnp.exp(c_end), st.shape).T
    kb_end = jnp.concatenate([k * e_end, -(b * e_end)], axis=0).astype(BF16)
    vu = jnp.concatenate([v, u], axis=0).astype(BF16)
    return y, decay_col * st + bd * _dot_tn(kb_end, vu)


def _rwkv_chunk_kernel(r_ref, k_ref, v_ref, kk_ref, b_ref, ld_ref, tri_ref, hl_ref, bd_ref, lvl_ref, y_ref, st_ref):
    @pl.when(pl.program_id(0) == 0)
    def _():
        st_ref[...] = jnp.zeros_like(st_ref)

    wp = RWKV_PACK * RWKV_HEAD_DIM
    tri, hl, bd = tri_ref[...], hl_ref[...], bd_ref[...]
    n_pack = r_ref.shape[2] // wp
    where = [(bi, slice(g * wp, (g + 1) * wp)) for bi in range(r_ref.shape[0]) for g in range(n_pack)]
    loaded = [tuple(ref[bi, :, cols] for ref in (r_ref, k_ref, v_ref, kk_ref, b_ref, ld_ref)) + (st_ref[i],)
              for i, (bi, cols) in enumerate(where)]
    results = _run_interleaved([_rwkv_chain(*args, tri, hl, bd, lvl_ref) for args in loaded])
    for i, ((bi, cols), (y, st_new)) in enumerate(zip(where, results)):
        y_ref[bi, :, cols] = y
        st_ref[i] = st_new


def rwkv_chunk(r, k, v, kk, b, ld, bsz, seq):
    t, pk = RWKV_CHUNK, RWKV_PACK
    n_chunk = seq // t
    w = RWKV_WIDTH
    wp = pk * RWKV_HEAD_DIM
    assert t == RWKV_HEAD_DIM
    tri, hl, bd, lvl = _rwkv_masks()
    blk = pl.BlockSpec((bsz, t, w), lambda c: (0, c, 0))
    f2 = lambda c: (0, 0)
    shaped = lambda a: a.reshape(bsz, seq, w)
    y = pl.pallas_call(
        _rwkv_chunk_kernel,
        grid=(n_chunk,),
        in_specs=[blk] * 6 + [pl.BlockSpec(tri.shape, f2), pl.BlockSpec(hl.shape, f2), pl.BlockSpec(bd.shape, f2),
                              pl.BlockSpec(lvl.shape, lambda c: (0, 0, 0))],
        out_specs=blk,
        out_shape=jax.ShapeDtypeStruct((bsz, seq, w), F32),
        scratch_shapes=[pltpu.VMEM((bsz * (w // wp), wp, wp), F32)],
        compiler_params=_params("arbitrary"),
        name="rwkv_chunk",
    )(shaped(r), shaped(k), shaped(v), shaped(kk), shaped(b), shaped(ld), tri, hl, bd, lvl)
    return y.reshape(bsz * seq, w)


def _rwkv_post_kernel(y_ref, bonus_ref, g_ref, lng_ref, lnb_ref, ones_ref, o_ref):
    y = y_ref[...]
    inv = 1.0 / RWKV_HEAD_DIM
    mu = _dot_split(y, ones_ref[...], 'a', 2) * inv
    yc = y - mu
    var = _dot_split(yc * yc, ones_ref[...], 'a', 2) * inv
    yn = yc * lax.rsqrt(var + RWKV_GN_EPS) * lng_ref[...] + lnb_ref[...]
    o_ref[...] = ((yn + bonus_ref[...]) * g_ref[...]).astype(o_ref.dtype)


def rwkv_post(y, bonus, g, ln_g, ln_b, tm=1024):
    n, w = y.shape
    tm = min(tm, n)
    row = pl.BlockSpec((tm, w), lambda i: (i, 0))
    vec = pl.BlockSpec((1, w), lambda i: (0, 0))
    return pl.pallas_call(
        _rwkv_post_kernel,
        grid=(n // tm,),
        in_specs=[row, row, row, vec, vec, pl.BlockSpec((w, w), lambda i: (0, 0))],
        out_specs=row,
        out_shape=jax.ShapeDtypeStruct((n, w), BF16),
        compiler_params=_params("parallel"),
        name="rwkv_post",
    )(y, bonus, g, ln_g.reshape(1, w), ln_b.reshape(1, w), _head_ones(w, RWKV_HEAD_DIM))


def rwkv7_mixer(h, bsz, seq, mu, w0, w_up, a0, a_up, g_up, k_k, k_a, r_k, ln_g, ln_b):
    r, k, v, kk, b, ld, g, bonus = rwkv_pre(h, bsz, seq, mu, w0, w_up, a0, a_up, g_up, k_k, k_a, r_k)
    y = rwkv_chunk(r, k, v, kk, b, ld, bsz, seq)
    return rwkv_post(y, bonus, g, ln_g, ln_b)


AB_IN = S5_WIDTH + NSA_WIDTH + 6 * NSA_KV_W + NSA_GATE_COLS
AB_IN_PADDED = -(-AB_IN // LANES) * LANES
NSA_GATE_COL_BLOCK = (AB_IN - NSA_GATE_COLS) // LANES
PROJ_TM = 512


def kernel(x, ab_w_in, ab_w_out, s5_lam_re, s5_lam_im, s5_log_dt, s5_b_re, s5_b_im, s5_c_re, s5_c_im, s5_d, s5_w_glu, s5_b_glu, nsa_pe_k, nsa_pe_v, nsa_ck_w1, nsa_ck_b1, nsa_ck_w2, nsa_cv_w1, nsa_cv_b1, nsa_cv_w2, cd_w_in, cd_w_out, rwkv_mu, rwkv_w0, rwkv_w_up, rwkv_a0, rwkv_a_up, rwkv_g_up, rwkv_k_k, rwkv_k_a, rwkv_r_k, rwkv_ln_g, rwkv_ln_b, ret_ln_g, ret_ln_b, ln1_g, ln1_b, ln2_g, ln2_b, moe_router, moe_bias, moe_w1, moe_w3, moe_w2, sh_w1, sh_w3, sh_w2):
    bsz, seq, d = x.shape
    assert (AB_IN - NSA_GATE_COLS) % LANES == 0
    xf = x.reshape(bsz * seq, d)
    x_in = xf
    for layer in range(DEPTH):
        i = layer // 2
        if layer % 2 == 0:
            w_in = jnp.pad(ab_w_in[i], ((0, 0), (0, AB_IN_PADDED - AB_IN))).astype(BF16)
            h, u3 = project(x_in, w_in, PROJ_TM, chunked=(S5_CHUNK, S5_WIDTH))
            y_1 = s5_mixer(h, u3, bsz, seq, s5_lam_re[i], s5_lam_im[i], s5_log_dt[i], s5_b_re[i], s5_b_im[i],
                           s5_c_re[i], s5_c_im[i], s5_d[i], s5_w_glu[i], s5_b_glu[i])
            y_2 = nsa_mixer(h, bsz, seq, NSA_GATE_COL_BLOCK, nsa_pe_k[i], nsa_pe_v[i], nsa_ck_w1[i], nsa_ck_b1[i],
                            nsa_ck_w2[i], nsa_cv_w1[i], nsa_cv_b1[i], nsa_cv_w2[i])
            w_out = ab_w_out[i]
        else:
            h = project(x_in, cd_w_in[i].astype(BF16), PROJ_TM)
            y_1 = rwkv7_mixer(h, bsz, seq, rwkv_mu[i], rwkv_w0[i], rwkv_w_up[i], rwkv_a0[i], rwkv_a_up[i],
                              rwkv_g_up[i], rwkv_k_k[i], rwkv_k_a[i], rwkv_r_k[i], rwkv_ln_g[i], rwkv_ln_b[i])
            y_2 = retention_mixer(h, bsz, seq, RWKV_COLS, ret_ln_g[i], ret_ln_b[i])
            w_out = cd_w_out[i]
        xf, gates = out_proj_ln_route(y_1, y_2, w_out, xf, ln1_g[layer], ln1_b[layer],
                                      moe_router[layer], moe_bias[layer])
        xf, x_in = moe_experts_ln(xf, gates, moe_w1, moe_w3, moe_w2, layer,
                                  sh_w1, sh_w3, sh_w2[layer], ln2_g[layer], ln2_b[layer])
    return xf.reshape(bsz, seq, d)
```

```python
import functools
import math

import jax
import jax.numpy as jnp
import numpy as np
from jax import lax
from jax.experimental import pallas as pl
from jax.experimental.pallas import tpu as pltpu

F32 = jnp.float32
BF16 = jnp.bfloat16
FP8 = jnp.float8_e4m3fn
FP8_MAX = 448.0
FP8_TINY = 1e-30

VMEM_LIMIT_BYTES = 52 * 1024 * 1024
LANES = 128

LN_EPS = 1e-5
DEPTH = 2
ALPHA = (2 * DEPTH) ** 0.25

S5_GROUPS, S5_GROUP_CH, S5_STATE = 32, 16, 64
S5_WIDTH = S5_GROUPS * S5_GROUP_CH
S5_CHUNK = 16
S5_PACK = 8
NSA_HEADS, NSA_KV_GROUPS, NSA_HEAD_DIM = 8, 2, 64
NSA_HPG = NSA_HEADS // NSA_KV_GROUPS
NSA_WIDTH = NSA_HEADS * NSA_HEAD_DIM
NSA_ROT_DIM = NSA_HEAD_DIM // 4
ROPE_THETA = 500000.0
CMP_BLOCK, CMP_STRIDE, CMP_HIDDEN = 32, 16, 128
SLC_BLOCK, N_SLC, WINDOW = 64, 16, 512
Q_BLOCK = 256
NSA_KEY_TILE = 1024
FORCE_SCORE = 1e6
MASK_VALUE = -1e30
RWKV_HEADS, RWKV_HEAD_DIM = 8, 64
RWKV_WIDTH = RWKV_HEADS * RWKV_HEAD_DIM
RWKV_LORA_W, RWKV_LORA_A, RWKV_LORA_G = 64, 64, 128
RWKV_COLS = 3 * RWKV_WIDTH + RWKV_LORA_W + RWKV_LORA_A + RWKV_LORA_G
RWKV_GN_EPS = 64e-5
RWKV_CHUNK = 64
RWKV_PACK = 4
RET_HEADS, RET_DK, RET_DV, RET_CHUNK = 4, 64, 128, 128
RET_THETA = 10000.0
RET_GN_EPS = 1e-5
N_EXPERTS, TOP_K, EXPERT_FF = 64, 8, 256
N_EXPERT_GROUPS, TOPK_GROUPS = 8, 4
EXPERTS_PER_GROUP = N_EXPERTS // N_EXPERT_GROUPS
ROUTED_SCALE = 2.5
MOE_EXPERTS_PER_STEP = 4


def _params(*sem):
    return pltpu.CompilerParams(dimension_semantics=sem, vmem_limit_bytes=VMEM_LIMIT_BYTES)


def _dot(a, b, **kw):
    return jnp.dot(a, b, preferred_element_type=F32, **kw)


def _dot_nt(a, b, **kw):
    return lax.dot_general(a, b, (((1,), (1,)), ((), ())), preferred_element_type=F32, **kw)


def _dot_tn(a, b, **kw):
    return lax.dot_general(a, b, (((0,), (0,)), ((), ())), preferred_element_type=F32, **kw)


def _dot_split(a, b, split, parts):
    rest = a if split == 'a' else b
    acc = None
    for _ in range(parts):
        piece = rest.astype(BF16)
        term = _dot(piece, b) if split == 'a' else _dot(a, piece)
        acc = term if acc is None else acc + term
        rest = rest - piece.astype(F32)
    return acc


def _run_interleaved(gens):
    results = [None] * len(gens)
    live = list(range(len(gens)))
    while live:
        for i in list(live):
            try:
                next(gens[i])
            except StopIteration as done:
                results[i] = done.value
                live.remove(i)
    return results


def _gelu(x):
    return 0.5 * x * (1.0 + jnp.tanh(math.sqrt(2.0 / math.pi) * (x + 0.044715 * (x * x * x))))


def _sigmoid(x):
    return 1.0 / (1.0 + jnp.exp(-x))


def _layer_norm_rows(z, g, b):
    mu = jnp.mean(z, axis=-1, keepdims=True)
    zc = z - mu
    var = jnp.mean(zc * zc, axis=-1, keepdims=True)
    return zc * lax.rsqrt(var + LN_EPS) * g + b


def _proj_kernel(x_ref, w_ref, o_ref, *chunked_ref):
    y = _dot(x_ref[...].astype(BF16), w_ref[...])
    o_ref[...] = y
    for c_ref in chunked_ref:
        rows, t, w = c_ref.shape
        c_ref[...] = y[:, :w].reshape(rows, t, w)


def project(x, w_bf16, tm, chunked=None):
    m, k = x.shape
    n = w_bf16.shape[1]
    out_specs = [pl.BlockSpec((tm, n), lambda i: (i, 0))]
    out_shape = [jax.ShapeDtypeStruct((m, n), F32)]
    if chunked is not None:
        t, w = chunked
        out_specs.append(pl.BlockSpec((tm // t, t, w), lambda i: (i, 0, 0)))
        out_shape.append(jax.ShapeDtypeStruct((m // t, t, w), F32))
    out = pl.pallas_call(
        _proj_kernel,
        grid=(m // tm,),
        in_specs=[pl.BlockSpec((tm, k), lambda i: (i, 0)), pl.BlockSpec((k, n), lambda i: (0, 0))],
        out_specs=out_specs,
        out_shape=out_shape,
        compiler_params=_params("parallel"),
        name="project",
    )(x, w_bf16)
    return out if chunked is not None else out[0]


def _out_proj_ln_kernel(ya_ref, yb_ref, wa_ref, wb_ref, x_ref, g_ref, b_ref, rt_ref, rbias_ref, o_ref, gate_ref):
    mix = _dot(ya_ref[...], wa_ref[...]) + _dot(yb_ref[...], wb_ref[...])
    y = _layer_norm_rows(ALPHA * x_ref[...] + mix, g_ref[...], b_ref[...])
    o_ref[...] = y
    gate_ref[...] = _route(y, rt_ref[...], rbias_ref[...])


def out_proj_ln_route(ya, yb, w_out, x, g, b, router, router_bias, tm=512):
    n, d = x.shape
    ka, kb = ya.shape[1], yb.shape[1]
    wa = w_out[:ka].astype(BF16)
    wb = w_out[ka:].astype(BF16)
    row = lambda i: (i, 0)
    fixed = lambda i: (0, 0)
    return pl.pallas_call(
        _out_proj_ln_kernel,
        grid=(n // tm,),
        in_specs=[pl.BlockSpec((tm, ka), row), pl.BlockSpec((tm, kb), row),
                  pl.BlockSpec((ka, d), fixed), pl.BlockSpec((kb, d), fixed),
                  pl.BlockSpec((tm, d), row), pl.BlockSpec((1, d), fixed), pl.BlockSpec((1, d), fixed),
                  pl.BlockSpec((N_EXPERTS, d), fixed), pl.BlockSpec((N_EXPERTS, 1), fixed)],
        out_specs=[pl.BlockSpec((tm, d), row), pl.BlockSpec((tm, LANES), row)],
        out_shape=[jax.ShapeDtypeStruct((n, d), F32), jax.ShapeDtypeStruct((n, LANES), F32)],
        compiler_params=_params("parallel"),
        name="out_proj_ln_route",
    )(ya, yb, wa, wb, x, g.reshape(1, d), b.reshape(1, d), router.T, router_bias.reshape(N_EXPERTS, 1))


def _route(x, rt, bias):
    tr = x.shape[0]
    x_hi, rt_hi = x.astype(BF16), rt.astype(BF16)
    x_lo, rt_lo = (x - x_hi.astype(F32)).astype(BF16), (rt - rt_hi.astype(F32)).astype(BF16)
    scores = _sigmoid(_dot_nt(rt_hi, x_hi) + _dot_nt(rt_hi, x_lo) + _dot_nt(rt_lo, x_hi))
    biased = scores + bias
    grp = biased.reshape(N_EXPERT_GROUPS, EXPERTS_PER_GROUP, tr)
    pos = lax.broadcasted_iota(jnp.int32, grp.shape, 1)
    m1 = jnp.max(grp, axis=1, keepdims=True)
    first = jnp.min(jnp.where(grp == m1, pos, EXPERTS_PER_GROUP), axis=1, keepdims=True)
    m2 = jnp.max(jnp.where(pos == first, -jnp.inf, grp), axis=1, keepdims=True)
    gscore = (m1 + m2).reshape(N_EXPERT_GROUPS, tr)
    gidx = lax.broadcasted_iota(jnp.int32, gscore.shape, 0)
    grank = jnp.zeros(gscore.shape, F32)
    for j in range(N_EXPERT_GROUPS):
        row = gscore[j:j + 1, :]
        grank = grank + jnp.where(gidx > j, jnp.where(row >= gscore, 1.0, 0.0), jnp.where(row > gscore, 1.0, 0.0))
    gkeep = jnp.where(grank < TOPK_GROUPS, 1.0, 0.0)
    keep = jnp.broadcast_to(gkeep[:, None, :], grp.shape).reshape(N_EXPERTS, tr)
    masked = jnp.where(keep > 0.5, biased, -jnp.inf)
    eidx = lax.broadcasted_iota(jnp.int32, masked.shape, 0)
    rank = jnp.zeros(masked.shape, F32)
    for j in range(N_EXPERTS):
        row = masked[j:j + 1, :]
        rank = rank + jnp.where(eidx > j, jnp.where(row >= masked, 1.0, 0.0), jnp.where(row > masked, 1.0, 0.0))
    gate = jnp.where(rank < TOP_K, scores, 0.0)
    gate = gate / jnp.sum(gate, axis=0, keepdims=True) * ROUTED_SCALE
    return jnp.concatenate([gate, jnp.zeros((LANES - N_EXPERTS, tr), F32)], axis=0).T


def _quantize_fp8(a, axes):
    amax = jnp.max(jnp.abs(a), axis=axes, keepdims=True)
    scale = jnp.maximum(amax, FP8_TINY) * (1.0 / FP8_MAX)
    return (a * (1.0 / scale)).astype(FP8), scale


def _swiglu_hidden(xq, x_scale, w1q, w3q, w_scale, gate=None):
    col1 = x_scale * w_scale[0:1, 0:1]
    col3 = x_scale * w_scale[1:2, 0:1]
    if gate is not None:
        col3 = col3 * gate
    h1 = _dot(xq, w1q) * col1
    return h1 * _sigmoid(h1) * (_dot(xq, w3q) * col3)


def _experts_ln_kernel(x_ref, gate_ref, w1_ref, w3_ref, ws_ref, w2_ref, sw1_ref, sw3_ref, sws_ref, sw2_ref,
                       g_ref, b_ref, o_ref, obf_ref, acc_ref, xq_ref, xs_ref):
    step = pl.program_id(1)
    per_step = w1_ref.shape[0]

    @pl.when(step == 0)
    def _():
        xq, xs = _quantize_fp8(x_ref[...], (1,))
        xq_ref[...] = xq
        xs_ref[...] = xs
        h = _swiglu_hidden(xq, xs, sw1_ref[...], sw3_ref[...], sws_ref[...])
        acc_ref[...] = _dot(h.astype(BF16), sw2_ref[...].astype(BF16))

    lane = lax.broadcasted_iota(jnp.int32, gate_ref.shape, 1)
    gates = gate_ref[...]
    xq, xs = xq_ref[...], xs_ref[...]
    hidden = []
    for j in range(per_step):
        gcol = jnp.sum(jnp.where(lane == step * per_step + j, gates, 0.0), axis=1, keepdims=True)
        hidden.append(_swiglu_hidden(xq, xs, w1_ref[j], w3_ref[j], ws_ref[j], gcol).astype(BF16))
    w2 = w2_ref[0].astype(BF16)
    acc_ref[...] += _dot(jnp.concatenate(hidden, axis=1), w2.reshape(per_step * w2.shape[1], w2.shape[2]))

    @pl.when(step == pl.num_programs(1) - 1)
    def _():
        y = _layer_norm_rows(ALPHA * x_ref[...] + acc_ref[...], g_ref[...], b_ref[...])
        o_ref[...] = y
        obf_ref[...] = y.astype(BF16)


def _quantize_weights_kernel(w1_ref, w3_ref, q1_ref, q3_ref, s_ref):
    for e in range(q1_ref.shape[0]):
        q1, s1 = _quantize_fp8(w1_ref[0, e], (0, 1))
        q3, s3 = _quantize_fp8(w3_ref[0, e], (0, 1))
        q1_ref[e] = q1
        q3_ref[e] = q3
        ff = s_ref.shape[2]
        s_ref[e] = jnp.concatenate([jnp.broadcast_to(s1, (1, ff)), jnp.broadcast_to(s3, (1, ff))], axis=0)


def _quantize_expert_weights(w1_layers, w3_layers, layer):
    _, ne, d, ff = w1_layers.shape
    per_step = math.gcd(ne, MOE_EXPERTS_PER_STEP)
    blk = lambda *shape: pl.BlockSpec((per_step,) + shape, lambda i: (i, 0, 0))
    src = pl.BlockSpec((1, per_step, d, ff), lambda i: (layer, i, 0, 0))
    return pl.pallas_call(
        _quantize_weights_kernel,
        grid=(ne // per_step,),
        in_specs=[src, src],
        out_specs=[blk(d, ff), blk(d, ff), blk(2, ff)],
        out_shape=[jax.ShapeDtypeStruct((ne, d, ff), FP8), jax.ShapeDtypeStruct((ne, d, ff), FP8),
                   jax.ShapeDtypeStruct((ne, 2, ff), F32)],
        compiler_params=_params("parallel"),
        name="quantize_expert_weights",
    )(w1_layers, w3_layers)


def moe_experts_ln(x, gates, w1_layers, w3_layers, w2_layers, layer, sw1_layers, sw3_layers, sw2, g, b, tm=1024):
    n, d = x.shape
    ne = w1_layers.shape[1]
    w1q, w3q, ws = _quantize_expert_weights(w1_layers, w3_layers, layer)
    sw1q, sw3q, sws = (a[0] for a in _quantize_expert_weights(sw1_layers[:, None], sw3_layers[:, None], layer))
    tok = lambda i, e: (i, 0)
    fixed = lambda i, e: (0, 0)
    per_expert = lambda *blk: pl.BlockSpec((MOE_EXPERTS_PER_STEP,) + blk, lambda i, e: (e, 0, 0))
    return pl.pallas_call(
        _experts_ln_kernel,
        grid=(n // tm, ne // MOE_EXPERTS_PER_STEP),
        in_specs=[pl.BlockSpec((tm, d), tok), pl.BlockSpec((tm, LANES), tok),
                  per_expert(d, EXPERT_FF), per_expert(d, EXPERT_FF), per_expert(2, EXPERT_FF),
                  pl.BlockSpec((1, MOE_EXPERTS_PER_STEP, EXPERT_FF, d), lambda i, e: (layer, e, 0, 0)),
                  pl.BlockSpec((d, EXPERT_FF), fixed), pl.BlockSpec((d, EXPERT_FF), fixed),
                  pl.BlockSpec((2, EXPERT_FF), fixed), pl.BlockSpec((EXPERT_FF, d), fixed),
                  pl.BlockSpec((1, d), fixed), pl.BlockSpec((1, d), fixed)],
        out_specs=[pl.BlockSpec((tm, d), tok), pl.BlockSpec((tm, d), tok)],
        out_shape=[jax.ShapeDtypeStruct((n, d), F32), jax.ShapeDtypeStruct((n, d), BF16)],
        scratch_shapes=[pltpu.VMEM((tm, d), F32), pltpu.VMEM((tm, d), FP8), pltpu.VMEM((tm, 1), F32)],
        compiler_params=_params("parallel", "arbitrary"),
        name="moe_experts_ln",
    )(x, gates, w1q, w3q, ws, w2_layers, sw1q, sw3q, sws, sw2, g.reshape(1, d), b.reshape(1, d))


def _s5_tables(lam_re, lam_im, log_dt, b_re, b_im, c_re, c_im, n_chunk):
    t, h, p = S5_CHUNK, S5_GROUP_CH, S5_STATE
    dt = jnp.exp(log_dt.astype(F32))[:, None]
    den = lam_re ** 2 + lam_im ** 2

    def lam_pow(k):
        k = jnp.asarray(k, F32)[..., None, None]
        mag = jnp.exp(lam_re * dt * k)
        return mag * jnp.cos(lam_im * dt * k), mag * jnp.sin(lam_im * dt * k)

    lb_re, lb_im = lam_pow(1.0)
    f_re = ((lb_re - 1.0) * lam_re + lb_im * lam_im) / den
    f_im = (lb_im * lam_re - (lb_re - 1.0) * lam_im) / den
    bb_re = f_re[..., None] * b_re - f_im[..., None] * b_im
    bb_im = f_re[..., None] * b_im + f_im[..., None] * b_re
    pr, pi = lam_pow(jnp.arange(t))
    cl_re = c_re[None] * pr[:, :, None, :] - c_im[None] * pi[:, :, None, :]
    cl_im = c_re[None] * pi[:, :, None, :] + c_im[None] * pr[:, :, None, :]
    klag = jnp.einsum('tgop,gpi->tgoi', cl_re, bb_re) - jnp.einsum('tgop,gpi->tgoi', cl_im, bb_im)
    nb = S5_GROUPS // S5_PACK
    split = lambda a, axis: a.reshape(a.shape[:axis] + (nb, S5_PACK) + a.shape[axis + 1:])
    eye = jnp.eye(S5_PACK, dtype=F32)
    lag_t = jnp.transpose(split(klag, 1), (1, 0, 2, 4, 3))
    lag_t = (lag_t[:, :, :, :, None, :] * eye[None, None, :, None, :, None]).reshape(nb, t, LANES, LANES)
    qr, qi = lam_pow(t - 1 - jnp.arange(t))
    st_re = qr[..., None] * bb_re[None] - qi[..., None] * bb_im[None]
    st_im = qr[..., None] * bb_im[None] + qi[..., None] * bb_re[None]
    st = jnp.stack([st_re, st_im], axis=0)
    st_t = jnp.transpose(split(st, 2), (2, 1, 3, 5, 0, 4)).reshape(nb, t, LANES, 2 * p)
    er, ei = lam_pow(jnp.arange(t) + 1)
    x_re = c_re[None] * er[:, :, None, :] - c_im[None] * ei[:, :, None, :]
    x_im = c_re[None] * ei[:, :, None, :] + c_im[None] * er[:, :, None, :]
    cr = jnp.stack([x_re, -x_im], axis=0)
    cr_t = jnp.transpose(split(cr, 2), (2, 1, 0, 5, 3, 4)).reshape(nb, t, 2 * p, LANES)
    levels = max(1, int(math.log2(n_chunk)))
    sr, si = lam_pow(t * (2.0 ** jnp.arange(levels)))
    sr = sr.reshape(levels, nb, S5_PACK * p)
    si = si.reshape(levels, nb, S5_PACK * p)
    a1 = jnp.concatenate([sr, sr], axis=-1)
    a2 = jnp.concatenate([-si, si], axis=-1)
    scan = jnp.transpose(jnp.stack([a1, a2], axis=1), (2, 0, 1, 3))
    return lag_t, st_t, cr_t, scan.astype(F32)


def _s5_build_tables(lag_ref, st_ref, cr_ref, wtoe_ref, wstate_ref, wcross_ref):
    t = lag_ref.shape[1]
    p = S5_STATE
    kp = S5_PACK * p
    wtoe_ref[...] = jnp.zeros_like(wtoe_ref)
    for d in range(t):
        tile = lag_ref[0, d].astype(BF16)
        for j in range(t - d):
            wtoe_ref[j * LANES:(j + 1) * LANES, (j + d) * LANES:(j + d + 1) * LANES] = tile
    lane = lax.broadcasted_iota(jnp.int32, (LANES, LANES), 1)
    row_g = lax.broadcasted_iota(jnp.int32, (LANES, kp), 0) // S5_GROUP_CH
    same_s = row_g == lax.broadcasted_iota(jnp.int32, (LANES, kp), 1) // p
    for j in range(t):
        a = st_ref[0, j]
        swapped = pltpu.roll(a, p, 1)
        for c, both in enumerate((jnp.where(lane < p, a, swapped), jnp.where(lane < p, swapped, a))):
            wide = jnp.concatenate([both] * (kp // LANES), axis=1)
            wstate_ref[j * LANES:(j + 1) * LANES, c * kp:(c + 1) * kp] = jnp.where(same_s, wide, 0.0).astype(BF16)
    same_c = (lax.broadcasted_iota(jnp.int32, (kp, LANES), 0) // p
              == lax.broadcasted_iota(jnp.int32, (kp, LANES), 1) // S5_GROUP_CH)
    for i in range(t):
        a = cr_ref[0, i]
        for c in range(2):
            tall = jnp.concatenate([a[c * p:(c + 1) * p]] * S5_PACK, axis=0)
            wcross_ref[c * kp:(c + 1) * kp, i * LANES:(i + 1) * LANES] = jnp.where(same_c, tall, 0.0).astype(BF16)


def _s5_kernel(u_ref, lag_ref, st_ref, cr_ref, scan_ref, o_ref, wtoe_ref, wstate_ref, wcross_ref):
    @pl.when(pl.program_id(1) == 0)
    def _():
        _s5_build_tables(lag_ref, st_ref, cr_ref, wtoe_ref, wstate_ref, wcross_ref)

    n_chunk, t, _ = u_ref.shape
    x = jnp.concatenate([u_ref[:, j, :] for j in range(t)], axis=1).astype(BF16)
    local = _dot(x, wtoe_ref[...])
    state = _dot(x, wstate_ref[...])
    row = lax.broadcasted_iota(jnp.int32, state.shape, 0)
    s = jnp.where(row >= 1, pltpu.roll(state, 1, 0), 0.0)
    half = state.shape[1] // 2
    level = 0
    d = 1
    while d < n_chunk:
        mult = scan_ref[0, level]
        prev = jnp.where(row >= d, pltpu.roll(s, d, 0), 0.0)
        s = s + mult[0:1, :] * prev + mult[1:2, :] * pltpu.roll(prev, half, 1)
        d *= 2
        level += 1
    y = local + _dot(s.astype(BF16), wcross_ref[...])
    for i in range(t):
        o_ref[:, i, :] = y[:, i * LANES:(i + 1) * LANES]


def s5_scan(u3, bsz, lag_t, st_t, cr_t, scan):
    rows, t, w = u3.shape
    n_chunk = rows // bsz
    kp2 = 2 * S5_PACK * S5_STATE
    table = lambda a: pl.BlockSpec((1,) + a.shape[1:], lambda j, b: (j, 0, 0, 0))
    return pl.pallas_call(
        _s5_kernel,
        grid=(w // LANES, bsz),
        in_specs=[pl.BlockSpec((n_chunk, t, LANES), lambda j, b: (b, 0, j)),
                  table(lag_t), table(st_t), table(cr_t), table(scan)],
        out_specs=pl.BlockSpec((n_chunk, t, LANES), lambda j, b: (b, 0, j)),
        out_shape=jax.ShapeDtypeStruct(u3.shape, F32),
        scratch_shapes=[pltpu.VMEM((t * LANES, t * LANES), BF16), pltpu.VMEM((t * LANES, kp2), BF16),
                        pltpu.VMEM((kp2, t * LANES), BF16)],
        compiler_params=_params("arbitrary", "arbitrary"),
        name="s5_scan",
    )(u3, lag_t, st_t, cr_t, scan)


def _s5_post_kernel(y_ref, u_ref, d_ref, w_ref, b_ref, o_ref):
    u = u_ref[...]
    y = _gelu(y_ref[...].reshape(u.shape) + d_ref[...] * u)
    o_ref[...] = (y * _sigmoid(_dot(y.astype(BF16), w_ref[...]) + b_ref[...])).astype(o_ref.dtype)


def s5_post(y3, h, d_skip, w_glu, b_glu, tm=1024):
    rows, t, w = y3.shape
    n = rows * t
    tm = min(tm, n)
    row = lambda i: (i, 0)
    fixed = lambda i: (0, 0)
    return pl.pallas_call(
        _s5_post_kernel,
        grid=(n // tm,),
        in_specs=[pl.BlockSpec((tm // t, t, w), lambda i: (i, 0, 0)), pl.BlockSpec((tm, w), row),
                  pl.BlockSpec((1, w), fixed), pl.BlockSpec((w, w), fixed), pl.BlockSpec((1, w), fixed)],
        out_specs=pl.BlockSpec((tm, w), row),
        out_shape=jax.ShapeDtypeStruct((n, w), BF16),
        compiler_params=_params("parallel"),
        name="s5_post",
    )(y3, h, d_skip.reshape(1, w), w_glu.astype(BF16), b_glu.reshape(1, w))


def s5_mixer(h, u3, bsz, seq, lam_re, lam_im, log_dt, b_re, b_im, c_re, c_im, d_skip, w_glu, b_glu):
    tables = _s5_tables(lam_re, lam_im, log_dt, b_re, b_im, c_re, c_im, seq // S5_CHUNK)
    return s5_post(s5_scan(u3, bsz, *tables), h, d_skip, w_glu, b_glu)


def _rope_tables(pos, rot_dim, theta, head_dim, n_heads):
    half = rot_dim // 2
    f32 = np.float32
    inv_freq = f32(theta) ** (-np.arange(half, dtype=f32) / f32(half))
    ang = (pos.astype(f32)[:, None] * inv_freq[None, :]).astype(np.float64)
    cos, sin = np.cos(ang), np.sin(ang)
    rest = head_dim - rot_dim
    n = pos.shape[0]
    c = np.concatenate([cos, cos, np.ones((n, rest))], axis=1)
    s_up = np.concatenate([-sin, np.zeros((n, half + rest))], axis=1)
    s_dn = np.concatenate([np.zeros((n, half)), sin, np.zeros((n, rest))], axis=1)
    tile = lambda a: jnp.asarray(np.tile(a, (1, n_heads)), F32)
    return tile(c), tile(s_up), tile(s_dn)


def _rope_apply(x, c, s_up, s_dn, half):
    return x * c + pltpu.roll(x, LANES - half, 1) * s_up + pltpu.roll(x, half, 1) * s_dn


def _retention_tables():
    c = RET_CHUNK
    log_gamma = np.log(1.0 - 2.0 ** (-5.0 - np.arange(RET_HEADS, dtype=np.float64)))
    i = np.arange(c, dtype=np.float64)
    diff = i[:, None] - i[None, :]
    decay = np.where(diff >= 0, np.exp(diff[None] * log_gamma[:, None, None]), 0.0)
    qdec = np.repeat(np.exp((i + 1.0)[:, None] * log_gamma[None, :]), RET_DK, axis=1)
    kdec = np.repeat(np.exp((c - 1.0 - i)[:, None] * log_gamma[None, :]), RET_DK, axis=1)
    chunk_decay = [float(v) for v in np.exp(c * log_gamma)]
    return jnp.asarray(decay, F32), jnp.asarray(qdec, F32), jnp.asarray(kdec, F32), chunk_decay


def _retention_kernel(chunk_decay, q_ref, k_ref, v0_ref, v1_ref, g0_ref, g1_ref, c_ref, su_ref, sd_ref,
                      dec_ref, qdec_ref, kdec_ref, lng_ref, lnb_ref, o_ref, state_ref):
    @pl.when(pl.program_id(1) == 0)
    def _():
        state_ref[...] = jnp.zeros_like(state_ref)

    half = RET_DK // 2
    tabs = (c_ref[...], su_ref[...], sd_ref[...])
    q = jnp.concatenate([_rope_apply(q_ref[:, s:s + LANES], *tabs, half) for s in (0, LANES)], axis=1)
    k = jnp.concatenate([_rope_apply(k_ref[:, s:s + LANES], *tabs, half) for s in (0, LANES)], axis=1)
    k = k * (RET_DK ** -0.5)
    q_dec = q * qdec_ref[...]
    k_dec = k * kdec_ref[...]
    v = jnp.concatenate([v0_ref[...], v1_ref[...]], axis=1)
    gate = jnp.concatenate([g0_ref[...], g1_ref[...]], axis=1)
    states = [state_ref[h] for h in range(RET_HEADS)]

    def head(h):
        ks = slice(h * RET_DK, (h + 1) * RET_DK)
        vh = v[:, h * RET_DV:(h + 1) * RET_DV].astype(BF16)
        scores = _dot_nt(q[:, ks].astype(BF16), k[:, ks].astype(BF16)) * dec_ref[h]
        yield
        y = _dot(scores.astype(BF16), vh) + _dot(q_dec[:, ks].astype(BF16), states[h].astype(BF16))
        yield
        new_state = states[h] * chunk_decay[h] + _dot_tn(k_dec[:, ks].astype(BF16), vh)
        yield
        mu = jnp.mean(y, axis=-1, keepdims=True)
        yc = y - mu
        var = jnp.mean(yc * yc, axis=-1, keepdims=True)
        return yc * lax.rsqrt(var + RET_GN_EPS), new_state

    results = _run_interleaved([head(h) for h in range(RET_HEADS)])
    outs = [r[0] for r in results]
    for h, (_, new_state) in enumerate(results):
        state_ref[h] = new_state
    yn = jnp.concatenate(outs, axis=1) * lng_ref[...] + lnb_ref[...]
    o_ref[...] = (gate * _sigmoid(gate) * yn).astype(o_ref.dtype)


def retention_mixer(h, bsz, seq, col0, ln_g, ln_b):
    c = RET_CHUNK
    n_chunk = seq // c
    qk_w = RET_HEADS * RET_DK
    v_w = RET_HEADS * RET_DV
    assert col0 % qk_w == 0 and qk_w == 2 * LANES and v_w == 2 * qk_w
    cb = col0 // qk_w
    rc, rsu, rsd = _rope_tables(np.arange(seq), RET_DK, RET_THETA, RET_DK, 2)
    dec, qdec, kdec, chunk_decay = _retention_tables()
    row = lambda j: (lambda b, n: (b * n_chunk + n, j))
    pos = lambda b, n: (n, 0)
    fixed2 = lambda b, n: (0, 0)
    kern = functools.partial(_retention_kernel, chunk_decay)
    return pl.pallas_call(
        kern,
        grid=(bsz, n_chunk),
        in_specs=[pl.BlockSpec((c, qk_w), row(cb)), pl.BlockSpec((c, qk_w), row(cb + 1)),
                  pl.BlockSpec((c, qk_w), row(cb + 2)), pl.BlockSpec((c, qk_w), row(cb + 3)),
                  pl.BlockSpec((c, qk_w), row(cb + 4)), pl.BlockSpec((c, qk_w), row(cb + 5)),
                  pl.BlockSpec((c, LANES), pos), pl.BlockSpec((c, LANES), pos), pl.BlockSpec((c, LANES), pos),
                  pl.BlockSpec((RET_HEADS, c, c), lambda b, n: (0, 0, 0)),
                  pl.BlockSpec((c, qk_w), fixed2), pl.BlockSpec((c, qk_w), fixed2),
                  pl.BlockSpec((1, v_w), fixed2), pl.BlockSpec((1, v_w), fixed2)],
        out_specs=pl.BlockSpec((c, v_w), lambda b, n: (b * n_chunk + n, 0)),
        out_shape=jax.ShapeDtypeStruct((bsz * seq, v_w), BF16),
        scratch_shapes=[pltpu.VMEM((RET_HEADS, RET_DK, RET_DV), F32)],
        compiler_params=_params("parallel", "arbitrary"),
        name="retention",
    )(h, h, h, h, h, h, rc, rsu, rsd, dec, qdec, kdec, ln_g.reshape(1, v_w), ln_b.reshape(1, v_w))


NSA_KV_W = NSA_KV_GROUPS * NSA_HEAD_DIM
NSA_VT_ROWS = NSA_HEAD_DIM + 16
NSA_GATE_COLS = 3 * NSA_HEADS


def _nsa_prep_kernel(q_ref, kvc_ref, kvs_ref, kvw_ref, c_ref, su_ref, sd_ref,
                     qo_ref, kc_ref, vc_ref, ks_ref, vs_ref, kw_ref, vw_ref):
    half = NSA_ROT_DIM // 2
    tabs = (c_ref[...], su_ref[...], sd_ref[...])
    scale = NSA_HEAD_DIM ** -0.5 * math.log2(math.e)
    q = jnp.concatenate(
        [_rope_apply(q_ref[:, s:s + LANES], *tabs, half) * scale for s in range(0, NSA_WIDTH, LANES)], axis=1)
    qo_ref[0] = q.T.astype(qo_ref.dtype)

    def split(x, o_ref):
        for g in range(NSA_KV_GROUPS):
            o_ref[0, g] = x[:, g * NSA_HEAD_DIM:(g + 1) * NSA_HEAD_DIM].astype(o_ref.dtype)

    def split_t(x, o_ref):
        xt = x.T
        for g in range(NSA_KV_GROUPS):
            o_ref[0, g, :NSA_HEAD_DIM] = xt[g * NSA_HEAD_DIM:(g + 1) * NSA_HEAD_DIM, :].astype(o_ref.dtype)
            o_ref[0, g, NSA_HEAD_DIM:] = jnp.ones((NSA_VT_ROWS - NSA_HEAD_DIM, xt.shape[1]), o_ref.dtype)

    split(kvc_ref[:, :NSA_KV_W], kc_ref)
    split(kvc_ref[:, NSA_KV_W:], vc_ref)
    n_hot = ks_ref.shape[3] - NSA_HEAD_DIM
    tl = q_ref.shape[0]
    blk = (pl.program_id(1) * tl + lax.broadcasted_iota(jnp.int32, (tl, n_hot), 0)) // SLC_BLOCK
    one_hot = jnp.where(blk % n_hot == lax.broadcasted_iota(jnp.int32, (tl, n_hot), 1), 1.0, 0.0)
    ks = _rope_apply(kvs_ref[:, :NSA_KV_W], *tabs, half)
    for g in range(NSA_KV_GROUPS):
        ks_ref[0, g] = jnp.concatenate([ks[:, g * NSA_HEAD_DIM:(g + 1) * NSA_HEAD_DIM], one_hot],
                                       axis=1).astype(ks_ref.dtype)
    split_t(kvs_ref[:, NSA_KV_W:], vs_ref)
    split(_rope_apply(kvw_ref[:, :NSA_KV_W], *tabs, half), kw_ref)
    split_t(kvw_ref[:, NSA_KV_W:], vw_ref)


def nsa_prep(h, bsz, seq, tl=512):
    tl = min(tl, seq)
    nl = seq // tl
    rc, rsu, rsd = _rope_tables(np.arange(seq), NSA_ROT_DIM, ROPE_THETA, NSA_HEAD_DIM, LANES // NSA_HEAD_DIM)
    row = lambda j: (lambda b, l: (b * nl + l, j))
    pos = lambda b, l: (l, 0)
    kv_out = pl.BlockSpec((1, NSA_KV_GROUPS, tl, NSA_HEAD_DIM), lambda b, l: (b, 0, l, 0))
    kv_shape = lambda dt: jax.ShapeDtypeStruct((bsz, NSA_KV_GROUPS, seq, NSA_HEAD_DIM), dt)
    vt_out = pl.BlockSpec((1, NSA_KV_GROUPS, NSA_VT_ROWS, tl), lambda b, l: (b, 0, 0, l))
    vt_shape = jax.ShapeDtypeStruct((bsz, NSA_KV_GROUPS, NSA_VT_ROWS, seq), BF16)
    ks_w = NSA_HEAD_DIM + min(NSA_KEY_TILE, seq) // SLC_BLOCK
    ks_out = pl.BlockSpec((1, NSA_KV_GROUPS, tl, ks_w), lambda b, l: (b, 0, l, 0))
    ks_shape = jax.ShapeDtypeStruct((bsz, NSA_KV_GROUPS, seq, ks_w), BF16)
    two = 2 * NSA_KV_W
    return pl.pallas_call(
        _nsa_prep_kernel,
        grid=(bsz, nl),
        in_specs=[pl.BlockSpec((tl, NSA_WIDTH), row(1)),
                  pl.BlockSpec((tl, two), row(4)), pl.BlockSpec((tl, two), row(5)), pl.BlockSpec((tl, two), row(6)),
                  pl.BlockSpec((tl, LANES), pos), pl.BlockSpec((tl, LANES), pos), pl.BlockSpec((tl, LANES), pos)],
        out_specs=[pl.BlockSpec((1, NSA_WIDTH, tl), lambda b, l: (b, 0, l)),
                   kv_out, kv_out, ks_out, vt_out, kv_out, vt_out],
        out_shape=[jax.ShapeDtypeStruct((bsz, NSA_WIDTH, seq), BF16),
                   kv_shape(F32), kv_shape(F32), ks_shape, vt_shape, kv_shape(BF16), vt_shape],
        compiler_params=_params("parallel", "parallel"),
        name="nsa_prep",
    )(h, h, h, h, rc, rsu, rsd)


def _nsa_compress_kernel(hk_ref, hv_ref, pek_ref, pev_ref, kw1_ref, kb1_ref, kw2_ref, vw1_ref, vb1_ref, vw2_ref,
                         c_ref, su_ref, sd_ref, ko_ref, vo_ref):
    def mlp(h_ref, pe_ref, w1_ref, b1_ref, w2_ref):
        hb = h_ref[0, 0]
        rows = hb.shape[0]
        first = _dot((hb + pe_ref[0:1, :]).astype(BF16), w1_ref[0])
        second = _dot((hb + pe_ref[1:2, :]).astype(BF16), w1_ref[1])
        hid = _gelu(first + pltpu.roll(second, rows - 1, 0) + b1_ref[...])
        return _dot(hid.astype(BF16), w2_ref[...])

    kc = _rope_apply(mlp(hk_ref, pek_ref, kw1_ref, kb1_ref, kw2_ref), c_ref[...], su_ref[...], sd_ref[...],
                     NSA_ROT_DIM // 2)
    vc = mlp(hv_ref, pev_ref, vw1_ref, vb1_ref, vw2_ref)
    ko_ref[0, 0] = kc[:, :NSA_HEAD_DIM].astype(ko_ref.dtype)
    vo_ref[0, 0, :NSA_HEAD_DIM] = vc.T[:NSA_HEAD_DIM, :].astype(vo_ref.dtype)
    vo_ref[0, 0, NSA_HEAD_DIM:] = jnp.ones((NSA_VT_ROWS - NSA_HEAD_DIM, vc.shape[0]), vo_ref.dtype)


def nsa_compress(kc, vc, pe_k, pe_v, ck_w1, ck_b1, ck_w2, cv_w1, cv_b1, cv_w2):
    bsz, grp, seq, d = kc.shape
    n_rows = seq // CMP_STRIDE
    flat = CMP_STRIDE * d
    cmp_end = np.arange(n_rows) * CMP_STRIDE + CMP_BLOCK - 1
    rc, rsu, rsd = _rope_tables(cmp_end, NSA_ROT_DIM, ROPE_THETA, NSA_HEAD_DIM, LANES // NSA_HEAD_DIM)
    pad_w2 = lambda w: jnp.pad(w, ((0, 0), (0, LANES - d))).astype(BF16)
    blk = pl.BlockSpec((1, 1, n_rows, flat), lambda b, g: (b, g, 0, 0))
    f2 = lambda b, g: (0, 0)
    f3 = lambda b, g: (0, 0, 0)
    w_specs = [pl.BlockSpec((2, flat, CMP_HIDDEN), f3), pl.BlockSpec((1, CMP_HIDDEN), f2),
               pl.BlockSpec((CMP_HIDDEN, LANES), f2)]
    return pl.pallas_call(
        _nsa_compress_kernel,
        grid=(bsz, grp),
        in_specs=[blk, blk, pl.BlockSpec((2, flat), f2), pl.BlockSpec((2, flat), f2)] + w_specs + w_specs
                 + [pl.BlockSpec((n_rows, LANES), f2)] * 3,
        out_specs=[pl.BlockSpec((1, 1, n_rows, d), lambda b, g: (b, g, 0, 0)),
                   pl.BlockSpec((1, 1, NSA_VT_ROWS, n_rows), lambda b, g: (b, g, 0, 0))],
        out_shape=[jax.ShapeDtypeStruct((bsz, grp, n_rows, d), BF16),
                   jax.ShapeDtypeStruct((bsz, grp, NSA_VT_ROWS, n_rows), BF16)],
        compiler_params=_params("parallel", "parallel"),
        name="nsa_compress",
    )(kc.reshape(bsz, grp, n_rows, flat), vc.reshape(bsz, grp, n_rows, flat),
      pe_k.reshape(2, flat), pe_v.reshape(2, flat),
      ck_w1.reshape(2, flat, CMP_HIDDEN).astype(BF16), ck_b1.reshape(1, CMP_HIDDEN), pad_w2(ck_w2),
      cv_w1.reshape(2, flat, CMP_HIDDEN).astype(BF16), cv_b1.reshape(1, CMP_HIDDEN), pad_w2(cv_w2),
      rc, rsu, rsd)


def _per_head(x):
    return jnp.concatenate([x] * NSA_HPG, axis=1)


def _nsa_attn_kernel(seq, tk, qt_ref, gate_ref, kc_ref, vct_ref, ks_ref, vst_ref, kw_ref, vwt_ref, mmapt_ref,
                     o_ref, sel_ref):
    n_blk = seq // SLC_BLOCK
    n_sel = min(N_SLC, n_blk)
    hd = NSA_HEAD_DIM
    w = NSA_HPG * hd
    groups = range(NSA_KV_GROUPS)
    q0 = pl.program_id(1) * Q_BLOCK
    t_l = q0 + lax.broadcasted_iota(jnp.int32, (1, Q_BLOCK), 1)

    def select(g):
        qg = qt_ref[0, g * w:(g + 1) * w, :]
        qst = jnp.concatenate([qg[h * hd:(h + 1) * hd, :] for h in range(NSA_HPG)], axis=1)
        kc = kc_ref[0, g]
        n_cmp = kc.shape[0]
        cmp_end = lax.broadcasted_iota(jnp.int32, (n_cmp, 1), 0) * CMP_STRIDE + (CMP_BLOCK - 1)
        s = _dot(kc, qst) + _per_head(jnp.where(cmp_end <= t_l, 0.0, MASK_VALUE))
        yield
        p = jnp.exp2(s - jnp.max(s, axis=0, keepdims=True))
        any_key = _per_head(jnp.where(t_l >= CMP_BLOCK - 1, 1.0, 0.0))
        pv = _dot(vct_ref[0, g], p.astype(BF16))
        inv_l = any_key / pv[hd:hd + 1]
        o_cmp = pv[:hd] * inv_l
        yield
        p = p * inv_l
        imp = p[:, 0:Q_BLOCK]
        for h in range(1, NSA_HPG):
            imp = imp + p[:, h * Q_BLOCK:(h + 1) * Q_BLOCK]
        imp_slc = _dot_split(mmapt_ref[...], imp, 'b', 3)
        yield
        blk = lax.broadcasted_iota(jnp.int32, (n_blk, 1), 0)
        cur = t_l // SLC_BLOCK
        score = jnp.where(blk == 0, FORCE_SCORE,
                          jnp.where(blk == cur, FORCE_SCORE, jnp.where(blk == cur - 1, FORCE_SCORE, imp_slc)))
        score = jnp.where(blk * SLC_BLOCK <= t_l, score, -FORCE_SCORE)
        sel = jnp.zeros((n_blk, Q_BLOCK), F32)
        for _ in range(n_sel):
            best = jnp.max(score, axis=0, keepdims=True)
            idx = jnp.min(jnp.where(score == best, blk, n_blk), axis=0, keepdims=True)
            pick = blk == idx
            sel = jnp.where(pick, 1.0, sel)
            score = jnp.where(pick, -jnp.inf, score)
            yield
        sel_ref[g] = sel
        return qst, o_cmp

    selected = _run_interleaved([select(g) for g in groups])
    qst = [r[0] for r in selected]
    o_cmp = [r[1] for r in selected]

    blocks_per_tile = tk // SLC_BLOCK
    assert ks_ref.shape[3] == hd + blocks_per_tile and Q_BLOCK <= tk and tk % Q_BLOCK == 0

    def slc_tile(kt, carry, causal_bias=None):
        k0 = pl.multiple_of(kt * tk, tk)
        out = []
        for g in groups:
            m, acc = carry[g]
            sel_rows = sel_ref[g, pl.ds(pl.multiple_of(kt * blocks_per_tile, blocks_per_tile), blocks_per_tile), :]
            q_aug = jnp.concatenate([qst[g], _per_head((sel_rows - 1.0) * -MASK_VALUE).astype(BF16)], axis=0)
            s = _dot(ks_ref[0, g, pl.ds(k0, tk), :], q_aug)
            if causal_bias is not None:
                s = s + causal_bias
            m_new = jnp.maximum(m, jnp.max(s, axis=0, keepdims=True))
            alpha = jnp.exp2(m - m_new)
            p = jnp.exp2(s - m_new)
            acc = alpha * acc + _dot(vst_ref[0, g, :, pl.ds(k0, tk)], p.astype(BF16))
            out.append((m_new, acc))
        return tuple(out)

    n_full = q0 // tk
    cols = NSA_HPG * Q_BLOCK
    init = tuple((jnp.full((1, cols), MASK_VALUE, F32), jnp.zeros((NSA_VT_ROWS, cols), F32)) for _ in groups)
    slc = lax.fori_loop(0, n_full, slc_tile, init)
    kpos = n_full * tk + lax.broadcasted_iota(jnp.int32, (tk, 1), 0)
    slc = slc_tile(n_full, slc, _per_head(jnp.where(kpos <= t_l, 0.0, MASK_VALUE)))

    band = WINDOW + Q_BLOCK
    w0 = pl.multiple_of(jnp.maximum(q0 - WINDOW, 0), Q_BLOCK)
    kpos = w0 + lax.broadcasted_iota(jnp.int32, (band, 1), 0)
    win_bias = _per_head(jnp.where(kpos <= t_l, jnp.where(kpos > t_l - WINDOW, 0.0, MASK_VALUE), MASK_VALUE))
    sig_t = _sigmoid(gate_ref[...]).T

    def finish(g):
        s = _dot(kw_ref[0, g, pl.ds(w0, band), :], qst[g]) + win_bias
        yield
        p = jnp.exp2(s - jnp.max(s, axis=0, keepdims=True))
        pv = _dot(vwt_ref[0, g, :, pl.ds(w0, band)], p.astype(BF16))
        o_win = pv[:hd] / pv[hd:hd + 1]
        yield
        acc = slc[g][1]
        acc_slc, l_slc = acc[:hd], acc[hd:hd + 1]

        def gate(branch):
            first = (g * NSA_HPG) * 3 + branch
            return jnp.concatenate([sig_t[first + 3 * h:first + 3 * h + 1, :] for h in range(NSA_HPG)], axis=1)

        out_t = gate(0) * o_cmp[g] + gate(1) * (acc_slc / l_slc) + gate(2) * o_win
        pairs = []
        for h in range(0, NSA_HPG, 2):
            two = jnp.concatenate([out_t[:, h * Q_BLOCK:(h + 1) * Q_BLOCK],
                                   out_t[:, (h + 1) * Q_BLOCK:(h + 2) * Q_BLOCK]], axis=0)
            pairs.append(two.T)
        o_ref[:, g * w:(g + 1) * w] = jnp.concatenate(pairs, axis=1).astype(o_ref.dtype)

    _run_interleaved([finish(g) for g in groups])


def _nsa_pool_matrix(seq):
    n_blk = seq // SLC_BLOCK
    n_rows = seq // CMP_STRIDE
    per_stride = SLC_BLOCK // CMP_STRIDE
    span = CMP_BLOCK // CMP_STRIDE
    pool = np.zeros((n_blk, n_rows), np.float32)
    for j in range(n_blk):
        for m in range(per_stride):
            for n in range(span):
                c = per_stride * j + m + n - (span - 1)
                if 0 <= c < n_rows - 1:
                    pool[j, c] += 1.0
    return jnp.asarray(pool, BF16)


def nsa_attention(qt, h, gate_col_block, k_cmp, v_cmp_t, ks, vs_t, kw, vw_t, bsz, seq):
    tk = min(NSA_KEY_TILE, seq)
    nq = seq // Q_BLOCK
    pool = _nsa_pool_matrix(seq)
    n_rows = k_cmp.shape[2]
    d = NSA_HEAD_DIM
    qblk = lambda b, i: (b * nq + i, 0)
    whole = lambda *shape: pl.BlockSpec((1, NSA_KV_GROUPS) + shape, lambda b, i: (b, 0, 0, 0))
    kern = functools.partial(_nsa_attn_kernel, seq, tk)
    return pl.pallas_call(
        kern,
        grid=(bsz, nq),
        in_specs=[pl.BlockSpec((1, NSA_WIDTH, Q_BLOCK), lambda b, i: (b, 0, i)),
                  pl.BlockSpec((Q_BLOCK, LANES), lambda b, i: (b * nq + i, gate_col_block)),
                  whole(n_rows, d), whole(NSA_VT_ROWS, n_rows), whole(seq, ks.shape[3]), whole(NSA_VT_ROWS, seq),
                  whole(seq, d), whole(NSA_VT_ROWS, seq),
                  pl.BlockSpec(pool.shape, lambda b, i: (0, 0))],
        out_specs=pl.BlockSpec((Q_BLOCK, NSA_WIDTH), qblk),
        out_shape=jax.ShapeDtypeStruct((bsz * seq, NSA_WIDTH), BF16),
        scratch_shapes=[pltpu.VMEM((NSA_KV_GROUPS, seq // SLC_BLOCK, Q_BLOCK), F32)],
        compiler_params=_params("parallel", "arbitrary"),
        name="nsa_attention",
    )(qt, h, k_cmp, v_cmp_t, ks, vs_t, kw, vw_t, pool)


def nsa_mixer(h, bsz, seq, gate_col_block, pe_k, pe_v, ck_w1, ck_b1, ck_w2, cv_w1, cv_b1, cv_w2):
    qt, kc, vc, ks, vs_t, kw, vw_t = nsa_prep(h, bsz, seq)
    k_cmp, v_cmp_t = nsa_compress(kc, vc, pe_k, pe_v, ck_w1, ck_b1, ck_w2, cv_w1, cv_b1, cv_w2)
    return nsa_attention(qt, h, gate_col_block, k_cmp, v_cmp_t, ks, vs_t, kw, vw_t, bsz, seq)


def _head_ones(width, head_dim):
    idx = np.arange(width) // head_dim
    return jnp.asarray(idx[:, None] == idx[None, :], BF16)


def _softplus(x):
    return jnp.maximum(x, 0.0) + jnp.log(1.0 + jnp.exp(-jnp.abs(x)))


def _rwkv_pre_kernel(p_ref, prev_ref, mu_ref, w0_ref, wup_ref, a0_ref, aup_ref, gup_ref, kk_ref, ka_ref, rk_ref,
                     ones_ref, r_o, k_o, v_o, kk_o, b_o, ld_o, g_o, bonus_o):
    w = RWKV_WIDTH
    p = p_ref[...]
    first_row = jnp.where(pl.program_id(1) == 0, 0.0, prev_ref[7:8, :])
    is_row0 = lax.broadcasted_iota(jnp.int32, p.shape, 0) == 0
    prev = jnp.where(is_row0, first_row, pltpu.roll(p, 1, 0))
    ps = p + (prev - p) * mu_ref[...]
    r, k, v = ps[:, 0:w], ps[:, w:2 * w], ps[:, 2 * w:3 * w]
    o = 3 * w
    w_lo = ps[:, o:o + RWKV_LORA_W]
    a_lo = ps[:, o + RWKV_LORA_W:o + RWKV_LORA_W + RWKV_LORA_A]
    g_lo = ps[:, o + RWKV_LORA_W + RWKV_LORA_A:]
    wlog = -_softplus(-(w0_ref[...] + _dot(jnp.tanh(w_lo).astype(BF16), wup_ref[...]))) - 0.5
    a = _sigmoid(a0_ref[...] + _dot(a_lo.astype(BF16), aup_ref[...]))
    g = _dot(_sigmoid(g_lo).astype(BF16), gup_ref[...])
    kk = k * kk_ref[...]
    norm = jnp.sqrt(_dot_split(kk * kk, ones_ref[...], 'a', 2))
    kk = kk / jnp.maximum(norm, 1e-12)
    k2 = k * (1.0 + (a - 1.0) * ka_ref[...])
    r_o[...] = r
    k_o[...] = k2
    v_o[...] = v
    kk_o[...] = kk
    b_o[...] = kk * a
    ld_o[...] = -jnp.exp(wlog)
    g_o[...] = g
    bonus_o[...] = _dot_split(r * k2 * rk_ref[...], ones_ref[...], 'a', 2) * v


def rwkv_pre(h, bsz, seq, mu, w0, w_up, a0, a_up, g_up, k_k, k_a, r_k, tl=512):
    tl = min(tl, seq)
    nl = seq // tl
    w = RWKV_WIDTH
    cols = RWKV_COLS
    ones = _head_ones(w, RWKV_HEAD_DIM)
    f2 = lambda b, l: (0, 0)
    vec = pl.BlockSpec((1, w), f2)
    out_spec = pl.BlockSpec((tl, w), lambda b, l: (b * nl + l, 0))
    out_shape = jax.ShapeDtypeStruct((bsz * seq, w), F32)
    return pl.pallas_call(
        _rwkv_pre_kernel,
        grid=(bsz, nl),
        in_specs=[pl.BlockSpec((tl, cols), lambda b, l: (b * nl + l, 0)),
                  pl.BlockSpec((8, cols), lambda b, l: (jnp.maximum((b * seq + l * tl) // 8 - 1, 0), 0)),
                  pl.BlockSpec((1, cols), f2), vec, pl.BlockSpec((RWKV_LORA_W, w), f2),
                  vec, pl.BlockSpec((RWKV_LORA_A, w), f2), pl.BlockSpec((RWKV_LORA_G, w), f2),
                  vec, vec, vec, pl.BlockSpec((w, w), f2)],
        out_specs=[out_spec] * 8,
        out_shape=[out_shape] * 8,
        compiler_params=_params("parallel", "parallel"),
        name="rwkv_pre",
    )(h, h, mu.reshape(1, cols), w0.reshape(1, w), w_up.astype(BF16), a0.reshape(1, w), a_up.astype(BF16),
      g_up.astype(BF16), k_k.reshape(1, w), k_a.reshape(1, w), r_k.reshape(1, w), ones)


def _rwkv_masks():
    t, pk = RWKV_CHUNK, RWKV_PACK
    n = t * pk
    ri = np.arange(n)
    same = (ri[:, None] // t) == (ri[None, :] // t)
    tt, ss = ri[:, None] % t, ri[None, :] % t
    levels = []
    k = 1
    while k < t:
        levels.append(same & (tt // (2 * k) == ss // (2 * k)) & ((tt // k) % 2 == 1) & ((ss // k) % 2 == 0))
        k *= 2
    lvl = np.stack(levels).astype(np.float32)
    tri = (np.arange(t)[:, None] >= np.arange(t)[None, :]).astype(np.float32)
    head_lane = ((ri[:, None] // t) == (np.arange(pk * RWKV_HEAD_DIM)[None, :] // RWKV_HEAD_DIM)).astype(np.float32)
    return (jnp.asarray(tri, BF16), jnp.asarray(head_lane), jnp.asarray(same.astype(np.float32)), jnp.asarray(lvl))


def _rwkv_chain(r, k, v, kk, b, ld, st, tri, hl, bd, lvl_ref):
    t, pk = RWKV_CHUNK, RWKV_PACK
    n = t * pk
    c = _dot_split(tri, ld, 'b', 3)
    yield
    c_end = c[t - 1:t, :]
    e_neg = jnp.exp(-c)
    e_end = jnp.exp(c_end - c)
    kkd = (kk * jnp.exp(c - ld)).astype(BF16)
    rd = (r * jnp.exp(c)).astype(BF16)

    def big(x):
        return (jnp.concatenate([x] * pk, axis=0) * hl).astype(BF16)

    st_b = st.astype(BF16)
    v_big = big(v)
    a_all = _dot_nt(jnp.concatenate([kkd, rd], axis=0),
                    jnp.concatenate([big(k * e_neg), big(b * e_neg)], axis=0))
    yield
    ti = lax.broadcasted_iota(jnp.int32, (t, n), 0)
    si = lax.broadcasted_iota(jnp.int32, (t, n), 1) % t
    strict = ti > si
    incl = ti >= si
    a_kk = jnp.where(strict, a_all[:t, :n], 0.0)
    a_kb = jnp.where(strict, a_all[:t, n:], 0.0)
    a_rk = jnp.where(incl, a_all[t:, :n], 0.0)
    a_rb = jnp.where(incl, a_all[t:, n:], 0.0)
    rhs = _dot(kkd, st_b) + _dot(a_kk.astype(BF16), v_big)
    yield
    a_bd = jnp.concatenate([a_kb] * pk, axis=0) * bd
    m = jnp.where(lax.broadcasted_iota(jnp.int32, (n, n), 0) == lax.broadcasted_iota(jnp.int32, (n, n), 1), 1.0, 0.0)
    for lv in range(lvl_ref.shape[0]):
        mb = m.astype(BF16)
        ma = _dot(mb, (a_bd * lvl_ref[lv]).astype(BF16)).astype(BF16)
        yield
        m = m - _dot(ma, mb)
        yield
    u_big = _dot(m.astype(BF16), big(rhs))
    yield
    u = u_big[0:t]
    for h in range(1, pk):
        u = u + u_big[h * t:(h + 1) * t]
    y = _dot(rd, st_b) + _dot(a_rk.astype(BF16), v_big) - _dot(a_rb.astype(BF16), big(u))
    yield
    decay_col = jnp.broadcast_to(jnp.exp(c_end), st.shape).T
    kb_end = jnp.concatenate([k * e_end, -(b * e_end)], axis=0).astype(BF16)
    vu = jnp.concatenate([v, u], axis=0).astype(BF16)
    return y, decay_col * st + bd * _dot_tn(kb_end, vu)


def _rwkv_chunk_kernel(r_ref, k_ref, v_ref, kk_ref, b_ref, ld_ref, bonus_ref, g_ref, tri_ref, hl_ref, bd_ref,
                       lvl_ref, lng_ref, lnb_ref, ones_ref, y_ref, st_ref):
    @pl.when(pl.program_id(0) == 0)
    def _():
        st_ref[...] = jnp.zeros_like(st_ref)

    wp = RWKV_PACK * RWKV_HEAD_DIM
    tri, hl, bd = tri_ref[...], hl_ref[...], bd_ref[...]
    n_pack = r_ref.shape[2] // wp
    where = [(bi, slice(g * wp, (g + 1) * wp)) for bi in range(r_ref.shape[0]) for g in range(n_pack)]
    loaded = [tuple(ref[bi, :, cols] for ref in (r_ref, k_ref, v_ref, kk_ref, b_ref, ld_ref)) + (st_ref[i],)
              for i, (bi, cols) in enumerate(where)]
    results = _run_interleaved([_rwkv_chain(*args, tri, hl, bd, lvl_ref) for args in loaded])
    inv = 1.0 / RWKV_HEAD_DIM
    ones = ones_ref[...]
    for i, ((bi, cols), (y, st_new)) in enumerate(zip(where, results)):
        st_ref[i] = st_new
        mu = _dot_split(y, ones, 'a', 2) * inv
        yc = y - mu
        var = _dot_split(yc * yc, ones, 'a', 2) * inv
        yn = yc * lax.rsqrt(var + RWKV_GN_EPS) * lng_ref[:, cols] + lnb_ref[:, cols]
        y_ref[bi, :, cols] = ((yn + bonus_ref[bi, :, cols]) * g_ref[bi, :, cols]).astype(y_ref.dtype)


def rwkv_chunk(r, k, v, kk, b, ld, bonus, g, ln_g, ln_b, bsz, seq):
    t, pk = RWKV_CHUNK, RWKV_PACK
    n_chunk = seq // t
    w = RWKV_WIDTH
    wp = pk * RWKV_HEAD_DIM
    assert t == RWKV_HEAD_DIM
    tri, hl, bd, lvl = _rwkv_masks()
    blk = pl.BlockSpec((bsz, t, w), lambda c: (0, c, 0))
    f2 = lambda c: (0, 0)
    shaped = lambda a: a.reshape(bsz, seq, w)
    y = pl.pallas_call(
        _rwkv_chunk_kernel,
        grid=(n_chunk,),
        in_specs=[blk] * 8 + [pl.BlockSpec(tri.shape, f2), pl.BlockSpec(hl.shape, f2), pl.BlockSpec(bd.shape, f2),
                              pl.BlockSpec(lvl.shape, lambda c: (0, 0, 0)),
                              pl.BlockSpec((1, w), f2), pl.BlockSpec((1, w), f2), pl.BlockSpec((wp, wp), f2)],
        out_specs=blk,
        out_shape=jax.ShapeDtypeStruct((bsz, seq, w), BF16),
        scratch_shapes=[pltpu.VMEM((bsz * (w // wp), wp, wp), F32)],
        compiler_params=_params("arbitrary"),
        name="rwkv_chunk",
    )(shaped(r), shaped(k), shaped(v), shaped(kk), shaped(b), shaped(ld), shaped(bonus), shaped(g), tri, hl, bd, lvl,
      ln_g.reshape(1, w), ln_b.reshape(1, w), _head_ones(wp, RWKV_HEAD_DIM))
    return y.reshape(bsz * seq, w)


def rwkv7_mixer(h, bsz, seq, mu, w0, w_up, a0, a_up, g_up, k_k, k_a, r_k, ln_g, ln_b):
    r, k, v, kk, b, ld, g, bonus = rwkv_pre(h, bsz, seq, mu, w0, w_up, a0, a_up, g_up, k_k, k_a, r_k)
    return rwkv_chunk(r, k, v, kk, b, ld, bonus, g, ln_g, ln_b, bsz, seq)


AB_IN = S5_WIDTH + NSA_WIDTH + 6 * NSA_KV_W + NSA_GATE_COLS
AB_IN_PADDED = -(-AB_IN // LANES) * LANES
NSA_GATE_COL_BLOCK = (AB_IN - NSA_GATE_COLS) // LANES
PROJ_TM = 512


def kernel(x, ab_w_in, ab_w_out, s5_lam_re, s5_lam_im, s5_log_dt, s5_b_re, s5_b_im, s5_c_re, s5_c_im, s5_d, s5_w_glu, s5_b_glu, nsa_pe_k, nsa_pe_v, nsa_ck_w1, nsa_ck_b1, nsa_ck_w2, nsa_cv_w1, nsa_cv_b1, nsa_cv_w2, cd_w_in, cd_w_out, rwkv_mu, rwkv_w0, rwkv_w_up, rwkv_a0, rwkv_a_up, rwkv_g_up, rwkv_k_k, rwkv_k_a, rwkv_r_k, rwkv_ln_g, rwkv_ln_b, ret_ln_g, ret_ln_b, ln1_g, ln1_b, ln2_g, ln2_b, moe_router, moe_bias, moe_w1, moe_w3, moe_w2, sh_w1, sh_w3, sh_w2):
    bsz, seq, d = x.shape
    assert (AB_IN - NSA_GATE_COLS) % LANES == 0
    xf = x.reshape(bsz * seq, d)
    x_in = xf
    for layer in range(DEPTH):
        i = layer // 2
        if layer % 2 == 0:
            w_in = jnp.pad(ab_w_in[i], ((0, 0), (0, AB_IN_PADDED - AB_IN))).astype(BF16)
            h, u3 = project(x_in, w_in, PROJ_TM, chunked=(S5_CHUNK, S5_WIDTH))
            y_1 = s5_mixer(h, u3, bsz, seq, s5_lam_re[i], s5_lam_im[i], s5_log_dt[i], s5_b_re[i], s5_b_im[i],
                           s5_c_re[i], s5_c_im[i], s5_d[i], s5_w_glu[i], s5_b_glu[i])
            y_2 = nsa_mixer(h, bsz, seq, NSA_GATE_COL_BLOCK, nsa_pe_k[i], nsa_pe_v[i], nsa_ck_w1[i], nsa_ck_b1[i],
                            nsa_ck_w2[i], nsa_cv_w1[i], nsa_cv_b1[i], nsa_cv_w2[i])
            w_out = ab_w_out[i]
        else:
            h = project(x_in, cd_w_in[i].astype(BF16), PROJ_TM)
            y_1 = rwkv7_mixer(h, bsz, seq, rwkv_mu[i], rwkv_w0[i], rwkv_w_up[i], rwkv_a0[i], rwkv_a_up[i],
                              rwkv_g_up[i], rwkv_k_k[i], rwkv_k_a[i], rwkv_r_k[i], rwkv_ln_g[i], rwkv_ln_b[i])
            y_2 = retention_mixer(h, bsz, seq, RWKV_COLS, ret_ln_g[i], ret_ln_b[i])
            w_out = cd_w_out[i]
        xf, gates = out_proj_ln_route(y_1, y_2, w_out, xf, ln1_g[layer], ln1_b[layer],
                                      moe_router[layer], moe_bias[layer])
        xf, x_in = moe_experts_ln(xf, gates, moe_w1, moe_w3, moe_w2, layer,
                                  sh_w1, sh_w3, sh_w2[layer], ln2_g[layer], ln2_b[layer])
    return xf.reshape(bsz, seq, d)
```
